```python
import math
import jax, jax.numpy as jnp
from jax import lax
import numpy as np

D_MODEL = 1024
BATCH = 32
SEQ = 2048
DEPTH = 2

EPS = 1e-6
N_BRANCH = 3
BRANCH_W = D_MODEL // 2
SSM_GROUP = 16
SSM_GROUPS = BRANCH_W // SSM_GROUP
SSM_STATE = 64
DT_MIN = 1e-3
DT_MAX = 1e-1
HEAD_DIM = 64
ATT_HEADS = BRANCH_W // HEAD_DIM
DILATED_PATTERNS = ((128, 1), (512, 4), (2048, 16))
N_PAT = len(DILATED_PATTERNS)
ATT_BLOCK = 128
ATT_SCALE = HEAD_DIM ** -0.5
CONV_WIDTH = 31
D_FF = -(-(8 * D_MODEL) // (3 * 256)) * 256

COL_U = BRANCH_W
COL_Q = N_PAT * BRANCH_W
COL_KV = BRANCH_W
COL_CONV = 2 * BRANCH_W
COL_GATE = N_BRANCH * D_MODEL
SPLIT_POINTS = (COL_U,
                COL_U + COL_Q,
                COL_U + COL_Q + COL_KV,
                COL_U + COL_Q + 2 * COL_KV,
                COL_U + COL_Q + 2 * COL_KV + COL_CONV)
IN_COLS = SPLIT_POINTS[-1] + COL_GATE

kernel_name = 'hybrid_s5_dilated_attn_conformer_conv_block'


def rms_norm(x, g):
    xf = x.astype(jnp.float32)
    y = xf * lax.rsqrt(jnp.mean(xf * xf, axis=-1, keepdims=True) + EPS)
    return (y * g.astype(jnp.float32)).astype(x.dtype)


def _complex_affine_combine(e1, e2):
    a1r, a1i, b1r, b1i = e1
    a2r, a2i, b2r, b2i = e2
    return (a1r * a2r - a1i * a2i,
            a1r * a2i + a1i * a2r,
            a2r * b1r - a2i * b1i + b2r,
            a2r * b1i + a2i * b1r + b2i)


def s5_branch(u, lam_re, lam_im, log_dt, b_re, b_im, c_re, c_im, d_skip, w_glu):
    bsz, seq, _ = u.shape
    f32 = jnp.float32
    uf = u.astype(f32).reshape(bsz, seq, SSM_GROUPS, SSM_GROUP)
    lam_re = lam_re.astype(f32)
    lam_im = lam_im.astype(f32)
    dt = jnp.exp(log_dt.astype(f32))[:, None]
    mag = jnp.exp(lam_re * dt)
    ab_re = mag * jnp.cos(lam_im * dt)
    ab_im = mag * jnp.sin(lam_im * dt)
    nr, ni = ab_re - 1.0, ab_im
    den = lam_re * lam_re + lam_im * lam_im
    z_re = ((nr * lam_re + ni * lam_im) / den)[..., None]
    z_im = ((ni * lam_re - nr * lam_im) / den)[..., None]
    b_re = b_re.astype(f32)
    b_im = b_im.astype(f32)
    bb_re = z_re * b_re - z_im * b_im
    bb_im = z_re * b_im + z_im * b_re
    bu_re = jnp.einsum('blgh,gph->blgp', uf, bb_re)
    bu_im = jnp.einsum('blgh,gph->blgp', uf, bb_im)
    a_re = jnp.broadcast_to(ab_re, (1, seq, SSM_GROUPS, SSM_STATE))
    a_im = jnp.broadcast_to(ab_im, (1, seq, SSM_GROUPS, SSM_STATE))
    _, _, s_re, s_im = lax.associative_scan(
        _complex_affine_combine, (a_re, a_im, bu_re, bu_im), axis=1)
    y = (jnp.einsum('blgp,ghp->blgh', s_re, c_re.astype(f32))
         - jnp.einsum('blgp,ghp->blgh', s_im, c_im.astype(f32)))
    y = y.reshape(bsz, seq, BRANCH_W) + d_skip.astype(f32) * uf.reshape(bsz, seq, BRANCH_W)
    y = jax.nn.gelu(y).astype(u.dtype)
    z = y @ w_glu
    return z[..., :D_MODEL] * jax.nn.sigmoid(z[..., D_MODEL:])


def _dilated_group(q, k, v, window, dilation):
    bsz, seq, nh, hd = q.shape
    ls = seq // dilation
    nb = -(-ls // ATT_BLOCK)
    lp = nb * ATT_BLOCK
    w_sub = window // dilation

    def to_sub(t):
        t = t.reshape(bsz, ls, dilation, nh, hd).transpose(0, 2, 3, 1, 4)
        t = jnp.pad(t, ((0, 0), (0, 0), (0, 0), (0, lp - ls), (0, 0)))
        return t.reshape(bsz, dilation, nh, nb, ATT_BLOCK, hd)

    def with_prev(t):
        prev = jnp.pad(t, ((0, 0), (0, 0), (0, 0), (1, 0), (0, 0), (0, 0)))[:, :, :, :-1]
        return jnp.concatenate([prev, t], axis=4)

    qb = to_sub(q)
    kc = with_prev(to_sub(k))
    vc = with_prev(to_sub(v))
    s = jnp.einsum('bdhnqe,bdhnke->bdhnqk', qb, kc).astype(jnp.float32) * ATT_SCALE
    qi = jnp.arange(ATT_BLOCK)[:, None]
    kj = jnp.arange(2 * ATT_BLOCK)[None, :]
    dist = qi - kj + ATT_BLOCK
    kpos = jnp.arange(nb)[:, None, None] * ATT_BLOCK + kj[None] - ATT_BLOCK
    valid = (dist >= 0) & (dist <= w_sub) & (kpos >= 0)
    s = jnp.where(valid, s, -jnp.inf)
    m = jnp.max(s, axis=-1, keepdims=True)
    p = jnp.exp(s - m)
    den = jnp.sum(p, axis=-1, keepdims=True)
    o = jnp.einsum('bdhnqk,bdhnke->bdhnqe', p, vc.astype(jnp.float32)) / den
    lse = (m + jnp.log(den))[..., 0]
    o = o.reshape(bsz, dilation, nh, lp, hd)[:, :, :, :ls]
    o = o.transpose(0, 3, 1, 2, 4).reshape(bsz, seq, nh, hd)
    lse = lse.reshape(bsz, dilation, nh, lp)[:, :, :, :ls]
    lse = lse.transpose(0, 3, 1, 2).reshape(bsz, seq, nh)
    return o, lse


def dilated_attention(q, k, v):
    outs, lses = [], []
    for p_idx, (window, dilation) in enumerate(DILATED_PATTERNS):
        o, lse = _dilated_group(q[:, :, p_idx], k, v, window, dilation)
        outs.append(o)
        lses.append(lse)
    wts = jax.nn.softmax(jnp.stack(lses, axis=0), axis=0)
    return jnp.sum(wts[..., None] * jnp.stack(outs, axis=0), axis=0)


def conformer_conv(cv, conv_w, conv_b, ln_g, ln_b, w_pw2):
    a, g = jnp.split(cv, 2, axis=-1)
    h = a * jax.nn.sigmoid(g)
    h = lax.conv_general_dilated(
        h, conv_w[:, None, :], window_strides=(1,),
        padding=[(CONV_WIDTH - 1, 0)],
        dimension_numbers=('NWC', 'WIO', 'NWC'),
        feature_group_count=BRANCH_W) + conv_b
    hf = h.astype(jnp.float32)
    mu = jnp.mean(hf, axis=-1, keepdims=True)
    var = jnp.mean(jnp.square(hf - mu), axis=-1, keepdims=True)
    hn = (hf - mu) * lax.rsqrt(var + EPS) * ln_g.astype(jnp.float32) + ln_b.astype(jnp.float32)
    h = jax.nn.silu(hn).astype(cv.dtype)
    return h @ w_pw2


def _fwd_setup_inputs(seed: int = 0) -> dict:
    key = jax.random.key(seed)
    ks = jax.random.split(key, 24)
    f32 = jnp.float32

    def nrm(k, shape, scale):
        return jax.random.normal(k, shape, f32) * scale

    lam_im_base = jnp.pi * jnp.arange(SSM_STATE, dtype=f32)
    return {
        'x': nrm(ks[0], (BATCH, SEQ, D_MODEL), 1.0),
        'norm1_g': 1.0 + nrm(ks[1], (DEPTH, D_MODEL), 0.05),
        'w_in': nrm(ks[2], (DEPTH, D_MODEL, IN_COLS), D_MODEL ** -0.5),
        'b_gate': nrm(ks[3], (DEPTH, COL_GATE), 0.02),
        'ssm_lambda_re': -0.5 + nrm(ks[4], (DEPTH, SSM_GROUPS, SSM_STATE), 0.01),
        'ssm_lambda_im': lam_im_base + nrm(ks[5], (DEPTH, SSM_GROUPS, SSM_STATE), 0.01),
        'ssm_log_dt': jax.random.uniform(ks[6], (DEPTH, SSM_GROUPS), f32,
                                         math.log(DT_MIN), math.log(DT_MAX)),
        'ssm_b_re': nrm(ks[7], (DEPTH, SSM_GROUPS, SSM_STATE, SSM_GROUP), (2 * SSM_GROUP) ** -0.5),
        'ssm_b_im': nrm(ks[8], (DEPTH, SSM_GROUPS, SSM_STATE, SSM_GROUP), (2 * SSM_GROUP) ** -0.5),
        'ssm_c_re': nrm(ks[9], (DEPTH, SSM_GROUPS, SSM_GROUP, SSM_STATE), SSM_STATE ** -0.25),
        'ssm_c_im': nrm(ks[10], (DEPTH, SSM_GROUPS, SSM_GROUP, SSM_STATE), SSM_STATE ** -0.25),
        'ssm_d': nrm(ks[11], (DEPTH, BRANCH_W), 1.0),
        'w_ssm_glu': nrm(ks[12], (DEPTH, BRANCH_W, 2 * D_MODEL), BRANCH_W ** -0.5),
        'w_att_up': nrm(ks[13], (DEPTH, BRANCH_W, D_MODEL), BRANCH_W ** -0.5),
        'conv_w': nrm(ks[14], (DEPTH, CONV_WIDTH, BRANCH_W), CONV_WIDTH ** -0.5),
        'conv_b': nrm(ks[15], (DEPTH, BRANCH_W), 0.02),
        'conv_ln_g': 1.0 + nrm(ks[16], (DEPTH, BRANCH_W), 0.05),
        'conv_ln_b': nrm(ks[17], (DEPTH, BRANCH_W), 0.02),
        'w_conv_pw2': nrm(ks[18], (DEPTH, BRANCH_W, D_MODEL), BRANCH_W ** -0.5),
        'w_out': nrm(ks[19], (DEPTH, D_MODEL, D_MODEL), D_MODEL ** -0.5),
        'norm2_g': 1.0 + nrm(ks[20], (DEPTH, D_MODEL), 0.05),
        'w_ffn_in': nrm(ks[21], (DEPTH, D_MODEL, 2 * D_FF), D_MODEL ** -0.5),
        'w_ffn_out': nrm(ks[22], (DEPTH, D_FF, D_MODEL), D_FF ** -0.5),
        'final_g': 1.0 + nrm(ks[23], (D_MODEL,), 0.05),
    }


def _fwd_reference(x, norm1_g, w_in, b_gate, ssm_lambda_re, ssm_lambda_im, ssm_log_dt,
              ssm_b_re, ssm_b_im, ssm_c_re, ssm_c_im, ssm_d, w_ssm_glu, w_att_up,
              conv_w, conv_b, conv_ln_g, conv_ln_b, w_conv_pw2, w_out,
              norm2_g, w_ffn_in, w_ffn_out, final_g):
    bsz, seq, _ = x.shape
    for l in range(DEPTH):
        h = rms_norm(x, norm1_g[l])
        proj = h @ w_in[l]
        u, q, k, v, cv, g = jnp.split(proj, SPLIT_POINTS, axis=-1)
        y_s = s5_branch(u, ssm_lambda_re[l], ssm_lambda_im[l], ssm_log_dt[l],
                        ssm_b_re[l], ssm_b_im[l], ssm_c_re[l], ssm_c_im[l],
                        ssm_d[l], w_ssm_glu[l])
        q = q.reshape(bsz, seq, N_PAT, ATT_HEADS, HEAD_DIM)
        k = k.reshape(bsz, seq, ATT_HEADS, HEAD_DIM)
        v = v.reshape(bsz, seq, ATT_HEADS, HEAD_DIM)
        o = dilated_attention(q, k, v).astype(x.dtype).reshape(bsz, seq, BRANCH_W)
        y_a = o @ w_att_up[l]
        y_c = conformer_conv(cv, conv_w[l], conv_b[l], conv_ln_g[l], conv_ln_b[l], w_conv_pw2[l])
        gate = jax.nn.sigmoid((g + b_gate[l]).astype(jnp.float32)).astype(x.dtype)
        gate = gate.reshape(bsz, seq, N_BRANCH, D_MODEL)
        merged = gate[:, :, 0] * y_s + gate[:, :, 1] * y_a + gate[:, :, 2] * y_c
        x = x + merged @ w_out[l]
        h = rms_norm(x, norm2_g[l])
        z = h @ w_ffn_in[l]
        x = x + (jax.nn.silu(z[..., :D_FF]) * z[..., D_FF:]) @ w_ffn_out[l]
    return rms_norm(x, final_g)


import jax as _jax
import jax.numpy as _jnp

TWIN_FORMAT = 'train_step'
FWD_PARAMS = ['x', 'norm1_g', 'w_in', 'b_gate', 'ssm_lambda_re', 'ssm_lambda_im', 'ssm_log_dt', 'ssm_b_re', 'ssm_b_im', 'ssm_c_re', 'ssm_c_im', 'ssm_d', 'w_ssm_glu', 'w_att_up', 'conv_w', 'conv_b', 'conv_ln_g', 'conv_ln_b', 'w_conv_pw2', 'w_out', 'norm2_g', 'w_ffn_in', 'w_ffn_out', 'final_g']
TWIN_WEIGHTS = ['norm1_g', 'w_in', 'b_gate', 'ssm_lambda_re', 'ssm_lambda_im', 'ssm_log_dt', 'ssm_b_re', 'ssm_b_im', 'ssm_c_re', 'ssm_c_im', 'ssm_d', 'w_ssm_glu', 'w_att_up', 'conv_w', 'conv_b', 'conv_ln_g', 'conv_ln_b', 'w_conv_pw2', 'w_out', 'norm2_g', 'w_ffn_in', 'w_ffn_out', 'final_g']
TWIN_DIFF_INPUT = 'x'
TWIN_INPUTS = ['x', 'norm1_g', 'w_in', 'b_gate', 'ssm_lambda_re', 'ssm_lambda_im', 'ssm_log_dt', 'ssm_b_re', 'ssm_b_im', 'ssm_c_re', 'ssm_c_im', 'ssm_d', 'w_ssm_glu', 'w_att_up', 'conv_w', 'conv_b', 'conv_ln_g', 'conv_ln_b', 'w_conv_pw2', 'w_out', 'norm2_g', 'w_ffn_in', 'w_ffn_out', 'final_g', 'loss_target', 'm_norm1_g', 'm_w_in', 'm_b_gate', 'm_ssm_lambda_re', 'm_ssm_lambda_im', 'm_ssm_log_dt', 'm_ssm_b_re', 'm_ssm_b_im', 'm_ssm_c_re', 'm_ssm_c_im', 'm_ssm_d', 'm_w_ssm_glu', 'm_w_att_up', 'm_conv_w', 'm_conv_b', 'm_conv_ln_g', 'm_conv_ln_b', 'm_w_conv_pw2', 'm_w_out', 'm_norm2_g', 'm_w_ffn_in', 'm_w_ffn_out', 'm_final_g', 'v_norm1_g', 'v_w_in', 'v_b_gate', 'v_ssm_lambda_re', 'v_ssm_lambda_im', 'v_ssm_log_dt', 'v_ssm_b_re', 'v_ssm_b_im', 'v_ssm_c_re', 'v_ssm_c_im', 'v_ssm_d', 'v_w_ssm_glu', 'v_w_att_up', 'v_conv_w', 'v_conv_b', 'v_conv_ln_g', 'v_conv_ln_b', 'v_w_conv_pw2', 'v_w_out', 'v_norm2_g', 'v_w_ffn_in', 'v_w_ffn_out', 'v_final_g']
TWIN_OUTPUTS = ['loss', 'grad_x', 'grad_norm1_g', 'grad_w_in', 'grad_b_gate', 'grad_ssm_lambda_re', 'grad_ssm_lambda_im', 'grad_ssm_log_dt', 'grad_ssm_b_re', 'grad_ssm_b_im', 'grad_ssm_c_re', 'grad_ssm_c_im', 'grad_ssm_d', 'grad_w_ssm_glu', 'grad_w_att_up', 'grad_conv_w', 'grad_conv_b', 'grad_conv_ln_g', 'grad_conv_ln_b', 'grad_w_conv_pw2', 'grad_w_out', 'grad_norm2_g', 'grad_w_ffn_in', 'grad_w_ffn_out', 'grad_final_g', 'delta_norm1_g', 'delta_w_in', 'delta_b_gate', 'delta_ssm_lambda_re', 'delta_ssm_lambda_im', 'delta_ssm_log_dt', 'delta_ssm_b_re', 'delta_ssm_b_im', 'delta_ssm_c_re', 'delta_ssm_c_im', 'delta_ssm_d', 'delta_w_ssm_glu', 'delta_w_att_up', 'delta_conv_w', 'delta_conv_b', 'delta_conv_ln_g', 'delta_conv_ln_b', 'delta_w_conv_pw2', 'delta_w_out', 'delta_norm2_g', 'delta_w_ffn_in', 'delta_w_ffn_out', 'delta_final_g', 'new_m_norm1_g', 'new_m_w_in', 'new_m_b_gate', 'new_m_ssm_lambda_re', 'new_m_ssm_lambda_im', 'new_m_ssm_log_dt', 'new_m_ssm_b_re', 'new_m_ssm_b_im', 'new_m_ssm_c_re', 'new_m_ssm_c_im', 'new_m_ssm_d', 'new_m_w_ssm_glu', 'new_m_w_att_up', 'new_m_conv_w', 'new_m_conv_b', 'new_m_conv_ln_g', 'new_m_conv_ln_b', 'new_m_w_conv_pw2', 'new_m_w_out', 'new_m_norm2_g', 'new_m_w_ffn_in', 'new_m_w_ffn_out', 'new_m_final_g', 'new_v_norm1_g', 'new_v_w_in', 'new_v_b_gate', 'new_v_ssm_lambda_re', 'new_v_ssm_lambda_im', 'new_v_ssm_log_dt', 'new_v_ssm_b_re', 'new_v_ssm_b_im', 'new_v_ssm_c_re', 'new_v_ssm_c_im', 'new_v_ssm_d', 'new_v_w_ssm_glu', 'new_v_w_att_up', 'new_v_conv_w', 'new_v_conv_b', 'new_v_conv_ln_g', 'new_v_conv_ln_b', 'new_v_w_conv_pw2', 'new_v_w_out', 'new_v_norm2_g', 'new_v_w_ffn_in', 'new_v_w_ffn_out', 'new_v_final_g']
TWIN_LEAF_KINDS = {'loss': 'loss', 'grad_x': 'grad_x', 'grad_norm1_g': 'grad_w', 'grad_w_in': 'grad_w', 'grad_b_gate': 'grad_w', 'grad_ssm_lambda_re': 'grad_w', 'grad_ssm_lambda_im': 'grad_w', 'grad_ssm_log_dt': 'grad_w', 'grad_ssm_b_re': 'grad_w', 'grad_ssm_b_im': 'grad_w', 'grad_ssm_c_re': 'grad_w', 'grad_ssm_c_im': 'grad_w', 'grad_ssm_d': 'grad_w', 'grad_w_ssm_glu': 'grad_w', 'grad_w_att_up': 'grad_w', 'grad_conv_w': 'grad_w', 'grad_conv_b': 'grad_w', 'grad_conv_ln_g': 'grad_w', 'grad_conv_ln_b': 'grad_w', 'grad_w_conv_pw2': 'grad_w', 'grad_w_out': 'grad_w', 'grad_norm2_g': 'grad_w', 'grad_w_ffn_in': 'grad_w', 'grad_w_ffn_out': 'grad_w', 'grad_final_g': 'grad_w', 'delta_norm1_g': 'delta_w', 'delta_w_in': 'delta_w', 'delta_b_gate': 'delta_w', 'delta_ssm_lambda_re': 'delta_w', 'delta_ssm_lambda_im': 'delta_w', 'delta_ssm_log_dt': 'delta_w', 'delta_ssm_b_re': 'delta_w', 'delta_ssm_b_im': 'delta_w', 'delta_ssm_c_re': 'delta_w', 'delta_ssm_c_im': 'delta_w', 'delta_ssm_d': 'delta_w', 'delta_w_ssm_glu': 'delta_w', 'delta_w_att_up': 'delta_w', 'delta_conv_w': 'delta_w', 'delta_conv_b': 'delta_w', 'delta_conv_ln_g': 'delta_w', 'delta_conv_ln_b': 'delta_w', 'delta_w_conv_pw2': 'delta_w', 'delta_w_out': 'delta_w', 'delta_norm2_g': 'delta_w', 'delta_w_ffn_in': 'delta_w', 'delta_w_ffn_out': 'delta_w', 'delta_final_g': 'delta_w', 'new_m_norm1_g': 'new_m', 'new_m_w_in': 'new_m', 'new_m_b_gate': 'new_m', 'new_m_ssm_lambda_re': 'new_m', 'new_m_ssm_lambda_im': 'new_m', 'new_m_ssm_log_dt': 'new_m', 'new_m_ssm_b_re': 'new_m', 'new_m_ssm_b_im': 'new_m', 'new_m_ssm_c_re': 'new_m', 'new_m_ssm_c_im': 'new_m', 'new_m_ssm_d': 'new_m', 'new_m_w_ssm_glu': 'new_m', 'new_m_w_att_up': 'new_m', 'new_m_conv_w': 'new_m', 'new_m_conv_b': 'new_m', 'new_m_conv_ln_g': 'new_m', 'new_m_conv_ln_b': 'new_m', 'new_m_w_conv_pw2': 'new_m', 'new_m_w_out': 'new_m', 'new_m_norm2_g': 'new_m', 'new_m_w_ffn_in': 'new_m', 'new_m_w_ffn_out': 'new_m', 'new_m_final_g': 'new_m', 'new_v_norm1_g': 'new_v', 'new_v_w_in': 'new_v', 'new_v_b_gate': 'new_v', 'new_v_ssm_lambda_re': 'new_v', 'new_v_ssm_lambda_im': 'new_v', 'new_v_ssm_log_dt': 'new_v', 'new_v_ssm_b_re': 'new_v', 'new_v_ssm_b_im': 'new_v', 'new_v_ssm_c_re': 'new_v', 'new_v_ssm_c_im': 'new_v', 'new_v_ssm_d': 'new_v', 'new_v_w_ssm_glu': 'new_v', 'new_v_w_att_up': 'new_v', 'new_v_conv_w': 'new_v', 'new_v_conv_b': 'new_v', 'new_v_conv_ln_g': 'new_v', 'new_v_conv_ln_b': 'new_v', 'new_v_w_conv_pw2': 'new_v', 'new_v_w_out': 'new_v', 'new_v_norm2_g': 'new_v', 'new_v_w_ffn_in': 'new_v', 'new_v_w_ffn_out': 'new_v', 'new_v_final_g': 'new_v'}


def _forward(args):
    return _fwd_reference(*[args[k] for k in FWD_PARAMS])


def _output_shape():
    out = _jax.eval_shape(lambda: _forward(_fwd_setup_inputs(0)))
    return out.shape, out.dtype

N_MICROBATCH = 1
ADAM_LR = 0.001
ADAM_B1 = 0.9
ADAM_B2 = 0.999
ADAM_EPS = 1e-08
ADAM_WD = 0.01
ADAM_STEP = 10
PER_EXAMPLE_BATCH_AXIS = {'x': 0, 'loss_target': 0}
SHARED_INPUTS = []
_WEIGHT_DTYPES = {'norm1_g': _jnp.float32, 'w_in': _jnp.float32, 'b_gate': _jnp.float32, 'ssm_lambda_re': _jnp.float32, 'ssm_lambda_im': _jnp.float32, 'ssm_log_dt': _jnp.float32, 'ssm_b_re': _jnp.float32, 'ssm_b_im': _jnp.float32, 'ssm_c_re': _jnp.float32, 'ssm_c_im': _jnp.float32, 'ssm_d': _jnp.float32, 'w_ssm_glu': _jnp.float32, 'w_att_up': _jnp.float32, 'conv_w': _jnp.float32, 'conv_b': _jnp.float32, 'conv_ln_g': _jnp.float32, 'conv_ln_b': _jnp.float32, 'w_conv_pw2': _jnp.float32, 'w_out': _jnp.float32, 'norm2_g': _jnp.float32, 'w_ffn_in': _jnp.float32, 'w_ffn_out': _jnp.float32, 'final_g': _jnp.float32}
MOMENT_SCALE = {'norm1_g': 1.232972e-01, 'w_in': 4.628610e-02, 'b_gate': 2.388657e-02, 'ssm_lambda_re': 1.661114e-02, 'ssm_lambda_im': 1.841255e-02, 'ssm_log_dt': 8.438828e+00, 'ssm_b_re': 1.068083e-02, 'ssm_b_im': 1.090700e-02, 'ssm_c_re': 5.289609e-03, 'ssm_c_im': 5.424982e-03, 'ssm_d': 9.153582e-02, 'w_ssm_glu': 4.218429e-02, 'w_att_up': 3.212928e-02, 'conv_w': 1.225719e-01, 'conv_b': 2.809790e-01, 'conv_ln_g': 1.643320e-01, 'conv_ln_b': 1.822972e-01, 'w_conv_pw2': 9.106225e-02, 'w_out': 1.052280e-01, 'norm2_g': 1.769795e-01, 'w_ffn_in': 7.542381e-02, 'w_ffn_out': 1.238880e-01, 'final_g': 6.392474e+01}


def _to_microbatches(a, axis):
    t = _jnp.moveaxis(a, axis, 0)
    t = t.reshape((N_MICROBATCH, t.shape[0] // N_MICROBATCH) + t.shape[1:])
    return _jnp.moveaxis(t, 1, axis + 1)


def setup_inputs(seed: int = 0) -> dict:
    inp = _fwd_setup_inputs(seed)
    key = _jax.random.fold_in(_jax.random.key(seed), 7919)
    shape, _ = _output_shape()
    out = dict(inp)
    out["loss_target"] = _jax.random.normal(_jax.random.fold_in(key, 0), shape, _jnp.float32)
    for i, name in enumerate(TWIN_WEIGHTS):
        w = inp[name].astype(_jnp.float32)
        if MOMENT_SCALE is None:
            s = _jnp.sqrt(_jnp.mean(_jnp.square(w)) + 1e-30)
        else:
            s = MOMENT_SCALE[name]
        km, kv = _jax.random.split(_jax.random.fold_in(key, i + 1))
        out[name] = w
        out["m_" + name] = s * _jax.random.normal(km, w.shape, _jnp.float32)
        out["v_" + name] = (s * s) * _jax.random.uniform(kv, w.shape, _jnp.float32, 0.5, 1.5)
    if N_MICROBATCH > 1:
        for name, axis in PER_EXAMPLE_BATCH_AXIS.items():
            out[name] = _to_microbatches(out[name], axis)
    return {'x': out['x'], 'norm1_g': out['norm1_g'], 'w_in': out['w_in'], 'b_gate': out['b_gate'], 'ssm_lambda_re': out['ssm_lambda_re'], 'ssm_lambda_im': out['ssm_lambda_im'], 'ssm_log_dt': out['ssm_log_dt'], 'ssm_b_re': out['ssm_b_re'], 'ssm_b_im': out['ssm_b_im'], 'ssm_c_re': out['ssm_c_re'], 'ssm_c_im': out['ssm_c_im'], 'ssm_d': out['ssm_d'], 'w_ssm_glu': out['w_ssm_glu'], 'w_att_up': out['w_att_up'], 'conv_w': out['conv_w'], 'conv_b': out['conv_b'], 'conv_ln_g': out['conv_ln_g'], 'conv_ln_b': out['conv_ln_b'], 'w_conv_pw2': out['w_conv_pw2'], 'w_out': out['w_out'], 'norm2_g': out['norm2_g'], 'w_ffn_in': out['w_ffn_in'], 'w_ffn_out': out['w_ffn_out'], 'final_g': out['final_g'], 'loss_target': out['loss_target'], 'm_norm1_g': out['m_norm1_g'], 'm_w_in': out['m_w_in'], 'm_b_gate': out['m_b_gate'], 'm_ssm_lambda_re': out['m_ssm_lambda_re'], 'm_ssm_lambda_im': out['m_ssm_lambda_im'], 'm_ssm_log_dt': out['m_ssm_log_dt'], 'm_ssm_b_re': out['m_ssm_b_re'], 'm_ssm_b_im': out['m_ssm_b_im'], 'm_ssm_c_re': out['m_ssm_c_re'], 'm_ssm_c_im': out['m_ssm_c_im'], 'm_ssm_d': out['m_ssm_d'], 'm_w_ssm_glu': out['m_w_ssm_glu'], 'm_w_att_up': out['m_w_att_up'], 'm_conv_w': out['m_conv_w'], 'm_conv_b': out['m_conv_b'], 'm_conv_ln_g': out['m_conv_ln_g'], 'm_conv_ln_b': out['m_conv_ln_b'], 'm_w_conv_pw2': out['m_w_conv_pw2'], 'm_w_out': out['m_w_out'], 'm_norm2_g': out['m_norm2_g'], 'm_w_ffn_in': out['m_w_ffn_in'], 'm_w_ffn_out': out['m_w_ffn_out'], 'm_final_g': out['m_final_g'], 'v_norm1_g': out['v_norm1_g'], 'v_w_in': out['v_w_in'], 'v_b_gate': out['v_b_gate'], 'v_ssm_lambda_re': out['v_ssm_lambda_re'], 'v_ssm_lambda_im': out['v_ssm_lambda_im'], 'v_ssm_log_dt': out['v_ssm_log_dt'], 'v_ssm_b_re': out['v_ssm_b_re'], 'v_ssm_b_im': out['v_ssm_b_im'], 'v_ssm_c_re': out['v_ssm_c_re'], 'v_ssm_c_im': out['v_ssm_c_im'], 'v_ssm_d': out['v_ssm_d'], 'v_w_ssm_glu': out['v_w_ssm_glu'], 'v_w_att_up': out['v_w_att_up'], 'v_conv_w': out['v_conv_w'], 'v_conv_b': out['v_conv_b'], 'v_conv_ln_g': out['v_conv_ln_g'], 'v_conv_ln_b': out['v_conv_ln_b'], 'v_w_conv_pw2': out['v_w_conv_pw2'], 'v_w_out': out['v_w_out'], 'v_norm2_g': out['v_norm2_g'], 'v_w_ffn_in': out['v_w_ffn_in'], 'v_w_ffn_out': out['v_w_ffn_out'], 'v_final_g': out['v_final_g']}


def _loss(weights, diff, rest, loss_target):
    with _jax.named_scope("forward"):
        args = {**rest, TWIN_DIFF_INPUT: diff, **{k: w.astype(_WEIGHT_DTYPES[k]) for k, w in weights.items()}}
        y = _forward(args)
    with _jax.named_scope("loss_head"):
        err = _jnp.square(y.astype(_jnp.float32) - loss_target)
        return 0.5 * _jnp.sum(_jnp.mean(err, axis=-1)) if err.ndim else 0.5 * err


def _adamw(w, g, m, v):
    m = ADAM_B1 * m + (1.0 - ADAM_B1) * g
    v = ADAM_B2 * v + (1.0 - ADAM_B2) * _jnp.square(g)
    m_hat = m / (1.0 - ADAM_B1 ** ADAM_STEP)
    v_hat = v / (1.0 - ADAM_B2 ** ADAM_STEP)
    delta = -ADAM_LR * (m_hat / (_jnp.sqrt(v_hat) + ADAM_EPS) + ADAM_WD * w)
    return delta, m, v


def reference(x, norm1_g, w_in, b_gate, ssm_lambda_re, ssm_lambda_im, ssm_log_dt, ssm_b_re, ssm_b_im, ssm_c_re, ssm_c_im, ssm_d, w_ssm_glu, w_att_up, conv_w, conv_b, conv_ln_g, conv_ln_b, w_conv_pw2, w_out, norm2_g, w_ffn_in, w_ffn_out, final_g, loss_target, m_norm1_g, m_w_in, m_b_gate, m_ssm_lambda_re, m_ssm_lambda_im, m_ssm_log_dt, m_ssm_b_re, m_ssm_b_im, m_ssm_c_re, m_ssm_c_im, m_ssm_d, m_w_ssm_glu, m_w_att_up, m_conv_w, m_conv_b, m_conv_ln_g, m_conv_ln_b, m_w_conv_pw2, m_w_out, m_norm2_g, m_w_ffn_in, m_w_ffn_out, m_final_g, v_norm1_g, v_w_in, v_b_gate, v_ssm_lambda_re, v_ssm_lambda_im, v_ssm_log_dt, v_ssm_b_re, v_ssm_b_im, v_ssm_c_re, v_ssm_c_im, v_ssm_d, v_w_ssm_glu, v_w_att_up, v_conv_w, v_conv_b, v_conv_ln_g, v_conv_ln_b, v_w_conv_pw2, v_w_out, v_norm2_g, v_w_ffn_in, v_w_ffn_out, v_final_g):
    given = dict(x=x, norm1_g=norm1_g, w_in=w_in, b_gate=b_gate, ssm_lambda_re=ssm_lambda_re, ssm_lambda_im=ssm_lambda_im, ssm_log_dt=ssm_log_dt, ssm_b_re=ssm_b_re, ssm_b_im=ssm_b_im, ssm_c_re=ssm_c_re, ssm_c_im=ssm_c_im, ssm_d=ssm_d, w_ssm_glu=w_ssm_glu, w_att_up=w_att_up, conv_w=conv_w, conv_b=conv_b, conv_ln_g=conv_ln_g, conv_ln_b=conv_ln_b, w_conv_pw2=w_conv_pw2, w_out=w_out, norm2_g=norm2_g, w_ffn_in=w_ffn_in, w_ffn_out=w_ffn_out, final_g=final_g, loss_target=loss_target, m_norm1_g=m_norm1_g, m_w_in=m_w_in, m_b_gate=m_b_gate, m_ssm_lambda_re=m_ssm_lambda_re, m_ssm_lambda_im=m_ssm_lambda_im, m_ssm_log_dt=m_ssm_log_dt, m_ssm_b_re=m_ssm_b_re, m_ssm_b_im=m_ssm_b_im, m_ssm_c_re=m_ssm_c_re, m_ssm_c_im=m_ssm_c_im, m_ssm_d=m_ssm_d, m_w_ssm_glu=m_w_ssm_glu, m_w_att_up=m_w_att_up, m_conv_w=m_conv_w, m_conv_b=m_conv_b, m_conv_ln_g=m_conv_ln_g, m_conv_ln_b=m_conv_ln_b, m_w_conv_pw2=m_w_conv_pw2, m_w_out=m_w_out, m_norm2_g=m_norm2_g, m_w_ffn_in=m_w_ffn_in, m_w_ffn_out=m_w_ffn_out, m_final_g=m_final_g, v_norm1_g=v_norm1_g, v_w_in=v_w_in, v_b_gate=v_b_gate, v_ssm_lambda_re=v_ssm_lambda_re, v_ssm_lambda_im=v_ssm_lambda_im, v_ssm_log_dt=v_ssm_log_dt, v_ssm_b_re=v_ssm_b_re, v_ssm_b_im=v_ssm_b_im, v_ssm_c_re=v_ssm_c_re, v_ssm_c_im=v_ssm_c_im, v_ssm_d=v_ssm_d, v_w_ssm_glu=v_w_ssm_glu, v_w_att_up=v_w_att_up, v_conv_w=v_conv_w, v_conv_b=v_conv_b, v_conv_ln_g=v_conv_ln_g, v_conv_ln_b=v_conv_ln_b, v_w_conv_pw2=v_w_conv_pw2, v_w_out=v_w_out, v_norm2_g=v_norm2_g, v_w_ffn_in=v_w_ffn_in, v_w_ffn_out=v_w_ffn_out, v_final_g=v_final_g)
    weights = {n: given[n] for n in TWIN_WEIGHTS}
    shared = {n: given[n] for n in SHARED_INPUTS}
    per_example = {n: given[n] for n in ['x']}
    grad_fn = _jax.value_and_grad(_loss, argnums=(0, 1))

    def one_microbatch(ex, loss_target):
        ex = dict(ex)
        diff = ex.pop(TWIN_DIFF_INPUT)
        return grad_fn(weights, diff, {**shared, **ex}, loss_target)

    if N_MICROBATCH == 1:
        loss, (grad_w, grad_x) = one_microbatch(per_example, given["loss_target"])
    else:
        def body(carry, xs):
            loss_sum, grad_sum = carry
            l_k, (gw_k, gx_k) = one_microbatch(xs[0], xs[1])
            with _jax.named_scope("update"):
                return (loss_sum + l_k, _jax.tree.map(_jnp.add, grad_sum, gw_k)), gx_k

        init = (_jnp.zeros((), _jnp.float32), _jax.tree.map(_jnp.zeros_like, weights))
        (loss, grad_w), grad_x = _jax.lax.scan(body, init, (per_example, given["loss_target"]))
    with _jax.named_scope("update"):
        delta_w, new_m, new_v = {}, {}, {}
        for n in TWIN_WEIGHTS:
            delta_w[n], new_m[n], new_v[n] = _adamw(weights[n], grad_w[n], given["m_" + n], given["v_" + n])
    return (loss, grad_x, *[grad_w[n] for n in TWIN_WEIGHTS], *[delta_w[n] for n in TWIN_WEIGHTS],
            *[new_m[n] for n in TWIN_WEIGHTS], *[new_v[n] for n in TWIN_WEIGHTS])
```

```python
import functools
import math

import jax
import jax.numpy as jnp
from jax import lax
from jax.experimental import pallas as pl
from jax.experimental.pallas import tpu as pltpu

f32 = jnp.float32
bf16 = jnp.bfloat16

D = 1024
DEPTH = 2
EPS = 1e-6
BW = 512
NG = 32
GH = 16
NS = 64
NSTATE = NG * NS
HD = 64
NH = 8
PATTERNS = ((128, 1), (512, 4), (2048, 16))
ABLK = 128
ATT_SCALE = HD ** -0.5
CW = 31
DFF = 2816
INC = 7168
NDEV = 8
NSH_IN = INC // NDEV
NSH_FF = 2 * DFF // NDEV
ADAM_LR, ADAM_B1, ADAM_B2, ADAM_EPS, ADAM_WD, ADAM_STEP = 0.001, 0.9, 0.999, 1e-08, 0.01, 10

TB = 512
TS = 256
SJ = 4
SW = NSTATE // SJ
SU = BW // SJ
NEG = -1e30
MESH = pl.DeviceIdType.MESH
ANY = pl.BlockSpec(memory_space=pl.ANY)


def _cp(n_axes, vmem_mb=48):
    return pltpu.CompilerParams(dimension_semantics=("arbitrary",) * n_axes,
                                vmem_limit_bytes=vmem_mb * 1024 * 1024)


def _dot(a, b):
    return jnp.dot(a, b, preferred_element_type=f32)


def _dot_nt(a, b):
    return lax.dot_general(a, b, (((1,), (1,)), ((), ())), preferred_element_type=f32)


def _dot_tn(a, b):
    return lax.dot_general(a, b, (((0,), (0,)), ((), ())), preferred_element_type=f32)


def _dot_hi(a, b):
    return jnp.dot(a, b, precision=lax.Precision.HIGHEST, preferred_element_type=f32)


def _sigmoid(x):
    return 1.0 / (1.0 + jnp.exp(-x))


_GC = math.sqrt(2.0 / math.pi)


def _gelu(x):
    return 0.5 * x * (1.0 + jnp.tanh(_GC * (x + 0.044715 * x * x * x)))


def _gelu_grad(x):
    t = jnp.tanh(_GC * (x + 0.044715 * x * x * x))
    return 0.5 * (1.0 + t) + 0.5 * x * (1.0 - t * t) * _GC * (1.0 + 3.0 * 0.044715 * x * x)


def _rms_stats(x):
    return lax.rsqrt(jnp.mean(x * x, axis=-1, keepdims=True) + EPS)


def _rms_bwd(x, g, dh):
    r = _rms_stats(x)
    dyg = dh * g
    dx = r * dyg - x * (r * r * r) * jnp.mean(dyg * x, axis=-1, keepdims=True)
    dg = jnp.sum(dh * x * r, axis=0, keepdims=True)
    return dx, dg


def rms_fwd(x, g):
    n = x.shape[0]

    def body(x_ref, g_ref, h_ref):
        xv = x_ref[...]
        h_ref[...] = (xv * _rms_stats(xv) * g_ref[...]).astype(bf16)

    return pl.pallas_call(
        body, name="rms_fwd", grid=(n // TB,),
        in_specs=[pl.BlockSpec((TB, D), lambda i: (i, 0)), pl.BlockSpec((1, D), lambda i: (0, 0))],
        out_specs=pl.BlockSpec((TB, D), lambda i: (i, 0)),
        out_shape=jax.ShapeDtypeStruct((n, D), bf16), compiler_params=_cp(1))(x, g)


def inproj(h, w4, layer):
    n = h.shape[0]
    tm = 1024

    def body(h_ref, w_ref, o_ref):
        o_ref[...] = _dot(h_ref[...], w_ref[0, 0]).astype(bf16)

    return pl.pallas_call(
        body, name="inproj", grid=(NDEV, n // tm),
        in_specs=[pl.BlockSpec((tm, D), lambda s, i: (i, 0)),
                  pl.BlockSpec((1, 1, D, NSH_IN), lambda s, i: (layer, s, 0, 0))],
        out_specs=pl.BlockSpec((tm, NSH_IN), lambda s, i: (i, s)),
        out_shape=jax.ShapeDtypeStruct((n, INC), bf16), compiler_params=_cp(2))(h, w4)


def inproj_bwd(dproj, w4, layer, x, g, dres):
    n = x.shape[0]
    tm = 1024

    def body(dp_ref, w_ref, x_ref, g_ref, dr_ref, dx_ref, dg_ref, acc):
        i, s = pl.program_id(0), pl.program_id(1)

        @pl.when(s == 0)
        def _():
            acc[...] = jnp.zeros_like(acc)

        @pl.when((s == 0) & (i == 0))
        def _():
            dg_ref[...] = jnp.zeros_like(dg_ref)

        acc[...] += _dot_nt(dp_ref[...], w_ref[0, 0])

        @pl.when(s == NDEV - 1)
        def _():
            dx, dg = _rms_bwd(x_ref[...], g_ref[...], acc[...])
            dx_ref[...] = dr_ref[...] + dx
            dg_ref[...] += dg

    return pl.pallas_call(
        body, name="inproj_bwd", grid=(n // tm, NDEV),
        in_specs=[pl.BlockSpec((tm, NSH_IN), lambda i, s: (i, s)),
                  pl.BlockSpec((1, 1, D, NSH_IN), lambda i, s: (layer, s, 0, 0)),
                  pl.BlockSpec((tm, D), lambda i, s: (i, 0)),
                  pl.BlockSpec((1, D), lambda i, s: (0, 0)),
                  pl.BlockSpec((tm, D), lambda i, s: (i, 0))],
        out_specs=[pl.BlockSpec((tm, D), lambda i, s: (i, 0)), pl.BlockSpec((1, D), lambda i, s: (0, 0))],
        out_shape=[jax.ShapeDtypeStruct((n, D), f32), jax.ShapeDtypeStruct((1, D), f32)],
        scratch_shapes=[pltpu.VMEM((tm, D), f32)], compiler_params=_cp(2))(dproj, w4, x, g, dres)


def mm_tn(a, b, a_spec, b_spec, n_sh, ka, nb, m, name):
    tm = 1024

    def body(a_ref, b_ref, o_ref, acc):
        i = pl.program_id(1)

        @pl.when(i == 0)
        def _():
            acc[...] = jnp.zeros_like(acc)

        av = a_ref[...].reshape(tm, ka).astype(bf16)
        bv = b_ref[...].reshape(tm, nb).astype(bf16)
        acc[...] += _dot_tn(av, bv)

        @pl.when(i == m // tm - 1)
        def _():
            o_ref[0] = acc[...].astype(bf16)

    return pl.pallas_call(
        body, name=name, grid=(n_sh, m // tm), in_specs=[a_spec, b_spec],
        out_specs=pl.BlockSpec((1, ka, nb), lambda s, i: (s, 0, 0)),
        out_shape=jax.ShapeDtypeStruct((n_sh, ka, nb), bf16),
        scratch_shapes=[pltpu.VMEM((ka, nb), f32)], compiler_params=_cp(2))(a, b)


def ffn_fwd(x1, g2, w1, w2, layer):
    n = x1.shape[0]
    w2p = w2.reshape(DEPTH, 4, NSH_FF, D)

    def body(x_ref, g_ref, wa_ref, wb_ref, w2_ref, o_ref, h_sc):
        s = pl.program_id(1)

        @pl.when(s == 0)
        def _():
            xv = x_ref[...]
            h_sc[...] = (xv * _rms_stats(xv) * g_ref[...]).astype(bf16)
            o_ref[...] = xv

        h = h_sc[...]
        z1 = _dot(h, wa_ref[0, 0])
        z2 = _dot(h, wb_ref[0, 0])
        a = (z1 * _sigmoid(z1) * z2).astype(bf16)
        o_ref[...] += _dot(a, w2_ref[0, 0])

    return pl.pallas_call(
        body, name="ffn_fwd", grid=(n // TB, 4),
        in_specs=[pl.BlockSpec((TB, D), lambda i, s: (i, 0)),
                  pl.BlockSpec((1, D), lambda i, s: (0, 0)),
                  pl.BlockSpec((1, 1, D, NSH_FF), lambda i, s: (layer, s, 0, 0)),
                  pl.BlockSpec((1, 1, D, NSH_FF), lambda i, s: (layer, s + 4, 0, 0)),
                  pl.BlockSpec((1, 1, NSH_FF, D), lambda i, s: (layer, s, 0, 0))],
        out_specs=pl.BlockSpec((TB, D), lambda i, s: (i, 0)),
        out_shape=jax.ShapeDtypeStruct((n, D), f32),
        scratch_shapes=[pltpu.VMEM((TB, D), bf16)], compiler_params=_cp(2))(x1, g2, w1, w1, w2p)


def ffn_bwd(x1, g2, w1, w2, layer, dx2):
    n = x1.shape[0]
    w2p = w2.reshape(DEPTH, 4, NSH_FF, D)

    def body(x_ref, g_ref, dy_ref, wa_ref, wb_ref, w2_ref,
             dx_ref, h_ref, dz1_ref, dz2_ref, a_ref, dg_ref, dh_sc, dyb_sc):
        i, s = pl.program_id(0), pl.program_id(1)

        @pl.when(s == 0)
        def _():
            xv = x_ref[...]
            h_ref[...] = (xv * _rms_stats(xv) * g_ref[...]).astype(bf16)
            dh_sc[...] = jnp.zeros_like(dh_sc)
            dyb_sc[...] = dy_ref[...].astype(bf16)

        @pl.when((s == 0) & (i == 0))
        def _():
            dg_ref[...] = jnp.zeros_like(dg_ref)

        h = h_ref[...]
        z1 = _dot(h, wa_ref[0, 0])
        z2 = _dot(h, wb_ref[0, 0])
        sg = _sigmoid(z1)
        sl = z1 * sg
        a_ref[0] = (sl * z2).astype(bf16)
        da = _dot_nt(dyb_sc[...], w2_ref[0, 0])
        dz2 = (da * sl).astype(bf16)
        dz1 = (da * z2 * sg * (1.0 + z1 * (1.0 - sg))).astype(bf16)
        dz1_ref[0] = dz1
        dz2_ref[0] = dz2
        dh_sc[...] += _dot_nt(dz1, wa_ref[0, 0]) + _dot_nt(dz2, wb_ref[0, 0])

        @pl.when(s == 3)
        def _():
            dx, dg = _rms_bwd(x_ref[...], g_ref[...], dh_sc[...])
            dx_ref[...] = dy_ref[...] + dx
            dg_ref[...] += dg

    tok = lambda i, s: (i, 0)
    sh3 = lambda i, s: (s, i, 0)
    return pl.pallas_call(
        body, name="ffn_bwd", grid=(n // TB, 4),
        in_specs=[pl.BlockSpec((TB, D), tok), pl.BlockSpec((1, D), lambda i, s: (0, 0)), pl.BlockSpec((TB, D), tok),
                  pl.BlockSpec((1, 1, D, NSH_FF), lambda i, s: (layer, s, 0, 0)),
                  pl.BlockSpec((1, 1, D, NSH_FF), lambda i, s: (layer, s + 4, 0, 0)),
                  pl.BlockSpec((1, 1, NSH_FF, D), lambda i, s: (layer, s, 0, 0))],
        out_specs=[pl.BlockSpec((TB, D), tok), pl.BlockSpec((TB, D), tok),
                   pl.BlockSpec((1, TB, NSH_FF), sh3), pl.BlockSpec((1, TB, NSH_FF), sh3),
                   pl.BlockSpec((1, TB, NSH_FF), sh3), pl.BlockSpec((1, D), lambda i, s: (0, 0))],
        out_shape=[jax.ShapeDtypeStruct((n, D), f32), jax.ShapeDtypeStruct((n, D), bf16),
                   jax.ShapeDtypeStruct((4, n, NSH_FF), bf16), jax.ShapeDtypeStruct((4, n, NSH_FF), bf16),
                   jax.ShapeDtypeStruct((4, n, NSH_FF), bf16), jax.ShapeDtypeStruct((1, D), f32)],
        scratch_shapes=[pltpu.VMEM((TB, D), f32), pltpu.VMEM((TB, D), bf16)],
        compiler_params=_cp(2))(x1, g2, dx2, w1, w1, w2p)


def loss_head(x, g, target):
    n = x.shape[0]

    def body(x_ref, g_ref, t_ref, l_ref, dx_ref, dg_ref):
        i = pl.program_id(0)

        @pl.when(i == 0)
        def _():
            l_ref[...] = jnp.zeros_like(l_ref)
            dg_ref[...] = jnp.zeros_like(dg_ref)

        xv = x_ref[...]
        y = xv * _rms_stats(xv) * g_ref[...]
        e = y - t_ref[...]
        l_ref[...] += 0.5 * jnp.sum(jnp.sum(e * e, axis=-1, keepdims=True), axis=0, keepdims=True) * (1.0 / D)
        dx, dg = _rms_bwd(xv, g_ref[...], e * (1.0 / D))
        dx_ref[...] = dx
        dg_ref[...] += dg

    tok = lambda i: (i, 0)
    return pl.pallas_call(
        body, name="loss_head", grid=(n // TB,),
        in_specs=[pl.BlockSpec((TB, D), tok), pl.BlockSpec((1, D), lambda i: (0, 0)), pl.BlockSpec((TB, D), tok)],
        out_specs=[pl.BlockSpec((8, 128), lambda i: (0, 0)), pl.BlockSpec((TB, D), tok),
                   pl.BlockSpec((1, D), lambda i: (0, 0))],
        out_shape=[jax.ShapeDtypeStruct((8, 128), f32), jax.ShapeDtypeStruct((n, D), f32),
                   jax.ShapeDtypeStruct((1, D), f32)],
        compiler_params=_cp(1))(x, g, target)


def _disc(lr, li, ld):
    dt = jnp.exp(ld)
    mag = jnp.exp(lr * dt)
    ar = mag * jnp.cos(li * dt)
    ai = mag * jnp.sin(li * dt)
    nr, ni = ar - 1.0, ai
    den = lr * lr + li * li
    zr = (nr * lr + ni * li) / den
    zi = (ni * lr - nr * li) / den
    return ar, ai, zr, zi


def _blockdiag_mask(shape):
    r = lax.broadcasted_iota(jnp.int32, shape, 0) // GH
    c = lax.broadcasted_iota(jnp.int32, shape, 1) // NS
    return r == c


def s5_params(lr, li, ld, btr, bti, ctr, cti):
    nlog = TS.bit_length() - 1

    def body(lr_ref, li_ref, ld_ref, btr_ref, bti_ref, ctr_ref, cti_ref, ap_ref, pf_ref, pr_ref, bb_ref, cb_ref):
        ar, ai, zr, zi = _disc(lr_ref[...], li_ref[...], ld_ref[...])
        pr_, pi_ = ar, ai
        ap_ref[...] = jnp.zeros_like(ap_ref)
        for k in range(nlog + 1):
            ap_ref[0, k:k + 1, :] = pr_
            ap_ref[1, k:k + 1, :] = pi_
            pr_, pi_ = pr_ * pr_ - pi_ * pi_, 2.0 * pr_ * pi_
        row = lax.broadcasted_iota(jnp.int32, (TS, NSTATE), 0)
        for rev in (False, True):
            tr = jnp.broadcast_to(ar, (TS, NSTATE))
            ti = jnp.broadcast_to(ai, (TS, NSTATE))
            for k in range(nlog):
                sh = 1 << k
                if rev:
                    qr = jnp.where(row < TS - sh, pltpu.roll(tr, TS - sh, 0), 1.0)
                    qi = jnp.where(row < TS - sh, pltpu.roll(ti, TS - sh, 0), 0.0)
                else:
                    qr = jnp.where(row >= sh, pltpu.roll(tr, sh, 0), 1.0)
                    qi = jnp.where(row >= sh, pltpu.roll(ti, sh, 0), 0.0)
                tr, ti = tr * qr - ti * qi, tr * qi + ti * qr
            dst = pr_ref if rev else pf_ref
            dst[0] = tr
            dst[1] = ti
        bbr = zr * btr_ref[...] - zi * bti_ref[...]
        bbi = zr * bti_ref[...] + zi * btr_ref[...]
        mask = _blockdiag_mask((SU, SW))
        for j in range(SJ):
            cols = slice(j * SW, (j + 1) * SW)
            for c, (vb, vc) in enumerate(((bbr, ctr_ref[...]), (bbi, cti_ref[...]))):
                bb_ref[c, j] = jnp.where(mask, jnp.tile(vb[:, cols], (SU // GH, 1)), 0.0).astype(bf16)
                cb_ref[c, j] = jnp.where(mask, jnp.tile(vc[:, cols], (SU // GH, 1)), 0.0).astype(bf16)

    return pl.pallas_call(
        body, name="s5_params",
        out_shape=[jax.ShapeDtypeStruct((2, 16, NSTATE), f32), jax.ShapeDtypeStruct((2, TS, NSTATE), f32),
                   jax.ShapeDtypeStruct((2, TS, NSTATE), f32), jax.ShapeDtypeStruct((2, SJ, SU, SW), bf16),
                   jax.ShapeDtypeStruct((2, SJ, SU, SW), bf16)],
        compiler_params=pltpu.CompilerParams(vmem_limit_bytes=56 * 1024 * 1024))(lr, li, ld, btr, bti, ctr, cti)


def s5_params_bwd(lr, li, ld, btr, bti, d_a, d_bb, d_cb):
    def body(lr_ref, li_ref, ld_ref, btr_ref, bti_ref, da_ref, dbb_ref, dcb_ref,
             dlr_ref, dli_ref, dld_ref, dbt_ref, dct_ref):
        mask = _blockdiag_mask((SU, SW))

        def fold(ref, c):
            parts = []
            for j in range(SJ):
                v = jnp.where(mask, ref[c, j], 0.0)
                parts.append(v.reshape(SU // GH, GH, SW).sum(axis=0))
            return jnp.concatenate(parts, axis=1)

        dct_ref[0] = fold(dcb_ref, 0)
        dct_ref[1] = fold(dcb_ref, 1)
        dbbr, dbbi = fold(dbb_ref, 0), fold(dbb_ref, 1)
        lrv, liv, ldv = lr_ref[...], li_ref[...], ld_ref[...]
        (ar, ai, zr, zi), vjp = jax.vjp(_disc, lrv, liv, ldv)
        btr, bti = btr_ref[...], bti_ref[...]
        dbt_ref[0] = zr * dbbr + zi * dbbi
        dbt_ref[1] = zr * dbbi - zi * dbbr
        dzr = jnp.sum(dbbr * btr + dbbi * bti, axis=0, keepdims=True)
        dzi = jnp.sum(dbbi * btr - dbbr * bti, axis=0, keepdims=True)
        dlr, dli, dld = vjp((da_ref[0:1, :], da_ref[1:2, :], dzr, dzi))
        dlr_ref[...] = dlr
        dli_ref[...] = dli
        ind = (lax.broadcasted_iota(jnp.int32, (NSTATE, 128), 0) // NS
               == lax.broadcasted_iota(jnp.int32, (NSTATE, 128), 1)).astype(f32)
        dld_ref[...] = _dot_hi(jnp.broadcast_to(dld, (8, NSTATE)), ind)

    return pl.pallas_call(
        body, name="s5_params_bwd",
        out_shape=[jax.ShapeDtypeStruct((1, NSTATE), f32), jax.ShapeDtypeStruct((1, NSTATE), f32),
                   jax.ShapeDtypeStruct((8, 128), f32), jax.ShapeDtypeStruct((2, GH, NSTATE), f32),
                   jax.ShapeDtypeStruct((2, GH, NSTATE), f32)],
        compiler_params=pltpu.CompilerParams(vmem_limit_bytes=56 * 1024 * 1024))(lr, li, ld, btr, bti, d_a, d_bb, d_cb)


def _scan_rows(br, bi, ap_ref, reverse):
    row = lax.broadcasted_iota(jnp.int32, br.shape, 0)
    sr, si = br, bi
    for k in range(TS.bit_length() - 1):
        sh = 1 << k
        ar = ap_ref[0, k:k + 1, :]
        ai = ap_ref[1, k:k + 1, :]
        if reverse:
            ai = -ai
            keep = row < TS - sh
            qr = jnp.where(keep, pltpu.roll(sr, TS - sh, 0), 0.0)
            qi = jnp.where(keep, pltpu.roll(si, TS - sh, 0), 0.0)
        else:
            keep = row >= sh
            qr = jnp.where(keep, pltpu.roll(sr, sh, 0), 0.0)
            qi = jnp.where(keep, pltpu.roll(si, sh, 0), 0.0)
        sr, si = sr + ar * qr - ai * qi, si + ar * qi + ai * qr
    return sr, si


def _s5_states(u, cin, ap_ref, pf_ref, bb_ref):
    sr, si = _scan_rows(_dot(u, bb_ref[0, 0]), _dot(u, bb_ref[1, 0]), ap_ref, False)
    cr, ci = cin[0:1, :], cin[1:2, :]
    pr_, pi_ = pf_ref[0], pf_ref[1]
    return sr + pr_ * cr - pi_ * ci, si + pr_ * ci + pi_ * cr


def s5_fwd(proj3, apow, pf, bb, cb, dskip):
    b, l, _ = proj3.shape
    nch = l // TS

    def body(u_ref, ap_ref, pf_ref, bb_ref, cb_ref, d_ref, y_ref, cin_ref, carry):
        i = pl.program_id(2)

        @pl.when(i == 0)
        def _():
            carry[...] = jnp.zeros_like(carry)

        cin = carry[...]
        cin_ref[0, 0] = cin
        u = u_ref[0]
        sr, si = _s5_states(u, cin, ap_ref, pf_ref, bb_ref)
        carry[0:1, :] = sr[TS - 1:TS, :]
        carry[1:2, :] = si[TS - 1:TS, :]
        y = _dot_nt(sr.astype(bf16), cb_ref[0, 0]) - _dot_nt(si.astype(bf16), cb_ref[1, 0])
        y_ref[0] = y + d_ref[...] * u.astype(f32)

    return pl.pallas_call(
        body, name="s5_fwd", grid=(SJ, b, nch),
        in_specs=[pl.BlockSpec((1, TS, SU), lambda j, bi, i: (bi, i, j)),
                  pl.BlockSpec((2, 16, SW), lambda j, bi, i: (0, 0, j)),
                  pl.BlockSpec((2, TS, SW), lambda j, bi, i: (0, 0, j)),
                  pl.BlockSpec((2, 1, SU, SW), lambda j, bi, i: (0, j, 0, 0)),
                  pl.BlockSpec((2, 1, SU, SW), lambda j, bi, i: (0, j, 0, 0)),
                  pl.BlockSpec((1, SU), lambda j, bi, i: (0, j))],
        out_specs=[pl.BlockSpec((1, TS, SU), lambda j, bi, i: (bi, i, j)),
                   pl.BlockSpec((1, 1, 2, SW), lambda j, bi, i: (bi, i, 0, j))],
        out_shape=[jax.ShapeDtypeStruct((b, l, BW), f32), jax.ShapeDtypeStruct((b, nch, 2, NSTATE), f32)],
        scratch_shapes=[pltpu.VMEM((2, SW), f32)], compiler_params=_cp(3))(proj3, apow, pf, bb, cb, dskip)


def s5_bwd(proj3, dy, cins, apow, pf, pr, bb, cb, dskip):
    b, l, _ = proj3.shape
    nch = l // TS

    def body(u_ref, dy_ref, cin_ref, ap_ref, pf_ref, pr_ref, bb_ref, cb_ref, d_ref,
             du_ref, da_ref, dbb_ref, dcb_ref, dd_ref, gcarry):
        bi, i = pl.program_id(1), pl.program_id(2)

        @pl.when(i == 0)
        def _():
            gcarry[...] = jnp.zeros_like(gcarry)

        @pl.when((i == 0) & (bi == 0))
        def _():
            da_ref[...] = jnp.zeros_like(da_ref)
            dbb_ref[...] = jnp.zeros_like(dbb_ref)
            dcb_ref[...] = jnp.zeros_like(dcb_ref)
            dd_ref[...] = jnp.zeros_like(dd_ref)

        u = u_ref[0]
        dyv = dy_ref[0]
        dyb = dyv.astype(bf16)
        cin = cin_ref[0, 0]
        sr, si = _s5_states(u, cin, ap_ref, pf_ref, bb_ref)
        gr, gi = _scan_rows(_dot(dyb, cb_ref[0, 0]), -_dot(dyb, cb_ref[1, 0]), ap_ref, True)
        gcr, gci = gcarry[0:1, :], gcarry[1:2, :]
        qr, qi = pr_ref[0], pr_ref[1]
        gr, gi = gr + qr * gcr + qi * gci, gi + qr * gci - qi * gcr
        gcarry[0:1, :] = gr[0:1, :]
        gcarry[1:2, :] = gi[0:1, :]
        srb, sib, grb, gib = sr.astype(bf16), si.astype(bf16), gr.astype(bf16), gi.astype(bf16)
        dcb_ref[0, 0] += _dot_tn(dyb, srb)
        dcb_ref[1, 0] -= _dot_tn(dyb, sib)
        dbb_ref[0, 0] += _dot_tn(u, grb)
        dbb_ref[1, 0] += _dot_tn(u, gib)
        du = _dot_nt(grb, bb_ref[0, 0]) + _dot_nt(gib, bb_ref[1, 0]) + d_ref[...] * dyv
        du_ref[0] = du.astype(bf16)
        dd_ref[...] += jnp.sum(dyv * u.astype(f32), axis=0, keepdims=True)
        row = lax.broadcasted_iota(jnp.int32, sr.shape, 0)
        spr = jnp.where(row == 0, cin[0:1, :], pltpu.roll(sr, 1, 0))
        spi = jnp.where(row == 0, cin[1:2, :], pltpu.roll(si, 1, 0))
        da_ref[0:1, :] += jnp.sum(spr * gr + spi * gi, axis=0, keepdims=True)
        da_ref[1:2, :] += jnp.sum(spr * gi - spi * gr, axis=0, keepdims=True)

    rc = lambda i: nch - 1 - i
    return pl.pallas_call(
        body, name="s5_bwd", grid=(SJ, b, nch),
        in_specs=[pl.BlockSpec((1, TS, SU), lambda j, bi, i: (bi, rc(i), j)),
                  pl.BlockSpec((1, TS, SU), lambda j, bi, i: (bi, rc(i), j)),
                  pl.BlockSpec((1, 1, 2, SW), lambda j, bi, i: (bi, rc(i), 0, j)),
                  pl.BlockSpec((2, 16, SW), lambda j, bi, i: (0, 0, j)),
                  pl.BlockSpec((2, TS, SW), lambda j, bi, i: (0, 0, j)),
                  pl.BlockSpec((2, TS, SW), lambda j, bi, i: (0, 0, j)),
                  pl.BlockSpec((2, 1, SU, SW), lambda j, bi, i: (0, j, 0, 0)),
                  pl.BlockSpec((2, 1, SU, SW), lambda j, bi, i: (0, j, 0, 0)),
                  pl.BlockSpec((1, SU), lambda j, bi, i: (0, j))],
        out_specs=[pl.BlockSpec((1, TS, SU), lambda j, bi, i: (bi, rc(i), j)),
                   pl.BlockSpec((2, SW), lambda j, bi, i: (0, j)),
                   pl.BlockSpec((2, 1, SU, SW), lambda j, bi, i: (0, j, 0, 0)),
                   pl.BlockSpec((2, 1, SU, SW), lambda j, bi, i: (0, j, 0, 0)),
                   pl.BlockSpec((1, SU), lambda j, bi, i: (0, j))],
        out_shape=[jax.ShapeDtypeStruct((b, l, BW), bf16), jax.ShapeDtypeStruct((2, NSTATE), f32),
                   jax.ShapeDtypeStruct((2, SJ, SU, SW), f32), jax.ShapeDtypeStruct((2, SJ, SU, SW), f32),
                   jax.ShapeDtypeStruct((1, BW), f32)],
        scratch_shapes=[pltpu.VMEM((2, SW), f32)],
        compiler_params=_cp(3))(proj3, dy, cins, apow, pf, pr, bb, cb, dskip)


def _att_mask(n):
    qi = lax.broadcasted_iota(jnp.int32, (ABLK, 2 * ABLK), 0)
    kj = lax.broadcasted_iota(jnp.int32, (ABLK, 2 * ABLK), 1)
    return (kj >= qi) & (kj <= qi + ABLK) & ((n > 0) | (kj >= ABLK))


def att_fwd(proj, bsz, g_idx, dil):
    n_tok = proj.shape[0]
    ls = n_tok // bsz // dil
    nb = ls // ABLK
    pv = proj.reshape(bsz, ls, dil * INC)
    cpb = INC // BW

    def body(q_ref, kp_ref, kc_ref, vp_ref, vc_ref, o_ref, lse_ref):
        n = pl.program_id(2)
        valid = _att_mask(n)
        lane = lax.broadcasted_iota(jnp.int32, (ABLK, 128), 1)
        lse_all = jnp.zeros((ABLK, 128), f32)
        for h in range(NH):
            cs = slice(h * HD, (h + 1) * HD)
            q = q_ref[0, :, cs]
            k = jnp.concatenate([kp_ref[0, :, cs], kc_ref[0, :, cs]], axis=0)
            v = jnp.concatenate([vp_ref[0, :, cs], vc_ref[0, :, cs]], axis=0)
            s = jnp.where(valid, _dot_nt(q, k) * ATT_SCALE, NEG)
            m = jnp.max(s, axis=-1, keepdims=True)
            p = jnp.exp(s - m)
            den = jnp.sum(p, axis=-1, keepdims=True)
            o = _dot(p.astype(bf16), v) / den
            o_ref[0, :, cs] = o.astype(bf16)
            lse_all = lse_all + jnp.where(lane == h, m + jnp.log(den), 0.0)
        lse_ref[0] = lse_all

    blk = lambda c: pl.BlockSpec((1, ABLK, BW), lambda b, r, n: (b, n, r * cpb + c))
    prev = lambda c: pl.BlockSpec((1, ABLK, BW), lambda b, r, n: (b, jnp.maximum(n - 1, 0), r * cpb + c))
    o, lse = pl.pallas_call(
        body, name=f"att_fwd{g_idx}", grid=(bsz, dil, nb),
        in_specs=[blk(1 + g_idx), prev(4), blk(4), prev(5), blk(5)],
        out_specs=[pl.BlockSpec((1, ABLK, BW), lambda b, r, n: (b, n, r)),
                   pl.BlockSpec((1, ABLK, 128), lambda b, r, n: (b, n, r))],
        out_shape=[jax.ShapeDtypeStruct((bsz, ls, dil * BW), bf16), jax.ShapeDtypeStruct((bsz, ls, dil * 128), f32)],
        compiler_params=_cp(3))(pv, pv, pv, pv, pv)
    return o.reshape(n_tok, BW), lse.reshape(n_tok, 128)


def att_bwd(proj, do, lse_tot, delta, bsz, g_idx, dil):
    n_tok = proj.shape[0]
    ls = n_tok // bsz // dil
    nb = ls // ABLK
    pv = proj.reshape(bsz, ls, dil * INC)
    dov = do.reshape(bsz, ls, dil * BW)
    lv = lse_tot.reshape(bsz, ls, dil * 128)
    dlv = delta.reshape(bsz, ls, dil * 128)
    cpb = INC // BW

    def body(q_ref, kp_ref, kc_ref, vp_ref, vc_ref, do_ref, l_ref, dl_ref, dq_ref, dk_ref, dv_ref):
        n = pl.program_id(2)

        @pl.when(n == 0)
        def _():
            dk_ref[...] = jnp.zeros_like(dk_ref)
            dv_ref[...] = jnp.zeros_like(dv_ref)

        valid = _att_mask(n)
        cur = pl.ds(pl.multiple_of(n * ABLK, ABLK), ABLK)
        prv = pl.ds(pl.multiple_of(jnp.maximum(n - 1, 0) * ABLK, ABLK), ABLK)
        for h in range(NH):
            cs = slice(h * HD, (h + 1) * HD)
            q = q_ref[0, :, cs]
            k = jnp.concatenate([kp_ref[0, :, cs], kc_ref[0, :, cs]], axis=0)
            v = jnp.concatenate([vp_ref[0, :, cs], vc_ref[0, :, cs]], axis=0)
            dob = do_ref[0, :, cs]
            s = _dot_nt(q, k) * ATT_SCALE
            p = jnp.where(valid, jnp.exp(jnp.minimum(s - l_ref[0, :, h:h + 1], 60.0)), 0.0)
            dp = _dot_nt(dob, v)
            ds = (p * (dp - dl_ref[0, :, h:h + 1]) * ATT_SCALE).astype(bf16)
            pb = p.astype(bf16)
            dq_ref[0, :, cs] = _dot(ds, k).astype(bf16)
            dk = _dot_tn(ds, q)
            dv = _dot_tn(pb, dob)
            dk_ref[0, cur, cs] += dk[ABLK:]
            dv_ref[0, cur, cs] += dv[ABLK:]

            @pl.when(n > 0)
            def _():
                dk_ref[0, prv, cs] += dk[:ABLK]
                dv_ref[0, prv, cs] += dv[:ABLK]

    blk = lambda c: pl.BlockSpec((1, ABLK, BW), lambda b, r, n: (b, n, r * cpb + c))
    prev = lambda c: pl.BlockSpec((1, ABLK, BW), lambda b, r, n: (b, jnp.maximum(n - 1, 0), r * cpb + c))
    own = pl.BlockSpec((1, ABLK, BW), lambda b, r, n: (b, n, r))
    own128 = pl.BlockSpec((1, ABLK, 128), lambda b, r, n: (b, n, r))
    whole = pl.BlockSpec((1, ls, BW), lambda b, r, n: (b, 0, r))
    dq, dk, dv = pl.pallas_call(
        body, name=f"att_bwd{g_idx}", grid=(bsz, dil, nb),
        in_specs=[blk(1 + g_idx), prev(4), blk(4), prev(5), blk(5), own, own128, own128],
        out_specs=[own, whole, whole],
        out_shape=[jax.ShapeDtypeStruct((bsz, ls, dil * BW), bf16), jax.ShapeDtypeStruct((bsz, ls, dil * BW), f32),
                   jax.ShapeDtypeStruct((bsz, ls, dil * BW), f32)],
        compiler_params=_cp(3))(pv, pv, pv, pv, pv, dov, lv, dlv)
    return dq.reshape(n_tok, BW), dk.reshape(n_tok, BW), dv.reshape(n_tok, BW)


CPAD = 32
CCH = 256


def conv_fwd(proj3, cw, cb):
    b, l, _ = proj3.shape

    def body(cv_ref, w_ref, b_ref, o_ref, pad):
        pad[0:CPAD, :] = jnp.zeros((CPAD, BW), f32)
        pad[CPAD:, :] = cv_ref[0, :, :BW].astype(f32) * _sigmoid(cv_ref[0, :, BW:].astype(f32))
        for c in range(l // CCH):
            acc = jnp.zeros((CCH, BW), f32) + b_ref[...]
            for k in range(CW):
                acc = acc + w_ref[k:k + 1, :] * pad[pl.ds(c * CCH + CPAD - (CW - 1) + k, CCH), :]
            o_ref[0, c * CCH:(c + 1) * CCH, :] = acc

    return pl.pallas_call(
        body, name="conv_fwd", grid=(b,),
        in_specs=[pl.BlockSpec((1, l, 2 * BW), lambda i: (i, 0, 3)),
                  pl.BlockSpec((32, BW), lambda i: (0, 0)), pl.BlockSpec((1, BW), lambda i: (0, 0))],
        out_specs=pl.BlockSpec((1, l, BW), lambda i: (i, 0, 0)),
        out_shape=jax.ShapeDtypeStruct((b, l, BW), f32),
        scratch_shapes=[pltpu.VMEM((l + CPAD, BW), f32)], compiler_params=_cp(1))(proj3, cw, cb)


def conv_bwd(proj3, dhc, cw):
    b, l, _ = proj3.shape

    def body(cv_ref, d_ref, w_ref, dcv_ref, dw_ref, db_ref, pad, dpad):
        i = pl.program_id(0)

        @pl.when(i == 0)
        def _():
            dw_ref[...] = jnp.zeros_like(dw_ref)
            db_ref[...] = jnp.zeros_like(db_ref)

        pad[0:CPAD, :] = jnp.zeros((CPAD, BW), f32)
        pad[CPAD:, :] = cv_ref[0, :, :BW].astype(f32) * _sigmoid(cv_ref[0, :, BW:].astype(f32))
        dpad[l:, :] = jnp.zeros((CPAD, BW), f32)
        dpad[0:l, :] = d_ref[0]
        db_ref[...] += jnp.sum(d_ref[0], axis=0, keepdims=True)
        for k in range(CW):
            acc = jnp.zeros((1, BW), f32)
            for c in range(l // CCH):
                acc = acc + jnp.sum(d_ref[0, c * CCH:(c + 1) * CCH, :]
                                    * pad[pl.ds(c * CCH + CPAD - (CW - 1) + k, CCH), :], axis=0, keepdims=True)
            dw_ref[k:k + 1, :] += acc
        for c in range(l // CCH):
            rows = slice(c * CCH, (c + 1) * CCH)
            acc = jnp.zeros((CCH, BW), f32)
            for k in range(CW):
                acc = acc + w_ref[k:k + 1, :] * dpad[pl.ds(c * CCH + (CW - 1) - k, CCH), :]
            a = cv_ref[0, rows, :BW].astype(f32)
            sg = _sigmoid(cv_ref[0, rows, BW:].astype(f32))
            dcv_ref[0, rows, :BW] = (acc * sg).astype(bf16)
            dcv_ref[0, rows, BW:] = (acc * a * sg * (1.0 - sg)).astype(bf16)

    return pl.pallas_call(
        body, name="conv_bwd", grid=(b,),
        in_specs=[pl.BlockSpec((1, l, 2 * BW), lambda i: (i, 0, 3)),
                  pl.BlockSpec((1, l, BW), lambda i: (i, 0, 0)),
                  pl.BlockSpec((32, BW), lambda i: (0, 0))],
        out_specs=[pl.BlockSpec((1, l, 2 * BW), lambda i: (i, 0, 0)),
                   pl.BlockSpec((32, BW), lambda i: (0, 0)), pl.BlockSpec((1, BW), lambda i: (0, 0))],
        out_shape=[jax.ShapeDtypeStruct((b, l, 2 * BW), bf16), jax.ShapeDtypeStruct((32, BW), f32),
                   jax.ShapeDtypeStruct((1, BW), f32)],
        scratch_shapes=[pltpu.VMEM((l + CPAD, BW), f32), pltpu.VMEM((l + CPAD, BW), f32)],
        compiler_params=_cp(1))(proj3, dhc, cw)


def _head_expand():
    r = lax.broadcasted_iota(jnp.int32, (128, BW), 0)
    c = lax.broadcasted_iota(jnp.int32, (128, BW), 1) // HD
    return (r == c).astype(f32)


def _head_reduce():
    r = lax.broadcasted_iota(jnp.int32, (BW, 128), 0) // HD
    c = lax.broadcasted_iota(jnp.int32, (BW, 128), 1)
    return (r == c).astype(f32)


def _merge_common(ys_ref, o_refs, l_refs, hc_ref, g_refs, bg_ref, lng_ref, lnb_ref, wglu_ref, watt_ref, wpw_ref):
    r = {}
    ysv = ys_ref[...]
    r["ys"] = ysv
    r["ysin"] = _gelu(ysv).astype(bf16)
    z = _dot(r["ysin"], wglu_ref[...])
    r["z1"], r["sg2"] = z[:, :D], _sigmoid(z[:, D:])
    r["y_s"] = r["z1"] * r["sg2"]
    ls = [lr_[...] for lr_ in l_refs]
    mx = jnp.maximum(jnp.maximum(ls[0], ls[1]), ls[2])
    es = [jnp.exp(v - mx) for v in ls]
    tot = es[0] + es[1] + es[2]
    r["lse_tot"] = mx + jnp.log(tot)
    e_mat = _head_expand()
    o = jnp.zeros(ysv.shape, f32)
    for e, o_ref in zip(es, o_refs):
        o = o + _dot_hi(e / tot, e_mat) * o_ref[...].astype(f32)
    r["o"] = o
    r["ob"] = o.astype(bf16)
    r["y_a"] = _dot(r["ob"], watt_ref[...])
    hc = hc_ref[...]
    mu = jnp.mean(hc, axis=-1, keepdims=True)
    xc = hc - mu
    rstd = lax.rsqrt(jnp.mean(xc * xc, axis=-1, keepdims=True) + EPS)
    r["xh"], r["rstd"] = xc * rstd, rstd
    hn = r["xh"] * lng_ref[...] + lnb_ref[...]
    r["hn"] = hn
    r["sgn"] = _sigmoid(hn)
    r["hs"] = (hn * r["sgn"]).astype(bf16)
    r["y_c"] = _dot(r["hs"], wpw_ref[...])
    r["gates"] = [_sigmoid(g_refs[k][...].astype(f32) + bg_ref[:, k * D:(k + 1) * D]) for k in range(3)]
    r["merged"] = r["gates"][0] * r["y_s"] + r["gates"][1] * r["y_a"] + r["gates"][2] * r["y_c"]
    return r


TBM = 256


def _merge_in_specs(tok, tb):
    w = lambda shape: pl.BlockSpec(shape, lambda i: (0, 0))
    return ([pl.BlockSpec((tb, D), tok), pl.BlockSpec((tb, BW), tok)]
            + [pl.BlockSpec((tb, BW), tok)] * 3 + [pl.BlockSpec((tb, 128), tok)] * 3
            + [pl.BlockSpec((tb, BW), tok)]
            + [pl.BlockSpec((tb, D), lambda i, k=k: (i, 4 + k)) for k in range(3)]
            + [w((1, 3 * D)), w((1, BW)), w((1, BW)), w((BW, 2 * D)), w((BW, D)), w((BW, D)), w((D, D))])


def merge_fwd(x, ys, os_, lses, hc, proj, bg, lng, lnb, wglu, watt, wpw, wout):
    n = x.shape[0]

    def body(x_ref, ys_ref, o1, o2, o3, l1, l2, l3, hc_ref, g0, g1, g2, bg_ref, lng_ref, lnb_ref,
             wglu_ref, watt_ref, wpw_ref, wout_ref, x1_ref):
        r = _merge_common(ys_ref, (o1, o2, o3), (l1, l2, l3), hc_ref, (g0, g1, g2), bg_ref, lng_ref, lnb_ref,
                          wglu_ref, watt_ref, wpw_ref)
        x1_ref[...] = x_ref[...] + _dot(r["merged"].astype(bf16), wout_ref[...])

    tok = lambda i: (i, 0)
    return pl.pallas_call(
        body, name="merge_fwd", grid=(n // TB,), in_specs=_merge_in_specs(tok, TB),
        out_specs=pl.BlockSpec((TB, D), tok), out_shape=jax.ShapeDtypeStruct((n, D), f32),
        compiler_params=_cp(1, 56))(x, ys, *os_, *lses, hc, proj, proj, proj, bg, lng, lnb, wglu, watt, wpw, wout)


def merge_bwd(dx1, ys, os_, lses, hc, proj, bg, lng, lnb, wglu, watt, wpw, wout):
    n = dx1.shape[0]

    def body(dx_ref, ys_ref, o1, o2, o3, l1, l2, l3, hc_ref, g0, g1, g2, bg_ref, lng_ref, lnb_ref,
             wglu_ref, watt_ref, wpw_ref, wout_ref,
             dys_ref, do_ref, delta_ref, ltot_ref, dhc_ref, dgate_ref, ysin_ref, dz_ref, ob_ref, dya_ref,
             hs_ref, dyc_ref, mg_ref, dbg_ref, dlng_ref, dlnb_ref):
        i = pl.program_id(0)

        @pl.when(i == 0)
        def _():
            dbg_ref[...] = jnp.zeros_like(dbg_ref)
            dlng_ref[...] = jnp.zeros_like(dlng_ref)
            dlnb_ref[...] = jnp.zeros_like(dlnb_ref)

        r = _merge_common(ys_ref, (o1, o2, o3), (l1, l2, l3), hc_ref, (g0, g1, g2), bg_ref, lng_ref, lnb_ref,
                          wglu_ref, watt_ref, wpw_ref)
        mg_ref[...] = r["merged"].astype(bf16)
        ysin_ref[...] = r["ysin"]
        ob_ref[...] = r["ob"]
        hs_ref[...] = r["hs"]
        ltot_ref[...] = r["lse_tot"]
        dm = _dot_nt(dx_ref[...].astype(bf16), wout_ref[...])
        ys3 = (r["y_s"], r["y_a"], r["y_c"])
        for k in range(3):
            gk = r["gates"][k]
            dgr = dm * ys3[k] * gk * (1.0 - gk)
            dgate_ref[:, k * D:(k + 1) * D] = dgr.astype(bf16)
            dbg_ref[:, k * D:(k + 1) * D] += jnp.sum(dgr, axis=0, keepdims=True)
        dy_s = dm * r["gates"][0]
        sg2 = r["sg2"]
        dz = jnp.concatenate([dy_s * sg2, dy_s * r["z1"] * sg2 * (1.0 - sg2)], axis=1).astype(bf16)
        dz_ref[...] = dz
        dys_ref[...] = _dot_nt(dz, wglu_ref[...]) * _gelu_grad(r["ys"])
        dya = (dm * r["gates"][1]).astype(bf16)
        dya_ref[...] = dya
        do = _dot_nt(dya, watt_ref[...])
        do_ref[...] = do.astype(bf16)
        delta_ref[...] = _dot_hi(do * r["o"], _head_reduce())
        dyc = (dm * r["gates"][2]).astype(bf16)
        dyc_ref[...] = dyc
        sgn, hn = r["sgn"], r["hn"]
        dhn = _dot_nt(dyc, wpw_ref[...]) * sgn * (1.0 + hn * (1.0 - sgn))
        dlng_ref[...] += jnp.sum(dhn * r["xh"], axis=0, keepdims=True)
        dlnb_ref[...] += jnp.sum(dhn, axis=0, keepdims=True)
        dxh = dhn * lng_ref[...]
        xh = r["xh"]
        dhc_ref[...] = r["rstd"] * (dxh - jnp.mean(dxh, axis=-1, keepdims=True)
                                    - xh * jnp.mean(dxh * xh, axis=-1, keepdims=True))

    tok = lambda i: (i, 0)
    fix = lambda i: (0, 0)
    outs = [("dys", BW, f32), ("do", BW, bf16), ("delta", 128, f32), ("lse_tot", 128, f32), ("dhc", BW, f32),
            ("dgate", 3 * D, bf16), ("ysin", BW, bf16), ("dz", 2 * D, bf16), ("ob", BW, bf16), ("dya", D, bf16),
            ("hs", BW, bf16), ("dyc", D, bf16), ("merged", D, bf16)]
    small = [("dbg", 3 * D), ("dlng", BW), ("dlnb", BW)]
    res = pl.pallas_call(
        body, name="merge_bwd", grid=(n // TBM,), in_specs=_merge_in_specs(tok, TBM),
        out_specs=[pl.BlockSpec((TBM, w), tok) for _, w, _ in outs] + [pl.BlockSpec((1, w), fix) for _, w in small],
        out_shape=[jax.ShapeDtypeStruct((n, w), dt) for _, w, dt in outs]
        + [jax.ShapeDtypeStruct((1, w), f32) for _, w in small],
        compiler_params=_cp(1, 56))(dx1, ys, *os_, *lses, hc, proj, proj, proj, bg, lng, lnb, wglu, watt, wpw, wout)
    return dict(zip([k for k, _, _ in outs] + [k for k, _ in small], res))


def assemble_dproj(du, dqs, dks, dvs, dcv, dgate):
    n = du.shape[0]

    def body(du_ref, q1, q2, q3, k1, k2, k3, v1, v2, v3, cv_ref, g_ref, o_ref):
        o_ref[:, 0:BW] = du_ref[...]
        for j, qr in enumerate((q1, q2, q3)):
            o_ref[:, (1 + j) * BW:(2 + j) * BW] = qr[...]
        o_ref[:, 4 * BW:5 * BW] = (k1[...] + k2[...] + k3[...]).astype(bf16)
        o_ref[:, 5 * BW:6 * BW] = (v1[...] + v2[...] + v3[...]).astype(bf16)
        o_ref[:, 6 * BW:8 * BW] = cv_ref[...]
        o_ref[:, 8 * BW:] = g_ref[...]

    tok = lambda i: (i, 0)
    t = lambda w: pl.BlockSpec((TB, w), tok)
    return pl.pallas_call(
        body, name="assemble_dproj", grid=(n // TB,),
        in_specs=[t(BW)] * 10 + [t(2 * BW), t(3 * D)], out_specs=t(INC),
        out_shape=jax.ShapeDtypeStruct((n, INC), bf16), compiler_params=_cp(1))(du, *dqs, *dks, *dvs, dcv, dgate)


def _me():
    return lax.axis_index("x"), lax.axis_index("y"), lax.axis_index("c")


def _peers():
    x, y, c = _me()
    return [(x, y, 1 - c), (1 - x, y, c), (1 - x, y, 1 - c), (x, 1 - y, c), (x, 1 - y, 1 - c),
            (1 - x, 1 - y, c), (1 - x, 1 - y, 1 - c)]


def _rank(p):
    return 4 * p[0] + 2 * p[1] + p[2]


def allgather(arrs, name):
    na = len(arrs)
    units = [(a, j) for a in range(na) for j in range(arrs[a].shape[0])]
    nu = len(units)

    def body(*refs):
        ins, outs = refs[:na], refs[na:2 * na]
        send, recv, loc = refs[2 * na:]
        me = _rank(_me())
        local, remote = [], []
        for u, (a, j) in enumerate(units):
            own = pltpu.make_async_copy(ins[a].at[j], outs[a].at[j, me], loc.at[u])
            own.start()
            local.append(own)
        for u, (a, j) in enumerate(units):
            for k, p in enumerate(_peers()):
                cp = pltpu.make_async_remote_copy(src_ref=ins[a].at[j], dst_ref=outs[a].at[j, me],
                                                  send_sem=send.at[u, k], recv_sem=recv.at[u, k],
                                                  device_id=p, device_id_type=MESH)
                cp.start()
                remote.append(cp)
        for cp in local:
            cp.wait()
        for cp in remote:
            cp.wait()

    return pl.pallas_call(
        body, name=name, in_specs=[ANY] * na, out_specs=[ANY] * na,
        out_shape=[jax.ShapeDtypeStruct((a.shape[0], NDEV) + a.shape[1:], a.dtype) for a in arrs],
        scratch_shapes=[pltpu.SemaphoreType.DMA((nu, NDEV - 1)), pltpu.SemaphoreType.DMA((nu, NDEV - 1)),
                        pltpu.SemaphoreType.DMA((nu,))])(*arrs)


def exchange_grads(parts):
    nw = len(parts)
    flat = [p for pair in parts for p in pair]
    na = len(flat)

    def body(*refs):
        ins, outs = refs[:na], refs[na:na + nw]
        send, recv, loc = refs[na + nw:]
        me = _rank(_me())
        local, remote = [], []
        for a in range(na):
            w, layer = divmod(a, DEPTH)
            own = pltpu.make_async_copy(ins[a].at[me], outs[w].at[layer, me], loc.at[a])
            own.start()
            local.append(own)
        for a in range(na):
            w, layer = divmod(a, DEPTH)
            for k, p in enumerate(_peers()):
                cp = pltpu.make_async_remote_copy(src_ref=ins[a].at[_rank(p)], dst_ref=outs[w].at[layer, me],
                                                  send_sem=send.at[a, k], recv_sem=recv.at[a, k],
                                                  device_id=p, device_id_type=MESH)
                cp.start()
                remote.append(cp)
        for cp in local:
            cp.wait()
        for cp in remote:
            cp.wait()

    return pl.pallas_call(
        body, name="exchange_grads", in_specs=[ANY] * na, out_specs=[ANY] * nw,
        out_shape=[jax.ShapeDtypeStruct((DEPTH,) + pair[0].shape, pair[0].dtype) for pair in parts],
        scratch_shapes=[pltpu.SemaphoreType.DMA((na, NDEV - 1)), pltpu.SemaphoreType.DMA((na, NDEV - 1)),
                        pltpu.SemaphoreType.DMA((na,))])(*flat)


_C1 = 1.0 / (1.0 - ADAM_B1 ** ADAM_STEP)
_C2 = 1.0 / (1.0 - ADAM_B2 ** ADAM_STEP)


def _adamw(w, g, m, v):
    m = ADAM_B1 * m + (1.0 - ADAM_B1) * g
    v = ADAM_B2 * v + (1.0 - ADAM_B2) * (g * g)
    delta = -ADAM_LR * ((m * _C1) / (jnp.sqrt(v * _C2) + ADAM_EPS) + ADAM_WD * w)
    return delta, m, v


def adam_big(recv, w, m, v, name):
    _, _, k, n = recv.shape
    tk = k
    while tk * n * 2 * NDEV > 4 * 1024 * 1024 and tk % 16 == 0:
        tk //= 2

    def body(r_ref, w_ref, m_ref, v_ref, g_ref, d_ref, nm_ref, nv_ref):
        g = r_ref[0, 0].astype(f32)
        for s in range(1, NDEV):
            g = g + r_ref[0, s].astype(f32)
        d, nm, nv = _adamw(w_ref[0], g, m_ref[0], v_ref[0])
        g_ref[0], d_ref[0], nm_ref[0], nv_ref[0] = g, d, nm, nv

    blk = pl.BlockSpec((1, tk, n), lambda l, i: (l, i, 0))
    return pl.pallas_call(
        body, name=name, grid=(DEPTH, k // tk),
        in_specs=[pl.BlockSpec((1, NDEV, tk, n), lambda l, i: (l, 0, i, 0)), blk, blk, blk],
        out_specs=[blk] * 4, out_shape=[jax.ShapeDtypeStruct(w.shape, f32)] * 4,
        compiler_params=_cp(2))(recv, w, m, v)


def adam_small(gath, w, m, v):
    r = w.shape[0]
    tr = 512

    def body(g_ref, w_ref, m_ref, v_ref, go_ref, d_ref, nm_ref, nv_ref):
        g = g_ref[0]
        for s in range(1, NDEV):
            g = g + g_ref[s]
        d, nm, nv = _adamw(w_ref[...], g, m_ref[...], v_ref[...])
        go_ref[...], d_ref[...], nm_ref[...], nv_ref[...] = g, d, nm, nv

    blk = pl.BlockSpec((tr, 128), lambda i: (i, 0))
    return pl.pallas_call(
        body, name="adam_small", grid=(r // tr,),
        in_specs=[pl.BlockSpec((NDEV, tr, 128), lambda i: (0, i, 0)), blk, blk, blk],
        out_specs=[blk] * 4, out_shape=[jax.ShapeDtypeStruct((r, 128), f32)] * 4,
        compiler_params=_cp(1))(gath, w, m, v)


SMALL = ["norm1_g", "b_gate", "ssm_lambda_re", "ssm_lambda_im", "ssm_log_dt", "ssm_b_re", "ssm_b_im",
         "ssm_c_re", "ssm_c_im", "ssm_d", "conv_w", "conv_b", "conv_ln_g", "conv_ln_b", "norm2_g", "final_g"]
BIG = ["w_in", "w_ssm_glu", "w_att_up", "w_conv_pw2", "w_out", "w_ffn_in", "w_ffn_out"]
ORDER = ["norm1_g", "w_in", "b_gate", "ssm_lambda_re", "ssm_lambda_im", "ssm_log_dt", "ssm_b_re", "ssm_b_im",
         "ssm_c_re", "ssm_c_im", "ssm_d", "w_ssm_glu", "w_att_up", "conv_w", "conv_b", "conv_ln_g", "conv_ln_b",
         "w_conv_pw2", "w_out", "norm2_g", "w_ffn_in", "w_ffn_out", "final_g"]
PACK_ROWS = 2560


def _pack(arrs):
    flat = jnp.concatenate([a.reshape(-1).astype(f32) for a in arrs])
    return jnp.pad(flat, (0, PACK_ROWS * 128 - flat.shape[0])).reshape(PACK_ROWS, 128)


def _unpack(pack, shapes):
    flat = pack.reshape(-1)
    out, off = [], 0
    for s in shapes:
        sz = math.prod(s)
        out.append(flat[off:off + sz].reshape(s))
        off += sz
    return out


def _bt(b):
    return b.transpose(2, 0, 1).reshape(GH, NSTATE)


def _bt_inv(bt):
    return bt.reshape(GH, NG, NS).transpose(1, 2, 0)


def _ct(c):
    return c.transpose(1, 0, 2).reshape(GH, NSTATE)


def _ct_inv(ct):
    return ct.reshape(GH, NG, NS).transpose(1, 0, 2)


def local_step(x, loss_target, P, G):
    bsz, seq, _ = x.shape
    n = bsz * seq

    def natural(g4, layer):
        k_, n_ = g4.shape[2], g4.shape[3]
        return g4[layer].transpose(1, 0, 2).reshape(k_, NDEV * n_)

    conv_w_full = G["conv_w"].transpose(0, 2, 1, 3).reshape(DEPTH, CW, BW)
    conv_w_pad = jnp.pad(conv_w_full, ((0, 0), (0, 1), (0, 0)))

    xs = x.reshape(n, D)
    saved = []
    for l in range(DEPTH):
        S = {"x": xs}
        h1 = rms_fwd(xs, P["norm1_g"][l][None])
        proj = inproj(h1, G["w_in"], l)
        proj3 = proj.reshape(bsz, seq, INC)
        lr = P["ssm_lambda_re"][l].reshape(1, NSTATE)
        li = P["ssm_lambda_im"][l].reshape(1, NSTATE)
        ld = jnp.repeat(P["ssm_log_dt"][l], NS).reshape(1, NSTATE)
        btr, bti = _bt(P["ssm_b_re"][l]), _bt(P["ssm_b_im"][l])
        apow, pf, pr, bb, cb = s5_params(lr, li, ld, btr, bti, _ct(P["ssm_c_re"][l]), _ct(P["ssm_c_im"][l]))
        dskip = P["ssm_d"][l][None]
        ys, cins = s5_fwd(proj3, apow, pf, bb, cb, dskip)
        att = [att_fwd(proj, bsz, gi, dil) for gi, (_, dil) in enumerate(PATTERNS)]
        hc = conv_fwd(proj3, conv_w_pad[l], P["conv_b"][l][None])
        wts = dict(wglu=natural(G["w_ssm_glu"], l), watt=natural(G["w_att_up"], l),
                   wpw=natural(G["w_conv_pw2"], l), wout=G["w_out"][l].reshape(D, D))
        mi = dict(ys=ys.reshape(n, BW), os_=[a[0] for a in att], lses=[a[1] for a in att], hc=hc.reshape(n, BW),
                  proj=proj, bg=P["b_gate"][l][None], lng=P["conv_ln_g"][l][None], lnb=P["conv_ln_b"][l][None],
                  **wts)
        x1 = merge_fwd(xs, **mi)
        x2 = ffn_fwd(x1, P["norm2_g"][l][None], G["w_ffn_in"], G["w_ffn_out"], l)
        S.update(h1=h1, proj=proj, proj3=proj3, tabs=(apow, pf, pr, bb, cb), cins=cins, mi=mi, x1=x1,
                 sp=(lr, li, ld, btr, bti), dskip=dskip)
        saved.append(S)
        xs = x2

    loss8, dx, dfinal = loss_head(xs, P["final_g"][None], loss_target.reshape(n, D))

    small_g = {k: [None] * DEPTH for k in SMALL if k != "final_g"}
    big_g = {k: [None] * DEPTH for k in BIG}
    tokblk = lambda w: pl.BlockSpec((1024, w), lambda s, i: (i, 0))
    colblk = lambda w: pl.BlockSpec((1024, w), lambda s, i: (i, s))
    sh3blk = lambda w: pl.BlockSpec((1, 1024, w), lambda s, i: (s, i, 0))
    for l in reversed(range(DEPTH)):
        S = saved[l]
        g2 = P["norm2_g"][l][None]
        dx1, h2, dz1, dz2, a4, dg2 = ffn_bwd(S["x1"], g2, G["w_ffn_in"], G["w_ffn_out"], l, dx)
        small_g["norm2_g"][l] = dg2
        dwa = mm_tn(h2, dz1, tokblk(D), sh3blk(NSH_FF), 4, D, NSH_FF, n, "dw_ffn_in_a")
        dwb = mm_tn(h2, dz2, tokblk(D), sh3blk(NSH_FF), 4, D, NSH_FF, n, "dw_ffn_in_b")
        big_g["w_ffn_in"][l] = jnp.concatenate([dwa, dwb], axis=0)
        big_g["w_ffn_out"][l] = mm_tn(a4, dx, sh3blk(NSH_FF), tokblk(D), 4, NSH_FF, D, n,
                                      "dw_ffn_out").reshape(NDEV, NSH_FF // 2, D)
        mb = merge_bwd(dx1, **S["mi"])
        small_g["b_gate"][l], small_g["conv_ln_g"][l], small_g["conv_ln_b"][l] = mb["dbg"], mb["dlng"], mb["dlnb"]
        big_g["w_ssm_glu"][l] = mm_tn(mb["ysin"], mb["dz"], tokblk(BW), colblk(256), NDEV, BW, 256, n, "dw_glu")
        big_g["w_att_up"][l] = mm_tn(mb["ob"], mb["dya"], tokblk(BW), colblk(128), NDEV, BW, 128, n, "dw_att")
        big_g["w_conv_pw2"][l] = mm_tn(mb["hs"], mb["dyc"], tokblk(BW), colblk(128), NDEV, BW, 128, n, "dw_pw2")
        big_g["w_out"][l] = mm_tn(mb["merged"], dx1, tokblk(D), tokblk(D), 1, D, D, n,
                                  "dw_out").reshape(NDEV, D // NDEV, D)
        dcv, dcw, dcb = conv_bwd(S["proj3"], mb["dhc"].reshape(bsz, seq, BW), conv_w_pad[l])
        small_g["conv_w"][l] = dcw[:CW].reshape(CW, NDEV, BW // NDEV).transpose(1, 0, 2)
        small_g["conv_b"][l] = dcb
        ab = [att_bwd(S["proj"], mb["do"], mb["lse_tot"], mb["delta"], bsz, gi, dil)
              for gi, (_, dil) in enumerate(PATTERNS)]
        apow, pf, pr, bb, cb = S["tabs"]
        du, d_a, d_bb, d_cb, d_d = s5_bwd(S["proj3"], mb["dys"].reshape(bsz, seq, BW), S["cins"], apow, pf, pr,
                                           bb, cb, S["dskip"])
        lr, li, ld, btr, bti = S["sp"]
        dlr, dli, dld, dbt, dct = s5_params_bwd(lr, li, ld, btr, bti, d_a, d_bb, d_cb)
        small_g["ssm_lambda_re"][l], small_g["ssm_lambda_im"][l] = dlr.reshape(NG, NS), dli.reshape(NG, NS)
        small_g["ssm_log_dt"][l] = dld[0, :NG]
        small_g["ssm_b_re"][l], small_g["ssm_b_im"][l] = _bt_inv(dbt[0]), _bt_inv(dbt[1])
        small_g["ssm_c_re"][l], small_g["ssm_c_im"][l] = _ct_inv(dct[0]), _ct_inv(dct[1])
        small_g["ssm_d"][l] = d_d
        dproj = assemble_dproj(du.reshape(n, BW), [a[0] for a in ab], [a[1] for a in ab], [a[2] for a in ab],
                               dcv.reshape(n, 2 * BW), mb["dgate"])
        big_g["w_in"][l] = mm_tn(S["h1"], dproj, tokblk(D), colblk(NSH_IN), NDEV, D, NSH_IN, n, "dw_in")
        dx, dg1 = inproj_bwd(dproj, G["w_in"], l, S["x"], P["norm1_g"][l][None], dx1)
        small_g["norm1_g"][l] = dg1
    return loss8, dx, dfinal, small_g, big_g


def kernel(x, norm1_g, w_in, b_gate, ssm_lambda_re, ssm_lambda_im, ssm_log_dt, ssm_b_re, ssm_b_im, ssm_c_re, ssm_c_im, ssm_d, w_ssm_glu, w_att_up, conv_w, conv_b, conv_ln_g, conv_ln_b, w_conv_pw2, w_out, norm2_g, w_ffn_in, w_ffn_out, final_g, loss_target, m_norm1_g, m_w_in, m_b_gate, m_ssm_lambda_re, m_ssm_lambda_im, m_ssm_log_dt, m_ssm_b_re, m_ssm_b_im, m_ssm_c_re, m_ssm_c_im, m_ssm_d, m_w_ssm_glu, m_w_att_up, m_conv_w, m_conv_b, m_conv_ln_g, m_conv_ln_b, m_w_conv_pw2, m_w_out, m_norm2_g, m_w_ffn_in, m_w_ffn_out, m_final_g, v_norm1_g, v_w_in, v_b_gate, v_ssm_lambda_re, v_ssm_lambda_im, v_ssm_log_dt, v_ssm_b_re, v_ssm_b_im, v_ssm_c_re, v_ssm_c_im, v_ssm_d, v_w_ssm_glu, v_w_att_up, v_conv_w, v_conv_b, v_conv_ln_g, v_conv_ln_b, v_w_conv_pw2, v_w_out, v_norm2_g, v_w_ffn_in, v_w_ffn_out, v_final_g):
    args = dict(locals())
    W = {k: args[k] for k in ORDER}
    M = {k: args["m_" + k] for k in ORDER}
    V = {k: args["v_" + k] for k in ORDER}
    bsz, seq, _ = x.shape
    n = bsz * seq
    me = 4 * lax.axis_index("x") + 2 * lax.axis_index("y") + lax.axis_index("c")

    gath = allgather([W[k].astype(bf16) for k in BIG] + [conv_w], "allgather_weights")
    G = dict(zip(BIG + ["conv_w"], gath))

    loss8, dx, dfinal, small_g, big_g = local_step(x, loss_target, W, G)

    recv = exchange_grads([big_g[k] for k in BIG])
    out = {}
    for k, r in zip(BIG, recv):
        shp = W[k].shape
        r4 = r.reshape(DEPTH, NDEV, shp[1], shp[2])
        out[k] = adam_big(r4, W[k], M[k], V[k], "adam_" + k)

    names = [k for k in SMALL if k != "final_g"]
    shapes = [(DEPTH, NDEV, CW, BW // NDEV) if k == "conv_w" else W[k].shape for k in names] + [(D,), (1,)]
    gpack = _pack([jnp.stack([g.reshape(shapes[i][1:]) for g in small_g[k]]) for i, k in enumerate(names)]
                  + [dfinal, loss8[0, :1]])

    def wpack(src):
        parts = [jnp.broadcast_to(src[k][:, None], shapes[i]) if k == "conv_w" else src[k] for i, k in enumerate(names)]
        return _pack(parts + [src["final_g"], jnp.ones((1,), f32)])

    (gall,) = allgather([gpack[None]], "allgather_small")
    sg, sd, sm, sv = [_unpack(p, shapes) for p in adam_small(gall[0], wpack(W), wpack(M), wpack(V))]
    for i, k in enumerate(names + ["final_g"]):
        vals = [t[i] for t in (sg, sd, sm, sv)]
        if k == "conv_w":
            vals = [lax.dynamic_index_in_dim(t, me, axis=1, keepdims=False) for t in vals]
        out[k] = vals
    loss = sg[-1].reshape(())

    res = [loss, dx.reshape(bsz, seq, D)]
    for j in range(4):
        res += [out[k][j] for k in ORDER]
    return tuple(res)
```

```python
import functools
import math

import jax
import jax.numpy as jnp
from jax import lax
from jax.experimental import pallas as pl
from jax.experimental.pallas import tpu as pltpu

f32 = jnp.float32
bf16 = jnp.bfloat16

D = 1024
DEPTH = 2
EPS = 1e-6
BW = 512
NG = 32
GH = 16
NS = 64
NSTATE = NG * NS
HD = 64
NH = 8
PATTERNS = ((128, 1), (512, 4), (2048, 16))
ABLK = 128
ATT_SCALE = HD ** -0.5
CW = 31
DFF = 2816
INC = 7168
NDEV = 8
NSH_IN = INC // NDEV
NSH_FF = 2 * DFF // NDEV
ADAM_LR, ADAM_B1, ADAM_B2, ADAM_EPS, ADAM_WD, ADAM_STEP = 0.001, 0.9, 0.999, 1e-08, 0.01, 10

TB = 512
TS = 256
SJ = 4
SW = NSTATE // SJ
SU = BW // SJ
NEG = -1e30
MESH = pl.DeviceIdType.MESH
ANY = pl.BlockSpec(memory_space=pl.ANY)


def _cp(n_axes, vmem_mb=48):
    return pltpu.CompilerParams(dimension_semantics=("arbitrary",) * n_axes,
                                vmem_limit_bytes=vmem_mb * 1024 * 1024)


def _dot(a, b):
    return jnp.dot(a, b, preferred_element_type=f32)


def _dot_nt(a, b):
    return lax.dot_general(a, b, (((1,), (1,)), ((), ())), preferred_element_type=f32)


def _dot_tn(a, b):
    return lax.dot_general(a, b, (((0,), (0,)), ((), ())), preferred_element_type=f32)


def _dot_hi(a, b):
    return jnp.dot(a, b, precision=lax.Precision.HIGHEST, preferred_element_type=f32)


def _sigmoid(x):
    return 1.0 / (1.0 + jnp.exp(-x))


_GC = math.sqrt(2.0 / math.pi)


def _gelu(x):
    return 0.5 * x * (1.0 + jnp.tanh(_GC * (x + 0.044715 * x * x * x)))


def _gelu_grad(x):
    t = jnp.tanh(_GC * (x + 0.044715 * x * x * x))
    return 0.5 * (1.0 + t) + 0.5 * x * (1.0 - t * t) * _GC * (1.0 + 3.0 * 0.044715 * x * x)


def _rms_stats(x):
    return lax.rsqrt(jnp.mean(x * x, axis=-1, keepdims=True) + EPS)


def _rms_bwd(x, g, dh):
    r = _rms_stats(x)
    dyg = dh * g
    dx = r * dyg - x * (r * r * r) * jnp.mean(dyg * x, axis=-1, keepdims=True)
    dg = jnp.sum(dh * x * r, axis=0, keepdims=True)
    return dx, dg


def rms_fwd(x, g):
    n = x.shape[0]

    def body(x_ref, g_ref, h_ref):
        xv = x_ref[...]
        h_ref[...] = (xv * _rms_stats(xv) * g_ref[...]).astype(bf16)

    return pl.pallas_call(
        body, name="rms_fwd", grid=(n // TB,),
        in_specs=[pl.BlockSpec((TB, D), lambda i: (i, 0)), pl.BlockSpec((1, D), lambda i: (0, 0))],
        out_specs=pl.BlockSpec((TB, D), lambda i: (i, 0)),
        out_shape=jax.ShapeDtypeStruct((n, D), bf16), compiler_params=_cp(1))(x, g)


def inproj(h, w4, layer):
    n = h.shape[0]
    tm = 1024

    def body(h_ref, w_ref, o_ref):
        o_ref[...] = _dot(h_ref[...], w_ref[0, 0]).astype(bf16)

    return pl.pallas_call(
        body, name="inproj", grid=(NDEV, n // tm),
        in_specs=[pl.BlockSpec((tm, D), lambda s, i: (i, 0)),
                  pl.BlockSpec((1, 1, D, NSH_IN), lambda s, i: (layer, s, 0, 0))],
        out_specs=pl.BlockSpec((tm, NSH_IN), lambda s, i: (i, s)),
        out_shape=jax.ShapeDtypeStruct((n, INC), bf16), compiler_params=_cp(2))(h, w4)


def inproj_bwd(dproj, w4, layer, x, g, dres):
    n = x.shape[0]
    tm = 1024

    def body(dp_ref, w_ref, x_ref, g_ref, dr_ref, dx_ref, dg_ref, acc):
        i, s = pl.program_id(0), pl.program_id(1)

        @pl.when(s == 0)
        def _():
            acc[...] = jnp.zeros_like(acc)

        @pl.when((s == 0) & (i == 0))
        def _():
            dg_ref[...] = jnp.zeros_like(dg_ref)

        acc[...] += _dot_nt(dp_ref[...], w_ref[0, 0])

        @pl.when(s == NDEV - 1)
        def _():
            dx, dg = _rms_bwd(x_ref[...], g_ref[...], acc[...])
            dx_ref[...] = dr_ref[...] + dx
            dg_ref[...] += dg

    return pl.pallas_call(
        body, name="inproj_bwd", grid=(n // tm, NDEV),
        in_specs=[pl.BlockSpec((tm, NSH_IN), lambda i, s: (i, s)),
                  pl.BlockSpec((1, 1, D, NSH_IN), lambda i, s: (layer, s, 0, 0)),
                  pl.BlockSpec((tm, D), lambda i, s: (i, 0)),
                  pl.BlockSpec((1, D), lambda i, s: (0, 0)),
                  pl.BlockSpec((tm, D), lambda i, s: (i, 0))],
        out_specs=[pl.BlockSpec((tm, D), lambda i, s: (i, 0)), pl.BlockSpec((1, D), lambda i, s: (0, 0))],
        out_shape=[jax.ShapeDtypeStruct((n, D), f32), jax.ShapeDtypeStruct((1, D), f32)],
        scratch_shapes=[pltpu.VMEM((tm, D), f32)], compiler_params=_cp(2))(dproj, w4, x, g, dres)


def mm_tn(a, b, a_spec, b_spec, n_sh, ka, nb, m, name):
    tm = 1024

    def body(a_ref, b_ref, o_ref, acc):
        i = pl.program_id(1)

        @pl.when(i == 0)
        def _():
            acc[...] = jnp.zeros_like(acc)

        av = a_ref[...].reshape(tm, ka).astype(bf16)
        bv = b_ref[...].reshape(tm, nb).astype(bf16)
        acc[...] += _dot_tn(av, bv)

        @pl.when(i == m // tm - 1)
        def _():
            o_ref[0] = acc[...].astype(bf16)

    return pl.pallas_call(
        body, name=name, grid=(n_sh, m // tm), in_specs=[a_spec, b_spec],
        out_specs=pl.BlockSpec((1, ka, nb), lambda s, i: (s, 0, 0)),
        out_shape=jax.ShapeDtypeStruct((n_sh, ka, nb), bf16),
        scratch_shapes=[pltpu.VMEM((ka, nb), f32)], compiler_params=_cp(2))(a, b)


def ffn_fwd(x1, g2, w1, w2, layer):
    n = x1.shape[0]
    w2p = w2.reshape(DEPTH, 4, NSH_FF, D)

    def body(x_ref, g_ref, wa_ref, wb_ref, w2_ref, o_ref, h_sc):
        s = pl.program_id(1)

        @pl.when(s == 0)
        def _():
            xv = x_ref[...]
            h_sc[...] = (xv * _rms_stats(xv) * g_ref[...]).astype(bf16)
            o_ref[...] = xv

        h = h_sc[...]
        z1 = _dot(h, wa_ref[0, 0])
        z2 = _dot(h, wb_ref[0, 0])
        a = (z1 * _sigmoid(z1) * z2).astype(bf16)
        o_ref[...] += _dot(a, w2_ref[0, 0])

    return pl.pallas_call(
        body, name="ffn_fwd", grid=(n // TB, 4),
        in_specs=[pl.BlockSpec((TB, D), lambda i, s: (i, 0)),
                  pl.BlockSpec((1, D), lambda i, s: (0, 0)),
                  pl.BlockSpec((1, 1, D, NSH_FF), lambda i, s: (layer, s, 0, 0)),
                  pl.BlockSpec((1, 1, D, NSH_FF), lambda i, s: (layer, s + 4, 0, 0)),
                  pl.BlockSpec((1, 1, NSH_FF, D), lambda i, s: (layer, s, 0, 0))],
        out_specs=pl.BlockSpec((TB, D), lambda i, s: (i, 0)),
        out_shape=jax.ShapeDtypeStruct((n, D), f32),
        scratch_shapes=[pltpu.VMEM((TB, D), bf16)], compiler_params=_cp(2))(x1, g2, w1, w1, w2p)


def ffn_bwd(x1, g2, w1, w2, layer, dx2):
    n = x1.shape[0]
    w2p = w2.reshape(DEPTH, 4, NSH_FF, D)

    def body(x_ref, g_ref, dy_ref, wa_ref, wb_ref, w2_ref,
             dx_ref, h_ref, dz1_ref, dz2_ref, a_ref, dg_ref, dh_sc, dyb_sc):
        i, s = pl.program_id(0), pl.program_id(1)

        @pl.when(s == 0)
        def _():
            xv = x_ref[...]
            h_ref[...] = (xv * _rms_stats(xv) * g_ref[...]).astype(bf16)
            dh_sc[...] = jnp.zeros_like(dh_sc)
            dyb_sc[...] = dy_ref[...].astype(bf16)

        @pl.when((s == 0) & (i == 0))
        def _():
            dg_ref[...] = jnp.zeros_like(dg_ref)

        h = h_ref[...]
        z1 = _dot(h, wa_ref[0, 0])
        z2 = _dot(h, wb_ref[0, 0])
        sg = _sigmoid(z1)
        sl = z1 * sg
        a_ref[0] = (sl * z2).astype(bf16)
        da = _dot_nt(dyb_sc[...], w2_ref[0, 0])
        dz2 = (da * sl).astype(bf16)
        dz1 = (da * z2 * sg * (1.0 + z1 * (1.0 - sg))).astype(bf16)
        dz1_ref[0] = dz1
        dz2_ref[0] = dz2
        dh_sc[...] += _dot_nt(dz1, wa_ref[0, 0]) + _dot_nt(dz2, wb_ref[0, 0])

        @pl.when(s == 3)
        def _():
            dx, dg = _rms_bwd(x_ref[...], g_ref[...], dh_sc[...])
            dx_ref[...] = dy_ref[...] + dx
            dg_ref[...] += dg

    tok = lambda i, s: (i, 0)
    sh3 = lambda i, s: (s, i, 0)
    return pl.pallas_call(
        body, name="ffn_bwd", grid=(n // TB, 4),
        in_specs=[pl.BlockSpec((TB, D), tok), pl.BlockSpec((1, D), lambda i, s: (0, 0)), pl.BlockSpec((TB, D), tok),
                  pl.BlockSpec((1, 1, D, NSH_FF), lambda i, s: (layer, s, 0, 0)),
                  pl.BlockSpec((1, 1, D, NSH_FF), lambda i, s: (layer, s + 4, 0, 0)),
                  pl.BlockSpec((1, 1, NSH_FF, D), lambda i, s: (layer, s, 0, 0))],
        out_specs=[pl.BlockSpec((TB, D), tok), pl.BlockSpec((TB, D), tok),
                   pl.BlockSpec((1, TB, NSH_FF), sh3), pl.BlockSpec((1, TB, NSH_FF), sh3),
                   pl.BlockSpec((1, TB, NSH_FF), sh3), pl.BlockSpec((1, D), lambda i, s: (0, 0))],
        out_shape=[jax.ShapeDtypeStruct((n, D), f32), jax.ShapeDtypeStruct((n, D), bf16),
                   jax.ShapeDtypeStruct((4, n, NSH_FF), bf16), jax.ShapeDtypeStruct((4, n, NSH_FF), bf16),
                   jax.ShapeDtypeStruct((4, n, NSH_FF), bf16), jax.ShapeDtypeStruct((1, D), f32)],
        scratch_shapes=[pltpu.VMEM((TB, D), f32), pltpu.VMEM((TB, D), bf16)],
        compiler_params=_cp(2))(x1, g2, dx2, w1, w1, w2p)


def loss_head(x, g, target):
    n = x.shape[0]

    def body(x_ref, g_ref, t_ref, l_ref, dx_ref, dg_ref):
        i = pl.program_id(0)

        @pl.when(i == 0)
        def _():
            l_ref[...] = jnp.zeros_like(l_ref)
            dg_ref[...] = jnp.zeros_like(dg_ref)

        xv = x_ref[...]
        y = xv * _rms_stats(xv) * g_ref[...]
        e = y - t_ref[...]
        l_ref[...] += 0.5 * jnp.sum(jnp.sum(e * e, axis=-1, keepdims=True), axis=0, keepdims=True) * (1.0 / D)
        dx, dg = _rms_bwd(xv, g_ref[...], e * (1.0 / D))
        dx_ref[...] = dx
        dg_ref[...] += dg

    tok = lambda i: (i, 0)
    return pl.pallas_call(
        body, name="loss_head", grid=(n // TB,),
        in_specs=[pl.BlockSpec((TB, D), tok), pl.BlockSpec((1, D), lambda i: (0, 0)), pl.BlockSpec((TB, D), tok)],
        out_specs=[pl.BlockSpec((8, 128), lambda i: (0, 0)), pl.BlockSpec((TB, D), tok),
                   pl.BlockSpec((1, D), lambda i: (0, 0))],
        out_shape=[jax.ShapeDtypeStruct((8, 128), f32), jax.ShapeDtypeStruct((n, D), f32),
                   jax.ShapeDtypeStruct((1, D), f32)],
        compiler_params=_cp(1))(x, g, target)


def _disc(lr, li, ld):
    dt = jnp.exp(ld)
    mag = jnp.exp(lr * dt)
    ar = mag * jnp.cos(li * dt)
    ai = mag * jnp.sin(li * dt)
    nr, ni = ar - 1.0, ai
    den = lr * lr + li * li
    zr = (nr * lr + ni * li) / den
    zi = (ni * lr - nr * li) / den
    return ar, ai, zr, zi


def _blockdiag_mask(shape):
    r = lax.broadcasted_iota(jnp.int32, shape, 0) // GH
    c = lax.broadcasted_iota(jnp.int32, shape, 1) // NS
    return r == c


def s5_params(lr, li, ld, btr, bti, ctr, cti):
    nlog = TS.bit_length() - 1

    def body(lr_ref, li_ref, ld_ref, btr_ref, bti_ref, ctr_ref, cti_ref, ap_ref, t8_ref, bb_ref, cb_ref):
        ar, ai, zr, zi = _disc(lr_ref[...], li_ref[...], ld_ref[...])
        pr_, pi_ = ar, ai
        ap_ref[...] = jnp.zeros_like(ap_ref)
        pw2 = []
        for k in range(nlog + 1):
            pw2.append((pr_, pi_))
            ap_ref[0, k:k + 1, :] = pr_
            ap_ref[1, k:k + 1, :] = pi_
            pr_, pi_ = pr_ * pr_ - pi_ * pi_, 2.0 * pr_ * pi_
        cm = lambda p, q: (p[0] * q[0] - p[1] * q[1], p[0] * q[1] + p[1] * q[0])
        pw = {1: pw2[0], 2: pw2[1], 4: pw2[2], 8: pw2[3]}
        pw[3], pw[5], pw[6] = cm(pw[2], pw[1]), cm(pw[4], pw[1]), cm(pw[4], pw[2])
        pw[7] = cm(pw[4], pw[3])
        row = lax.broadcasted_iota(jnp.int32, (8, NSTATE), 0)
        zero = jnp.zeros((8, NSTATE), f32)
        for c in range(2):
            for k in range(3):
                full = jnp.broadcast_to(pw2[k][c], (8, NSTATE))
                t8_ref[c, k] = jnp.where(row >= (1 << k), full, 0.0)
                t8_ref[c, 3 + k] = jnp.where(row + (1 << k) < 8, full, 0.0)
            up, down = zero, zero
            for j in range(8):
                up = up + jnp.where(row == j, pw[j + 1][c], 0.0)
                down = down + jnp.where(row == j, pw[8 - j][c], 0.0)
            t8_ref[c, 6] = up
            t8_ref[c, 7] = down
        bbr = zr * btr_ref[...] - zi * bti_ref[...]
        bbi = zr * bti_ref[...] + zi * btr_ref[...]
        mask = _blockdiag_mask((SU, SW))
        for j in range(SJ):
            cols = slice(j * SW, (j + 1) * SW)
            for c, (vb, vc) in enumerate(((bbr, ctr_ref[...]), (bbi, cti_ref[...]))):
                bb_ref[c, j] = jnp.where(mask, jnp.tile(vb[:, cols], (SU // GH, 1)), 0.0).astype(bf16)
                cb_ref[c, j] = jnp.where(mask, jnp.tile(vc[:, cols], (SU // GH, 1)), 0.0).astype(bf16)

    return pl.pallas_call(
        body, name="s5_params",
        out_shape=[jax.ShapeDtypeStruct((2, 16, NSTATE), f32), jax.ShapeDtypeStruct((2, 8, 8, NSTATE), f32),
                   jax.ShapeDtypeStruct((2, SJ, SU, SW), bf16), jax.ShapeDtypeStruct((2, SJ, SU, SW), bf16)],
        compiler_params=pltpu.CompilerParams(vmem_limit_bytes=56 * 1024 * 1024))(lr, li, ld, btr, bti, ctr, cti)


def s5_params_bwd(lr, li, ld, btr, bti, d_a, d_bb, d_cb):
    def body(lr_ref, li_ref, ld_ref, btr_ref, bti_ref, da_ref, dbb_ref, dcb_ref,
             dlr_ref, dli_ref, dld_ref, dbt_ref, dct_ref):
        mask = _blockdiag_mask((SU, SW))

        def fold(ref, c):
            parts = []
            for j in range(SJ):
                v = jnp.where(mask, ref[c, j], 0.0)
                parts.append(v.reshape(SU // GH, GH, SW).sum(axis=0))
            return jnp.concatenate(parts, axis=1)

        dct_ref[0] = fold(dcb_ref, 0)
        dct_ref[1] = fold(dcb_ref, 1)
        dbbr, dbbi = fold(dbb_ref, 0), fold(dbb_ref, 1)
        lrv, liv, ldv = lr_ref[...], li_ref[...], ld_ref[...]
        (ar, ai, zr, zi), vjp = jax.vjp(_disc, lrv, liv, ldv)
        btr, bti = btr_ref[...], bti_ref[...]
        dbt_ref[0] = zr * dbbr + zi * dbbi
        dbt_ref[1] = zr * dbbi - zi * dbbr
        dzr = jnp.sum(dbbr * btr + dbbi * bti, axis=0, keepdims=True)
        dzi = jnp.sum(dbbi * btr - dbbr * bti, axis=0, keepdims=True)
        dlr, dli, dld = vjp((da_ref[0:1, :], da_ref[1:2, :], dzr, dzi))
        dlr_ref[...] = dlr
        dli_ref[...] = dli
        ind = (lax.broadcasted_iota(jnp.int32, (NSTATE, 128), 0) // NS
               == lax.broadcasted_iota(jnp.int32, (NSTATE, 128), 1)).astype(f32)
        dld_ref[...] = _dot_hi(jnp.broadcast_to(dld, (8, NSTATE)), ind)

    return pl.pallas_call(
        body, name="s5_params_bwd",
        out_shape=[jax.ShapeDtypeStruct((1, NSTATE), f32), jax.ShapeDtypeStruct((1, NSTATE), f32),
                   jax.ShapeDtypeStruct((8, 128), f32), jax.ShapeDtypeStruct((2, GH, NSTATE), f32),
                   jax.ShapeDtypeStruct((2, GH, NSTATE), f32)],
        compiler_params=pltpu.CompilerParams(vmem_limit_bytes=56 * 1024 * 1024))(lr, li, ld, btr, bti, d_a, d_bb, d_cb)


def _fma(sr, si, ar, ai, qr, qi):
    return sr + ar * qr - ai * qi, si + ar * qi + ai * qr


def _scan_chunk(br, bi, cr, ci, ap_ref, t8_ref, reverse):
    t, w = br.shape
    nt = t // 8
    sg = -1.0 if reverse else 1.0
    sr, si = br.reshape(nt, 8, w), bi.reshape(nt, 8, w)
    for k in range(3):
        tk = 3 + k if reverse else k
        rot = 8 - (1 << k) if reverse else 1 << k
        sr, si = _fma(sr, si, t8_ref[0, tk][None], sg * t8_ref[1, tk][None],
                      pltpu.roll(sr, rot, 1), pltpu.roll(si, rot, 1))
    edge = 0 if reverse else 7
    er, ei = sr[:, edge, :], si[:, edge, :]
    row = lax.broadcasted_iota(jnp.int32, (nt, w), 0)
    first = row == (nt - 1 if reverse else 0)
    a8r, a8i = ap_ref[0, 3:4, :], sg * ap_ref[1, 3:4, :]
    er = er + jnp.where(first, a8r * cr - a8i * ci, 0.0)
    ei = ei + jnp.where(first, a8r * ci + a8i * cr, 0.0)
    for j in range(nt.bit_length() - 1):
        sh = 1 << j
        keep = (row < nt - sh) if reverse else (row >= sh)
        rot = nt - sh if reverse else sh
        er, ei = _fma(er, ei, ap_ref[0, 3 + j:4 + j, :], sg * ap_ref[1, 3 + j:4 + j, :],
                      jnp.where(keep, pltpu.roll(er, rot, 0), 0.0), jnp.where(keep, pltpu.roll(ei, rot, 0), 0.0))
    rot1 = nt - 1 if reverse else 1
    inr = jnp.where(first, cr, pltpu.roll(er, rot1, 0))
    ini = jnp.where(first, ci, pltpu.roll(ei, rot1, 0))
    tp = 7 if reverse else 6
    sr, si = _fma(sr, si, t8_ref[0, tp][None], sg * t8_ref[1, tp][None], inr[:, None, :], ini[:, None, :])
    last = 0 if reverse else nt - 1
    return sr.reshape(t, w), si.reshape(t, w), er[last:last + 1, :], ei[last:last + 1, :]


def s5_fwd(proj3, apow, t8, bb, cb, dskip):
    b, l, _ = proj3.shape
    nch = l // TS

    def body(u_ref, ap_ref, t8_ref, bb_ref, cb_ref, d_ref, y_ref, cin_ref, carry):
        i = pl.program_id(2)

        @pl.when(i == 0)
        def _():
            carry[...] = jnp.zeros_like(carry)

        cin = carry[...]
        cin_ref[0, 0] = cin
        u = u_ref[0]
        sr, si, cr, ci = _scan_chunk(_dot(u, bb_ref[0, 0]), _dot(u, bb_ref[1, 0]), cin[0:1, :], cin[1:2, :],
                                     ap_ref, t8_ref, False)
        carry[0:1, :] = cr
        carry[1:2, :] = ci
        y = _dot_nt(sr.astype(bf16), cb_ref[0, 0]) - _dot_nt(si.astype(bf16), cb_ref[1, 0])
        y_ref[0] = y + d_ref[...] * u.astype(f32)

    return pl.pallas_call(
        body, name="s5_fwd", grid=(SJ, b, nch),
        in_specs=[pl.BlockSpec((1, TS, SU), lambda j, bi, i: (bi, i, j)),
                  pl.BlockSpec((2, 16, SW), lambda j, bi, i: (0, 0, j)),
                  pl.BlockSpec((2, 8, 8, SW), lambda j, bi, i: (0, 0, 0, j)),
                  pl.BlockSpec((2, 1, SU, SW), lambda j, bi, i: (0, j, 0, 0)),
                  pl.BlockSpec((2, 1, SU, SW), lambda j, bi, i: (0, j, 0, 0)),
                  pl.BlockSpec((1, SU), lambda j, bi, i: (0, j))],
        out_specs=[pl.BlockSpec((1, TS, SU), lambda j, bi, i: (bi, i, j)),
                   pl.BlockSpec((1, 1, 2, SW), lambda j, bi, i: (bi, i, 0, j))],
        out_shape=[jax.ShapeDtypeStruct((b, l, BW), f32), jax.ShapeDtypeStruct((b, nch, 2, NSTATE), f32)],
        scratch_shapes=[pltpu.VMEM((2, SW), f32)], compiler_params=_cp(3))(proj3, apow, t8, bb, cb, dskip)


def s5_bwd(proj3, dy, cins, apow, t8, bb, cb, dskip):
    b, l, _ = proj3.shape
    nch = l // TS

    def body(u_ref, dy_ref, cin_ref, ap_ref, t8_ref, bb_ref, cb_ref, d_ref,
             du_ref, da_ref, dbb_ref, dcb_ref, dd_ref, gcarry):
        bi, i = pl.program_id(1), pl.program_id(2)

        @pl.when(i == 0)
        def _():
            gcarry[...] = jnp.zeros_like(gcarry)

        @pl.when((i == 0) & (bi == 0))
        def _():
            da_ref[...] = jnp.zeros_like(da_ref)
            dbb_ref[...] = jnp.zeros_like(dbb_ref)
            dcb_ref[...] = jnp.zeros_like(dcb_ref)
            dd_ref[...] = jnp.zeros_like(dd_ref)

        u = u_ref[0]
        dyv = dy_ref[0]
        dyb = dyv.astype(bf16)
        cin = cin_ref[0, 0]
        sr, si, _, _ = _scan_chunk(_dot(u, bb_ref[0, 0]), _dot(u, bb_ref[1, 0]), cin[0:1, :], cin[1:2, :],
                                   ap_ref, t8_ref, False)
        gr, gi, gor, goi = _scan_chunk(_dot(dyb, cb_ref[0, 0]), -_dot(dyb, cb_ref[1, 0]),
                                       gcarry[0:1, :], gcarry[1:2, :], ap_ref, t8_ref, True)
        gcarry[0:1, :] = gor
        gcarry[1:2, :] = goi
        srb, sib, grb, gib = sr.astype(bf16), si.astype(bf16), gr.astype(bf16), gi.astype(bf16)
        dcb_ref[0, 0] += _dot_tn(dyb, srb)
        dcb_ref[1, 0] -= _dot_tn(dyb, sib)
        dbb_ref[0, 0] += _dot_tn(u, grb)
        dbb_ref[1, 0] += _dot_tn(u, gib)
        du = _dot_nt(grb, bb_ref[0, 0]) + _dot_nt(gib, bb_ref[1, 0]) + d_ref[...] * dyv
        du_ref[0] = du.astype(bf16)
        dd_ref[...] += jnp.sum(dyv * u.astype(f32), axis=0, keepdims=True)
        row = lax.broadcasted_iota(jnp.int32, sr.shape, 0)
        spr = jnp.where(row == 0, cin[0:1, :], pltpu.roll(sr, 1, 0))
        spi = jnp.where(row == 0, cin[1:2, :], pltpu.roll(si, 1, 0))
        da_ref[0:1, :] += jnp.sum(spr * gr + spi * gi, axis=0, keepdims=True)
        da_ref[1:2, :] += jnp.sum(spr * gi - spi * gr, axis=0, keepdims=True)

    rc = lambda i: nch - 1 - i
    return pl.pallas_call(
        body, name="s5_bwd", grid=(SJ, b, nch),
        in_specs=[pl.BlockSpec((1, TS, SU), lambda j, bi, i: (bi, rc(i), j)),
                  pl.BlockSpec((1, TS, SU), lambda j, bi, i: (bi, rc(i), j)),
                  pl.BlockSpec((1, 1, 2, SW), lambda j, bi, i: (bi, rc(i), 0, j)),
                  pl.BlockSpec((2, 16, SW), lambda j, bi, i: (0, 0, j)),
                  pl.BlockSpec((2, 8, 8, SW), lambda j, bi, i: (0, 0, 0, j)),
                  pl.BlockSpec((2, 1, SU, SW), lambda j, bi, i: (0, j, 0, 0)),
                  pl.BlockSpec((2, 1, SU, SW), lambda j, bi, i: (0, j, 0, 0)),
                  pl.BlockSpec((1, SU), lambda j, bi, i: (0, j))],
        out_specs=[pl.BlockSpec((1, TS, SU), lambda j, bi, i: (bi, rc(i), j)),
                   pl.BlockSpec((2, SW), lambda j, bi, i: (0, j)),
                   pl.BlockSpec((2, 1, SU, SW), lambda j, bi, i: (0, j, 0, 0)),
                   pl.BlockSpec((2, 1, SU, SW), lambda j, bi, i: (0, j, 0, 0)),
                   pl.BlockSpec((1, SU), lambda j, bi, i: (0, j))],
        out_shape=[jax.ShapeDtypeStruct((b, l, BW), bf16), jax.ShapeDtypeStruct((2, NSTATE), f32),
                   jax.ShapeDtypeStruct((2, SJ, SU, SW), f32), jax.ShapeDtypeStruct((2, SJ, SU, SW), f32),
                   jax.ShapeDtypeStruct((1, BW), f32)],
        scratch_shapes=[pltpu.VMEM((2, SW), f32)],
        compiler_params=_cp(3))(proj3, dy, cins, apow, t8, bb, cb, dskip)


AHC = 2
AHW = AHC * 128


def _att_mask(n, nb):
    if nb == 1:
        qi = lax.broadcasted_iota(jnp.int32, (ABLK, ABLK), 0)
        kj = lax.broadcasted_iota(jnp.int32, (ABLK, ABLK), 1)
        return kj <= qi
    qi = lax.broadcasted_iota(jnp.int32, (ABLK, 2 * ABLK), 0)
    kj = lax.broadcasted_iota(jnp.int32, (ABLK, 2 * ABLK), 1)
    return (kj >= qi) & (kj <= qi + ABLK) & ((n > 0) | (kj >= ABLK))


def _att_rows(it, nb, dil):
    r, n = it // nb, it % nb
    cur = pl.ds(r + n * (ABLK * dil), ABLK, stride=dil)
    prv = pl.ds(r + jnp.maximum(n - 1, 0) * (ABLK * dil), ABLK, stride=dil)
    return n, cur, prv


def _att_keys(ref, c, cur, prv, nb):
    if nb == 1:
        return ref[c, cur, :].astype(bf16)
    return jnp.concatenate([ref[c, prv, :], ref[c, cur, :]], axis=0).astype(bf16)


def _to_chunks(src_ref, dst):
    for c in range(AHC):
        dst[c] = src_ref[0, :, c * 128:(c + 1) * 128].astype(f32)


def att_fwd(proj3, g_idx, dil):
    b, l, _ = proj3.shape
    nb = l // dil // ABLK
    nhalf = BW // AHW

    def body(q_ref, k_ref, v_ref, o_ref, lse_ref, qf, kf, vf, of):
        hh = pl.program_id(1)
        _to_chunks(q_ref, qf)
        _to_chunks(k_ref, kf)
        _to_chunks(v_ref, vf)
        lane = lax.broadcasted_iota(jnp.int32, (ABLK, 128), 1)

        def step(it, carry):
            n, cur, prv = _att_rows(it, nb, dil)
            valid = _att_mask(n, nb)
            lse_all = jnp.zeros((ABLK, 128), f32)
            for c in range(AHC):
                q = qf[c, cur, :].astype(bf16)
                k = _att_keys(kf, c, cur, prv, nb)
                v = _att_keys(vf, c, cur, prv, nb)
                outs = []
                for hl in range(2):
                    hs = slice(hl * HD, (hl + 1) * HD)
                    s = jnp.where(valid, _dot_nt(q[:, hs], k[:, hs]) * ATT_SCALE, NEG)
                    m = jnp.max(s, axis=-1, keepdims=True)
                    p = jnp.exp(s - m)
                    den = jnp.sum(p, axis=-1, keepdims=True)
                    outs.append(_dot(p.astype(bf16), v[:, hs]) / den)
                    lse_all = lse_all + jnp.where(lane == hh * (2 * AHC) + 2 * c + hl, m + jnp.log(den), 0.0)
                of[c, cur, :] = jnp.concatenate(outs, axis=1)

            @pl.when(hh == 0)
            def _():
                lse_ref[0, cur, :] = lse_all

            @pl.when(hh > 0)
            def _():
                lse_ref[0, cur, :] += lse_all

            return carry

        lax.fori_loop(0, dil * nb, step, 0)
        for c in range(AHC):
            o_ref[0, :, c * 128:(c + 1) * 128] = of[c].astype(bf16)

    col = lambda c: pl.BlockSpec((1, l, AHW), lambda bi, hh: (bi, 0, c * nhalf + hh))
    return pl.pallas_call(
        body, name=f"att_fwd{g_idx}", grid=(b, nhalf),
        in_specs=[col(1 + g_idx), col(4), col(5)],
        out_specs=[pl.BlockSpec((1, l, AHW), lambda bi, hh: (bi, 0, hh)),
                   pl.BlockSpec((1, l, 128), lambda bi, hh: (bi, 0, 0))],
        out_shape=[jax.ShapeDtypeStruct((b, l, BW), bf16), jax.ShapeDtypeStruct((b, l, 128), f32)],
        scratch_shapes=[pltpu.VMEM((AHC, l, 128), f32)] * 4,
        compiler_params=_cp(2))(proj3, proj3, proj3)


def att_bwd(proj3, do, lse_tot, delta, g_idx, dil):
    b, l, _ = proj3.shape
    nb = l // dil // ABLK
    nhalf = BW // AHW

    def body(q_ref, k_ref, v_ref, do_ref, l_ref, dl_ref, dq_ref, dk_ref, dv_ref, qf, kf, vf, dof):
        hh = pl.program_id(1)
        _to_chunks(q_ref, qf)
        _to_chunks(k_ref, kf)
        _to_chunks(v_ref, vf)
        _to_chunks(do_ref, dof)
        dk_ref[...] = jnp.zeros_like(dk_ref)
        dv_ref[...] = jnp.zeros_like(dv_ref)
        lane = lax.broadcasted_iota(jnp.int32, (ABLK, 128), 1)

        def step(it, carry):
            n, cur, prv = _att_rows(it, nb, dil)
            valid = _att_mask(n, nb)
            lse_b = l_ref[0, cur, :]
            dl_b = dl_ref[0, cur, :]
            for c in range(AHC):
                q = qf[c, cur, :].astype(bf16)
                dob = dof[c, cur, :].astype(bf16)
                k = _att_keys(kf, c, cur, prv, nb)
                v = _att_keys(vf, c, cur, prv, nb)
                dqs, dks, dvs = [], [], []
                for hl in range(2):
                    hs = slice(hl * HD, (hl + 1) * HD)
                    mine = lane == hh * (2 * AHC) + 2 * c + hl
                    lse_h = jnp.sum(jnp.where(mine, lse_b, 0.0), axis=-1, keepdims=True)
                    dl_h = jnp.sum(jnp.where(mine, dl_b, 0.0), axis=-1, keepdims=True)
                    s = _dot_nt(q[:, hs], k[:, hs]) * ATT_SCALE
                    p = jnp.where(valid, jnp.exp(jnp.minimum(s - lse_h, 60.0)), 0.0)
                    dp = _dot_nt(dob[:, hs], v[:, hs])
                    ds = (p * (dp - dl_h) * ATT_SCALE).astype(bf16)
                    dqs.append(_dot(ds, k[:, hs]))
                    dks.append(_dot_tn(ds, q[:, hs]))
                    dvs.append(_dot_tn(p.astype(bf16), dob[:, hs]))
                dq_ref[0, c, cur, :] = jnp.concatenate(dqs, axis=1)
                dk = jnp.concatenate(dks, axis=1)
                dv = jnp.concatenate(dvs, axis=1)
                if nb == 1:
                    dk_ref[0, c, cur, :] += dk
                    dv_ref[0, c, cur, :] += dv
                else:
                    dk_ref[0, c, cur, :] += dk[ABLK:]
                    dv_ref[0, c, cur, :] += dv[ABLK:]

                    @pl.when(n > 0)
                    def _():
                        dk_ref[0, c, prv, :] += dk[:ABLK]
                        dv_ref[0, c, prv, :] += dv[:ABLK]

            return carry

        lax.fori_loop(0, dil * nb, step, 0)

    col = lambda c: pl.BlockSpec((1, l, AHW), lambda bi, hh: (bi, 0, c * nhalf + hh))
    own = pl.BlockSpec((1, l, AHW), lambda bi, hh: (bi, 0, hh))
    own128 = pl.BlockSpec((1, l, 128), lambda bi, hh: (bi, 0, 0))
    chunked = pl.BlockSpec((1, AHC, l, 128), lambda bi, hh: (bi, hh, 0, 0))
    return pl.pallas_call(
        body, name=f"att_bwd{g_idx}", grid=(b, nhalf),
        in_specs=[col(1 + g_idx), col(4), col(5), own, own128, own128],
        out_specs=[chunked] * 3,
        out_shape=[jax.ShapeDtypeStruct((b, BW // 128, l, 128), f32)] * 3,
        scratch_shapes=[pltpu.VMEM((AHC, l, 128), f32)] * 4,
        compiler_params=_cp(2, 56))(proj3, proj3, proj3, do, lse_tot, delta)


CPAD = 32
CCH = 256


def _tap_slabs(ref, base):
    for mis in range(8):
        taps = (CW - 1 - mis) // 8 + 1
        yield mis, taps, ref[pl.ds(base + mis, CCH + 8 * (taps - 1)), :]


def conv_fwd(proj3, cw, cb):
    b, l, _ = proj3.shape

    def body(cv_ref, w_ref, b_ref, o_ref, pad):
        pad[0:CPAD, :] = jnp.zeros((CPAD, BW), f32)
        pad[CPAD:, :] = cv_ref[0, :, :BW].astype(f32) * _sigmoid(cv_ref[0, :, BW:].astype(f32))
        for c in range(l // CCH):
            acc = jnp.zeros((CCH, BW), f32) + b_ref[...]
            for mis, taps, xb in _tap_slabs(pad, c * CCH + CPAD - (CW - 1)):
                for a in range(taps):
                    k = 8 * a + mis
                    acc = acc + w_ref[k:k + 1, :] * xb[8 * a:8 * a + CCH]
            o_ref[0, c * CCH:(c + 1) * CCH, :] = acc

    return pl.pallas_call(
        body, name="conv_fwd", grid=(b,),
        in_specs=[pl.BlockSpec((1, l, 2 * BW), lambda i: (i, 0, 3)),
                  pl.BlockSpec((32, BW), lambda i: (0, 0)), pl.BlockSpec((1, BW), lambda i: (0, 0))],
        out_specs=pl.BlockSpec((1, l, BW), lambda i: (i, 0, 0)),
        out_shape=jax.ShapeDtypeStruct((b, l, BW), f32),
        scratch_shapes=[pltpu.VMEM((l + CPAD, BW), f32)], compiler_params=_cp(1))(proj3, cw, cb)


def conv_bwd(proj3, dhc, cw):
    b, l, _ = proj3.shape

    def body(cv_ref, d_ref, w_ref, dcv_ref, dw_ref, db_ref, pad, dpad):
        i = pl.program_id(0)

        @pl.when(i == 0)
        def _():
            dw_ref[...] = jnp.zeros_like(dw_ref)
            db_ref[...] = jnp.zeros_like(db_ref)

        pad[0:CPAD, :] = jnp.zeros((CPAD, BW), f32)
        pad[CPAD:, :] = cv_ref[0, :, :BW].astype(f32) * _sigmoid(cv_ref[0, :, BW:].astype(f32))
        dpad[l:, :] = jnp.zeros((CPAD, BW), f32)
        dpad[0:l, :] = d_ref[0]
        db_ref[...] += jnp.sum(d_ref[0], axis=0, keepdims=True)
        for mis in range(8):
            taps = (CW - 1 - mis) // 8 + 1
            dws = [jnp.zeros((8, BW), f32) for _ in range(taps)]
            for c in range(l // CCH):
                dv = d_ref[0, c * CCH:(c + 1) * CCH, :]
                xb = pad[pl.ds(c * CCH + CPAD - (CW - 1) + mis, CCH + 8 * (taps - 1)), :]
                for a in range(taps):
                    prod = dv * xb[8 * a:8 * a + CCH]
                    dws[a] = dws[a] + prod.reshape(CCH // 8, 8, BW).sum(axis=0)
            for a in range(taps):
                k = 8 * a + mis
                dw_ref[k:k + 1, :] += jnp.sum(dws[a], axis=0, keepdims=True)
        for c in range(l // CCH):
            rows = slice(c * CCH, (c + 1) * CCH)
            acc = jnp.zeros((CCH, BW), f32)
            for mis, taps, xb in _tap_slabs(dpad, c * CCH):
                for a in range(taps):
                    k = CW - 1 - (8 * a + mis)
                    acc = acc + w_ref[k:k + 1, :] * xb[8 * a:8 * a + CCH]
            a = cv_ref[0, rows, :BW].astype(f32)
            sg = _sigmoid(cv_ref[0, rows, BW:].astype(f32))
            dcv_ref[0, rows, :BW] = (acc * sg).astype(bf16)
            dcv_ref[0, rows, BW:] = (acc * a * sg * (1.0 - sg)).astype(bf16)

    return pl.pallas_call(
        body, name="conv_bwd", grid=(b,),
        in_specs=[pl.BlockSpec((1, l, 2 * BW), lambda i: (i, 0, 3)),
                  pl.BlockSpec((1, l, BW), lambda i: (i, 0, 0)),
                  pl.BlockSpec((32, BW), lambda i: (0, 0))],
        out_specs=[pl.BlockSpec((1, l, 2 * BW), lambda i: (i, 0, 0)),
                   pl.BlockSpec((32, BW), lambda i: (0, 0)), pl.BlockSpec((1, BW), lambda i: (0, 0))],
        out_shape=[jax.ShapeDtypeStruct((b, l, 2 * BW), bf16), jax.ShapeDtypeStruct((32, BW), f32),
                   jax.ShapeDtypeStruct((1, BW), f32)],
        scratch_shapes=[pltpu.VMEM((l + CPAD, BW), f32), pltpu.VMEM((l + CPAD, BW), f32)],
        compiler_params=_cp(1))(proj3, dhc, cw)


def _head_expand():
    r = lax.broadcasted_iota(jnp.int32, (128, BW), 0)
    c = lax.broadcasted_iota(jnp.int32, (128, BW), 1) // HD
    return (r == c).astype(f32)


def _head_reduce():
    r = lax.broadcasted_iota(jnp.int32, (BW, 128), 0) // HD
    c = lax.broadcasted_iota(jnp.int32, (BW, 128), 1)
    return (r == c).astype(f32)


def _merge_common(ys_ref, o_refs, l_refs, hc_ref, g_refs, bg_ref, lng_ref, lnb_ref, wglu_ref, watt_ref, wpw_ref):
    r = {}
    ysv = ys_ref[...]
    r["ys"] = ysv
    r["ysin"] = _gelu(ysv).astype(bf16)
    z = _dot(r["ysin"], wglu_ref[...])
    r["z1"], r["sg2"] = z[:, :D], _sigmoid(z[:, D:])
    r["y_s"] = r["z1"] * r["sg2"]
    ls = [lr_[...] for lr_ in l_refs]
    mx = jnp.maximum(jnp.maximum(ls[0], ls[1]), ls[2])
    es = [jnp.exp(v - mx) for v in ls]
    tot = es[0] + es[1] + es[2]
    r["lse_tot"] = mx + jnp.log(tot)
    e_mat = _head_expand()
    o = jnp.zeros(ysv.shape, f32)
    for e, o_ref in zip(es, o_refs):
        o = o + _dot_hi(e / tot, e_mat) * o_ref[...].astype(f32)
    r["o"] = o
    r["ob"] = o.astype(bf16)
    r["y_a"] = _dot(r["ob"], watt_ref[...])
    hc = hc_ref[...]
    mu = jnp.mean(hc, axis=-1, keepdims=True)
    xc = hc - mu
    rstd = lax.rsqrt(jnp.mean(xc * xc, axis=-1, keepdims=True) + EPS)
    r["xh"], r["rstd"] = xc * rstd, rstd
    hn = r["xh"] * lng_ref[...] + lnb_ref[...]
    r["hn"] = hn
    r["sgn"] = _sigmoid(hn)
    r["hs"] = (hn * r["sgn"]).astype(bf16)
    r["y_c"] = _dot(r["hs"], wpw_ref[...])
    r["gates"] = [_sigmoid(g_refs[k][...].astype(f32) + bg_ref[:, k * D:(k + 1) * D]) for k in range(3)]
    r["merged"] = r["gates"][0] * r["y_s"] + r["gates"][1] * r["y_a"] + r["gates"][2] * r["y_c"]
    return r


TBM = 256


def _merge_in_specs(tok, tb):
    w = lambda shape: pl.BlockSpec(shape, lambda i: (0, 0))
    return ([pl.BlockSpec((tb, D), tok), pl.BlockSpec((tb, BW), tok)]
            + [pl.BlockSpec((tb, BW), tok)] * 3 + [pl.BlockSpec((tb, 128), tok)] * 3
            + [pl.BlockSpec((tb, BW), tok)]
            + [pl.BlockSpec((tb, D), lambda i, k=k: (i, 4 + k)) for k in range(3)]
            + [w((1, 3 * D)), w((1, BW)), w((1, BW)), w((BW, 2 * D)), w((BW, D)), w((BW, D)), w((D, D))])


def merge_fwd(x, ys, os_, lses, hc, proj, bg, lng, lnb, wglu, watt, wpw, wout):
    n = x.shape[0]

    def body(x_ref, ys_ref, o1, o2, o3, l1, l2, l3, hc_ref, g0, g1, g2, bg_ref, lng_ref, lnb_ref,
             wglu_ref, watt_ref, wpw_ref, wout_ref, x1_ref):
        r = _merge_common(ys_ref, (o1, o2, o3), (l1, l2, l3), hc_ref, (g0, g1, g2), bg_ref, lng_ref, lnb_ref,
                          wglu_ref, watt_ref, wpw_ref)
        x1_ref[...] = x_ref[...] + _dot(r["merged"].astype(bf16), wout_ref[...])

    tok = lambda i: (i, 0)
    return pl.pallas_call(
        body, name="merge_fwd", grid=(n // TB,), in_specs=_merge_in_specs(tok, TB),
        out_specs=pl.BlockSpec((TB, D), tok), out_shape=jax.ShapeDtypeStruct((n, D), f32),
        compiler_params=_cp(1, 56))(x, ys, *os_, *lses, hc, proj, proj, proj, bg, lng, lnb, wglu, watt, wpw, wout)


def merge_bwd(dx1, ys, os_, lses, hc, proj, bg, lng, lnb, wglu, watt, wpw, wout):
    n = dx1.shape[0]

    def body(dx_ref, ys_ref, o1, o2, o3, l1, l2, l3, hc_ref, g0, g1, g2, bg_ref, lng_ref, lnb_ref,
             wglu_ref, watt_ref, wpw_ref, wout_ref,
             dys_ref, do_ref, delta_ref, ltot_ref, dhc_ref, dgate_ref, ysin_ref, dz_ref, ob_ref, dya_ref,
             hs_ref, dyc_ref, mg_ref, dbg_ref, dlng_ref, dlnb_ref):
        i = pl.program_id(0)

        @pl.when(i == 0)
        def _():
            dbg_ref[...] = jnp.zeros_like(dbg_ref)
            dlng_ref[...] = jnp.zeros_like(dlng_ref)
            dlnb_ref[...] = jnp.zeros_like(dlnb_ref)

        r = _merge_common(ys_ref, (o1, o2, o3), (l1, l2, l3), hc_ref, (g0, g1, g2), bg_ref, lng_ref, lnb_ref,
                          wglu_ref, watt_ref, wpw_ref)
        mg_ref[...] = r["merged"].astype(bf16)
        ysin_ref[...] = r["ysin"]
        ob_ref[...] = r["ob"]
        hs_ref[...] = r["hs"]
        ltot_ref[...] = r["lse_tot"]
        dm = _dot_nt(dx_ref[...].astype(bf16), wout_ref[...])
        ys3 = (r["y_s"], r["y_a"], r["y_c"])
        for k in range(3):
            gk = r["gates"][k]
            dgr = dm * ys3[k] * gk * (1.0 - gk)
            dgate_ref[:, k * D:(k + 1) * D] = dgr.astype(bf16)
            dbg_ref[:, k * D:(k + 1) * D] += jnp.sum(dgr, axis=0, keepdims=True)
        dy_s = dm * r["gates"][0]
        sg2 = r["sg2"]
        dz = jnp.concatenate([dy_s * sg2, dy_s * r["z1"] * sg2 * (1.0 - sg2)], axis=1).astype(bf16)
        dz_ref[...] = dz
        dys_ref[...] = _dot_nt(dz, wglu_ref[...]) * _gelu_grad(r["ys"])
        dya = (dm * r["gates"][1]).astype(bf16)
        dya_ref[...] = dya
        do = _dot_nt(dya, watt_ref[...])
        do_ref[...] = do.astype(bf16)
        delta_ref[...] = _dot_hi(do * r["o"], _head_reduce())
        dyc = (dm * r["gates"][2]).astype(bf16)
        dyc_ref[...] = dyc
        sgn, hn = r["sgn"], r["hn"]
        dhn = _dot_nt(dyc, wpw_ref[...]) * sgn * (1.0 + hn * (1.0 - sgn))
        dlng_ref[...] += jnp.sum(dhn * r["xh"], axis=0, keepdims=True)
        dlnb_ref[...] += jnp.sum(dhn, axis=0, keepdims=True)
        dxh = dhn * lng_ref[...]
        xh = r["xh"]
        dhc_ref[...] = r["rstd"] * (dxh - jnp.mean(dxh, axis=-1, keepdims=True)
                                    - xh * jnp.mean(dxh * xh, axis=-1, keepdims=True))

    tok = lambda i: (i, 0)
    fix = lambda i: (0, 0)
    outs = [("dys", BW, f32), ("do", BW, bf16), ("delta", 128, f32), ("lse_tot", 128, f32), ("dhc", BW, f32),
            ("dgate", 3 * D, bf16), ("ysin", BW, bf16), ("dz", 2 * D, bf16), ("ob", BW, bf16), ("dya", D, bf16),
            ("hs", BW, bf16), ("dyc", D, bf16), ("merged", D, bf16)]
    small = [("dbg", 3 * D), ("dlng", BW), ("dlnb", BW)]
    res = pl.pallas_call(
        body, name="merge_bwd", grid=(n // TBM,), in_specs=_merge_in_specs(tok, TBM),
        out_specs=[pl.BlockSpec((TBM, w), tok) for _, w, _ in outs] + [pl.BlockSpec((1, w), fix) for _, w in small],
        out_shape=[jax.ShapeDtypeStruct((n, w), dt) for _, w, dt in outs]
        + [jax.ShapeDtypeStruct((1, w), f32) for _, w in small],
        compiler_params=_cp(1, 56))(dx1, ys, *os_, *lses, hc, proj, proj, proj, bg, lng, lnb, wglu, watt, wpw, wout)
    return dict(zip([k for k, _, _ in outs] + [k for k, _ in small], res))


def assemble_dproj(du, dqs, dks, dvs, dcv, dgate):
    b, l, _ = du.shape
    nck = BW // 128

    def body(du_ref, q1, q2, q3, k1, k2, k3, v1, v2, v3, cv_ref, g_ref, o_ref):
        o_ref[0, :, 0:BW] = du_ref[0]
        for c in range(nck):
            for j, qr in enumerate((q1, q2, q3)):
                o_ref[0, :, (1 + j) * BW + c * 128:(1 + j) * BW + (c + 1) * 128] = qr[0, c].astype(bf16)
            o_ref[0, :, 4 * BW + c * 128:4 * BW + (c + 1) * 128] = (k1[0, c] + k2[0, c] + k3[0, c]).astype(bf16)
            o_ref[0, :, 5 * BW + c * 128:5 * BW + (c + 1) * 128] = (v1[0, c] + v2[0, c] + v3[0, c]).astype(bf16)
        o_ref[0, :, 6 * BW:8 * BW] = cv_ref[0]
        o_ref[0, :, 8 * BW:] = g_ref[0]

    t = lambda w: pl.BlockSpec((1, TB, w), lambda bi, i: (bi, i, 0))
    ck = pl.BlockSpec((1, nck, TB, 128), lambda bi, i: (bi, 0, i, 0))
    return pl.pallas_call(
        body, name="assemble_dproj", grid=(b, l // TB),
        in_specs=[t(BW)] + [ck] * 9 + [t(2 * BW), t(3 * D)], out_specs=t(INC),
        out_shape=jax.ShapeDtypeStruct((b, l, INC), bf16), compiler_params=_cp(2))(du, *dqs, *dks, *dvs, dcv, dgate)


def _me():
    return lax.axis_index("x"), lax.axis_index("y"), lax.axis_index("c")


def _peers():
    x, y, c = _me()
    return [(x, y, 1 - c), (1 - x, y, c), (1 - x, y, 1 - c), (x, 1 - y, c), (x, 1 - y, 1 - c),
            (1 - x, 1 - y, c), (1 - x, 1 - y, 1 - c)]


def _rank(p):
    return 4 * p[0] + 2 * p[1] + p[2]


def allgather(arrs, name):
    na = len(arrs)
    units = [(a, j) for a in range(na) for j in range(arrs[a].shape[0])]
    nu = len(units)

    def body(*refs):
        ins, outs = refs[:na], refs[na:2 * na]
        send, recv, loc = refs[2 * na:]
        me = _rank(_me())
        local, remote = [], []
        for u, (a, j) in enumerate(units):
            own = pltpu.make_async_copy(ins[a].at[j], outs[a].at[j, me], loc.at[u])
            own.start()
            local.append(own)
        for u, (a, j) in enumerate(units):
            for k, p in enumerate(_peers()):
                cp = pltpu.make_async_remote_copy(src_ref=ins[a].at[j], dst_ref=outs[a].at[j, me],
                                                  send_sem=send.at[u, k], recv_sem=recv.at[u, k],
                                                  device_id=p, device_id_type=MESH)
                cp.start()
                remote.append(cp)
        for cp in local:
            cp.wait()
        for cp in remote:
            cp.wait()

    return pl.pallas_call(
        body, name=name, in_specs=[ANY] * na, out_specs=[ANY] * na,
        out_shape=[jax.ShapeDtypeStruct((a.shape[0], NDEV) + a.shape[1:], a.dtype) for a in arrs],
        scratch_shapes=[pltpu.SemaphoreType.DMA((nu, NDEV - 1)), pltpu.SemaphoreType.DMA((nu, NDEV - 1)),
                        pltpu.SemaphoreType.DMA((nu,))])(*arrs)


def exchange_grads(parts):
    nw = len(parts)
    flat = [p for pair in parts for p in pair]
    na = len(flat)

    def body(*refs):
        ins, outs = refs[:na], refs[na:na + nw]
        send, recv, loc = refs[na + nw:]
        me = _rank(_me())
        local, remote = [], []
        for a in range(na):
            w, layer = divmod(a, DEPTH)
            own = pltpu.make_async_copy(ins[a].at[me], outs[w].at[layer, me], loc.at[a])
            own.start()
            local.append(own)
        for a in range(na):
            w, layer = divmod(a, DEPTH)
            for k, p in enumerate(_peers()):
                cp = pltpu.make_async_remote_copy(src_ref=ins[a].at[_rank(p)], dst_ref=outs[w].at[layer, me],
                                                  send_sem=send.at[a, k], recv_sem=recv.at[a, k],
                                                  device_id=p, device_id_type=MESH)
                cp.start()
                remote.append(cp)
        for cp in local:
            cp.wait()
        for cp in remote:
            cp.wait()

    return pl.pallas_call(
        body, name="exchange_grads", in_specs=[ANY] * na, out_specs=[ANY] * nw,
        out_shape=[jax.ShapeDtypeStruct((DEPTH,) + pair[0].shape, pair[0].dtype) for pair in parts],
        scratch_shapes=[pltpu.SemaphoreType.DMA((na, NDEV - 1)), pltpu.SemaphoreType.DMA((na, NDEV - 1)),
                        pltpu.SemaphoreType.DMA((na,))])(*flat)


_C1 = 1.0 / (1.0 - ADAM_B1 ** ADAM_STEP)
_C2 = 1.0 / (1.0 - ADAM_B2 ** ADAM_STEP)


def _adamw(w, g, m, v):
    m = ADAM_B1 * m + (1.0 - ADAM_B1) * g
    v = ADAM_B2 * v + (1.0 - ADAM_B2) * (g * g)
    delta = -ADAM_LR * ((m * _C1) / (jnp.sqrt(v * _C2) + ADAM_EPS) + ADAM_WD * w)
    return delta, m, v


def adam_big(recv, w, m, v, name):
    _, _, k, n = recv.shape
    tk = k
    while tk * n * 2 * NDEV > 4 * 1024 * 1024 and tk % 16 == 0:
        tk //= 2

    def body(r_ref, w_ref, m_ref, v_ref, g_ref, d_ref, nm_ref, nv_ref):
        g = r_ref[0, 0].astype(f32)
        for s in range(1, NDEV):
            g = g + r_ref[0, s].astype(f32)
        d, nm, nv = _adamw(w_ref[0], g, m_ref[0], v_ref[0])
        g_ref[0], d_ref[0], nm_ref[0], nv_ref[0] = g, d, nm, nv

    blk = pl.BlockSpec((1, tk, n), lambda l, i: (l, i, 0))
    return pl.pallas_call(
        body, name=name, grid=(DEPTH, k // tk),
        in_specs=[pl.BlockSpec((1, NDEV, tk, n), lambda l, i: (l, 0, i, 0)), blk, blk, blk],
        out_specs=[blk] * 4, out_shape=[jax.ShapeDtypeStruct(w.shape, f32)] * 4,
        compiler_params=_cp(2))(recv, w, m, v)


def adam_small(gath, w, m, v):
    r = w.shape[0]
    tr = 512

    def body(g_ref, w_ref, m_ref, v_ref, go_ref, d_ref, nm_ref, nv_ref):
        g = g_ref[0]
        for s in range(1, NDEV):
            g = g + g_ref[s]
        d, nm, nv = _adamw(w_ref[...], g, m_ref[...], v_ref[...])
        go_ref[...], d_ref[...], nm_ref[...], nv_ref[...] = g, d, nm, nv

    blk = pl.BlockSpec((tr, 128), lambda i: (i, 0))
    return pl.pallas_call(
        body, name="adam_small", grid=(r // tr,),
        in_specs=[pl.BlockSpec((NDEV, tr, 128), lambda i: (0, i, 0)), blk, blk, blk],
        out_specs=[blk] * 4, out_shape=[jax.ShapeDtypeStruct((r, 128), f32)] * 4,
        compiler_params=_cp(1))(gath, w, m, v)


SMALL = ["norm1_g", "b_gate", "ssm_lambda_re", "ssm_lambda_im", "ssm_log_dt", "ssm_b_re", "ssm_b_im",
         "ssm_c_re", "ssm_c_im", "ssm_d", "conv_w", "conv_b", "conv_ln_g", "conv_ln_b", "norm2_g", "final_g"]
BIG = ["w_in", "w_ssm_glu", "w_att_up", "w_conv_pw2", "w_out", "w_ffn_in", "w_ffn_out"]
ORDER = ["norm1_g", "w_in", "b_gate", "ssm_lambda_re", "ssm_lambda_im", "ssm_log_dt", "ssm_b_re", "ssm_b_im",
         "ssm_c_re", "ssm_c_im", "ssm_d", "w_ssm_glu", "w_att_up", "conv_w", "conv_b", "conv_ln_g", "conv_ln_b",
         "w_conv_pw2", "w_out", "norm2_g", "w_ffn_in", "w_ffn_out", "final_g"]
PACK_ROWS = 2560


def _pack(arrs):
    flat = jnp.concatenate([a.reshape(-1).astype(f32) for a in arrs])
    return jnp.pad(flat, (0, PACK_ROWS * 128 - flat.shape[0])).reshape(PACK_ROWS, 128)


def _unpack(pack, shapes):
    flat = pack.reshape(-1)
    out, off = [], 0
    for s in shapes:
        sz = math.prod(s)
        out.append(flat[off:off + sz].reshape(s))
        off += sz
    return out


def _bt(b):
    return b.transpose(2, 0, 1).reshape(GH, NSTATE)


def _bt_inv(bt):
    return bt.reshape(GH, NG, NS).transpose(1, 2, 0)


def _ct(c):
    return c.transpose(1, 0, 2).reshape(GH, NSTATE)


def _ct_inv(ct):
    return ct.reshape(GH, NG, NS).transpose(1, 0, 2)


def local_step(x, loss_target, P, G):
    bsz, seq, _ = x.shape
    n = bsz * seq

    def natural(g4, layer):
        k_, n_ = g4.shape[2], g4.shape[3]
        return g4[layer].transpose(1, 0, 2).reshape(k_, NDEV * n_)

    conv_w_full = G["conv_w"].transpose(0, 2, 1, 3).reshape(DEPTH, CW, BW)
    conv_w_pad = jnp.pad(conv_w_full, ((0, 0), (0, 1), (0, 0)))

    xs = x.reshape(n, D)
    saved = []
    for l in range(DEPTH):
        S = {"x": xs}
        h1 = rms_fwd(xs, P["norm1_g"][l][None])
        proj = inproj(h1, G["w_in"], l)
        proj3 = proj.reshape(bsz, seq, INC)
        lr = P["ssm_lambda_re"][l].reshape(1, NSTATE)
        li = P["ssm_lambda_im"][l].reshape(1, NSTATE)
        ld = jnp.repeat(P["ssm_log_dt"][l], NS).reshape(1, NSTATE)
        btr, bti = _bt(P["ssm_b_re"][l]), _bt(P["ssm_b_im"][l])
        apow, t8, bb, cb = s5_params(lr, li, ld, btr, bti, _ct(P["ssm_c_re"][l]), _ct(P["ssm_c_im"][l]))
        dskip = P["ssm_d"][l][None]
        ys, cins = s5_fwd(proj3, apow, t8, bb, cb, dskip)
        att = [att_fwd(proj3, gi, dil) for gi, (_, dil) in enumerate(PATTERNS)]
        hc = conv_fwd(proj3, conv_w_pad[l], P["conv_b"][l][None])
        wts = dict(wglu=natural(G["w_ssm_glu"], l), watt=natural(G["w_att_up"], l),
                   wpw=natural(G["w_conv_pw2"], l), wout=G["w_out"][l].reshape(D, D))
        mi = dict(ys=ys.reshape(n, BW), os_=[a[0].reshape(n, BW) for a in att],
                  lses=[a[1].reshape(n, 128) for a in att], hc=hc.reshape(n, BW),
                  proj=proj, bg=P["b_gate"][l][None], lng=P["conv_ln_g"][l][None], lnb=P["conv_ln_b"][l][None],
                  **wts)
        x1 = merge_fwd(xs, **mi)
        x2 = ffn_fwd(x1, P["norm2_g"][l][None], G["w_ffn_in"], G["w_ffn_out"], l)
        S.update(h1=h1, proj=proj, proj3=proj3, tabs=(apow, t8, bb, cb), cins=cins, mi=mi, x1=x1,
                 sp=(lr, li, ld, btr, bti), dskip=dskip)
        saved.append(S)
        xs = x2

    loss8, dx, dfinal = loss_head(xs, P["final_g"][None], loss_target.reshape(n, D))

    small_g = {k: [None] * DEPTH for k in SMALL if k != "final_g"}
    big_g = {k: [None] * DEPTH for k in BIG}
    tokblk = lambda w: pl.BlockSpec((1024, w), lambda s, i: (i, 0))
    colblk = lambda w: pl.BlockSpec((1024, w), lambda s, i: (i, s))
    sh3blk = lambda w: pl.BlockSpec((1, 1024, w), lambda s, i: (s, i, 0))
    for l in reversed(range(DEPTH)):
        S = saved[l]
        g2 = P["norm2_g"][l][None]
        dx1, h2, dz1, dz2, a4, dg2 = ffn_bwd(S["x1"], g2, G["w_ffn_in"], G["w_ffn_out"], l, dx)
        small_g["norm2_g"][l] = dg2
        dwa = mm_tn(h2, dz1, tokblk(D), sh3blk(NSH_FF), 4, D, NSH_FF, n, "dw_ffn_in_a")
        dwb = mm_tn(h2, dz2, tokblk(D), sh3blk(NSH_FF), 4, D, NSH_FF, n, "dw_ffn_in_b")
        big_g["w_ffn_in"][l] = jnp.concatenate([dwa, dwb], axis=0)
        big_g["w_ffn_out"][l] = mm_tn(a4, dx, sh3blk(NSH_FF), tokblk(D), 4, NSH_FF, D, n,
                                      "dw_ffn_out").reshape(NDEV, NSH_FF // 2, D)
        mb = merge_bwd(dx1, **S["mi"])
        small_g["b_gate"][l], small_g["conv_ln_g"][l], small_g["conv_ln_b"][l] = mb["dbg"], mb["dlng"], mb["dlnb"]
        big_g["w_ssm_glu"][l] = mm_tn(mb["ysin"], mb["dz"], tokblk(BW), colblk(256), NDEV, BW, 256, n, "dw_glu")
        big_g["w_att_up"][l] = mm_tn(mb["ob"], mb["dya"], tokblk(BW), colblk(128), NDEV, BW, 128, n, "dw_att")
        big_g["w_conv_pw2"][l] = mm_tn(mb["hs"], mb["dyc"], tokblk(BW), colblk(128), NDEV, BW, 128, n, "dw_pw2")
        big_g["w_out"][l] = mm_tn(mb["merged"], dx1, tokblk(D), tokblk(D), 1, D, D, n,
                                  "dw_out").reshape(NDEV, D // NDEV, D)
        dcv, dcw, dcb = conv_bwd(S["proj3"], mb["dhc"].reshape(bsz, seq, BW), conv_w_pad[l])
        small_g["conv_w"][l] = dcw[:CW].reshape(CW, NDEV, BW // NDEV).transpose(1, 0, 2)
        small_g["conv_b"][l] = dcb
        ab = [att_bwd(S["proj3"], mb["do"].reshape(bsz, seq, BW), mb["lse_tot"].reshape(bsz, seq, 128),
                      mb["delta"].reshape(bsz, seq, 128), gi, dil) for gi, (_, dil) in enumerate(PATTERNS)]
        apow, t8, bb, cb = S["tabs"]
        du, d_a, d_bb, d_cb, d_d = s5_bwd(S["proj3"], mb["dys"].reshape(bsz, seq, BW), S["cins"], apow, t8,
                                           bb, cb, S["dskip"])
        lr, li, ld, btr, bti = S["sp"]
        dlr, dli, dld, dbt, dct = s5_params_bwd(lr, li, ld, btr, bti, d_a, d_bb, d_cb)
        small_g["ssm_lambda_re"][l], small_g["ssm_lambda_im"][l] = dlr.reshape(NG, NS), dli.reshape(NG, NS)
        small_g["ssm_log_dt"][l] = dld[0, :NG]
        small_g["ssm_b_re"][l], small_g["ssm_b_im"][l] = _bt_inv(dbt[0]), _bt_inv(dbt[1])
        small_g["ssm_c_re"][l], small_g["ssm_c_im"][l] = _ct_inv(dct[0]), _ct_inv(dct[1])
        small_g["ssm_d"][l] = d_d
        dproj = assemble_dproj(du, [a[0] for a in ab], [a[1] for a in ab], [a[2] for a in ab],
                               dcv, mb["dgate"].reshape(bsz, seq, 3 * D)).reshape(n, INC)
        big_g["w_in"][l] = mm_tn(S["h1"], dproj, tokblk(D), colblk(NSH_IN), NDEV, D, NSH_IN, n, "dw_in")
        dx, dg1 = inproj_bwd(dproj, G["w_in"], l, S["x"], P["norm1_g"][l][None], dx1)
        small_g["norm1_g"][l] = dg1
    return loss8, dx, dfinal, small_g, big_g


def kernel(x, norm1_g, w_in, b_gate, ssm_lambda_re, ssm_lambda_im, ssm_log_dt, ssm_b_re, ssm_b_im, ssm_c_re, ssm_c_im, ssm_d, w_ssm_glu, w_att_up, conv_w, conv_b, conv_ln_g, conv_ln_b, w_conv_pw2, w_out, norm2_g, w_ffn_in, w_ffn_out, final_g, loss_target, m_norm1_g, m_w_in, m_b_gate, m_ssm_lambda_re, m_ssm_lambda_im, m_ssm_log_dt, m_ssm_b_re, m_ssm_b_im, m_ssm_c_re, m_ssm_c_im, m_ssm_d, m_w_ssm_glu, m_w_att_up, m_conv_w, m_conv_b, m_conv_ln_g, m_conv_ln_b, m_w_conv_pw2, m_w_out, m_norm2_g, m_w_ffn_in, m_w_ffn_out, m_final_g, v_norm1_g, v_w_in, v_b_gate, v_ssm_lambda_re, v_ssm_lambda_im, v_ssm_log_dt, v_ssm_b_re, v_ssm_b_im, v_ssm_c_re, v_ssm_c_im, v_ssm_d, v_w_ssm_glu, v_w_att_up, v_conv_w, v_conv_b, v_conv_ln_g, v_conv_ln_b, v_w_conv_pw2, v_w_out, v_norm2_g, v_w_ffn_in, v_w_ffn_out, v_final_g):
    args = dict(locals())
    W = {k: args[k] for k in ORDER}
    M = {k: args["m_" + k] for k in ORDER}
    V = {k: args["v_" + k] for k in ORDER}
    bsz, seq, _ = x.shape
    n = bsz * seq
    me = 4 * lax.axis_index("x") + 2 * lax.axis_index("y") + lax.axis_index("c")

    gath = allgather([W[k].astype(bf16) for k in BIG] + [conv_w], "allgather_weights")
    G = dict(zip(BIG + ["conv_w"], gath))

    loss8, dx, dfinal, small_g, big_g = local_step(x, loss_target, W, G)

    recv = exchange_grads([big_g[k] for k in BIG])
    out = {}
    for k, r in zip(BIG, recv):
        shp = W[k].shape
        r4 = r.reshape(DEPTH, NDEV, shp[1], shp[2])
        out[k] = adam_big(r4, W[k], M[k], V[k], "adam_" + k)

    names = [k for k in SMALL if k != "final_g"]
    shapes = [(DEPTH, NDEV, CW, BW // NDEV) if k == "conv_w" else W[k].shape for k in names] + [(D,), (1,)]
    gpack = _pack([jnp.stack([g.reshape(shapes[i][1:]) for g in small_g[k]]) for i, k in enumerate(names)]
                  + [dfinal, loss8[0, :1]])

    def wpack(src):
        parts = [jnp.broadcast_to(src[k][:, None], shapes[i]) if k == "conv_w" else src[k] for i, k in enumerate(names)]
        return _pack(parts + [src["final_g"], jnp.ones((1,), f32)])

    (gall,) = allgather([gpack[None]], "allgather_small")
    sg, sd, sm, sv = [_unpack(p, shapes) for p in adam_small(gall[0], wpack(W), wpack(M), wpack(V))]
    for i, k in enumerate(names + ["final_g"]):
        vals = [t[i] for t in (sg, sd, sm, sv)]
        if k == "conv_w":
            vals = [lax.dynamic_index_in_dim(t, me, axis=1, keepdims=False) for t in vals]
        out[k] = vals
    loss = sg[-1].reshape(())

    res = [loss, dx.reshape(bsz, seq, D)]
    for j in range(4):
        res += [out[k][j] for k in ORDER]
    return tuple(res)
```

```python
import functools
import math

import jax
import jax.numpy as jnp
from jax import lax
from jax.experimental import pallas as pl
from jax.experimental.pallas import tpu as pltpu

f32 = jnp.float32
bf16 = jnp.bfloat16

D = 1024
DEPTH = 2
EPS = 1e-6
BW = 512
NG = 32
GH = 16
NS = 64
NSTATE = NG * NS
HD = 64
NH = 8
PATTERNS = ((128, 1), (512, 4), (2048, 16))
ABLK = 128
ATT_SCALE = HD ** -0.5
CW = 31
DFF = 2816
INC = 7168
NDEV = 8
NSH_IN = INC // NDEV
NSH_FF = 2 * DFF // NDEV
ADAM_LR, ADAM_B1, ADAM_B2, ADAM_EPS, ADAM_WD, ADAM_STEP = 0.001, 0.9, 0.999, 1e-08, 0.01, 10

TB = 512
SJ = 4
SW = NSTATE // SJ
SU = BW // SJ
NEG = -1e30
MESH = pl.DeviceIdType.MESH
ANY = pl.BlockSpec(memory_space=pl.ANY)


def _cp(n_axes, vmem_mb=48):
    return pltpu.CompilerParams(dimension_semantics=("arbitrary",) * n_axes,
                                vmem_limit_bytes=vmem_mb * 1024 * 1024)


def _dot(a, b):
    return jnp.dot(a, b, preferred_element_type=f32)


def _dot_nt(a, b):
    return lax.dot_general(a, b, (((1,), (1,)), ((), ())), preferred_element_type=f32)


def _dot_tn(a, b):
    return lax.dot_general(a, b, (((0,), (0,)), ((), ())), preferred_element_type=f32)


def _dot_hi(a, b):
    return jnp.dot(a, b, precision=lax.Precision.HIGHEST, preferred_element_type=f32)


def _sigmoid(x):
    return 1.0 / (1.0 + jnp.exp(-x))


_GC = math.sqrt(2.0 / math.pi)


def _gelu(x):
    return 0.5 * x * (1.0 + jnp.tanh(_GC * (x + 0.044715 * x * x * x)))


def _gelu_grad(x):
    t = jnp.tanh(_GC * (x + 0.044715 * x * x * x))
    return 0.5 * (1.0 + t) + 0.5 * x * (1.0 - t * t) * _GC * (1.0 + 3.0 * 0.044715 * x * x)


def _rms_stats(x):
    return lax.rsqrt(jnp.mean(x * x, axis=-1, keepdims=True) + EPS)


def _rms_bwd(x, g, dh):
    r = _rms_stats(x)
    dyg = dh * g
    dx = r * dyg - x * (r * r * r) * jnp.mean(dyg * x, axis=-1, keepdims=True)
    dg = jnp.sum(dh * x * r, axis=0, keepdims=True)
    return dx, dg


def rms_fwd(x, g):
    n = x.shape[0]

    def body(x_ref, g_ref, h_ref):
        xv = x_ref[...]
        h_ref[...] = (xv * _rms_stats(xv) * g_ref[...]).astype(bf16)

    return pl.pallas_call(
        body, name="rms_fwd", grid=(n // TB,),
        in_specs=[pl.BlockSpec((TB, D), lambda i: (i, 0)), pl.BlockSpec((1, D), lambda i: (0, 0))],
        out_specs=pl.BlockSpec((TB, D), lambda i: (i, 0)),
        out_shape=jax.ShapeDtypeStruct((n, D), bf16), compiler_params=_cp(1))(x, g)


def inproj(h, w4, layer):
    n = h.shape[0]
    tm = 1024

    def body(h_ref, w_ref, o_ref):
        o_ref[...] = _dot(h_ref[...], w_ref[0, 0]).astype(bf16)

    return pl.pallas_call(
        body, name="inproj", grid=(NDEV, n // tm),
        in_specs=[pl.BlockSpec((tm, D), lambda s, i: (i, 0)),
                  pl.BlockSpec((1, 1, D, NSH_IN), lambda s, i: (layer, s, 0, 0))],
        out_specs=pl.BlockSpec((tm, NSH_IN), lambda s, i: (i, s)),
        out_shape=jax.ShapeDtypeStruct((n, INC), bf16), compiler_params=_cp(2))(h, w4)


def inproj_bwd(dproj, w4, layer, x, g, dres):
    n = x.shape[0]
    tm = 1024

    def body(dp_ref, w_ref, x_ref, g_ref, dr_ref, dx_ref, dg_ref, acc):
        i, s = pl.program_id(0), pl.program_id(1)

        @pl.when(s == 0)
        def _():
            acc[...] = jnp.zeros_like(acc)

        @pl.when((s == 0) & (i == 0))
        def _():
            dg_ref[...] = jnp.zeros_like(dg_ref)

        acc[...] += _dot_nt(dp_ref[...], w_ref[0, 0])

        @pl.when(s == NDEV - 1)
        def _():
            dx, dg = _rms_bwd(x_ref[...], g_ref[...], acc[...])
            dx_ref[...] = dr_ref[...] + dx
            dg_ref[...] += dg

    return pl.pallas_call(
        body, name="inproj_bwd", grid=(n // tm, NDEV),
        in_specs=[pl.BlockSpec((tm, NSH_IN), lambda i, s: (i, s)),
                  pl.BlockSpec((1, 1, D, NSH_IN), lambda i, s: (layer, s, 0, 0)),
                  pl.BlockSpec((tm, D), lambda i, s: (i, 0)),
                  pl.BlockSpec((1, D), lambda i, s: (0, 0)),
                  pl.BlockSpec((tm, D), lambda i, s: (i, 0))],
        out_specs=[pl.BlockSpec((tm, D), lambda i, s: (i, 0)), pl.BlockSpec((1, D), lambda i, s: (0, 0))],
        out_shape=[jax.ShapeDtypeStruct((n, D), f32), jax.ShapeDtypeStruct((1, D), f32)],
        scratch_shapes=[pltpu.VMEM((tm, D), f32)], compiler_params=_cp(2))(dproj, w4, x, g, dres)


def mm_tn(a, b, a_spec, b_spec, n_sh, ka, nb, m, name):
    tm = 1024

    def body(a_ref, b_ref, o_ref, acc):
        i = pl.program_id(1)

        @pl.when(i == 0)
        def _():
            acc[...] = jnp.zeros_like(acc)

        av = a_ref[...].reshape(tm, ka).astype(bf16)
        bv = b_ref[...].reshape(tm, nb).astype(bf16)
        acc[...] += _dot_tn(av, bv)

        @pl.when(i == m // tm - 1)
        def _():
            o_ref[0] = acc[...].astype(bf16)

    return pl.pallas_call(
        body, name=name, grid=(n_sh, m // tm), in_specs=[a_spec, b_spec],
        out_specs=pl.BlockSpec((1, ka, nb), lambda s, i: (s, 0, 0)),
        out_shape=jax.ShapeDtypeStruct((n_sh, ka, nb), bf16),
        scratch_shapes=[pltpu.VMEM((ka, nb), f32)], compiler_params=_cp(2))(a, b)


def ffn_fwd(x1, g2, w1, w2, layer):
    n = x1.shape[0]
    w2p = w2.reshape(DEPTH, 4, NSH_FF, D)

    def body(x_ref, g_ref, wa_ref, wb_ref, w2_ref, o_ref, h_sc):
        s = pl.program_id(1)

        @pl.when(s == 0)
        def _():
            xv = x_ref[...]
            h_sc[...] = (xv * _rms_stats(xv) * g_ref[...]).astype(bf16)
            o_ref[...] = xv

        h = h_sc[...]
        z1 = _dot(h, wa_ref[0, 0])
        z2 = _dot(h, wb_ref[0, 0])
        a = (z1 * _sigmoid(z1) * z2).astype(bf16)
        o_ref[...] += _dot(a, w2_ref[0, 0])

    return pl.pallas_call(
        body, name="ffn_fwd", grid=(n // TB, 4),
        in_specs=[pl.BlockSpec((TB, D), lambda i, s: (i, 0)),
                  pl.BlockSpec((1, D), lambda i, s: (0, 0)),
                  pl.BlockSpec((1, 1, D, NSH_FF), lambda i, s: (layer, s, 0, 0)),
                  pl.BlockSpec((1, 1, D, NSH_FF), lambda i, s: (layer, s + 4, 0, 0)),
                  pl.BlockSpec((1, 1, NSH_FF, D), lambda i, s: (layer, s, 0, 0))],
        out_specs=pl.BlockSpec((TB, D), lambda i, s: (i, 0)),
        out_shape=jax.ShapeDtypeStruct((n, D), f32),
        scratch_shapes=[pltpu.VMEM((TB, D), bf16)], compiler_params=_cp(2))(x1, g2, w1, w1, w2p)


def ffn_bwd(x1, g2, w1, w2, layer, dx2):
    n = x1.shape[0]
    w2p = w2.reshape(DEPTH, 4, NSH_FF, D)

    def body(x_ref, g_ref, dy_ref, wa_ref, wb_ref, w2_ref,
             dx_ref, h_ref, dz1_ref, dz2_ref, a_ref, dg_ref, dh_sc, dyb_sc):
        i, s = pl.program_id(0), pl.program_id(1)

        @pl.when(s == 0)
        def _():
            xv = x_ref[...]
            h_ref[...] = (xv * _rms_stats(xv) * g_ref[...]).astype(bf16)
            dh_sc[...] = jnp.zeros_like(dh_sc)
            dyb_sc[...] = dy_ref[...].astype(bf16)

        @pl.when((s == 0) & (i == 0))
        def _():
            dg_ref[...] = jnp.zeros_like(dg_ref)

        h = h_ref[...]
        z1 = _dot(h, wa_ref[0, 0])
        z2 = _dot(h, wb_ref[0, 0])
        sg = _sigmoid(z1)
        sl = z1 * sg
        a_ref[0] = (sl * z2).astype(bf16)
        da = _dot_nt(dyb_sc[...], w2_ref[0, 0])
        dz2 = (da * sl).astype(bf16)
        dz1 = (da * z2 * sg * (1.0 + z1 * (1.0 - sg))).astype(bf16)
        dz1_ref[0] = dz1
        dz2_ref[0] = dz2
        dh_sc[...] += _dot_nt(dz1, wa_ref[0, 0]) + _dot_nt(dz2, wb_ref[0, 0])

        @pl.when(s == 3)
        def _():
            dx, dg = _rms_bwd(x_ref[...], g_ref[...], dh_sc[...])
            dx_ref[...] = dy_ref[...] + dx
            dg_ref[...] += dg

    tok = lambda i, s: (i, 0)
    sh3 = lambda i, s: (s, i, 0)
    return pl.pallas_call(
        body, name="ffn_bwd", grid=(n // TB, 4),
        in_specs=[pl.BlockSpec((TB, D), tok), pl.BlockSpec((1, D), lambda i, s: (0, 0)), pl.BlockSpec((TB, D), tok),
                  pl.BlockSpec((1, 1, D, NSH_FF), lambda i, s: (layer, s, 0, 0)),
                  pl.BlockSpec((1, 1, D, NSH_FF), lambda i, s: (layer, s + 4, 0, 0)),
                  pl.BlockSpec((1, 1, NSH_FF, D), lambda i, s: (layer, s, 0, 0))],
        out_specs=[pl.BlockSpec((TB, D), tok), pl.BlockSpec((TB, D), tok),
                   pl.BlockSpec((1, TB, NSH_FF), sh3), pl.BlockSpec((1, TB, NSH_FF), sh3),
                   pl.BlockSpec((1, TB, NSH_FF), sh3), pl.BlockSpec((1, D), lambda i, s: (0, 0))],
        out_shape=[jax.ShapeDtypeStruct((n, D), f32), jax.ShapeDtypeStruct((n, D), bf16),
                   jax.ShapeDtypeStruct((4, n, NSH_FF), bf16), jax.ShapeDtypeStruct((4, n, NSH_FF), bf16),
                   jax.ShapeDtypeStruct((4, n, NSH_FF), bf16), jax.ShapeDtypeStruct((1, D), f32)],
        scratch_shapes=[pltpu.VMEM((TB, D), f32), pltpu.VMEM((TB, D), bf16)],
        compiler_params=_cp(2))(x1, g2, dx2, w1, w1, w2p)


def loss_head(x, g, target):
    n = x.shape[0]

    def body(x_ref, g_ref, t_ref, l_ref, dx_ref, dg_ref):
        i = pl.program_id(0)

        @pl.when(i == 0)
        def _():
            l_ref[...] = jnp.zeros_like(l_ref)
            dg_ref[...] = jnp.zeros_like(dg_ref)

        xv = x_ref[...]
        y = xv * _rms_stats(xv) * g_ref[...]
        e = y - t_ref[...]
        l_ref[...] += 0.5 * jnp.sum(jnp.sum(e * e, axis=-1, keepdims=True), axis=0, keepdims=True) * (1.0 / D)
        dx, dg = _rms_bwd(xv, g_ref[...], e * (1.0 / D))
        dx_ref[...] = dx
        dg_ref[...] += dg

    tok = lambda i: (i, 0)
    return pl.pallas_call(
        body, name="loss_head", grid=(n // TB,),
        in_specs=[pl.BlockSpec((TB, D), tok), pl.BlockSpec((1, D), lambda i: (0, 0)), pl.BlockSpec((TB, D), tok)],
        out_specs=[pl.BlockSpec((8, 128), lambda i: (0, 0)), pl.BlockSpec((TB, D), tok),
                   pl.BlockSpec((1, D), lambda i: (0, 0))],
        out_shape=[jax.ShapeDtypeStruct((8, 128), f32), jax.ShapeDtypeStruct((n, D), f32),
                   jax.ShapeDtypeStruct((1, D), f32)],
        compiler_params=_cp(1))(x, g, target)


def _disc(lr, li, ld):
    dt = jnp.exp(ld)
    mag = jnp.exp(lr * dt)
    ar = mag * jnp.cos(li * dt)
    ai = mag * jnp.sin(li * dt)
    nr, ni = ar - 1.0, ai
    den = lr * lr + li * li
    zr = (nr * lr + ni * li) / den
    zi = (ni * lr - nr * li) / den
    return ar, ai, zr, zi


def _blockdiag_mask(shape):
    r = lax.broadcasted_iota(jnp.int32, shape, 0) // GH
    c = lax.broadcasted_iota(jnp.int32, shape, 1) // NS
    return r == c


def s5_params(lr, li, ld, btr, bti, ctr, cti):
    def body(lr_ref, li_ref, ld_ref, btr_ref, bti_ref, ctr_ref, cti_ref, t8_ref, bb_ref, cb_ref):
        ar, ai, zr, zi = _disc(lr_ref[...], li_ref[...], ld_ref[...])
        pr_, pi_ = ar, ai
        pw2 = []
        for k in range(4):
            pw2.append((pr_, pi_))
            pr_, pi_ = pr_ * pr_ - pi_ * pi_, 2.0 * pr_ * pi_
        cm = lambda p, q: (p[0] * q[0] - p[1] * q[1], p[0] * q[1] + p[1] * q[0])
        pw = {1: pw2[0], 2: pw2[1], 4: pw2[2], 8: pw2[3]}
        pw[3], pw[5], pw[6] = cm(pw[2], pw[1]), cm(pw[4], pw[1]), cm(pw[4], pw[2])
        pw[7] = cm(pw[4], pw[3])
        row = lax.broadcasted_iota(jnp.int32, (8, NSTATE), 0)
        zero = jnp.zeros((8, NSTATE), f32)
        for c in range(2):
            for k in range(3):
                full = jnp.broadcast_to(pw2[k][c], (8, NSTATE))
                t8_ref[c, k] = jnp.where(row >= (1 << k), full, 0.0)
                t8_ref[c, 3 + k] = jnp.where(row + (1 << k) < 8, full, 0.0)
            up, down = zero, zero
            for j in range(8):
                up = up + jnp.where(row == j, pw[j + 1][c], 0.0)
                down = down + jnp.where(row == j, pw[8 - j][c], 0.0)
            t8_ref[c, 6] = up
            t8_ref[c, 7] = down
        bbr = zr * btr_ref[...] - zi * bti_ref[...]
        bbi = zr * bti_ref[...] + zi * btr_ref[...]
        mask = _blockdiag_mask((SU, SW))
        for j in range(SJ):
            cols = slice(j * SW, (j + 1) * SW)
            for c, (vb, vc) in enumerate(((bbr, ctr_ref[...]), (bbi, cti_ref[...]))):
                bb_ref[c, j] = jnp.where(mask, jnp.tile(vb[:, cols], (SU // GH, 1)), 0.0).astype(bf16)
                cb_ref[c, j] = jnp.where(mask, jnp.tile(vc[:, cols], (SU // GH, 1)), 0.0).astype(bf16)

    return pl.pallas_call(
        body, name="s5_params",
        out_shape=[jax.ShapeDtypeStruct((2, 8, 8, NSTATE), f32),
                   jax.ShapeDtypeStruct((2, SJ, SU, SW), bf16), jax.ShapeDtypeStruct((2, SJ, SU, SW), bf16)],
        compiler_params=pltpu.CompilerParams(vmem_limit_bytes=56 * 1024 * 1024))(lr, li, ld, btr, bti, ctr, cti)


def s5_params_bwd(lr, li, ld, btr, bti, d_a, d_bb, d_cb):
    def body(lr_ref, li_ref, ld_ref, btr_ref, bti_ref, da_ref, dbb_ref, dcb_ref,
             dlr_ref, dli_ref, dld_ref, dbt_ref, dct_ref):
        mask = _blockdiag_mask((SU, SW))

        def fold(ref, c):
            parts = []
            for j in range(SJ):
                v = jnp.where(mask, ref[c, j], 0.0)
                parts.append(v.reshape(SU // GH, GH, SW).sum(axis=0))
            return jnp.concatenate(parts, axis=1)

        dct_ref[0] = fold(dcb_ref, 0)
        dct_ref[1] = fold(dcb_ref, 1)
        dbbr, dbbi = fold(dbb_ref, 0), fold(dbb_ref, 1)
        lrv, liv, ldv = lr_ref[...], li_ref[...], ld_ref[...]
        (ar, ai, zr, zi), vjp = jax.vjp(_disc, lrv, liv, ldv)
        btr, bti = btr_ref[...], bti_ref[...]
        dbt_ref[0] = zr * dbbr + zi * dbbi
        dbt_ref[1] = zr * dbbi - zi * dbbr
        dzr = jnp.sum(dbbr * btr + dbbi * bti, axis=0, keepdims=True)
        dzi = jnp.sum(dbbi * btr - dbbr * bti, axis=0, keepdims=True)
        dlr, dli, dld = vjp((da_ref[0:1, :], da_ref[1:2, :], dzr, dzi))
        dlr_ref[...] = dlr
        dli_ref[...] = dli
        ind = (lax.broadcasted_iota(jnp.int32, (NSTATE, 128), 0) // NS
               == lax.broadcasted_iota(jnp.int32, (NSTATE, 128), 1)).astype(f32)
        dld_ref[...] = _dot_hi(jnp.broadcast_to(dld, (8, NSTATE)), ind)

    return pl.pallas_call(
        body, name="s5_params_bwd",
        out_shape=[jax.ShapeDtypeStruct((1, NSTATE), f32), jax.ShapeDtypeStruct((1, NSTATE), f32),
                   jax.ShapeDtypeStruct((8, 128), f32), jax.ShapeDtypeStruct((2, GH, NSTATE), f32),
                   jax.ShapeDtypeStruct((2, GH, NSTATE), f32)],
        compiler_params=pltpu.CompilerParams(vmem_limit_bytes=56 * 1024 * 1024))(lr, li, ld, btr, bti, d_a, d_bb, d_cb)


def _fma(sr, si, ar, ai, qr, qi):
    return sr + ar * qr - ai * qi, si + ar * qi + ai * qr


def _scan_tile(sr, si, cr, ci, t8_ref, reverse):
    sg = -1.0 if reverse else 1.0
    for k in range(3):
        tk = 3 + k if reverse else k
        rot = 8 - (1 << k) if reverse else 1 << k
        sr, si = _fma(sr, si, t8_ref[0, tk], sg * t8_ref[1, tk], pltpu.roll(sr, rot, 0), pltpu.roll(si, rot, 0))
    tp = 7 if reverse else 6
    sr, si = _fma(sr, si, t8_ref[0, tp], sg * t8_ref[1, tp], cr, ci)
    e = 0 if reverse else 7
    return sr, si, jnp.broadcast_to(sr[e:e + 1, :], sr.shape), jnp.broadcast_to(si[e:e + 1, :], si.shape)


S5MC = 512


def _s5_input_map(u_ref, bb_ref, sr_sc, si_sc, l):
    for c in range(l // S5MC):
        rows = slice(c * S5MC, (c + 1) * S5MC)
        u = u_ref[0, rows, :]
        sr_sc[rows, :] = _dot(u, bb_ref[0, 0])
        si_sc[rows, :] = _dot(u, bb_ref[1, 0])


def _s5_forward_scan(sr_sc, si_sc, t8_ref, l):
    def step(k, carry):
        rows = pl.ds(pl.multiple_of(k * 8, 8), 8)
        sr, si, cr, ci = _scan_tile(sr_sc[rows, :], si_sc[rows, :], carry[0], carry[1], t8_ref, False)
        sr_sc[rows, :] = sr
        si_sc[rows, :] = si
        return cr, ci

    zero = jnp.zeros((8, SW), f32)
    lax.fori_loop(0, l // 8, step, (zero, zero), unroll=4)


def s5_fwd(proj3, t8, bb, cb, dskip):
    b, l, _ = proj3.shape

    def body(u_ref, t8_ref, bb_ref, cb_ref, d_ref, y_ref, sr_sc, si_sc):
        _s5_input_map(u_ref, bb_ref, sr_sc, si_sc, l)
        _s5_forward_scan(sr_sc, si_sc, t8_ref, l)
        for c in range(l // S5MC):
            rows = slice(c * S5MC, (c + 1) * S5MC)
            y = (_dot_nt(sr_sc[rows, :].astype(bf16), cb_ref[0, 0])
                 - _dot_nt(si_sc[rows, :].astype(bf16), cb_ref[1, 0]))
            y_ref[0, rows, :] = y + d_ref[...] * u_ref[0, rows, :].astype(f32)

    return pl.pallas_call(
        body, name="s5_fwd", grid=(SJ, b),
        in_specs=[pl.BlockSpec((1, l, SU), lambda j, bi: (bi, 0, j)),
                  pl.BlockSpec((2, 8, 8, SW), lambda j, bi: (0, 0, 0, j)),
                  pl.BlockSpec((2, 1, SU, SW), lambda j, bi: (0, j, 0, 0)),
                  pl.BlockSpec((2, 1, SU, SW), lambda j, bi: (0, j, 0, 0)),
                  pl.BlockSpec((1, SU), lambda j, bi: (0, j))],
        out_specs=pl.BlockSpec((1, l, SU), lambda j, bi: (bi, 0, j)),
        out_shape=jax.ShapeDtypeStruct((b, l, BW), f32),
        scratch_shapes=[pltpu.VMEM((l, SW), f32)] * 2, compiler_params=_cp(2))(proj3, t8, bb, cb, dskip)


def s5_bwd(proj3, dy, t8, bb, cb, dskip):
    b, l, _ = proj3.shape
    nt = l // 8

    def body(u_ref, dy_ref, t8_ref, bb_ref, cb_ref, d_ref,
             du_ref, da_ref, dbb_ref, dcb_ref, dd_ref, sr_sc, si_sc, gr_sc, gi_sc):
        bi = pl.program_id(1)

        @pl.when(bi == 0)
        def _():
            da_ref[...] = jnp.zeros_like(da_ref)
            dbb_ref[...] = jnp.zeros_like(dbb_ref)
            dcb_ref[...] = jnp.zeros_like(dcb_ref)
            dd_ref[...] = jnp.zeros_like(dd_ref)

        _s5_input_map(u_ref, bb_ref, sr_sc, si_sc, l)
        _s5_forward_scan(sr_sc, si_sc, t8_ref, l)
        for c in range(l // S5MC):
            rows = slice(c * S5MC, (c + 1) * S5MC)
            dyb = dy_ref[0, rows, :].astype(bf16)
            gr_sc[rows, :] = _dot(dyb, cb_ref[0, 0])
            gi_sc[rows, :] = -_dot(dyb, cb_ref[1, 0])

        row = lax.broadcasted_iota(jnp.int32, (8, SW), 0)

        def step(i, carry):
            cr, ci, dar, dai = carry
            k = nt - 1 - i
            rows = pl.ds(pl.multiple_of(k * 8, 8), 8)
            gr, gi, cr, ci = _scan_tile(gr_sc[rows, :], gi_sc[rows, :], cr, ci, t8_ref, True)
            gr_sc[rows, :] = gr
            gi_sc[rows, :] = gi
            before = pl.ds(pl.multiple_of(jnp.maximum(k - 1, 0) * 8, 8), 8)
            live = jnp.where(k > 0, 1.0, 0.0)
            sr, si = sr_sc[rows, :], si_sc[rows, :]
            spr = jnp.where(row == 0, live * sr_sc[before, :][7:8, :], pltpu.roll(sr, 1, 0))
            spi = jnp.where(row == 0, live * si_sc[before, :][7:8, :], pltpu.roll(si, 1, 0))
            return cr, ci, dar + spr * gr + spi * gi, dai + spr * gi - spi * gr

        zero = jnp.zeros((8, SW), f32)
        _, _, dar, dai = lax.fori_loop(0, nt, step, (zero, zero, zero, zero), unroll=2)
        da_ref[0:1, :] += jnp.sum(dar, axis=0, keepdims=True)
        da_ref[1:2, :] += jnp.sum(dai, axis=0, keepdims=True)

        for c in range(l // S5MC):
            rows = slice(c * S5MC, (c + 1) * S5MC)
            u = u_ref[0, rows, :]
            dyv = dy_ref[0, rows, :]
            dyb = dyv.astype(bf16)
            grb, gib = gr_sc[rows, :].astype(bf16), gi_sc[rows, :].astype(bf16)
            dcb_ref[0, 0] += _dot_tn(dyb, sr_sc[rows, :].astype(bf16))
            dcb_ref[1, 0] -= _dot_tn(dyb, si_sc[rows, :].astype(bf16))
            dbb_ref[0, 0] += _dot_tn(u, grb)
            dbb_ref[1, 0] += _dot_tn(u, gib)
            du = _dot_nt(grb, bb_ref[0, 0]) + _dot_nt(gib, bb_ref[1, 0]) + d_ref[...] * dyv
            du_ref[0, rows, :] = du.astype(bf16)
            dd_ref[...] += jnp.sum(dyv * u.astype(f32), axis=0, keepdims=True)

    seq = pl.BlockSpec((1, l, SU), lambda j, bi: (bi, 0, j))
    tab = pl.BlockSpec((2, 1, SU, SW), lambda j, bi: (0, j, 0, 0))
    return pl.pallas_call(
        body, name="s5_bwd", grid=(SJ, b),
        in_specs=[seq, seq, pl.BlockSpec((2, 8, 8, SW), lambda j, bi: (0, 0, 0, j)), tab, tab,
                  pl.BlockSpec((1, SU), lambda j, bi: (0, j))],
        out_specs=[seq, pl.BlockSpec((2, SW), lambda j, bi: (0, j)), tab, tab,
                   pl.BlockSpec((1, SU), lambda j, bi: (0, j))],
        out_shape=[jax.ShapeDtypeStruct((b, l, BW), bf16), jax.ShapeDtypeStruct((2, NSTATE), f32),
                   jax.ShapeDtypeStruct((2, SJ, SU, SW), f32), jax.ShapeDtypeStruct((2, SJ, SU, SW), f32),
                   jax.ShapeDtypeStruct((1, BW), f32)],
        scratch_shapes=[pltpu.VMEM((l, SW), f32)] * 4,
        compiler_params=_cp(2))(proj3, dy, t8, bb, cb, dskip)


AHC = 2
AHW = AHC * 128


def _att_mask(n, nb):
    if nb == 1:
        qi = lax.broadcasted_iota(jnp.int32, (ABLK, ABLK), 0)
        kj = lax.broadcasted_iota(jnp.int32, (ABLK, ABLK), 1)
        return kj <= qi
    qi = lax.broadcasted_iota(jnp.int32, (ABLK, 2 * ABLK), 0)
    kj = lax.broadcasted_iota(jnp.int32, (ABLK, 2 * ABLK), 1)
    return (kj >= qi) & (kj <= qi + ABLK) & ((n > 0) | (kj >= ABLK))


def _att_rows(it, nb, dil):
    r, n = it // nb, it % nb
    cur = pl.ds(r + n * (ABLK * dil), ABLK, stride=dil)
    prv = pl.ds(r + jnp.maximum(n - 1, 0) * (ABLK * dil), ABLK, stride=dil)
    return n, cur, prv


def _att_keys(ref, c, cur, prv, nb):
    if nb == 1:
        return ref[c, cur, :].astype(bf16)
    return jnp.concatenate([ref[c, prv, :], ref[c, cur, :]], axis=0).astype(bf16)


def _to_chunks(src_ref, dst):
    for c in range(AHC):
        dst[c] = src_ref[0, :, c * 128:(c + 1) * 128].astype(f32)


def att_fwd(proj3, g_idx, dil):
    b, l, _ = proj3.shape
    nb = l // dil // ABLK
    nhalf = BW // AHW

    def body(q_ref, k_ref, v_ref, o_ref, lse_ref, qf, kf, vf, of):
        hh = pl.program_id(1)
        _to_chunks(q_ref, qf)
        _to_chunks(k_ref, kf)
        _to_chunks(v_ref, vf)
        lane = lax.broadcasted_iota(jnp.int32, (ABLK, 128), 1)

        def step(it, carry):
            n, cur, prv = _att_rows(it, nb, dil)
            valid = _att_mask(n, nb)
            lse_all = jnp.zeros((ABLK, 128), f32)
            for c in range(AHC):
                q = qf[c, cur, :].astype(bf16)
                k = _att_keys(kf, c, cur, prv, nb)
                v = _att_keys(vf, c, cur, prv, nb)
                outs = []
                for hl in range(2):
                    hs = slice(hl * HD, (hl + 1) * HD)
                    s = jnp.where(valid, _dot_nt(q[:, hs], k[:, hs]) * ATT_SCALE, NEG)
                    m = jnp.max(s, axis=-1, keepdims=True)
                    p = jnp.exp(s - m)
                    den = jnp.sum(p, axis=-1, keepdims=True)
                    outs.append(_dot(p.astype(bf16), v[:, hs]) / den)
                    lse_all = lse_all + jnp.where(lane == hh * (2 * AHC) + 2 * c + hl, m + jnp.log(den), 0.0)
                of[c, cur, :] = jnp.concatenate(outs, axis=1)

            @pl.when(hh == 0)
            def _():
                lse_ref[0, cur, :] = lse_all

            @pl.when(hh > 0)
            def _():
                lse_ref[0, cur, :] += lse_all

            return carry

        lax.fori_loop(0, dil * nb, step, 0, unroll=2)
        for c in range(AHC):
            o_ref[0, :, c * 128:(c + 1) * 128] = of[c].astype(bf16)

    col = lambda c: pl.BlockSpec((1, l, AHW), lambda bi, hh: (bi, 0, c * nhalf + hh))
    return pl.pallas_call(
        body, name=f"att_fwd{g_idx}", grid=(b, nhalf),
        in_specs=[col(1 + g_idx), col(4), col(5)],
        out_specs=[pl.BlockSpec((1, l, AHW), lambda bi, hh: (bi, 0, hh)),
                   pl.BlockSpec((1, l, 128), lambda bi, hh: (bi, 0, 0))],
        out_shape=[jax.ShapeDtypeStruct((b, l, BW), bf16), jax.ShapeDtypeStruct((b, l, 128), f32)],
        scratch_shapes=[pltpu.VMEM((AHC, l, 128), f32)] * 4,
        compiler_params=_cp(2))(proj3, proj3, proj3)


def att_bwd(proj3, do, lse_tot, delta, g_idx, dil):
    b, l, _ = proj3.shape
    nb = l // dil // ABLK
    nhalf = BW // AHW

    def body(q_ref, k_ref, v_ref, do_ref, l_ref, dl_ref, dq_ref, dk_ref, dv_ref, qf, kf, vf, dof):
        hh = pl.program_id(1)
        _to_chunks(q_ref, qf)
        _to_chunks(k_ref, kf)
        _to_chunks(v_ref, vf)
        _to_chunks(do_ref, dof)
        dk_ref[...] = jnp.zeros_like(dk_ref)
        dv_ref[...] = jnp.zeros_like(dv_ref)
        lane = lax.broadcasted_iota(jnp.int32, (ABLK, 128), 1)

        def step(it, carry):
            n, cur, prv = _att_rows(it, nb, dil)
            valid = _att_mask(n, nb)
            lse_b = l_ref[0, cur, :]
            dl_b = dl_ref[0, cur, :]
            for c in range(AHC):
                q = qf[c, cur, :].astype(bf16)
                dob = dof[c, cur, :].astype(bf16)
                k = _att_keys(kf, c, cur, prv, nb)
                v = _att_keys(vf, c, cur, prv, nb)
                dqs, dks, dvs = [], [], []
                for hl in range(2):
                    hs = slice(hl * HD, (hl + 1) * HD)
                    mine = lane == hh * (2 * AHC) + 2 * c + hl
                    lse_h = jnp.sum(jnp.where(mine, lse_b, 0.0), axis=-1, keepdims=True)
                    dl_h = jnp.sum(jnp.where(mine, dl_b, 0.0), axis=-1, keepdims=True)
                    s = _dot_nt(q[:, hs], k[:, hs]) * ATT_SCALE
                    p = jnp.where(valid, jnp.exp(jnp.minimum(s - lse_h, 60.0)), 0.0)
                    dp = _dot_nt(dob[:, hs], v[:, hs])
                    ds = (p * (dp - dl_h) * ATT_SCALE).astype(bf16)
                    dqs.append(_dot(ds, k[:, hs]))
                    dks.append(_dot_tn(ds, q[:, hs]))
                    dvs.append(_dot_tn(p.astype(bf16), dob[:, hs]))
                dq_ref[0, c, cur, :] = jnp.concatenate(dqs, axis=1)
                dk = jnp.concatenate(dks, axis=1)
                dv = jnp.concatenate(dvs, axis=1)
                if nb == 1:
                    dk_ref[0, c, cur, :] += dk
                    dv_ref[0, c, cur, :] += dv
                else:
                    dk_ref[0, c, cur, :] += dk[ABLK:]
                    dv_ref[0, c, cur, :] += dv[ABLK:]

                    @pl.when(n > 0)
                    def _():
                        dk_ref[0, c, prv, :] += dk[:ABLK]
                        dv_ref[0, c, prv, :] += dv[:ABLK]

            return carry

        lax.fori_loop(0, dil * nb, step, 0, unroll=2)

    col = lambda c: pl.BlockSpec((1, l, AHW), lambda bi, hh: (bi, 0, c * nhalf + hh))
    own = pl.BlockSpec((1, l, AHW), lambda bi, hh: (bi, 0, hh))
    own128 = pl.BlockSpec((1, l, 128), lambda bi, hh: (bi, 0, 0))
    chunked = pl.BlockSpec((1, AHC, l, 128), lambda bi, hh: (bi, hh, 0, 0))
    return pl.pallas_call(
        body, name=f"att_bwd{g_idx}", grid=(b, nhalf),
        in_specs=[col(1 + g_idx), col(4), col(5), own, own128, own128],
        out_specs=[chunked] * 3,
        out_shape=[jax.ShapeDtypeStruct((b, BW // 128, l, 128), f32)] * 3,
        scratch_shapes=[pltpu.VMEM((AHC, l, 128), f32)] * 4,
        compiler_params=_cp(2, 56))(proj3, proj3, proj3, do, lse_tot, delta)


CPAD = 32
CTAIL = 16
CR = 128
CSLAB = CR + 40


def _tap_windows(slab, off, mis):
    ntap = (CW - 1 - mis) // 8 + 1
    xb = slab[off + mis:off + mis + CR + 8 * (ntap - 1)]
    for a in range(ntap):
        yield 8 * a + mis, xb[8 * a:8 * a + CR]


def _fill_glu(cv_ref, pad, l):
    pad[0:CPAD, :] = jnp.zeros((CPAD, BW), f32)
    pad[CPAD:CPAD + l, :] = cv_ref[0, :, :BW].astype(f32) * _sigmoid(cv_ref[0, :, BW:].astype(f32))
    pad[CPAD + l:, :] = jnp.zeros((CTAIL, BW), f32)


def conv_fwd(proj3, cw, cb):
    b, l, _ = proj3.shape

    def body(cv_ref, w_ref, b_ref, o_ref, pad):
        _fill_glu(cv_ref, pad, l)
        for lc in range(BW // 128):
            lanes = slice(lc * 128, (lc + 1) * 128)
            wv = w_ref[:, lanes]

            def step(c, carry):
                base = pl.multiple_of(c * CR, CR)
                slab = pad[pl.ds(base, CSLAB), lanes]
                acc = jnp.zeros((CR, 128), f32) + b_ref[:, lanes]
                for mis in range(8):
                    for k, win in _tap_windows(slab, CPAD - (CW - 1), mis):
                        acc = acc + wv[k:k + 1] * win
                o_ref[0, pl.ds(base, CR), lanes] = acc
                return carry

            lax.fori_loop(0, l // CR, step, 0)

    return pl.pallas_call(
        body, name="conv_fwd", grid=(b,),
        in_specs=[pl.BlockSpec((1, l, 2 * BW), lambda i: (i, 0, 3)),
                  pl.BlockSpec((32, BW), lambda i: (0, 0)), pl.BlockSpec((1, BW), lambda i: (0, 0))],
        out_specs=pl.BlockSpec((1, l, BW), lambda i: (i, 0, 0)),
        out_shape=jax.ShapeDtypeStruct((b, l, BW), f32),
        scratch_shapes=[pltpu.VMEM((CPAD + l + CTAIL, BW), f32)], compiler_params=_cp(1))(proj3, cw, cb)


def conv_bwd(proj3, dhc, cw):
    b, l, _ = proj3.shape

    def body(cv_ref, d_ref, w_ref, dcv_ref, dw_ref, db_ref, pad, dpad):
        i = pl.program_id(0)

        @pl.when(i == 0)
        def _():
            dw_ref[...] = jnp.zeros_like(dw_ref)
            db_ref[...] = jnp.zeros_like(db_ref)

        _fill_glu(cv_ref, pad, l)
        dpad[0:l, :] = d_ref[0]
        dpad[l:, :] = jnp.zeros((CPAD + CTAIL, BW), f32)
        db_ref[...] += jnp.sum(d_ref[0], axis=0, keepdims=True)
        for lc in range(BW // 128):
            lanes = slice(lc * 128, (lc + 1) * 128)
            glanes = slice(BW + lc * 128, BW + (lc + 1) * 128)
            wv = w_ref[:, lanes]

            for mis in range(8):
                ntap = (CW - 1 - mis) // 8 + 1

                def dw_step(c, accs, mis=mis, lanes=lanes):
                    base = pl.multiple_of(c * CR, CR)
                    slab = pad[pl.ds(base, CSLAB), lanes]
                    dv = dpad[pl.ds(base, CR), lanes]
                    return tuple(acc + (dv * win).reshape(CR // 8, 8, 128).sum(axis=0) for acc, (_, win)
                                 in zip(accs, _tap_windows(slab, CPAD - (CW - 1), mis)))

                accs = lax.fori_loop(0, l // CR, dw_step, tuple(jnp.zeros((8, 128), f32) for _ in range(ntap)))
                for a in range(ntap):
                    k = 8 * a + mis
                    dw_ref[k:k + 1, lanes] += jnp.sum(accs[a], axis=0, keepdims=True)

            def dh_step(c, carry, lanes=lanes, glanes=glanes, wv=wv):
                base = pl.multiple_of(c * CR, CR)
                slab = dpad[pl.ds(base, CSLAB), lanes]
                acc = jnp.zeros((CR, 128), f32)
                for mis in range(8):
                    for kk, win in _tap_windows(slab, 0, mis):
                        acc = acc + wv[CW - 1 - kk:CW - kk] * win
                rows = pl.ds(base, CR)
                a = cv_ref[0, rows, lanes].astype(f32)
                sg = _sigmoid(cv_ref[0, rows, glanes].astype(f32))
                dcv_ref[0, rows, lanes] = (acc * sg).astype(bf16)
                dcv_ref[0, rows, glanes] = (acc * a * sg * (1.0 - sg)).astype(bf16)
                return carry

            lax.fori_loop(0, l // CR, dh_step, 0)

    return pl.pallas_call(
        body, name="conv_bwd", grid=(b,),
        in_specs=[pl.BlockSpec((1, l, 2 * BW), lambda i: (i, 0, 3)),
                  pl.BlockSpec((1, l, BW), lambda i: (i, 0, 0)),
                  pl.BlockSpec((32, BW), lambda i: (0, 0))],
        out_specs=[pl.BlockSpec((1, l, 2 * BW), lambda i: (i, 0, 0)),
                   pl.BlockSpec((32, BW), lambda i: (0, 0)), pl.BlockSpec((1, BW), lambda i: (0, 0))],
        out_shape=[jax.ShapeDtypeStruct((b, l, 2 * BW), bf16), jax.ShapeDtypeStruct((32, BW), f32),
                   jax.ShapeDtypeStruct((1, BW), f32)],
        scratch_shapes=[pltpu.VMEM((CPAD + l + CTAIL, BW), f32), pltpu.VMEM((l + CPAD + CTAIL, BW), f32)],
        compiler_params=_cp(1))(proj3, dhc, cw)


def _head_expand():
    r = lax.broadcasted_iota(jnp.int32, (128, BW), 0)
    c = lax.broadcasted_iota(jnp.int32, (128, BW), 1) // HD
    return (r == c).astype(f32)


def _head_reduce():
    r = lax.broadcasted_iota(jnp.int32, (BW, 128), 0) // HD
    c = lax.broadcasted_iota(jnp.int32, (BW, 128), 1)
    return (r == c).astype(f32)


def _merge_common(ys_ref, o_refs, l_refs, hc_ref, g_refs, bg_ref, lng_ref, lnb_ref, wglu_ref, watt_ref, wpw_ref):
    r = {}
    ysv = ys_ref[...]
    r["ys"] = ysv
    r["ysin"] = _gelu(ysv).astype(bf16)
    z = _dot(r["ysin"], wglu_ref[...])
    r["z1"], r["sg2"] = z[:, :D], _sigmoid(z[:, D:])
    r["y_s"] = r["z1"] * r["sg2"]
    ls = [lr_[...] for lr_ in l_refs]
    mx = jnp.maximum(jnp.maximum(ls[0], ls[1]), ls[2])
    es = [jnp.exp(v - mx) for v in ls]
    tot = es[0] + es[1] + es[2]
    r["lse_tot"] = mx + jnp.log(tot)
    e_mat = _head_expand()
    o = jnp.zeros(ysv.shape, f32)
    for e, o_ref in zip(es, o_refs):
        o = o + _dot_hi(e / tot, e_mat) * o_ref[...].astype(f32)
    r["o"] = o
    r["ob"] = o.astype(bf16)
    r["y_a"] = _dot(r["ob"], watt_ref[...])
    hc = hc_ref[...]
    mu = jnp.mean(hc, axis=-1, keepdims=True)
    xc = hc - mu
    rstd = lax.rsqrt(jnp.mean(xc * xc, axis=-1, keepdims=True) + EPS)
    r["xh"], r["rstd"] = xc * rstd, rstd
    hn = r["xh"] * lng_ref[...] + lnb_ref[...]
    r["hn"] = hn
    r["sgn"] = _sigmoid(hn)
    r["hs"] = (hn * r["sgn"]).astype(bf16)
    r["y_c"] = _dot(r["hs"], wpw_ref[...])
    r["gates"] = [_sigmoid(g_refs[k][...].astype(f32) + bg_ref[:, k * D:(k + 1) * D]) for k in range(3)]
    r["merged"] = r["gates"][0] * r["y_s"] + r["gates"][1] * r["y_a"] + r["gates"][2] * r["y_c"]
    return r


TBM = 256


def _merge_in_specs(tok, tb):
    w = lambda shape: pl.BlockSpec(shape, lambda i: (0, 0))
    return ([pl.BlockSpec((tb, D), tok), pl.BlockSpec((tb, BW), tok)]
            + [pl.BlockSpec((tb, BW), tok)] * 3 + [pl.BlockSpec((tb, 128), tok)] * 3
            + [pl.BlockSpec((tb, BW), tok)]
            + [pl.BlockSpec((tb, D), lambda i, k=k: (i, 4 + k)) for k in range(3)]
            + [w((1, 3 * D)), w((1, BW)), w((1, BW)), w((BW, 2 * D)), w((BW, D)), w((BW, D)), w((D, D))])


def merge_fwd(x, ys, os_, lses, hc, proj, bg, lng, lnb, wglu, watt, wpw, wout):
    n = x.shape[0]

    def body(x_ref, ys_ref, o1, o2, o3, l1, l2, l3, hc_ref, g0, g1, g2, bg_ref, lng_ref, lnb_ref,
             wglu_ref, watt_ref, wpw_ref, wout_ref, x1_ref):
        r = _merge_common(ys_ref, (o1, o2, o3), (l1, l2, l3), hc_ref, (g0, g1, g2), bg_ref, lng_ref, lnb_ref,
                          wglu_ref, watt_ref, wpw_ref)
        x1_ref[...] = x_ref[...] + _dot(r["merged"].astype(bf16), wout_ref[...])

    tok = lambda i: (i, 0)
    return pl.pallas_call(
        body, name="merge_fwd", grid=(n // TB,), in_specs=_merge_in_specs(tok, TB),
        out_specs=pl.BlockSpec((TB, D), tok), out_shape=jax.ShapeDtypeStruct((n, D), f32),
        compiler_params=_cp(1, 56))(x, ys, *os_, *lses, hc, proj, proj, proj, bg, lng, lnb, wglu, watt, wpw, wout)


def merge_bwd(dx1, ys, os_, lses, hc, proj, bg, lng, lnb, wglu, watt, wpw, wout):
    n = dx1.shape[0]

    def body(dx_ref, ys_ref, o1, o2, o3, l1, l2, l3, hc_ref, g0, g1, g2, bg_ref, lng_ref, lnb_ref,
             wglu_ref, watt_ref, wpw_ref, wout_ref,
             dys_ref, do_ref, delta_ref, ltot_ref, dhc_ref, dgate_ref, ysin_ref, dz_ref, ob_ref, dya_ref,
             hs_ref, dyc_ref, mg_ref, dbg_ref, dlng_ref, dlnb_ref):
        i = pl.program_id(0)

        @pl.when(i == 0)
        def _():
            dbg_ref[...] = jnp.zeros_like(dbg_ref)
            dlng_ref[...] = jnp.zeros_like(dlng_ref)
            dlnb_ref[...] = jnp.zeros_like(dlnb_ref)

        r = _merge_common(ys_ref, (o1, o2, o3), (l1, l2, l3), hc_ref, (g0, g1, g2), bg_ref, lng_ref, lnb_ref,
                          wglu_ref, watt_ref, wpw_ref)
        mg_ref[...] = r["merged"].astype(bf16)
        ysin_ref[...] = r["ysin"]
        ob_ref[...] = r["ob"]
        hs_ref[...] = r["hs"]
        ltot_ref[...] = r["lse_tot"]
        dm = _dot_nt(dx_ref[...].astype(bf16), wout_ref[...])
        ys3 = (r["y_s"], r["y_a"], r["y_c"])
        for k in range(3):
            gk = r["gates"][k]
            dgr = dm * ys3[k] * gk * (1.0 - gk)
            dgate_ref[:, k * D:(k + 1) * D] = dgr.astype(bf16)
            dbg_ref[:, k * D:(k + 1) * D] += jnp.sum(dgr, axis=0, keepdims=True)
        dy_s = dm * r["gates"][0]
        sg2 = r["sg2"]
        dz = jnp.concatenate([dy_s * sg2, dy_s * r["z1"] * sg2 * (1.0 - sg2)], axis=1).astype(bf16)
        dz_ref[...] = dz
        dys_ref[...] = _dot_nt(dz, wglu_ref[...]) * _gelu_grad(r["ys"])
        dya = (dm * r["gates"][1]).astype(bf16)
        dya_ref[...] = dya
        do = _dot_nt(dya, watt_ref[...])
        do_ref[...] = do.astype(bf16)
        delta_ref[...] = _dot_hi(do * r["o"], _head_reduce())
        dyc = (dm * r["gates"][2]).astype(bf16)
        dyc_ref[...] = dyc
        sgn, hn = r["sgn"], r["hn"]
        dhn = _dot_nt(dyc, wpw_ref[...]) * sgn * (1.0 + hn * (1.0 - sgn))
        dlng_ref[...] += jnp.sum(dhn * r["xh"], axis=0, keepdims=True)
        dlnb_ref[...] += jnp.sum(dhn, axis=0, keepdims=True)
        dxh = dhn * lng_ref[...]
        xh = r["xh"]
        dhc_ref[...] = r["rstd"] * (dxh - jnp.mean(dxh, axis=-1, keepdims=True)
                                    - xh * jnp.mean(dxh * xh, axis=-1, keepdims=True))

    tok = lambda i: (i, 0)
    fix = lambda i: (0, 0)
    outs = [("dys", BW, f32), ("do", BW, bf16), ("delta", 128, f32), ("lse_tot", 128, f32), ("dhc", BW, f32),
            ("dgate", 3 * D, bf16), ("ysin", BW, bf16), ("dz", 2 * D, bf16), ("ob", BW, bf16), ("dya", D, bf16),
            ("hs", BW, bf16), ("dyc", D, bf16), ("merged", D, bf16)]
    small = [("dbg", 3 * D), ("dlng", BW), ("dlnb", BW)]
    res = pl.pallas_call(
        body, name="merge_bwd", grid=(n // TBM,), in_specs=_merge_in_specs(tok, TBM),
        out_specs=[pl.BlockSpec((TBM, w), tok) for _, w, _ in outs] + [pl.BlockSpec((1, w), fix) for _, w in small],
        out_shape=[jax.ShapeDtypeStruct((n, w), dt) for _, w, dt in outs]
        + [jax.ShapeDtypeStruct((1, w), f32) for _, w in small],
        compiler_params=_cp(1, 56))(dx1, ys, *os_, *lses, hc, proj, proj, proj, bg, lng, lnb, wglu, watt, wpw, wout)
    return dict(zip([k for k, _, _ in outs] + [k for k, _ in small], res))


def assemble_dproj(du, dqs, dks, dvs, dcv, dgate):
    b, l, _ = du.shape
    nck = BW // 128

    def body(du_ref, q1, q2, q3, k1, k2, k3, v1, v2, v3, cv_ref, g_ref, o_ref):
        o_ref[0, :, 0:BW] = du_ref[0]
        for c in range(nck):
            for j, qr in enumerate((q1, q2, q3)):
                o_ref[0, :, (1 + j) * BW + c * 128:(1 + j) * BW + (c + 1) * 128] = qr[0, c].astype(bf16)
            o_ref[0, :, 4 * BW + c * 128:4 * BW + (c + 1) * 128] = (k1[0, c] + k2[0, c] + k3[0, c]).astype(bf16)
            o_ref[0, :, 5 * BW + c * 128:5 * BW + (c + 1) * 128] = (v1[0, c] + v2[0, c] + v3[0, c]).astype(bf16)
        o_ref[0, :, 6 * BW:8 * BW] = cv_ref[0]
        o_ref[0, :, 8 * BW:] = g_ref[0]

    t = lambda w: pl.BlockSpec((1, TB, w), lambda bi, i: (bi, i, 0))
    ck = pl.BlockSpec((1, nck, TB, 128), lambda bi, i: (bi, 0, i, 0))
    return pl.pallas_call(
        body, name="assemble_dproj", grid=(b, l // TB),
        in_specs=[t(BW)] + [ck] * 9 + [t(2 * BW), t(3 * D)], out_specs=t(INC),
        out_shape=jax.ShapeDtypeStruct((b, l, INC), bf16), compiler_params=_cp(2))(du, *dqs, *dks, *dvs, dcv, dgate)


def _me():
    return lax.axis_index("x"), lax.axis_index("y"), lax.axis_index("c")


def _peers():
    x, y, c = _me()
    return [(x, y, 1 - c), (1 - x, y, c), (1 - x, y, 1 - c), (x, 1 - y, c), (x, 1 - y, 1 - c),
            (1 - x, 1 - y, c), (1 - x, 1 - y, 1 - c)]


def _rank(p):
    return 4 * p[0] + 2 * p[1] + p[2]


def allgather(arrs, name):
    na = len(arrs)
    units = [(a, j) for a in range(na) for j in range(arrs[a].shape[0])]
    nu = len(units)

    def body(*refs):
        ins, outs = refs[:na], refs[na:2 * na]
        send, recv, loc = refs[2 * na:]
        me = _rank(_me())
        local, remote = [], []
        for u, (a, j) in enumerate(units):
            own = pltpu.make_async_copy(ins[a].at[j], outs[a].at[j, me], loc.at[u])
            own.start()
            local.append(own)
        for u, (a, j) in enumerate(units):
            for k, p in enumerate(_peers()):
                cp = pltpu.make_async_remote_copy(src_ref=ins[a].at[j], dst_ref=outs[a].at[j, me],
                                                  send_sem=send.at[u, k], recv_sem=recv.at[u, k],
                                                  device_id=p, device_id_type=MESH)
                cp.start()
                remote.append(cp)
        for cp in local:
            cp.wait()
        for cp in remote:
            cp.wait()

    return pl.pallas_call(
        body, name=name, in_specs=[ANY] * na, out_specs=[ANY] * na,
        out_shape=[jax.ShapeDtypeStruct((a.shape[0], NDEV) + a.shape[1:], a.dtype) for a in arrs],
        scratch_shapes=[pltpu.SemaphoreType.DMA((nu, NDEV - 1)), pltpu.SemaphoreType.DMA((nu, NDEV - 1)),
                        pltpu.SemaphoreType.DMA((nu,))])(*arrs)


def exchange_grads(parts):
    nw = len(parts)
    flat = [p for pair in parts for p in pair]
    na = len(flat)

    def body(*refs):
        ins, outs = refs[:na], refs[na:na + nw]
        send, recv, loc = refs[na + nw:]
        me = _rank(_me())
        local, remote = [], []
        for a in range(na):
            w, layer = divmod(a, DEPTH)
            own = pltpu.make_async_copy(ins[a].at[me], outs[w].at[layer, me], loc.at[a])
            own.start()
            local.append(own)
        for a in range(na):
            w, layer = divmod(a, DEPTH)
            for k, p in enumerate(_peers()):
                cp = pltpu.make_async_remote_copy(src_ref=ins[a].at[_rank(p)], dst_ref=outs[w].at[layer, me],
                                                  send_sem=send.at[a, k], recv_sem=recv.at[a, k],
                                                  device_id=p, device_id_type=MESH)
                cp.start()
                remote.append(cp)
        for cp in local:
            cp.wait()
        for cp in remote:
            cp.wait()

    return pl.pallas_call(
        body, name="exchange_grads", in_specs=[ANY] * na, out_specs=[ANY] * nw,
        out_shape=[jax.ShapeDtypeStruct((DEPTH,) + pair[0].shape, pair[0].dtype) for pair in parts],
        scratch_shapes=[pltpu.SemaphoreType.DMA((na, NDEV - 1)), pltpu.SemaphoreType.DMA((na, NDEV - 1)),
                        pltpu.SemaphoreType.DMA((na,))])(*flat)


_C1 = 1.0 / (1.0 - ADAM_B1 ** ADAM_STEP)
_C2 = 1.0 / (1.0 - ADAM_B2 ** ADAM_STEP)


def _adamw(w, g, m, v):
    m = ADAM_B1 * m + (1.0 - ADAM_B1) * g
    v = ADAM_B2 * v + (1.0 - ADAM_B2) * (g * g)
    delta = -ADAM_LR * ((m * _C1) / (jnp.sqrt(v * _C2) + ADAM_EPS) + ADAM_WD * w)
    return delta, m, v


def adam_big(recv, w, m, v, name):
    _, _, k, n = recv.shape
    tk = k
    while tk * n * 2 * NDEV > 4 * 1024 * 1024 and tk % 16 == 0:
        tk //= 2

    def body(r_ref, w_ref, m_ref, v_ref, g_ref, d_ref, nm_ref, nv_ref):
        g = r_ref[0, 0].astype(f32)
        for s in range(1, NDEV):
            g = g + r_ref[0, s].astype(f32)
        d, nm, nv = _adamw(w_ref[0], g, m_ref[0], v_ref[0])
        g_ref[0], d_ref[0], nm_ref[0], nv_ref[0] = g, d, nm, nv

    blk = pl.BlockSpec((1, tk, n), lambda l, i: (l, i, 0))
    return pl.pallas_call(
        body, name=name, grid=(DEPTH, k // tk),
        in_specs=[pl.BlockSpec((1, NDEV, tk, n), lambda l, i: (l, 0, i, 0)), blk, blk, blk],
        out_specs=[blk] * 4, out_shape=[jax.ShapeDtypeStruct(w.shape, f32)] * 4,
        compiler_params=_cp(2))(recv, w, m, v)


def adam_small(gath, w, m, v):
    r = w.shape[0]
    tr = 512

    def body(g_ref, w_ref, m_ref, v_ref, go_ref, d_ref, nm_ref, nv_ref):
        g = g_ref[0]
        for s in range(1, NDEV):
            g = g + g_ref[s]
        d, nm, nv = _adamw(w_ref[...], g, m_ref[...], v_ref[...])
        go_ref[...], d_ref[...], nm_ref[...], nv_ref[...] = g, d, nm, nv

    blk = pl.BlockSpec((tr, 128), lambda i: (i, 0))
    return pl.pallas_call(
        body, name="adam_small", grid=(r // tr,),
        in_specs=[pl.BlockSpec((NDEV, tr, 128), lambda i: (0, i, 0)), blk, blk, blk],
        out_specs=[blk] * 4, out_shape=[jax.ShapeDtypeStruct((r, 128), f32)] * 4,
        compiler_params=_cp(1))(gath, w, m, v)


SMALL = ["norm1_g", "b_gate", "ssm_lambda_re", "ssm_lambda_im", "ssm_log_dt", "ssm_b_re", "ssm_b_im",
         "ssm_c_re", "ssm_c_im", "ssm_d", "conv_w", "conv_b", "conv_ln_g", "conv_ln_b", "norm2_g", "final_g"]
BIG = ["w_in", "w_ssm_glu", "w_att_up", "w_conv_pw2", "w_out", "w_ffn_in", "w_ffn_out"]
ORDER = ["norm1_g", "w_in", "b_gate", "ssm_lambda_re", "ssm_lambda_im", "ssm_log_dt", "ssm_b_re", "ssm_b_im",
         "ssm_c_re", "ssm_c_im", "ssm_d", "w_ssm_glu", "w_att_up", "conv_w", "conv_b", "conv_ln_g", "conv_ln_b",
         "w_conv_pw2", "w_out", "norm2_g", "w_ffn_in", "w_ffn_out", "final_g"]
PACK_ROWS = 2560


def _pack(arrs):
    flat = jnp.concatenate([a.reshape(-1).astype(f32) for a in arrs])
    return jnp.pad(flat, (0, PACK_ROWS * 128 - flat.shape[0])).reshape(PACK_ROWS, 128)


def _unpack(pack, shapes):
    flat = pack.reshape(-1)
    out, off = [], 0
    for s in shapes:
        sz = math.prod(s)
        out.append(flat[off:off + sz].reshape(s))
        off += sz
    return out


def _bt(b):
    return b.transpose(2, 0, 1).reshape(GH, NSTATE)


def _bt_inv(bt):
    return bt.reshape(GH, NG, NS).transpose(1, 2, 0)


def _ct(c):
    return c.transpose(1, 0, 2).reshape(GH, NSTATE)


def _ct_inv(ct):
    return ct.reshape(GH, NG, NS).transpose(1, 0, 2)


def local_step(x, loss_target, P, G):
    bsz, seq, _ = x.shape
    n = bsz * seq

    def natural(g4, layer):
        k_, n_ = g4.shape[2], g4.shape[3]
        return g4[layer].transpose(1, 0, 2).reshape(k_, NDEV * n_)

    conv_w_full = G["conv_w"].transpose(0, 2, 1, 3).reshape(DEPTH, CW, BW)
    conv_w_pad = jnp.pad(conv_w_full, ((0, 0), (0, 1), (0, 0)))

    xs = x.reshape(n, D)
    saved = []
    for l in range(DEPTH):
        S = {"x": xs}
        h1 = rms_fwd(xs, P["norm1_g"][l][None])
        proj = inproj(h1, G["w_in"], l)
        proj3 = proj.reshape(bsz, seq, INC)
        lr = P["ssm_lambda_re"][l].reshape(1, NSTATE)
        li = P["ssm_lambda_im"][l].reshape(1, NSTATE)
        ld = jnp.repeat(P["ssm_log_dt"][l], NS).reshape(1, NSTATE)
        btr, bti = _bt(P["ssm_b_re"][l]), _bt(P["ssm_b_im"][l])
        t8, bb, cb = s5_params(lr, li, ld, btr, bti, _ct(P["ssm_c_re"][l]), _ct(P["ssm_c_im"][l]))
        dskip = P["ssm_d"][l][None]
        ys = s5_fwd(proj3, t8, bb, cb, dskip)
        att = [att_fwd(proj3, gi, dil) for gi, (_, dil) in enumerate(PATTERNS)]
        hc = conv_fwd(proj3, conv_w_pad[l], P["conv_b"][l][None])
        wts = dict(wglu=natural(G["w_ssm_glu"], l), watt=natural(G["w_att_up"], l),
                   wpw=natural(G["w_conv_pw2"], l), wout=G["w_out"][l].reshape(D, D))
        mi = dict(ys=ys.reshape(n, BW), os_=[a[0].reshape(n, BW) for a in att],
                  lses=[a[1].reshape(n, 128) for a in att], hc=hc.reshape(n, BW),
                  proj=proj, bg=P["b_gate"][l][None], lng=P["conv_ln_g"][l][None], lnb=P["conv_ln_b"][l][None],
                  **wts)
        x1 = merge_fwd(xs, **mi)
        x2 = ffn_fwd(x1, P["norm2_g"][l][None], G["w_ffn_in"], G["w_ffn_out"], l)
        S.update(h1=h1, proj=proj, proj3=proj3, tabs=(t8, bb, cb), mi=mi, x1=x1,
                 sp=(lr, li, ld, btr, bti), dskip=dskip)
        saved.append(S)
        xs = x2

    loss8, dx, dfinal = loss_head(xs, P["final_g"][None], loss_target.reshape(n, D))

    small_g = {k: [None] * DEPTH for k in SMALL if k != "final_g"}
    big_g = {k: [None] * DEPTH for k in BIG}
    tokblk = lambda w: pl.BlockSpec((1024, w), lambda s, i: (i, 0))
    colblk = lambda w: pl.BlockSpec((1024, w), lambda s, i: (i, s))
    sh3blk = lambda w: pl.BlockSpec((1, 1024, w), lambda s, i: (s, i, 0))
    for l in reversed(range(DEPTH)):
        S = saved[l]
        g2 = P["norm2_g"][l][None]
        dx1, h2, dz1, dz2, a4, dg2 = ffn_bwd(S["x1"], g2, G["w_ffn_in"], G["w_ffn_out"], l, dx)
        small_g["norm2_g"][l] = dg2
        dwa = mm_tn(h2, dz1, tokblk(D), sh3blk(NSH_FF), 4, D, NSH_FF, n, "dw_ffn_in_a")
        dwb = mm_tn(h2, dz2, tokblk(D), sh3blk(NSH_FF), 4, D, NSH_FF, n, "dw_ffn_in_b")
        big_g["w_ffn_in"][l] = jnp.concatenate([dwa, dwb], axis=0)
        big_g["w_ffn_out"][l] = mm_tn(a4, dx, sh3blk(NSH_FF), tokblk(D), 4, NSH_FF, D, n,
                                      "dw_ffn_out").reshape(NDEV, NSH_FF // 2, D)
        mb = merge_bwd(dx1, **S["mi"])
        small_g["b_gate"][l], small_g["conv_ln_g"][l], small_g["conv_ln_b"][l] = mb["dbg"], mb["dlng"], mb["dlnb"]
        big_g["w_ssm_glu"][l] = mm_tn(mb["ysin"], mb["dz"], tokblk(BW), colblk(256), NDEV, BW, 256, n, "dw_glu")
        big_g["w_att_up"][l] = mm_tn(mb["ob"], mb["dya"], tokblk(BW), colblk(128), NDEV, BW, 128, n, "dw_att")
        big_g["w_conv_pw2"][l] = mm_tn(mb["hs"], mb["dyc"], tokblk(BW), colblk(128), NDEV, BW, 128, n, "dw_pw2")
        big_g["w_out"][l] = mm_tn(mb["merged"], dx1, tokblk(D), tokblk(D), 1, D, D, n,
                                  "dw_out").reshape(NDEV, D // NDEV, D)
        dcv, dcw, dcb = conv_bwd(S["proj3"], mb["dhc"].reshape(bsz, seq, BW), conv_w_pad[l])
        small_g["conv_w"][l] = dcw[:CW].reshape(CW, NDEV, BW // NDEV).transpose(1, 0, 2)
        small_g["conv_b"][l] = dcb
        ab = [att_bwd(S["proj3"], mb["do"].reshape(bsz, seq, BW), mb["lse_tot"].reshape(bsz, seq, 128),
                      mb["delta"].reshape(bsz, seq, 128), gi, dil) for gi, (_, dil) in enumerate(PATTERNS)]
        t8, bb, cb = S["tabs"]
        du, d_a, d_bb, d_cb, d_d = s5_bwd(S["proj3"], mb["dys"].reshape(bsz, seq, BW), t8, bb, cb, S["dskip"])
        lr, li, ld, btr, bti = S["sp"]
        dlr, dli, dld, dbt, dct = s5_params_bwd(lr, li, ld, btr, bti, d_a, d_bb, d_cb)
        small_g["ssm_lambda_re"][l], small_g["ssm_lambda_im"][l] = dlr.reshape(NG, NS), dli.reshape(NG, NS)
        small_g["ssm_log_dt"][l] = dld[0, :NG]
        small_g["ssm_b_re"][l], small_g["ssm_b_im"][l] = _bt_inv(dbt[0]), _bt_inv(dbt[1])
        small_g["ssm_c_re"][l], small_g["ssm_c_im"][l] = _ct_inv(dct[0]), _ct_inv(dct[1])
        small_g["ssm_d"][l] = d_d
        dproj = assemble_dproj(du, [a[0] for a in ab], [a[1] for a in ab], [a[2] for a in ab],
                               dcv, mb["dgate"].reshape(bsz, seq, 3 * D)).reshape(n, INC)
        big_g["w_in"][l] = mm_tn(S["h1"], dproj, tokblk(D), colblk(NSH_IN), NDEV, D, NSH_IN, n, "dw_in")
        dx, dg1 = inproj_bwd(dproj, G["w_in"], l, S["x"], P["norm1_g"][l][None], dx1)
        small_g["norm1_g"][l] = dg1
    return loss8, dx, dfinal, small_g, big_g


def kernel(x, norm1_g, w_in, b_gate, ssm_lambda_re, ssm_lambda_im, ssm_log_dt, ssm_b_re, ssm_b_im, ssm_c_re, ssm_c_im, ssm_d, w_ssm_glu, w_att_up, conv_w, conv_b, conv_ln_g, conv_ln_b, w_conv_pw2, w_out, norm2_g, w_ffn_in, w_ffn_out, final_g, loss_target, m_norm1_g, m_w_in, m_b_gate, m_ssm_lambda_re, m_ssm_lambda_im, m_ssm_log_dt, m_ssm_b_re, m_ssm_b_im, m_ssm_c_re, m_ssm_c_im, m_ssm_d, m_w_ssm_glu, m_w_att_up, m_conv_w, m_conv_b, m_conv_ln_g, m_conv_ln_b, m_w_conv_pw2, m_w_out, m_norm2_g, m_w_ffn_in, m_w_ffn_out, m_final_g, v_norm1_g, v_w_in, v_b_gate, v_ssm_lambda_re, v_ssm_lambda_im, v_ssm_log_dt, v_ssm_b_re, v_ssm_b_im, v_ssm_c_re, v_ssm_c_im, v_ssm_d, v_w_ssm_glu, v_w_att_up, v_conv_w, v_conv_b, v_conv_ln_g, v_conv_ln_b, v_w_conv_pw2, v_w_out, v_norm2_g, v_w_ffn_in, v_w_ffn_out, v_final_g):
    args = dict(locals())
    W = {k: args[k] for k in ORDER}
    M = {k: args["m_" + k] for k in ORDER}
    V = {k: args["v_" + k] for k in ORDER}
    bsz, seq, _ = x.shape
    n = bsz * seq
    me = 4 * lax.axis_index("x") + 2 * lax.axis_index("y") + lax.axis_index("c")

    gath = allgather([W[k].astype(bf16) for k in BIG] + [conv_w], "allgather_weights")
    G = dict(zip(BIG + ["conv_w"], gath))

    loss8, dx, dfinal, small_g, big_g = local_step(x, loss_target, W, G)

    recv = exchange_grads([big_g[k] for k in BIG])
    out = {}
    for k, r in zip(BIG, recv):
        shp = W[k].shape
        r4 = r.reshape(DEPTH, NDEV, shp[1], shp[2])
        out[k] = adam_big(r4, W[k], M[k], V[k], "adam_" + k)

    names = [k for k in SMALL if k != "final_g"]
    shapes = [(DEPTH, NDEV, CW, BW // NDEV) if k == "conv_w" else W[k].shape for k in names] + [(D,), (1,)]
    gpack = _pack([jnp.stack([g.reshape(shapes[i][1:]) for g in small_g[k]]) for i, k in enumerate(names)]
                  + [dfinal, loss8[0, :1]])

    def wpack(src):
        parts = [jnp.broadcast_to(src[k][:, None], shapes[i]) if k == "conv_w" else src[k] for i, k in enumerate(names)]
        return _pack(parts + [src["final_g"], jnp.ones((1,), f32)])

    (gall,) = allgather([gpack[None]], "allgather_small")
    sg, sd, sm, sv = [_unpack(p, shapes) for p in adam_small(gall[0], wpack(W), wpack(M), wpack(V))]
    for i, k in enumerate(names + ["final_g"]):
        vals = [t[i] for t in (sg, sd, sm, sv)]
        if k == "conv_w":
            vals = [lax.dynamic_index_in_dim(t, me, axis=1, keepdims=False) for t in vals]
        out[k] = vals
    loss = sg[-1].reshape(())

    res = [loss, dx.reshape(bsz, seq, D)]
    for j in range(4):
        res += [out[k][j] for k in ORDER]
    return tuple(res)
```

```python
import functools
import math

import jax
import jax.numpy as jnp
from jax import lax
from jax.experimental import pallas as pl
from jax.experimental.pallas import tpu as pltpu

f32 = jnp.float32
bf16 = jnp.bfloat16

D = 1024
DEPTH = 2
EPS = 1e-6
BW = 512
NG = 32
GH = 16
NS = 64
NSTATE = NG * NS
HD = 64
NH = 8
PATTERNS = ((128, 1), (512, 4), (2048, 16))
ABLK = 128
ATT_SCALE = HD ** -0.5
CW = 31
DFF = 2816
INC = 7168
NDEV = 8
NSH_IN = INC // NDEV
NSH_FF = 2 * DFF // NDEV
ADAM_LR, ADAM_B1, ADAM_B2, ADAM_EPS, ADAM_WD, ADAM_STEP = 0.001, 0.9, 0.999, 1e-08, 0.01, 10

TB = 512
SJ = 4
SW = NSTATE // SJ
SU = BW // SJ
NEG = -1e30
MESH = pl.DeviceIdType.MESH
ANY = pl.BlockSpec(memory_space=pl.ANY)


def _cp(n_axes, vmem_mb=48):
    return pltpu.CompilerParams(dimension_semantics=("arbitrary",) * n_axes,
                                vmem_limit_bytes=vmem_mb * 1024 * 1024)


def _dot(a, b):
    return jnp.dot(a, b, preferred_element_type=f32)


def _dot_nt(a, b):
    return lax.dot_general(a, b, (((1,), (1,)), ((), ())), preferred_element_type=f32)


def _dot_tn(a, b):
    return lax.dot_general(a, b, (((0,), (0,)), ((), ())), preferred_element_type=f32)


def _dot_hi(a, b):
    return jnp.dot(a, b, precision=lax.Precision.HIGHEST, preferred_element_type=f32)


def _sigmoid(x):
    return 1.0 / (1.0 + jnp.exp(-x))


_GC = math.sqrt(2.0 / math.pi)


def _gelu(x):
    return 0.5 * x * (1.0 + jnp.tanh(_GC * (x + 0.044715 * x * x * x)))


def _gelu_grad(x):
    t = jnp.tanh(_GC * (x + 0.044715 * x * x * x))
    return 0.5 * (1.0 + t) + 0.5 * x * (1.0 - t * t) * _GC * (1.0 + 3.0 * 0.044715 * x * x)


def _rms_stats(x):
    return lax.rsqrt(jnp.mean(x * x, axis=-1, keepdims=True) + EPS)


def _rms_bwd(x, g, dh):
    r = _rms_stats(x)
    dyg = dh * g
    dx = r * dyg - x * (r * r * r) * jnp.mean(dyg * x, axis=-1, keepdims=True)
    dg = jnp.sum(dh * x * r, axis=0, keepdims=True)
    return dx, dg


def rms_fwd(x, g):
    n = x.shape[0]

    def body(x_ref, g_ref, h_ref):
        xv = x_ref[...]
        h_ref[...] = (xv * _rms_stats(xv) * g_ref[...]).astype(bf16)

    return pl.pallas_call(
        body, name="rms_fwd", grid=(n // TB,),
        in_specs=[pl.BlockSpec((TB, D), lambda i: (i, 0)), pl.BlockSpec((1, D), lambda i: (0, 0))],
        out_specs=pl.BlockSpec((TB, D), lambda i: (i, 0)),
        out_shape=jax.ShapeDtypeStruct((n, D), bf16), compiler_params=_cp(1))(x, g)


def inproj(h, w4, layer):
    n = h.shape[0]
    tm = 1024

    def body(h_ref, w_ref, o_ref):
        o_ref[...] = _dot(h_ref[...], w_ref[0, 0]).astype(bf16)

    return pl.pallas_call(
        body, name="inproj", grid=(NDEV, n // tm),
        in_specs=[pl.BlockSpec((tm, D), lambda s, i: (i, 0)),
                  pl.BlockSpec((1, 1, D, NSH_IN), lambda s, i: (layer, s, 0, 0))],
        out_specs=pl.BlockSpec((tm, NSH_IN), lambda s, i: (i, s)),
        out_shape=jax.ShapeDtypeStruct((n, INC), bf16), compiler_params=_cp(2))(h, w4)


def inproj_bwd(dproj, w4, layer, x, g, dres):
    n = x.shape[0]
    tm = 1024

    def body(dp_ref, w_ref, x_ref, g_ref, dr_ref, dx_ref, dg_ref, acc):
        i, s = pl.program_id(0), pl.program_id(1)

        @pl.when(s == 0)
        def _():
            acc[...] = jnp.zeros_like(acc)

        @pl.when((s == 0) & (i == 0))
        def _():
            dg_ref[...] = jnp.zeros_like(dg_ref)

        acc[...] += _dot_nt(dp_ref[...], w_ref[0, 0])

        @pl.when(s == NDEV - 1)
        def _():
            dx, dg = _rms_bwd(x_ref[...], g_ref[...], acc[...])
            dx_ref[...] = dr_ref[...] + dx
            dg_ref[...] += dg

    return pl.pallas_call(
        body, name="inproj_bwd", grid=(n // tm, NDEV),
        in_specs=[pl.BlockSpec((tm, NSH_IN), lambda i, s: (i, s)),
                  pl.BlockSpec((1, 1, D, NSH_IN), lambda i, s: (layer, s, 0, 0)),
                  pl.BlockSpec((tm, D), lambda i, s: (i, 0)),
                  pl.BlockSpec((1, D), lambda i, s: (0, 0)),
                  pl.BlockSpec((tm, D), lambda i, s: (i, 0))],
        out_specs=[pl.BlockSpec((tm, D), lambda i, s: (i, 0)), pl.BlockSpec((1, D), lambda i, s: (0, 0))],
        out_shape=[jax.ShapeDtypeStruct((n, D), f32), jax.ShapeDtypeStruct((1, D), f32)],
        scratch_shapes=[pltpu.VMEM((tm, D), f32)], compiler_params=_cp(2))(dproj, w4, x, g, dres)


def mm_tn(a, b, a_spec, b_spec, n_sh, ka, nb, m, name):
    tm = 1024

    def body(a_ref, b_ref, o_ref, acc):
        i = pl.program_id(1)

        @pl.when(i == 0)
        def _():
            acc[...] = jnp.zeros_like(acc)

        av = a_ref[...].reshape(tm, ka).astype(bf16)
        bv = b_ref[...].reshape(tm, nb).astype(bf16)
        acc[...] += _dot_tn(av, bv)

        @pl.when(i == m // tm - 1)
        def _():
            o_ref[0] = acc[...].astype(bf16)

    return pl.pallas_call(
        body, name=name, grid=(n_sh, m // tm), in_specs=[a_spec, b_spec],
        out_specs=pl.BlockSpec((1, ka, nb), lambda s, i: (s, 0, 0)),
        out_shape=jax.ShapeDtypeStruct((n_sh, ka, nb), bf16),
        scratch_shapes=[pltpu.VMEM((ka, nb), f32)], compiler_params=_cp(2))(a, b)


def ffn_fwd(x1, g2, w1, w2, layer):
    n = x1.shape[0]
    w2p = w2.reshape(w2.shape[0], 4, NSH_FF, D)

    def body(x_ref, g_ref, wa_ref, wb_ref, w2_ref, o_ref, h_sc):
        s = pl.program_id(1)

        @pl.when(s == 0)
        def _():
            xv = x_ref[...]
            h_sc[...] = (xv * _rms_stats(xv) * g_ref[...]).astype(bf16)
            o_ref[...] = xv

        h = h_sc[...]
        z1 = _dot(h, wa_ref[0, 0])
        z2 = _dot(h, wb_ref[0, 0])
        a = (z1 * _sigmoid(z1) * z2).astype(bf16)
        o_ref[...] += _dot(a, w2_ref[0, 0])

    return pl.pallas_call(
        body, name="ffn_fwd", grid=(n // TB, 4),
        in_specs=[pl.BlockSpec((TB, D), lambda i, s: (i, 0)),
                  pl.BlockSpec((1, D), lambda i, s: (0, 0)),
                  pl.BlockSpec((1, 1, D, NSH_FF), lambda i, s: (layer, s, 0, 0)),
                  pl.BlockSpec((1, 1, D, NSH_FF), lambda i, s: (layer, s + 4, 0, 0)),
                  pl.BlockSpec((1, 1, NSH_FF, D), lambda i, s: (layer, s, 0, 0))],
        out_specs=pl.BlockSpec((TB, D), lambda i, s: (i, 0)),
        out_shape=jax.ShapeDtypeStruct((n, D), f32),
        scratch_shapes=[pltpu.VMEM((TB, D), bf16)], compiler_params=_cp(2))(x1, g2, w1, w1, w2p)


def ffn_bwd(x1, g2, w1, w2, layer, dx2):
    n = x1.shape[0]
    w2p = w2.reshape(w2.shape[0], 4, NSH_FF, D)

    def body(x_ref, g_ref, dy_ref, wa_ref, wb_ref, w2_ref,
             dx_ref, h_ref, dz1_ref, dz2_ref, a_ref, dg_ref, dh_sc, dyb_sc):
        i, s = pl.program_id(0), pl.program_id(1)

        @pl.when(s == 0)
        def _():
            xv = x_ref[...]
            h_ref[...] = (xv * _rms_stats(xv) * g_ref[...]).astype(bf16)
            dh_sc[...] = jnp.zeros_like(dh_sc)
            dyb_sc[...] = dy_ref[...].astype(bf16)

        @pl.when((s == 0) & (i == 0))
        def _():
            dg_ref[...] = jnp.zeros_like(dg_ref)

        h = h_ref[...]
        z1 = _dot(h, wa_ref[0, 0])
        z2 = _dot(h, wb_ref[0, 0])
        sg = _sigmoid(z1)
        sl = z1 * sg
        a_ref[0] = (sl * z2).astype(bf16)
        da = _dot_nt(dyb_sc[...], w2_ref[0, 0])
        dz2 = (da * sl).astype(bf16)
        dz1 = (da * z2 * sg * (1.0 + z1 * (1.0 - sg))).astype(bf16)
        dz1_ref[0] = dz1
        dz2_ref[0] = dz2
        dh_sc[...] += _dot_nt(dz1, wa_ref[0, 0]) + _dot_nt(dz2, wb_ref[0, 0])

        @pl.when(s == 3)
        def _():
            dx, dg = _rms_bwd(x_ref[...], g_ref[...], dh_sc[...])
            dx_ref[...] = dy_ref[...] + dx
            dg_ref[...] += dg

    tok = lambda i, s: (i, 0)
    sh3 = lambda i, s: (s, i, 0)
    return pl.pallas_call(
        body, name="ffn_bwd", grid=(n // TB, 4),
        in_specs=[pl.BlockSpec((TB, D), tok), pl.BlockSpec((1, D), lambda i, s: (0, 0)), pl.BlockSpec((TB, D), tok),
                  pl.BlockSpec((1, 1, D, NSH_FF), lambda i, s: (layer, s, 0, 0)),
                  pl.BlockSpec((1, 1, D, NSH_FF), lambda i, s: (layer, s + 4, 0, 0)),
                  pl.BlockSpec((1, 1, NSH_FF, D), lambda i, s: (layer, s, 0, 0))],
        out_specs=[pl.BlockSpec((TB, D), tok), pl.BlockSpec((TB, D), tok),
                   pl.BlockSpec((1, TB, NSH_FF), sh3), pl.BlockSpec((1, TB, NSH_FF), sh3),
                   pl.BlockSpec((1, TB, NSH_FF), sh3), pl.BlockSpec((1, D), lambda i, s: (0, 0))],
        out_shape=[jax.ShapeDtypeStruct((n, D), f32), jax.ShapeDtypeStruct((n, D), bf16),
                   jax.ShapeDtypeStruct((4, n, NSH_FF), bf16), jax.ShapeDtypeStruct((4, n, NSH_FF), bf16),
                   jax.ShapeDtypeStruct((4, n, NSH_FF), bf16), jax.ShapeDtypeStruct((1, D), f32)],
        scratch_shapes=[pltpu.VMEM((TB, D), f32), pltpu.VMEM((TB, D), bf16)],
        compiler_params=_cp(2))(x1, g2, dx2, w1, w1, w2p)


def loss_head(x, g, target):
    n = x.shape[0]

    def body(x_ref, g_ref, t_ref, l_ref, dx_ref, dg_ref):
        i = pl.program_id(0)

        @pl.when(i == 0)
        def _():
            l_ref[...] = jnp.zeros_like(l_ref)
            dg_ref[...] = jnp.zeros_like(dg_ref)

        xv = x_ref[...]
        y = xv * _rms_stats(xv) * g_ref[...]
        e = y - t_ref[...]
        l_ref[...] += 0.5 * jnp.sum(jnp.sum(e * e, axis=-1, keepdims=True), axis=0, keepdims=True) * (1.0 / D)
        dx, dg = _rms_bwd(xv, g_ref[...], e * (1.0 / D))
        dx_ref[...] = dx
        dg_ref[...] += dg

    tok = lambda i: (i, 0)
    return pl.pallas_call(
        body, name="loss_head", grid=(n // TB,),
        in_specs=[pl.BlockSpec((TB, D), tok), pl.BlockSpec((1, D), lambda i: (0, 0)), pl.BlockSpec((TB, D), tok)],
        out_specs=[pl.BlockSpec((8, 128), lambda i: (0, 0)), pl.BlockSpec((TB, D), tok),
                   pl.BlockSpec((1, D), lambda i: (0, 0))],
        out_shape=[jax.ShapeDtypeStruct((8, 128), f32), jax.ShapeDtypeStruct((n, D), f32),
                   jax.ShapeDtypeStruct((1, D), f32)],
        compiler_params=_cp(1))(x, g, target)


def _disc(lr, li, ld):
    dt = jnp.exp(ld)
    mag = jnp.exp(lr * dt)
    ar = mag * jnp.cos(li * dt)
    ai = mag * jnp.sin(li * dt)
    nr, ni = ar - 1.0, ai
    den = lr * lr + li * li
    zr = (nr * lr + ni * li) / den
    zi = (ni * lr - nr * li) / den
    return ar, ai, zr, zi


def _blockdiag_mask(shape):
    r = lax.broadcasted_iota(jnp.int32, shape, 0) // GH
    c = lax.broadcasted_iota(jnp.int32, shape, 1) // NS
    return r == c


def s5_params(lr, li, ld, btr, bti, ctr, cti):
    def body(lr_ref, li_ref, ld_ref, btr_ref, bti_ref, ctr_ref, cti_ref, t8_ref, bb_ref, cb_ref):
        ar, ai, zr, zi = _disc(lr_ref[...], li_ref[...], ld_ref[...])
        pr_, pi_ = ar, ai
        pw2 = []
        for k in range(4):
            pw2.append((pr_, pi_))
            pr_, pi_ = pr_ * pr_ - pi_ * pi_, 2.0 * pr_ * pi_
        cm = lambda p, q: (p[0] * q[0] - p[1] * q[1], p[0] * q[1] + p[1] * q[0])
        pw = {1: pw2[0], 2: pw2[1], 4: pw2[2], 8: pw2[3]}
        pw[3], pw[5], pw[6] = cm(pw[2], pw[1]), cm(pw[4], pw[1]), cm(pw[4], pw[2])
        pw[7] = cm(pw[4], pw[3])
        row = lax.broadcasted_iota(jnp.int32, (8, NSTATE), 0)
        zero = jnp.zeros((8, NSTATE), f32)
        for c in range(2):
            for k in range(3):
                full = jnp.broadcast_to(pw2[k][c], (8, NSTATE))
                t8_ref[c, k] = jnp.where(row >= (1 << k), full, 0.0)
                t8_ref[c, 3 + k] = jnp.where(row + (1 << k) < 8, full, 0.0)
            up, down = zero, zero
            for j in range(8):
                up = up + jnp.where(row == j, pw[j + 1][c], 0.0)
                down = down + jnp.where(row == j, pw[8 - j][c], 0.0)
            t8_ref[c, 6] = up
            t8_ref[c, 7] = down
        bbr = zr * btr_ref[...] - zi * bti_ref[...]
        bbi = zr * bti_ref[...] + zi * btr_ref[...]
        mask = _blockdiag_mask((SU, SW))
        for j in range(SJ):
            cols = slice(j * SW, (j + 1) * SW)
            for c, (vb, vc) in enumerate(((bbr, ctr_ref[...]), (bbi, cti_ref[...]))):
                bb_ref[c, j] = jnp.where(mask, jnp.tile(vb[:, cols], (SU // GH, 1)), 0.0).astype(bf16)
                cb_ref[c, j] = jnp.where(mask, jnp.tile(vc[:, cols], (SU // GH, 1)), 0.0).astype(bf16)

    return pl.pallas_call(
        body, name="s5_params",
        out_shape=[jax.ShapeDtypeStruct((2, 8, 8, NSTATE), f32),
                   jax.ShapeDtypeStruct((2, SJ, SU, SW), bf16), jax.ShapeDtypeStruct((2, SJ, SU, SW), bf16)],
        compiler_params=pltpu.CompilerParams(vmem_limit_bytes=56 * 1024 * 1024))(lr, li, ld, btr, bti, ctr, cti)


def s5_params_bwd(lr, li, ld, btr, bti, d_a, d_bb, d_cb):
    def body(lr_ref, li_ref, ld_ref, btr_ref, bti_ref, da_ref, dbb_ref, dcb_ref,
             dlr_ref, dli_ref, dld_ref, dbt_ref, dct_ref):
        mask = _blockdiag_mask((SU, SW))

        def fold(ref, c):
            parts = []
            for j in range(SJ):
                v = jnp.where(mask, ref[c, j], 0.0)
                parts.append(v.reshape(SU // GH, GH, SW).sum(axis=0))
            return jnp.concatenate(parts, axis=1)

        dct_ref[0] = fold(dcb_ref, 0)
        dct_ref[1] = fold(dcb_ref, 1)
        dbbr, dbbi = fold(dbb_ref, 0), fold(dbb_ref, 1)
        lrv, liv, ldv = lr_ref[...], li_ref[...], ld_ref[...]
        (ar, ai, zr, zi), vjp = jax.vjp(_disc, lrv, liv, ldv)
        btr, bti = btr_ref[...], bti_ref[...]
        dbt_ref[0] = zr * dbbr + zi * dbbi
        dbt_ref[1] = zr * dbbi - zi * dbbr
        dzr = jnp.sum(dbbr * btr + dbbi * bti, axis=0, keepdims=True)
        dzi = jnp.sum(dbbi * btr - dbbr * bti, axis=0, keepdims=True)
        dlr, dli, dld = vjp((da_ref[0:1, :], da_ref[1:2, :], dzr, dzi))
        dlr_ref[...] = dlr
        dli_ref[...] = dli
        ind = (lax.broadcasted_iota(jnp.int32, (NSTATE, 128), 0) // NS
               == lax.broadcasted_iota(jnp.int32, (NSTATE, 128), 1)).astype(f32)
        dld_ref[...] = _dot_hi(jnp.broadcast_to(dld, (8, NSTATE)), ind)

    return pl.pallas_call(
        body, name="s5_params_bwd",
        out_shape=[jax.ShapeDtypeStruct((1, NSTATE), f32), jax.ShapeDtypeStruct((1, NSTATE), f32),
                   jax.ShapeDtypeStruct((8, 128), f32), jax.ShapeDtypeStruct((2, GH, NSTATE), f32),
                   jax.ShapeDtypeStruct((2, GH, NSTATE), f32)],
        compiler_params=pltpu.CompilerParams(vmem_limit_bytes=56 * 1024 * 1024))(lr, li, ld, btr, bti, d_a, d_bb, d_cb)


def _fma(sr, si, ar, ai, qr, qi):
    return sr + ar * qr - ai * qi, si + ar * qi + ai * qr


def _scan_tile(sr, si, cr, ci, t8_ref, reverse):
    sg = -1.0 if reverse else 1.0
    for k in range(3):
        tk = 3 + k if reverse else k
        rot = 8 - (1 << k) if reverse else 1 << k
        sr, si = _fma(sr, si, t8_ref[0, tk], sg * t8_ref[1, tk], pltpu.roll(sr, rot, 0), pltpu.roll(si, rot, 0))
    tp = 7 if reverse else 6
    sr, si = _fma(sr, si, t8_ref[0, tp], sg * t8_ref[1, tp], cr, ci)
    e = 0 if reverse else 7
    return sr, si, jnp.broadcast_to(sr[e:e + 1, :], sr.shape), jnp.broadcast_to(si[e:e + 1, :], si.shape)


S5MC = 512


def _s5_input_map(u_ref, bb_ref, sr_sc, si_sc, l):
    for c in range(l // S5MC):
        rows = slice(c * S5MC, (c + 1) * S5MC)
        u = u_ref[0, rows, :]
        sr_sc[rows, :] = _dot(u, bb_ref[0, 0])
        si_sc[rows, :] = _dot(u, bb_ref[1, 0])


def _s5_forward_scan(sr_sc, si_sc, t8_ref, l):
    def step(k, carry):
        rows = pl.ds(pl.multiple_of(k * 8, 8), 8)
        sr, si, cr, ci = _scan_tile(sr_sc[rows, :], si_sc[rows, :], carry[0], carry[1], t8_ref, False)
        sr_sc[rows, :] = sr
        si_sc[rows, :] = si
        return cr, ci

    zero = jnp.zeros((8, SW), f32)
    lax.fori_loop(0, l // 8, step, (zero, zero), unroll=4)


def s5_fwd(proj3, t8, bb, cb, dskip):
    b, l, _ = proj3.shape

    def body(u_ref, t8_ref, bb_ref, cb_ref, d_ref, y_ref, sr_sc, si_sc):
        _s5_input_map(u_ref, bb_ref, sr_sc, si_sc, l)
        _s5_forward_scan(sr_sc, si_sc, t8_ref, l)
        for c in range(l // S5MC):
            rows = slice(c * S5MC, (c + 1) * S5MC)
            y = (_dot_nt(sr_sc[rows, :].astype(bf16), cb_ref[0, 0])
                 - _dot_nt(si_sc[rows, :].astype(bf16), cb_ref[1, 0]))
            y_ref[0, rows, :] = y + d_ref[...] * u_ref[0, rows, :].astype(f32)

    return pl.pallas_call(
        body, name="s5_fwd", grid=(SJ, b),
        in_specs=[pl.BlockSpec((1, l, SU), lambda j, bi: (bi, 0, j)),
                  pl.BlockSpec((2, 8, 8, SW), lambda j, bi: (0, 0, 0, j)),
                  pl.BlockSpec((2, 1, SU, SW), lambda j, bi: (0, j, 0, 0)),
                  pl.BlockSpec((2, 1, SU, SW), lambda j, bi: (0, j, 0, 0)),
                  pl.BlockSpec((1, SU), lambda j, bi: (0, j))],
        out_specs=pl.BlockSpec((1, l, SU), lambda j, bi: (bi, 0, j)),
        out_shape=jax.ShapeDtypeStruct((b, l, BW), f32),
        scratch_shapes=[pltpu.VMEM((l, SW), f32)] * 2, compiler_params=_cp(2))(proj3, t8, bb, cb, dskip)


def s5_bwd(proj3, dy, t8, bb, cb, dskip):
    b, l, _ = proj3.shape
    nt = l // 8

    def body(u_ref, dy_ref, t8_ref, bb_ref, cb_ref, d_ref,
             du_ref, da_ref, dbb_ref, dcb_ref, dd_ref, sr_sc, si_sc, gr_sc, gi_sc):
        bi = pl.program_id(1)

        @pl.when(bi == 0)
        def _():
            da_ref[...] = jnp.zeros_like(da_ref)
            dbb_ref[...] = jnp.zeros_like(dbb_ref)
            dcb_ref[...] = jnp.zeros_like(dcb_ref)
            dd_ref[...] = jnp.zeros_like(dd_ref)

        _s5_input_map(u_ref, bb_ref, sr_sc, si_sc, l)
        _s5_forward_scan(sr_sc, si_sc, t8_ref, l)
        for c in range(l // S5MC):
            rows = slice(c * S5MC, (c + 1) * S5MC)
            dyb = dy_ref[0, rows, :].astype(bf16)
            gr_sc[rows, :] = _dot(dyb, cb_ref[0, 0])
            gi_sc[rows, :] = -_dot(dyb, cb_ref[1, 0])

        row = lax.broadcasted_iota(jnp.int32, (8, SW), 0)

        def step(i, carry):
            cr, ci, dar, dai = carry
            k = nt - 1 - i
            rows = pl.ds(pl.multiple_of(k * 8, 8), 8)
            gr, gi, cr, ci = _scan_tile(gr_sc[rows, :], gi_sc[rows, :], cr, ci, t8_ref, True)
            gr_sc[rows, :] = gr
            gi_sc[rows, :] = gi
            before = pl.ds(pl.multiple_of(jnp.maximum(k - 1, 0) * 8, 8), 8)
            live = jnp.where(k > 0, 1.0, 0.0)
            sr, si = sr_sc[rows, :], si_sc[rows, :]
            spr = jnp.where(row == 0, live * sr_sc[before, :][7:8, :], pltpu.roll(sr, 1, 0))
            spi = jnp.where(row == 0, live * si_sc[before, :][7:8, :], pltpu.roll(si, 1, 0))
            return cr, ci, dar + spr * gr + spi * gi, dai + spr * gi - spi * gr

        zero = jnp.zeros((8, SW), f32)
        _, _, dar, dai = lax.fori_loop(0, nt, step, (zero, zero, zero, zero), unroll=2)
        da_ref[0:1, :] += jnp.sum(dar, axis=0, keepdims=True)
        da_ref[1:2, :] += jnp.sum(dai, axis=0, keepdims=True)

        for c in range(l // S5MC):
            rows = slice(c * S5MC, (c + 1) * S5MC)
            u = u_ref[0, rows, :]
            dyv = dy_ref[0, rows, :]
            dyb = dyv.astype(bf16)
            grb, gib = gr_sc[rows, :].astype(bf16), gi_sc[rows, :].astype(bf16)
            dcb_ref[0, 0] += _dot_tn(dyb, sr_sc[rows, :].astype(bf16))
            dcb_ref[1, 0] -= _dot_tn(dyb, si_sc[rows, :].astype(bf16))
            dbb_ref[0, 0] += _dot_tn(u, grb)
            dbb_ref[1, 0] += _dot_tn(u, gib)
            du = _dot_nt(grb, bb_ref[0, 0]) + _dot_nt(gib, bb_ref[1, 0]) + d_ref[...] * dyv
            du_ref[0, rows, :] = du.astype(bf16)
            dd_ref[...] += jnp.sum(dyv * u.astype(f32), axis=0, keepdims=True)

    seq = pl.BlockSpec((1, l, SU), lambda j, bi: (bi, 0, j))
    tab = pl.BlockSpec((2, 1, SU, SW), lambda j, bi: (0, j, 0, 0))
    return pl.pallas_call(
        body, name="s5_bwd", grid=(SJ, b),
        in_specs=[seq, seq, pl.BlockSpec((2, 8, 8, SW), lambda j, bi: (0, 0, 0, j)), tab, tab,
                  pl.BlockSpec((1, SU), lambda j, bi: (0, j))],
        out_specs=[seq, pl.BlockSpec((2, SW), lambda j, bi: (0, j)), tab, tab,
                   pl.BlockSpec((1, SU), lambda j, bi: (0, j))],
        out_shape=[jax.ShapeDtypeStruct((b, l, BW), bf16), jax.ShapeDtypeStruct((2, NSTATE), f32),
                   jax.ShapeDtypeStruct((2, SJ, SU, SW), f32), jax.ShapeDtypeStruct((2, SJ, SU, SW), f32),
                   jax.ShapeDtypeStruct((1, BW), f32)],
        scratch_shapes=[pltpu.VMEM((l, SW), f32)] * 4,
        compiler_params=_cp(2))(proj3, dy, t8, bb, cb, dskip)


AHC = 2
AHW = AHC * 128


def _att_mask(n, nb):
    if nb == 1:
        qi = lax.broadcasted_iota(jnp.int32, (ABLK, ABLK), 0)
        kj = lax.broadcasted_iota(jnp.int32, (ABLK, ABLK), 1)
        return kj <= qi
    qi = lax.broadcasted_iota(jnp.int32, (ABLK, 2 * ABLK), 0)
    kj = lax.broadcasted_iota(jnp.int32, (ABLK, 2 * ABLK), 1)
    return (kj >= qi) & (kj <= qi + ABLK) & ((n > 0) | (kj >= ABLK))


def _att_rows(it, nb, dil):
    r, n = it // nb, it % nb
    cur = pl.ds(r + n * (ABLK * dil), ABLK, stride=dil)
    prv = pl.ds(r + jnp.maximum(n - 1, 0) * (ABLK * dil), ABLK, stride=dil)
    return n, cur, prv


def _att_keys(ref, c, cur, prv, nb):
    if nb == 1:
        return ref[c, cur, :].astype(bf16)
    return jnp.concatenate([ref[c, prv, :], ref[c, cur, :]], axis=0).astype(bf16)


def _to_chunks(src_ref, dst):
    for c in range(AHC):
        dst[c] = src_ref[0, :, c * 128:(c + 1) * 128].astype(f32)


def att_fwd(proj3, g_idx, dil):
    b, l, _ = proj3.shape
    nb = l // dil // ABLK
    nhalf = BW // AHW

    def body(q_ref, k_ref, v_ref, o_ref, lse_ref, qf, kf, vf, of):
        hh = pl.program_id(1)
        _to_chunks(q_ref, qf)
        _to_chunks(k_ref, kf)
        _to_chunks(v_ref, vf)
        lane = lax.broadcasted_iota(jnp.int32, (ABLK, 128), 1)

        def step(it, carry):
            n, cur, prv = _att_rows(it, nb, dil)
            valid = _att_mask(n, nb)
            lse_all = jnp.zeros((ABLK, 128), f32)
            for c in range(AHC):
                q = qf[c, cur, :].astype(bf16)
                k = _att_keys(kf, c, cur, prv, nb)
                v = _att_keys(vf, c, cur, prv, nb)
                outs = []
                for hl in range(2):
                    hs = slice(hl * HD, (hl + 1) * HD)
                    s = jnp.where(valid, _dot_nt(q[:, hs], k[:, hs]) * ATT_SCALE, NEG)
                    m = jnp.max(s, axis=-1, keepdims=True)
                    p = jnp.exp(s - m)
                    den = jnp.sum(p, axis=-1, keepdims=True)
                    outs.append(_dot(p.astype(bf16), v[:, hs]) / den)
                    lse_all = lse_all + jnp.where(lane == hh * (2 * AHC) + 2 * c + hl, m + jnp.log(den), 0.0)
                of[c, cur, :] = jnp.concatenate(outs, axis=1)

            @pl.when(hh == 0)
            def _():
                lse_ref[0, cur, :] = lse_all

            @pl.when(hh > 0)
            def _():
                lse_ref[0, cur, :] += lse_all

            return carry

        lax.fori_loop(0, dil * nb, step, 0, unroll=2)
        for c in range(AHC):
            o_ref[0, :, c * 128:(c + 1) * 128] = of[c].astype(bf16)

    col = lambda c: pl.BlockSpec((1, l, AHW), lambda bi, hh: (bi, 0, c * nhalf + hh))
    return pl.pallas_call(
        body, name=f"att_fwd{g_idx}", grid=(b, nhalf),
        in_specs=[col(1 + g_idx), col(4), col(5)],
        out_specs=[pl.BlockSpec((1, l, AHW), lambda bi, hh: (bi, 0, hh)),
                   pl.BlockSpec((1, l, 128), lambda bi, hh: (bi, 0, 0))],
        out_shape=[jax.ShapeDtypeStruct((b, l, BW), bf16), jax.ShapeDtypeStruct((b, l, 128), f32)],
        scratch_shapes=[pltpu.VMEM((AHC, l, 128), f32)] * 4,
        compiler_params=_cp(2))(proj3, proj3, proj3)


def att_bwd(proj3, do, lse_tot, delta, g_idx, dil):
    b, l, _ = proj3.shape
    nb = l // dil // ABLK
    nhalf = BW // AHW

    def body(q_ref, k_ref, v_ref, do_ref, l_ref, dl_ref, dq_ref, dk_ref, dv_ref, qf, kf, vf, dof):
        hh = pl.program_id(1)
        _to_chunks(q_ref, qf)
        _to_chunks(k_ref, kf)
        _to_chunks(v_ref, vf)
        _to_chunks(do_ref, dof)
        dk_ref[...] = jnp.zeros_like(dk_ref)
        dv_ref[...] = jnp.zeros_like(dv_ref)
        lane = lax.broadcasted_iota(jnp.int32, (ABLK, 128), 1)

        def step(it, carry):
            n, cur, prv = _att_rows(it, nb, dil)
            valid = _att_mask(n, nb)
            lse_b = l_ref[0, cur, :]
            dl_b = dl_ref[0, cur, :]
            for c in range(AHC):
                q = qf[c, cur, :].astype(bf16)
                dob = dof[c, cur, :].astype(bf16)
                k = _att_keys(kf, c, cur, prv, nb)
                v = _att_keys(vf, c, cur, prv, nb)
                dqs, dks, dvs = [], [], []
                for hl in range(2):
                    hs = slice(hl * HD, (hl + 1) * HD)
                    mine = lane == hh * (2 * AHC) + 2 * c + hl
                    lse_h = jnp.sum(jnp.where(mine, lse_b, 0.0), axis=-1, keepdims=True)
                    dl_h = jnp.sum(jnp.where(mine, dl_b, 0.0), axis=-1, keepdims=True)
                    s = _dot_nt(q[:, hs], k[:, hs]) * ATT_SCALE
                    p = jnp.where(valid, jnp.exp(jnp.minimum(s - lse_h, 60.0)), 0.0)
                    dp = _dot_nt(dob[:, hs], v[:, hs])
                    ds = (p * (dp - dl_h) * ATT_SCALE).astype(bf16)
                    dqs.append(_dot(ds, k[:, hs]))
                    dks.append(_dot_tn(ds, q[:, hs]))
                    dvs.append(_dot_tn(p.astype(bf16), dob[:, hs]))
                dq_ref[0, c, cur, :] = jnp.concatenate(dqs, axis=1)
                dk = jnp.concatenate(dks, axis=1)
                dv = jnp.concatenate(dvs, axis=1)
                if nb == 1:
                    dk_ref[0, c, cur, :] += dk
                    dv_ref[0, c, cur, :] += dv
                else:
                    dk_ref[0, c, cur, :] += dk[ABLK:]
                    dv_ref[0, c, cur, :] += dv[ABLK:]

                    @pl.when(n > 0)
                    def _():
                        dk_ref[0, c, prv, :] += dk[:ABLK]
                        dv_ref[0, c, prv, :] += dv[:ABLK]

            return carry

        lax.fori_loop(0, dil * nb, step, 0, unroll=2)

    col = lambda c: pl.BlockSpec((1, l, AHW), lambda bi, hh: (bi, 0, c * nhalf + hh))
    own = pl.BlockSpec((1, l, AHW), lambda bi, hh: (bi, 0, hh))
    own128 = pl.BlockSpec((1, l, 128), lambda bi, hh: (bi, 0, 0))
    chunked = pl.BlockSpec((1, AHC, l, 128), lambda bi, hh: (bi, hh, 0, 0))
    return pl.pallas_call(
        body, name=f"att_bwd{g_idx}", grid=(b, nhalf),
        in_specs=[col(1 + g_idx), col(4), col(5), own, own128, own128],
        out_specs=[chunked] * 3,
        out_shape=[jax.ShapeDtypeStruct((b, BW // 128, l, 128), f32)] * 3,
        scratch_shapes=[pltpu.VMEM((AHC, l, 128), f32)] * 4,
        compiler_params=_cp(2, 56))(proj3, proj3, proj3, do, lse_tot, delta)


CPAD = 32
CTAIL = 16
CR = 128
CSLAB = CR + 40


def _tap_windows(slab, off, mis):
    ntap = (CW - 1 - mis) // 8 + 1
    xb = slab[off + mis:off + mis + CR + 8 * (ntap - 1)]
    for a in range(ntap):
        yield 8 * a + mis, xb[8 * a:8 * a + CR]


def _fill_glu(cv_ref, pad, l):
    pad[0:CPAD, :] = jnp.zeros((CPAD, BW), f32)
    pad[CPAD:CPAD + l, :] = cv_ref[0, :, :BW].astype(f32) * _sigmoid(cv_ref[0, :, BW:].astype(f32))
    pad[CPAD + l:, :] = jnp.zeros((CTAIL, BW), f32)


def conv_fwd(proj3, cw, cb):
    b, l, _ = proj3.shape

    def body(cv_ref, w_ref, b_ref, o_ref, pad):
        _fill_glu(cv_ref, pad, l)
        for lc in range(BW // 128):
            lanes = slice(lc * 128, (lc + 1) * 128)
            wv = w_ref[:, lanes]

            def step(c, carry):
                base = pl.multiple_of(c * CR, CR)
                slab = pad[pl.ds(base, CSLAB), lanes]
                acc = jnp.zeros((CR, 128), f32) + b_ref[:, lanes]
                for mis in range(8):
                    for k, win in _tap_windows(slab, CPAD - (CW - 1), mis):
                        acc = acc + wv[k:k + 1] * win
                o_ref[0, pl.ds(base, CR), lanes] = acc
                return carry

            lax.fori_loop(0, l // CR, step, 0)

    return pl.pallas_call(
        body, name="conv_fwd", grid=(b,),
        in_specs=[pl.BlockSpec((1, l, 2 * BW), lambda i: (i, 0, 3)),
                  pl.BlockSpec((32, BW), lambda i: (0, 0)), pl.BlockSpec((1, BW), lambda i: (0, 0))],
        out_specs=pl.BlockSpec((1, l, BW), lambda i: (i, 0, 0)),
        out_shape=jax.ShapeDtypeStruct((b, l, BW), f32),
        scratch_shapes=[pltpu.VMEM((CPAD + l + CTAIL, BW), f32)], compiler_params=_cp(1))(proj3, cw, cb)


def conv_bwd(proj3, dhc, cw):
    b, l, _ = proj3.shape

    def body(cv_ref, d_ref, w_ref, dcv_ref, dw_ref, db_ref, pad, dpad):
        i = pl.program_id(0)

        @pl.when(i == 0)
        def _():
            dw_ref[...] = jnp.zeros_like(dw_ref)
            db_ref[...] = jnp.zeros_like(db_ref)

        _fill_glu(cv_ref, pad, l)
        dpad[0:l, :] = d_ref[0]
        dpad[l:, :] = jnp.zeros((CPAD + CTAIL, BW), f32)
        db_ref[...] += jnp.sum(d_ref[0], axis=0, keepdims=True)
        for lc in range(BW // 128):
            lanes = slice(lc * 128, (lc + 1) * 128)
            glanes = slice(BW + lc * 128, BW + (lc + 1) * 128)
            wv = w_ref[:, lanes]

            for mis in range(8):
                ntap = (CW - 1 - mis) // 8 + 1

                def dw_step(c, accs, mis=mis, lanes=lanes):
                    base = pl.multiple_of(c * CR, CR)
                    slab = pad[pl.ds(base, CSLAB), lanes]
                    dv = dpad[pl.ds(base, CR), lanes]
                    return tuple(acc + (dv * win).reshape(CR // 8, 8, 128).sum(axis=0) for acc, (_, win)
                                 in zip(accs, _tap_windows(slab, CPAD - (CW - 1), mis)))

                accs = lax.fori_loop(0, l // CR, dw_step, tuple(jnp.zeros((8, 128), f32) for _ in range(ntap)))
                for a in range(ntap):
                    k = 8 * a + mis
                    dw_ref[k:k + 1, lanes] += jnp.sum(accs[a], axis=0, keepdims=True)

            def dh_step(c, carry, lanes=lanes, glanes=glanes, wv=wv):
                base = pl.multiple_of(c * CR, CR)
                slab = dpad[pl.ds(base, CSLAB), lanes]
                acc = jnp.zeros((CR, 128), f32)
                for mis in range(8):
                    for kk, win in _tap_windows(slab, 0, mis):
                        acc = acc + wv[CW - 1 - kk:CW - kk] * win
                rows = pl.ds(base, CR)
                a = cv_ref[0, rows, lanes].astype(f32)
                sg = _sigmoid(cv_ref[0, rows, glanes].astype(f32))
                dcv_ref[0, rows, lanes] = (acc * sg).astype(bf16)
                dcv_ref[0, rows, glanes] = (acc * a * sg * (1.0 - sg)).astype(bf16)
                return carry

            lax.fori_loop(0, l // CR, dh_step, 0)

    return pl.pallas_call(
        body, name="conv_bwd", grid=(b,),
        in_specs=[pl.BlockSpec((1, l, 2 * BW), lambda i: (i, 0, 3)),
                  pl.BlockSpec((1, l, BW), lambda i: (i, 0, 0)),
                  pl.BlockSpec((32, BW), lambda i: (0, 0))],
        out_specs=[pl.BlockSpec((1, l, 2 * BW), lambda i: (i, 0, 0)),
                   pl.BlockSpec((32, BW), lambda i: (0, 0)), pl.BlockSpec((1, BW), lambda i: (0, 0))],
        out_shape=[jax.ShapeDtypeStruct((b, l, 2 * BW), bf16), jax.ShapeDtypeStruct((32, BW), f32),
                   jax.ShapeDtypeStruct((1, BW), f32)],
        scratch_shapes=[pltpu.VMEM((CPAD + l + CTAIL, BW), f32), pltpu.VMEM((l + CPAD + CTAIL, BW), f32)],
        compiler_params=_cp(1))(proj3, dhc, cw)


def _head_expand():
    r = lax.broadcasted_iota(jnp.int32, (128, BW), 0)
    c = lax.broadcasted_iota(jnp.int32, (128, BW), 1) // HD
    return (r == c).astype(f32)


def _head_reduce():
    r = lax.broadcasted_iota(jnp.int32, (BW, 128), 0) // HD
    c = lax.broadcasted_iota(jnp.int32, (BW, 128), 1)
    return (r == c).astype(f32)


def _merge_common(ys_ref, o_refs, l_refs, hc_ref, g_refs, bg_ref, lng_ref, lnb_ref, wglu_ref, watt_ref, wpw_ref):
    r = {}
    ysv = ys_ref[...]
    r["ys"] = ysv
    r["ysin"] = _gelu(ysv).astype(bf16)
    z = _dot(r["ysin"], wglu_ref[...])
    r["z1"], r["sg2"] = z[:, :D], _sigmoid(z[:, D:])
    r["y_s"] = r["z1"] * r["sg2"]
    ls = [lr_[...] for lr_ in l_refs]
    mx = jnp.maximum(jnp.maximum(ls[0], ls[1]), ls[2])
    es = [jnp.exp(v - mx) for v in ls]
    tot = es[0] + es[1] + es[2]
    r["lse_tot"] = mx + jnp.log(tot)
    e_mat = _head_expand()
    o = jnp.zeros(ysv.shape, f32)
    for e, o_ref in zip(es, o_refs):
        o = o + _dot_hi(e / tot, e_mat) * o_ref[...].astype(f32)
    r["o"] = o
    r["ob"] = o.astype(bf16)
    r["y_a"] = _dot(r["ob"], watt_ref[...])
    hc = hc_ref[...]
    mu = jnp.mean(hc, axis=-1, keepdims=True)
    xc = hc - mu
    rstd = lax.rsqrt(jnp.mean(xc * xc, axis=-1, keepdims=True) + EPS)
    r["xh"], r["rstd"] = xc * rstd, rstd
    hn = r["xh"] * lng_ref[...] + lnb_ref[...]
    r["hn"] = hn
    r["sgn"] = _sigmoid(hn)
    r["hs"] = (hn * r["sgn"]).astype(bf16)
    r["y_c"] = _dot(r["hs"], wpw_ref[...])
    r["gates"] = [_sigmoid(g_refs[k][...].astype(f32) + bg_ref[:, k * D:(k + 1) * D]) for k in range(3)]
    r["merged"] = r["gates"][0] * r["y_s"] + r["gates"][1] * r["y_a"] + r["gates"][2] * r["y_c"]
    return r


TBM = 256


def _merge_in_specs(tok, tb):
    w = lambda shape: pl.BlockSpec(shape, lambda i: (0, 0))
    return ([pl.BlockSpec((tb, D), tok), pl.BlockSpec((tb, BW), tok)]
            + [pl.BlockSpec((tb, BW), tok)] * 3 + [pl.BlockSpec((tb, 128), tok)] * 3
            + [pl.BlockSpec((tb, BW), tok)]
            + [pl.BlockSpec((tb, D), lambda i, k=k: (i, 4 + k)) for k in range(3)]
            + [w((1, 3 * D)), w((1, BW)), w((1, BW)), w((BW, 2 * D)), w((BW, D)), w((BW, D)), w((D, D))])


def merge_fwd(x, ys, os_, lses, hc, proj, bg, lng, lnb, wglu, watt, wpw, wout):
    n = x.shape[0]

    def body(x_ref, ys_ref, o1, o2, o3, l1, l2, l3, hc_ref, g0, g1, g2, bg_ref, lng_ref, lnb_ref,
             wglu_ref, watt_ref, wpw_ref, wout_ref, x1_ref):
        r = _merge_common(ys_ref, (o1, o2, o3), (l1, l2, l3), hc_ref, (g0, g1, g2), bg_ref, lng_ref, lnb_ref,
                          wglu_ref, watt_ref, wpw_ref)
        x1_ref[...] = x_ref[...] + _dot(r["merged"].astype(bf16), wout_ref[...])

    tok = lambda i: (i, 0)
    return pl.pallas_call(
        body, name="merge_fwd", grid=(n // TB,), in_specs=_merge_in_specs(tok, TB),
        out_specs=pl.BlockSpec((TB, D), tok), out_shape=jax.ShapeDtypeStruct((n, D), f32),
        compiler_params=_cp(1, 56))(x, ys, *os_, *lses, hc, proj, proj, proj, bg, lng, lnb, wglu, watt, wpw, wout)


def merge_bwd(dx1, ys, os_, lses, hc, proj, bg, lng, lnb, wglu, watt, wpw, wout):
    n = dx1.shape[0]

    def body(dx_ref, ys_ref, o1, o2, o3, l1, l2, l3, hc_ref, g0, g1, g2, bg_ref, lng_ref, lnb_ref,
             wglu_ref, watt_ref, wpw_ref, wout_ref,
             dys_ref, do_ref, delta_ref, ltot_ref, dhc_ref, dgate_ref, ysin_ref, dz_ref, ob_ref, dya_ref,
             hs_ref, dyc_ref, mg_ref, dbg_ref, dlng_ref, dlnb_ref):
        i = pl.program_id(0)

        @pl.when(i == 0)
        def _():
            dbg_ref[...] = jnp.zeros_like(dbg_ref)
            dlng_ref[...] = jnp.zeros_like(dlng_ref)
            dlnb_ref[...] = jnp.zeros_like(dlnb_ref)

        r = _merge_common(ys_ref, (o1, o2, o3), (l1, l2, l3), hc_ref, (g0, g1, g2), bg_ref, lng_ref, lnb_ref,
                          wglu_ref, watt_ref, wpw_ref)
        mg_ref[...] = r["merged"].astype(bf16)
        ysin_ref[...] = r["ysin"]
        ob_ref[...] = r["ob"]
        hs_ref[...] = r["hs"]
        ltot_ref[...] = r["lse_tot"]
        dm = _dot_nt(dx_ref[...].astype(bf16), wout_ref[...])
        ys3 = (r["y_s"], r["y_a"], r["y_c"])
        for k in range(3):
            gk = r["gates"][k]
            dgr = dm * ys3[k] * gk * (1.0 - gk)
            dgate_ref[:, k * D:(k + 1) * D] = dgr.astype(bf16)
            dbg_ref[:, k * D:(k + 1) * D] += jnp.sum(dgr, axis=0, keepdims=True)
        dy_s = dm * r["gates"][0]
        sg2 = r["sg2"]
        dz = jnp.concatenate([dy_s * sg2, dy_s * r["z1"] * sg2 * (1.0 - sg2)], axis=1).astype(bf16)
        dz_ref[...] = dz
        dys_ref[...] = _dot_nt(dz, wglu_ref[...]) * _gelu_grad(r["ys"])
        dya = (dm * r["gates"][1]).astype(bf16)
        dya_ref[...] = dya
        do = _dot_nt(dya, watt_ref[...])
        do_ref[...] = do.astype(bf16)
        delta_ref[...] = _dot_hi(do * r["o"], _head_reduce())
        dyc = (dm * r["gates"][2]).astype(bf16)
        dyc_ref[...] = dyc
        sgn, hn = r["sgn"], r["hn"]
        dhn = _dot_nt(dyc, wpw_ref[...]) * sgn * (1.0 + hn * (1.0 - sgn))
        dlng_ref[...] += jnp.sum(dhn * r["xh"], axis=0, keepdims=True)
        dlnb_ref[...] += jnp.sum(dhn, axis=0, keepdims=True)
        dxh = dhn * lng_ref[...]
        xh = r["xh"]
        dhc_ref[...] = r["rstd"] * (dxh - jnp.mean(dxh, axis=-1, keepdims=True)
                                    - xh * jnp.mean(dxh * xh, axis=-1, keepdims=True))

    tok = lambda i: (i, 0)
    fix = lambda i: (0, 0)
    outs = [("dys", BW, f32), ("do", BW, bf16), ("delta", 128, f32), ("lse_tot", 128, f32), ("dhc", BW, f32),
            ("dgate", 3 * D, bf16), ("ysin", BW, bf16), ("dz", 2 * D, bf16), ("ob", BW, bf16), ("dya", D, bf16),
            ("hs", BW, bf16), ("dyc", D, bf16), ("merged", D, bf16)]
    small = [("dbg", 3 * D), ("dlng", BW), ("dlnb", BW)]
    res = pl.pallas_call(
        body, name="merge_bwd", grid=(n // TBM,), in_specs=_merge_in_specs(tok, TBM),
        out_specs=[pl.BlockSpec((TBM, w), tok) for _, w, _ in outs] + [pl.BlockSpec((1, w), fix) for _, w in small],
        out_shape=[jax.ShapeDtypeStruct((n, w), dt) for _, w, dt in outs]
        + [jax.ShapeDtypeStruct((1, w), f32) for _, w in small],
        compiler_params=_cp(1, 56))(dx1, ys, *os_, *lses, hc, proj, proj, proj, bg, lng, lnb, wglu, watt, wpw, wout)
    return dict(zip([k for k, _, _ in outs] + [k for k, _ in small], res))


def assemble_dproj(du, dqs, dks, dvs, dcv, dgate):
    b, l, _ = du.shape
    nck = BW // 128

    def body(du_ref, q1, q2, q3, k1, k2, k3, v1, v2, v3, cv_ref, g_ref, o_ref):
        o_ref[0, :, 0:BW] = du_ref[0]
        for c in range(nck):
            for j, qr in enumerate((q1, q2, q3)):
                o_ref[0, :, (1 + j) * BW + c * 128:(1 + j) * BW + (c + 1) * 128] = qr[0, c].astype(bf16)
            o_ref[0, :, 4 * BW + c * 128:4 * BW + (c + 1) * 128] = (k1[0, c] + k2[0, c] + k3[0, c]).astype(bf16)
            o_ref[0, :, 5 * BW + c * 128:5 * BW + (c + 1) * 128] = (v1[0, c] + v2[0, c] + v3[0, c]).astype(bf16)
        o_ref[0, :, 6 * BW:8 * BW] = cv_ref[0]
        o_ref[0, :, 8 * BW:] = g_ref[0]

    t = lambda w: pl.BlockSpec((1, TB, w), lambda bi, i: (bi, i, 0))
    ck = pl.BlockSpec((1, nck, TB, 128), lambda bi, i: (bi, 0, i, 0))
    return pl.pallas_call(
        body, name="assemble_dproj", grid=(b, l // TB),
        in_specs=[t(BW)] + [ck] * 9 + [t(2 * BW), t(3 * D)], out_specs=t(INC),
        out_shape=jax.ShapeDtypeStruct((b, l, INC), bf16), compiler_params=_cp(2))(du, *dqs, *dks, *dvs, dcv, dgate)


def _me():
    return lax.axis_index("x"), lax.axis_index("y"), lax.axis_index("c")


def _peers():
    x, y, c = _me()
    return [(x, y, 1 - c), (1 - x, y, c), (1 - x, y, 1 - c), (x, 1 - y, c), (x, 1 - y, 1 - c),
            (1 - x, 1 - y, c), (1 - x, 1 - y, 1 - c)]


def _rank(p):
    return 4 * p[0] + 2 * p[1] + p[2]


def allgather(arrs, name):
    na = len(arrs)
    units = [(a, j) for a in range(na) for j in range(arrs[a].shape[0])]
    nu = len(units)

    def body(*refs):
        ins, outs = refs[:na], refs[na:2 * na]
        send, recv, loc = refs[2 * na:]
        me = _rank(_me())
        local, remote = [], []
        for u, (a, j) in enumerate(units):
            own = pltpu.make_async_copy(ins[a].at[j], outs[a].at[j, me], loc.at[u])
            own.start()
            local.append(own)
        for u, (a, j) in enumerate(units):
            for k, p in enumerate(_peers()):
                cp = pltpu.make_async_remote_copy(src_ref=ins[a].at[j], dst_ref=outs[a].at[j, me],
                                                  send_sem=send.at[u, k], recv_sem=recv.at[u, k],
                                                  device_id=p, device_id_type=MESH)
                cp.start()
                remote.append(cp)
        for cp in local:
            cp.wait()
        for cp in remote:
            cp.wait()

    return pl.pallas_call(
        body, name=name, in_specs=[ANY] * na, out_specs=[ANY] * na,
        out_shape=[jax.ShapeDtypeStruct((a.shape[0], NDEV) + a.shape[1:], a.dtype) for a in arrs],
        scratch_shapes=[pltpu.SemaphoreType.DMA((nu, NDEV - 1)), pltpu.SemaphoreType.DMA((nu, NDEV - 1)),
                        pltpu.SemaphoreType.DMA((nu,))])(*arrs)


HBM = pl.BlockSpec(memory_space=pltpu.HBM)
SEM = pl.BlockSpec(memory_space=pltpu.SEMAPHORE)
_EFFECT = pltpu.SideEffectType.DATAFLOW_SIDE_EFFECTING


def _push_copies(srcs, lands, send, recv, scatter):
    me = _rank(_me())
    out = []
    for i in range(len(srcs)):
        for k, p in enumerate(_peers()):
            src = srcs[i].at[_rank(p)] if scatter else srcs[i]
            dst = lands[i].at[k] if scatter else lands[i].at[me]
            j = i * (NDEV - 1) + k
            out.append(pltpu.make_async_remote_copy(src_ref=src, dst_ref=dst, send_sem=send.at[j],
                                                    recv_sem=recv.at[j], device_id=p, device_id_type=MESH))
    return out


def push_start(srcs, lands, scatter, name, token):
    n = len(srcs)
    token = jnp.zeros((8, 128), f32) if token is None else token

    def body(*refs):
        for cp in _push_copies(refs[:n], refs[n:2 * n], refs[2 * n + 1], refs[2 * n + 2], scatter):
            cp.start()
        refs[-1][...] = refs[2 * n][...]

    sems = pltpu.SemaphoreType.DMA((n * (NDEV - 1),))
    vmem = pl.BlockSpec(memory_space=pltpu.VMEM)
    res = pl.pallas_call(
        body, name=name, in_specs=[HBM] * (2 * n) + [vmem], out_specs=[SEM, SEM] + [HBM] * (2 * n) + [vmem],
        out_shape=[sems, sems] + [pltpu.HBM(a.shape, a.dtype) for a in list(srcs) + list(lands)]
        + [jax.ShapeDtypeStruct((8, 128), f32)],
        input_output_aliases={i: 2 + i for i in range(2 * n)},
        compiler_params=pltpu.CompilerParams(has_side_effects=_EFFECT),
    )(*[pltpu.with_memory_space_constraint(a, pltpu.HBM) for a in list(srcs) + list(lands)], token)
    return res[0], res[1], res[2:2 + n], res[2 + n:2 + 2 * n], res[-1]


def push_wait(send, recv, srcs, lands, after, scatter, name):
    n = len(srcs)

    def body(*refs):
        for cp in _push_copies(refs[:n], refs[n:2 * n], refs[2 * n], refs[2 * n + 1], scatter):
            cp.wait_send()
            cp.wait_recv()

    res = pl.pallas_call(
        body, name=name, in_specs=[HBM] * (2 * n) + [SEM, SEM, ANY], out_specs=[HBM] * (2 * n),
        out_shape=[pltpu.HBM(a.shape, a.dtype) for a in list(srcs) + list(lands)],
        input_output_aliases={i: i for i in range(2 * n)},
        compiler_params=pltpu.CompilerParams(has_side_effects=_EFFECT),
    )(*srcs, *lands, send, recv, after)
    return res[:n], res[n:]


_C1 = 1.0 / (1.0 - ADAM_B1 ** ADAM_STEP)
_C2 = 1.0 / (1.0 - ADAM_B2 ** ADAM_STEP)


def _adamw(w, g, m, v):
    m = ADAM_B1 * m + (1.0 - ADAM_B1) * g
    v = ADAM_B2 * v + (1.0 - ADAM_B2) * (g * g)
    delta = -ADAM_LR * ((m * _C1) / (jnp.sqrt(v * _C2) + ADAM_EPS) + ADAM_WD * w)
    return delta, m, v


def adam_big(lands, owns, w, m, v, name):
    _, k, n = lands[0].shape
    tk = k
    while tk * n * 2 * NDEV > 2 * 1024 * 1024 and tk % 16 == 0:
        tk //= 2

    def body(*refs):
        l_refs, o_refs = refs[:DEPTH], refs[DEPTH:2 * DEPTH]
        w_ref, m_ref, v_ref, g_ref, d_ref, nm_ref, nv_ref = refs[2 * DEPTH:]
        for l in range(DEPTH):
            g = o_refs[l][...].astype(f32)
            for s in range(NDEV - 1):
                g = g + l_refs[l][s].astype(f32)
            d, nm, nv = _adamw(w_ref[l], g, m_ref[l], v_ref[l])
            g_ref[l], d_ref[l], nm_ref[l], nv_ref[l] = g, d, nm, nv

    blk = pl.BlockSpec((DEPTH, tk, n), lambda i: (0, i, 0))
    return pl.pallas_call(
        body, name=name, grid=(k // tk,),
        in_specs=[pl.BlockSpec((NDEV - 1, tk, n), lambda i: (0, i, 0))] * DEPTH
        + [pl.BlockSpec((tk, n), lambda i: (i, 0))] * DEPTH + [blk, blk, blk],
        out_specs=[blk] * 4, out_shape=[jax.ShapeDtypeStruct(w.shape, f32)] * 4,
        compiler_params=_cp(1))(*lands, *owns, w, m, v)


def adam_small(gath, w, m, v):
    r = w.shape[0]
    tr = 512

    def body(g_ref, w_ref, m_ref, v_ref, go_ref, d_ref, nm_ref, nv_ref):
        g = g_ref[0]
        for s in range(1, NDEV):
            g = g + g_ref[s]
        d, nm, nv = _adamw(w_ref[...], g, m_ref[...], v_ref[...])
        go_ref[...], d_ref[...], nm_ref[...], nv_ref[...] = g, d, nm, nv

    blk = pl.BlockSpec((tr, 128), lambda i: (i, 0))
    return pl.pallas_call(
        body, name="adam_small", grid=(r // tr,),
        in_specs=[pl.BlockSpec((NDEV, tr, 128), lambda i: (0, i, 0)), blk, blk, blk],
        out_specs=[blk] * 4, out_shape=[jax.ShapeDtypeStruct((r, 128), f32)] * 4,
        compiler_params=_cp(1))(gath, w, m, v)


SMALL = ["norm1_g", "b_gate", "ssm_lambda_re", "ssm_lambda_im", "ssm_log_dt", "ssm_b_re", "ssm_b_im",
         "ssm_c_re", "ssm_c_im", "ssm_d", "conv_w", "conv_b", "conv_ln_g", "conv_ln_b", "norm2_g", "final_g"]
BIG = ["w_in", "w_ssm_glu", "w_att_up", "w_conv_pw2", "w_out", "w_ffn_in", "w_ffn_out"]
ORDER = ["norm1_g", "w_in", "b_gate", "ssm_lambda_re", "ssm_lambda_im", "ssm_log_dt", "ssm_b_re", "ssm_b_im",
         "ssm_c_re", "ssm_c_im", "ssm_d", "w_ssm_glu", "w_att_up", "conv_w", "conv_b", "conv_ln_g", "conv_ln_b",
         "w_conv_pw2", "w_out", "norm2_g", "w_ffn_in", "w_ffn_out", "final_g"]
PACK_ROWS = 2560


def _pack(arrs):
    flat = jnp.concatenate([a.reshape(-1).astype(f32) for a in arrs])
    return jnp.pad(flat, (0, PACK_ROWS * 128 - flat.shape[0])).reshape(PACK_ROWS, 128)


def _unpack(pack, shapes):
    flat = pack.reshape(-1)
    out, off = [], 0
    for s in shapes:
        sz = math.prod(s)
        out.append(flat[off:off + sz].reshape(s))
        off += sz
    return out


def _bt(b):
    return b.transpose(2, 0, 1).reshape(GH, NSTATE)


def _bt_inv(bt):
    return bt.reshape(GH, NG, NS).transpose(1, 2, 0)


def _ct(c):
    return c.transpose(1, 0, 2).reshape(GH, NSTATE)


def _ct_inv(ct):
    return ct.reshape(GH, NG, NS).transpose(1, 0, 2)


def local_step(x, loss_target, P, weights, on_grads):
    bsz, seq, _ = x.shape
    n = bsz * seq

    def natural(g3):
        return g3.transpose(1, 0, 2).reshape(g3.shape[1], NDEV * g3.shape[2])

    xs = x.reshape(n, D)
    saved = []
    conv_w_pad = None
    for l in range(DEPTH):
        S = {"x": xs}
        h1 = rms_fwd(xs, P["norm1_g"][l][None])
        G = dict(weights(l, "in", h1))
        if conv_w_pad is None:
            conv_w_full = G["conv_w"].transpose(1, 2, 0, 3).reshape(DEPTH, CW, BW)
            conv_w_pad = jnp.pad(conv_w_full, ((0, 0), (0, 1), (0, 0)))
        w_in4 = G["w_in"][None]
        proj = inproj(h1, w_in4, 0)
        proj3 = proj.reshape(bsz, seq, INC)
        lr = P["ssm_lambda_re"][l].reshape(1, NSTATE)
        li = P["ssm_lambda_im"][l].reshape(1, NSTATE)
        ld = jnp.repeat(P["ssm_log_dt"][l], NS).reshape(1, NSTATE)
        btr, bti = _bt(P["ssm_b_re"][l]), _bt(P["ssm_b_im"][l])
        t8, bb, cb = s5_params(lr, li, ld, btr, bti, _ct(P["ssm_c_re"][l]), _ct(P["ssm_c_im"][l]))
        dskip = P["ssm_d"][l][None]
        ys = s5_fwd(proj3, t8, bb, cb, dskip)
        att = [att_fwd(proj3, gi, dil) for gi, (_, dil) in enumerate(PATTERNS)]
        hc = conv_fwd(proj3, conv_w_pad[l], P["conv_b"][l][None])
        G.update(weights(l, "mix", hc))
        wts = dict(wglu=natural(G["w_ssm_glu"]), watt=natural(G["w_att_up"]),
                   wpw=natural(G["w_conv_pw2"]), wout=G["w_out"].reshape(D, D))
        mi = dict(ys=ys.reshape(n, BW), os_=[a[0].reshape(n, BW) for a in att],
                  lses=[a[1].reshape(n, 128) for a in att], hc=hc.reshape(n, BW),
                  proj=proj, bg=P["b_gate"][l][None], lng=P["conv_ln_g"][l][None], lnb=P["conv_ln_b"][l][None],
                  **wts)
        x1 = merge_fwd(xs, **mi)
        G.update(weights(l, "ffn", x1))
        w_ffn = (G["w_ffn_in"][None], G["w_ffn_out"][None])
        x2 = ffn_fwd(x1, P["norm2_g"][l][None], *w_ffn, 0)
        S.update(h1=h1, proj=proj, proj3=proj3, tabs=(t8, bb, cb), mi=mi, x1=x1, w_in4=w_in4, w_ffn=w_ffn,
                 sp=(lr, li, ld, btr, bti), dskip=dskip)
        saved.append(S)
        xs = x2

    loss8, dx, dfinal = loss_head(xs, P["final_g"][None], loss_target.reshape(n, D))

    small_g = {k: [None] * DEPTH for k in SMALL if k != "final_g"}
    tokblk = lambda w: pl.BlockSpec((1024, w), lambda s, i: (i, 0))
    colblk = lambda w: pl.BlockSpec((1024, w), lambda s, i: (i, s))
    sh3blk = lambda w: pl.BlockSpec((1, 1024, w), lambda s, i: (s, i, 0))
    for l in reversed(range(DEPTH)):
        S = saved[l]
        g2 = P["norm2_g"][l][None]
        dx1, h2, dz1, dz2, a4, dg2 = ffn_bwd(S["x1"], g2, *S["w_ffn"], 0, dx)
        small_g["norm2_g"][l] = dg2
        dwa = mm_tn(h2, dz1, tokblk(D), sh3blk(NSH_FF), 4, D, NSH_FF, n, "dw_ffn_in_a")
        dwb = mm_tn(h2, dz2, tokblk(D), sh3blk(NSH_FF), 4, D, NSH_FF, n, "dw_ffn_in_b")
        on_grads(l, "ffn", dict(
            w_ffn_in=jnp.concatenate([dwa, dwb], axis=0),
            w_ffn_out=mm_tn(a4, dx, sh3blk(NSH_FF), tokblk(D), 4, NSH_FF, D, n,
                            "dw_ffn_out").reshape(NDEV, NSH_FF // 2, D)))
        mb = merge_bwd(dx1, **S["mi"])
        small_g["b_gate"][l], small_g["conv_ln_g"][l], small_g["conv_ln_b"][l] = mb["dbg"], mb["dlng"], mb["dlnb"]
        on_grads(l, "mix", dict(
            w_ssm_glu=mm_tn(mb["ysin"], mb["dz"], tokblk(BW), colblk(256), NDEV, BW, 256, n, "dw_glu"),
            w_att_up=mm_tn(mb["ob"], mb["dya"], tokblk(BW), colblk(128), NDEV, BW, 128, n, "dw_att"),
            w_conv_pw2=mm_tn(mb["hs"], mb["dyc"], tokblk(BW), colblk(128), NDEV, BW, 128, n, "dw_pw2"),
            w_out=mm_tn(mb["merged"], dx1, tokblk(D), tokblk(D), 1, D, D, n, "dw_out").reshape(NDEV, D // NDEV, D)))
        dcv, dcw, dcb = conv_bwd(S["proj3"], mb["dhc"].reshape(bsz, seq, BW), conv_w_pad[l])
        small_g["conv_w"][l] = dcw[:CW].reshape(CW, NDEV, BW // NDEV).transpose(1, 0, 2)
        small_g["conv_b"][l] = dcb
        ab = [att_bwd(S["proj3"], mb["do"].reshape(bsz, seq, BW), mb["lse_tot"].reshape(bsz, seq, 128),
                      mb["delta"].reshape(bsz, seq, 128), gi, dil) for gi, (_, dil) in enumerate(PATTERNS)]
        t8, bb, cb = S["tabs"]
        du, d_a, d_bb, d_cb, d_d = s5_bwd(S["proj3"], mb["dys"].reshape(bsz, seq, BW), t8, bb, cb, S["dskip"])
        lr, li, ld, btr, bti = S["sp"]
        dlr, dli, dld, dbt, dct = s5_params_bwd(lr, li, ld, btr, bti, d_a, d_bb, d_cb)
        small_g["ssm_lambda_re"][l], small_g["ssm_lambda_im"][l] = dlr.reshape(NG, NS), dli.reshape(NG, NS)
        small_g["ssm_log_dt"][l] = dld[0, :NG]
        small_g["ssm_b_re"][l], small_g["ssm_b_im"][l] = _bt_inv(dbt[0]), _bt_inv(dbt[1])
        small_g["ssm_c_re"][l], small_g["ssm_c_im"][l] = _ct_inv(dct[0]), _ct_inv(dct[1])
        small_g["ssm_d"][l] = d_d
        dproj = assemble_dproj(du, [a[0] for a in ab], [a[1] for a in ab], [a[2] for a in ab],
                               dcv, mb["dgate"].reshape(bsz, seq, 3 * D)).reshape(n, INC)
        on_grads(l, "in", dict(w_in=mm_tn(S["h1"], dproj, tokblk(D), colblk(NSH_IN), NDEV, D, NSH_IN, n, "dw_in")))
        dx, dg1 = inproj_bwd(dproj, S["w_in4"], 0, S["x"], P["norm1_g"][l][None], dx1)
        small_g["norm1_g"][l] = dg1
    return loss8, dx, dfinal, small_g


def kernel(x, norm1_g, w_in, b_gate, ssm_lambda_re, ssm_lambda_im, ssm_log_dt, ssm_b_re, ssm_b_im, ssm_c_re, ssm_c_im, ssm_d, w_ssm_glu, w_att_up, conv_w, conv_b, conv_ln_g, conv_ln_b, w_conv_pw2, w_out, norm2_g, w_ffn_in, w_ffn_out, final_g, loss_target, m_norm1_g, m_w_in, m_b_gate, m_ssm_lambda_re, m_ssm_lambda_im, m_ssm_log_dt, m_ssm_b_re, m_ssm_b_im, m_ssm_c_re, m_ssm_c_im, m_ssm_d, m_w_ssm_glu, m_w_att_up, m_conv_w, m_conv_b, m_conv_ln_g, m_conv_ln_b, m_w_conv_pw2, m_w_out, m_norm2_g, m_w_ffn_in, m_w_ffn_out, m_final_g, v_norm1_g, v_w_in, v_b_gate, v_ssm_lambda_re, v_ssm_lambda_im, v_ssm_log_dt, v_ssm_b_re, v_ssm_b_im, v_ssm_c_re, v_ssm_c_im, v_ssm_d, v_w_ssm_glu, v_w_att_up, v_conv_w, v_conv_b, v_conv_ln_g, v_conv_ln_b, v_w_conv_pw2, v_w_out, v_norm2_g, v_w_ffn_in, v_w_ffn_out, v_final_g):
    args = dict(locals())
    W = {k: args[k] for k in ORDER}
    M = {k: args["m_" + k] for k in ORDER}
    V = {k: args["v_" + k] for k in ORDER}
    bsz, seq, _ = x.shape
    n = bsz * seq
    me = 4 * lax.axis_index("x") + 2 * lax.axis_index("y") + lax.axis_index("c")

    groups = {"in": ["w_in"], "mix": ["w_ssm_glu", "w_att_up", "w_conv_pw2", "w_out"], "ffn": ["w_ffn_in", "w_ffn_out"]}
    wb = {k: W[k].astype(bf16) for k in BIG}

    def landing(shard):
        return lax.dynamic_update_index_in_dim(lax.empty((NDEV,) + shard.shape, shard.dtype), shard, me, 0)

    plan = [("gather_a", [("w_in", 0), ("conv_w", None)]),
            ("gather_b", [(k, 0) for k in groups["mix"] + groups["ffn"]]),
            ("gather_c", [(k, 1) for k in BIG])]
    pending, token = {}, None
    for name, items in plan:
        shards = [conv_w if l is None else wb[k][l] for k, l in items]
        send, recv, s_thru, l_thru, token = push_start(shards, [landing(s) for s in shards], False, name, token)
        pending[name] = (send, recv, s_thru, l_thru, items)
    gathered = {}

    def weights(l, group, after):
        name = "gather_c" if l == 1 else ("gather_a" if group == "in" else "gather_b")
        if name in pending:
            send, recv, s_thru, l_thru, items = pending.pop(name)
            for item, arr in zip(items, push_wait(send, recv, s_thru, l_thru, after, False, name + "_wait")[1]):
                gathered[item] = arr
        res = {k: gathered[(k, l)] for k in groups[group]}
        if group == "in":
            res["conv_w"] = gathered[("conv_w", None)]
        return res

    big_g = {k: [None] * DEPTH for k in BIG}
    flights = []

    def start_exchange(items, name):
        parts = [big_g[k][l] for k, l in items]
        lands = [lax.empty((NDEV - 1,) + p.shape[1:], p.dtype) for p in parts]
        send, recv, s_thru, l_thru, _ = push_start(parts, lands, True, name, None)
        flights.append((send, recv, s_thru, l_thru, items, name))

    def on_grads(l, group, grads):
        for k, g in grads.items():
            big_g[k][l] = g
        if l == 1 and group == "in":
            start_exchange([(k, 1) for k in BIG], "exchange_l1")
        elif l == 0:
            start_exchange([(k, 0) for k in groups[group]], "exchange_l0_" + group)

    loss8, dx, dfinal, small_g = local_step(x, loss_target, W, weights, on_grads)

    landed, own = {}, {}
    for send, recv, s_thru, l_thru, items, name in flights:
        srcs, lands = push_wait(send, recv, s_thru, l_thru, dx, True, name + "_wait")
        for item, src, land in zip(items, srcs, lands):
            landed[item] = land
            own[item] = lax.dynamic_index_in_dim(src, me, 0, keepdims=False)
    out = {}
    for k in BIG:
        items = [(k, l) for l in range(DEPTH)]
        out[k] = adam_big([landed[i] for i in items], [own[i] for i in items], W[k], M[k], V[k], "adam_" + k)

    names = [k for k in SMALL if k != "final_g"]
    shapes = [(DEPTH, NDEV, CW, BW // NDEV) if k == "conv_w" else W[k].shape for k in names] + [(D,), (1,)]
    gpack = _pack([jnp.stack([g.reshape(shapes[i][1:]) for g in small_g[k]]) for i, k in enumerate(names)]
                  + [dfinal, loss8[0, :1]])

    def wpack(src):
        parts = [jnp.broadcast_to(src[k][:, None], shapes[i]) if k == "conv_w" else src[k] for i, k in enumerate(names)]
        return _pack(parts + [src["final_g"], jnp.ones((1,), f32)])

    (gall,) = allgather([gpack[None]], "allgather_small")
    sg, sd, sm, sv = [_unpack(p, shapes) for p in adam_small(gall[0], wpack(W), wpack(M), wpack(V))]
    for i, k in enumerate(names + ["final_g"]):
        vals = [t[i] for t in (sg, sd, sm, sv)]
        if k == "conv_w":
            vals = [lax.dynamic_index_in_dim(t, me, axis=1, keepdims=False) for t in vals]
        out[k] = vals
    loss = sg[-1].reshape(())

    res = [loss, dx.reshape(bsz, seq, D)]
    for j in range(4):
        res += [out[k][j] for k in ORDER]
    return tuple(res)
```

```python
import functools
import math

import jax
import jax.numpy as jnp
from jax import lax
from jax.experimental import pallas as pl
from jax.experimental.pallas import tpu as pltpu

f32 = jnp.float32
bf16 = jnp.bfloat16

D = 1024
DEPTH = 2
EPS = 1e-6
BW = 512
NG = 32
GH = 16
NS = 64
NSTATE = NG * NS
HD = 64
NH = 8
PATTERNS = ((128, 1), (512, 4), (2048, 16))
ABLK = 128
ATT_SCALE = HD ** -0.5
CW = 31
DFF = 2816
INC = 7168
NDEV = 8
NSH_IN = INC // NDEV
NSH_FF = 2 * DFF // NDEV
ADAM_LR, ADAM_B1, ADAM_B2, ADAM_EPS, ADAM_WD, ADAM_STEP = 0.001, 0.9, 0.999, 1e-08, 0.01, 10

TB = 512
SJ = 4
SW = NSTATE // SJ
SU = BW // SJ
NEG = -1e30
MESH = pl.DeviceIdType.MESH
ANY = pl.BlockSpec(memory_space=pl.ANY)


def _cp(n_axes, vmem_mb=48):
    return pltpu.CompilerParams(dimension_semantics=("arbitrary",) * n_axes,
                                vmem_limit_bytes=vmem_mb * 1024 * 1024)


def _dot(a, b):
    return jnp.dot(a, b, preferred_element_type=f32)


def _dot_nt(a, b):
    return lax.dot_general(a, b, (((1,), (1,)), ((), ())), preferred_element_type=f32)


def _dot_tn(a, b):
    return lax.dot_general(a, b, (((0,), (0,)), ((), ())), preferred_element_type=f32)


def _dot_hi(a, b):
    return jnp.dot(a, b, precision=lax.Precision.HIGHEST, preferred_element_type=f32)


def _sigmoid(x):
    return 1.0 / (1.0 + jnp.exp(-x))


_GC = math.sqrt(2.0 / math.pi)


def _gelu(x):
    return 0.5 * x * (1.0 + jnp.tanh(_GC * (x + 0.044715 * x * x * x)))


def _gelu_grad(x):
    t = jnp.tanh(_GC * (x + 0.044715 * x * x * x))
    return 0.5 * (1.0 + t) + 0.5 * x * (1.0 - t * t) * _GC * (1.0 + 3.0 * 0.044715 * x * x)


def _rms_stats(x):
    return lax.rsqrt(jnp.mean(x * x, axis=-1, keepdims=True) + EPS)


def _rms_bwd(x, g, dh):
    r = _rms_stats(x)
    dyg = dh * g
    dx = r * dyg - x * (r * r * r) * jnp.mean(dyg * x, axis=-1, keepdims=True)
    dg = jnp.sum(dh * x * r, axis=0, keepdims=True)
    return dx, dg


def rms_fwd(x, g):
    n = x.shape[0]

    def body(x_ref, g_ref, h_ref):
        xv = x_ref[...]
        h_ref[...] = (xv * _rms_stats(xv) * g_ref[...]).astype(bf16)

    return pl.pallas_call(
        body, name="rms_fwd", grid=(n // TB,),
        in_specs=[pl.BlockSpec((TB, D), lambda i: (i, 0)), pl.BlockSpec((1, D), lambda i: (0, 0))],
        out_specs=pl.BlockSpec((TB, D), lambda i: (i, 0)),
        out_shape=jax.ShapeDtypeStruct((n, D), bf16), compiler_params=_cp(1))(x, g)


def inproj(h, w4, layer):
    n = h.shape[0]
    tm = 1024

    def body(h_ref, w_ref, o_ref):
        o_ref[...] = _dot(h_ref[...], w_ref[0, 0]).astype(bf16)

    return pl.pallas_call(
        body, name="inproj", grid=(NDEV, n // tm),
        in_specs=[pl.BlockSpec((tm, D), lambda s, i: (i, 0)),
                  pl.BlockSpec((1, 1, D, NSH_IN), lambda s, i: (layer, s, 0, 0))],
        out_specs=pl.BlockSpec((tm, NSH_IN), lambda s, i: (i, s)),
        out_shape=jax.ShapeDtypeStruct((n, INC), bf16), compiler_params=_cp(2))(h, w4)


def inproj_bwd(dproj, w4, layer, x, g, dres):
    n = x.shape[0]
    tm = 1024

    def body(dp_ref, w_ref, x_ref, g_ref, dr_ref, dx_ref, dg_ref, acc):
        i, s = pl.program_id(0), pl.program_id(1)

        @pl.when(s == 0)
        def _():
            acc[...] = jnp.zeros_like(acc)

        @pl.when((s == 0) & (i == 0))
        def _():
            dg_ref[...] = jnp.zeros_like(dg_ref)

        acc[...] += _dot_nt(dp_ref[...], w_ref[0, 0])

        @pl.when(s == NDEV - 1)
        def _():
            dx, dg = _rms_bwd(x_ref[...], g_ref[...], acc[...])
            dx_ref[...] = dr_ref[...] + dx
            dg_ref[...] += dg

    return pl.pallas_call(
        body, name="inproj_bwd", grid=(n // tm, NDEV),
        in_specs=[pl.BlockSpec((tm, NSH_IN), lambda i, s: (i, s)),
                  pl.BlockSpec((1, 1, D, NSH_IN), lambda i, s: (layer, s, 0, 0)),
                  pl.BlockSpec((tm, D), lambda i, s: (i, 0)),
                  pl.BlockSpec((1, D), lambda i, s: (0, 0)),
                  pl.BlockSpec((tm, D), lambda i, s: (i, 0))],
        out_specs=[pl.BlockSpec((tm, D), lambda i, s: (i, 0)), pl.BlockSpec((1, D), lambda i, s: (0, 0))],
        out_shape=[jax.ShapeDtypeStruct((n, D), f32), jax.ShapeDtypeStruct((1, D), f32)],
        scratch_shapes=[pltpu.VMEM((tm, D), f32)], compiler_params=_cp(2))(dproj, w4, x, g, dres)


def mm_tn(a, b, a_spec, b_spec, n_sh, ka, nb, m, name):
    tm = 1024

    def body(a_ref, b_ref, o_ref, acc):
        i = pl.program_id(1)

        @pl.when(i == 0)
        def _():
            acc[...] = jnp.zeros_like(acc)

        av = a_ref[...].reshape(tm, ka).astype(bf16)
        bv = b_ref[...].reshape(tm, nb).astype(bf16)
        acc[...] += _dot_tn(av, bv)

        @pl.when(i == m // tm - 1)
        def _():
            o_ref[0] = acc[...].astype(bf16)

    return pl.pallas_call(
        body, name=name, grid=(n_sh, m // tm), in_specs=[a_spec, b_spec],
        out_specs=pl.BlockSpec((1, ka, nb), lambda s, i: (s, 0, 0)),
        out_shape=jax.ShapeDtypeStruct((n_sh, ka, nb), bf16),
        scratch_shapes=[pltpu.VMEM((ka, nb), f32)], compiler_params=_cp(2))(a, b)


def ffn_fwd(x1, g2, w1, w2, layer):
    n = x1.shape[0]
    w2p = w2.reshape(w2.shape[0], 4, NSH_FF, D)

    def body(x_ref, g_ref, wa_ref, wb_ref, w2_ref, o_ref, h_sc):
        s = pl.program_id(1)

        @pl.when(s == 0)
        def _():
            xv = x_ref[...]
            h_sc[...] = (xv * _rms_stats(xv) * g_ref[...]).astype(bf16)
            o_ref[...] = xv

        h = h_sc[...]
        z1 = _dot(h, wa_ref[0, 0])
        z2 = _dot(h, wb_ref[0, 0])
        a = (z1 * _sigmoid(z1) * z2).astype(bf16)
        o_ref[...] += _dot(a, w2_ref[0, 0])

    return pl.pallas_call(
        body, name="ffn_fwd", grid=(n // TB, 4),
        in_specs=[pl.BlockSpec((TB, D), lambda i, s: (i, 0)),
                  pl.BlockSpec((1, D), lambda i, s: (0, 0)),
                  pl.BlockSpec((1, 1, D, NSH_FF), lambda i, s: (layer, s, 0, 0)),
                  pl.BlockSpec((1, 1, D, NSH_FF), lambda i, s: (layer, s + 4, 0, 0)),
                  pl.BlockSpec((1, 1, NSH_FF, D), lambda i, s: (layer, s, 0, 0))],
        out_specs=pl.BlockSpec((TB, D), lambda i, s: (i, 0)),
        out_shape=jax.ShapeDtypeStruct((n, D), f32),
        scratch_shapes=[pltpu.VMEM((TB, D), bf16)], compiler_params=_cp(2))(x1, g2, w1, w1, w2p)


def ffn_bwd(x1, g2, w1, w2, layer, dx2):
    n = x1.shape[0]
    w2p = w2.reshape(w2.shape[0], 4, NSH_FF, D)

    def body(x_ref, g_ref, dy_ref, wa_ref, wb_ref, w2_ref,
             dx_ref, h_ref, dz1_ref, dz2_ref, a_ref, dg_ref, dh_sc, dyb_sc):
        i, s = pl.program_id(0), pl.program_id(1)

        @pl.when(s == 0)
        def _():
            xv = x_ref[...]
            h_ref[...] = (xv * _rms_stats(xv) * g_ref[...]).astype(bf16)
            dh_sc[...] = jnp.zeros_like(dh_sc)
            dyb_sc[...] = dy_ref[...].astype(bf16)

        @pl.when((s == 0) & (i == 0))
        def _():
            dg_ref[...] = jnp.zeros_like(dg_ref)

        h = h_ref[...]
        z1 = _dot(h, wa_ref[0, 0])
        z2 = _dot(h, wb_ref[0, 0])
        sg = _sigmoid(z1)
        sl = z1 * sg
        a_ref[0] = (sl * z2).astype(bf16)
        da = _dot_nt(dyb_sc[...], w2_ref[0, 0])
        dz2 = (da * sl).astype(bf16)
        dz1 = (da * z2 * sg * (1.0 + z1 * (1.0 - sg))).astype(bf16)
        dz1_ref[0] = dz1
        dz2_ref[0] = dz2
        dh_sc[...] += _dot_nt(dz1, wa_ref[0, 0]) + _dot_nt(dz2, wb_ref[0, 0])

        @pl.when(s == 3)
        def _():
            dx, dg = _rms_bwd(x_ref[...], g_ref[...], dh_sc[...])
            dx_ref[...] = dy_ref[...] + dx
            dg_ref[...] += dg

    tok = lambda i, s: (i, 0)
    sh3 = lambda i, s: (s, i, 0)
    return pl.pallas_call(
        body, name="ffn_bwd", grid=(n // TB, 4),
        in_specs=[pl.BlockSpec((TB, D), tok), pl.BlockSpec((1, D), lambda i, s: (0, 0)), pl.BlockSpec((TB, D), tok),
                  pl.BlockSpec((1, 1, D, NSH_FF), lambda i, s: (layer, s, 0, 0)),
                  pl.BlockSpec((1, 1, D, NSH_FF), lambda i, s: (layer, s + 4, 0, 0)),
                  pl.BlockSpec((1, 1, NSH_FF, D), lambda i, s: (layer, s, 0, 0))],
        out_specs=[pl.BlockSpec((TB, D), tok), pl.BlockSpec((TB, D), tok),
                   pl.BlockSpec((1, TB, NSH_FF), sh3), pl.BlockSpec((1, TB, NSH_FF), sh3),
                   pl.BlockSpec((1, TB, NSH_FF), sh3), pl.BlockSpec((1, D), lambda i, s: (0, 0))],
        out_shape=[jax.ShapeDtypeStruct((n, D), f32), jax.ShapeDtypeStruct((n, D), bf16),
                   jax.ShapeDtypeStruct((4, n, NSH_FF), bf16), jax.ShapeDtypeStruct((4, n, NSH_FF), bf16),
                   jax.ShapeDtypeStruct((4, n, NSH_FF), bf16), jax.ShapeDtypeStruct((1, D), f32)],
        scratch_shapes=[pltpu.VMEM((TB, D), f32), pltpu.VMEM((TB, D), bf16)],
        compiler_params=_cp(2))(x1, g2, dx2, w1, w1, w2p)


def loss_head(x, g, target):
    n = x.shape[0]

    def body(x_ref, g_ref, t_ref, l_ref, dx_ref, dg_ref):
        i = pl.program_id(0)

        @pl.when(i == 0)
        def _():
            l_ref[...] = jnp.zeros_like(l_ref)
            dg_ref[...] = jnp.zeros_like(dg_ref)

        xv = x_ref[...]
        y = xv * _rms_stats(xv) * g_ref[...]
        e = y - t_ref[...]
        l_ref[...] += 0.5 * jnp.sum(jnp.sum(e * e, axis=-1, keepdims=True), axis=0, keepdims=True) * (1.0 / D)
        dx, dg = _rms_bwd(xv, g_ref[...], e * (1.0 / D))
        dx_ref[...] = dx
        dg_ref[...] += dg

    tok = lambda i: (i, 0)
    return pl.pallas_call(
        body, name="loss_head", grid=(n // TB,),
        in_specs=[pl.BlockSpec((TB, D), tok), pl.BlockSpec((1, D), lambda i: (0, 0)), pl.BlockSpec((TB, D), tok)],
        out_specs=[pl.BlockSpec((8, 128), lambda i: (0, 0)), pl.BlockSpec((TB, D), tok),
                   pl.BlockSpec((1, D), lambda i: (0, 0))],
        out_shape=[jax.ShapeDtypeStruct((8, 128), f32), jax.ShapeDtypeStruct((n, D), f32),
                   jax.ShapeDtypeStruct((1, D), f32)],
        compiler_params=_cp(1))(x, g, target)


def _disc(lr, li, ld):
    dt = jnp.exp(ld)
    mag = jnp.exp(lr * dt)
    ar = mag * jnp.cos(li * dt)
    ai = mag * jnp.sin(li * dt)
    nr, ni = ar - 1.0, ai
    den = lr * lr + li * li
    zr = (nr * lr + ni * li) / den
    zi = (ni * lr - nr * li) / den
    return ar, ai, zr, zi


def _blockdiag_mask(shape):
    r = lax.broadcasted_iota(jnp.int32, shape, 0) // GH
    c = lax.broadcasted_iota(jnp.int32, shape, 1) // NS
    return r == c


def s5_params(lr, li, ld, btr, bti, ctr, cti):
    def body(lr_ref, li_ref, ld_ref, btr_ref, bti_ref, ctr_ref, cti_ref, t8_ref, bb_ref, cb_ref):
        ar, ai, zr, zi = _disc(lr_ref[...], li_ref[...], ld_ref[...])
        pr_, pi_ = ar, ai
        pw2 = []
        for k in range(4):
            pw2.append((pr_, pi_))
            pr_, pi_ = pr_ * pr_ - pi_ * pi_, 2.0 * pr_ * pi_
        cm = lambda p, q: (p[0] * q[0] - p[1] * q[1], p[0] * q[1] + p[1] * q[0])
        pw = {1: pw2[0], 2: pw2[1], 4: pw2[2], 8: pw2[3]}
        pw[3], pw[5], pw[6] = cm(pw[2], pw[1]), cm(pw[4], pw[1]), cm(pw[4], pw[2])
        pw[7] = cm(pw[4], pw[3])
        row = lax.broadcasted_iota(jnp.int32, (8, NSTATE), 0)
        zero = jnp.zeros((8, NSTATE), f32)
        for c in range(2):
            for k in range(3):
                full = jnp.broadcast_to(pw2[k][c], (8, NSTATE))
                t8_ref[c, k] = jnp.where(row >= (1 << k), full, 0.0)
                t8_ref[c, 3 + k] = jnp.where(row + (1 << k) < 8, full, 0.0)
            up, down = zero, zero
            for j in range(8):
                up = up + jnp.where(row == j, pw[j + 1][c], 0.0)
                down = down + jnp.where(row == j, pw[8 - j][c], 0.0)
            t8_ref[c, 6] = up
            t8_ref[c, 7] = down
        bbr = zr * btr_ref[...] - zi * bti_ref[...]
        bbi = zr * bti_ref[...] + zi * btr_ref[...]
        mask = _blockdiag_mask((SU, SW))
        for j in range(SJ):
            cols = slice(j * SW, (j + 1) * SW)
            for c, (vb, vc) in enumerate(((bbr, ctr_ref[...]), (bbi, cti_ref[...]))):
                bb_ref[c, j] = jnp.where(mask, jnp.tile(vb[:, cols], (SU // GH, 1)), 0.0).astype(bf16)
                cb_ref[c, j] = jnp.where(mask, jnp.tile(vc[:, cols], (SU // GH, 1)), 0.0).astype(bf16)

    return pl.pallas_call(
        body, name="s5_params",
        out_shape=[jax.ShapeDtypeStruct((2, 8, 8, NSTATE), f32),
                   jax.ShapeDtypeStruct((2, SJ, SU, SW), bf16), jax.ShapeDtypeStruct((2, SJ, SU, SW), bf16)],
        compiler_params=pltpu.CompilerParams(vmem_limit_bytes=56 * 1024 * 1024))(lr, li, ld, btr, bti, ctr, cti)


def s5_params_bwd(lr, li, ld, btr, bti, d_a, d_bb, d_cb):
    def body(lr_ref, li_ref, ld_ref, btr_ref, bti_ref, da_ref, dbb_ref, dcb_ref,
             dlr_ref, dli_ref, dld_ref, dbt_ref, dct_ref):
        mask = _blockdiag_mask((SU, SW))

        def fold(ref, c):
            parts = []
            for j in range(SJ):
                v = jnp.where(mask, ref[c, j], 0.0)
                parts.append(v.reshape(SU // GH, GH, SW).sum(axis=0))
            return jnp.concatenate(parts, axis=1)

        dct_ref[0] = fold(dcb_ref, 0)
        dct_ref[1] = fold(dcb_ref, 1)
        dbbr, dbbi = fold(dbb_ref, 0), fold(dbb_ref, 1)
        lrv, liv, ldv = lr_ref[...], li_ref[...], ld_ref[...]
        (ar, ai, zr, zi), vjp = jax.vjp(_disc, lrv, liv, ldv)
        btr, bti = btr_ref[...], bti_ref[...]
        dbt_ref[0] = zr * dbbr + zi * dbbi
        dbt_ref[1] = zr * dbbi - zi * dbbr
        dzr = jnp.sum(dbbr * btr + dbbi * bti, axis=0, keepdims=True)
        dzi = jnp.sum(dbbi * btr - dbbr * bti, axis=0, keepdims=True)
        dlr, dli, dld = vjp((da_ref[0:1, :], da_ref[1:2, :], dzr, dzi))
        dlr_ref[...] = dlr
        dli_ref[...] = dli
        ind = (lax.broadcasted_iota(jnp.int32, (NSTATE, 128), 0) // NS
               == lax.broadcasted_iota(jnp.int32, (NSTATE, 128), 1)).astype(f32)
        dld_ref[...] = _dot_hi(jnp.broadcast_to(dld, (8, NSTATE)), ind)

    return pl.pallas_call(
        body, name="s5_params_bwd",
        out_shape=[jax.ShapeDtypeStruct((1, NSTATE), f32), jax.ShapeDtypeStruct((1, NSTATE), f32),
                   jax.ShapeDtypeStruct((8, 128), f32), jax.ShapeDtypeStruct((2, GH, NSTATE), f32),
                   jax.ShapeDtypeStruct((2, GH, NSTATE), f32)],
        compiler_params=pltpu.CompilerParams(vmem_limit_bytes=56 * 1024 * 1024))(lr, li, ld, btr, bti, d_a, d_bb, d_cb)


def _fma(sr, si, ar, ai, qr, qi):
    return sr + ar * qr - ai * qi, si + ar * qi + ai * qr


def _scan_tile(sr, si, cr, ci, t8_ref, reverse):
    sg = -1.0 if reverse else 1.0
    for k in range(3):
        tk = 3 + k if reverse else k
        rot = 8 - (1 << k) if reverse else 1 << k
        sr, si = _fma(sr, si, t8_ref[0, tk], sg * t8_ref[1, tk], pltpu.roll(sr, rot, 0), pltpu.roll(si, rot, 0))
    tp = 7 if reverse else 6
    sr, si = _fma(sr, si, t8_ref[0, tp], sg * t8_ref[1, tp], cr, ci)
    e = 0 if reverse else 7
    return sr, si, jnp.broadcast_to(sr[e:e + 1, :], sr.shape), jnp.broadcast_to(si[e:e + 1, :], si.shape)


S5MC = 512


def _s5_input_map(u_ref, bb_ref, sr_sc, si_sc, l):
    for c in range(l // S5MC):
        rows = slice(c * S5MC, (c + 1) * S5MC)
        u = u_ref[0, rows, :]
        sr_sc[rows, :] = _dot(u, bb_ref[0, 0])
        si_sc[rows, :] = _dot(u, bb_ref[1, 0])


def _s5_forward_scan(sr_sc, si_sc, t8_ref, l):
    def step(k, carry):
        rows = pl.ds(pl.multiple_of(k * 8, 8), 8)
        sr, si, cr, ci = _scan_tile(sr_sc[rows, :], si_sc[rows, :], carry[0], carry[1], t8_ref, False)
        sr_sc[rows, :] = sr
        si_sc[rows, :] = si
        return cr, ci

    zero = jnp.zeros((8, SW), f32)
    lax.fori_loop(0, l // 8, step, (zero, zero), unroll=4)


def s5_fwd(proj3, t8, bb, cb, dskip):
    b, l, _ = proj3.shape

    def body(u_ref, t8_ref, bb_ref, cb_ref, d_ref, y_ref, sr_sc, si_sc):
        _s5_input_map(u_ref, bb_ref, sr_sc, si_sc, l)
        _s5_forward_scan(sr_sc, si_sc, t8_ref, l)
        for c in range(l // S5MC):
            rows = slice(c * S5MC, (c + 1) * S5MC)
            y = (_dot_nt(sr_sc[rows, :].astype(bf16), cb_ref[0, 0])
                 - _dot_nt(si_sc[rows, :].astype(bf16), cb_ref[1, 0]))
            y_ref[0, rows, :] = y + d_ref[...] * u_ref[0, rows, :].astype(f32)

    return pl.pallas_call(
        body, name="s5_fwd", grid=(SJ, b),
        in_specs=[pl.BlockSpec((1, l, SU), lambda j, bi: (bi, 0, j)),
                  pl.BlockSpec((2, 8, 8, SW), lambda j, bi: (0, 0, 0, j)),
                  pl.BlockSpec((2, 1, SU, SW), lambda j, bi: (0, j, 0, 0)),
                  pl.BlockSpec((2, 1, SU, SW), lambda j, bi: (0, j, 0, 0)),
                  pl.BlockSpec((1, SU), lambda j, bi: (0, j))],
        out_specs=pl.BlockSpec((1, l, SU), lambda j, bi: (bi, 0, j)),
        out_shape=jax.ShapeDtypeStruct((b, l, BW), f32),
        scratch_shapes=[pltpu.VMEM((l, SW), f32)] * 2, compiler_params=_cp(2))(proj3, t8, bb, cb, dskip)


def s5_bwd(proj3, dy, t8, bb, cb, dskip):
    b, l, _ = proj3.shape
    nt = l // 8

    def body(u_ref, dy_ref, t8_ref, bb_ref, cb_ref, d_ref,
             du_ref, da_ref, dbb_ref, dcb_ref, dd_ref, sr_sc, si_sc, gr_sc, gi_sc):
        bi = pl.program_id(1)

        @pl.when(bi == 0)
        def _():
            da_ref[...] = jnp.zeros_like(da_ref)
            dbb_ref[...] = jnp.zeros_like(dbb_ref)
            dcb_ref[...] = jnp.zeros_like(dcb_ref)
            dd_ref[...] = jnp.zeros_like(dd_ref)

        _s5_input_map(u_ref, bb_ref, sr_sc, si_sc, l)
        _s5_forward_scan(sr_sc, si_sc, t8_ref, l)
        for c in range(l // S5MC):
            rows = slice(c * S5MC, (c + 1) * S5MC)
            dyb = dy_ref[0, rows, :].astype(bf16)
            gr_sc[rows, :] = _dot(dyb, cb_ref[0, 0])
            gi_sc[rows, :] = -_dot(dyb, cb_ref[1, 0])

        row = lax.broadcasted_iota(jnp.int32, (8, SW), 0)

        def step(i, carry):
            cr, ci, dar, dai = carry
            k = nt - 1 - i
            rows = pl.ds(pl.multiple_of(k * 8, 8), 8)
            gr, gi, cr, ci = _scan_tile(gr_sc[rows, :], gi_sc[rows, :], cr, ci, t8_ref, True)
            gr_sc[rows, :] = gr
            gi_sc[rows, :] = gi
            before = pl.ds(pl.multiple_of(jnp.maximum(k - 1, 0) * 8, 8), 8)
            live = jnp.where(k > 0, 1.0, 0.0)
            sr, si = sr_sc[rows, :], si_sc[rows, :]
            spr = jnp.where(row == 0, live * sr_sc[before, :][7:8, :], pltpu.roll(sr, 1, 0))
            spi = jnp.where(row == 0, live * si_sc[before, :][7:8, :], pltpu.roll(si, 1, 0))
            return cr, ci, dar + spr * gr + spi * gi, dai + spr * gi - spi * gr

        zero = jnp.zeros((8, SW), f32)
        _, _, dar, dai = lax.fori_loop(0, nt, step, (zero, zero, zero, zero), unroll=2)
        da_ref[0:1, :] += jnp.sum(dar, axis=0, keepdims=True)
        da_ref[1:2, :] += jnp.sum(dai, axis=0, keepdims=True)

        for c in range(l // S5MC):
            rows = slice(c * S5MC, (c + 1) * S5MC)
            u = u_ref[0, rows, :]
            dyv = dy_ref[0, rows, :]
            dyb = dyv.astype(bf16)
            grb, gib = gr_sc[rows, :].astype(bf16), gi_sc[rows, :].astype(bf16)
            dcb_ref[0, 0] += _dot_tn(dyb, sr_sc[rows, :].astype(bf16))
            dcb_ref[1, 0] -= _dot_tn(dyb, si_sc[rows, :].astype(bf16))
            dbb_ref[0, 0] += _dot_tn(u, grb)
            dbb_ref[1, 0] += _dot_tn(u, gib)
            du = _dot_nt(grb, bb_ref[0, 0]) + _dot_nt(gib, bb_ref[1, 0]) + d_ref[...] * dyv
            du_ref[0, rows, :] = du.astype(bf16)
            dd_ref[...] += jnp.sum(dyv * u.astype(f32), axis=0, keepdims=True)

    seq = pl.BlockSpec((1, l, SU), lambda j, bi: (bi, 0, j))
    tab = pl.BlockSpec((2, 1, SU, SW), lambda j, bi: (0, j, 0, 0))
    return pl.pallas_call(
        body, name="s5_bwd", grid=(SJ, b),
        in_specs=[seq, seq, pl.BlockSpec((2, 8, 8, SW), lambda j, bi: (0, 0, 0, j)), tab, tab,
                  pl.BlockSpec((1, SU), lambda j, bi: (0, j))],
        out_specs=[seq, pl.BlockSpec((2, SW), lambda j, bi: (0, j)), tab, tab,
                   pl.BlockSpec((1, SU), lambda j, bi: (0, j))],
        out_shape=[jax.ShapeDtypeStruct((b, l, BW), bf16), jax.ShapeDtypeStruct((2, NSTATE), f32),
                   jax.ShapeDtypeStruct((2, SJ, SU, SW), f32), jax.ShapeDtypeStruct((2, SJ, SU, SW), f32),
                   jax.ShapeDtypeStruct((1, BW), f32)],
        scratch_shapes=[pltpu.VMEM((l, SW), f32)] * 4,
        compiler_params=_cp(2))(proj3, dy, t8, bb, cb, dskip)


AHC = 2
AHW = AHC * 128


def _att_mask(n, nb):
    if nb == 1:
        qi = lax.broadcasted_iota(jnp.int32, (ABLK, ABLK), 0)
        kj = lax.broadcasted_iota(jnp.int32, (ABLK, ABLK), 1)
        return kj <= qi
    qi = lax.broadcasted_iota(jnp.int32, (ABLK, 2 * ABLK), 0)
    kj = lax.broadcasted_iota(jnp.int32, (ABLK, 2 * ABLK), 1)
    return (kj >= qi) & (kj <= qi + ABLK) & ((n > 0) | (kj >= ABLK))


def _att_rows(it, nb, dil):
    r, n = it // nb, it % nb
    cur = pl.ds(r + n * (ABLK * dil), ABLK, stride=dil)
    prv = pl.ds(r + jnp.maximum(n - 1, 0) * (ABLK * dil), ABLK, stride=dil)
    return n, cur, prv


def _att_keys(ref, c, cur, prv, nb):
    if nb == 1:
        x = ref[c, cur, :].astype(bf16)
    else:
        x = jnp.concatenate([ref[c, prv, :], ref[c, cur, :]], axis=0).astype(bf16)
    head0 = lax.broadcasted_iota(jnp.int32, x.shape, 1) < HD
    zero = jnp.zeros_like(x)
    return jnp.concatenate([jnp.where(head0, x, zero), jnp.where(head0, zero, x)], axis=0)


def _per_head(nk, a0, a1):
    col = lax.broadcasted_iota(jnp.int32, (ABLK, 2 * nk), 1)
    return jnp.where(col < nk, a0, a1)


def _to_chunks(src_ref, dst):
    for c in range(AHC):
        dst[c] = src_ref[0, :, c * 128:(c + 1) * 128].astype(f32)


def att_fwd(proj3, g_idx, dil):
    b, l, _ = proj3.shape
    nb = l // dil // ABLK
    nhalf = BW // AHW

    def body(q_ref, k_ref, v_ref, o_ref, lse_ref, qf, kf, vf, of):
        hh = pl.program_id(1)
        _to_chunks(q_ref, qf)
        _to_chunks(k_ref, kf)
        _to_chunks(v_ref, vf)
        lane = lax.broadcasted_iota(jnp.int32, (ABLK, 128), 1)

        def step(it, carry):
            n, cur, prv = _att_rows(it, nb, dil)
            valid = _att_mask(n, nb)
            valid = jnp.concatenate([valid, valid], axis=1)
            nk = valid.shape[1] // 2
            lse_all = jnp.zeros((ABLK, 128), f32)
            for c in range(AHC):
                q = (qf[c, cur, :] * ATT_SCALE).astype(bf16)
                k = _att_keys(kf, c, cur, prv, nb)
                v = _att_keys(vf, c, cur, prv, nb)
                s = jnp.where(valid, _dot_nt(q, k), NEG)
                m0 = jnp.max(s[:, :nk], axis=-1, keepdims=True)
                m1 = jnp.max(s[:, nk:], axis=-1, keepdims=True)
                p = jnp.exp(s - _per_head(nk, m0, m1))
                den0 = jnp.sum(p[:, :nk], axis=-1, keepdims=True)
                den1 = jnp.sum(p[:, nk:], axis=-1, keepdims=True)
                of[c, cur, :] = _dot(p.astype(bf16), v) * jnp.where(lane < HD, 1.0 / den0, 1.0 / den1)
                head = hh * (2 * AHC) + 2 * c
                lse_all = (lse_all + jnp.where(lane == head, m0 + jnp.log(den0), 0.0)
                           + jnp.where(lane == head + 1, m1 + jnp.log(den1), 0.0))

            lse_ref[0, 0, cur, :] = lse_all
            return carry

        lax.fori_loop(0, dil * nb, step, 0, unroll=4)
        for c in range(AHC):
            o_ref[0, :, c * 128:(c + 1) * 128] = of[c].astype(bf16)

    col = lambda c: pl.BlockSpec((1, l, AHW), lambda bi, hh: (bi, 0, c * nhalf + hh))
    return pl.pallas_call(
        body, name=f"att_fwd{g_idx}", grid=(b, nhalf),
        in_specs=[col(1 + g_idx), col(4), col(5)],
        out_specs=[pl.BlockSpec((1, l, AHW), lambda bi, hh: (bi, 0, hh)),
                   pl.BlockSpec((1, 1, l, 128), lambda bi, hh: (bi, hh, 0, 0))],
        out_shape=[jax.ShapeDtypeStruct((b, l, BW), bf16), jax.ShapeDtypeStruct((b, nhalf, l, 128), f32)],
        scratch_shapes=[pltpu.VMEM((AHC, l, 128), f32)] * 4,
        compiler_params=_cp(2))(proj3, proj3, proj3)


def att_bwd(proj3, do, lse_tot, delta, g_idx, dil):
    b, l, _ = proj3.shape
    nb = l // dil // ABLK
    nhalf = BW // AHW

    def body(q_ref, k_ref, v_ref, do_ref, l_ref, dl_ref, dq_ref, dk_ref, dv_ref, qf, kf, vf, dof):
        hh = pl.program_id(1)
        _to_chunks(q_ref, qf)
        _to_chunks(k_ref, kf)
        _to_chunks(v_ref, vf)
        _to_chunks(do_ref, dof)
        dk_ref[...] = jnp.zeros_like(dk_ref)
        dv_ref[...] = jnp.zeros_like(dv_ref)
        lane = lax.broadcasted_iota(jnp.int32, (ABLK, 128), 1)

        def step(it, carry):
            n, cur, prv = _att_rows(it, nb, dil)
            valid = _att_mask(n, nb)
            valid = jnp.concatenate([valid, valid], axis=1)
            nk = valid.shape[1] // 2
            lse_b = l_ref[0, cur, :]
            dl_b = dl_ref[0, cur, :]
            head0 = lax.broadcasted_iota(jnp.int32, (nk, 128), 1) < HD
            for c in range(AHC):
                q = (qf[c, cur, :] * ATT_SCALE).astype(bf16)
                dob = dof[c, cur, :].astype(bf16)
                k = _att_keys(kf, c, cur, prv, nb)
                v = _att_keys(vf, c, cur, prv, nb)
                head = hh * (2 * AHC) + 2 * c
                pick = lambda a, h: jnp.sum(jnp.where(lane == h, a, 0.0), axis=-1, keepdims=True)
                lse_h = _per_head(nk, pick(lse_b, head), pick(lse_b, head + 1))
                dl_h = _per_head(nk, pick(dl_b, head), pick(dl_b, head + 1))
                s = _dot_nt(q, k)
                p = jnp.where(valid, jnp.exp(jnp.minimum(s - lse_h, 60.0)), 0.0)
                ds = (p * (_dot_nt(dob, v) - dl_h)).astype(bf16)
                dq_ref[0, c, cur, :] = _dot(ds, k) * ATT_SCALE
                dk2 = _dot_tn(ds, q)
                dv2 = _dot_tn(p.astype(bf16), dob)
                dk = jnp.where(head0, dk2[:nk], dk2[nk:])
                dv = jnp.where(head0, dv2[:nk], dv2[nk:])
                if nb == 1:
                    dk_ref[0, c, cur, :] += dk
                    dv_ref[0, c, cur, :] += dv
                else:
                    dk_ref[0, c, cur, :] += dk[ABLK:]
                    dv_ref[0, c, cur, :] += dv[ABLK:]
                    dk_ref[0, c, prv, :] += dk[:ABLK]
                    dv_ref[0, c, prv, :] += dv[:ABLK]

            return carry

        lax.fori_loop(0, dil * nb, step, 0, unroll=4)

    col = lambda c: pl.BlockSpec((1, l, AHW), lambda bi, hh: (bi, 0, c * nhalf + hh))
    own = pl.BlockSpec((1, l, AHW), lambda bi, hh: (bi, 0, hh))
    own128 = pl.BlockSpec((1, l, 128), lambda bi, hh: (bi, 0, 0))
    chunked = pl.BlockSpec((1, AHC, l, 128), lambda bi, hh: (bi, hh, 0, 0))
    return pl.pallas_call(
        body, name=f"att_bwd{g_idx}", grid=(b, nhalf),
        in_specs=[col(1 + g_idx), col(4), col(5), own, own128, own128],
        out_specs=[chunked] * 3,
        out_shape=[jax.ShapeDtypeStruct((b, BW // 128, l, 128), f32)] * 3,
        scratch_shapes=[pltpu.VMEM((AHC, l, 128), f32)] * 4,
        compiler_params=_cp(2, 56))(proj3, proj3, proj3, do, lse_tot, delta)


CPAD = 32
CTAIL = 16
CR = 128
CSLAB = CR + 40


def _tap_windows(slab, off, mis):
    ntap = (CW - 1 - mis) // 8 + 1
    xb = slab[off + mis:off + mis + CR + 8 * (ntap - 1)]
    for a in range(ntap):
        yield 8 * a + mis, xb[8 * a:8 * a + CR]


def _fill_glu(cv_ref, pad, l):
    pad[0:CPAD, :] = jnp.zeros((CPAD, BW), f32)
    pad[CPAD:CPAD + l, :] = cv_ref[0, :, :BW].astype(f32) * _sigmoid(cv_ref[0, :, BW:].astype(f32))
    pad[CPAD + l:, :] = jnp.zeros((CTAIL, BW), f32)


def conv_fwd(proj3, cw, cb):
    b, l, _ = proj3.shape

    def body(cv_ref, w_ref, b_ref, o_ref, pad):
        _fill_glu(cv_ref, pad, l)
        for lc in range(BW // 128):
            lanes = slice(lc * 128, (lc + 1) * 128)
            wv = w_ref[:, lanes]

            def step(c, carry):
                base = pl.multiple_of(c * CR, CR)
                slab = pad[pl.ds(base, CSLAB), lanes]
                acc = jnp.zeros((CR, 128), f32) + b_ref[:, lanes]
                for mis in range(8):
                    for k, win in _tap_windows(slab, CPAD - (CW - 1), mis):
                        acc = acc + wv[k:k + 1] * win
                o_ref[0, pl.ds(base, CR), lanes] = acc
                return carry

            lax.fori_loop(0, l // CR, step, 0)

    return pl.pallas_call(
        body, name="conv_fwd", grid=(b,),
        in_specs=[pl.BlockSpec((1, l, 2 * BW), lambda i: (i, 0, 3)),
                  pl.BlockSpec((32, BW), lambda i: (0, 0)), pl.BlockSpec((1, BW), lambda i: (0, 0))],
        out_specs=pl.BlockSpec((1, l, BW), lambda i: (i, 0, 0)),
        out_shape=jax.ShapeDtypeStruct((b, l, BW), f32),
        scratch_shapes=[pltpu.VMEM((CPAD + l + CTAIL, BW), f32)], compiler_params=_cp(1))(proj3, cw, cb)


def conv_bwd(proj3, dhc, cw):
    b, l, _ = proj3.shape

    def body(cv_ref, d_ref, w_ref, dcv_ref, dw_ref, db_ref, pad, dpad):
        i = pl.program_id(0)

        @pl.when(i == 0)
        def _():
            dw_ref[...] = jnp.zeros_like(dw_ref)
            db_ref[...] = jnp.zeros_like(db_ref)

        _fill_glu(cv_ref, pad, l)
        dpad[0:l, :] = d_ref[0]
        dpad[l:, :] = jnp.zeros((CPAD + CTAIL, BW), f32)
        db_ref[...] += jnp.sum(d_ref[0], axis=0, keepdims=True)
        for lc in range(BW // 128):
            lanes = slice(lc * 128, (lc + 1) * 128)
            glanes = slice(BW + lc * 128, BW + (lc + 1) * 128)
            wv = w_ref[:, lanes]

            for mis in range(8):
                ntap = (CW - 1 - mis) // 8 + 1

                def dw_step(c, accs, mis=mis, lanes=lanes):
                    base = pl.multiple_of(c * CR, CR)
                    slab = pad[pl.ds(base, CSLAB), lanes]
                    dv = dpad[pl.ds(base, CR), lanes]
                    return tuple(acc + (dv * win).reshape(CR // 8, 8, 128).sum(axis=0) for acc, (_, win)
                                 in zip(accs, _tap_windows(slab, CPAD - (CW - 1), mis)))

                accs = lax.fori_loop(0, l // CR, dw_step, tuple(jnp.zeros((8, 128), f32) for _ in range(ntap)))
                for a in range(ntap):
                    k = 8 * a + mis
                    dw_ref[k:k + 1, lanes] += jnp.sum(accs[a], axis=0, keepdims=True)

            def dh_step(c, carry, lanes=lanes, glanes=glanes, wv=wv):
                base = pl.multiple_of(c * CR, CR)
                slab = dpad[pl.ds(base, CSLAB), lanes]
                acc = jnp.zeros((CR, 128), f32)
                for mis in range(8):
                    for kk, win in _tap_windows(slab, 0, mis):
                        acc = acc + wv[CW - 1 - kk:CW - kk] * win
                rows = pl.ds(base, CR)
                a = cv_ref[0, rows, lanes].astype(f32)
                sg = _sigmoid(cv_ref[0, rows, glanes].astype(f32))
                dcv_ref[0, rows, lanes] = (acc * sg).astype(bf16)
                dcv_ref[0, rows, glanes] = (acc * a * sg * (1.0 - sg)).astype(bf16)
                return carry

            lax.fori_loop(0, l // CR, dh_step, 0)

    return pl.pallas_call(
        body, name="conv_bwd", grid=(b,),
        in_specs=[pl.BlockSpec((1, l, 2 * BW), lambda i: (i, 0, 3)),
                  pl.BlockSpec((1, l, BW), lambda i: (i, 0, 0)),
                  pl.BlockSpec((32, BW), lambda i: (0, 0))],
        out_specs=[pl.BlockSpec((1, l, 2 * BW), lambda i: (i, 0, 0)),
                   pl.BlockSpec((32, BW), lambda i: (0, 0)), pl.BlockSpec((1, BW), lambda i: (0, 0))],
        out_shape=[jax.ShapeDtypeStruct((b, l, 2 * BW), bf16), jax.ShapeDtypeStruct((32, BW), f32),
                   jax.ShapeDtypeStruct((1, BW), f32)],
        scratch_shapes=[pltpu.VMEM((CPAD + l + CTAIL, BW), f32), pltpu.VMEM((l + CPAD + CTAIL, BW), f32)],
        compiler_params=_cp(1))(proj3, dhc, cw)


def _head_expand():
    r = lax.broadcasted_iota(jnp.int32, (128, BW), 0)
    c = lax.broadcasted_iota(jnp.int32, (128, BW), 1) // HD
    return (r == c).astype(f32)


def _head_reduce():
    r = lax.broadcasted_iota(jnp.int32, (BW, 128), 0) // HD
    c = lax.broadcasted_iota(jnp.int32, (BW, 128), 1)
    return (r == c).astype(f32)


def _merge_common(ys_ref, o_refs, l_refs, hc_ref, g_refs, bg_ref, lng_ref, lnb_ref, wglu_ref, watt_ref, wpw_ref):
    r = {}
    ysv = ys_ref[...]
    r["ys"] = ysv
    r["ysin"] = _gelu(ysv).astype(bf16)
    z = _dot(r["ysin"], wglu_ref[...])
    r["z1"], r["sg2"] = z[:, :D], _sigmoid(z[:, D:])
    r["y_s"] = r["z1"] * r["sg2"]
    ls = [lr_[0, 0] + lr_[0, 1] for lr_ in l_refs]
    mx = jnp.maximum(jnp.maximum(ls[0], ls[1]), ls[2])
    es = [jnp.exp(v - mx) for v in ls]
    tot = es[0] + es[1] + es[2]
    r["lse_tot"] = mx + jnp.log(tot)
    e_mat = _head_expand()
    o = jnp.zeros(ysv.shape, f32)
    for e, o_ref in zip(es, o_refs):
        o = o + _dot_hi(e / tot, e_mat) * o_ref[...].astype(f32)
    r["o"] = o
    r["ob"] = o.astype(bf16)
    r["y_a"] = _dot(r["ob"], watt_ref[...])
    hc = hc_ref[...]
    mu = jnp.mean(hc, axis=-1, keepdims=True)
    xc = hc - mu
    rstd = lax.rsqrt(jnp.mean(xc * xc, axis=-1, keepdims=True) + EPS)
    r["xh"], r["rstd"] = xc * rstd, rstd
    hn = r["xh"] * lng_ref[...] + lnb_ref[...]
    r["hn"] = hn
    r["sgn"] = _sigmoid(hn)
    r["hs"] = (hn * r["sgn"]).astype(bf16)
    r["y_c"] = _dot(r["hs"], wpw_ref[...])
    r["gates"] = [_sigmoid(g_refs[k][...].astype(f32) + bg_ref[:, k * D:(k + 1) * D]) for k in range(3)]
    r["merged"] = r["gates"][0] * r["y_s"] + r["gates"][1] * r["y_a"] + r["gates"][2] * r["y_c"]
    return r


TBM = 256


def _merge_in_specs(tok, tb, lses):
    w = lambda shape: pl.BlockSpec(shape, lambda i: (0, 0))
    nbl = lses[0].shape[2] // tb
    return ([pl.BlockSpec((tb, D), tok), pl.BlockSpec((tb, BW), tok)]
            + [pl.BlockSpec((tb, BW), tok)] * 3
            + [pl.BlockSpec((1, 2, tb, 128), lambda i: (i // nbl, 0, i % nbl, 0))] * 3
            + [pl.BlockSpec((tb, BW), tok)]
            + [pl.BlockSpec((tb, D), lambda i, k=k: (i, 4 + k)) for k in range(3)]
            + [w((1, 3 * D)), w((1, BW)), w((1, BW)), w((BW, 2 * D)), w((BW, D)), w((BW, D)), w((D, D))])


def merge_fwd(x, ys, os_, lses, hc, proj, bg, lng, lnb, wglu, watt, wpw, wout):
    n = x.shape[0]

    def body(x_ref, ys_ref, o1, o2, o3, l1, l2, l3, hc_ref, g0, g1, g2, bg_ref, lng_ref, lnb_ref,
             wglu_ref, watt_ref, wpw_ref, wout_ref, x1_ref):
        r = _merge_common(ys_ref, (o1, o2, o3), (l1, l2, l3), hc_ref, (g0, g1, g2), bg_ref, lng_ref, lnb_ref,
                          wglu_ref, watt_ref, wpw_ref)
        x1_ref[...] = x_ref[...] + _dot(r["merged"].astype(bf16), wout_ref[...])

    tok = lambda i: (i, 0)
    return pl.pallas_call(
        body, name="merge_fwd", grid=(n // TB,), in_specs=_merge_in_specs(tok, TB, lses),
        out_specs=pl.BlockSpec((TB, D), tok), out_shape=jax.ShapeDtypeStruct((n, D), f32),
        compiler_params=_cp(1, 56))(x, ys, *os_, *lses, hc, proj, proj, proj, bg, lng, lnb, wglu, watt, wpw, wout)


def merge_bwd(dx1, ys, os_, lses, hc, proj, bg, lng, lnb, wglu, watt, wpw, wout):
    n = dx1.shape[0]

    def body(dx_ref, ys_ref, o1, o2, o3, l1, l2, l3, hc_ref, g0, g1, g2, bg_ref, lng_ref, lnb_ref,
             wglu_ref, watt_ref, wpw_ref, wout_ref,
             dys_ref, do_ref, delta_ref, ltot_ref, dhc_ref, dgate_ref, ysin_ref, dz_ref, ob_ref, dya_ref,
             hs_ref, dyc_ref, mg_ref, dbg_ref, dlng_ref, dlnb_ref):
        i = pl.program_id(0)

        @pl.when(i == 0)
        def _():
            dbg_ref[...] = jnp.zeros_like(dbg_ref)
            dlng_ref[...] = jnp.zeros_like(dlng_ref)
            dlnb_ref[...] = jnp.zeros_like(dlnb_ref)

        r = _merge_common(ys_ref, (o1, o2, o3), (l1, l2, l3), hc_ref, (g0, g1, g2), bg_ref, lng_ref, lnb_ref,
                          wglu_ref, watt_ref, wpw_ref)
        mg_ref[...] = r["merged"].astype(bf16)
        ysin_ref[...] = r["ysin"]
        ob_ref[...] = r["ob"]
        hs_ref[...] = r["hs"]
        ltot_ref[...] = r["lse_tot"]
        dm = _dot_nt(dx_ref[...].astype(bf16), wout_ref[...])
        ys3 = (r["y_s"], r["y_a"], r["y_c"])
        for k in range(3):
            gk = r["gates"][k]
            dgr = dm * ys3[k] * gk * (1.0 - gk)
            dgate_ref[:, k * D:(k + 1) * D] = dgr.astype(bf16)
            dbg_ref[:, k * D:(k + 1) * D] += jnp.sum(dgr, axis=0, keepdims=True)
        dy_s = dm * r["gates"][0]
        sg2 = r["sg2"]
        dz = jnp.concatenate([dy_s * sg2, dy_s * r["z1"] * sg2 * (1.0 - sg2)], axis=1).astype(bf16)
        dz_ref[...] = dz
        dys_ref[...] = _dot_nt(dz, wglu_ref[...]) * _gelu_grad(r["ys"])
        dya = (dm * r["gates"][1]).astype(bf16)
        dya_ref[...] = dya
        do = _dot_nt(dya, watt_ref[...])
        do_ref[...] = do.astype(bf16)
        delta_ref[...] = _dot_hi(do * r["o"], _head_reduce())
        dyc = (dm * r["gates"][2]).astype(bf16)
        dyc_ref[...] = dyc
        sgn, hn = r["sgn"], r["hn"]
        dhn = _dot_nt(dyc, wpw_ref[...]) * sgn * (1.0 + hn * (1.0 - sgn))
        dlng_ref[...] += jnp.sum(dhn * r["xh"], axis=0, keepdims=True)
        dlnb_ref[...] += jnp.sum(dhn, axis=0, keepdims=True)
        dxh = dhn * lng_ref[...]
        xh = r["xh"]
        dhc_ref[...] = r["rstd"] * (dxh - jnp.mean(dxh, axis=-1, keepdims=True)
                                    - xh * jnp.mean(dxh * xh, axis=-1, keepdims=True))

    tok = lambda i: (i, 0)
    fix = lambda i: (0, 0)
    outs = [("dys", BW, f32), ("do", BW, bf16), ("delta", 128, f32), ("lse_tot", 128, f32), ("dhc", BW, f32),
            ("dgate", 3 * D, bf16), ("ysin", BW, bf16), ("dz", 2 * D, bf16), ("ob", BW, bf16), ("dya", D, bf16),
            ("hs", BW, bf16), ("dyc", D, bf16), ("merged", D, bf16)]
    small = [("dbg", 3 * D), ("dlng", BW), ("dlnb", BW)]
    res = pl.pallas_call(
        body, name="merge_bwd", grid=(n // TBM,), in_specs=_merge_in_specs(tok, TBM, lses),
        out_specs=[pl.BlockSpec((TBM, w), tok) for _, w, _ in outs] + [pl.BlockSpec((1, w), fix) for _, w in small],
        out_shape=[jax.ShapeDtypeStruct((n, w), dt) for _, w, dt in outs]
        + [jax.ShapeDtypeStruct((1, w), f32) for _, w in small],
        compiler_params=_cp(1, 56))(dx1, ys, *os_, *lses, hc, proj, proj, proj, bg, lng, lnb, wglu, watt, wpw, wout)
    return dict(zip([k for k, _, _ in outs] + [k for k, _ in small], res))


def assemble_dproj(du, dqs, dks, dvs, dcv, dgate):
    b, l, _ = du.shape
    nck = BW // 128

    def body(du_ref, q1, q2, q3, k1, k2, k3, v1, v2, v3, cv_ref, g_ref, o_ref):
        o_ref[0, :, 0:BW] = du_ref[0]
        for c in range(nck):
            for j, qr in enumerate((q1, q2, q3)):
                o_ref[0, :, (1 + j) * BW + c * 128:(1 + j) * BW + (c + 1) * 128] = qr[0, c].astype(bf16)
            o_ref[0, :, 4 * BW + c * 128:4 * BW + (c + 1) * 128] = (k1[0, c] + k2[0, c] + k3[0, c]).astype(bf16)
            o_ref[0, :, 5 * BW + c * 128:5 * BW + (c + 1) * 128] = (v1[0, c] + v2[0, c] + v3[0, c]).astype(bf16)
        o_ref[0, :, 6 * BW:8 * BW] = cv_ref[0]
        o_ref[0, :, 8 * BW:] = g_ref[0]

    t = lambda w: pl.BlockSpec((1, TB, w), lambda bi, i: (bi, i, 0))
    ck = pl.BlockSpec((1, nck, TB, 128), lambda bi, i: (bi, 0, i, 0))
    return pl.pallas_call(
        body, name="assemble_dproj", grid=(b, l // TB),
        in_specs=[t(BW)] + [ck] * 9 + [t(2 * BW), t(3 * D)], out_specs=t(INC),
        out_shape=jax.ShapeDtypeStruct((b, l, INC), bf16), compiler_params=_cp(2))(du, *dqs, *dks, *dvs, dcv, dgate)


def _me():
    return lax.axis_index("x"), lax.axis_index("y"), lax.axis_index("c")


def _peers():
    x, y, c = _me()
    return [(x, y, 1 - c), (1 - x, y, c), (1 - x, y, 1 - c), (x, 1 - y, c), (x, 1 - y, 1 - c),
            (1 - x, 1 - y, c), (1 - x, 1 - y, 1 - c)]


def _rank(p):
    return 4 * p[0] + 2 * p[1] + p[2]


def allgather(arrs, name):
    na = len(arrs)
    units = [(a, j) for a in range(na) for j in range(arrs[a].shape[0])]
    nu = len(units)

    def body(*refs):
        ins, outs = refs[:na], refs[na:2 * na]
        send, recv, loc = refs[2 * na:]
        me = _rank(_me())
        local, remote = [], []
        for u, (a, j) in enumerate(units):
            own = pltpu.make_async_copy(ins[a].at[j], outs[a].at[j, me], loc.at[u])
            own.start()
            local.append(own)
        for u, (a, j) in enumerate(units):
            for k, p in enumerate(_peers()):
                cp = pltpu.make_async_remote_copy(src_ref=ins[a].at[j], dst_ref=outs[a].at[j, me],
                                                  send_sem=send.at[u, k], recv_sem=recv.at[u, k],
                                                  device_id=p, device_id_type=MESH)
                cp.start()
                remote.append(cp)
        for cp in local:
            cp.wait()
        for cp in remote:
            cp.wait()

    return pl.pallas_call(
        body, name=name, in_specs=[ANY] * na, out_specs=[ANY] * na,
        out_shape=[jax.ShapeDtypeStruct((a.shape[0], NDEV) + a.shape[1:], a.dtype) for a in arrs],
        scratch_shapes=[pltpu.SemaphoreType.DMA((nu, NDEV - 1)), pltpu.SemaphoreType.DMA((nu, NDEV - 1)),
                        pltpu.SemaphoreType.DMA((nu,))])(*arrs)


HBM = pl.BlockSpec(memory_space=pltpu.HBM)
SEM = pl.BlockSpec(memory_space=pltpu.SEMAPHORE)
_EFFECT = pltpu.SideEffectType.DATAFLOW_SIDE_EFFECTING


def _push_copies(srcs, lands, send, recv, scatter):
    me = _rank(_me())
    out = []
    for i in range(len(srcs)):
        for k, p in enumerate(_peers()):
            src = srcs[i].at[_rank(p)] if scatter else srcs[i]
            dst = lands[i].at[k] if scatter else lands[i].at[me]
            j = i * (NDEV - 1) + k
            out.append(pltpu.make_async_remote_copy(src_ref=src, dst_ref=dst, send_sem=send.at[j],
                                                    recv_sem=recv.at[j], device_id=p, device_id_type=MESH))
    return out


def push_start(srcs, lands, scatter, name, token):
    n = len(srcs)
    token = jnp.zeros((8, 128), f32) if token is None else token

    def body(*refs):
        for cp in _push_copies(refs[:n], refs[n:2 * n], refs[2 * n + 1], refs[2 * n + 2], scatter):
            cp.start()
        refs[-1][...] = refs[2 * n][...]

    sems = pltpu.SemaphoreType.DMA((n * (NDEV - 1),))
    vmem = pl.BlockSpec(memory_space=pltpu.VMEM)
    res = pl.pallas_call(
        body, name=name, in_specs=[HBM] * (2 * n) + [vmem], out_specs=[SEM, SEM] + [HBM] * (2 * n) + [vmem],
        out_shape=[sems, sems] + [pltpu.HBM(a.shape, a.dtype) for a in list(srcs) + list(lands)]
        + [jax.ShapeDtypeStruct((8, 128), f32)],
        input_output_aliases={i: 2 + i for i in range(2 * n)},
        compiler_params=pltpu.CompilerParams(has_side_effects=_EFFECT),
    )(*[pltpu.with_memory_space_constraint(a, pltpu.HBM) for a in list(srcs) + list(lands)], token)
    return res[0], res[1], res[2:2 + n], res[2 + n:2 + 2 * n], res[-1]


def push_wait(send, recv, srcs, lands, after, scatter, name):
    n = len(srcs)

    def body(*refs):
        for cp in _push_copies(refs[:n], refs[n:2 * n], refs[2 * n], refs[2 * n + 1], scatter):
            cp.wait_send()
            cp.wait_recv()

    res = pl.pallas_call(
        body, name=name, in_specs=[HBM] * (2 * n) + [SEM, SEM, ANY], out_specs=[HBM] * (2 * n),
        out_shape=[pltpu.HBM(a.shape, a.dtype) for a in list(srcs) + list(lands)],
        input_output_aliases={i: i for i in range(2 * n)},
        compiler_params=pltpu.CompilerParams(has_side_effects=_EFFECT),
    )(*srcs, *lands, send, recv, after)
    return res[:n], res[n:]


_C1 = 1.0 / (1.0 - ADAM_B1 ** ADAM_STEP)
_C2 = 1.0 / (1.0 - ADAM_B2 ** ADAM_STEP)


def _adamw(w, g, m, v):
    m = ADAM_B1 * m + (1.0 - ADAM_B1) * g
    v = ADAM_B2 * v + (1.0 - ADAM_B2) * (g * g)
    delta = -ADAM_LR * ((m * _C1) / (jnp.sqrt(v * _C2) + ADAM_EPS) + ADAM_WD * w)
    return delta, m, v


def adam_big(lands, owns, w, m, v, name):
    _, k, n = lands[0].shape
    tk = k
    while tk * n * 2 * NDEV > 2 * 1024 * 1024 and tk % 16 == 0:
        tk //= 2

    def body(*refs):
        l_refs, o_refs = refs[:DEPTH], refs[DEPTH:2 * DEPTH]
        w_ref, m_ref, v_ref, g_ref, d_ref, nm_ref, nv_ref = refs[2 * DEPTH:]
        for l in range(DEPTH):
            g = o_refs[l][...].astype(f32)
            for s in range(NDEV - 1):
                g = g + l_refs[l][s].astype(f32)
            d, nm, nv = _adamw(w_ref[l], g, m_ref[l], v_ref[l])
            g_ref[l], d_ref[l], nm_ref[l], nv_ref[l] = g, d, nm, nv

    blk = pl.BlockSpec((DEPTH, tk, n), lambda i: (0, i, 0))
    return pl.pallas_call(
        body, name=name, grid=(k // tk,),
        in_specs=[pl.BlockSpec((NDEV - 1, tk, n), lambda i: (0, i, 0))] * DEPTH
        + [pl.BlockSpec((tk, n), lambda i: (i, 0))] * DEPTH + [blk, blk, blk],
        out_specs=[blk] * 4, out_shape=[jax.ShapeDtypeStruct(w.shape, f32)] * 4,
        compiler_params=_cp(1))(*lands, *owns, w, m, v)


def adam_small(gath, w, m, v):
    r = w.shape[0]
    tr = 512

    def body(g_ref, w_ref, m_ref, v_ref, go_ref, d_ref, nm_ref, nv_ref):
        g = g_ref[0]
        for s in range(1, NDEV):
            g = g + g_ref[s]
        d, nm, nv = _adamw(w_ref[...], g, m_ref[...], v_ref[...])
        go_ref[...], d_ref[...], nm_ref[...], nv_ref[...] = g, d, nm, nv

    blk = pl.BlockSpec((tr, 128), lambda i: (i, 0))
    return pl.pallas_call(
        body, name="adam_small", grid=(r // tr,),
        in_specs=[pl.BlockSpec((NDEV, tr, 128), lambda i: (0, i, 0)), blk, blk, blk],
        out_specs=[blk] * 4, out_shape=[jax.ShapeDtypeStruct((r, 128), f32)] * 4,
        compiler_params=_cp(1))(gath, w, m, v)


SMALL = ["norm1_g", "b_gate", "ssm_lambda_re", "ssm_lambda_im", "ssm_log_dt", "ssm_b_re", "ssm_b_im",
         "ssm_c_re", "ssm_c_im", "ssm_d", "conv_w", "conv_b", "conv_ln_g", "conv_ln_b", "norm2_g", "final_g"]
BIG = ["w_in", "w_ssm_glu", "w_att_up", "w_conv_pw2", "w_out", "w_ffn_in", "w_ffn_out"]
ORDER = ["norm1_g", "w_in", "b_gate", "ssm_lambda_re", "ssm_lambda_im", "ssm_log_dt", "ssm_b_re", "ssm_b_im",
         "ssm_c_re", "ssm_c_im", "ssm_d", "w_ssm_glu", "w_att_up", "conv_w", "conv_b", "conv_ln_g", "conv_ln_b",
         "w_conv_pw2", "w_out", "norm2_g", "w_ffn_in", "w_ffn_out", "final_g"]
PACK_ROWS = 2560


def _pack(arrs):
    flat = jnp.concatenate([a.reshape(-1).astype(f32) for a in arrs])
    return jnp.pad(flat, (0, PACK_ROWS * 128 - flat.shape[0])).reshape(PACK_ROWS, 128)


def _unpack(pack, shapes):
    flat = pack.reshape(-1)
    out, off = [], 0
    for s in shapes:
        sz = math.prod(s)
        out.append(flat[off:off + sz].reshape(s))
        off += sz
    return out


def _bt(b):
    return b.transpose(2, 0, 1).reshape(GH, NSTATE)


def _bt_inv(bt):
    return bt.reshape(GH, NG, NS).transpose(1, 2, 0)


def _ct(c):
    return c.transpose(1, 0, 2).reshape(GH, NSTATE)


def _ct_inv(ct):
    return ct.reshape(GH, NG, NS).transpose(1, 0, 2)


def local_step(x, loss_target, P, weights, on_grads):
    bsz, seq, _ = x.shape
    n = bsz * seq

    def natural(g3):
        return g3.transpose(1, 0, 2).reshape(g3.shape[1], NDEV * g3.shape[2])

    xs = x.reshape(n, D)
    saved = []
    conv_w_pad = None
    for l in range(DEPTH):
        S = {"x": xs}
        h1 = rms_fwd(xs, P["norm1_g"][l][None])
        G = dict(weights(l, "in", h1))
        if conv_w_pad is None:
            conv_w_full = G["conv_w"].transpose(1, 2, 0, 3).reshape(DEPTH, CW, BW)
            conv_w_pad = jnp.pad(conv_w_full, ((0, 0), (0, 1), (0, 0)))
        w_in4 = G["w_in"][None]
        proj = inproj(h1, w_in4, 0)
        proj3 = proj.reshape(bsz, seq, INC)
        lr = P["ssm_lambda_re"][l].reshape(1, NSTATE)
        li = P["ssm_lambda_im"][l].reshape(1, NSTATE)
        ld = jnp.repeat(P["ssm_log_dt"][l], NS).reshape(1, NSTATE)
        btr, bti = _bt(P["ssm_b_re"][l]), _bt(P["ssm_b_im"][l])
        t8, bb, cb = s5_params(lr, li, ld, btr, bti, _ct(P["ssm_c_re"][l]), _ct(P["ssm_c_im"][l]))
        dskip = P["ssm_d"][l][None]
        ys = s5_fwd(proj3, t8, bb, cb, dskip)
        att = [att_fwd(proj3, gi, dil) for gi, (_, dil) in enumerate(PATTERNS)]
        hc = conv_fwd(proj3, conv_w_pad[l], P["conv_b"][l][None])
        G.update(weights(l, "mix", hc))
        wts = dict(wglu=natural(G["w_ssm_glu"]), watt=natural(G["w_att_up"]),
                   wpw=natural(G["w_conv_pw2"]), wout=G["w_out"].reshape(D, D))
        mi = dict(ys=ys.reshape(n, BW), os_=[a[0].reshape(n, BW) for a in att],
                  lses=[a[1] for a in att], hc=hc.reshape(n, BW),
                  proj=proj, bg=P["b_gate"][l][None], lng=P["conv_ln_g"][l][None], lnb=P["conv_ln_b"][l][None],
                  **wts)
        x1 = merge_fwd(xs, **mi)
        G.update(weights(l, "ffn", x1))
        w_ffn = (G["w_ffn_in"][None], G["w_ffn_out"][None])
        x2 = ffn_fwd(x1, P["norm2_g"][l][None], *w_ffn, 0)
        S.update(h1=h1, proj=proj, proj3=proj3, tabs=(t8, bb, cb), mi=mi, x1=x1, w_in4=w_in4, w_ffn=w_ffn,
                 sp=(lr, li, ld, btr, bti), dskip=dskip)
        saved.append(S)
        xs = x2

    loss8, dx, dfinal = loss_head(xs, P["final_g"][None], loss_target.reshape(n, D))

    small_g = {k: [None] * DEPTH for k in SMALL if k != "final_g"}
    tokblk = lambda w: pl.BlockSpec((1024, w), lambda s, i: (i, 0))
    colblk = lambda w: pl.BlockSpec((1024, w), lambda s, i: (i, s))
    sh3blk = lambda w: pl.BlockSpec((1, 1024, w), lambda s, i: (s, i, 0))
    for l in reversed(range(DEPTH)):
        S = saved[l]
        g2 = P["norm2_g"][l][None]
        dx1, h2, dz1, dz2, a4, dg2 = ffn_bwd(S["x1"], g2, *S["w_ffn"], 0, dx)
        small_g["norm2_g"][l] = dg2
        dwa = mm_tn(h2, dz1, tokblk(D), sh3blk(NSH_FF), 4, D, NSH_FF, n, "dw_ffn_in_a")
        dwb = mm_tn(h2, dz2, tokblk(D), sh3blk(NSH_FF), 4, D, NSH_FF, n, "dw_ffn_in_b")
        on_grads(l, "ffn", dict(
            w_ffn_in=jnp.concatenate([dwa, dwb], axis=0),
            w_ffn_out=mm_tn(a4, dx, sh3blk(NSH_FF), tokblk(D), 4, NSH_FF, D, n,
                            "dw_ffn_out").reshape(NDEV, NSH_FF // 2, D)))
        mb = merge_bwd(dx1, **S["mi"])
        small_g["b_gate"][l], small_g["conv_ln_g"][l], small_g["conv_ln_b"][l] = mb["dbg"], mb["dlng"], mb["dlnb"]
        on_grads(l, "mix", dict(
            w_ssm_glu=mm_tn(mb["ysin"], mb["dz"], tokblk(BW), colblk(256), NDEV, BW, 256, n, "dw_glu"),
            w_att_up=mm_tn(mb["ob"], mb["dya"], tokblk(BW), colblk(128), NDEV, BW, 128, n, "dw_att"),
            w_conv_pw2=mm_tn(mb["hs"], mb["dyc"], tokblk(BW), colblk(128), NDEV, BW, 128, n, "dw_pw2"),
            w_out=mm_tn(mb["merged"], dx1, tokblk(D), tokblk(D), 1, D, D, n, "dw_out").reshape(NDEV, D // NDEV, D)))
        dcv, dcw, dcb = conv_bwd(S["proj3"], mb["dhc"].reshape(bsz, seq, BW), conv_w_pad[l])
        small_g["conv_w"][l] = dcw[:CW].reshape(CW, NDEV, BW // NDEV).transpose(1, 0, 2)
        small_g["conv_b"][l] = dcb
        ab = [att_bwd(S["proj3"], mb["do"].reshape(bsz, seq, BW), mb["lse_tot"].reshape(bsz, seq, 128),
                      mb["delta"].reshape(bsz, seq, 128), gi, dil) for gi, (_, dil) in enumerate(PATTERNS)]
        t8, bb, cb = S["tabs"]
        du, d_a, d_bb, d_cb, d_d = s5_bwd(S["proj3"], mb["dys"].reshape(bsz, seq, BW), t8, bb, cb, S["dskip"])
        lr, li, ld, btr, bti = S["sp"]
        dlr, dli, dld, dbt, dct = s5_params_bwd(lr, li, ld, btr, bti, d_a, d_bb, d_cb)
        small_g["ssm_lambda_re"][l], small_g["ssm_lambda_im"][l] = dlr.reshape(NG, NS), dli.reshape(NG, NS)
        small_g["ssm_log_dt"][l] = dld[0, :NG]
        small_g["ssm_b_re"][l], small_g["ssm_b_im"][l] = _bt_inv(dbt[0]), _bt_inv(dbt[1])
        small_g["ssm_c_re"][l], small_g["ssm_c_im"][l] = _ct_inv(dct[0]), _ct_inv(dct[1])
        small_g["ssm_d"][l] = d_d
        dproj = assemble_dproj(du, [a[0] for a in ab], [a[1] for a in ab], [a[2] for a in ab],
                               dcv, mb["dgate"].reshape(bsz, seq, 3 * D)).reshape(n, INC)
        toks = [on_grads(l, "in", dict(w_in=mm_tn(S["h1"], dproj, tokblk(D), colblk(NSH_IN), NDEV, D, NSH_IN, n,
                                                  "dw_in")))]
        if l == 0:
            toks.append(on_grads(l, "small", dict(small_g=small_g, loss8=loss8, dfinal=dfinal)))
        g1 = P["norm1_g"][l][None]
        for tok in toks:
            if tok is not None:
                g1 = g1 + tok[0:1, 0:1]
        dx, dg1 = inproj_bwd(dproj, S["w_in4"], 0, S["x"], g1, dx1)
        small_g["norm1_g"][l] = dg1
    return loss8, dx, dfinal, small_g


def kernel(x, norm1_g, w_in, b_gate, ssm_lambda_re, ssm_lambda_im, ssm_log_dt, ssm_b_re, ssm_b_im, ssm_c_re, ssm_c_im, ssm_d, w_ssm_glu, w_att_up, conv_w, conv_b, conv_ln_g, conv_ln_b, w_conv_pw2, w_out, norm2_g, w_ffn_in, w_ffn_out, final_g, loss_target, m_norm1_g, m_w_in, m_b_gate, m_ssm_lambda_re, m_ssm_lambda_im, m_ssm_log_dt, m_ssm_b_re, m_ssm_b_im, m_ssm_c_re, m_ssm_c_im, m_ssm_d, m_w_ssm_glu, m_w_att_up, m_conv_w, m_conv_b, m_conv_ln_g, m_conv_ln_b, m_w_conv_pw2, m_w_out, m_norm2_g, m_w_ffn_in, m_w_ffn_out, m_final_g, v_norm1_g, v_w_in, v_b_gate, v_ssm_lambda_re, v_ssm_lambda_im, v_ssm_log_dt, v_ssm_b_re, v_ssm_b_im, v_ssm_c_re, v_ssm_c_im, v_ssm_d, v_w_ssm_glu, v_w_att_up, v_conv_w, v_conv_b, v_conv_ln_g, v_conv_ln_b, v_w_conv_pw2, v_w_out, v_norm2_g, v_w_ffn_in, v_w_ffn_out, v_final_g):
    args = dict(locals())
    W = {k: args[k] for k in ORDER}
    M = {k: args["m_" + k] for k in ORDER}
    V = {k: args["v_" + k] for k in ORDER}
    bsz, seq, _ = x.shape
    n = bsz * seq
    me = 4 * lax.axis_index("x") + 2 * lax.axis_index("y") + lax.axis_index("c")

    groups = {"in": ["w_in"], "mix": ["w_ssm_glu", "w_att_up", "w_conv_pw2", "w_out"], "ffn": ["w_ffn_in", "w_ffn_out"]}
    wb = {k: W[k].astype(bf16) for k in BIG}

    def landing(shard):
        return lax.dynamic_update_index_in_dim(lax.empty((NDEV,) + shard.shape, shard.dtype), shard, me, 0)

    plan = [("gather_a", [("w_in", 0), ("conv_w", None)]),
            ("gather_b", [(k, 0) for k in groups["mix"] + groups["ffn"]]),
            ("gather_c", [(k, 1) for k in BIG])]
    pending, token = {}, None
    for name, items in plan:
        shards = [conv_w if l is None else wb[k][l] for k, l in items]
        send, recv, s_thru, l_thru, token = push_start(shards, [landing(s) for s in shards], False, name, token)
        pending[name] = (send, recv, s_thru, l_thru, items)
    gathered = {}

    def weights(l, group, after):
        name = "gather_c" if l == 1 else ("gather_a" if group == "in" else "gather_b")
        if name in pending:
            send, recv, s_thru, l_thru, items = pending.pop(name)
            for item, arr in zip(items, push_wait(send, recv, s_thru, l_thru, after, False, name + "_wait")[1]):
                gathered[item] = arr
        res = {k: gathered[(k, l)] for k in groups[group]}
        if group == "in":
            res["conv_w"] = gathered[("conv_w", None)]
        return res

    big_g = {k: [None] * DEPTH for k in BIG}
    flights = []

    def start_exchange(items, name):
        parts = [big_g[k][l] for k, l in items]
        lands = [lax.empty((NDEV - 1,) + p.shape[1:], p.dtype) for p in parts]
        send, recv, s_thru, l_thru, tok = push_start(parts, lands, True, name, None)
        flights.append((send, recv, s_thru, l_thru, items, name))
        return tok

    names = [k for k in SMALL if k != "final_g"]
    shapes = [(DEPTH, NDEV, CW, BW // NDEV) if k == "conv_w" else W[k].shape for k in names] + [(D,), (1,)]
    small_flight = []

    def start_small(small_g, loss8, dfinal):
        sg_ = dict(small_g, norm1_g=[jnp.zeros((1, D), f32), small_g["norm1_g"][1]])
        gpack = _pack([jnp.stack([g.reshape(shapes[i][1:]) for g in sg_[k]]) for i, k in enumerate(names)]
                      + [dfinal, loss8[0, :1]])
        send, recv, s_thru, l_thru, tok = push_start([gpack], [landing(gpack)], False, "gather_small", None)
        small_flight.append((send, recv, s_thru, l_thru))
        return tok

    def on_grads(l, group, grads):
        if group == "small":
            return start_small(**grads)
        for k, g in grads.items():
            big_g[k][l] = g
        if l == 1 and group == "in":
            return start_exchange([(k, 1) for k in BIG], "exchange_l1")
        if l == 0:
            return start_exchange([(k, 0) for k in groups[group]], "exchange_l0_" + group)
        return None

    loss8, dx, dfinal, small_g = local_step(x, loss_target, W, weights, on_grads)

    landed, own = {}, {}
    for send, recv, s_thru, l_thru, items, name in flights:
        srcs, lands = push_wait(send, recv, s_thru, l_thru, dx, True, name + "_wait")
        for item, src, land in zip(items, srcs, lands):
            landed[item] = land
            own[item] = lax.dynamic_index_in_dim(src, me, 0, keepdims=False)
    out = {}
    for k in BIG:
        items = [(k, l) for l in range(DEPTH)]
        out[k] = adam_big([landed[i] for i in items], [own[i] for i in items], W[k], M[k], V[k], "adam_" + k)

    def wpack(src):
        parts = [jnp.broadcast_to(src[k][:, None], shapes[i]) if k == "conv_w" else src[k] for i, k in enumerate(names)]
        return _pack(parts + [src["final_g"], jnp.ones((1,), f32)])

    send, recv, s_thru, l_thru = small_flight[0]
    gall = push_wait(send, recv, s_thru, l_thru, dx, False, "gather_small_wait")[1][0]
    (late,) = allgather([small_g["norm1_g"][0].reshape(1, D // 128, 128)], "allgather_late")
    gall = lax.dynamic_update_slice(gall, late[0], (0, 0, 0))
    sg, sd, sm, sv = [_unpack(p, shapes) for p in adam_small(gall, wpack(W), wpack(M), wpack(V))]
    for i, k in enumerate(names + ["final_g"]):
        vals = [t[i] for t in (sg, sd, sm, sv)]
        if k == "conv_w":
            vals = [lax.dynamic_index_in_dim(t, me, axis=1, keepdims=False) for t in vals]
        out[k] = vals
    loss = sg[-1].reshape(())

    res = [loss, dx.reshape(bsz, seq, D)]
    for j in range(4):
        res += [out[k][j] for k in ORDER]
    return tuple(res)
```

```python
import functools
import math

import jax
import jax.numpy as jnp
from jax import lax
from jax.experimental import pallas as pl
from jax.experimental.pallas import tpu as pltpu

f32 = jnp.float32
bf16 = jnp.bfloat16

D = 1024
DEPTH = 2
EPS = 1e-6
BW = 512
NG = 32
GH = 16
NS = 64
NSTATE = NG * NS
HD = 64
NH = 8
PATTERNS = ((128, 1), (512, 4), (2048, 16))
ABLK = 128
ATT_SCALE = HD ** -0.5
CW = 31
DFF = 2816
INC = 7168
NDEV = 8
NSH_IN = INC // NDEV
NSH_FF = 2 * DFF // NDEV
ADAM_LR, ADAM_B1, ADAM_B2, ADAM_EPS, ADAM_WD, ADAM_STEP = 0.001, 0.9, 0.999, 1e-08, 0.01, 10

TB = 512
SJ = 4
SW = NSTATE // SJ
SU = BW // SJ
NEG = -1e30
MESH = pl.DeviceIdType.MESH
ANY = pl.BlockSpec(memory_space=pl.ANY)


def _cp(n_axes, vmem_mb=48):
    return pltpu.CompilerParams(dimension_semantics=("arbitrary",) * n_axes,
                                vmem_limit_bytes=vmem_mb * 1024 * 1024)


def _dot(a, b):
    return jnp.dot(a, b, preferred_element_type=f32)


def _dot_nt(a, b):
    return lax.dot_general(a, b, (((1,), (1,)), ((), ())), preferred_element_type=f32)


def _dot_tn(a, b):
    return lax.dot_general(a, b, (((0,), (0,)), ((), ())), preferred_element_type=f32)


def _dot_hi(a, b):
    return jnp.dot(a, b, precision=lax.Precision.HIGHEST, preferred_element_type=f32)


def _sigmoid(x):
    return 1.0 / (1.0 + jnp.exp(-x))


_GC = math.sqrt(2.0 / math.pi)


def _gelu(x):
    return 0.5 * x * (1.0 + jnp.tanh(_GC * (x + 0.044715 * x * x * x)))


def _gelu_grad(x):
    t = jnp.tanh(_GC * (x + 0.044715 * x * x * x))
    return 0.5 * (1.0 + t) + 0.5 * x * (1.0 - t * t) * _GC * (1.0 + 3.0 * 0.044715 * x * x)


def _rms_stats(x):
    return lax.rsqrt(jnp.mean(x * x, axis=-1, keepdims=True) + EPS)


def _rms_bwd(x, g, dh):
    r = _rms_stats(x)
    dyg = dh * g
    dx = r * dyg - x * (r * r * r) * jnp.mean(dyg * x, axis=-1, keepdims=True)
    dg = jnp.sum(dh * x * r, axis=0, keepdims=True)
    return dx, dg


def rms_fwd(x, g):
    n = x.shape[0]

    def body(x_ref, g_ref, h_ref):
        xv = x_ref[...]
        h_ref[...] = (xv * _rms_stats(xv) * g_ref[...]).astype(bf16)

    return pl.pallas_call(
        body, name="rms_fwd", grid=(n // TB,),
        in_specs=[pl.BlockSpec((TB, D), lambda i: (i, 0)), pl.BlockSpec((1, D), lambda i: (0, 0))],
        out_specs=pl.BlockSpec((TB, D), lambda i: (i, 0)),
        out_shape=jax.ShapeDtypeStruct((n, D), bf16), compiler_params=_cp(1))(x, g)


def inproj(h, w4, layer):
    n = h.shape[0]
    tm = 1024

    def body(h_ref, w_ref, o_ref):
        o_ref[...] = _dot(h_ref[...], w_ref[0, 0]).astype(bf16)

    return pl.pallas_call(
        body, name="inproj", grid=(NDEV, n // tm),
        in_specs=[pl.BlockSpec((tm, D), lambda s, i: (i, 0)),
                  pl.BlockSpec((1, 1, D, NSH_IN), lambda s, i: (layer, s, 0, 0))],
        out_specs=pl.BlockSpec((tm, NSH_IN), lambda s, i: (i, s)),
        out_shape=jax.ShapeDtypeStruct((n, INC), bf16), compiler_params=_cp(2))(h, w4)


def inproj_bwd(dproj, w4, layer, x, g, dres):
    n = x.shape[0]
    tm = 1024

    def body(dp_ref, w_ref, x_ref, g_ref, dr_ref, dx_ref, dg_ref, acc):
        i, s = pl.program_id(0), pl.program_id(1)

        @pl.when(s == 0)
        def _():
            acc[...] = jnp.zeros_like(acc)

        @pl.when((s == 0) & (i == 0))
        def _():
            dg_ref[...] = jnp.zeros_like(dg_ref)

        acc[...] += _dot_nt(dp_ref[...], w_ref[0, 0])

        @pl.when(s == NDEV - 1)
        def _():
            dx, dg = _rms_bwd(x_ref[...], g_ref[...], acc[...])
            dx_ref[...] = dr_ref[...] + dx
            dg_ref[...] += dg

    return pl.pallas_call(
        body, name="inproj_bwd", grid=(n // tm, NDEV),
        in_specs=[pl.BlockSpec((tm, NSH_IN), lambda i, s: (i, s)),
                  pl.BlockSpec((1, 1, D, NSH_IN), lambda i, s: (layer, s, 0, 0)),
                  pl.BlockSpec((tm, D), lambda i, s: (i, 0)),
                  pl.BlockSpec((1, D), lambda i, s: (0, 0)),
                  pl.BlockSpec((tm, D), lambda i, s: (i, 0))],
        out_specs=[pl.BlockSpec((tm, D), lambda i, s: (i, 0)), pl.BlockSpec((1, D), lambda i, s: (0, 0))],
        out_shape=[jax.ShapeDtypeStruct((n, D), f32), jax.ShapeDtypeStruct((1, D), f32)],
        scratch_shapes=[pltpu.VMEM((tm, D), f32)], compiler_params=_cp(2))(dproj, w4, x, g, dres)


def mm_tn(a, b, a_spec, b_spec, n_sh, ka, nb, m, name):
    tm = 1024

    def body(a_ref, b_ref, o_ref, acc):
        i = pl.program_id(1)

        @pl.when(i == 0)
        def _():
            acc[...] = jnp.zeros_like(acc)

        av = a_ref[...].reshape(tm, ka).astype(bf16)
        bv = b_ref[...].reshape(tm, nb).astype(bf16)
        acc[...] += _dot_tn(av, bv)

        @pl.when(i == m // tm - 1)
        def _():
            o_ref[0] = acc[...].astype(bf16)

    return pl.pallas_call(
        body, name=name, grid=(n_sh, m // tm), in_specs=[a_spec, b_spec],
        out_specs=pl.BlockSpec((1, ka, nb), lambda s, i: (s, 0, 0)),
        out_shape=jax.ShapeDtypeStruct((n_sh, ka, nb), bf16),
        scratch_shapes=[pltpu.VMEM((ka, nb), f32)], compiler_params=_cp(2))(a, b)


def ffn_fwd(x1, g2, w1, w2, layer):
    n = x1.shape[0]
    w2p = w2.reshape(w2.shape[0], 4, NSH_FF, D)

    def body(x_ref, g_ref, wa_ref, wb_ref, w2_ref, o_ref, h_sc):
        s = pl.program_id(1)

        @pl.when(s == 0)
        def _():
            xv = x_ref[...]
            h_sc[...] = (xv * _rms_stats(xv) * g_ref[...]).astype(bf16)
            o_ref[...] = xv

        h = h_sc[...]
        z1 = _dot(h, wa_ref[0, 0])
        z2 = _dot(h, wb_ref[0, 0])
        a = (z1 * _sigmoid(z1) * z2).astype(bf16)
        o_ref[...] += _dot(a, w2_ref[0, 0])

    return pl.pallas_call(
        body, name="ffn_fwd", grid=(n // TB, 4),
        in_specs=[pl.BlockSpec((TB, D), lambda i, s: (i, 0)),
                  pl.BlockSpec((1, D), lambda i, s: (0, 0)),
                  pl.BlockSpec((1, 1, D, NSH_FF), lambda i, s: (layer, s, 0, 0)),
                  pl.BlockSpec((1, 1, D, NSH_FF), lambda i, s: (layer, s + 4, 0, 0)),
                  pl.BlockSpec((1, 1, NSH_FF, D), lambda i, s: (layer, s, 0, 0))],
        out_specs=pl.BlockSpec((TB, D), lambda i, s: (i, 0)),
        out_shape=jax.ShapeDtypeStruct((n, D), f32),
        scratch_shapes=[pltpu.VMEM((TB, D), bf16)], compiler_params=_cp(2))(x1, g2, w1, w1, w2p)


def ffn_bwd(x1, g2, w1, w2, layer, dx2):
    n = x1.shape[0]
    w2p = w2.reshape(w2.shape[0], 4, NSH_FF, D)

    def body(x_ref, g_ref, dy_ref, wa_ref, wb_ref, w2_ref,
             dx_ref, h_ref, dz1_ref, dz2_ref, a_ref, dg_ref, dh_sc, dyb_sc):
        i, s = pl.program_id(0), pl.program_id(1)

        @pl.when(s == 0)
        def _():
            xv = x_ref[...]
            h_ref[...] = (xv * _rms_stats(xv) * g_ref[...]).astype(bf16)
            dh_sc[...] = jnp.zeros_like(dh_sc)
            dyb_sc[...] = dy_ref[...].astype(bf16)

        @pl.when((s == 0) & (i == 0))
        def _():
            dg_ref[...] = jnp.zeros_like(dg_ref)

        h = h_ref[...]
        z1 = _dot(h, wa_ref[0, 0])
        z2 = _dot(h, wb_ref[0, 0])
        sg = _sigmoid(z1)
        sl = z1 * sg
        a_ref[0] = (sl * z2).astype(bf16)
        da = _dot_nt(dyb_sc[...], w2_ref[0, 0])
        dz2 = (da * sl).astype(bf16)
        dz1 = (da * z2 * sg * (1.0 + z1 * (1.0 - sg))).astype(bf16)
        dz1_ref[0] = dz1
        dz2_ref[0] = dz2
        dh_sc[...] += _dot_nt(dz1, wa_ref[0, 0]) + _dot_nt(dz2, wb_ref[0, 0])

        @pl.when(s == 3)
        def _():
            dx, dg = _rms_bwd(x_ref[...], g_ref[...], dh_sc[...])
            dx_ref[...] = dy_ref[...] + dx
            dg_ref[...] += dg

    tok = lambda i, s: (i, 0)
    sh3 = lambda i, s: (s, i, 0)
    return pl.pallas_call(
        body, name="ffn_bwd", grid=(n // TB, 4),
        in_specs=[pl.BlockSpec((TB, D), tok), pl.BlockSpec((1, D), lambda i, s: (0, 0)), pl.BlockSpec((TB, D), tok),
                  pl.BlockSpec((1, 1, D, NSH_FF), lambda i, s: (layer, s, 0, 0)),
                  pl.BlockSpec((1, 1, D, NSH_FF), lambda i, s: (layer, s + 4, 0, 0)),
                  pl.BlockSpec((1, 1, NSH_FF, D), lambda i, s: (layer, s, 0, 0))],
        out_specs=[pl.BlockSpec((TB, D), tok), pl.BlockSpec((TB, D), tok),
                   pl.BlockSpec((1, TB, NSH_FF), sh3), pl.BlockSpec((1, TB, NSH_FF), sh3),
                   pl.BlockSpec((1, TB, NSH_FF), sh3), pl.BlockSpec((1, D), lambda i, s: (0, 0))],
        out_shape=[jax.ShapeDtypeStruct((n, D), f32), jax.ShapeDtypeStruct((n, D), bf16),
                   jax.ShapeDtypeStruct((4, n, NSH_FF), bf16), jax.ShapeDtypeStruct((4, n, NSH_FF), bf16),
                   jax.ShapeDtypeStruct((4, n, NSH_FF), bf16), jax.ShapeDtypeStruct((1, D), f32)],
        scratch_shapes=[pltpu.VMEM((TB, D), f32), pltpu.VMEM((TB, D), bf16)],
        compiler_params=_cp(2))(x1, g2, dx2, w1, w1, w2p)


def loss_head(x, g, target):
    n = x.shape[0]

    def body(x_ref, g_ref, t_ref, l_ref, dx_ref, dg_ref):
        i = pl.program_id(0)

        @pl.when(i == 0)
        def _():
            l_ref[...] = jnp.zeros_like(l_ref)
            dg_ref[...] = jnp.zeros_like(dg_ref)

        xv = x_ref[...]
        y = xv * _rms_stats(xv) * g_ref[...]
        e = y - t_ref[...]
        l_ref[...] += 0.5 * jnp.sum(jnp.sum(e * e, axis=-1, keepdims=True), axis=0, keepdims=True) * (1.0 / D)
        dx, dg = _rms_bwd(xv, g_ref[...], e * (1.0 / D))
        dx_ref[...] = dx
        dg_ref[...] += dg

    tok = lambda i: (i, 0)
    return pl.pallas_call(
        body, name="loss_head", grid=(n // TB,),
        in_specs=[pl.BlockSpec((TB, D), tok), pl.BlockSpec((1, D), lambda i: (0, 0)), pl.BlockSpec((TB, D), tok)],
        out_specs=[pl.BlockSpec((8, 128), lambda i: (0, 0)), pl.BlockSpec((TB, D), tok),
                   pl.BlockSpec((1, D), lambda i: (0, 0))],
        out_shape=[jax.ShapeDtypeStruct((8, 128), f32), jax.ShapeDtypeStruct((n, D), f32),
                   jax.ShapeDtypeStruct((1, D), f32)],
        compiler_params=_cp(1))(x, g, target)


def _disc(lr, li, ld):
    dt = jnp.exp(ld)
    mag = jnp.exp(lr * dt)
    ar = mag * jnp.cos(li * dt)
    ai = mag * jnp.sin(li * dt)
    nr, ni = ar - 1.0, ai
    den = lr * lr + li * li
    zr = (nr * lr + ni * li) / den
    zi = (ni * lr - nr * li) / den
    return ar, ai, zr, zi


def _blockdiag_mask(shape):
    r = lax.broadcasted_iota(jnp.int32, shape, 0) // GH
    c = lax.broadcasted_iota(jnp.int32, shape, 1) // NS
    return r == c


def s5_params(lr, li, ld, btr, bti, ctr, cti):
    def body(lr_ref, li_ref, ld_ref, btr_ref, bti_ref, ctr_ref, cti_ref, t8_ref, bb_ref, cb_ref):
        ar, ai, zr, zi = _disc(lr_ref[...], li_ref[...], ld_ref[...])
        pr_, pi_ = ar, ai
        pw2 = []
        for k in range(4):
            pw2.append((pr_, pi_))
            pr_, pi_ = pr_ * pr_ - pi_ * pi_, 2.0 * pr_ * pi_
        cm = lambda p, q: (p[0] * q[0] - p[1] * q[1], p[0] * q[1] + p[1] * q[0])
        pw = {1: pw2[0], 2: pw2[1], 4: pw2[2], 8: pw2[3]}
        pw[3], pw[5], pw[6] = cm(pw[2], pw[1]), cm(pw[4], pw[1]), cm(pw[4], pw[2])
        pw[7] = cm(pw[4], pw[3])
        row = lax.broadcasted_iota(jnp.int32, (8, NSTATE), 0)
        zero = jnp.zeros((8, NSTATE), f32)
        for c in range(2):
            for k in range(3):
                full = jnp.broadcast_to(pw2[k][c], (8, NSTATE))
                t8_ref[c, k] = jnp.where(row >= (1 << k), full, 0.0)
                t8_ref[c, 3 + k] = jnp.where(row + (1 << k) < 8, full, 0.0)
            up, down = zero, zero
            for j in range(8):
                up = up + jnp.where(row == j, pw[j + 1][c], 0.0)
                down = down + jnp.where(row == j, pw[8 - j][c], 0.0)
            t8_ref[c, 6] = up
            t8_ref[c, 7] = down
        bbr = zr * btr_ref[...] - zi * bti_ref[...]
        bbi = zr * bti_ref[...] + zi * btr_ref[...]
        mask = _blockdiag_mask((SU, SW))
        for j in range(SJ):
            cols = slice(j * SW, (j + 1) * SW)
            for c, (vb, vc) in enumerate(((bbr, ctr_ref[...]), (bbi, cti_ref[...]))):
                bb_ref[c, j] = jnp.where(mask, jnp.tile(vb[:, cols], (SU // GH, 1)), 0.0).astype(bf16)
                cb_ref[c, j] = jnp.where(mask, jnp.tile(vc[:, cols], (SU // GH, 1)), 0.0).astype(bf16)

    return pl.pallas_call(
        body, name="s5_params",
        out_shape=[jax.ShapeDtypeStruct((2, 8, 8, NSTATE), f32),
                   jax.ShapeDtypeStruct((2, SJ, SU, SW), bf16), jax.ShapeDtypeStruct((2, SJ, SU, SW), bf16)],
        compiler_params=pltpu.CompilerParams(vmem_limit_bytes=56 * 1024 * 1024))(lr, li, ld, btr, bti, ctr, cti)


def s5_params_bwd(lr, li, ld, btr, bti, d_a, d_bb, d_cb):
    def body(lr_ref, li_ref, ld_ref, btr_ref, bti_ref, da_ref, dbb_ref, dcb_ref,
             dlr_ref, dli_ref, dld_ref, dbt_ref, dct_ref):
        mask = _blockdiag_mask((SU, SW))

        def fold(ref, c):
            parts = []
            for j in range(SJ):
                v = jnp.where(mask, ref[c, j], 0.0)
                parts.append(v.reshape(SU // GH, GH, SW).sum(axis=0))
            return jnp.concatenate(parts, axis=1)

        dct_ref[0] = fold(dcb_ref, 0)
        dct_ref[1] = fold(dcb_ref, 1)
        dbbr, dbbi = fold(dbb_ref, 0), fold(dbb_ref, 1)
        lrv, liv, ldv = lr_ref[...], li_ref[...], ld_ref[...]
        (ar, ai, zr, zi), vjp = jax.vjp(_disc, lrv, liv, ldv)
        btr, bti = btr_ref[...], bti_ref[...]
        dbt_ref[0] = zr * dbbr + zi * dbbi
        dbt_ref[1] = zr * dbbi - zi * dbbr
        dzr = jnp.sum(dbbr * btr + dbbi * bti, axis=0, keepdims=True)
        dzi = jnp.sum(dbbi * btr - dbbr * bti, axis=0, keepdims=True)
        dlr, dli, dld = vjp((da_ref[0:1, :], da_ref[1:2, :], dzr, dzi))
        dlr_ref[...] = dlr
        dli_ref[...] = dli
        ind = (lax.broadcasted_iota(jnp.int32, (NSTATE, 128), 0) // NS
               == lax.broadcasted_iota(jnp.int32, (NSTATE, 128), 1)).astype(f32)
        dld_ref[...] = _dot_hi(jnp.broadcast_to(dld, (8, NSTATE)), ind)

    return pl.pallas_call(
        body, name="s5_params_bwd",
        out_shape=[jax.ShapeDtypeStruct((1, NSTATE), f32), jax.ShapeDtypeStruct((1, NSTATE), f32),
                   jax.ShapeDtypeStruct((8, 128), f32), jax.ShapeDtypeStruct((2, GH, NSTATE), f32),
                   jax.ShapeDtypeStruct((2, GH, NSTATE), f32)],
        compiler_params=pltpu.CompilerParams(vmem_limit_bytes=56 * 1024 * 1024))(lr, li, ld, btr, bti, d_a, d_bb, d_cb)


def _fma(sr, si, ar, ai, qr, qi):
    return sr + ar * qr - ai * qi, si + ar * qi + ai * qr


def _scan_tile(sr, si, cr, ci, t8_ref, reverse):
    sg = -1.0 if reverse else 1.0
    for k in range(3):
        tk = 3 + k if reverse else k
        rot = 8 - (1 << k) if reverse else 1 << k
        sr, si = _fma(sr, si, t8_ref[0, tk], sg * t8_ref[1, tk], pltpu.roll(sr, rot, 0), pltpu.roll(si, rot, 0))
    tp = 7 if reverse else 6
    sr, si = _fma(sr, si, t8_ref[0, tp], sg * t8_ref[1, tp], cr, ci)
    e = 0 if reverse else 7
    return sr, si, jnp.broadcast_to(sr[e:e + 1, :], sr.shape), jnp.broadcast_to(si[e:e + 1, :], si.shape)


S5MC = 512


def _s5_input_map(u_ref, bb_ref, sr_sc, si_sc, l):
    for c in range(l // S5MC):
        rows = slice(c * S5MC, (c + 1) * S5MC)
        u = u_ref[0, rows, :]
        sr_sc[rows, :] = _dot(u, bb_ref[0, 0])
        si_sc[rows, :] = _dot(u, bb_ref[1, 0])


def _s5_forward_scan(sr_sc, si_sc, t8_ref, l):
    def step(k, carry):
        rows = pl.ds(pl.multiple_of(k * 8, 8), 8)
        sr, si, cr, ci = _scan_tile(sr_sc[rows, :], si_sc[rows, :], carry[0], carry[1], t8_ref, False)
        sr_sc[rows, :] = sr
        si_sc[rows, :] = si
        return cr, ci

    zero = jnp.zeros((8, SW), f32)
    lax.fori_loop(0, l // 8, step, (zero, zero), unroll=4)


def s5_fwd(proj3, t8, bb, cb, dskip):
    b, l, _ = proj3.shape

    def body(u_ref, t8_ref, bb_ref, cb_ref, d_ref, y_ref, sr_sc, si_sc):
        _s5_input_map(u_ref, bb_ref, sr_sc, si_sc, l)
        _s5_forward_scan(sr_sc, si_sc, t8_ref, l)
        for c in range(l // S5MC):
            rows = slice(c * S5MC, (c + 1) * S5MC)
            y = (_dot_nt(sr_sc[rows, :].astype(bf16), cb_ref[0, 0])
                 - _dot_nt(si_sc[rows, :].astype(bf16), cb_ref[1, 0]))
            y_ref[0, rows, :] = y + d_ref[...] * u_ref[0, rows, :].astype(f32)

    return pl.pallas_call(
        body, name="s5_fwd", grid=(SJ, b),
        in_specs=[pl.BlockSpec((1, l, SU), lambda j, bi: (bi, 0, j)),
                  pl.BlockSpec((2, 8, 8, SW), lambda j, bi: (0, 0, 0, j)),
                  pl.BlockSpec((2, 1, SU, SW), lambda j, bi: (0, j, 0, 0)),
                  pl.BlockSpec((2, 1, SU, SW), lambda j, bi: (0, j, 0, 0)),
                  pl.BlockSpec((1, SU), lambda j, bi: (0, j))],
        out_specs=pl.BlockSpec((1, l, SU), lambda j, bi: (bi, 0, j)),
        out_shape=jax.ShapeDtypeStruct((b, l, BW), f32),
        scratch_shapes=[pltpu.VMEM((l, SW), f32)] * 2, compiler_params=_cp(2))(proj3, t8, bb, cb, dskip)


def s5_bwd(proj3, dy, t8, bb, cb, dskip):
    b, l, _ = proj3.shape
    nt = l // 8

    def body(u_ref, dy_ref, t8_ref, bb_ref, cb_ref, d_ref,
             du_ref, da_ref, dbb_ref, dcb_ref, dd_ref, sr_sc, si_sc, gr_sc, gi_sc):
        bi = pl.program_id(1)

        @pl.when(bi == 0)
        def _():
            da_ref[...] = jnp.zeros_like(da_ref)
            dbb_ref[...] = jnp.zeros_like(dbb_ref)
            dcb_ref[...] = jnp.zeros_like(dcb_ref)
            dd_ref[...] = jnp.zeros_like(dd_ref)

        _s5_input_map(u_ref, bb_ref, sr_sc, si_sc, l)
        _s5_forward_scan(sr_sc, si_sc, t8_ref, l)
        for c in range(l // S5MC):
            rows = slice(c * S5MC, (c + 1) * S5MC)
            dyb = dy_ref[0, rows, :].astype(bf16)
            gr_sc[rows, :] = _dot(dyb, cb_ref[0, 0])
            gi_sc[rows, :] = -_dot(dyb, cb_ref[1, 0])

        row = lax.broadcasted_iota(jnp.int32, (8, SW), 0)

        def step(i, carry):
            cr, ci, dar, dai = carry
            k = nt - 1 - i
            rows = pl.ds(pl.multiple_of(k * 8, 8), 8)
            gr, gi, cr, ci = _scan_tile(gr_sc[rows, :], gi_sc[rows, :], cr, ci, t8_ref, True)
            gr_sc[rows, :] = gr
            gi_sc[rows, :] = gi
            before = pl.ds(pl.multiple_of(jnp.maximum(k - 1, 0) * 8, 8), 8)
            live = jnp.where(k > 0, 1.0, 0.0)
            sr, si = sr_sc[rows, :], si_sc[rows, :]
            spr = jnp.where(row == 0, live * sr_sc[before, :][7:8, :], pltpu.roll(sr, 1, 0))
            spi = jnp.where(row == 0, live * si_sc[before, :][7:8, :], pltpu.roll(si, 1, 0))
            return cr, ci, dar + spr * gr + spi * gi, dai + spr * gi - spi * gr

        zero = jnp.zeros((8, SW), f32)
        _, _, dar, dai = lax.fori_loop(0, nt, step, (zero, zero, zero, zero), unroll=2)
        da_ref[0:1, :] += jnp.sum(dar, axis=0, keepdims=True)
        da_ref[1:2, :] += jnp.sum(dai, axis=0, keepdims=True)

        for c in range(l // S5MC):
            rows = slice(c * S5MC, (c + 1) * S5MC)
            u = u_ref[0, rows, :]
            dyv = dy_ref[0, rows, :]
            dyb = dyv.astype(bf16)
            grb, gib = gr_sc[rows, :].astype(bf16), gi_sc[rows, :].astype(bf16)
            dcb_ref[0, 0] += _dot_tn(dyb, sr_sc[rows, :].astype(bf16))
            dcb_ref[1, 0] -= _dot_tn(dyb, si_sc[rows, :].astype(bf16))
            dbb_ref[0, 0] += _dot_tn(u, grb)
            dbb_ref[1, 0] += _dot_tn(u, gib)
            du = _dot_nt(grb, bb_ref[0, 0]) + _dot_nt(gib, bb_ref[1, 0]) + d_ref[...] * dyv
            du_ref[0, rows, :] = du.astype(bf16)
            dd_ref[...] += jnp.sum(dyv * u.astype(f32), axis=0, keepdims=True)

    seq = pl.BlockSpec((1, l, SU), lambda j, bi: (bi, 0, j))
    tab = pl.BlockSpec((2, 1, SU, SW), lambda j, bi: (0, j, 0, 0))
    return pl.pallas_call(
        body, name="s5_bwd", grid=(SJ, b),
        in_specs=[seq, seq, pl.BlockSpec((2, 8, 8, SW), lambda j, bi: (0, 0, 0, j)), tab, tab,
                  pl.BlockSpec((1, SU), lambda j, bi: (0, j))],
        out_specs=[seq, pl.BlockSpec((2, SW), lambda j, bi: (0, j)), tab, tab,
                   pl.BlockSpec((1, SU), lambda j, bi: (0, j))],
        out_shape=[jax.ShapeDtypeStruct((b, l, BW), bf16), jax.ShapeDtypeStruct((2, NSTATE), f32),
                   jax.ShapeDtypeStruct((2, SJ, SU, SW), f32), jax.ShapeDtypeStruct((2, SJ, SU, SW), f32),
                   jax.ShapeDtypeStruct((1, BW), f32)],
        scratch_shapes=[pltpu.VMEM((l, SW), f32)] * 4,
        compiler_params=_cp(2))(proj3, dy, t8, bb, cb, dskip)


AHC = 2
AHW = AHC * 128


def _att_mask(n, nb):
    if nb == 1:
        qi = lax.broadcasted_iota(jnp.int32, (ABLK, ABLK), 0)
        kj = lax.broadcasted_iota(jnp.int32, (ABLK, ABLK), 1)
        return kj <= qi
    qi = lax.broadcasted_iota(jnp.int32, (ABLK, 2 * ABLK), 0)
    kj = lax.broadcasted_iota(jnp.int32, (ABLK, 2 * ABLK), 1)
    return (kj >= qi) & (kj <= qi + ABLK) & ((n > 0) | (kj >= ABLK))


def _att_rows(it, nb, dil):
    r, n = it // nb, it % nb
    cur = pl.ds(r + n * (ABLK * dil), ABLK, stride=dil)
    prv = pl.ds(r + jnp.maximum(n - 1, 0) * (ABLK * dil), ABLK, stride=dil)
    return n, cur, prv


def _att_keys(ref, c, cur, prv, nb):
    if nb == 1:
        x = ref[c, cur, :].astype(bf16)
    else:
        x = jnp.concatenate([ref[c, prv, :], ref[c, cur, :]], axis=0).astype(bf16)
    head0 = lax.broadcasted_iota(jnp.int32, x.shape, 1) < HD
    zero = jnp.zeros_like(x)
    return jnp.concatenate([jnp.where(head0, x, zero), jnp.where(head0, zero, x)], axis=0)


def _per_head(nk, a0, a1):
    col = lax.broadcasted_iota(jnp.int32, (ABLK, 2 * nk), 1)
    return jnp.where(col < nk, a0, a1)


def _to_chunks(src_ref, dst):
    for c in range(AHC):
        dst[c] = src_ref[0, :, c * 128:(c + 1) * 128].astype(f32)


def att_fwd(proj3, g_idx, dil):
    b, l, _ = proj3.shape
    nb = l // dil // ABLK
    nhalf = BW // AHW

    def body(q_ref, k_ref, v_ref, o_ref, lse_ref, qf, kf, vf, of):
        hh = pl.program_id(1)
        _to_chunks(q_ref, qf)
        _to_chunks(k_ref, kf)
        _to_chunks(v_ref, vf)
        lane = lax.broadcasted_iota(jnp.int32, (ABLK, 128), 1)

        def step(it, carry):
            n, cur, prv = _att_rows(it, nb, dil)
            valid = _att_mask(n, nb)
            valid = jnp.concatenate([valid, valid], axis=1)
            nk = valid.shape[1] // 2
            lse_all = jnp.zeros((ABLK, 128), f32)
            for c in range(AHC):
                q = (qf[c, cur, :] * ATT_SCALE).astype(bf16)
                k = _att_keys(kf, c, cur, prv, nb)
                v = _att_keys(vf, c, cur, prv, nb)
                s = jnp.where(valid, _dot_nt(q, k), NEG)
                m0 = jnp.max(s[:, :nk], axis=-1, keepdims=True)
                m1 = jnp.max(s[:, nk:], axis=-1, keepdims=True)
                p = jnp.exp(s - _per_head(nk, m0, m1))
                den0 = jnp.sum(p[:, :nk], axis=-1, keepdims=True)
                den1 = jnp.sum(p[:, nk:], axis=-1, keepdims=True)
                of[c, cur, :] = _dot(p.astype(bf16), v) * jnp.where(lane < HD, 1.0 / den0, 1.0 / den1)
                head = hh * (2 * AHC) + 2 * c
                lse_all = (lse_all + jnp.where(lane == head, m0 + jnp.log(den0), 0.0)
                           + jnp.where(lane == head + 1, m1 + jnp.log(den1), 0.0))

            lse_ref[0, 0, cur, :] = lse_all
            return carry

        lax.fori_loop(0, dil * nb, step, 0, unroll=4)
        for c in range(AHC):
            o_ref[0, :, c * 128:(c + 1) * 128] = of[c].astype(bf16)

    col = lambda c: pl.BlockSpec((1, l, AHW), lambda bi, hh: (bi, 0, c * nhalf + hh))
    return pl.pallas_call(
        body, name=f"att_fwd{g_idx}", grid=(b, nhalf),
        in_specs=[col(1 + g_idx), col(4), col(5)],
        out_specs=[pl.BlockSpec((1, l, AHW), lambda bi, hh: (bi, 0, hh)),
                   pl.BlockSpec((1, 1, l, 128), lambda bi, hh: (bi, hh, 0, 0))],
        out_shape=[jax.ShapeDtypeStruct((b, l, BW), bf16), jax.ShapeDtypeStruct((b, nhalf, l, 128), f32)],
        scratch_shapes=[pltpu.VMEM((AHC, l, 128), f32)] * 4,
        compiler_params=_cp(2))(proj3, proj3, proj3)


def att_bwd(proj3, do, lse_tot, delta, g_idx, dil):
    b, l, _ = proj3.shape
    nb = l // dil // ABLK
    nhalf = BW // AHW

    def body(q_ref, k_ref, v_ref, do_ref, l_ref, dl_ref, dq_ref, dk_ref, dv_ref, qf, kf, vf, dof):
        hh = pl.program_id(1)
        _to_chunks(q_ref, qf)
        _to_chunks(k_ref, kf)
        _to_chunks(v_ref, vf)
        _to_chunks(do_ref, dof)
        dk_ref[...] = jnp.zeros_like(dk_ref)
        dv_ref[...] = jnp.zeros_like(dv_ref)
        lane = lax.broadcasted_iota(jnp.int32, (ABLK, 128), 1)

        def step(it, carry):
            n, cur, prv = _att_rows(it, nb, dil)
            valid = _att_mask(n, nb)
            valid = jnp.concatenate([valid, valid], axis=1)
            nk = valid.shape[1] // 2
            lse_b = l_ref[0, cur, :]
            dl_b = dl_ref[0, cur, :]
            head0 = lax.broadcasted_iota(jnp.int32, (nk, 128), 1) < HD
            for c in range(AHC):
                q = (qf[c, cur, :] * ATT_SCALE).astype(bf16)
                dob = dof[c, cur, :].astype(bf16)
                k = _att_keys(kf, c, cur, prv, nb)
                v = _att_keys(vf, c, cur, prv, nb)
                head = hh * (2 * AHC) + 2 * c
                pick = lambda a, h: jnp.sum(jnp.where(lane == h, a, 0.0), axis=-1, keepdims=True)
                lse_h = _per_head(nk, pick(lse_b, head), pick(lse_b, head + 1))
                dl_h = _per_head(nk, pick(dl_b, head), pick(dl_b, head + 1))
                s = _dot_nt(q, k)
                p = jnp.where(valid, jnp.exp(jnp.minimum(s - lse_h, 60.0)), 0.0)
                ds = (p * (_dot_nt(dob, v) - dl_h)).astype(bf16)
                dq_ref[0, c, cur, :] = _dot(ds, k) * ATT_SCALE
                dk2 = _dot_tn(ds, q)
                dv2 = _dot_tn(p.astype(bf16), dob)
                dk = jnp.where(head0, dk2[:nk], dk2[nk:])
                dv = jnp.where(head0, dv2[:nk], dv2[nk:])
                if nb == 1:
                    dk_ref[0, c, cur, :] += dk
                    dv_ref[0, c, cur, :] += dv
                else:
                    dk_ref[0, c, cur, :] += dk[ABLK:]
                    dv_ref[0, c, cur, :] += dv[ABLK:]
                    dk_ref[0, c, prv, :] += dk[:ABLK]
                    dv_ref[0, c, prv, :] += dv[:ABLK]

            return carry

        lax.fori_loop(0, dil * nb, step, 0, unroll=4)

    col = lambda c: pl.BlockSpec((1, l, AHW), lambda bi, hh: (bi, 0, c * nhalf + hh))
    own = pl.BlockSpec((1, l, AHW), lambda bi, hh: (bi, 0, hh))
    own128 = pl.BlockSpec((1, l, 128), lambda bi, hh: (bi, 0, 0))
    chunked = pl.BlockSpec((1, AHC, l, 128), lambda bi, hh: (bi, hh, 0, 0))
    return pl.pallas_call(
        body, name=f"att_bwd{g_idx}", grid=(b, nhalf),
        in_specs=[col(1 + g_idx), col(4), col(5), own, own128, own128],
        out_specs=[chunked] * 3,
        out_shape=[jax.ShapeDtypeStruct((b, BW // 128, l, 128), f32)] * 3,
        scratch_shapes=[pltpu.VMEM((AHC, l, 128), f32)] * 4,
        compiler_params=_cp(2, 56))(proj3, proj3, proj3, do, lse_tot, delta)


CPAD = 32
CTAIL = 16
CR = 128
CSLAB = CR + 40


def _tap_windows(slab, off, mis):
    ntap = (CW - 1 - mis) // 8 + 1
    xb = slab[off + mis:off + mis + CR + 8 * (ntap - 1)]
    for a in range(ntap):
        yield 8 * a + mis, xb[8 * a:8 * a + CR]


def _fill_glu(cv_ref, pad, l):
    pad[0:CPAD, :] = jnp.zeros((CPAD, BW), f32)
    pad[CPAD:CPAD + l, :] = cv_ref[0, :, :BW].astype(f32) * _sigmoid(cv_ref[0, :, BW:].astype(f32))
    pad[CPAD + l:, :] = jnp.zeros((CTAIL, BW), f32)


def conv_fwd(proj3, cw, cb):
    b, l, _ = proj3.shape

    def body(cv_ref, w_ref, b_ref, o_ref, pad):
        _fill_glu(cv_ref, pad, l)
        for lc in range(BW // 128):
            lanes = slice(lc * 128, (lc + 1) * 128)
            wv = w_ref[:, lanes]

            def step(c, carry):
                base = pl.multiple_of(c * CR, CR)
                slab = pad[pl.ds(base, CSLAB), lanes]
                acc = jnp.zeros((CR, 128), f32) + b_ref[:, lanes]
                for mis in range(8):
                    for k, win in _tap_windows(slab, CPAD - (CW - 1), mis):
                        acc = acc + wv[k:k + 1] * win
                o_ref[0, pl.ds(base, CR), lanes] = acc
                return carry

            lax.fori_loop(0, l // CR, step, 0)

    return pl.pallas_call(
        body, name="conv_fwd", grid=(b,),
        in_specs=[pl.BlockSpec((1, l, 2 * BW), lambda i: (i, 0, 3)),
                  pl.BlockSpec((32, BW), lambda i: (0, 0)), pl.BlockSpec((1, BW), lambda i: (0, 0))],
        out_specs=pl.BlockSpec((1, l, BW), lambda i: (i, 0, 0)),
        out_shape=jax.ShapeDtypeStruct((b, l, BW), f32),
        scratch_shapes=[pltpu.VMEM((CPAD + l + CTAIL, BW), f32)], compiler_params=_cp(1))(proj3, cw, cb)


def conv_bwd(proj3, dhc, cw):
    b, l, _ = proj3.shape

    def body(cv_ref, d_ref, w_ref, dcv_ref, dw_ref, db_ref, pad, dpad):
        i = pl.program_id(0)

        @pl.when(i == 0)
        def _():
            dw_ref[...] = jnp.zeros_like(dw_ref)
            db_ref[...] = jnp.zeros_like(db_ref)

        _fill_glu(cv_ref, pad, l)
        dpad[0:l, :] = d_ref[0]
        dpad[l:, :] = jnp.zeros((CPAD + CTAIL, BW), f32)
        db_ref[...] += jnp.sum(d_ref[0], axis=0, keepdims=True)
        for lc in range(BW // 128):
            lanes = slice(lc * 128, (lc + 1) * 128)
            glanes = slice(BW + lc * 128, BW + (lc + 1) * 128)
            wv = w_ref[:, lanes]

            for mis in range(8):
                ntap = (CW - 1 - mis) // 8 + 1

                def dw_step(c, accs, mis=mis, lanes=lanes):
                    base = pl.multiple_of(c * CR, CR)
                    slab = pad[pl.ds(base, CSLAB), lanes]
                    dv = dpad[pl.ds(base, CR), lanes]
                    return tuple(acc + (dv * win).reshape(CR // 8, 8, 128).sum(axis=0) for acc, (_, win)
                                 in zip(accs, _tap_windows(slab, CPAD - (CW - 1), mis)))

                accs = lax.fori_loop(0, l // CR, dw_step, tuple(jnp.zeros((8, 128), f32) for _ in range(ntap)))
                for a in range(ntap):
                    k = 8 * a + mis
                    dw_ref[k:k + 1, lanes] += jnp.sum(accs[a], axis=0, keepdims=True)

            def dh_step(c, carry, lanes=lanes, glanes=glanes, wv=wv):
                base = pl.multiple_of(c * CR, CR)
                slab = dpad[pl.ds(base, CSLAB), lanes]
                acc = jnp.zeros((CR, 128), f32)
                for mis in range(8):
                    for kk, win in _tap_windows(slab, 0, mis):
                        acc = acc + wv[CW - 1 - kk:CW - kk] * win
                rows = pl.ds(base, CR)
                a = cv_ref[0, rows, lanes].astype(f32)
                sg = _sigmoid(cv_ref[0, rows, glanes].astype(f32))
                dcv_ref[0, rows, lanes] = (acc * sg).astype(bf16)
                dcv_ref[0, rows, glanes] = (acc * a * sg * (1.0 - sg)).astype(bf16)
                return carry

            lax.fori_loop(0, l // CR, dh_step, 0)

    return pl.pallas_call(
        body, name="conv_bwd", grid=(b,),
        in_specs=[pl.BlockSpec((1, l, 2 * BW), lambda i: (i, 0, 3)),
                  pl.BlockSpec((1, l, BW), lambda i: (i, 0, 0)),
                  pl.BlockSpec((32, BW), lambda i: (0, 0))],
        out_specs=[pl.BlockSpec((1, l, 2 * BW), lambda i: (i, 0, 0)),
                   pl.BlockSpec((32, BW), lambda i: (0, 0)), pl.BlockSpec((1, BW), lambda i: (0, 0))],
        out_shape=[jax.ShapeDtypeStruct((b, l, 2 * BW), bf16), jax.ShapeDtypeStruct((32, BW), f32),
                   jax.ShapeDtypeStruct((1, BW), f32)],
        scratch_shapes=[pltpu.VMEM((CPAD + l + CTAIL, BW), f32), pltpu.VMEM((l + CPAD + CTAIL, BW), f32)],
        compiler_params=_cp(1))(proj3, dhc, cw)


def _head_expand():
    r = lax.broadcasted_iota(jnp.int32, (128, BW), 0)
    c = lax.broadcasted_iota(jnp.int32, (128, BW), 1) // HD
    return (r == c).astype(f32)


def _head_reduce():
    r = lax.broadcasted_iota(jnp.int32, (BW, 128), 0) // HD
    c = lax.broadcasted_iota(jnp.int32, (BW, 128), 1)
    return (r == c).astype(f32)


def _merge_common(ys_ref, o_refs, l_refs, hc_ref, g_refs, bg_ref, lng_ref, lnb_ref, wglu_ref, watt_ref, wpw_ref):
    r = {}
    ysv = ys_ref[...]
    r["ys"] = ysv
    r["ysin"] = _gelu(ysv).astype(bf16)
    z = _dot(r["ysin"], wglu_ref[...])
    r["z1"], r["sg2"] = z[:, :D], _sigmoid(z[:, D:])
    r["y_s"] = r["z1"] * r["sg2"]
    ls = [lr_[0, 0] + lr_[0, 1] for lr_ in l_refs]
    mx = jnp.maximum(jnp.maximum(ls[0], ls[1]), ls[2])
    es = [jnp.exp(v - mx) for v in ls]
    tot = es[0] + es[1] + es[2]
    r["lse_tot"] = mx + jnp.log(tot)
    e_mat = _head_expand()
    o = jnp.zeros(ysv.shape, f32)
    for e, o_ref in zip(es, o_refs):
        o = o + _dot_hi(e / tot, e_mat) * o_ref[...].astype(f32)
    r["o"] = o
    r["ob"] = o.astype(bf16)
    r["y_a"] = _dot(r["ob"], watt_ref[...])
    hc = hc_ref[...]
    mu = jnp.mean(hc, axis=-1, keepdims=True)
    xc = hc - mu
    rstd = lax.rsqrt(jnp.mean(xc * xc, axis=-1, keepdims=True) + EPS)
    r["xh"], r["rstd"] = xc * rstd, rstd
    hn = r["xh"] * lng_ref[...] + lnb_ref[...]
    r["hn"] = hn
    r["sgn"] = _sigmoid(hn)
    r["hs"] = (hn * r["sgn"]).astype(bf16)
    r["y_c"] = _dot(r["hs"], wpw_ref[...])
    r["gates"] = [_sigmoid(g_refs[k][...].astype(f32) + bg_ref[:, k * D:(k + 1) * D]) for k in range(3)]
    r["merged"] = r["gates"][0] * r["y_s"] + r["gates"][1] * r["y_a"] + r["gates"][2] * r["y_c"]
    return r


TBM = 256


def _merge_in_specs(tok, tb, lses):
    w = lambda shape: pl.BlockSpec(shape, lambda i: (0, 0))
    nbl = lses[0].shape[2] // tb
    return ([pl.BlockSpec((tb, D), tok), pl.BlockSpec((tb, BW), tok)]
            + [pl.BlockSpec((tb, BW), tok)] * 3
            + [pl.BlockSpec((1, 2, tb, 128), lambda i: (i // nbl, 0, i % nbl, 0))] * 3
            + [pl.BlockSpec((tb, BW), tok)]
            + [pl.BlockSpec((tb, D), lambda i, k=k: (i, 4 + k)) for k in range(3)]
            + [w((1, 3 * D)), w((1, BW)), w((1, BW)), w((BW, 2 * D)), w((BW, D)), w((BW, D)), w((D, D))])


def merge_fwd(x, ys, os_, lses, hc, proj, bg, lng, lnb, wglu, watt, wpw, wout):
    n = x.shape[0]

    def body(x_ref, ys_ref, o1, o2, o3, l1, l2, l3, hc_ref, g0, g1, g2, bg_ref, lng_ref, lnb_ref,
             wglu_ref, watt_ref, wpw_ref, wout_ref, x1_ref):
        r = _merge_common(ys_ref, (o1, o2, o3), (l1, l2, l3), hc_ref, (g0, g1, g2), bg_ref, lng_ref, lnb_ref,
                          wglu_ref, watt_ref, wpw_ref)
        x1_ref[...] = x_ref[...] + _dot(r["merged"].astype(bf16), wout_ref[...])

    tok = lambda i: (i, 0)
    return pl.pallas_call(
        body, name="merge_fwd", grid=(n // TB,), in_specs=_merge_in_specs(tok, TB, lses),
        out_specs=pl.BlockSpec((TB, D), tok), out_shape=jax.ShapeDtypeStruct((n, D), f32),
        compiler_params=_cp(1, 56))(x, ys, *os_, *lses, hc, proj, proj, proj, bg, lng, lnb, wglu, watt, wpw, wout)


def merge_bwd(dx1, ys, os_, lses, hc, proj, bg, lng, lnb, wglu, watt, wpw, wout):
    n = dx1.shape[0]

    def body(dx_ref, ys_ref, o1, o2, o3, l1, l2, l3, hc_ref, g0, g1, g2, bg_ref, lng_ref, lnb_ref,
             wglu_ref, watt_ref, wpw_ref, wout_ref,
             dys_ref, do_ref, delta_ref, ltot_ref, dhc_ref, dgate_ref, ysin_ref, dz_ref, ob_ref, dya_ref,
             hs_ref, dyc_ref, mg_ref, dbg_ref, dlng_ref, dlnb_ref):
        i = pl.program_id(0)

        @pl.when(i == 0)
        def _():
            dbg_ref[...] = jnp.zeros_like(dbg_ref)
            dlng_ref[...] = jnp.zeros_like(dlng_ref)
            dlnb_ref[...] = jnp.zeros_like(dlnb_ref)

        r = _merge_common(ys_ref, (o1, o2, o3), (l1, l2, l3), hc_ref, (g0, g1, g2), bg_ref, lng_ref, lnb_ref,
                          wglu_ref, watt_ref, wpw_ref)
        mg_ref[...] = r["merged"].astype(bf16)
        ysin_ref[...] = r["ysin"]
        ob_ref[...] = r["ob"]
        hs_ref[...] = r["hs"]
        ltot_ref[...] = r["lse_tot"]
        dm = _dot_nt(dx_ref[...].astype(bf16), wout_ref[...])
        ys3 = (r["y_s"], r["y_a"], r["y_c"])
        for k in range(3):
            gk = r["gates"][k]
            dgr = dm * ys3[k] * gk * (1.0 - gk)
            dgate_ref[:, k * D:(k + 1) * D] = dgr.astype(bf16)
            dbg_ref[:, k * D:(k + 1) * D] += jnp.sum(dgr, axis=0, keepdims=True)
        dy_s = dm * r["gates"][0]
        sg2 = r["sg2"]
        dz = jnp.concatenate([dy_s * sg2, dy_s * r["z1"] * sg2 * (1.0 - sg2)], axis=1).astype(bf16)
        dz_ref[...] = dz
        dys_ref[...] = _dot_nt(dz, wglu_ref[...]) * _gelu_grad(r["ys"])
        dya = (dm * r["gates"][1]).astype(bf16)
        dya_ref[...] = dya
        do = _dot_nt(dya, watt_ref[...])
        do_ref[...] = do.astype(bf16)
        delta_ref[...] = _dot_hi(do * r["o"], _head_reduce())
        dyc = (dm * r["gates"][2]).astype(bf16)
        dyc_ref[...] = dyc
        sgn, hn = r["sgn"], r["hn"]
        dhn = _dot_nt(dyc, wpw_ref[...]) * sgn * (1.0 + hn * (1.0 - sgn))
        dlng_ref[...] += jnp.sum(dhn * r["xh"], axis=0, keepdims=True)
        dlnb_ref[...] += jnp.sum(dhn, axis=0, keepdims=True)
        dxh = dhn * lng_ref[...]
        xh = r["xh"]
        dhc_ref[...] = r["rstd"] * (dxh - jnp.mean(dxh, axis=-1, keepdims=True)
                                    - xh * jnp.mean(dxh * xh, axis=-1, keepdims=True))

    tok = lambda i: (i, 0)
    fix = lambda i: (0, 0)
    outs = [("dys", BW, f32), ("do", BW, bf16), ("delta", 128, f32), ("lse_tot", 128, f32), ("dhc", BW, f32),
            ("dgate", 3 * D, bf16), ("ysin", BW, bf16), ("dz", 2 * D, bf16), ("ob", BW, bf16), ("dya", D, bf16),
            ("hs", BW, bf16), ("dyc", D, bf16), ("merged", D, bf16)]
    small = [("dbg", 3 * D), ("dlng", BW), ("dlnb", BW)]
    res = pl.pallas_call(
        body, name="merge_bwd", grid=(n // TBM,), in_specs=_merge_in_specs(tok, TBM, lses),
        out_specs=[pl.BlockSpec((TBM, w), tok) for _, w, _ in outs] + [pl.BlockSpec((1, w), fix) for _, w in small],
        out_shape=[jax.ShapeDtypeStruct((n, w), dt) for _, w, dt in outs]
        + [jax.ShapeDtypeStruct((1, w), f32) for _, w in small],
        compiler_params=_cp(1, 56))(dx1, ys, *os_, *lses, hc, proj, proj, proj, bg, lng, lnb, wglu, watt, wpw, wout)
    return dict(zip([k for k, _, _ in outs] + [k for k, _ in small], res))


def assemble_dproj(du, dqs, dks, dvs, dcv, dgate):
    b, l, _ = du.shape
    nck = BW // 128

    def body(du_ref, q1, q2, q3, k1, k2, k3, v1, v2, v3, cv_ref, g_ref, o_ref):
        o_ref[0, :, 0:BW] = du_ref[0]
        for c in range(nck):
            for j, qr in enumerate((q1, q2, q3)):
                o_ref[0, :, (1 + j) * BW + c * 128:(1 + j) * BW + (c + 1) * 128] = qr[0, c].astype(bf16)
            o_ref[0, :, 4 * BW + c * 128:4 * BW + (c + 1) * 128] = (k1[0, c] + k2[0, c] + k3[0, c]).astype(bf16)
            o_ref[0, :, 5 * BW + c * 128:5 * BW + (c + 1) * 128] = (v1[0, c] + v2[0, c] + v3[0, c]).astype(bf16)
        o_ref[0, :, 6 * BW:8 * BW] = cv_ref[0]
        o_ref[0, :, 8 * BW:] = g_ref[0]

    t = lambda w: pl.BlockSpec((1, TB, w), lambda bi, i: (bi, i, 0))
    ck = pl.BlockSpec((1, nck, TB, 128), lambda bi, i: (bi, 0, i, 0))
    return pl.pallas_call(
        body, name="assemble_dproj", grid=(b, l // TB),
        in_specs=[t(BW)] + [ck] * 9 + [t(2 * BW), t(3 * D)], out_specs=t(INC),
        out_shape=jax.ShapeDtypeStruct((b, l, INC), bf16), compiler_params=_cp(2))(du, *dqs, *dks, *dvs, dcv, dgate)


def _me():
    return lax.axis_index("x"), lax.axis_index("y"), lax.axis_index("c")


def _peers():
    x, y, c = _me()
    return [(x, y, 1 - c), (1 - x, y, c), (1 - x, y, 1 - c), (x, 1 - y, c), (x, 1 - y, 1 - c),
            (1 - x, 1 - y, c), (1 - x, 1 - y, 1 - c)]


def _rank(p):
    return 4 * p[0] + 2 * p[1] + p[2]


def allgather(arrs, name):
    na = len(arrs)
    units = [(a, j) for a in range(na) for j in range(arrs[a].shape[0])]
    nu = len(units)

    def body(*refs):
        ins, outs = refs[:na], refs[na:2 * na]
        send, recv, loc = refs[2 * na:]
        me = _rank(_me())
        local, remote = [], []
        for u, (a, j) in enumerate(units):
            own = pltpu.make_async_copy(ins[a].at[j], outs[a].at[j, me], loc.at[u])
            own.start()
            local.append(own)
        for u, (a, j) in enumerate(units):
            for k, p in enumerate(_peers()):
                cp = pltpu.make_async_remote_copy(src_ref=ins[a].at[j], dst_ref=outs[a].at[j, me],
                                                  send_sem=send.at[u, k], recv_sem=recv.at[u, k],
                                                  device_id=p, device_id_type=MESH)
                cp.start()
                remote.append(cp)
        for cp in local:
            cp.wait()
        for cp in remote:
            cp.wait()

    return pl.pallas_call(
        body, name=name, in_specs=[ANY] * na, out_specs=[ANY] * na,
        out_shape=[jax.ShapeDtypeStruct((a.shape[0], NDEV) + a.shape[1:], a.dtype) for a in arrs],
        scratch_shapes=[pltpu.SemaphoreType.DMA((nu, NDEV - 1)), pltpu.SemaphoreType.DMA((nu, NDEV - 1)),
                        pltpu.SemaphoreType.DMA((nu,))])(*arrs)


HBM = pl.BlockSpec(memory_space=pltpu.HBM)
SEM = pl.BlockSpec(memory_space=pltpu.SEMAPHORE)
_EFFECT = pltpu.SideEffectType.DATAFLOW_SIDE_EFFECTING


def _push_copies(srcs, lands, send, recv, scatter):
    me = _rank(_me())
    out = []
    for i in range(len(srcs)):
        for k, p in enumerate(_peers()):
            src = srcs[i].at[_rank(p)] if scatter else srcs[i]
            dst = lands[i].at[k] if scatter else lands[i].at[me]
            j = i * (NDEV - 1) + k
            out.append(pltpu.make_async_remote_copy(src_ref=src, dst_ref=dst, send_sem=send.at[j],
                                                    recv_sem=recv.at[j], device_id=p, device_id_type=MESH))
    return out


def push_start(srcs, lands, scatter, name, token):
    n = len(srcs)
    token = jnp.zeros((8, 128), f32) if token is None else token

    def body(*refs):
        for cp in _push_copies(refs[:n], refs[n:2 * n], refs[2 * n + 1], refs[2 * n + 2], scatter):
            cp.start()
        refs[-1][...] = refs[2 * n][...]

    sems = pltpu.SemaphoreType.DMA((n * (NDEV - 1),))
    vmem = pl.BlockSpec(memory_space=pltpu.VMEM)
    res = pl.pallas_call(
        body, name=name, in_specs=[HBM] * (2 * n) + [vmem], out_specs=[SEM, SEM] + [HBM] * (2 * n) + [vmem],
        out_shape=[sems, sems] + [pltpu.HBM(a.shape, a.dtype) for a in list(srcs) + list(lands)]
        + [jax.ShapeDtypeStruct((8, 128), f32)],
        input_output_aliases={i: 2 + i for i in range(2 * n)},
        compiler_params=pltpu.CompilerParams(has_side_effects=_EFFECT),
    )(*[pltpu.with_memory_space_constraint(a, pltpu.HBM) for a in list(srcs) + list(lands)], token)
    return res[0], res[1], res[2:2 + n], res[2 + n:2 + 2 * n], res[-1]


def push_wait(send, recv, srcs, lands, after, scatter, name):
    n = len(srcs)

    def body(*refs):
        for cp in _push_copies(refs[:n], refs[n:2 * n], refs[2 * n], refs[2 * n + 1], scatter):
            cp.wait_send()
            cp.wait_recv()

    res = pl.pallas_call(
        body, name=name, in_specs=[HBM] * (2 * n) + [SEM, SEM, ANY], out_specs=[HBM] * (2 * n),
        out_shape=[pltpu.HBM(a.shape, a.dtype) for a in list(srcs) + list(lands)],
        input_output_aliases={i: i for i in range(2 * n)},
        compiler_params=pltpu.CompilerParams(has_side_effects=_EFFECT),
    )(*srcs, *lands, send, recv, after)
    return res[:n], res[n:]


_C1 = 1.0 / (1.0 - ADAM_B1 ** ADAM_STEP)
_C2 = 1.0 / (1.0 - ADAM_B2 ** ADAM_STEP)


def _adamw(w, g, m, v):
    m = ADAM_B1 * m + (1.0 - ADAM_B1) * g
    v = ADAM_B2 * v + (1.0 - ADAM_B2) * (g * g)
    delta = -ADAM_LR * ((m * _C1) / (jnp.sqrt(v * _C2) + ADAM_EPS) + ADAM_WD * w)
    return delta, m, v


def adam_big(lands, owns, w, m, v, name):
    _, k, n = lands[0].shape
    tk = k
    while tk * n * 2 * NDEV > 2 * 1024 * 1024 and tk % 16 == 0:
        tk //= 2

    def body(*refs):
        l_refs, o_refs = refs[:DEPTH], refs[DEPTH:2 * DEPTH]
        w_ref, m_ref, v_ref, g_ref, d_ref, nm_ref, nv_ref = refs[2 * DEPTH:]
        for l in range(DEPTH):
            g = o_refs[l][...].astype(f32)
            for s in range(NDEV - 1):
                g = g + l_refs[l][s].astype(f32)
            d, nm, nv = _adamw(w_ref[l], g, m_ref[l], v_ref[l])
            g_ref[l], d_ref[l], nm_ref[l], nv_ref[l] = g, d, nm, nv

    blk = pl.BlockSpec((DEPTH, tk, n), lambda i: (0, i, 0))
    return pl.pallas_call(
        body, name=name, grid=(k // tk,),
        in_specs=[pl.BlockSpec((NDEV - 1, tk, n), lambda i: (0, i, 0))] * DEPTH
        + [pl.BlockSpec((tk, n), lambda i: (i, 0))] * DEPTH + [blk, blk, blk],
        out_specs=[blk] * 4, out_shape=[jax.ShapeDtypeStruct(w.shape, f32)] * 4,
        compiler_params=_cp(1))(*lands, *owns, w, m, v)


def adam_small(gath, w, m, v):
    r = w.shape[0]
    tr = 512

    def body(g_ref, w_ref, m_ref, v_ref, go_ref, d_ref, nm_ref, nv_ref):
        g = g_ref[0]
        for s in range(1, NDEV):
            g = g + g_ref[s]
        d, nm, nv = _adamw(w_ref[...], g, m_ref[...], v_ref[...])
        go_ref[...], d_ref[...], nm_ref[...], nv_ref[...] = g, d, nm, nv

    blk = pl.BlockSpec((tr, 128), lambda i: (i, 0))
    return pl.pallas_call(
        body, name="adam_small", grid=(r // tr,),
        in_specs=[pl.BlockSpec((NDEV, tr, 128), lambda i: (0, i, 0)), blk, blk, blk],
        out_specs=[blk] * 4, out_shape=[jax.ShapeDtypeStruct((r, 128), f32)] * 4,
        compiler_params=_cp(1))(gath, w, m, v)


SMALL = ["norm1_g", "b_gate", "ssm_lambda_re", "ssm_lambda_im", "ssm_log_dt", "ssm_b_re", "ssm_b_im",
         "ssm_c_re", "ssm_c_im", "ssm_d", "conv_w", "conv_b", "conv_ln_g", "conv_ln_b", "norm2_g", "final_g"]
BIG = ["w_in", "w_ssm_glu", "w_att_up", "w_conv_pw2", "w_out", "w_ffn_in", "w_ffn_out"]
ORDER = ["norm1_g", "w_in", "b_gate", "ssm_lambda_re", "ssm_lambda_im", "ssm_log_dt", "ssm_b_re", "ssm_b_im",
         "ssm_c_re", "ssm_c_im", "ssm_d", "w_ssm_glu", "w_att_up", "conv_w", "conv_b", "conv_ln_g", "conv_ln_b",
         "w_conv_pw2", "w_out", "norm2_g", "w_ffn_in", "w_ffn_out", "final_g"]
PACK_ROWS = 2560


def _pack(arrs):
    flat = jnp.concatenate([a.reshape(-1).astype(f32) for a in arrs])
    return jnp.pad(flat, (0, PACK_ROWS * 128 - flat.shape[0])).reshape(PACK_ROWS, 128)


def _unpack(pack, shapes):
    flat = pack.reshape(-1)
    out, off = [], 0
    for s in shapes:
        sz = math.prod(s)
        out.append(flat[off:off + sz].reshape(s))
        off += sz
    return out


def _bt(b):
    return b.transpose(2, 0, 1).reshape(GH, NSTATE)


def _bt_inv(bt):
    return bt.reshape(GH, NG, NS).transpose(1, 2, 0)


def _ct(c):
    return c.transpose(1, 0, 2).reshape(GH, NSTATE)


def _ct_inv(ct):
    return ct.reshape(GH, NG, NS).transpose(1, 0, 2)


def local_step(x, loss_target, P, weights, on_grads, start_token=None):
    bsz, seq, _ = x.shape
    n = bsz * seq

    def natural(g3):
        return g3.transpose(1, 0, 2).reshape(g3.shape[1], NDEV * g3.shape[2])

    tokens = [] if start_token is None else [start_token]

    def after_pushes(a):
        while tokens:
            a = a + tokens.pop()[0:1, 0:1]
        return a

    def pushed(tok):
        if tok is not None:
            tokens.append(tok)

    xs = x.reshape(n, D)
    saved = []
    conv_w_pad = None
    for l in range(DEPTH):
        S = {"x": xs}
        h1 = rms_fwd(xs, after_pushes(P["norm1_g"][l][None]))
        G = dict(weights(l, "in", h1))
        if conv_w_pad is None:
            conv_w_full = G["conv_w"].transpose(1, 2, 0, 3).reshape(DEPTH, CW, BW)
            conv_w_pad = jnp.pad(conv_w_full, ((0, 0), (0, 1), (0, 0)))
        w_in4 = G["w_in"][None]
        proj = inproj(h1, w_in4, 0)
        proj3 = proj.reshape(bsz, seq, INC)
        lr = P["ssm_lambda_re"][l].reshape(1, NSTATE)
        li = P["ssm_lambda_im"][l].reshape(1, NSTATE)
        ld = jnp.repeat(P["ssm_log_dt"][l], NS).reshape(1, NSTATE)
        btr, bti = _bt(P["ssm_b_re"][l]), _bt(P["ssm_b_im"][l])
        t8, bb, cb = s5_params(lr, li, ld, btr, bti, _ct(P["ssm_c_re"][l]), _ct(P["ssm_c_im"][l]))
        dskip = P["ssm_d"][l][None]
        ys = s5_fwd(proj3, t8, bb, cb, dskip)
        att = [att_fwd(proj3, gi, dil) for gi, (_, dil) in enumerate(PATTERNS)]
        hc = conv_fwd(proj3, conv_w_pad[l], P["conv_b"][l][None])
        G.update(weights(l, "mix", hc))
        wts = dict(wglu=natural(G["w_ssm_glu"]), watt=natural(G["w_att_up"]),
                   wpw=natural(G["w_conv_pw2"]), wout=G["w_out"].reshape(D, D))
        mi = dict(ys=ys.reshape(n, BW), os_=[a[0].reshape(n, BW) for a in att],
                  lses=[a[1] for a in att], hc=hc.reshape(n, BW),
                  proj=proj, bg=P["b_gate"][l][None], lng=P["conv_ln_g"][l][None], lnb=P["conv_ln_b"][l][None],
                  **wts)
        x1 = merge_fwd(xs, **mi)
        G.update(weights(l, "ffn", x1))
        w_ffn = (G["w_ffn_in"][None], G["w_ffn_out"][None])
        x2 = ffn_fwd(x1, P["norm2_g"][l][None], *w_ffn, 0)
        S.update(h1=h1, proj=proj, proj3=proj3, tabs=(t8, bb, cb), mi=mi, x1=x1, w_in4=w_in4, w_ffn=w_ffn,
                 sp=(lr, li, ld, btr, bti), dskip=dskip)
        saved.append(S)
        xs = x2

    loss8, dx, dfinal = loss_head(xs, P["final_g"][None], loss_target.reshape(n, D))

    small_g = {k: [None] * DEPTH for k in SMALL if k != "final_g"}
    tokblk = lambda w: pl.BlockSpec((1024, w), lambda s, i: (i, 0))
    colblk = lambda w: pl.BlockSpec((1024, w), lambda s, i: (i, s))
    sh3blk = lambda w: pl.BlockSpec((1, 1024, w), lambda s, i: (s, i, 0))
    for l in reversed(range(DEPTH)):
        S = saved[l]
        g2 = P["norm2_g"][l][None]
        dx1, h2, dz1, dz2, a4, dg2 = ffn_bwd(S["x1"], after_pushes(g2), *S["w_ffn"], 0, dx)
        small_g["norm2_g"][l] = dg2
        dwa = mm_tn(h2, dz1, tokblk(D), sh3blk(NSH_FF), 4, D, NSH_FF, n, "dw_ffn_in_a")
        dwb = mm_tn(h2, dz2, tokblk(D), sh3blk(NSH_FF), 4, D, NSH_FF, n, "dw_ffn_in_b")
        pushed(on_grads(l, "ffn", dict(
            w_ffn_in=jnp.concatenate([dwa, dwb], axis=0),
            w_ffn_out=mm_tn(a4, dx, sh3blk(NSH_FF), tokblk(D), 4, NSH_FF, D, n,
                            "dw_ffn_out").reshape(NDEV, NSH_FF // 2, D))))
        mb = merge_bwd(dx1, **dict(S["mi"], lng=after_pushes(S["mi"]["lng"])))
        small_g["b_gate"][l], small_g["conv_ln_g"][l], small_g["conv_ln_b"][l] = mb["dbg"], mb["dlng"], mb["dlnb"]
        pushed(on_grads(l, "mix", dict(
            w_ssm_glu=mm_tn(mb["ysin"], mb["dz"], tokblk(BW), colblk(256), NDEV, BW, 256, n, "dw_glu"),
            w_att_up=mm_tn(mb["ob"], mb["dya"], tokblk(BW), colblk(128), NDEV, BW, 128, n, "dw_att"),
            w_conv_pw2=mm_tn(mb["hs"], mb["dyc"], tokblk(BW), colblk(128), NDEV, BW, 128, n, "dw_pw2"),
            w_out=mm_tn(mb["merged"], dx1, tokblk(D), tokblk(D), 1, D, D, n, "dw_out").reshape(NDEV, D // NDEV, D))))
        dcv, dcw, dcb = conv_bwd(S["proj3"], mb["dhc"].reshape(bsz, seq, BW), after_pushes(conv_w_pad[l]))
        small_g["conv_w"][l] = dcw[:CW].reshape(CW, NDEV, BW // NDEV).transpose(1, 0, 2)
        small_g["conv_b"][l] = dcb
        ab = [att_bwd(S["proj3"], mb["do"].reshape(bsz, seq, BW), mb["lse_tot"].reshape(bsz, seq, 128),
                      mb["delta"].reshape(bsz, seq, 128), gi, dil) for gi, (_, dil) in enumerate(PATTERNS)]
        t8, bb, cb = S["tabs"]
        du, d_a, d_bb, d_cb, d_d = s5_bwd(S["proj3"], mb["dys"].reshape(bsz, seq, BW), t8, bb, cb, S["dskip"])
        lr, li, ld, btr, bti = S["sp"]
        dlr, dli, dld, dbt, dct = s5_params_bwd(lr, li, ld, btr, bti, d_a, d_bb, d_cb)
        small_g["ssm_lambda_re"][l], small_g["ssm_lambda_im"][l] = dlr.reshape(NG, NS), dli.reshape(NG, NS)
        small_g["ssm_log_dt"][l] = dld[0, :NG]
        small_g["ssm_b_re"][l], small_g["ssm_b_im"][l] = _bt_inv(dbt[0]), _bt_inv(dbt[1])
        small_g["ssm_c_re"][l], small_g["ssm_c_im"][l] = _ct_inv(dct[0]), _ct_inv(dct[1])
        small_g["ssm_d"][l] = d_d
        dproj = assemble_dproj(du, [a[0] for a in ab], [a[1] for a in ab], [a[2] for a in ab],
                               dcv, mb["dgate"].reshape(bsz, seq, 3 * D)).reshape(n, INC)
        pushed(on_grads(l, "in", dict(w_in=mm_tn(S["h1"], dproj, tokblk(D), colblk(NSH_IN), NDEV, D, NSH_IN, n,
                                                 "dw_in"))))
        if l == 0:
            pushed(on_grads(l, "small", dict(small_g=small_g, loss8=loss8, dfinal=dfinal)))
        dx, dg1 = inproj_bwd(dproj, S["w_in4"], 0, S["x"], after_pushes(P["norm1_g"][l][None]), dx1)
        small_g["norm1_g"][l] = dg1
    return loss8, dx, dfinal, small_g


def kernel(x, norm1_g, w_in, b_gate, ssm_lambda_re, ssm_lambda_im, ssm_log_dt, ssm_b_re, ssm_b_im, ssm_c_re, ssm_c_im, ssm_d, w_ssm_glu, w_att_up, conv_w, conv_b, conv_ln_g, conv_ln_b, w_conv_pw2, w_out, norm2_g, w_ffn_in, w_ffn_out, final_g, loss_target, m_norm1_g, m_w_in, m_b_gate, m_ssm_lambda_re, m_ssm_lambda_im, m_ssm_log_dt, m_ssm_b_re, m_ssm_b_im, m_ssm_c_re, m_ssm_c_im, m_ssm_d, m_w_ssm_glu, m_w_att_up, m_conv_w, m_conv_b, m_conv_ln_g, m_conv_ln_b, m_w_conv_pw2, m_w_out, m_norm2_g, m_w_ffn_in, m_w_ffn_out, m_final_g, v_norm1_g, v_w_in, v_b_gate, v_ssm_lambda_re, v_ssm_lambda_im, v_ssm_log_dt, v_ssm_b_re, v_ssm_b_im, v_ssm_c_re, v_ssm_c_im, v_ssm_d, v_w_ssm_glu, v_w_att_up, v_conv_w, v_conv_b, v_conv_ln_g, v_conv_ln_b, v_w_conv_pw2, v_w_out, v_norm2_g, v_w_ffn_in, v_w_ffn_out, v_final_g):
    args = dict(locals())
    W = {k: args[k] for k in ORDER}
    M = {k: args["m_" + k] for k in ORDER}
    V = {k: args["v_" + k] for k in ORDER}
    bsz, seq, _ = x.shape
    n = bsz * seq
    me = 4 * lax.axis_index("x") + 2 * lax.axis_index("y") + lax.axis_index("c")

    groups = {"in": ["w_in"], "mix": ["w_ssm_glu", "w_att_up", "w_conv_pw2", "w_out"], "ffn": ["w_ffn_in", "w_ffn_out"]}
    wb = {k: W[k].astype(bf16) for k in BIG}

    def landing(shard):
        return lax.dynamic_update_index_in_dim(lax.empty((NDEV,) + shard.shape, shard.dtype), shard, me, 0)

    plan = [("gather_a", [("w_in", 0), ("conv_w", None)]),
            ("gather_b", [(k, 0) for k in groups["mix"] + groups["ffn"]]),
            ("gather_c", [(k, 1) for k in BIG])]
    pending, token = {}, None
    for name, items in plan:
        shards = [conv_w if l is None else wb[k][l] for k, l in items]
        send, recv, s_thru, l_thru, token = push_start(shards, [landing(s) for s in shards], False, name, token)
        pending[name] = (send, recv, s_thru, l_thru, items)
    gathered = {}

    def weights(l, group, after):
        name = "gather_c" if l == 1 else ("gather_a" if group == "in" else "gather_b")
        if name in pending:
            send, recv, s_thru, l_thru, items = pending.pop(name)
            for item, arr in zip(items, push_wait(send, recv, s_thru, l_thru, after, False, name + "_wait")[1]):
                gathered[item] = arr
        res = {k: gathered[(k, l)] for k in groups[group]}
        if group == "in":
            res["conv_w"] = gathered[("conv_w", None)]
        return res

    big_g = {k: [None] * DEPTH for k in BIG}
    flights = []

    def start_exchange(items, name):
        parts = [big_g[k][l] for k, l in items]
        lands = [lax.empty((NDEV - 1,) + p.shape[1:], p.dtype) for p in parts]
        send, recv, s_thru, l_thru, tok = push_start(parts, lands, True, name, None)
        flights.append((send, recv, s_thru, l_thru, items, name))
        return tok

    names = [k for k in SMALL if k != "final_g"]
    shapes = [(DEPTH, NDEV, CW, BW // NDEV) if k == "conv_w" else W[k].shape for k in names] + [(D,), (1,)]
    small_flight = []

    def start_small(small_g, loss8, dfinal):
        sg_ = dict(small_g, norm1_g=[jnp.zeros((1, D), f32), small_g["norm1_g"][1]])
        gpack = _pack([jnp.stack([g.reshape(shapes[i][1:]) for g in sg_[k]]) for i, k in enumerate(names)]
                      + [dfinal, loss8[0, :1]])
        send, recv, s_thru, l_thru, tok = push_start([gpack], [landing(gpack)], False, "gather_small", None)
        small_flight.append((send, recv, s_thru, l_thru))
        return tok

    def on_grads(l, group, grads):
        if group == "small":
            return start_small(**grads)
        for k, g in grads.items():
            big_g[k][l] = g
        if l == 1 and group == "in":
            return start_exchange([(k, 1) for k in BIG], "exchange_l1")
        if l == 0:
            return start_exchange([(k, 0) for k in groups[group]], "exchange_l0_" + group)
        return None

    loss8, dx, dfinal, small_g = local_step(x, loss_target, W, weights, on_grads, token)

    landed, own = {}, {}
    for send, recv, s_thru, l_thru, items, name in flights:
        srcs, lands = push_wait(send, recv, s_thru, l_thru, dx, True, name + "_wait")
        for item, src, land in zip(items, srcs, lands):
            landed[item] = land
            own[item] = lax.dynamic_index_in_dim(src, me, 0, keepdims=False)
    out = {}
    for k in BIG:
        items = [(k, l) for l in range(DEPTH)]
        out[k] = adam_big([landed[i] for i in items], [own[i] for i in items], W[k], M[k], V[k], "adam_" + k)

    def wpack(src):
        parts = [jnp.broadcast_to(src[k][:, None], shapes[i]) if k == "conv_w" else src[k] for i, k in enumerate(names)]
        return _pack(parts + [src["final_g"], jnp.ones((1,), f32)])

    send, recv, s_thru, l_thru = small_flight[0]
    gall = push_wait(send, recv, s_thru, l_thru, dx, False, "gather_small_wait")[1][0]
    (late,) = allgather([small_g["norm1_g"][0].reshape(1, D // 128, 128)], "allgather_late")
    gall = lax.dynamic_update_slice(gall, late[0], (0, 0, 0))
    sg, sd, sm, sv = [_unpack(p, shapes) for p in adam_small(gall, wpack(W), wpack(M), wpack(V))]
    for i, k in enumerate(names + ["final_g"]):
        vals = [t[i] for t in (sg, sd, sm, sv)]
        if k == "conv_w":
            vals = [lax.dynamic_index_in_dim(t, me, axis=1, keepdims=False) for t in vals]
        out[k] = vals
    loss = sg[-1].reshape(())

    res = [loss, dx.reshape(bsz, seq, D)]
    for j in range(4):
        res += [out[k][j] for k in ORDER]
    return tuple(res)
```

```python
import functools
import math

import jax
import jax.numpy as jnp
from jax import lax
from jax.experimental import pallas as pl
from jax.experimental.pallas import tpu as pltpu

f32 = jnp.float32
bf16 = jnp.bfloat16

D = 1024
DEPTH = 2
EPS = 1e-6
BW = 512
NG = 32
GH = 16
NS = 64
NSTATE = NG * NS
HD = 64
NH = 8
PATTERNS = ((128, 1), (512, 4), (2048, 16))
ABLK = 128
ATT_SCALE = HD ** -0.5
CW = 31
DFF = 2816
INC = 7168
NDEV = 8
NSH_IN = INC // NDEV
NSH_FF = 2 * DFF // NDEV
ADAM_LR, ADAM_B1, ADAM_B2, ADAM_EPS, ADAM_WD, ADAM_STEP = 0.001, 0.9, 0.999, 1e-08, 0.01, 10

TB = 512
SJ = 4
SW = NSTATE // SJ
SU = BW // SJ
NEG = -1e30
MESH = pl.DeviceIdType.MESH
ANY = pl.BlockSpec(memory_space=pl.ANY)


def _cp(n_axes, vmem_mb=48):
    return pltpu.CompilerParams(dimension_semantics=("arbitrary",) * n_axes,
                                vmem_limit_bytes=vmem_mb * 1024 * 1024)


def _dot(a, b):
    return jnp.dot(a, b, preferred_element_type=f32)


def _dot_nt(a, b):
    return lax.dot_general(a, b, (((1,), (1,)), ((), ())), preferred_element_type=f32)


def _dot_tn(a, b):
    return lax.dot_general(a, b, (((0,), (0,)), ((), ())), preferred_element_type=f32)


def _dot_hi(a, b):
    return jnp.dot(a, b, precision=lax.Precision.HIGHEST, preferred_element_type=f32)


def _sigmoid(x):
    return 1.0 / (1.0 + jnp.exp(-x))


_GC = math.sqrt(2.0 / math.pi)


def _gelu(x):
    return 0.5 * x * (1.0 + jnp.tanh(_GC * (x + 0.044715 * x * x * x)))


def _gelu_grad(x):
    t = jnp.tanh(_GC * (x + 0.044715 * x * x * x))
    return 0.5 * (1.0 + t) + 0.5 * x * (1.0 - t * t) * _GC * (1.0 + 3.0 * 0.044715 * x * x)


def _rms_stats(x):
    return lax.rsqrt(jnp.mean(x * x, axis=-1, keepdims=True) + EPS)


def _rms_bwd(x, g, dh):
    r = _rms_stats(x)
    dyg = dh * g
    dx = r * dyg - x * (r * r * r) * jnp.mean(dyg * x, axis=-1, keepdims=True)
    dg = jnp.sum(dh * x * r, axis=0, keepdims=True)
    return dx, dg


def rms_fwd(x, g):
    n = x.shape[0]

    def body(x_ref, g_ref, h_ref):
        xv = x_ref[...]
        h_ref[...] = (xv * _rms_stats(xv) * g_ref[...]).astype(bf16)

    return pl.pallas_call(
        body, name="rms_fwd", grid=(n // TB,),
        in_specs=[pl.BlockSpec((TB, D), lambda i: (i, 0)), pl.BlockSpec((1, D), lambda i: (0, 0))],
        out_specs=pl.BlockSpec((TB, D), lambda i: (i, 0)),
        out_shape=jax.ShapeDtypeStruct((n, D), bf16), compiler_params=_cp(1))(x, g)


def inproj(h, w4, layer):
    n = h.shape[0]
    tm = 1024

    def body(h_ref, w_ref, o_ref):
        o_ref[...] = _dot(h_ref[...], w_ref[0, 0]).astype(bf16)

    return pl.pallas_call(
        body, name="inproj", grid=(NDEV, n // tm),
        in_specs=[pl.BlockSpec((tm, D), lambda s, i: (i, 0)),
                  pl.BlockSpec((1, 1, D, NSH_IN), lambda s, i: (layer, s, 0, 0))],
        out_specs=pl.BlockSpec((tm, NSH_IN), lambda s, i: (i, s)),
        out_shape=jax.ShapeDtypeStruct((n, INC), bf16), compiler_params=_cp(2))(h, w4)


def inproj_bwd(dproj, w4, layer, x, g, dres):
    n = x.shape[0]
    tm = 1024

    def body(dp_ref, w_ref, x_ref, g_ref, dr_ref, dx_ref, dg_ref, acc):
        i, s = pl.program_id(0), pl.program_id(1)

        @pl.when(s == 0)
        def _():
            acc[...] = jnp.zeros_like(acc)

        @pl.when((s == 0) & (i == 0))
        def _():
            dg_ref[...] = jnp.zeros_like(dg_ref)

        acc[...] += _dot_nt(dp_ref[...], w_ref[0, 0])

        @pl.when(s == NDEV - 1)
        def _():
            dx, dg = _rms_bwd(x_ref[...], g_ref[...], acc[...])
            dx_ref[...] = dr_ref[...] + dx
            dg_ref[...] += dg

    return pl.pallas_call(
        body, name="inproj_bwd", grid=(n // tm, NDEV),
        in_specs=[pl.BlockSpec((tm, NSH_IN), lambda i, s: (i, s)),
                  pl.BlockSpec((1, 1, D, NSH_IN), lambda i, s: (layer, s, 0, 0)),
                  pl.BlockSpec((tm, D), lambda i, s: (i, 0)),
                  pl.BlockSpec((1, D), lambda i, s: (0, 0)),
                  pl.BlockSpec((tm, D), lambda i, s: (i, 0))],
        out_specs=[pl.BlockSpec((tm, D), lambda i, s: (i, 0)), pl.BlockSpec((1, D), lambda i, s: (0, 0))],
        out_shape=[jax.ShapeDtypeStruct((n, D), f32), jax.ShapeDtypeStruct((1, D), f32)],
        scratch_shapes=[pltpu.VMEM((tm, D), f32)], compiler_params=_cp(2))(dproj, w4, x, g, dres)


def mm_tn(a, b, a_spec, b_spec, n_sh, ka, nb, m, name):
    tm = 1024

    def body(a_ref, b_ref, o_ref, acc):
        i = pl.program_id(1)

        @pl.when(i == 0)
        def _():
            acc[...] = jnp.zeros_like(acc)

        av = a_ref[...].reshape(tm, ka).astype(bf16)
        bv = b_ref[...].reshape(tm, nb).astype(bf16)
        acc[...] += _dot_tn(av, bv)

        @pl.when(i == m // tm - 1)
        def _():
            o_ref[0] = acc[...].astype(bf16)

    return pl.pallas_call(
        body, name=name, grid=(n_sh, m // tm), in_specs=[a_spec, b_spec],
        out_specs=pl.BlockSpec((1, ka, nb), lambda s, i: (s, 0, 0)),
        out_shape=jax.ShapeDtypeStruct((n_sh, ka, nb), bf16),
        scratch_shapes=[pltpu.VMEM((ka, nb), f32)], compiler_params=_cp(2))(a, b)


def dw_mix(ysin, dz, ob, dya, hs, dyc, merged, dx1):
    n = ysin.shape[0]
    tm = 512
    pairs = ((BW, 2 * D), (BW, D), (BW, D), (D, D))

    def body(a0, b0, a1, b1, a2, b2, a3, b3, o0, o1, o2, o3, c0, c1, c2, c3):
        i = pl.program_id(0)
        accs = (c0, c1, c2, c3)

        @pl.when(i == 0)
        def _():
            for c in accs:
                c[...] = jnp.zeros_like(c)

        for a, b_, c in zip((a0, a1, a2, a3), (b0, b1, b2, b3), accs):
            c[...] += _dot_tn(a[...], b_[...].astype(bf16))

        @pl.when(i == n // tm - 1)
        def _():
            for s in range(NDEV):
                o0[s] = c0[:, s * 256:(s + 1) * 256].astype(bf16)
                o1[s] = c1[:, s * 128:(s + 1) * 128].astype(bf16)
                o2[s] = c2[:, s * 128:(s + 1) * 128].astype(bf16)
                o3[s] = c3[s * 128:(s + 1) * 128, :].astype(bf16)

    tok = lambda w: pl.BlockSpec((tm, w), lambda i: (i, 0))
    whole = lambda shape: pl.BlockSpec(shape, lambda i: (0, 0, 0))
    outs = [(NDEV, BW, 256), (NDEV, BW, 128), (NDEV, BW, 128), (NDEV, D // NDEV, D)]
    return pl.pallas_call(
        body, name="dw_mix", grid=(n // tm,),
        in_specs=[tok(w) for pair in pairs for w in pair],
        out_specs=[whole(s) for s in outs], out_shape=[jax.ShapeDtypeStruct(s, bf16) for s in outs],
        scratch_shapes=[pltpu.VMEM(p, f32) for p in pairs],
        compiler_params=_cp(1, 56))(ysin, dz, ob, dya, hs, dyc, merged, dx1)


def ffn_fwd(x1, g2, w1, w2, layer):
    n = x1.shape[0]
    w2p = w2.reshape(w2.shape[0], 4, NSH_FF, D)

    def body(x_ref, g_ref, wa_ref, wb_ref, w2_ref, o_ref, z1_ref, z2_ref, h_sc):
        s = pl.program_id(1)

        @pl.when(s == 0)
        def _():
            xv = x_ref[...]
            h_sc[...] = (xv * _rms_stats(xv) * g_ref[...]).astype(bf16)
            o_ref[...] = xv

        h = h_sc[...]
        z1 = _dot(h, wa_ref[0, 0])
        z2 = _dot(h, wb_ref[0, 0])
        z1_ref[0] = z1.astype(bf16)
        z2_ref[0] = z2.astype(bf16)
        a = (z1 * _sigmoid(z1) * z2).astype(bf16)
        o_ref[...] += _dot(a, w2_ref[0, 0])

    sh3 = pl.BlockSpec((1, TB, NSH_FF), lambda i, s: (s, i, 0))
    return pl.pallas_call(
        body, name="ffn_fwd", grid=(n // TB, 4),
        in_specs=[pl.BlockSpec((TB, D), lambda i, s: (i, 0)),
                  pl.BlockSpec((1, D), lambda i, s: (0, 0)),
                  pl.BlockSpec((1, 1, D, NSH_FF), lambda i, s: (layer, s, 0, 0)),
                  pl.BlockSpec((1, 1, D, NSH_FF), lambda i, s: (layer, s + 4, 0, 0)),
                  pl.BlockSpec((1, 1, NSH_FF, D), lambda i, s: (layer, s, 0, 0))],
        out_specs=[pl.BlockSpec((TB, D), lambda i, s: (i, 0)), sh3, sh3],
        out_shape=[jax.ShapeDtypeStruct((n, D), f32), jax.ShapeDtypeStruct((4, n, NSH_FF), bf16),
                   jax.ShapeDtypeStruct((4, n, NSH_FF), bf16)],
        scratch_shapes=[pltpu.VMEM((TB, D), bf16)], compiler_params=_cp(2))(x1, g2, w1, w1, w2p)


def ffn_bwd(x1, g2, w1, w2, layer, dx2, z1s, z2s):
    n = x1.shape[0]
    w2p = w2.reshape(w2.shape[0], 4, NSH_FF, D)

    def body(x_ref, g_ref, dy_ref, wa_ref, wb_ref, w2_ref, z1_ref, z2_ref,
             dx_ref, h_ref, dz1_ref, dz2_ref, a_ref, dg_ref, dh_sc, dyb_sc):
        i, s = pl.program_id(0), pl.program_id(1)

        @pl.when(s == 0)
        def _():
            xv = x_ref[...]
            h_ref[...] = (xv * _rms_stats(xv) * g_ref[...]).astype(bf16)
            dh_sc[...] = jnp.zeros_like(dh_sc)
            dyb_sc[...] = dy_ref[...].astype(bf16)

        @pl.when((s == 0) & (i == 0))
        def _():
            dg_ref[...] = jnp.zeros_like(dg_ref)

        z1 = z1_ref[0].astype(f32)
        z2 = z2_ref[0].astype(f32)
        sg = _sigmoid(z1)
        sl = z1 * sg
        a_ref[0] = (sl * z2).astype(bf16)
        da = _dot_nt(dyb_sc[...], w2_ref[0, 0])
        dz2 = (da * sl).astype(bf16)
        dz1 = (da * z2 * sg * (1.0 + z1 * (1.0 - sg))).astype(bf16)
        dz1_ref[0] = dz1
        dz2_ref[0] = dz2
        dh_sc[...] += _dot_nt(dz1, wa_ref[0, 0]) + _dot_nt(dz2, wb_ref[0, 0])

        @pl.when(s == 3)
        def _():
            dx, dg = _rms_bwd(x_ref[...], g_ref[...], dh_sc[...])
            dx_ref[...] = dy_ref[...] + dx
            dg_ref[...] += dg

    tok = lambda i, s: (i, 0)
    sh3 = lambda i, s: (s, i, 0)
    return pl.pallas_call(
        body, name="ffn_bwd", grid=(n // TB, 4),
        in_specs=[pl.BlockSpec((TB, D), tok), pl.BlockSpec((1, D), lambda i, s: (0, 0)), pl.BlockSpec((TB, D), tok),
                  pl.BlockSpec((1, 1, D, NSH_FF), lambda i, s: (layer, s, 0, 0)),
                  pl.BlockSpec((1, 1, D, NSH_FF), lambda i, s: (layer, s + 4, 0, 0)),
                  pl.BlockSpec((1, 1, NSH_FF, D), lambda i, s: (layer, s, 0, 0)),
                  pl.BlockSpec((1, TB, NSH_FF), sh3), pl.BlockSpec((1, TB, NSH_FF), sh3)],
        out_specs=[pl.BlockSpec((TB, D), tok), pl.BlockSpec((TB, D), tok),
                   pl.BlockSpec((1, TB, NSH_FF), sh3), pl.BlockSpec((1, TB, NSH_FF), sh3),
                   pl.BlockSpec((1, TB, NSH_FF), sh3), pl.BlockSpec((1, D), lambda i, s: (0, 0))],
        out_shape=[jax.ShapeDtypeStruct((n, D), f32), jax.ShapeDtypeStruct((n, D), bf16),
                   jax.ShapeDtypeStruct((4, n, NSH_FF), bf16), jax.ShapeDtypeStruct((4, n, NSH_FF), bf16),
                   jax.ShapeDtypeStruct((4, n, NSH_FF), bf16), jax.ShapeDtypeStruct((1, D), f32)],
        scratch_shapes=[pltpu.VMEM((TB, D), f32), pltpu.VMEM((TB, D), bf16)],
        compiler_params=_cp(2))(x1, g2, dx2, w1, w1, w2p, z1s, z2s)


def loss_head(x, g, target):
    n = x.shape[0]

    def body(x_ref, g_ref, t_ref, l_ref, dx_ref, dg_ref):
        i = pl.program_id(0)

        @pl.when(i == 0)
        def _():
            l_ref[...] = jnp.zeros_like(l_ref)
            dg_ref[...] = jnp.zeros_like(dg_ref)

        xv = x_ref[...]
        y = xv * _rms_stats(xv) * g_ref[...]
        e = y - t_ref[...]
        l_ref[...] += 0.5 * jnp.sum(jnp.sum(e * e, axis=-1, keepdims=True), axis=0, keepdims=True) * (1.0 / D)
        dx, dg = _rms_bwd(xv, g_ref[...], e * (1.0 / D))
        dx_ref[...] = dx
        dg_ref[...] += dg

    tok = lambda i: (i, 0)
    return pl.pallas_call(
        body, name="loss_head", grid=(n // TB,),
        in_specs=[pl.BlockSpec((TB, D), tok), pl.BlockSpec((1, D), lambda i: (0, 0)), pl.BlockSpec((TB, D), tok)],
        out_specs=[pl.BlockSpec((8, 128), lambda i: (0, 0)), pl.BlockSpec((TB, D), tok),
                   pl.BlockSpec((1, D), lambda i: (0, 0))],
        out_shape=[jax.ShapeDtypeStruct((8, 128), f32), jax.ShapeDtypeStruct((n, D), f32),
                   jax.ShapeDtypeStruct((1, D), f32)],
        compiler_params=_cp(1))(x, g, target)


def _disc(lr, li, ld):
    dt = jnp.exp(ld)
    mag = jnp.exp(lr * dt)
    ar = mag * jnp.cos(li * dt)
    ai = mag * jnp.sin(li * dt)
    nr, ni = ar - 1.0, ai
    den = lr * lr + li * li
    zr = (nr * lr + ni * li) / den
    zi = (ni * lr - nr * li) / den
    return ar, ai, zr, zi


def _blockdiag_mask(shape):
    r = lax.broadcasted_iota(jnp.int32, shape, 0) // GH
    c = lax.broadcasted_iota(jnp.int32, shape, 1) // NS
    return r == c


def s5_params(lr, li, ld, btr, bti, ctr, cti):
    def body(lr_ref, li_ref, ld_ref, btr_ref, bti_ref, ctr_ref, cti_ref, t8_ref, bb_ref, cb_ref):
        ar, ai, zr, zi = _disc(lr_ref[...], li_ref[...], ld_ref[...])
        pr_, pi_ = ar, ai
        pw2 = []
        for k in range(4):
            pw2.append((pr_, pi_))
            pr_, pi_ = pr_ * pr_ - pi_ * pi_, 2.0 * pr_ * pi_
        cm = lambda p, q: (p[0] * q[0] - p[1] * q[1], p[0] * q[1] + p[1] * q[0])
        pw = {1: pw2[0], 2: pw2[1], 4: pw2[2], 8: pw2[3]}
        pw[3], pw[5], pw[6] = cm(pw[2], pw[1]), cm(pw[4], pw[1]), cm(pw[4], pw[2])
        pw[7] = cm(pw[4], pw[3])
        row = lax.broadcasted_iota(jnp.int32, (8, NSTATE), 0)
        zero = jnp.zeros((8, NSTATE), f32)
        for c in range(2):
            for k in range(3):
                full = jnp.broadcast_to(pw2[k][c], (8, NSTATE))
                t8_ref[c, k] = jnp.where(row >= (1 << k), full, 0.0)
                t8_ref[c, 3 + k] = jnp.where(row + (1 << k) < 8, full, 0.0)
            up, down = zero, zero
            for j in range(8):
                up = up + jnp.where(row == j, pw[j + 1][c], 0.0)
                down = down + jnp.where(row == j, pw[8 - j][c], 0.0)
            t8_ref[c, 6] = up
            t8_ref[c, 7] = down
        bbr = zr * btr_ref[...] - zi * bti_ref[...]
        bbi = zr * bti_ref[...] + zi * btr_ref[...]
        mask = _blockdiag_mask((SU, SW))
        for j in range(SJ):
            cols = slice(j * SW, (j + 1) * SW)
            for c, (vb, vc) in enumerate(((bbr, ctr_ref[...]), (bbi, cti_ref[...]))):
                bb_ref[c, j] = jnp.where(mask, jnp.tile(vb[:, cols], (SU // GH, 1)), 0.0).astype(bf16)
                cb_ref[c, j] = jnp.where(mask, jnp.tile(vc[:, cols], (SU // GH, 1)), 0.0).astype(bf16)

    return pl.pallas_call(
        body, name="s5_params",
        out_shape=[jax.ShapeDtypeStruct((2, 8, 8, NSTATE), f32),
                   jax.ShapeDtypeStruct((2, SJ, SU, SW), bf16), jax.ShapeDtypeStruct((2, SJ, SU, SW), bf16)],
        compiler_params=pltpu.CompilerParams(vmem_limit_bytes=56 * 1024 * 1024))(lr, li, ld, btr, bti, ctr, cti)


def s5_params_bwd(lr, li, ld, btr, bti, d_a, d_bb, d_cb):
    def body(lr_ref, li_ref, ld_ref, btr_ref, bti_ref, da_ref, dbb_ref, dcb_ref,
             dlr_ref, dli_ref, dld_ref, dbt_ref, dct_ref):
        mask = _blockdiag_mask((SU, SW))

        def fold(ref, c):
            parts = []
            for j in range(SJ):
                v = jnp.where(mask, ref[c, j], 0.0)
                parts.append(v.reshape(SU // GH, GH, SW).sum(axis=0))
            return jnp.concatenate(parts, axis=1)

        dct_ref[0] = fold(dcb_ref, 0)
        dct_ref[1] = fold(dcb_ref, 1)
        dbbr, dbbi = fold(dbb_ref, 0), fold(dbb_ref, 1)
        lrv, liv, ldv = lr_ref[...], li_ref[...], ld_ref[...]
        (ar, ai, zr, zi), vjp = jax.vjp(_disc, lrv, liv, ldv)
        btr, bti = btr_ref[...], bti_ref[...]
        dbt_ref[0] = zr * dbbr + zi * dbbi
        dbt_ref[1] = zr * dbbi - zi * dbbr
        dzr = jnp.sum(dbbr * btr + dbbi * bti, axis=0, keepdims=True)
        dzi = jnp.sum(dbbi * btr - dbbr * bti, axis=0, keepdims=True)
        dlr, dli, dld = vjp((da_ref[0:1, :], da_ref[1:2, :], dzr, dzi))
        dlr_ref[...] = dlr
        dli_ref[...] = dli
        ind = (lax.broadcasted_iota(jnp.int32, (NSTATE, 128), 0) // NS
               == lax.broadcasted_iota(jnp.int32, (NSTATE, 128), 1)).astype(f32)
        dld_ref[...] = _dot_hi(jnp.broadcast_to(dld, (8, NSTATE)), ind)

    return pl.pallas_call(
        body, name="s5_params_bwd",
        out_shape=[jax.ShapeDtypeStruct((1, NSTATE), f32), jax.ShapeDtypeStruct((1, NSTATE), f32),
                   jax.ShapeDtypeStruct((8, 128), f32), jax.ShapeDtypeStruct((2, GH, NSTATE), f32),
                   jax.ShapeDtypeStruct((2, GH, NSTATE), f32)],
        compiler_params=pltpu.CompilerParams(vmem_limit_bytes=56 * 1024 * 1024))(lr, li, ld, btr, bti, d_a, d_bb, d_cb)


def _fma(sr, si, ar, ai, qr, qi):
    return sr + ar * qr - ai * qi, si + ar * qi + ai * qr


def _scan_tile(sr, si, cr, ci, t8_ref, reverse):
    sg = -1.0 if reverse else 1.0
    for k in range(3):
        tk = 3 + k if reverse else k
        rot = 8 - (1 << k) if reverse else 1 << k
        sr, si = _fma(sr, si, t8_ref[0, tk], sg * t8_ref[1, tk], pltpu.roll(sr, rot, 0), pltpu.roll(si, rot, 0))
    tp = 7 if reverse else 6
    sr, si = _fma(sr, si, t8_ref[0, tp], sg * t8_ref[1, tp], cr, ci)
    e = 0 if reverse else 7
    return sr, si, jnp.broadcast_to(sr[e:e + 1, :], sr.shape), jnp.broadcast_to(si[e:e + 1, :], si.shape)


S5MC = 512


def _s5_input_map(u_ref, bb_ref, sr_sc, si_sc, l):
    for c in range(l // S5MC):
        rows = slice(c * S5MC, (c + 1) * S5MC)
        u = u_ref[0, rows, :]
        sr_sc[rows, :] = _dot(u, bb_ref[0, 0])
        si_sc[rows, :] = _dot(u, bb_ref[1, 0])


def _s5_forward_scan(sr_sc, si_sc, t8_ref, l):
    def step(k, carry):
        rows = pl.ds(pl.multiple_of(k * 8, 8), 8)
        sr, si, cr, ci = _scan_tile(sr_sc[rows, :], si_sc[rows, :], carry[0], carry[1], t8_ref, False)
        sr_sc[rows, :] = sr
        si_sc[rows, :] = si
        return cr, ci

    zero = jnp.zeros((8, SW), f32)
    lax.fori_loop(0, l // 8, step, (zero, zero), unroll=4)


def s5_fwd(proj3, t8, bb, cb, dskip):
    b, l, _ = proj3.shape

    def body(u_ref, t8_ref, bb_ref, cb_ref, d_ref, y_ref, sr_sc, si_sc):
        _s5_input_map(u_ref, bb_ref, sr_sc, si_sc, l)
        _s5_forward_scan(sr_sc, si_sc, t8_ref, l)
        for c in range(l // S5MC):
            rows = slice(c * S5MC, (c + 1) * S5MC)
            y = (_dot_nt(sr_sc[rows, :].astype(bf16), cb_ref[0, 0])
                 - _dot_nt(si_sc[rows, :].astype(bf16), cb_ref[1, 0]))
            y_ref[0, rows, :] = y + d_ref[...] * u_ref[0, rows, :].astype(f32)

    return pl.pallas_call(
        body, name="s5_fwd", grid=(SJ, b),
        in_specs=[pl.BlockSpec((1, l, SU), lambda j, bi: (bi, 0, j)),
                  pl.BlockSpec((2, 8, 8, SW), lambda j, bi: (0, 0, 0, j)),
                  pl.BlockSpec((2, 1, SU, SW), lambda j, bi: (0, j, 0, 0)),
                  pl.BlockSpec((2, 1, SU, SW), lambda j, bi: (0, j, 0, 0)),
                  pl.BlockSpec((1, SU), lambda j, bi: (0, j))],
        out_specs=pl.BlockSpec((1, l, SU), lambda j, bi: (bi, 0, j)),
        out_shape=jax.ShapeDtypeStruct((b, l, BW), f32),
        scratch_shapes=[pltpu.VMEM((l, SW), f32)] * 2, compiler_params=_cp(2))(proj3, t8, bb, cb, dskip)


def s5_bwd(proj3, dy, t8, bb, cb, dskip):
    b, l, _ = proj3.shape
    nt = l // 8

    def body(u_ref, dy_ref, t8_ref, bb_ref, cb_ref, d_ref,
             du_ref, da_ref, dbb_ref, dcb_ref, dd_ref, sr_sc, si_sc, gr_sc, gi_sc):
        bi = pl.program_id(1)

        @pl.when(bi == 0)
        def _():
            da_ref[...] = jnp.zeros_like(da_ref)
            dbb_ref[...] = jnp.zeros_like(dbb_ref)
            dcb_ref[...] = jnp.zeros_like(dcb_ref)
            dd_ref[...] = jnp.zeros_like(dd_ref)

        _s5_input_map(u_ref, bb_ref, sr_sc, si_sc, l)
        _s5_forward_scan(sr_sc, si_sc, t8_ref, l)
        for c in range(l // S5MC):
            rows = slice(c * S5MC, (c + 1) * S5MC)
            dyb = dy_ref[0, rows, :].astype(bf16)
            gr_sc[rows, :] = _dot(dyb, cb_ref[0, 0])
            gi_sc[rows, :] = -_dot(dyb, cb_ref[1, 0])

        row = lax.broadcasted_iota(jnp.int32, (8, SW), 0)

        def step(i, carry):
            cr, ci, dar, dai = carry
            k = nt - 1 - i
            rows = pl.ds(pl.multiple_of(k * 8, 8), 8)
            gr, gi, cr, ci = _scan_tile(gr_sc[rows, :], gi_sc[rows, :], cr, ci, t8_ref, True)
            gr_sc[rows, :] = gr
            gi_sc[rows, :] = gi
            before = pl.ds(pl.multiple_of(jnp.maximum(k - 1, 0) * 8, 8), 8)
            live = jnp.where(k > 0, 1.0, 0.0)
            sr, si = sr_sc[rows, :], si_sc[rows, :]
            spr = jnp.where(row == 0, live * sr_sc[before, :][7:8, :], pltpu.roll(sr, 1, 0))
            spi = jnp.where(row == 0, live * si_sc[before, :][7:8, :], pltpu.roll(si, 1, 0))
            return cr, ci, dar + spr * gr + spi * gi, dai + spr * gi - spi * gr

        zero = jnp.zeros((8, SW), f32)
        _, _, dar, dai = lax.fori_loop(0, nt, step, (zero, zero, zero, zero), unroll=2)
        da_ref[0:1, :] += jnp.sum(dar, axis=0, keepdims=True)
        da_ref[1:2, :] += jnp.sum(dai, axis=0, keepdims=True)

        for c in range(l // S5MC):
            rows = slice(c * S5MC, (c + 1) * S5MC)
            u = u_ref[0, rows, :]
            dyv = dy_ref[0, rows, :]
            dyb = dyv.astype(bf16)
            grb, gib = gr_sc[rows, :].astype(bf16), gi_sc[rows, :].astype(bf16)
            dcb_ref[0, 0] += _dot_tn(dyb, sr_sc[rows, :].astype(bf16))
            dcb_ref[1, 0] -= _dot_tn(dyb, si_sc[rows, :].astype(bf16))
            dbb_ref[0, 0] += _dot_tn(u, grb)
            dbb_ref[1, 0] += _dot_tn(u, gib)
            du = _dot_nt(grb, bb_ref[0, 0]) + _dot_nt(gib, bb_ref[1, 0]) + d_ref[...] * dyv
            du_ref[0, rows, :] = du.astype(bf16)
            dd_ref[...] += jnp.sum(dyv * u.astype(f32), axis=0, keepdims=True)

    seq = pl.BlockSpec((1, l, SU), lambda j, bi: (bi, 0, j))
    tab = pl.BlockSpec((2, 1, SU, SW), lambda j, bi: (0, j, 0, 0))
    return pl.pallas_call(
        body, name="s5_bwd", grid=(SJ, b),
        in_specs=[seq, seq, pl.BlockSpec((2, 8, 8, SW), lambda j, bi: (0, 0, 0, j)), tab, tab,
                  pl.BlockSpec((1, SU), lambda j, bi: (0, j))],
        out_specs=[seq, pl.BlockSpec((2, SW), lambda j, bi: (0, j)), tab, tab,
                   pl.BlockSpec((1, SU), lambda j, bi: (0, j))],
        out_shape=[jax.ShapeDtypeStruct((b, l, BW), bf16), jax.ShapeDtypeStruct((2, NSTATE), f32),
                   jax.ShapeDtypeStruct((2, SJ, SU, SW), f32), jax.ShapeDtypeStruct((2, SJ, SU, SW), f32),
                   jax.ShapeDtypeStruct((1, BW), f32)],
        scratch_shapes=[pltpu.VMEM((l, SW), f32)] * 4,
        compiler_params=_cp(2))(proj3, dy, t8, bb, cb, dskip)


AHC = 2
AHW = AHC * 128


def _att_mask(n, nb):
    if nb == 1:
        qi = lax.broadcasted_iota(jnp.int32, (ABLK, ABLK), 0)
        kj = lax.broadcasted_iota(jnp.int32, (ABLK, ABLK), 1)
        return kj <= qi
    qi = lax.broadcasted_iota(jnp.int32, (ABLK, 2 * ABLK), 0)
    kj = lax.broadcasted_iota(jnp.int32, (ABLK, 2 * ABLK), 1)
    return (kj >= qi) & (kj <= qi + ABLK) & ((n > 0) | (kj >= ABLK))


def _att_rows(it, nb, dil):
    r, n = it // nb, it % nb
    cur = pl.ds(r + n * (ABLK * dil), ABLK, stride=dil)
    prv = pl.ds(r + jnp.maximum(n - 1, 0) * (ABLK * dil), ABLK, stride=dil)
    return n, cur, prv


def _att_keys(ref, c, cur, prv, nb):
    if nb == 1:
        x = ref[c, cur, :].astype(bf16)
    else:
        x = jnp.concatenate([ref[c, prv, :], ref[c, cur, :]], axis=0).astype(bf16)
    head0 = lax.broadcasted_iota(jnp.int32, x.shape, 1) < HD
    zero = jnp.zeros_like(x)
    return jnp.concatenate([jnp.where(head0, x, zero), jnp.where(head0, zero, x)], axis=0)


def _per_head(nk, a0, a1):
    col = lax.broadcasted_iota(jnp.int32, (ABLK, 2 * nk), 1)
    return jnp.where(col < nk, a0, a1)


def _to_chunks(src_ref, dst):
    for c in range(AHC):
        dst[c] = src_ref[0, :, c * 128:(c + 1) * 128].astype(f32)


def att_fwd(proj3, g_idx, dil):
    b, l, _ = proj3.shape
    nb = l // dil // ABLK
    nhalf = BW // AHW

    def body(q_ref, k_ref, v_ref, o_ref, lse_ref, qf, kf, vf, of):
        hh = pl.program_id(1)
        _to_chunks(q_ref, qf)
        _to_chunks(k_ref, kf)
        _to_chunks(v_ref, vf)
        lane = lax.broadcasted_iota(jnp.int32, (ABLK, 128), 1)

        def step(it, carry):
            n, cur, prv = _att_rows(it, nb, dil)
            valid = _att_mask(n, nb)
            valid = jnp.concatenate([valid, valid], axis=1)
            nk = valid.shape[1] // 2
            lse_all = jnp.zeros((ABLK, 128), f32)
            for c in range(AHC):
                q = (qf[c, cur, :] * ATT_SCALE).astype(bf16)
                k = _att_keys(kf, c, cur, prv, nb)
                v = _att_keys(vf, c, cur, prv, nb)
                s = jnp.where(valid, _dot_nt(q, k), NEG)
                m0 = jnp.max(s[:, :nk], axis=-1, keepdims=True)
                m1 = jnp.max(s[:, nk:], axis=-1, keepdims=True)
                p = jnp.exp(s - _per_head(nk, m0, m1))
                den0 = jnp.sum(p[:, :nk], axis=-1, keepdims=True)
                den1 = jnp.sum(p[:, nk:], axis=-1, keepdims=True)
                of[c, cur, :] = _dot(p.astype(bf16), v) * jnp.where(lane < HD, 1.0 / den0, 1.0 / den1)
                head = hh * (2 * AHC) + 2 * c
                lse_all = (lse_all + jnp.where(lane == head, m0 + jnp.log(den0), 0.0)
                           + jnp.where(lane == head + 1, m1 + jnp.log(den1), 0.0))

            lse_ref[0, 0, cur, :] = lse_all
            return carry

        lax.fori_loop(0, dil * nb, step, 0, unroll=4)
        for c in range(AHC):
            o_ref[0, :, c * 128:(c + 1) * 128] = of[c].astype(bf16)

    col = lambda c: pl.BlockSpec((1, l, AHW), lambda bi, hh: (bi, 0, c * nhalf + hh))
    return pl.pallas_call(
        body, name=f"att_fwd{g_idx}", grid=(b, nhalf),
        in_specs=[col(1 + g_idx), col(4), col(5)],
        out_specs=[pl.BlockSpec((1, l, AHW), lambda bi, hh: (bi, 0, hh)),
                   pl.BlockSpec((1, 1, l, 128), lambda bi, hh: (bi, hh, 0, 0))],
        out_shape=[jax.ShapeDtypeStruct((b, l, BW), bf16), jax.ShapeDtypeStruct((b, nhalf, l, 128), f32)],
        scratch_shapes=[pltpu.VMEM((AHC, l, 128), f32)] * 4,
        compiler_params=_cp(2))(proj3, proj3, proj3)


def att_bwd(proj3, do, lse_tot, delta, g_idx, dil):
    b, l, _ = proj3.shape
    nb = l // dil // ABLK
    nhalf = BW // AHW

    def body(q_ref, k_ref, v_ref, do_ref, l_ref, dl_ref, dq_out, dk_out, dv_out, qf, kf, vf, dof,
             dq_ref, dk_ref, dv_ref):
        hh = pl.program_id(1)
        _to_chunks(q_ref, qf)
        _to_chunks(k_ref, kf)
        _to_chunks(v_ref, vf)
        _to_chunks(do_ref, dof)
        dk_ref[...] = jnp.zeros_like(dk_ref)
        dv_ref[...] = jnp.zeros_like(dv_ref)
        lane = lax.broadcasted_iota(jnp.int32, (ABLK, 128), 1)

        def step(it, carry):
            n, cur, prv = _att_rows(it, nb, dil)
            valid = _att_mask(n, nb)
            valid = jnp.concatenate([valid, valid], axis=1)
            nk = valid.shape[1] // 2
            lse_b = l_ref[0, cur, :]
            dl_b = dl_ref[0, cur, :]
            head0 = lax.broadcasted_iota(jnp.int32, (nk, 128), 1) < HD
            for c in range(AHC):
                q = (qf[c, cur, :] * ATT_SCALE).astype(bf16)
                dob = dof[c, cur, :].astype(bf16)
                k = _att_keys(kf, c, cur, prv, nb)
                v = _att_keys(vf, c, cur, prv, nb)
                head = hh * (2 * AHC) + 2 * c
                pick = lambda a, h: jnp.sum(jnp.where(lane == h, a, 0.0), axis=-1, keepdims=True)
                lse_h = _per_head(nk, pick(lse_b, head), pick(lse_b, head + 1))
                dl_h = _per_head(nk, pick(dl_b, head), pick(dl_b, head + 1))
                s = _dot_nt(q, k)
                p = jnp.where(valid, jnp.exp(jnp.minimum(s - lse_h, 60.0)), 0.0)
                ds = (p * (_dot_nt(dob, v) - dl_h)).astype(bf16)
                dq_ref[0, c, cur, :] = _dot(ds, k) * ATT_SCALE
                dk2 = _dot_tn(ds, q)
                dv2 = _dot_tn(p.astype(bf16), dob)
                dk = jnp.where(head0, dk2[:nk], dk2[nk:])
                dv = jnp.where(head0, dv2[:nk], dv2[nk:])
                if nb == 1:
                    dk_ref[0, c, cur, :] += dk
                    dv_ref[0, c, cur, :] += dv
                else:
                    dk_ref[0, c, cur, :] += dk[ABLK:]
                    dv_ref[0, c, cur, :] += dv[ABLK:]
                    dk_ref[0, c, prv, :] += dk[:ABLK]
                    dv_ref[0, c, prv, :] += dv[:ABLK]

            return carry

        lax.fori_loop(0, dil * nb, step, 0, unroll=4)
        dq_out[0] = dq_ref[0].astype(bf16)
        dk_out[0] = dk_ref[0].astype(bf16)
        dv_out[0] = dv_ref[0].astype(bf16)

    col = lambda c: pl.BlockSpec((1, l, AHW), lambda bi, hh: (bi, 0, c * nhalf + hh))
    own = pl.BlockSpec((1, l, AHW), lambda bi, hh: (bi, 0, hh))
    own128 = pl.BlockSpec((1, l, 128), lambda bi, hh: (bi, 0, 0))
    chunked = pl.BlockSpec((1, AHC, l, 128), lambda bi, hh: (bi, hh, 0, 0))
    return pl.pallas_call(
        body, name=f"att_bwd{g_idx}", grid=(b, nhalf),
        in_specs=[col(1 + g_idx), col(4), col(5), own, own128, own128],
        out_specs=[chunked] * 3,
        out_shape=[jax.ShapeDtypeStruct((b, BW // 128, l, 128), bf16)] * 3,
        scratch_shapes=[pltpu.VMEM((AHC, l, 128), f32)] * 4 + [pltpu.VMEM((1, AHC, l, 128), f32)] * 3,
        compiler_params=_cp(2, 56))(proj3, proj3, proj3, do, lse_tot, delta)


CPAD = 32
CTAIL = 16
CR = 128
CSLAB = CR + 40


def _tap_windows(slab, off, mis):
    ntap = (CW - 1 - mis) // 8 + 1
    xb = slab[off + mis:off + mis + CR + 8 * (ntap - 1)]
    for a in range(ntap):
        yield 8 * a + mis, xb[8 * a:8 * a + CR]


def _fill_glu(cv_ref, pad, l):
    pad[0:CPAD, :] = jnp.zeros((CPAD, BW), f32)
    pad[CPAD:CPAD + l, :] = cv_ref[0, :, :BW].astype(f32) * _sigmoid(cv_ref[0, :, BW:].astype(f32))
    pad[CPAD + l:, :] = jnp.zeros((CTAIL, BW), f32)


def conv_fwd(proj3, cw, cb):
    b, l, _ = proj3.shape

    def body(cv_ref, w_ref, b_ref, o_ref, pad):
        _fill_glu(cv_ref, pad, l)
        for lc in range(BW // 128):
            lanes = slice(lc * 128, (lc + 1) * 128)
            wv = w_ref[:, lanes]

            def step(c, carry):
                base = pl.multiple_of(c * CR, CR)
                slab = pad[pl.ds(base, CSLAB), lanes]
                acc = jnp.zeros((CR, 128), f32) + b_ref[:, lanes]
                for mis in range(8):
                    for k, win in _tap_windows(slab, CPAD - (CW - 1), mis):
                        acc = acc + wv[k:k + 1] * win
                o_ref[0, pl.ds(base, CR), lanes] = acc
                return carry

            lax.fori_loop(0, l // CR, step, 0)

    return pl.pallas_call(
        body, name="conv_fwd", grid=(b,),
        in_specs=[pl.BlockSpec((1, l, 2 * BW), lambda i: (i, 0, 3)),
                  pl.BlockSpec((32, BW), lambda i: (0, 0)), pl.BlockSpec((1, BW), lambda i: (0, 0))],
        out_specs=pl.BlockSpec((1, l, BW), lambda i: (i, 0, 0)),
        out_shape=jax.ShapeDtypeStruct((b, l, BW), f32),
        scratch_shapes=[pltpu.VMEM((CPAD + l + CTAIL, BW), f32)], compiler_params=_cp(1))(proj3, cw, cb)


def conv_bwd(proj3, dhc, cw):
    b, l, _ = proj3.shape

    def body(cv_ref, d_ref, w_ref, dcv_ref, dw_ref, db_ref, pad, dpad):
        i = pl.program_id(0)

        @pl.when(i == 0)
        def _():
            dw_ref[...] = jnp.zeros_like(dw_ref)
            db_ref[...] = jnp.zeros_like(db_ref)

        _fill_glu(cv_ref, pad, l)
        dpad[0:l, :] = d_ref[0]
        dpad[l:, :] = jnp.zeros((CPAD + CTAIL, BW), f32)
        db_ref[...] += jnp.sum(d_ref[0], axis=0, keepdims=True)
        for lc in range(BW // 128):
            lanes = slice(lc * 128, (lc + 1) * 128)
            glanes = slice(BW + lc * 128, BW + (lc + 1) * 128)
            wv = w_ref[:, lanes]

            for mis in range(8):
                ntap = (CW - 1 - mis) // 8 + 1

                def dw_step(c, accs, mis=mis, lanes=lanes):
                    base = pl.multiple_of(c * CR, CR)
                    slab = pad[pl.ds(base, CSLAB), lanes]
                    dv = dpad[pl.ds(base, CR), lanes]
                    return tuple(acc + (dv * win).reshape(CR // 8, 8, 128).sum(axis=0) for acc, (_, win)
                                 in zip(accs, _tap_windows(slab, CPAD - (CW - 1), mis)))

                accs = lax.fori_loop(0, l // CR, dw_step, tuple(jnp.zeros((8, 128), f32) for _ in range(ntap)))
                for a in range(ntap):
                    k = 8 * a + mis
                    dw_ref[k:k + 1, lanes] += jnp.sum(accs[a], axis=0, keepdims=True)

            def dh_step(c, carry, lanes=lanes, glanes=glanes, wv=wv):
                base = pl.multiple_of(c * CR, CR)
                slab = dpad[pl.ds(base, CSLAB), lanes]
                acc = jnp.zeros((CR, 128), f32)
                for mis in range(8):
                    for kk, win in _tap_windows(slab, 0, mis):
                        acc = acc + wv[CW - 1 - kk:CW - kk] * win
                rows = pl.ds(base, CR)
                a = cv_ref[0, rows, lanes].astype(f32)
                sg = _sigmoid(cv_ref[0, rows, glanes].astype(f32))
                dcv_ref[0, rows, lanes] = (acc * sg).astype(bf16)
                dcv_ref[0, rows, glanes] = (acc * a * sg * (1.0 - sg)).astype(bf16)
                return carry

            lax.fori_loop(0, l // CR, dh_step, 0)

    return pl.pallas_call(
        body, name="conv_bwd", grid=(b,),
        in_specs=[pl.BlockSpec((1, l, 2 * BW), lambda i: (i, 0, 3)),
                  pl.BlockSpec((1, l, BW), lambda i: (i, 0, 0)),
                  pl.BlockSpec((32, BW), lambda i: (0, 0))],
        out_specs=[pl.BlockSpec((1, l, 2 * BW), lambda i: (i, 0, 0)),
                   pl.BlockSpec((32, BW), lambda i: (0, 0)), pl.BlockSpec((1, BW), lambda i: (0, 0))],
        out_shape=[jax.ShapeDtypeStruct((b, l, 2 * BW), bf16), jax.ShapeDtypeStruct((32, BW), f32),
                   jax.ShapeDtypeStruct((1, BW), f32)],
        scratch_shapes=[pltpu.VMEM((CPAD + l + CTAIL, BW), f32), pltpu.VMEM((l + CPAD + CTAIL, BW), f32)],
        compiler_params=_cp(1))(proj3, dhc, cw)


def _head_expand():
    r = lax.broadcasted_iota(jnp.int32, (128, BW), 0)
    c = lax.broadcasted_iota(jnp.int32, (128, BW), 1) // HD
    return (r == c).astype(f32)


def _head_reduce():
    r = lax.broadcasted_iota(jnp.int32, (BW, 128), 0) // HD
    c = lax.broadcasted_iota(jnp.int32, (BW, 128), 1)
    return (r == c).astype(f32)


def _merge_common(ys_ref, o_refs, l_refs, hc_ref, g_refs, bg_ref, lng_ref, lnb_ref, wglu_ref, watt_ref, wpw_ref):
    r = {}
    ysv = ys_ref[...]
    r["ys"] = ysv
    r["ysin"] = _gelu(ysv).astype(bf16)
    z = _dot(r["ysin"], wglu_ref[...])
    r["z1"], r["sg2"] = z[:, :D], _sigmoid(z[:, D:])
    r["y_s"] = r["z1"] * r["sg2"]
    ls = [lr_[0, 0] + lr_[0, 1] for lr_ in l_refs]
    mx = jnp.maximum(jnp.maximum(ls[0], ls[1]), ls[2])
    es = [jnp.exp(v - mx) for v in ls]
    tot = es[0] + es[1] + es[2]
    r["lse_tot"] = mx + jnp.log(tot)
    e_mat = _head_expand()
    o = jnp.zeros(ysv.shape, f32)
    for e, o_ref in zip(es, o_refs):
        o = o + _dot_hi(e / tot, e_mat) * o_ref[...].astype(f32)
    r["o"] = o
    r["ob"] = o.astype(bf16)
    r["y_a"] = _dot(r["ob"], watt_ref[...])
    hc = hc_ref[...]
    mu = jnp.mean(hc, axis=-1, keepdims=True)
    xc = hc - mu
    rstd = lax.rsqrt(jnp.mean(xc * xc, axis=-1, keepdims=True) + EPS)
    r["xh"], r["rstd"] = xc * rstd, rstd
    hn = r["xh"] * lng_ref[...] + lnb_ref[...]
    r["hn"] = hn
    r["sgn"] = _sigmoid(hn)
    r["hs"] = (hn * r["sgn"]).astype(bf16)
    r["y_c"] = _dot(r["hs"], wpw_ref[...])
    r["gates"] = [_sigmoid(g_refs[k][...].astype(f32) + bg_ref[:, k * D:(k + 1) * D]) for k in range(3)]
    r["merged"] = r["gates"][0] * r["y_s"] + r["gates"][1] * r["y_a"] + r["gates"][2] * r["y_c"]
    return r


TBM = 256


def _merge_in_specs(tok, tb, lses):
    w = lambda shape: pl.BlockSpec(shape, lambda i: (0, 0))
    nbl = lses[0].shape[2] // tb
    return ([pl.BlockSpec((tb, D), tok), pl.BlockSpec((tb, BW), tok)]
            + [pl.BlockSpec((tb, BW), tok)] * 3
            + [pl.BlockSpec((1, 2, tb, 128), lambda i: (i // nbl, 0, i % nbl, 0))] * 3
            + [pl.BlockSpec((tb, BW), tok)]
            + [pl.BlockSpec((tb, D), lambda i, k=k: (i, 4 + k)) for k in range(3)]
            + [w((1, 3 * D)), w((1, BW)), w((1, BW)), w((BW, 2 * D)), w((BW, D)), w((BW, D)), w((D, D))])


def merge_fwd(x, ys, os_, lses, hc, proj, bg, lng, lnb, wglu, watt, wpw, wout):
    n = x.shape[0]

    def body(x_ref, ys_ref, o1, o2, o3, l1, l2, l3, hc_ref, g0, g1, g2, bg_ref, lng_ref, lnb_ref,
             wglu_ref, watt_ref, wpw_ref, wout_ref, x1_ref):
        r = _merge_common(ys_ref, (o1, o2, o3), (l1, l2, l3), hc_ref, (g0, g1, g2), bg_ref, lng_ref, lnb_ref,
                          wglu_ref, watt_ref, wpw_ref)
        x1_ref[...] = x_ref[...] + _dot(r["merged"].astype(bf16), wout_ref[...])

    tok = lambda i: (i, 0)
    return pl.pallas_call(
        body, name="merge_fwd", grid=(n // TB,), in_specs=_merge_in_specs(tok, TB, lses),
        out_specs=pl.BlockSpec((TB, D), tok), out_shape=jax.ShapeDtypeStruct((n, D), f32),
        compiler_params=_cp(1, 56))(x, ys, *os_, *lses, hc, proj, proj, proj, bg, lng, lnb, wglu, watt, wpw, wout)


def merge_bwd(dx1, ys, os_, lses, hc, proj, bg, lng, lnb, wglu, watt, wpw, wout):
    n = dx1.shape[0]

    def body(dx_ref, ys_ref, o1, o2, o3, l1, l2, l3, hc_ref, g0, g1, g2, bg_ref, lng_ref, lnb_ref,
             wglu_ref, watt_ref, wpw_ref, wout_ref,
             dys_ref, do_ref, delta_ref, ltot_ref, dhc_ref, dgate_ref, ysin_ref, dz_ref, ob_ref, dya_ref,
             hs_ref, dyc_ref, mg_ref, dbg_ref, dlng_ref, dlnb_ref):
        i = pl.program_id(0)

        @pl.when(i == 0)
        def _():
            dbg_ref[...] = jnp.zeros_like(dbg_ref)
            dlng_ref[...] = jnp.zeros_like(dlng_ref)
            dlnb_ref[...] = jnp.zeros_like(dlnb_ref)

        r = _merge_common(ys_ref, (o1, o2, o3), (l1, l2, l3), hc_ref, (g0, g1, g2), bg_ref, lng_ref, lnb_ref,
                          wglu_ref, watt_ref, wpw_ref)
        mg_ref[...] = r["merged"].astype(bf16)
        ysin_ref[...] = r["ysin"]
        ob_ref[...] = r["ob"]
        hs_ref[...] = r["hs"]
        ltot_ref[...] = r["lse_tot"]
        dm = _dot_nt(dx_ref[...].astype(bf16), wout_ref[...])
        ys3 = (r["y_s"], r["y_a"], r["y_c"])
        for k in range(3):
            gk = r["gates"][k]
            dgr = dm * ys3[k] * gk * (1.0 - gk)
            dgate_ref[:, k * D:(k + 1) * D] = dgr.astype(bf16)
            dbg_ref[:, k * D:(k + 1) * D] += jnp.sum(dgr, axis=0, keepdims=True)
        dy_s = dm * r["gates"][0]
        sg2 = r["sg2"]
        dz = jnp.concatenate([dy_s * sg2, dy_s * r["z1"] * sg2 * (1.0 - sg2)], axis=1).astype(bf16)
        dz_ref[...] = dz
        dys_ref[...] = _dot_nt(dz, wglu_ref[...]) * _gelu_grad(r["ys"])
        dya = (dm * r["gates"][1]).astype(bf16)
        dya_ref[...] = dya
        do = _dot_nt(dya, watt_ref[...])
        do_ref[...] = do.astype(bf16)
        delta_ref[...] = _dot_hi(do * r["o"], _head_reduce())
        dyc = (dm * r["gates"][2]).astype(bf16)
        dyc_ref[...] = dyc
        sgn, hn = r["sgn"], r["hn"]
        dhn = _dot_nt(dyc, wpw_ref[...]) * sgn * (1.0 + hn * (1.0 - sgn))
        dlng_ref[...] += jnp.sum(dhn * r["xh"], axis=0, keepdims=True)
        dlnb_ref[...] += jnp.sum(dhn, axis=0, keepdims=True)
        dxh = dhn * lng_ref[...]
        xh = r["xh"]
        dhc_ref[...] = r["rstd"] * (dxh - jnp.mean(dxh, axis=-1, keepdims=True)
                                    - xh * jnp.mean(dxh * xh, axis=-1, keepdims=True))

    tok = lambda i: (i, 0)
    fix = lambda i: (0, 0)
    outs = [("dys", BW, f32), ("do", BW, bf16), ("delta", 128, f32), ("lse_tot", 128, f32), ("dhc", BW, f32),
            ("dgate", 3 * D, bf16), ("ysin", BW, bf16), ("dz", 2 * D, bf16), ("ob", BW, bf16), ("dya", D, bf16),
            ("hs", BW, bf16), ("dyc", D, bf16), ("merged", D, bf16)]
    small = [("dbg", 3 * D), ("dlng", BW), ("dlnb", BW)]
    res = pl.pallas_call(
        body, name="merge_bwd", grid=(n // TBM,), in_specs=_merge_in_specs(tok, TBM, lses),
        out_specs=[pl.BlockSpec((TBM, w), tok) for _, w, _ in outs] + [pl.BlockSpec((1, w), fix) for _, w in small],
        out_shape=[jax.ShapeDtypeStruct((n, w), dt) for _, w, dt in outs]
        + [jax.ShapeDtypeStruct((1, w), f32) for _, w in small],
        compiler_params=_cp(1, 56))(dx1, ys, *os_, *lses, hc, proj, proj, proj, bg, lng, lnb, wglu, watt, wpw, wout)
    return dict(zip([k for k, _, _ in outs] + [k for k, _ in small], res))


def assemble_dproj(du, dqs, dks, dvs, dcv, dgate):
    b, l, _ = du.shape
    nck = BW // 128

    def body(du_ref, q1, q2, q3, k1, k2, k3, v1, v2, v3, cv_ref, g_ref, o_ref):
        o_ref[0, :, 0:BW] = du_ref[0]
        for c in range(nck):
            for j, qr in enumerate((q1, q2, q3)):
                o_ref[0, :, (1 + j) * BW + c * 128:(1 + j) * BW + (c + 1) * 128] = qr[0, c]
            add3 = lambda r1, r2, r3: (r1[0, c].astype(f32) + r2[0, c].astype(f32) + r3[0, c].astype(f32)).astype(bf16)
            o_ref[0, :, 4 * BW + c * 128:4 * BW + (c + 1) * 128] = add3(k1, k2, k3)
            o_ref[0, :, 5 * BW + c * 128:5 * BW + (c + 1) * 128] = add3(v1, v2, v3)
        o_ref[0, :, 6 * BW:8 * BW] = cv_ref[0]
        o_ref[0, :, 8 * BW:] = g_ref[0]

    t = lambda w: pl.BlockSpec((1, TB, w), lambda bi, i: (bi, i, 0))
    ck = pl.BlockSpec((1, nck, TB, 128), lambda bi, i: (bi, 0, i, 0))
    return pl.pallas_call(
        body, name="assemble_dproj", grid=(b, l // TB),
        in_specs=[t(BW)] + [ck] * 9 + [t(2 * BW), t(3 * D)], out_specs=t(INC),
        out_shape=jax.ShapeDtypeStruct((b, l, INC), bf16), compiler_params=_cp(2))(du, *dqs, *dks, *dvs, dcv, dgate)


def _me():
    return lax.axis_index("x"), lax.axis_index("y"), lax.axis_index("c")


def _peers():
    x, y, c = _me()
    return [(x, y, 1 - c), (1 - x, y, c), (1 - x, y, 1 - c), (x, 1 - y, c), (x, 1 - y, 1 - c),
            (1 - x, 1 - y, c), (1 - x, 1 - y, 1 - c)]


def _rank(p):
    return 4 * p[0] + 2 * p[1] + p[2]


def allgather(arrs, name):
    na = len(arrs)
    units = [(a, j) for a in range(na) for j in range(arrs[a].shape[0])]
    nu = len(units)

    def body(*refs):
        ins, outs = refs[:na], refs[na:2 * na]
        send, recv, loc = refs[2 * na:]
        me = _rank(_me())
        local, remote = [], []
        for u, (a, j) in enumerate(units):
            own = pltpu.make_async_copy(ins[a].at[j], outs[a].at[j, me], loc.at[u])
            own.start()
            local.append(own)
        for u, (a, j) in enumerate(units):
            for k, p in enumerate(_peers()):
                cp = pltpu.make_async_remote_copy(src_ref=ins[a].at[j], dst_ref=outs[a].at[j, me],
                                                  send_sem=send.at[u, k], recv_sem=recv.at[u, k],
                                                  device_id=p, device_id_type=MESH)
                cp.start()
                remote.append(cp)
        for cp in local:
            cp.wait()
        for cp in remote:
            cp.wait()

    return pl.pallas_call(
        body, name=name, in_specs=[ANY] * na, out_specs=[ANY] * na,
        out_shape=[jax.ShapeDtypeStruct((a.shape[0], NDEV) + a.shape[1:], a.dtype) for a in arrs],
        scratch_shapes=[pltpu.SemaphoreType.DMA((nu, NDEV - 1)), pltpu.SemaphoreType.DMA((nu, NDEV - 1)),
                        pltpu.SemaphoreType.DMA((nu,))])(*arrs)


HBM = pl.BlockSpec(memory_space=pltpu.HBM)
SEM = pl.BlockSpec(memory_space=pltpu.SEMAPHORE)
_EFFECT = pltpu.SideEffectType.DATAFLOW_SIDE_EFFECTING


def _push_copies(srcs, lands, send, recv, scatter):
    me = _rank(_me())
    out = []
    for i in range(len(srcs)):
        for k, p in enumerate(_peers()):
            src = srcs[i].at[_rank(p)] if scatter else srcs[i]
            dst = lands[i].at[k] if scatter else lands[i].at[me]
            j = i * (NDEV - 1) + k
            out.append(pltpu.make_async_remote_copy(src_ref=src, dst_ref=dst, send_sem=send.at[j],
                                                    recv_sem=recv.at[j], device_id=p, device_id_type=MESH))
    return out


def push_start(srcs, lands, scatter, name, token):
    n = len(srcs)
    token = jnp.zeros((8, 128), f32) if token is None else token

    def body(*refs):
        for cp in _push_copies(refs[:n], refs[n:2 * n], refs[2 * n + 1], refs[2 * n + 2], scatter):
            cp.start()
        refs[-1][...] = refs[2 * n][...]

    sems = pltpu.SemaphoreType.DMA((n * (NDEV - 1),))
    vmem = pl.BlockSpec(memory_space=pltpu.VMEM)
    res = pl.pallas_call(
        body, name=name, in_specs=[HBM] * (2 * n) + [vmem], out_specs=[SEM, SEM] + [HBM] * (2 * n) + [vmem],
        out_shape=[sems, sems] + [pltpu.HBM(a.shape, a.dtype) for a in list(srcs) + list(lands)]
        + [jax.ShapeDtypeStruct((8, 128), f32)],
        input_output_aliases={i: 2 + i for i in range(2 * n)},
        compiler_params=pltpu.CompilerParams(has_side_effects=_EFFECT),
    )(*[pltpu.with_memory_space_constraint(a, pltpu.HBM) for a in list(srcs) + list(lands)], token)
    return res[0], res[1], res[2:2 + n], res[2 + n:2 + 2 * n], res[-1]


def push_wait(send, recv, srcs, lands, after, scatter, name):
    n = len(srcs)

    def body(*refs):
        for cp in _push_copies(refs[:n], refs[n:2 * n], refs[2 * n], refs[2 * n + 1], scatter):
            cp.wait_send()
            cp.wait_recv()

    res = pl.pallas_call(
        body, name=name, in_specs=[HBM] * (2 * n) + [SEM, SEM, ANY], out_specs=[HBM] * (2 * n),
        out_shape=[pltpu.HBM(a.shape, a.dtype) for a in list(srcs) + list(lands)],
        input_output_aliases={i: i for i in range(2 * n)},
        compiler_params=pltpu.CompilerParams(has_side_effects=_EFFECT),
    )(*srcs, *lands, send, recv, after)
    return res[:n], res[n:]


_C1 = 1.0 / (1.0 - ADAM_B1 ** ADAM_STEP)
_C2 = 1.0 / (1.0 - ADAM_B2 ** ADAM_STEP)


def _adamw(w, g, m, v):
    m = ADAM_B1 * m + (1.0 - ADAM_B1) * g
    v = ADAM_B2 * v + (1.0 - ADAM_B2) * (g * g)
    delta = -ADAM_LR * ((m * _C1) / (jnp.sqrt(v * _C2) + ADAM_EPS) + ADAM_WD * w)
    return delta, m, v


def adam_big(lands, owns, w, m, v, name):
    _, k, n = lands[0].shape
    tk = k
    while tk * n * 2 * NDEV > 2 * 1024 * 1024 and tk % 16 == 0:
        tk //= 2

    def body(*refs):
        l_refs, o_refs = refs[:DEPTH], refs[DEPTH:2 * DEPTH]
        w_ref, m_ref, v_ref, g_ref, d_ref, nm_ref, nv_ref = refs[2 * DEPTH:]
        for l in range(DEPTH):
            g = o_refs[l][...].astype(f32)
            for s in range(NDEV - 1):
                g = g + l_refs[l][s].astype(f32)
            d, nm, nv = _adamw(w_ref[l], g, m_ref[l], v_ref[l])
            g_ref[l], d_ref[l], nm_ref[l], nv_ref[l] = g, d, nm, nv

    blk = pl.BlockSpec((DEPTH, tk, n), lambda i: (0, i, 0))
    return pl.pallas_call(
        body, name=name, grid=(k // tk,),
        in_specs=[pl.BlockSpec((NDEV - 1, tk, n), lambda i: (0, i, 0))] * DEPTH
        + [pl.BlockSpec((tk, n), lambda i: (i, 0))] * DEPTH + [blk, blk, blk],
        out_specs=[blk] * 4, out_shape=[jax.ShapeDtypeStruct(w.shape, f32)] * 4,
        compiler_params=_cp(1))(*lands, *owns, w, m, v)


def adam_small(gath, w, m, v):
    r = w.shape[0]
    tr = 512

    def body(g_ref, w_ref, m_ref, v_ref, go_ref, d_ref, nm_ref, nv_ref):
        g = g_ref[0]
        for s in range(1, NDEV):
            g = g + g_ref[s]
        d, nm, nv = _adamw(w_ref[...], g, m_ref[...], v_ref[...])
        go_ref[...], d_ref[...], nm_ref[...], nv_ref[...] = g, d, nm, nv

    blk = pl.BlockSpec((tr, 128), lambda i: (i, 0))
    return pl.pallas_call(
        body, name="adam_small", grid=(r // tr,),
        in_specs=[pl.BlockSpec((NDEV, tr, 128), lambda i: (0, i, 0)), blk, blk, blk],
        out_specs=[blk] * 4, out_shape=[jax.ShapeDtypeStruct((r, 128), f32)] * 4,
        compiler_params=_cp(1))(gath, w, m, v)


SMALL = ["norm1_g", "b_gate", "ssm_lambda_re", "ssm_lambda_im", "ssm_log_dt", "ssm_b_re", "ssm_b_im",
         "ssm_c_re", "ssm_c_im", "ssm_d", "conv_w", "conv_b", "conv_ln_g", "conv_ln_b", "norm2_g", "final_g"]
BIG = ["w_in", "w_ssm_glu", "w_att_up", "w_conv_pw2", "w_out", "w_ffn_in", "w_ffn_out"]
ORDER = ["norm1_g", "w_in", "b_gate", "ssm_lambda_re", "ssm_lambda_im", "ssm_log_dt", "ssm_b_re", "ssm_b_im",
         "ssm_c_re", "ssm_c_im", "ssm_d", "w_ssm_glu", "w_att_up", "conv_w", "conv_b", "conv_ln_g", "conv_ln_b",
         "w_conv_pw2", "w_out", "norm2_g", "w_ffn_in", "w_ffn_out", "final_g"]
PACK_ROWS = 2560


def _pack(arrs):
    flat = jnp.concatenate([a.reshape(-1).astype(f32) for a in arrs])
    return jnp.pad(flat, (0, PACK_ROWS * 128 - flat.shape[0])).reshape(PACK_ROWS, 128)


def _unpack(pack, shapes):
    flat = pack.reshape(-1)
    out, off = [], 0
    for s in shapes:
        sz = math.prod(s)
        out.append(flat[off:off + sz].reshape(s))
        off += sz
    return out


def _bt(b):
    return b.transpose(2, 0, 1).reshape(GH, NSTATE)


def _bt_inv(bt):
    return bt.reshape(GH, NG, NS).transpose(1, 2, 0)


def _ct(c):
    return c.transpose(1, 0, 2).reshape(GH, NSTATE)


def _ct_inv(ct):
    return ct.reshape(GH, NG, NS).transpose(1, 0, 2)


def local_step(x, loss_target, P, weights, on_grads, start_token=None):
    bsz, seq, _ = x.shape
    n = bsz * seq

    def natural(g3):
        return g3.transpose(1, 0, 2).reshape(g3.shape[1], NDEV * g3.shape[2])

    tokens = [] if start_token is None else [start_token]

    def after_pushes(a):
        while tokens:
            a = a + tokens.pop()[0:1, 0:1]
        return a

    def pushed(tok):
        if tok is not None:
            tokens.append(tok)

    xs = x.reshape(n, D)
    saved = []
    conv_w_pad = None
    for l in range(DEPTH):
        S = {"x": xs}
        h1 = rms_fwd(xs, after_pushes(P["norm1_g"][l][None]))
        G = dict(weights(l, "in", h1))
        if conv_w_pad is None:
            conv_w_full = G["conv_w"].transpose(1, 2, 0, 3).reshape(DEPTH, CW, BW)
            conv_w_pad = jnp.pad(conv_w_full, ((0, 0), (0, 1), (0, 0)))
        w_in4 = G["w_in"][None]
        proj = inproj(h1, w_in4, 0)
        proj3 = proj.reshape(bsz, seq, INC)
        lr = P["ssm_lambda_re"][l].reshape(1, NSTATE)
        li = P["ssm_lambda_im"][l].reshape(1, NSTATE)
        ld = jnp.repeat(P["ssm_log_dt"][l], NS).reshape(1, NSTATE)
        btr, bti = _bt(P["ssm_b_re"][l]), _bt(P["ssm_b_im"][l])
        t8, bb, cb = s5_params(lr, li, ld, btr, bti, _ct(P["ssm_c_re"][l]), _ct(P["ssm_c_im"][l]))
        dskip = P["ssm_d"][l][None]
        ys = s5_fwd(proj3, t8, bb, cb, dskip)
        att = [att_fwd(proj3, gi, dil) for gi, (_, dil) in enumerate(PATTERNS)]
        hc = conv_fwd(proj3, conv_w_pad[l], P["conv_b"][l][None])
        G.update(weights(l, "mix", hc))
        wts = dict(wglu=natural(G["w_ssm_glu"]), watt=natural(G["w_att_up"]),
                   wpw=natural(G["w_conv_pw2"]), wout=G["w_out"].reshape(D, D))
        mi = dict(ys=ys.reshape(n, BW), os_=[a[0].reshape(n, BW) for a in att],
                  lses=[a[1] for a in att], hc=hc.reshape(n, BW),
                  proj=proj, bg=P["b_gate"][l][None], lng=P["conv_ln_g"][l][None], lnb=P["conv_ln_b"][l][None],
                  **wts)
        x1 = merge_fwd(xs, **mi)
        G.update(weights(l, "ffn", x1))
        w_ffn = (G["w_ffn_in"][None], G["w_ffn_out"][None])
        x2, z1s, z2s = ffn_fwd(x1, P["norm2_g"][l][None], *w_ffn, 0)
        S.update(h1=h1, proj=proj, proj3=proj3, tabs=(t8, bb, cb), mi=mi, x1=x1, w_in4=w_in4, w_ffn=w_ffn, zs=(z1s, z2s),
                 sp=(lr, li, ld, btr, bti), dskip=dskip)
        saved.append(S)
        xs = x2

    loss8, dx, dfinal = loss_head(xs, P["final_g"][None], loss_target.reshape(n, D))

    small_g = {k: [None] * DEPTH for k in SMALL if k != "final_g"}
    tokblk = lambda w: pl.BlockSpec((1024, w), lambda s, i: (i, 0))
    colblk = lambda w: pl.BlockSpec((1024, w), lambda s, i: (i, s))
    sh3blk = lambda w: pl.BlockSpec((1, 1024, w), lambda s, i: (s, i, 0))
    for l in reversed(range(DEPTH)):
        S = saved[l]
        g2 = P["norm2_g"][l][None]
        dx1, h2, dz1, dz2, a4, dg2 = ffn_bwd(S["x1"], after_pushes(g2), *S["w_ffn"], 0, dx, *S["zs"])
        small_g["norm2_g"][l] = dg2
        dwa = mm_tn(h2, dz1, tokblk(D), sh3blk(NSH_FF), 4, D, NSH_FF, n, "dw_ffn_in_a")
        dwb = mm_tn(h2, dz2, tokblk(D), sh3blk(NSH_FF), 4, D, NSH_FF, n, "dw_ffn_in_b")
        pushed(on_grads(l, "ffn", dict(
            w_ffn_in=jnp.concatenate([dwa, dwb], axis=0),
            w_ffn_out=mm_tn(a4, dx, sh3blk(NSH_FF), tokblk(D), 4, NSH_FF, D, n,
                            "dw_ffn_out").reshape(NDEV, NSH_FF // 2, D))))
        mb = merge_bwd(dx1, **dict(S["mi"], lng=after_pushes(S["mi"]["lng"])))
        small_g["b_gate"][l], small_g["conv_ln_g"][l], small_g["conv_ln_b"][l] = mb["dbg"], mb["dlng"], mb["dlnb"]
        dws = dw_mix(mb["ysin"], mb["dz"], mb["ob"], mb["dya"], mb["hs"], mb["dyc"], mb["merged"], dx1)
        pushed(on_grads(l, "mix", dict(zip(("w_ssm_glu", "w_att_up", "w_conv_pw2", "w_out"), dws))))
        dcv, dcw, dcb = conv_bwd(S["proj3"], mb["dhc"].reshape(bsz, seq, BW), after_pushes(conv_w_pad[l]))
        small_g["conv_w"][l] = dcw[:CW].reshape(CW, NDEV, BW // NDEV).transpose(1, 0, 2)
        small_g["conv_b"][l] = dcb
        ab = [att_bwd(S["proj3"], mb["do"].reshape(bsz, seq, BW), mb["lse_tot"].reshape(bsz, seq, 128),
                      mb["delta"].reshape(bsz, seq, 128), gi, dil) for gi, (_, dil) in enumerate(PATTERNS)]
        t8, bb, cb = S["tabs"]
        du, d_a, d_bb, d_cb, d_d = s5_bwd(S["proj3"], mb["dys"].reshape(bsz, seq, BW), t8, bb, cb, S["dskip"])
        lr, li, ld, btr, bti = S["sp"]
        dlr, dli, dld, dbt, dct = s5_params_bwd(lr, li, ld, btr, bti, d_a, d_bb, d_cb)
        small_g["ssm_lambda_re"][l], small_g["ssm_lambda_im"][l] = dlr.reshape(NG, NS), dli.reshape(NG, NS)
        small_g["ssm_log_dt"][l] = dld[0, :NG]
        small_g["ssm_b_re"][l], small_g["ssm_b_im"][l] = _bt_inv(dbt[0]), _bt_inv(dbt[1])
        small_g["ssm_c_re"][l], small_g["ssm_c_im"][l] = _ct_inv(dct[0]), _ct_inv(dct[1])
        small_g["ssm_d"][l] = d_d
        dproj = assemble_dproj(du, [a[0] for a in ab], [a[1] for a in ab], [a[2] for a in ab],
                               dcv, mb["dgate"].reshape(bsz, seq, 3 * D)).reshape(n, INC)
        pushed(on_grads(l, "in", dict(w_in=mm_tn(S["h1"], dproj, tokblk(D), colblk(NSH_IN), NDEV, D, NSH_IN, n,
                                                 "dw_in"))))
        if l == 0:
            pushed(on_grads(l, "small", dict(small_g=small_g, loss8=loss8, dfinal=dfinal)))
        dx, dg1 = inproj_bwd(dproj, S["w_in4"], 0, S["x"], after_pushes(P["norm1_g"][l][None]), dx1)
        small_g["norm1_g"][l] = dg1
    return loss8, dx, dfinal, small_g


def kernel(x, norm1_g, w_in, b_gate, ssm_lambda_re, ssm_lambda_im, ssm_log_dt, ssm_b_re, ssm_b_im, ssm_c_re, ssm_c_im, ssm_d, w_ssm_glu, w_att_up, conv_w, conv_b, conv_ln_g, conv_ln_b, w_conv_pw2, w_out, norm2_g, w_ffn_in, w_ffn_out, final_g, loss_target, m_norm1_g, m_w_in, m_b_gate, m_ssm_lambda_re, m_ssm_lambda_im, m_ssm_log_dt, m_ssm_b_re, m_ssm_b_im, m_ssm_c_re, m_ssm_c_im, m_ssm_d, m_w_ssm_glu, m_w_att_up, m_conv_w, m_conv_b, m_conv_ln_g, m_conv_ln_b, m_w_conv_pw2, m_w_out, m_norm2_g, m_w_ffn_in, m_w_ffn_out, m_final_g, v_norm1_g, v_w_in, v_b_gate, v_ssm_lambda_re, v_ssm_lambda_im, v_ssm_log_dt, v_ssm_b_re, v_ssm_b_im, v_ssm_c_re, v_ssm_c_im, v_ssm_d, v_w_ssm_glu, v_w_att_up, v_conv_w, v_conv_b, v_conv_ln_g, v_conv_ln_b, v_w_conv_pw2, v_w_out, v_norm2_g, v_w_ffn_in, v_w_ffn_out, v_final_g):
    args = dict(locals())
    W = {k: args[k] for k in ORDER}
    M = {k: args["m_" + k] for k in ORDER}
    V = {k: args["v_" + k] for k in ORDER}
    bsz, seq, _ = x.shape
    n = bsz * seq
    me = 4 * lax.axis_index("x") + 2 * lax.axis_index("y") + lax.axis_index("c")

    groups = {"in": ["w_in"], "mix": ["w_ssm_glu", "w_att_up", "w_conv_pw2", "w_out"], "ffn": ["w_ffn_in", "w_ffn_out"]}
    wb = {k: W[k].astype(bf16) for k in BIG}

    def landing(shard):
        return lax.dynamic_update_index_in_dim(lax.empty((NDEV,) + shard.shape, shard.dtype), shard, me, 0)

    plan = [("gather_a", [("w_in", 0), ("conv_w", None)]),
            ("gather_b", [(k, 0) for k in groups["mix"] + groups["ffn"]]),
            ("gather_c", [(k, 1) for k in BIG])]
    pending, token = {}, None
    for name, items in plan:
        shards = [conv_w if l is None else wb[k][l] for k, l in items]
        send, recv, s_thru, l_thru, token = push_start(shards, [landing(s) for s in shards], False, name, token)
        pending[name] = (send, recv, s_thru, l_thru, items)
    gathered = {}

    def weights(l, group, after):
        name = "gather_c" if l == 1 else ("gather_a" if group == "in" else "gather_b")
        if name in pending:
            send, recv, s_thru, l_thru, items = pending.pop(name)
            for item, arr in zip(items, push_wait(send, recv, s_thru, l_thru, after, False, name + "_wait")[1]):
                gathered[item] = arr
        res = {k: gathered[(k, l)] for k in groups[group]}
        if group == "in":
            res["conv_w"] = gathered[("conv_w", None)]
        return res

    big_g = {k: [None] * DEPTH for k in BIG}
    flights = []

    def start_exchange(items, name):
        parts = [big_g[k][l] for k, l in items]
        lands = [lax.empty((NDEV - 1,) + p.shape[1:], p.dtype) for p in parts]
        send, recv, s_thru, l_thru, tok = push_start(parts, lands, True, name, None)
        flights.append((send, recv, s_thru, l_thru, items, name))
        return tok

    names = [k for k in SMALL if k != "final_g"]
    shapes = [(DEPTH, NDEV, CW, BW // NDEV) if k == "conv_w" else W[k].shape for k in names] + [(D,), (1,)]
    small_flight = []

    def start_small(small_g, loss8, dfinal):
        sg_ = dict(small_g, norm1_g=[jnp.zeros((1, D), f32), small_g["norm1_g"][1]])
        gpack = _pack([jnp.stack([g.reshape(shapes[i][1:]) for g in sg_[k]]) for i, k in enumerate(names)]
                      + [dfinal, loss8[0, :1]])
        send, recv, s_thru, l_thru, tok = push_start([gpack], [landing(gpack)], False, "gather_small", None)
        small_flight.append((send, recv, s_thru, l_thru))
        return tok

    def on_grads(l, group, grads):
        if group == "small":
            return start_small(**grads)
        for k, g in grads.items():
            big_g[k][l] = g
        if l == 1 and group == "in":
            return start_exchange([(k, 1) for k in BIG], "exchange_l1")
        if l == 0:
            return start_exchange([(k, 0) for k in groups[group]], "exchange_l0_" + group)
        return None

    loss8, dx, dfinal, small_g = local_step(x, loss_target, W, weights, on_grads, token)

    landed, own = {}, {}
    for send, recv, s_thru, l_thru, items, name in flights:
        srcs, lands = push_wait(send, recv, s_thru, l_thru, dx, True, name + "_wait")
        for item, src, land in zip(items, srcs, lands):
            landed[item] = land
            own[item] = lax.dynamic_index_in_dim(src, me, 0, keepdims=False)
    out = {}
    for k in BIG:
        items = [(k, l) for l in range(DEPTH)]
        out[k] = adam_big([landed[i] for i in items], [own[i] for i in items], W[k], M[k], V[k], "adam_" + k)

    def wpack(src):
        parts = [jnp.broadcast_to(src[k][:, None], shapes[i]) if k == "conv_w" else src[k] for i, k in enumerate(names)]
        return _pack(parts + [src["final_g"], jnp.ones((1,), f32)])

    send, recv, s_thru, l_thru = small_flight[0]
    gall = push_wait(send, recv, s_thru, l_thru, dx, False, "gather_small_wait")[1][0]
    (late,) = allgather([small_g["norm1_g"][0].reshape(1, D // 128, 128)], "allgather_late")
    gall = lax.dynamic_update_slice(gall, late[0], (0, 0, 0))
    sg, sd, sm, sv = [_unpack(p, shapes) for p in adam_small(gall, wpack(W), wpack(M), wpack(V))]
    for i, k in enumerate(names + ["final_g"]):
        vals = [t[i] for t in (sg, sd, sm, sv)]
        if k == "conv_w":
            vals = [lax.dynamic_index_in_dim(t, me, axis=1, keepdims=False) for t in vals]
        out[k] = vals
    loss = sg[-1].reshape(())

    res = [loss, dx.reshape(bsz, seq, D)]
    for j in range(4):
        res += [out[k][j] for k in ORDER]
    return tuple(res)
```

```python
import functools
import math

import jax
import jax.numpy as jnp
from jax import lax
from jax.experimental import pallas as pl
from jax.experimental.pallas import tpu as pltpu

f32 = jnp.float32
bf16 = jnp.bfloat16

D = 1024
DEPTH = 2
EPS = 1e-6
BW = 512
NG = 32
GH = 16
NS = 64
NSTATE = NG * NS
HD = 64
NH = 8
PATTERNS = ((128, 1), (512, 4), (2048, 16))
ABLK = 128
ATT_SCALE = HD ** -0.5
CW = 31
DFF = 2816
INC = 7168
NDEV = 8
NSH_IN = INC // NDEV
NSH_FF = 2 * DFF // NDEV
ADAM_LR, ADAM_B1, ADAM_B2, ADAM_EPS, ADAM_WD, ADAM_STEP = 0.001, 0.9, 0.999, 1e-08, 0.01, 10

TB = 512
SJ = 4
SW = NSTATE // SJ
SU = BW // SJ
NEG = -1e30
MESH = pl.DeviceIdType.MESH
ANY = pl.BlockSpec(memory_space=pl.ANY)


def _cp(n_axes, vmem_mb=48):
    return pltpu.CompilerParams(dimension_semantics=("arbitrary",) * n_axes,
                                vmem_limit_bytes=vmem_mb * 1024 * 1024)


def _dot(a, b):
    return jnp.dot(a, b, preferred_element_type=f32)


def _dot_nt(a, b):
    return lax.dot_general(a, b, (((1,), (1,)), ((), ())), preferred_element_type=f32)


def _dot_tn(a, b):
    return lax.dot_general(a, b, (((0,), (0,)), ((), ())), preferred_element_type=f32)


def _dot_hi(a, b):
    return jnp.dot(a, b, precision=lax.Precision.HIGHEST, preferred_element_type=f32)


def _sigmoid(x):
    return 1.0 / (1.0 + jnp.exp(-x))


_GC = math.sqrt(2.0 / math.pi)


def _gelu(x):
    return 0.5 * x * (1.0 + jnp.tanh(_GC * (x + 0.044715 * x * x * x)))


def _gelu_grad(x):
    t = jnp.tanh(_GC * (x + 0.044715 * x * x * x))
    return 0.5 * (1.0 + t) + 0.5 * x * (1.0 - t * t) * _GC * (1.0 + 3.0 * 0.044715 * x * x)


def _rms_stats(x):
    return lax.rsqrt(jnp.mean(x * x, axis=-1, keepdims=True) + EPS)


def _rms_bwd(x, g, dh):
    r = _rms_stats(x)
    dyg = dh * g
    dx = r * dyg - x * (r * r * r) * jnp.mean(dyg * x, axis=-1, keepdims=True)
    dg = jnp.sum(dh * x * r, axis=0, keepdims=True)
    return dx, dg


def rms_fwd(x, g):
    n = x.shape[0]

    def body(x_ref, g_ref, h_ref):
        xv = x_ref[...]
        h_ref[...] = (xv * _rms_stats(xv) * g_ref[...]).astype(bf16)

    return pl.pallas_call(
        body, name="rms_fwd", grid=(n // TB,),
        in_specs=[pl.BlockSpec((TB, D), lambda i: (i, 0)), pl.BlockSpec((1, D), lambda i: (0, 0))],
        out_specs=pl.BlockSpec((TB, D), lambda i: (i, 0)),
        out_shape=jax.ShapeDtypeStruct((n, D), bf16), compiler_params=_cp(1))(x, g)


def inproj(h, w4, layer):
    n = h.shape[0]
    tm = 1024

    def body(h_ref, w_ref, o_ref):
        o_ref[...] = _dot(h_ref[...], w_ref[0, 0]).astype(bf16)

    return pl.pallas_call(
        body, name="inproj", grid=(NDEV, n // tm),
        in_specs=[pl.BlockSpec((tm, D), lambda s, i: (i, 0)),
                  pl.BlockSpec((1, 1, D, NSH_IN), lambda s, i: (layer, s, 0, 0))],
        out_specs=pl.BlockSpec((tm, NSH_IN), lambda s, i: (i, s)),
        out_shape=jax.ShapeDtypeStruct((n, INC), bf16), compiler_params=_cp(2))(h, w4)


def inproj_bwd(dproj, w4, layer, x, g, dres):
    n = x.shape[0]
    tm = 1024

    def body(dp_ref, w_ref, x_ref, g_ref, dr_ref, dx_ref, dg_ref, acc):
        i, s = pl.program_id(0), pl.program_id(1)

        @pl.when(s == 0)
        def _():
            acc[...] = jnp.zeros_like(acc)

        @pl.when((s == 0) & (i == 0))
        def _():
            dg_ref[...] = jnp.zeros_like(dg_ref)

        acc[...] += _dot_nt(dp_ref[...], w_ref[0, 0])

        @pl.when(s == NDEV - 1)
        def _():
            dx, dg = _rms_bwd(x_ref[...], g_ref[...], acc[...])
            dx_ref[...] = dr_ref[...] + dx
            dg_ref[...] += dg

    return pl.pallas_call(
        body, name="inproj_bwd", grid=(n // tm, NDEV),
        in_specs=[pl.BlockSpec((tm, NSH_IN), lambda i, s: (i, s)),
                  pl.BlockSpec((1, 1, D, NSH_IN), lambda i, s: (layer, s, 0, 0)),
                  pl.BlockSpec((tm, D), lambda i, s: (i, 0)),
                  pl.BlockSpec((1, D), lambda i, s: (0, 0)),
                  pl.BlockSpec((tm, D), lambda i, s: (i, 0))],
        out_specs=[pl.BlockSpec((tm, D), lambda i, s: (i, 0)), pl.BlockSpec((1, D), lambda i, s: (0, 0))],
        out_shape=[jax.ShapeDtypeStruct((n, D), f32), jax.ShapeDtypeStruct((1, D), f32)],
        scratch_shapes=[pltpu.VMEM((tm, D), f32)], compiler_params=_cp(2))(dproj, w4, x, g, dres)


def mm_tn(a, b, a_spec, b_spec, n_sh, ka, nb, m, name):
    tm = 1024

    def body(a_ref, b_ref, o_ref, acc):
        i = pl.program_id(1)

        @pl.when(i == 0)
        def _():
            acc[...] = jnp.zeros_like(acc)

        av = a_ref[...].reshape(tm, ka).astype(bf16)
        bv = b_ref[...].reshape(tm, nb).astype(bf16)
        acc[...] += _dot_tn(av, bv)

        @pl.when(i == m // tm - 1)
        def _():
            o_ref[0] = acc[...].astype(bf16)

    return pl.pallas_call(
        body, name=name, grid=(n_sh, m // tm), in_specs=[a_spec, b_spec],
        out_specs=pl.BlockSpec((1, ka, nb), lambda s, i: (s, 0, 0)),
        out_shape=jax.ShapeDtypeStruct((n_sh, ka, nb), bf16),
        scratch_shapes=[pltpu.VMEM((ka, nb), f32)], compiler_params=_cp(2))(a, b)


def dw_mix(ysin, dz, ob, dya, hs, dyc, merged, dx1):
    n = ysin.shape[0]
    tm = 512
    pairs = ((BW, 2 * D), (BW, D), (BW, D), (D, D))

    def body(a0, b0, a1, b1, a2, b2, a3, b3, o0, o1, o2, o3, c0, c1, c2, c3):
        i = pl.program_id(0)
        accs = (c0, c1, c2, c3)

        @pl.when(i == 0)
        def _():
            for c in accs:
                c[...] = jnp.zeros_like(c)

        for a, b_, c in zip((a0, a1, a2, a3), (b0, b1, b2, b3), accs):
            c[...] += _dot_tn(a[...], b_[...].astype(bf16))

        @pl.when(i == n // tm - 1)
        def _():
            for s in range(NDEV):
                o0[s] = c0[:, s * 256:(s + 1) * 256].astype(bf16)
                o1[s] = c1[:, s * 128:(s + 1) * 128].astype(bf16)
                o2[s] = c2[:, s * 128:(s + 1) * 128].astype(bf16)
                o3[s] = c3[s * 128:(s + 1) * 128, :].astype(bf16)

    tok = lambda w: pl.BlockSpec((tm, w), lambda i: (i, 0))
    whole = lambda shape: pl.BlockSpec(shape, lambda i: (0, 0, 0))
    outs = [(NDEV, BW, 256), (NDEV, BW, 128), (NDEV, BW, 128), (NDEV, D // NDEV, D)]
    return pl.pallas_call(
        body, name="dw_mix", grid=(n // tm,),
        in_specs=[tok(w) for pair in pairs for w in pair],
        out_specs=[whole(s) for s in outs], out_shape=[jax.ShapeDtypeStruct(s, bf16) for s in outs],
        scratch_shapes=[pltpu.VMEM(p, f32) for p in pairs],
        compiler_params=_cp(1, 56))(ysin, dz, ob, dya, hs, dyc, merged, dx1)


def ffn_fwd(x1, g2, w1, w2, layer):
    n = x1.shape[0]
    w2p = w2.reshape(w2.shape[0], 4, NSH_FF, D)

    def body(x_ref, g_ref, wa_ref, wb_ref, w2_ref, o_ref, z1_ref, z2_ref, h_sc):
        s = pl.program_id(1)

        @pl.when(s == 0)
        def _():
            xv = x_ref[...]
            h_sc[...] = (xv * _rms_stats(xv) * g_ref[...]).astype(bf16)
            o_ref[...] = xv

        h = h_sc[...]
        z1 = _dot(h, wa_ref[0, 0])
        z2 = _dot(h, wb_ref[0, 0])
        z1_ref[0] = z1.astype(bf16)
        z2_ref[0] = z2.astype(bf16)
        a = (z1 * _sigmoid(z1) * z2).astype(bf16)
        o_ref[...] += _dot(a, w2_ref[0, 0])

    sh3 = pl.BlockSpec((1, TB, NSH_FF), lambda i, s: (s, i, 0))
    return pl.pallas_call(
        body, name="ffn_fwd", grid=(n // TB, 4),
        in_specs=[pl.BlockSpec((TB, D), lambda i, s: (i, 0)),
                  pl.BlockSpec((1, D), lambda i, s: (0, 0)),
                  pl.BlockSpec((1, 1, D, NSH_FF), lambda i, s: (layer, s, 0, 0)),
                  pl.BlockSpec((1, 1, D, NSH_FF), lambda i, s: (layer, s + 4, 0, 0)),
                  pl.BlockSpec((1, 1, NSH_FF, D), lambda i, s: (layer, s, 0, 0))],
        out_specs=[pl.BlockSpec((TB, D), lambda i, s: (i, 0)), sh3, sh3],
        out_shape=[jax.ShapeDtypeStruct((n, D), f32), jax.ShapeDtypeStruct((4, n, NSH_FF), bf16),
                   jax.ShapeDtypeStruct((4, n, NSH_FF), bf16)],
        scratch_shapes=[pltpu.VMEM((TB, D), bf16)], compiler_params=_cp(2))(x1, g2, w1, w1, w2p)


def ffn_bwd(x1, g2, w1, w2, layer, dx2, z1s, z2s):
    n = x1.shape[0]
    w2p = w2.reshape(w2.shape[0], 4, NSH_FF, D)

    def body(x_ref, g_ref, dy_ref, wa_ref, wb_ref, w2_ref, z1_ref, z2_ref,
             dx_ref, h_ref, dz1_ref, dz2_ref, a_ref, dg_ref, dh_sc, dyb_sc):
        i, s = pl.program_id(0), pl.program_id(1)

        @pl.when(s == 0)
        def _():
            xv = x_ref[...]
            h_ref[...] = (xv * _rms_stats(xv) * g_ref[...]).astype(bf16)
            dh_sc[...] = jnp.zeros_like(dh_sc)
            dyb_sc[...] = dy_ref[...].astype(bf16)

        @pl.when((s == 0) & (i == 0))
        def _():
            dg_ref[...] = jnp.zeros_like(dg_ref)

        z1 = z1_ref[0].astype(f32)
        z2 = z2_ref[0].astype(f32)
        sg = _sigmoid(z1)
        sl = z1 * sg
        a_ref[0] = (sl * z2).astype(bf16)
        da = _dot_nt(dyb_sc[...], w2_ref[0, 0])
        dz2 = (da * sl).astype(bf16)
        dz1 = (da * z2 * sg * (1.0 + z1 * (1.0 - sg))).astype(bf16)
        dz1_ref[0] = dz1
        dz2_ref[0] = dz2
        dh_sc[...] += _dot_nt(dz1, wa_ref[0, 0]) + _dot_nt(dz2, wb_ref[0, 0])

        @pl.when(s == 3)
        def _():
            dx, dg = _rms_bwd(x_ref[...], g_ref[...], dh_sc[...])
            dx_ref[...] = dy_ref[...] + dx
            dg_ref[...] += dg

    tok = lambda i, s: (i, 0)
    sh3 = lambda i, s: (s, i, 0)
    return pl.pallas_call(
        body, name="ffn_bwd", grid=(n // TB, 4),
        in_specs=[pl.BlockSpec((TB, D), tok), pl.BlockSpec((1, D), lambda i, s: (0, 0)), pl.BlockSpec((TB, D), tok),
                  pl.BlockSpec((1, 1, D, NSH_FF), lambda i, s: (layer, s, 0, 0)),
                  pl.BlockSpec((1, 1, D, NSH_FF), lambda i, s: (layer, s + 4, 0, 0)),
                  pl.BlockSpec((1, 1, NSH_FF, D), lambda i, s: (layer, s, 0, 0)),
                  pl.BlockSpec((1, TB, NSH_FF), sh3), pl.BlockSpec((1, TB, NSH_FF), sh3)],
        out_specs=[pl.BlockSpec((TB, D), tok), pl.BlockSpec((TB, D), tok),
                   pl.BlockSpec((1, TB, NSH_FF), sh3), pl.BlockSpec((1, TB, NSH_FF), sh3),
                   pl.BlockSpec((1, TB, NSH_FF), sh3), pl.BlockSpec((1, D), lambda i, s: (0, 0))],
        out_shape=[jax.ShapeDtypeStruct((n, D), f32), jax.ShapeDtypeStruct((n, D), bf16),
                   jax.ShapeDtypeStruct((4, n, NSH_FF), bf16), jax.ShapeDtypeStruct((4, n, NSH_FF), bf16),
                   jax.ShapeDtypeStruct((4, n, NSH_FF), bf16), jax.ShapeDtypeStruct((1, D), f32)],
        scratch_shapes=[pltpu.VMEM((TB, D), f32), pltpu.VMEM((TB, D), bf16)],
        compiler_params=_cp(2))(x1, g2, dx2, w1, w1, w2p, z1s, z2s)


def loss_head(x, g, target):
    n = x.shape[0]

    def body(x_ref, g_ref, t_ref, l_ref, dx_ref, dg_ref):
        i = pl.program_id(0)

        @pl.when(i == 0)
        def _():
            l_ref[...] = jnp.zeros_like(l_ref)
            dg_ref[...] = jnp.zeros_like(dg_ref)

        xv = x_ref[...]
        y = xv * _rms_stats(xv) * g_ref[...]
        e = y - t_ref[...]
        l_ref[...] += 0.5 * jnp.sum(jnp.sum(e * e, axis=-1, keepdims=True), axis=0, keepdims=True) * (1.0 / D)
        dx, dg = _rms_bwd(xv, g_ref[...], e * (1.0 / D))
        dx_ref[...] = dx
        dg_ref[...] += dg

    tok = lambda i: (i, 0)
    return pl.pallas_call(
        body, name="loss_head", grid=(n // TB,),
        in_specs=[pl.BlockSpec((TB, D), tok), pl.BlockSpec((1, D), lambda i: (0, 0)), pl.BlockSpec((TB, D), tok)],
        out_specs=[pl.BlockSpec((8, 128), lambda i: (0, 0)), pl.BlockSpec((TB, D), tok),
                   pl.BlockSpec((1, D), lambda i: (0, 0))],
        out_shape=[jax.ShapeDtypeStruct((8, 128), f32), jax.ShapeDtypeStruct((n, D), f32),
                   jax.ShapeDtypeStruct((1, D), f32)],
        compiler_params=_cp(1))(x, g, target)


def _disc(lr, li, ld):
    dt = jnp.exp(ld)
    mag = jnp.exp(lr * dt)
    ar = mag * jnp.cos(li * dt)
    ai = mag * jnp.sin(li * dt)
    nr, ni = ar - 1.0, ai
    den = lr * lr + li * li
    zr = (nr * lr + ni * li) / den
    zi = (ni * lr - nr * li) / den
    return ar, ai, zr, zi


def _blockdiag_mask(shape):
    r = lax.broadcasted_iota(jnp.int32, shape, 0) // GH
    c = lax.broadcasted_iota(jnp.int32, shape, 1) // NS
    return r == c


def s5_params(lr, li, ld, btr, bti, ctr, cti):
    def body(lr_ref, li_ref, ld_ref, btr_ref, bti_ref, ctr_ref, cti_ref, t8_ref, bb_ref, cb_ref):
        ar, ai, zr, zi = _disc(lr_ref[...], li_ref[...], ld_ref[...])
        pr_, pi_ = ar, ai
        pw2 = []
        for k in range(4):
            pw2.append((pr_, pi_))
            pr_, pi_ = pr_ * pr_ - pi_ * pi_, 2.0 * pr_ * pi_
        cm = lambda p, q: (p[0] * q[0] - p[1] * q[1], p[0] * q[1] + p[1] * q[0])
        pw = {1: pw2[0], 2: pw2[1], 4: pw2[2], 8: pw2[3]}
        pw[3], pw[5], pw[6] = cm(pw[2], pw[1]), cm(pw[4], pw[1]), cm(pw[4], pw[2])
        pw[7] = cm(pw[4], pw[3])
        row = lax.broadcasted_iota(jnp.int32, (8, NSTATE), 0)
        zero = jnp.zeros((8, NSTATE), f32)
        for c in range(2):
            for k in range(3):
                full = jnp.broadcast_to(pw2[k][c], (8, NSTATE))
                t8_ref[c, k] = jnp.where(row >= (1 << k), full, 0.0)
                t8_ref[c, 3 + k] = jnp.where(row + (1 << k) < 8, full, 0.0)
            up, down = zero, zero
            for j in range(8):
                up = up + jnp.where(row == j, pw[j + 1][c], 0.0)
                down = down + jnp.where(row == j, pw[8 - j][c], 0.0)
            t8_ref[c, 6] = up
            t8_ref[c, 7] = down
        bbr = zr * btr_ref[...] - zi * bti_ref[...]
        bbi = zr * bti_ref[...] + zi * btr_ref[...]
        mask = _blockdiag_mask((SU, SW))
        for j in range(SJ):
            cols = slice(j * SW, (j + 1) * SW)
            for c, (vb, vc) in enumerate(((bbr, ctr_ref[...]), (bbi, cti_ref[...]))):
                bb_ref[c, j] = jnp.where(mask, jnp.tile(vb[:, cols], (SU // GH, 1)), 0.0).astype(bf16)
                cb_ref[c, j] = jnp.where(mask, jnp.tile(vc[:, cols], (SU // GH, 1)), 0.0).astype(bf16)

    return pl.pallas_call(
        body, name="s5_params",
        out_shape=[jax.ShapeDtypeStruct((2, 8, 8, NSTATE), f32),
                   jax.ShapeDtypeStruct((2, SJ, SU, SW), bf16), jax.ShapeDtypeStruct((2, SJ, SU, SW), bf16)],
        compiler_params=pltpu.CompilerParams(vmem_limit_bytes=56 * 1024 * 1024))(lr, li, ld, btr, bti, ctr, cti)


def s5_params_bwd(lr, li, ld, btr, bti, d_a, d_bb, d_cb):
    def body(lr_ref, li_ref, ld_ref, btr_ref, bti_ref, da_ref, dbb_ref, dcb_ref,
             dlr_ref, dli_ref, dld_ref, dbt_ref, dct_ref):
        mask = _blockdiag_mask((SU, SW))

        def fold(ref, c):
            parts = []
            for j in range(SJ):
                v = jnp.where(mask, ref[c, j], 0.0)
                parts.append(v.reshape(SU // GH, GH, SW).sum(axis=0))
            return jnp.concatenate(parts, axis=1)

        dct_ref[0] = fold(dcb_ref, 0)
        dct_ref[1] = fold(dcb_ref, 1)
        dbbr, dbbi = fold(dbb_ref, 0), fold(dbb_ref, 1)
        lrv, liv, ldv = lr_ref[...], li_ref[...], ld_ref[...]
        (ar, ai, zr, zi), vjp = jax.vjp(_disc, lrv, liv, ldv)
        btr, bti = btr_ref[...], bti_ref[...]
        dbt_ref[0] = zr * dbbr + zi * dbbi
        dbt_ref[1] = zr * dbbi - zi * dbbr
        dzr = jnp.sum(dbbr * btr + dbbi * bti, axis=0, keepdims=True)
        dzi = jnp.sum(dbbi * btr - dbbr * bti, axis=0, keepdims=True)
        dlr, dli, dld = vjp((da_ref[0:1, :], da_ref[1:2, :], dzr, dzi))
        dlr_ref[...] = dlr
        dli_ref[...] = dli
        ind = (lax.broadcasted_iota(jnp.int32, (NSTATE, 128), 0) // NS
               == lax.broadcasted_iota(jnp.int32, (NSTATE, 128), 1)).astype(f32)
        dld_ref[...] = _dot_hi(jnp.broadcast_to(dld, (8, NSTATE)), ind)

    return pl.pallas_call(
        body, name="s5_params_bwd",
        out_shape=[jax.ShapeDtypeStruct((1, NSTATE), f32), jax.ShapeDtypeStruct((1, NSTATE), f32),
                   jax.ShapeDtypeStruct((8, 128), f32), jax.ShapeDtypeStruct((2, GH, NSTATE), f32),
                   jax.ShapeDtypeStruct((2, GH, NSTATE), f32)],
        compiler_params=pltpu.CompilerParams(vmem_limit_bytes=56 * 1024 * 1024))(lr, li, ld, btr, bti, d_a, d_bb, d_cb)


def _fma(sr, si, ar, ai, qr, qi):
    return sr + ar * qr - ai * qi, si + ar * qi + ai * qr


def _scan_tile(sr, si, cr, ci, t8_ref, reverse):
    sg = -1.0 if reverse else 1.0
    for k in range(3):
        tk = 3 + k if reverse else k
        rot = 8 - (1 << k) if reverse else 1 << k
        sr, si = _fma(sr, si, t8_ref[0, tk], sg * t8_ref[1, tk], pltpu.roll(sr, rot, 0), pltpu.roll(si, rot, 0))
    tp = 7 if reverse else 6
    sr, si = _fma(sr, si, t8_ref[0, tp], sg * t8_ref[1, tp], cr, ci)
    e = 0 if reverse else 7
    return sr, si, jnp.broadcast_to(sr[e:e + 1, :], sr.shape), jnp.broadcast_to(si[e:e + 1, :], si.shape)


S5MC = 512


def _s5_input_map(u_ref, bb_ref, sr_sc, si_sc, l):
    for c in range(l // S5MC):
        rows = slice(c * S5MC, (c + 1) * S5MC)
        u = u_ref[0, rows, :]
        sr_sc[rows, :] = _dot(u, bb_ref[0, 0])
        si_sc[rows, :] = _dot(u, bb_ref[1, 0])


def _s5_forward_scan(sr_sc, si_sc, t8_ref, l):
    def step(k, carry):
        rows = pl.ds(pl.multiple_of(k * 8, 8), 8)
        sr, si, cr, ci = _scan_tile(sr_sc[rows, :], si_sc[rows, :], carry[0], carry[1], t8_ref, False)
        sr_sc[rows, :] = sr
        si_sc[rows, :] = si
        return cr, ci

    zero = jnp.zeros((8, SW), f32)
    lax.fori_loop(0, l // 8, step, (zero, zero), unroll=4)


def s5_fwd(proj3, t8, bb, cb, dskip):
    b, l, _ = proj3.shape

    def body(u_ref, t8_ref, bb_ref, cb_ref, d_ref, y_ref, sr_out, si_out):
        sr_sc, si_sc = sr_out.at[0], si_out.at[0]
        _s5_input_map(u_ref, bb_ref, sr_sc, si_sc, l)
        _s5_forward_scan(sr_sc, si_sc, t8_ref, l)
        for c in range(l // S5MC):
            rows = slice(c * S5MC, (c + 1) * S5MC)
            y = (_dot_nt(sr_sc[rows, :].astype(bf16), cb_ref[0, 0])
                 - _dot_nt(si_sc[rows, :].astype(bf16), cb_ref[1, 0]))
            y_ref[0, rows, :] = y + d_ref[...] * u_ref[0, rows, :].astype(f32)

    return pl.pallas_call(
        body, name="s5_fwd", grid=(SJ, b),
        in_specs=[pl.BlockSpec((1, l, SU), lambda j, bi: (bi, 0, j)),
                  pl.BlockSpec((2, 8, 8, SW), lambda j, bi: (0, 0, 0, j)),
                  pl.BlockSpec((2, 1, SU, SW), lambda j, bi: (0, j, 0, 0)),
                  pl.BlockSpec((2, 1, SU, SW), lambda j, bi: (0, j, 0, 0)),
                  pl.BlockSpec((1, SU), lambda j, bi: (0, j))],
        out_specs=[pl.BlockSpec((1, l, SU), lambda j, bi: (bi, 0, j)),
                   pl.BlockSpec((1, l, SW), lambda j, bi: (bi, 0, j)),
                   pl.BlockSpec((1, l, SW), lambda j, bi: (bi, 0, j))],
        out_shape=[jax.ShapeDtypeStruct((b, l, BW), f32), jax.ShapeDtypeStruct((b, l, NSTATE), f32),
                   jax.ShapeDtypeStruct((b, l, NSTATE), f32)],
        compiler_params=_cp(2))(proj3, t8, bb, cb, dskip)


def s5_bwd(proj3, dy, s_re, s_im, t8, bb, cb, dskip):
    b, l, _ = proj3.shape
    nt = l // 8

    def body(u_ref, dy_ref, sr_in, si_in, t8_ref, bb_ref, cb_ref, d_ref,
             du_ref, da_ref, dbb_ref, dcb_ref, dd_ref, gr_sc, gi_sc):
        bi = pl.program_id(1)
        sr_sc, si_sc = sr_in.at[0], si_in.at[0]

        @pl.when(bi == 0)
        def _():
            da_ref[...] = jnp.zeros_like(da_ref)
            dbb_ref[...] = jnp.zeros_like(dbb_ref)
            dcb_ref[...] = jnp.zeros_like(dcb_ref)
            dd_ref[...] = jnp.zeros_like(dd_ref)

        for c in range(l // S5MC):
            rows = slice(c * S5MC, (c + 1) * S5MC)
            dyb = dy_ref[0, rows, :].astype(bf16)
            gr_sc[rows, :] = _dot(dyb, cb_ref[0, 0])
            gi_sc[rows, :] = -_dot(dyb, cb_ref[1, 0])

        row = lax.broadcasted_iota(jnp.int32, (8, SW), 0)

        def step(i, carry):
            cr, ci, dar, dai = carry
            k = nt - 1 - i
            rows = pl.ds(pl.multiple_of(k * 8, 8), 8)
            gr, gi, cr, ci = _scan_tile(gr_sc[rows, :], gi_sc[rows, :], cr, ci, t8_ref, True)
            gr_sc[rows, :] = gr
            gi_sc[rows, :] = gi
            before = pl.ds(pl.multiple_of(jnp.maximum(k - 1, 0) * 8, 8), 8)
            live = jnp.where(k > 0, 1.0, 0.0)
            sr, si = sr_sc[rows, :], si_sc[rows, :]
            spr = jnp.where(row == 0, live * sr_sc[before, :][7:8, :], pltpu.roll(sr, 1, 0))
            spi = jnp.where(row == 0, live * si_sc[before, :][7:8, :], pltpu.roll(si, 1, 0))
            return cr, ci, dar + spr * gr + spi * gi, dai + spr * gi - spi * gr

        zero = jnp.zeros((8, SW), f32)
        _, _, dar, dai = lax.fori_loop(0, nt, step, (zero, zero, zero, zero), unroll=2)
        da_ref[0:1, :] += jnp.sum(dar, axis=0, keepdims=True)
        da_ref[1:2, :] += jnp.sum(dai, axis=0, keepdims=True)

        for c in range(l // S5MC):
            rows = slice(c * S5MC, (c + 1) * S5MC)
            u = u_ref[0, rows, :]
            dyv = dy_ref[0, rows, :]
            dyb = dyv.astype(bf16)
            grb, gib = gr_sc[rows, :].astype(bf16), gi_sc[rows, :].astype(bf16)
            dcb_ref[0, 0] += _dot_tn(dyb, sr_sc[rows, :].astype(bf16))
            dcb_ref[1, 0] -= _dot_tn(dyb, si_sc[rows, :].astype(bf16))
            dbb_ref[0, 0] += _dot_tn(u, grb)
            dbb_ref[1, 0] += _dot_tn(u, gib)
            du = _dot_nt(grb, bb_ref[0, 0]) + _dot_nt(gib, bb_ref[1, 0]) + d_ref[...] * dyv
            du_ref[0, rows, :] = du.astype(bf16)
            dd_ref[...] += jnp.sum(dyv * u.astype(f32), axis=0, keepdims=True)

    seq = pl.BlockSpec((1, l, SU), lambda j, bi: (bi, 0, j))
    sts = pl.BlockSpec((1, l, SW), lambda j, bi: (bi, 0, j))
    tab = pl.BlockSpec((2, 1, SU, SW), lambda j, bi: (0, j, 0, 0))
    return pl.pallas_call(
        body, name="s5_bwd", grid=(SJ, b),
        in_specs=[seq, seq, sts, sts, pl.BlockSpec((2, 8, 8, SW), lambda j, bi: (0, 0, 0, j)), tab, tab,
                  pl.BlockSpec((1, SU), lambda j, bi: (0, j))],
        out_specs=[seq, pl.BlockSpec((2, SW), lambda j, bi: (0, j)), tab, tab,
                   pl.BlockSpec((1, SU), lambda j, bi: (0, j))],
        out_shape=[jax.ShapeDtypeStruct((b, l, BW), bf16), jax.ShapeDtypeStruct((2, NSTATE), f32),
                   jax.ShapeDtypeStruct((2, SJ, SU, SW), f32), jax.ShapeDtypeStruct((2, SJ, SU, SW), f32),
                   jax.ShapeDtypeStruct((1, BW), f32)],
        scratch_shapes=[pltpu.VMEM((l, SW), f32)] * 2,
        compiler_params=_cp(2))(proj3, dy, s_re, s_im, t8, bb, cb, dskip)


AHC = 2
AHW = AHC * 128


def _att_mask(n, nb):
    if nb == 1:
        qi = lax.broadcasted_iota(jnp.int32, (ABLK, ABLK), 0)
        kj = lax.broadcasted_iota(jnp.int32, (ABLK, ABLK), 1)
        return kj <= qi
    qi = lax.broadcasted_iota(jnp.int32, (ABLK, 2 * ABLK), 0)
    kj = lax.broadcasted_iota(jnp.int32, (ABLK, 2 * ABLK), 1)
    return (kj >= qi) & (kj <= qi + ABLK) & ((n > 0) | (kj >= ABLK))


def _att_rows(it, nb, dil):
    r, n = it // nb, it % nb
    cur = pl.ds(r + n * (ABLK * dil), ABLK, stride=dil)
    prv = pl.ds(r + jnp.maximum(n - 1, 0) * (ABLK * dil), ABLK, stride=dil)
    return n, cur, prv


def _att_keys(ref, c, cur, prv, nb):
    if nb == 1:
        x = ref[c, cur, :].astype(bf16)
    else:
        x = jnp.concatenate([ref[c, prv, :], ref[c, cur, :]], axis=0).astype(bf16)
    head0 = lax.broadcasted_iota(jnp.int32, x.shape, 1) < HD
    zero = jnp.zeros_like(x)
    return jnp.concatenate([jnp.where(head0, x, zero), jnp.where(head0, zero, x)], axis=0)


def _per_head(nk, a0, a1):
    col = lax.broadcasted_iota(jnp.int32, (ABLK, 2 * nk), 1)
    return jnp.where(col < nk, a0, a1)


def _to_chunks(src_ref, dst):
    for c in range(AHC):
        dst[c] = src_ref[0, :, c * 128:(c + 1) * 128].astype(f32)


def att_fwd(proj3, g_idx, dil):
    b, l, _ = proj3.shape
    nb = l // dil // ABLK
    nhalf = BW // AHW

    def body(q_ref, k_ref, v_ref, o_ref, lse_ref, qf, kf, vf, of):
        hh = pl.program_id(1)
        _to_chunks(q_ref, qf)
        _to_chunks(k_ref, kf)
        _to_chunks(v_ref, vf)
        lane = lax.broadcasted_iota(jnp.int32, (ABLK, 128), 1)

        def step(it, carry):
            n, cur, prv = _att_rows(it, nb, dil)
            valid = _att_mask(n, nb)
            valid = jnp.concatenate([valid, valid], axis=1)
            nk = valid.shape[1] // 2
            lse_all = jnp.zeros((ABLK, 128), f32)
            for c in range(AHC):
                q = (qf[c, cur, :] * ATT_SCALE).astype(bf16)
                k = _att_keys(kf, c, cur, prv, nb)
                v = _att_keys(vf, c, cur, prv, nb)
                s = jnp.where(valid, _dot_nt(q, k), NEG)
                m0 = jnp.max(s[:, :nk], axis=-1, keepdims=True)
                m1 = jnp.max(s[:, nk:], axis=-1, keepdims=True)
                p = jnp.exp(s - _per_head(nk, m0, m1))
                den0 = jnp.sum(p[:, :nk], axis=-1, keepdims=True)
                den1 = jnp.sum(p[:, nk:], axis=-1, keepdims=True)
                of[c, cur, :] = _dot(p.astype(bf16), v) * jnp.where(lane < HD, 1.0 / den0, 1.0 / den1)
                head = hh * (2 * AHC) + 2 * c
                lse_all = (lse_all + jnp.where(lane == head, m0 + jnp.log(den0), 0.0)
                           + jnp.where(lane == head + 1, m1 + jnp.log(den1), 0.0))

            lse_ref[0, 0, cur, :] = lse_all
            return carry

        lax.fori_loop(0, dil * nb, step, 0, unroll=4)
        for c in range(AHC):
            o_ref[0, :, c * 128:(c + 1) * 128] = of[c].astype(bf16)

    col = lambda c: pl.BlockSpec((1, l, AHW), lambda bi, hh: (bi, 0, c * nhalf + hh))
    return pl.pallas_call(
        body, name=f"att_fwd{g_idx}", grid=(b, nhalf),
        in_specs=[col(1 + g_idx), col(4), col(5)],
        out_specs=[pl.BlockSpec((1, l, AHW), lambda bi, hh: (bi, 0, hh)),
                   pl.BlockSpec((1, 1, l, 128), lambda bi, hh: (bi, hh, 0, 0))],
        out_shape=[jax.ShapeDtypeStruct((b, l, BW), bf16), jax.ShapeDtypeStruct((b, nhalf, l, 128), f32)],
        scratch_shapes=[pltpu.VMEM((AHC, l, 128), f32)] * 4,
        compiler_params=_cp(2))(proj3, proj3, proj3)


def att_bwd(proj3, do, lse_tot, delta, g_idx, dil):
    b, l, _ = proj3.shape
    nb = l // dil // ABLK
    nhalf = BW // AHW

    def body(q_ref, k_ref, v_ref, do_ref, l_ref, dl_ref, dq_out, dk_out, dv_out, qf, kf, vf, dof,
             dq_ref, dk_ref, dv_ref):
        hh = pl.program_id(1)
        _to_chunks(q_ref, qf)
        _to_chunks(k_ref, kf)
        _to_chunks(v_ref, vf)
        _to_chunks(do_ref, dof)
        dk_ref[...] = jnp.zeros_like(dk_ref)
        dv_ref[...] = jnp.zeros_like(dv_ref)
        lane = lax.broadcasted_iota(jnp.int32, (ABLK, 128), 1)

        def step(it, carry):
            n, cur, prv = _att_rows(it, nb, dil)
            valid = _att_mask(n, nb)
            valid = jnp.concatenate([valid, valid], axis=1)
            nk = valid.shape[1] // 2
            lse_b = l_ref[0, cur, :]
            dl_b = dl_ref[0, cur, :]
            head0 = lax.broadcasted_iota(jnp.int32, (nk, 128), 1) < HD
            for c in range(AHC):
                q = (qf[c, cur, :] * ATT_SCALE).astype(bf16)
                dob = dof[c, cur, :].astype(bf16)
                k = _att_keys(kf, c, cur, prv, nb)
                v = _att_keys(vf, c, cur, prv, nb)
                head = hh * (2 * AHC) + 2 * c
                pick = lambda a, h: jnp.sum(jnp.where(lane == h, a, 0.0), axis=-1, keepdims=True)
                lse_h = _per_head(nk, pick(lse_b, head), pick(lse_b, head + 1))
                dl_h = _per_head(nk, pick(dl_b, head), pick(dl_b, head + 1))
                s = _dot_nt(q, k)
                p = jnp.where(valid, jnp.exp(jnp.minimum(s - lse_h, 60.0)), 0.0)
                ds = (p * (_dot_nt(dob, v) - dl_h)).astype(bf16)
                dq_ref[0, c, cur, :] = _dot(ds, k) * ATT_SCALE
                dk2 = _dot_tn(ds, q)
                dv2 = _dot_tn(p.astype(bf16), dob)
                dk = jnp.where(head0, dk2[:nk], dk2[nk:])
                dv = jnp.where(head0, dv2[:nk], dv2[nk:])
                if nb == 1:
                    dk_ref[0, c, cur, :] += dk
                    dv_ref[0, c, cur, :] += dv
                else:
                    dk_ref[0, c, cur, :] += dk[ABLK:]
                    dv_ref[0, c, cur, :] += dv[ABLK:]
                    dk_ref[0, c, prv, :] += dk[:ABLK]
                    dv_ref[0, c, prv, :] += dv[:ABLK]

            return carry

        lax.fori_loop(0, dil * nb, step, 0, unroll=4)
        dq_out[0] = dq_ref[0].astype(bf16)
        dk_out[0] = dk_ref[0].astype(bf16)
        dv_out[0] = dv_ref[0].astype(bf16)

    col = lambda c: pl.BlockSpec((1, l, AHW), lambda bi, hh: (bi, 0, c * nhalf + hh))
    own = pl.BlockSpec((1, l, AHW), lambda bi, hh: (bi, 0, hh))
    own128 = pl.BlockSpec((1, l, 128), lambda bi, hh: (bi, 0, 0))
    chunked = pl.BlockSpec((1, AHC, l, 128), lambda bi, hh: (bi, hh, 0, 0))
    return pl.pallas_call(
        body, name=f"att_bwd{g_idx}", grid=(b, nhalf),
        in_specs=[col(1 + g_idx), col(4), col(5), own, own128, own128],
        out_specs=[chunked] * 3,
        out_shape=[jax.ShapeDtypeStruct((b, BW // 128, l, 128), bf16)] * 3,
        scratch_shapes=[pltpu.VMEM((AHC, l, 128), f32)] * 4 + [pltpu.VMEM((1, AHC, l, 128), f32)] * 3,
        compiler_params=_cp(2, 56))(proj3, proj3, proj3, do, lse_tot, delta)


CPAD = 32
CTAIL = 16
CR = 128
CSLAB = CR + 40


def _tap_windows(slab, off, mis):
    ntap = (CW - 1 - mis) // 8 + 1
    xb = slab[off + mis:off + mis + CR + 8 * (ntap - 1)]
    for a in range(ntap):
        yield 8 * a + mis, xb[8 * a:8 * a + CR]


def _fill_glu(cv_ref, pad, l):
    pad[0:CPAD, :] = jnp.zeros((CPAD, BW), f32)
    pad[CPAD:CPAD + l, :] = cv_ref[0, :, :BW].astype(f32) * _sigmoid(cv_ref[0, :, BW:].astype(f32))
    pad[CPAD + l:, :] = jnp.zeros((CTAIL, BW), f32)


def conv_fwd(proj3, cw, cb):
    b, l, _ = proj3.shape

    def body(cv_ref, w_ref, b_ref, o_ref, pad):
        _fill_glu(cv_ref, pad, l)
        for lc in range(BW // 128):
            lanes = slice(lc * 128, (lc + 1) * 128)
            wv = w_ref[:, lanes]

            def step(c, carry):
                base = pl.multiple_of(c * CR, CR)
                slab = pad[pl.ds(base, CSLAB), lanes]
                acc = jnp.zeros((CR, 128), f32) + b_ref[:, lanes]
                for mis in range(8):
                    for k, win in _tap_windows(slab, CPAD - (CW - 1), mis):
                        acc = acc + wv[k:k + 1] * win
                o_ref[0, pl.ds(base, CR), lanes] = acc
                return carry

            lax.fori_loop(0, l // CR, step, 0)

    return pl.pallas_call(
        body, name="conv_fwd", grid=(b,),
        in_specs=[pl.BlockSpec((1, l, 2 * BW), lambda i: (i, 0, 3)),
                  pl.BlockSpec((32, BW), lambda i: (0, 0)), pl.BlockSpec((1, BW), lambda i: (0, 0))],
        out_specs=pl.BlockSpec((1, l, BW), lambda i: (i, 0, 0)),
        out_shape=jax.ShapeDtypeStruct((b, l, BW), f32),
        scratch_shapes=[pltpu.VMEM((CPAD + l + CTAIL, BW), f32)], compiler_params=_cp(1))(proj3, cw, cb)


def conv_bwd(proj3, dhc, cw):
    b, l, _ = proj3.shape

    def body(cv_ref, d_ref, w_ref, dcv_ref, dw_ref, db_ref, pad, dpad):
        i = pl.program_id(0)

        @pl.when(i == 0)
        def _():
            dw_ref[...] = jnp.zeros_like(dw_ref)
            db_ref[...] = jnp.zeros_like(db_ref)

        _fill_glu(cv_ref, pad, l)
        dpad[0:l, :] = d_ref[0]
        dpad[l:, :] = jnp.zeros((CPAD + CTAIL, BW), f32)
        db_ref[...] += jnp.sum(d_ref[0], axis=0, keepdims=True)
        for lc in range(BW // 128):
            lanes = slice(lc * 128, (lc + 1) * 128)
            glanes = slice(BW + lc * 128, BW + (lc + 1) * 128)
            wv = w_ref[:, lanes]

            for mis in range(8):
                ntap = (CW - 1 - mis) // 8 + 1

                def dw_step(c, accs, mis=mis, lanes=lanes):
                    base = pl.multiple_of(c * CR, CR)
                    slab = pad[pl.ds(base, CSLAB), lanes]
                    dv = dpad[pl.ds(base, CR), lanes]
                    return tuple(acc + (dv * win).reshape(CR // 8, 8, 128).sum(axis=0) for acc, (_, win)
                                 in zip(accs, _tap_windows(slab, CPAD - (CW - 1), mis)))

                accs = lax.fori_loop(0, l // CR, dw_step, tuple(jnp.zeros((8, 128), f32) for _ in range(ntap)))
                for a in range(ntap):
                    k = 8 * a + mis
                    dw_ref[k:k + 1, lanes] += jnp.sum(accs[a], axis=0, keepdims=True)

            def dh_step(c, carry, lanes=lanes, glanes=glanes, wv=wv):
                base = pl.multiple_of(c * CR, CR)
                slab = dpad[pl.ds(base, CSLAB), lanes]
                acc = jnp.zeros((CR, 128), f32)
                for mis in range(8):
                    for kk, win in _tap_windows(slab, 0, mis):
                        acc = acc + wv[CW - 1 - kk:CW - kk] * win
                rows = pl.ds(base, CR)
                a = cv_ref[0, rows, lanes].astype(f32)
                sg = _sigmoid(cv_ref[0, rows, glanes].astype(f32))
                dcv_ref[0, rows, lanes] = (acc * sg).astype(bf16)
                dcv_ref[0, rows, glanes] = (acc * a * sg * (1.0 - sg)).astype(bf16)
                return carry

            lax.fori_loop(0, l // CR, dh_step, 0)

    return pl.pallas_call(
        body, name="conv_bwd", grid=(b,),
        in_specs=[pl.BlockSpec((1, l, 2 * BW), lambda i: (i, 0, 3)),
                  pl.BlockSpec((1, l, BW), lambda i: (i, 0, 0)),
                  pl.BlockSpec((32, BW), lambda i: (0, 0))],
        out_specs=[pl.BlockSpec((1, l, 2 * BW), lambda i: (i, 0, 0)),
                   pl.BlockSpec((32, BW), lambda i: (0, 0)), pl.BlockSpec((1, BW), lambda i: (0, 0))],
        out_shape=[jax.ShapeDtypeStruct((b, l, 2 * BW), bf16), jax.ShapeDtypeStruct((32, BW), f32),
                   jax.ShapeDtypeStruct((1, BW), f32)],
        scratch_shapes=[pltpu.VMEM((CPAD + l + CTAIL, BW), f32), pltpu.VMEM((l + CPAD + CTAIL, BW), f32)],
        compiler_params=_cp(1))(proj3, dhc, cw)


def _head_expand():
    r = lax.broadcasted_iota(jnp.int32, (128, BW), 0)
    c = lax.broadcasted_iota(jnp.int32, (128, BW), 1) // HD
    return (r == c).astype(f32)


def _head_reduce():
    r = lax.broadcasted_iota(jnp.int32, (BW, 128), 0) // HD
    c = lax.broadcasted_iota(jnp.int32, (BW, 128), 1)
    return (r == c).astype(f32)


def _merge_common(ys_ref, o_refs, l_refs, hc_ref, g_refs, bg_ref, lng_ref, lnb_ref, wglu_ref, watt_ref, wpw_ref):
    r = {}
    ysv = ys_ref[...]
    r["ys"] = ysv
    r["ysin"] = _gelu(ysv).astype(bf16)
    z = _dot(r["ysin"], wglu_ref[...])
    r["z1"], r["sg2"] = z[:, :D], _sigmoid(z[:, D:])
    r["y_s"] = r["z1"] * r["sg2"]
    ls = [lr_[0, 0] + lr_[0, 1] for lr_ in l_refs]
    mx = jnp.maximum(jnp.maximum(ls[0], ls[1]), ls[2])
    es = [jnp.exp(v - mx) for v in ls]
    tot = es[0] + es[1] + es[2]
    r["lse_tot"] = mx + jnp.log(tot)
    e_mat = _head_expand()
    o = jnp.zeros(ysv.shape, f32)
    for e, o_ref in zip(es, o_refs):
        o = o + _dot_hi(e / tot, e_mat) * o_ref[...].astype(f32)
    r["o"] = o
    r["ob"] = o.astype(bf16)
    r["y_a"] = _dot(r["ob"], watt_ref[...])
    hc = hc_ref[...]
    mu = jnp.mean(hc, axis=-1, keepdims=True)
    xc = hc - mu
    rstd = lax.rsqrt(jnp.mean(xc * xc, axis=-1, keepdims=True) + EPS)
    r["xh"], r["rstd"] = xc * rstd, rstd
    hn = r["xh"] * lng_ref[...] + lnb_ref[...]
    r["hn"] = hn
    r["sgn"] = _sigmoid(hn)
    r["hs"] = (hn * r["sgn"]).astype(bf16)
    r["y_c"] = _dot(r["hs"], wpw_ref[...])
    r["gates"] = [_sigmoid(g_refs[k][...].astype(f32) + bg_ref[:, k * D:(k + 1) * D]) for k in range(3)]
    r["merged"] = r["gates"][0] * r["y_s"] + r["gates"][1] * r["y_a"] + r["gates"][2] * r["y_c"]
    return r


TBM = 256


def _merge_in_specs(tok, tb, lses):
    w = lambda shape: pl.BlockSpec(shape, lambda i: (0, 0))
    nbl = lses[0].shape[2] // tb
    return ([pl.BlockSpec((tb, D), tok), pl.BlockSpec((tb, BW), tok)]
            + [pl.BlockSpec((tb, BW), tok)] * 3
            + [pl.BlockSpec((1, 2, tb, 128), lambda i: (i // nbl, 0, i % nbl, 0))] * 3
            + [pl.BlockSpec((tb, BW), tok)]
            + [pl.BlockSpec((tb, D), lambda i, k=k: (i, 4 + k)) for k in range(3)]
            + [w((1, 3 * D)), w((1, BW)), w((1, BW)), w((BW, 2 * D)), w((BW, D)), w((BW, D)), w((D, D))])


def merge_fwd(x, ys, os_, lses, hc, proj, bg, lng, lnb, wglu, watt, wpw, wout):
    n = x.shape[0]

    def body(x_ref, ys_ref, o1, o2, o3, l1, l2, l3, hc_ref, g0, g1, g2, bg_ref, lng_ref, lnb_ref,
             wglu_ref, watt_ref, wpw_ref, wout_ref, x1_ref):
        r = _merge_common(ys_ref, (o1, o2, o3), (l1, l2, l3), hc_ref, (g0, g1, g2), bg_ref, lng_ref, lnb_ref,
                          wglu_ref, watt_ref, wpw_ref)
        x1_ref[...] = x_ref[...] + _dot(r["merged"].astype(bf16), wout_ref[...])

    tok = lambda i: (i, 0)
    return pl.pallas_call(
        body, name="merge_fwd", grid=(n // TB,), in_specs=_merge_in_specs(tok, TB, lses),
        out_specs=pl.BlockSpec((TB, D), tok), out_shape=jax.ShapeDtypeStruct((n, D), f32),
        compiler_params=_cp(1, 56))(x, ys, *os_, *lses, hc, proj, proj, proj, bg, lng, lnb, wglu, watt, wpw, wout)


def merge_bwd(dx1, ys, os_, lses, hc, proj, bg, lng, lnb, wglu, watt, wpw, wout):
    n = dx1.shape[0]

    def body(dx_ref, ys_ref, o1, o2, o3, l1, l2, l3, hc_ref, g0, g1, g2, bg_ref, lng_ref, lnb_ref,
             wglu_ref, watt_ref, wpw_ref, wout_ref,
             dys_ref, do_ref, delta_ref, ltot_ref, dhc_ref, dgate_ref, ysin_ref, dz_ref, ob_ref, dya_ref,
             hs_ref, dyc_ref, mg_ref, dbg_ref, dlng_ref, dlnb_ref):
        i = pl.program_id(0)

        @pl.when(i == 0)
        def _():
            dbg_ref[...] = jnp.zeros_like(dbg_ref)
            dlng_ref[...] = jnp.zeros_like(dlng_ref)
            dlnb_ref[...] = jnp.zeros_like(dlnb_ref)

        r = _merge_common(ys_ref, (o1, o2, o3), (l1, l2, l3), hc_ref, (g0, g1, g2), bg_ref, lng_ref, lnb_ref,
                          wglu_ref, watt_ref, wpw_ref)
        mg_ref[...] = r["merged"].astype(bf16)
        ysin_ref[...] = r["ysin"]
        ob_ref[...] = r["ob"]
        hs_ref[...] = r["hs"]
        ltot_ref[...] = r["lse_tot"]
        dm = _dot_nt(dx_ref[...].astype(bf16), wout_ref[...])
        ys3 = (r["y_s"], r["y_a"], r["y_c"])
        for k in range(3):
            gk = r["gates"][k]
            dgr = dm * ys3[k] * gk * (1.0 - gk)
            dgate_ref[:, k * D:(k + 1) * D] = dgr.astype(bf16)
            dbg_ref[:, k * D:(k + 1) * D] += jnp.sum(dgr, axis=0, keepdims=True)
        dy_s = dm * r["gates"][0]
        sg2 = r["sg2"]
        dz = jnp.concatenate([dy_s * sg2, dy_s * r["z1"] * sg2 * (1.0 - sg2)], axis=1).astype(bf16)
        dz_ref[...] = dz
        dys_ref[...] = _dot_nt(dz, wglu_ref[...]) * _gelu_grad(r["ys"])
        dya = (dm * r["gates"][1]).astype(bf16)
        dya_ref[...] = dya
        do = _dot_nt(dya, watt_ref[...])
        do_ref[...] = do.astype(bf16)
        delta_ref[...] = _dot_hi(do * r["o"], _head_reduce())
        dyc = (dm * r["gates"][2]).astype(bf16)
        dyc_ref[...] = dyc
        sgn, hn = r["sgn"], r["hn"]
        dhn = _dot_nt(dyc, wpw_ref[...]) * sgn * (1.0 + hn * (1.0 - sgn))
        dlng_ref[...] += jnp.sum(dhn * r["xh"], axis=0, keepdims=True)
        dlnb_ref[...] += jnp.sum(dhn, axis=0, keepdims=True)
        dxh = dhn * lng_ref[...]
        xh = r["xh"]
        dhc_ref[...] = r["rstd"] * (dxh - jnp.mean(dxh, axis=-1, keepdims=True)
                                    - xh * jnp.mean(dxh * xh, axis=-1, keepdims=True))

    tok = lambda i: (i, 0)
    fix = lambda i: (0, 0)
    outs = [("dys", BW, f32), ("do", BW, bf16), ("delta", 128, f32), ("lse_tot", 128, f32), ("dhc", BW, f32),
            ("dgate", 3 * D, bf16), ("ysin", BW, bf16), ("dz", 2 * D, bf16), ("ob", BW, bf16), ("dya", D, bf16),
            ("hs", BW, bf16), ("dyc", D, bf16), ("merged", D, bf16)]
    small = [("dbg", 3 * D), ("dlng", BW), ("dlnb", BW)]
    res = pl.pallas_call(
        body, name="merge_bwd", grid=(n // TBM,), in_specs=_merge_in_specs(tok, TBM, lses),
        out_specs=[pl.BlockSpec((TBM, w), tok) for _, w, _ in outs] + [pl.BlockSpec((1, w), fix) for _, w in small],
        out_shape=[jax.ShapeDtypeStruct((n, w), dt) for _, w, dt in outs]
        + [jax.ShapeDtypeStruct((1, w), f32) for _, w in small],
        compiler_params=_cp(1, 56))(dx1, ys, *os_, *lses, hc, proj, proj, proj, bg, lng, lnb, wglu, watt, wpw, wout)
    return dict(zip([k for k, _, _ in outs] + [k for k, _ in small], res))


def assemble_dproj(du, dqs, dks, dvs, dcv, dgate):
    b, l, _ = du.shape
    nck = BW // 128

    def body(du_ref, q1, q2, q3, k1, k2, k3, v1, v2, v3, cv_ref, g_ref, o_ref):
        o_ref[0, :, 0:BW] = du_ref[0]
        for c in range(nck):
            for j, qr in enumerate((q1, q2, q3)):
                o_ref[0, :, (1 + j) * BW + c * 128:(1 + j) * BW + (c + 1) * 128] = qr[0, c]
            add3 = lambda r1, r2, r3: (r1[0, c].astype(f32) + r2[0, c].astype(f32) + r3[0, c].astype(f32)).astype(bf16)
            o_ref[0, :, 4 * BW + c * 128:4 * BW + (c + 1) * 128] = add3(k1, k2, k3)
            o_ref[0, :, 5 * BW + c * 128:5 * BW + (c + 1) * 128] = add3(v1, v2, v3)
        o_ref[0, :, 6 * BW:8 * BW] = cv_ref[0]
        o_ref[0, :, 8 * BW:] = g_ref[0]

    t = lambda w: pl.BlockSpec((1, TB, w), lambda bi, i: (bi, i, 0))
    ck = pl.BlockSpec((1, nck, TB, 128), lambda bi, i: (bi, 0, i, 0))
    return pl.pallas_call(
        body, name="assemble_dproj", grid=(b, l // TB),
        in_specs=[t(BW)] + [ck] * 9 + [t(2 * BW), t(3 * D)], out_specs=t(INC),
        out_shape=jax.ShapeDtypeStruct((b, l, INC), bf16), compiler_params=_cp(2))(du, *dqs, *dks, *dvs, dcv, dgate)


def _me():
    return lax.axis_index("x"), lax.axis_index("y"), lax.axis_index("c")


def _peers():
    x, y, c = _me()
    return [(x, y, 1 - c), (1 - x, y, c), (1 - x, y, 1 - c), (x, 1 - y, c), (x, 1 - y, 1 - c),
            (1 - x, 1 - y, c), (1 - x, 1 - y, 1 - c)]


def _rank(p):
    return 4 * p[0] + 2 * p[1] + p[2]


def allgather(arrs, name):
    na = len(arrs)
    units = [(a, j) for a in range(na) for j in range(arrs[a].shape[0])]
    nu = len(units)

    def body(*refs):
        ins, outs = refs[:na], refs[na:2 * na]
        send, recv, loc = refs[2 * na:]
        me = _rank(_me())
        local, remote = [], []
        for u, (a, j) in enumerate(units):
            own = pltpu.make_async_copy(ins[a].at[j], outs[a].at[j, me], loc.at[u])
            own.start()
            local.append(own)
        for u, (a, j) in enumerate(units):
            for k, p in enumerate(_peers()):
                cp = pltpu.make_async_remote_copy(src_ref=ins[a].at[j], dst_ref=outs[a].at[j, me],
                                                  send_sem=send.at[u, k], recv_sem=recv.at[u, k],
                                                  device_id=p, device_id_type=MESH)
                cp.start()
                remote.append(cp)
        for cp in local:
            cp.wait()
        for cp in remote:
            cp.wait()

    return pl.pallas_call(
        body, name=name, in_specs=[ANY] * na, out_specs=[ANY] * na,
        out_shape=[jax.ShapeDtypeStruct((a.shape[0], NDEV) + a.shape[1:], a.dtype) for a in arrs],
        scratch_shapes=[pltpu.SemaphoreType.DMA((nu, NDEV - 1)), pltpu.SemaphoreType.DMA((nu, NDEV - 1)),
                        pltpu.SemaphoreType.DMA((nu,))])(*arrs)


HBM = pl.BlockSpec(memory_space=pltpu.HBM)
SEM = pl.BlockSpec(memory_space=pltpu.SEMAPHORE)
_EFFECT = pltpu.SideEffectType.DATAFLOW_SIDE_EFFECTING


def _push_copies(srcs, lands, send, recv, scatter):
    me = _rank(_me())
    out = []
    for i in range(len(srcs)):
        for k, p in enumerate(_peers()):
            src = srcs[i].at[_rank(p)] if scatter else srcs[i]
            dst = lands[i].at[k] if scatter else lands[i].at[me]
            j = i * (NDEV - 1) + k
            out.append(pltpu.make_async_remote_copy(src_ref=src, dst_ref=dst, send_sem=send.at[j],
                                                    recv_sem=recv.at[j], device_id=p, device_id_type=MESH))
    return out


def push_start(srcs, lands, scatter, name, token):
    n = len(srcs)
    token = jnp.zeros((8, 128), f32) if token is None else token

    def body(*refs):
        for cp in _push_copies(refs[:n], refs[n:2 * n], refs[2 * n + 1], refs[2 * n + 2], scatter):
            cp.start()
        refs[-1][...] = refs[2 * n][...]

    sems = pltpu.SemaphoreType.DMA((n * (NDEV - 1),))
    vmem = pl.BlockSpec(memory_space=pltpu.VMEM)
    res = pl.pallas_call(
        body, name=name, in_specs=[HBM] * (2 * n) + [vmem], out_specs=[SEM, SEM] + [HBM] * (2 * n) + [vmem],
        out_shape=[sems, sems] + [pltpu.HBM(a.shape, a.dtype) for a in list(srcs) + list(lands)]
        + [jax.ShapeDtypeStruct((8, 128), f32)],
        input_output_aliases={i: 2 + i for i in range(2 * n)},
        compiler_params=pltpu.CompilerParams(has_side_effects=_EFFECT),
    )(*[pltpu.with_memory_space_constraint(a, pltpu.HBM) for a in list(srcs) + list(lands)], token)
    return res[0], res[1], res[2:2 + n], res[2 + n:2 + 2 * n], res[-1]


def push_wait(send, recv, srcs, lands, after, scatter, name):
    n = len(srcs)

    def body(*refs):
        for cp in _push_copies(refs[:n], refs[n:2 * n], refs[2 * n], refs[2 * n + 1], scatter):
            cp.wait_send()
            cp.wait_recv()

    res = pl.pallas_call(
        body, name=name, in_specs=[HBM] * (2 * n) + [SEM, SEM, ANY], out_specs=[HBM] * (2 * n),
        out_shape=[pltpu.HBM(a.shape, a.dtype) for a in list(srcs) + list(lands)],
        input_output_aliases={i: i for i in range(2 * n)},
        compiler_params=pltpu.CompilerParams(has_side_effects=_EFFECT),
    )(*srcs, *lands, send, recv, after)
    return res[:n], res[n:]


_C1 = 1.0 / (1.0 - ADAM_B1 ** ADAM_STEP)
_C2 = 1.0 / (1.0 - ADAM_B2 ** ADAM_STEP)


def _adamw(w, g, m, v):
    m = ADAM_B1 * m + (1.0 - ADAM_B1) * g
    v = ADAM_B2 * v + (1.0 - ADAM_B2) * (g * g)
    delta = -ADAM_LR * ((m * _C1) / (jnp.sqrt(v * _C2) + ADAM_EPS) + ADAM_WD * w)
    return delta, m, v


def adam_big(lands, owns, w, m, v, name):
    _, k, n = lands[0].shape
    tk = k
    while tk * n * 2 * NDEV > 2 * 1024 * 1024 and tk % 16 == 0:
        tk //= 2

    def body(*refs):
        l_refs, o_refs = refs[:DEPTH], refs[DEPTH:2 * DEPTH]
        w_ref, m_ref, v_ref, g_ref, d_ref, nm_ref, nv_ref = refs[2 * DEPTH:]
        for l in range(DEPTH):
            g = o_refs[l][...].astype(f32)
            for s in range(NDEV - 1):
                g = g + l_refs[l][s].astype(f32)
            d, nm, nv = _adamw(w_ref[l], g, m_ref[l], v_ref[l])
            g_ref[l], d_ref[l], nm_ref[l], nv_ref[l] = g, d, nm, nv

    blk = pl.BlockSpec((DEPTH, tk, n), lambda i: (0, i, 0))
    return pl.pallas_call(
        body, name=name, grid=(k // tk,),
        in_specs=[pl.BlockSpec((NDEV - 1, tk, n), lambda i: (0, i, 0))] * DEPTH
        + [pl.BlockSpec((tk, n), lambda i: (i, 0))] * DEPTH + [blk, blk, blk],
        out_specs=[blk] * 4, out_shape=[jax.ShapeDtypeStruct(w.shape, f32)] * 4,
        compiler_params=_cp(1))(*lands, *owns, w, m, v)


def adam_small(gath, w, m, v):
    r = w.shape[0]
    tr = 512

    def body(g_ref, w_ref, m_ref, v_ref, go_ref, d_ref, nm_ref, nv_ref):
        g = g_ref[0]
        for s in range(1, NDEV):
            g = g + g_ref[s]
        d, nm, nv = _adamw(w_ref[...], g, m_ref[...], v_ref[...])
        go_ref[...], d_ref[...], nm_ref[...], nv_ref[...] = g, d, nm, nv

    blk = pl.BlockSpec((tr, 128), lambda i: (i, 0))
    return pl.pallas_call(
        body, name="adam_small", grid=(r // tr,),
        in_specs=[pl.BlockSpec((NDEV, tr, 128), lambda i: (0, i, 0)), blk, blk, blk],
        out_specs=[blk] * 4, out_shape=[jax.ShapeDtypeStruct((r, 128), f32)] * 4,
        compiler_params=_cp(1))(gath, w, m, v)


SMALL = ["norm1_g", "b_gate", "ssm_lambda_re", "ssm_lambda_im", "ssm_log_dt", "ssm_b_re", "ssm_b_im",
         "ssm_c_re", "ssm_c_im", "ssm_d", "conv_w", "conv_b", "conv_ln_g", "conv_ln_b", "norm2_g", "final_g"]
BIG = ["w_in", "w_ssm_glu", "w_att_up", "w_conv_pw2", "w_out", "w_ffn_in", "w_ffn_out"]
ORDER = ["norm1_g", "w_in", "b_gate", "ssm_lambda_re", "ssm_lambda_im", "ssm_log_dt", "ssm_b_re", "ssm_b_im",
         "ssm_c_re", "ssm_c_im", "ssm_d", "w_ssm_glu", "w_att_up", "conv_w", "conv_b", "conv_ln_g", "conv_ln_b",
         "w_conv_pw2", "w_out", "norm2_g", "w_ffn_in", "w_ffn_out", "final_g"]
PACK_ROWS = 2560


def _pack(arrs):
    flat = jnp.concatenate([a.reshape(-1).astype(f32) for a in arrs])
    return jnp.pad(flat, (0, PACK_ROWS * 128 - flat.shape[0])).reshape(PACK_ROWS, 128)


def _unpack(pack, shapes):
    flat = pack.reshape(-1)
    out, off = [], 0
    for s in shapes:
        sz = math.prod(s)
        out.append(flat[off:off + sz].reshape(s))
        off += sz
    return out


def _bt(b):
    return b.transpose(2, 0, 1).reshape(GH, NSTATE)


def _bt_inv(bt):
    return bt.reshape(GH, NG, NS).transpose(1, 2, 0)


def _ct(c):
    return c.transpose(1, 0, 2).reshape(GH, NSTATE)


def _ct_inv(ct):
    return ct.reshape(GH, NG, NS).transpose(1, 0, 2)


def local_step(x, loss_target, P, weights, on_grads, start_token=None):
    bsz, seq, _ = x.shape
    n = bsz * seq

    def natural(g3):
        return g3.transpose(1, 0, 2).reshape(g3.shape[1], NDEV * g3.shape[2])

    tokens = [] if start_token is None else [start_token]

    def after_pushes(a):
        while tokens:
            a = a + tokens.pop()[0:1, 0:1]
        return a

    def pushed(tok):
        if tok is not None:
            tokens.append(tok)

    xs = x.reshape(n, D)
    saved = []
    conv_w_pad = None
    for l in range(DEPTH):
        S = {"x": xs}
        h1 = rms_fwd(xs, after_pushes(P["norm1_g"][l][None]))
        G = dict(weights(l, "in", h1))
        if conv_w_pad is None:
            conv_w_full = G["conv_w"].transpose(1, 2, 0, 3).reshape(DEPTH, CW, BW)
            conv_w_pad = jnp.pad(conv_w_full, ((0, 0), (0, 1), (0, 0)))
        w_in4 = G["w_in"][None]
        proj = inproj(h1, w_in4, 0)
        proj3 = proj.reshape(bsz, seq, INC)
        lr = P["ssm_lambda_re"][l].reshape(1, NSTATE)
        li = P["ssm_lambda_im"][l].reshape(1, NSTATE)
        ld = jnp.repeat(P["ssm_log_dt"][l], NS).reshape(1, NSTATE)
        btr, bti = _bt(P["ssm_b_re"][l]), _bt(P["ssm_b_im"][l])
        t8, bb, cb = s5_params(lr, li, ld, btr, bti, _ct(P["ssm_c_re"][l]), _ct(P["ssm_c_im"][l]))
        dskip = P["ssm_d"][l][None]
        ys, s_re, s_im = s5_fwd(proj3, t8, bb, cb, dskip)
        att = [att_fwd(proj3, gi, dil) for gi, (_, dil) in enumerate(PATTERNS)]
        hc = conv_fwd(proj3, conv_w_pad[l], P["conv_b"][l][None])
        G.update(weights(l, "mix", hc))
        wts = dict(wglu=natural(G["w_ssm_glu"]), watt=natural(G["w_att_up"]),
                   wpw=natural(G["w_conv_pw2"]), wout=G["w_out"].reshape(D, D))
        mi = dict(ys=ys.reshape(n, BW), os_=[a[0].reshape(n, BW) for a in att],
                  lses=[a[1] for a in att], hc=hc.reshape(n, BW),
                  proj=proj, bg=P["b_gate"][l][None], lng=P["conv_ln_g"][l][None], lnb=P["conv_ln_b"][l][None],
                  **wts)
        x1 = merge_fwd(xs, **mi)
        G.update(weights(l, "ffn", x1))
        w_ffn = (G["w_ffn_in"][None], G["w_ffn_out"][None])
        x2, z1s, z2s = ffn_fwd(x1, P["norm2_g"][l][None], *w_ffn, 0)
        S.update(h1=h1, proj=proj, proj3=proj3, tabs=(s_re, s_im, t8, bb, cb), mi=mi, x1=x1, w_in4=w_in4, w_ffn=w_ffn,
                 zs=(z1s, z2s),
                 sp=(lr, li, ld, btr, bti), dskip=dskip)
        saved.append(S)
        xs = x2

    loss8, dx, dfinal = loss_head(xs, P["final_g"][None], loss_target.reshape(n, D))

    small_g = {k: [None] * DEPTH for k in SMALL if k != "final_g"}
    tokblk = lambda w: pl.BlockSpec((1024, w), lambda s, i: (i, 0))
    colblk = lambda w: pl.BlockSpec((1024, w), lambda s, i: (i, s))
    sh3blk = lambda w: pl.BlockSpec((1, 1024, w), lambda s, i: (s, i, 0))
    for l in reversed(range(DEPTH)):
        S = saved[l]
        g2 = P["norm2_g"][l][None]
        dx1, h2, dz1, dz2, a4, dg2 = ffn_bwd(S["x1"], after_pushes(g2), *S["w_ffn"], 0, dx, *S["zs"])
        small_g["norm2_g"][l] = dg2
        dwa = mm_tn(h2, dz1, tokblk(D), sh3blk(NSH_FF), 4, D, NSH_FF, n, "dw_ffn_in_a")
        dwb = mm_tn(h2, dz2, tokblk(D), sh3blk(NSH_FF), 4, D, NSH_FF, n, "dw_ffn_in_b")
        pushed(on_grads(l, "ffn", dict(
            w_ffn_in=jnp.concatenate([dwa, dwb], axis=0),
            w_ffn_out=mm_tn(a4, dx, sh3blk(NSH_FF), tokblk(D), 4, NSH_FF, D, n,
                            "dw_ffn_out").reshape(NDEV, NSH_FF // 2, D))))
        mb = merge_bwd(dx1, **dict(S["mi"], lng=after_pushes(S["mi"]["lng"])))
        small_g["b_gate"][l], small_g["conv_ln_g"][l], small_g["conv_ln_b"][l] = mb["dbg"], mb["dlng"], mb["dlnb"]
        dws = dw_mix(mb["ysin"], mb["dz"], mb["ob"], mb["dya"], mb["hs"], mb["dyc"], mb["merged"], dx1)
        pushed(on_grads(l, "mix", dict(zip(("w_ssm_glu", "w_att_up", "w_conv_pw2", "w_out"), dws))))
        dcv, dcw, dcb = conv_bwd(S["proj3"], mb["dhc"].reshape(bsz, seq, BW), after_pushes(conv_w_pad[l]))
        small_g["conv_w"][l] = dcw[:CW].reshape(CW, NDEV, BW // NDEV).transpose(1, 0, 2)
        small_g["conv_b"][l] = dcb
        ab = [att_bwd(S["proj3"], mb["do"].reshape(bsz, seq, BW), mb["lse_tot"].reshape(bsz, seq, 128),
                      mb["delta"].reshape(bsz, seq, 128), gi, dil) for gi, (_, dil) in enumerate(PATTERNS)]
        du, d_a, d_bb, d_cb, d_d = s5_bwd(S["proj3"], mb["dys"].reshape(bsz, seq, BW), *S["tabs"], S["dskip"])
        lr, li, ld, btr, bti = S["sp"]
        dlr, dli, dld, dbt, dct = s5_params_bwd(lr, li, ld, btr, bti, d_a, d_bb, d_cb)
        small_g["ssm_lambda_re"][l], small_g["ssm_lambda_im"][l] = dlr.reshape(NG, NS), dli.reshape(NG, NS)
        small_g["ssm_log_dt"][l] = dld[0, :NG]
        small_g["ssm_b_re"][l], small_g["ssm_b_im"][l] = _bt_inv(dbt[0]), _bt_inv(dbt[1])
        small_g["ssm_c_re"][l], small_g["ssm_c_im"][l] = _ct_inv(dct[0]), _ct_inv(dct[1])
        small_g["ssm_d"][l] = d_d
        dproj = assemble_dproj(du, [a[0] for a in ab], [a[1] for a in ab], [a[2] for a in ab],
                               dcv, mb["dgate"].reshape(bsz, seq, 3 * D)).reshape(n, INC)
        pushed(on_grads(l, "in", dict(w_in=mm_tn(S["h1"], dproj, tokblk(D), colblk(NSH_IN), NDEV, D, NSH_IN, n,
                                                 "dw_in"))))
        if l == 0:
            pushed(on_grads(l, "small", dict(small_g=small_g, loss8=loss8, dfinal=dfinal)))
        dx, dg1 = inproj_bwd(dproj, S["w_in4"], 0, S["x"], after_pushes(P["norm1_g"][l][None]), dx1)
        small_g["norm1_g"][l] = dg1
    return loss8, dx, dfinal, small_g


def kernel(x, norm1_g, w_in, b_gate, ssm_lambda_re, ssm_lambda_im, ssm_log_dt, ssm_b_re, ssm_b_im, ssm_c_re, ssm_c_im, ssm_d, w_ssm_glu, w_att_up, conv_w, conv_b, conv_ln_g, conv_ln_b, w_conv_pw2, w_out, norm2_g, w_ffn_in, w_ffn_out, final_g, loss_target, m_norm1_g, m_w_in, m_b_gate, m_ssm_lambda_re, m_ssm_lambda_im, m_ssm_log_dt, m_ssm_b_re, m_ssm_b_im, m_ssm_c_re, m_ssm_c_im, m_ssm_d, m_w_ssm_glu, m_w_att_up, m_conv_w, m_conv_b, m_conv_ln_g, m_conv_ln_b, m_w_conv_pw2, m_w_out, m_norm2_g, m_w_ffn_in, m_w_ffn_out, m_final_g, v_norm1_g, v_w_in, v_b_gate, v_ssm_lambda_re, v_ssm_lambda_im, v_ssm_log_dt, v_ssm_b_re, v_ssm_b_im, v_ssm_c_re, v_ssm_c_im, v_ssm_d, v_w_ssm_glu, v_w_att_up, v_conv_w, v_conv_b, v_conv_ln_g, v_conv_ln_b, v_w_conv_pw2, v_w_out, v_norm2_g, v_w_ffn_in, v_w_ffn_out, v_final_g):
    args = dict(locals())
    W = {k: args[k] for k in ORDER}
    M = {k: args["m_" + k] for k in ORDER}
    V = {k: args["v_" + k] for k in ORDER}
    bsz, seq, _ = x.shape
    n = bsz * seq
    me = 4 * lax.axis_index("x") + 2 * lax.axis_index("y") + lax.axis_index("c")

    groups = {"in": ["w_in"], "mix": ["w_ssm_glu", "w_att_up", "w_conv_pw2", "w_out"], "ffn": ["w_ffn_in", "w_ffn_out"]}
    wb = {k: W[k].astype(bf16) for k in BIG}

    def landing(shard):
        return lax.dynamic_update_index_in_dim(lax.empty((NDEV,) + shard.shape, shard.dtype), shard, me, 0)

    plan = [("gather_a", [("w_in", 0), ("conv_w", None)]),
            ("gather_b", [(k, 0) for k in groups["mix"] + groups["ffn"]]),
            ("gather_c", [(k, 1) for k in BIG])]
    pending, token = {}, None
    for name, items in plan:
        shards = [conv_w if l is None else wb[k][l] for k, l in items]
        send, recv, s_thru, l_thru, token = push_start(shards, [landing(s) for s in shards], False, name, token)
        pending[name] = (send, recv, s_thru, l_thru, items)
    gathered = {}

    def weights(l, group, after):
        name = "gather_c" if l == 1 else ("gather_a" if group == "in" else "gather_b")
        if name in pending:
            send, recv, s_thru, l_thru, items = pending.pop(name)
            for item, arr in zip(items, push_wait(send, recv, s_thru, l_thru, after, False, name + "_wait")[1]):
                gathered[item] = arr
        res = {k: gathered[(k, l)] for k in groups[group]}
        if group == "in":
            res["conv_w"] = gathered[("conv_w", None)]
        return res

    big_g = {k: [None] * DEPTH for k in BIG}
    flights = []

    def start_exchange(items, name):
        parts = [big_g[k][l] for k, l in items]
        lands = [lax.empty((NDEV - 1,) + p.shape[1:], p.dtype) for p in parts]
        send, recv, s_thru, l_thru, tok = push_start(parts, lands, True, name, None)
        flights.append((send, recv, s_thru, l_thru, items, name))
        return tok

    names = [k for k in SMALL if k != "final_g"]
    shapes = [(DEPTH, NDEV, CW, BW // NDEV) if k == "conv_w" else W[k].shape for k in names] + [(D,), (1,)]
    small_flight = []

    def start_small(small_g, loss8, dfinal):
        sg_ = dict(small_g, norm1_g=[jnp.zeros((1, D), f32), small_g["norm1_g"][1]])
        gpack = _pack([jnp.stack([g.reshape(shapes[i][1:]) for g in sg_[k]]) for i, k in enumerate(names)]
                      + [dfinal, loss8[0, :1]])
        send, recv, s_thru, l_thru, tok = push_start([gpack], [landing(gpack)], False, "gather_small", None)
        small_flight.append((send, recv, s_thru, l_thru))
        return tok

    def on_grads(l, group, grads):
        if group == "small":
            return start_small(**grads)
        for k, g in grads.items():
            big_g[k][l] = g
        if l == 1 and group == "in":
            return start_exchange([(k, 1) for k in BIG], "exchange_l1")
        if l == 0:
            return start_exchange([(k, 0) for k in groups[group]], "exchange_l0_" + group)
        return None

    loss8, dx, dfinal, small_g = local_step(x, loss_target, W, weights, on_grads, token)

    landed, own = {}, {}
    for send, recv, s_thru, l_thru, items, name in flights:
        srcs, lands = push_wait(send, recv, s_thru, l_thru, dx, True, name + "_wait")
        for item, src, land in zip(items, srcs, lands):
            landed[item] = land
            own[item] = lax.dynamic_index_in_dim(src, me, 0, keepdims=False)
    out = {}
    for k in BIG:
        items = [(k, l) for l in range(DEPTH)]
        out[k] = adam_big([landed[i] for i in items], [own[i] for i in items], W[k], M[k], V[k], "adam_" + k)

    def wpack(src):
        parts = [jnp.broadcast_to(src[k][:, None], shapes[i]) if k == "conv_w" else src[k] for i, k in enumerate(names)]
        return _pack(parts + [src["final_g"], jnp.ones((1,), f32)])

    send, recv, s_thru, l_thru = small_flight[0]
    gall = push_wait(send, recv, s_thru, l_thru, dx, False, "gather_small_wait")[1][0]
    (late,) = allgather([small_g["norm1_g"][0].reshape(1, D // 128, 128)], "allgather_late")
    gall = lax.dynamic_update_slice(gall, late[0], (0, 0, 0))
    sg, sd, sm, sv = [_unpack(p, shapes) for p in adam_small(gall, wpack(W), wpack(M), wpack(V))]
    for i, k in enumerate(names + ["final_g"]):
        vals = [t[i] for t in (sg, sd, sm, sv)]
        if k == "conv_w":
            vals = [lax.dynamic_index_in_dim(t, me, axis=1, keepdims=False) for t in vals]
        out[k] = vals
    loss = sg[-1].reshape(())

    res = [loss, dx.reshape(bsz, seq, D)]
    for j in range(4):
        res += [out[k][j] for k in ORDER]
    return tuple(res)
```

```python
import functools
import math

import jax
import jax.numpy as jnp
from jax import lax
from jax.experimental import pallas as pl
from jax.experimental.pallas import tpu as pltpu

f32 = jnp.float32
bf16 = jnp.bfloat16

D = 1024
DEPTH = 2
EPS = 1e-6
BW = 512
NG = 32
GH = 16
NS = 64
NSTATE = NG * NS
HD = 64
NH = 8
PATTERNS = ((128, 1), (512, 4), (2048, 16))
ABLK = 128
ATT_SCALE = HD ** -0.5
CW = 31
DFF = 2816
INC = 7168
NDEV = 8
NSH_IN = INC // NDEV
NSH_FF = 2 * DFF // NDEV
ADAM_LR, ADAM_B1, ADAM_B2, ADAM_EPS, ADAM_WD, ADAM_STEP = 0.001, 0.9, 0.999, 1e-08, 0.01, 10

TB = 512
SJ = 4
SW = NSTATE // SJ
SU = BW // SJ
NEG = -1e30
MESH = pl.DeviceIdType.MESH
ANY = pl.BlockSpec(memory_space=pl.ANY)


def _cp(n_axes, vmem_mb=48):
    return pltpu.CompilerParams(dimension_semantics=("arbitrary",) * n_axes,
                                vmem_limit_bytes=vmem_mb * 1024 * 1024)


def _dot(a, b):
    return jnp.dot(a, b, preferred_element_type=f32)


def _dot_nt(a, b):
    return lax.dot_general(a, b, (((1,), (1,)), ((), ())), preferred_element_type=f32)


def _dot_tn(a, b):
    return lax.dot_general(a, b, (((0,), (0,)), ((), ())), preferred_element_type=f32)


def _dot_hi(a, b):
    return jnp.dot(a, b, precision=lax.Precision.HIGHEST, preferred_element_type=f32)


def _sigmoid(x):
    return 1.0 / (1.0 + jnp.exp(-x))


_GC = math.sqrt(2.0 / math.pi)


def _gelu(x):
    return 0.5 * x * (1.0 + jnp.tanh(_GC * (x + 0.044715 * x * x * x)))


def _gelu_grad(x):
    t = jnp.tanh(_GC * (x + 0.044715 * x * x * x))
    return 0.5 * (1.0 + t) + 0.5 * x * (1.0 - t * t) * _GC * (1.0 + 3.0 * 0.044715 * x * x)


def _rms_stats(x):
    return lax.rsqrt(jnp.mean(x * x, axis=-1, keepdims=True) + EPS)


def _rms_bwd(x, g, dh):
    r = _rms_stats(x)
    dyg = dh * g
    dx = r * dyg - x * (r * r * r) * jnp.mean(dyg * x, axis=-1, keepdims=True)
    dg = jnp.sum(dh * x * r, axis=0, keepdims=True)
    return dx, dg


def rms_fwd(x, g):
    n = x.shape[0]

    def body(x_ref, g_ref, h_ref):
        xv = x_ref[...]
        h_ref[...] = (xv * _rms_stats(xv) * g_ref[...]).astype(bf16)

    return pl.pallas_call(
        body, name="rms_fwd", grid=(n // TB,),
        in_specs=[pl.BlockSpec((TB, D), lambda i: (i, 0)), pl.BlockSpec((1, D), lambda i: (0, 0))],
        out_specs=pl.BlockSpec((TB, D), lambda i: (i, 0)),
        out_shape=jax.ShapeDtypeStruct((n, D), bf16), compiler_params=_cp(1))(x, g)


def inproj(h, w4, layer):
    n = h.shape[0]
    tm = 1024
    nblk, wblk = w4.shape[1], w4.shape[3]

    def body(h_ref, w_ref, o_ref):
        o_ref[...] = _dot(h_ref[...], w_ref[0, 0]).astype(bf16)

    return pl.pallas_call(
        body, name="inproj", grid=(nblk, n // tm),
        in_specs=[pl.BlockSpec((tm, D), lambda s, i: (i, 0)),
                  pl.BlockSpec((1, 1, D, wblk), lambda s, i: (layer, s, 0, 0))],
        out_specs=pl.BlockSpec((tm, wblk), lambda s, i: (i, s)),
        out_shape=jax.ShapeDtypeStruct((n, INC), bf16), compiler_params=_cp(2))(h, w4)


def inproj_bwd(dproj, w4, layer, x, g, dres):
    n = x.shape[0]
    tm = 1024
    nblk, wblk = w4.shape[1], w4.shape[3]

    def body(dp_ref, w_ref, x_ref, g_ref, dr_ref, dx_ref, dg_ref, acc):
        i, s = pl.program_id(0), pl.program_id(1)

        @pl.when(s == 0)
        def _():
            acc[...] = jnp.zeros_like(acc)

        @pl.when((s == 0) & (i == 0))
        def _():
            dg_ref[...] = jnp.zeros_like(dg_ref)

        acc[...] += _dot_nt(dp_ref[...], w_ref[0, 0])

        @pl.when(s == nblk - 1)
        def _():
            dx, dg = _rms_bwd(x_ref[...], g_ref[...], acc[...])
            dx_ref[...] = dr_ref[...] + dx
            dg_ref[...] += dg

    return pl.pallas_call(
        body, name="inproj_bwd", grid=(n // tm, nblk),
        in_specs=[pl.BlockSpec((tm, wblk), lambda i, s: (i, s)),
                  pl.BlockSpec((1, 1, D, wblk), lambda i, s: (layer, s, 0, 0)),
                  pl.BlockSpec((tm, D), lambda i, s: (i, 0)),
                  pl.BlockSpec((1, D), lambda i, s: (0, 0)),
                  pl.BlockSpec((tm, D), lambda i, s: (i, 0))],
        out_specs=[pl.BlockSpec((tm, D), lambda i, s: (i, 0)), pl.BlockSpec((1, D), lambda i, s: (0, 0))],
        out_shape=[jax.ShapeDtypeStruct((n, D), f32), jax.ShapeDtypeStruct((1, D), f32)],
        scratch_shapes=[pltpu.VMEM((tm, D), f32)], compiler_params=_cp(2))(dproj, w4, x, g, dres)


def mm_tn(a, b, a_spec, b_spec, n_sh, ka, nb, m, name):
    tm = 1024

    def body(a_ref, b_ref, o_ref, acc):
        i = pl.program_id(1)

        @pl.when(i == 0)
        def _():
            acc[...] = jnp.zeros_like(acc)

        av = a_ref[...].reshape(tm, ka).astype(bf16)
        bv = b_ref[...].reshape(tm, nb).astype(bf16)
        acc[...] += _dot_tn(av, bv)

        @pl.when(i == m // tm - 1)
        def _():
            o_ref[0] = acc[...].astype(bf16)

    return pl.pallas_call(
        body, name=name, grid=(n_sh, m // tm), in_specs=[a_spec, b_spec],
        out_specs=pl.BlockSpec((1, ka, nb), lambda s, i: (s, 0, 0)),
        out_shape=jax.ShapeDtypeStruct((n_sh, ka, nb), bf16),
        scratch_shapes=[pltpu.VMEM((ka, nb), f32)], compiler_params=_cp(2))(a, b)


def dw_mix(ysin, dz, ob, dya, hs, dyc, merged, dx1):
    n = ysin.shape[0]
    tm = 512
    pairs = ((BW, 2 * D), (BW, D), (BW, D), (D, D))

    def body(a0, b0, a1, b1, a2, b2, a3, b3, o0, o1, o2, o3, c0, c1, c2, c3):
        i = pl.program_id(0)
        accs = (c0, c1, c2, c3)

        @pl.when(i == 0)
        def _():
            for c in accs:
                c[...] = jnp.zeros_like(c)

        for a, b_, c in zip((a0, a1, a2, a3), (b0, b1, b2, b3), accs):
            c[...] += _dot_tn(a[...], b_[...].astype(bf16))

        @pl.when(i == n // tm - 1)
        def _():
            for s in range(NDEV):
                o0[s] = c0[:, s * 256:(s + 1) * 256].astype(bf16)
                o1[s] = c1[:, s * 128:(s + 1) * 128].astype(bf16)
                o2[s] = c2[:, s * 128:(s + 1) * 128].astype(bf16)
                o3[s] = c3[s * 128:(s + 1) * 128, :].astype(bf16)

    tok = lambda w: pl.BlockSpec((tm, w), lambda i: (i, 0))
    whole = lambda shape: pl.BlockSpec(shape, lambda i: (0, 0, 0))
    outs = [(NDEV, BW, 256), (NDEV, BW, 128), (NDEV, BW, 128), (NDEV, D // NDEV, D)]
    return pl.pallas_call(
        body, name="dw_mix", grid=(n // tm,),
        in_specs=[tok(w) for pair in pairs for w in pair],
        out_specs=[whole(s) for s in outs], out_shape=[jax.ShapeDtypeStruct(s, bf16) for s in outs],
        scratch_shapes=[pltpu.VMEM(p, f32) for p in pairs],
        compiler_params=_cp(1, 56))(ysin, dz, ob, dya, hs, dyc, merged, dx1)


def ffn_fwd(x1, g2, w1, w2, layer):
    n = x1.shape[0]
    w2p = w2.reshape(w2.shape[0], 4, NSH_FF, D)

    def body(x_ref, g_ref, wa_ref, wb_ref, w2_ref, o_ref, z1_ref, z2_ref, h_sc):
        s = pl.program_id(1)

        @pl.when(s == 0)
        def _():
            xv = x_ref[...]
            h_sc[...] = (xv * _rms_stats(xv) * g_ref[...]).astype(bf16)
            o_ref[...] = xv

        h = h_sc[...]
        z1 = _dot(h, wa_ref[0, 0])
        z2 = _dot(h, wb_ref[0, 0])
        z1_ref[0] = z1.astype(bf16)
        z2_ref[0] = z2.astype(bf16)
        a = (z1 * _sigmoid(z1) * z2).astype(bf16)
        o_ref[...] += _dot(a, w2_ref[0, 0])

    sh3 = pl.BlockSpec((1, TB, NSH_FF), lambda i, s: (s, i, 0))
    return pl.pallas_call(
        body, name="ffn_fwd", grid=(n // TB, 4),
        in_specs=[pl.BlockSpec((TB, D), lambda i, s: (i, 0)),
                  pl.BlockSpec((1, D), lambda i, s: (0, 0)),
                  pl.BlockSpec((1, 1, D, NSH_FF), lambda i, s: (layer, s, 0, 0)),
                  pl.BlockSpec((1, 1, D, NSH_FF), lambda i, s: (layer, s + 4, 0, 0)),
                  pl.BlockSpec((1, 1, NSH_FF, D), lambda i, s: (layer, s, 0, 0))],
        out_specs=[pl.BlockSpec((TB, D), lambda i, s: (i, 0)), sh3, sh3],
        out_shape=[jax.ShapeDtypeStruct((n, D), f32), jax.ShapeDtypeStruct((4, n, NSH_FF), bf16),
                   jax.ShapeDtypeStruct((4, n, NSH_FF), bf16)],
        scratch_shapes=[pltpu.VMEM((TB, D), bf16)], compiler_params=_cp(2))(x1, g2, w1, w1, w2p)


def ffn_bwd(x1, g2, w1, w2, layer, dx2, z1s, z2s):
    n = x1.shape[0]
    w2p = w2.reshape(w2.shape[0], 4, NSH_FF, D)

    def body(x_ref, g_ref, dy_ref, wa_ref, wb_ref, w2_ref, z1_ref, z2_ref,
             dx_ref, h_ref, dz1_ref, dz2_ref, a_ref, dg_ref, dh_sc, dyb_sc):
        i, s = pl.program_id(0), pl.program_id(1)

        @pl.when(s == 0)
        def _():
            xv = x_ref[...]
            h_ref[...] = (xv * _rms_stats(xv) * g_ref[...]).astype(bf16)
            dh_sc[...] = jnp.zeros_like(dh_sc)
            dyb_sc[...] = dy_ref[...].astype(bf16)

        @pl.when((s == 0) & (i == 0))
        def _():
            dg_ref[...] = jnp.zeros_like(dg_ref)

        z1 = z1_ref[0].astype(f32)
        z2 = z2_ref[0].astype(f32)
        sg = _sigmoid(z1)
        sl = z1 * sg
        a_ref[0] = (sl * z2).astype(bf16)
        da = _dot_nt(dyb_sc[...], w2_ref[0, 0])
        dz2 = (da * sl).astype(bf16)
        dz1 = (da * z2 * sg * (1.0 + z1 * (1.0 - sg))).astype(bf16)
        dz1_ref[0] = dz1
        dz2_ref[0] = dz2
        dh_sc[...] += _dot_nt(dz1, wa_ref[0, 0]) + _dot_nt(dz2, wb_ref[0, 0])

        @pl.when(s == 3)
        def _():
            dx, dg = _rms_bwd(x_ref[...], g_ref[...], dh_sc[...])
            dx_ref[...] = dy_ref[...] + dx
            dg_ref[...] += dg

    tok = lambda i, s: (i, 0)
    sh3 = lambda i, s: (s, i, 0)
    return pl.pallas_call(
        body, name="ffn_bwd", grid=(n // TB, 4),
        in_specs=[pl.BlockSpec((TB, D), tok), pl.BlockSpec((1, D), lambda i, s: (0, 0)), pl.BlockSpec((TB, D), tok),
                  pl.BlockSpec((1, 1, D, NSH_FF), lambda i, s: (layer, s, 0, 0)),
                  pl.BlockSpec((1, 1, D, NSH_FF), lambda i, s: (layer, s + 4, 0, 0)),
                  pl.BlockSpec((1, 1, NSH_FF, D), lambda i, s: (layer, s, 0, 0)),
                  pl.BlockSpec((1, TB, NSH_FF), sh3), pl.BlockSpec((1, TB, NSH_FF), sh3)],
        out_specs=[pl.BlockSpec((TB, D), tok), pl.BlockSpec((TB, D), tok),
                   pl.BlockSpec((1, TB, NSH_FF), sh3), pl.BlockSpec((1, TB, NSH_FF), sh3),
                   pl.BlockSpec((1, TB, NSH_FF), sh3), pl.BlockSpec((1, D), lambda i, s: (0, 0))],
        out_shape=[jax.ShapeDtypeStruct((n, D), f32), jax.ShapeDtypeStruct((n, D), bf16),
                   jax.ShapeDtypeStruct((4, n, NSH_FF), bf16), jax.ShapeDtypeStruct((4, n, NSH_FF), bf16),
                   jax.ShapeDtypeStruct((4, n, NSH_FF), bf16), jax.ShapeDtypeStruct((1, D), f32)],
        scratch_shapes=[pltpu.VMEM((TB, D), f32), pltpu.VMEM((TB, D), bf16)],
        compiler_params=_cp(2))(x1, g2, dx2, w1, w1, w2p, z1s, z2s)


def loss_head(x, g, target):
    n = x.shape[0]

    def body(x_ref, g_ref, t_ref, l_ref, dx_ref, dg_ref):
        i = pl.program_id(0)

        @pl.when(i == 0)
        def _():
            l_ref[...] = jnp.zeros_like(l_ref)
            dg_ref[...] = jnp.zeros_like(dg_ref)

        xv = x_ref[...]
        y = xv * _rms_stats(xv) * g_ref[...]
        e = y - t_ref[...]
        l_ref[...] += 0.5 * jnp.sum(jnp.sum(e * e, axis=-1, keepdims=True), axis=0, keepdims=True) * (1.0 / D)
        dx, dg = _rms_bwd(xv, g_ref[...], e * (1.0 / D))
        dx_ref[...] = dx
        dg_ref[...] += dg

    tok = lambda i: (i, 0)
    return pl.pallas_call(
        body, name="loss_head", grid=(n // TB,),
        in_specs=[pl.BlockSpec((TB, D), tok), pl.BlockSpec((1, D), lambda i: (0, 0)), pl.BlockSpec((TB, D), tok)],
        out_specs=[pl.BlockSpec((8, 128), lambda i: (0, 0)), pl.BlockSpec((TB, D), tok),
                   pl.BlockSpec((1, D), lambda i: (0, 0))],
        out_shape=[jax.ShapeDtypeStruct((8, 128), f32), jax.ShapeDtypeStruct((n, D), f32),
                   jax.ShapeDtypeStruct((1, D), f32)],
        compiler_params=_cp(1))(x, g, target)


def _disc(lr, li, ld):
    dt = jnp.exp(ld)
    mag = jnp.exp(lr * dt)
    ar = mag * jnp.cos(li * dt)
    ai = mag * jnp.sin(li * dt)
    nr, ni = ar - 1.0, ai
    den = lr * lr + li * li
    zr = (nr * lr + ni * li) / den
    zi = (ni * lr - nr * li) / den
    return ar, ai, zr, zi


def _blockdiag_mask(shape):
    r = lax.broadcasted_iota(jnp.int32, shape, 0) // GH
    c = lax.broadcasted_iota(jnp.int32, shape, 1) // NS
    return r == c


def s5_params(lr, li, ld, btr, bti, ctr, cti):
    def body(lr_ref, li_ref, ld_ref, btr_ref, bti_ref, ctr_ref, cti_ref, t8_ref, bb_ref, cb_ref):
        ar, ai, zr, zi = _disc(lr_ref[...], li_ref[...], ld_ref[...])
        pr_, pi_ = ar, ai
        pw2 = []
        for k in range(4):
            pw2.append((pr_, pi_))
            pr_, pi_ = pr_ * pr_ - pi_ * pi_, 2.0 * pr_ * pi_
        cm = lambda p, q: (p[0] * q[0] - p[1] * q[1], p[0] * q[1] + p[1] * q[0])
        pw = {1: pw2[0], 2: pw2[1], 4: pw2[2], 8: pw2[3]}
        pw[3], pw[5], pw[6] = cm(pw[2], pw[1]), cm(pw[4], pw[1]), cm(pw[4], pw[2])
        pw[7] = cm(pw[4], pw[3])
        row = lax.broadcasted_iota(jnp.int32, (8, NSTATE), 0)
        zero = jnp.zeros((8, NSTATE), f32)
        for c in range(2):
            for k in range(3):
                full = jnp.broadcast_to(pw2[k][c], (8, NSTATE))
                t8_ref[c, k] = jnp.where(row >= (1 << k), full, 0.0)
                t8_ref[c, 3 + k] = jnp.where(row + (1 << k) < 8, full, 0.0)
            up, down = zero, zero
            for j in range(8):
                up = up + jnp.where(row == j, pw[j + 1][c], 0.0)
                down = down + jnp.where(row == j, pw[8 - j][c], 0.0)
            t8_ref[c, 6] = up
            t8_ref[c, 7] = down
        bbr = zr * btr_ref[...] - zi * bti_ref[...]
        bbi = zr * bti_ref[...] + zi * btr_ref[...]
        mask = _blockdiag_mask((SU, SW))
        for j in range(SJ):
            cols = slice(j * SW, (j + 1) * SW)
            for c, (vb, vc) in enumerate(((bbr, ctr_ref[...]), (bbi, cti_ref[...]))):
                bb_ref[c, j] = jnp.where(mask, jnp.tile(vb[:, cols], (SU // GH, 1)), 0.0).astype(bf16)
                cb_ref[c, j] = jnp.where(mask, jnp.tile(vc[:, cols], (SU // GH, 1)), 0.0).astype(bf16)

    return pl.pallas_call(
        body, name="s5_params",
        out_shape=[jax.ShapeDtypeStruct((2, 8, 8, NSTATE), f32),
                   jax.ShapeDtypeStruct((2, SJ, SU, SW), bf16), jax.ShapeDtypeStruct((2, SJ, SU, SW), bf16)],
        compiler_params=pltpu.CompilerParams(vmem_limit_bytes=56 * 1024 * 1024))(lr, li, ld, btr, bti, ctr, cti)


def s5_params_bwd(lr, li, ld, btr, bti, d_a, d_bb, d_cb):
    def body(lr_ref, li_ref, ld_ref, btr_ref, bti_ref, da_ref, dbb_ref, dcb_ref,
             dlr_ref, dli_ref, dld_ref, dbt_ref, dct_ref):
        mask = _blockdiag_mask((SU, SW))

        def fold(ref, c):
            parts = []
            for j in range(SJ):
                v = jnp.where(mask, ref[c, j], 0.0)
                parts.append(v.reshape(SU // GH, GH, SW).sum(axis=0))
            return jnp.concatenate(parts, axis=1)

        dct_ref[0] = fold(dcb_ref, 0)
        dct_ref[1] = fold(dcb_ref, 1)
        dbbr, dbbi = fold(dbb_ref, 0), fold(dbb_ref, 1)
        lrv, liv, ldv = lr_ref[...], li_ref[...], ld_ref[...]
        (ar, ai, zr, zi), vjp = jax.vjp(_disc, lrv, liv, ldv)
        btr, bti = btr_ref[...], bti_ref[...]
        dbt_ref[0] = zr * dbbr + zi * dbbi
        dbt_ref[1] = zr * dbbi - zi * dbbr
        dzr = jnp.sum(dbbr * btr + dbbi * bti, axis=0, keepdims=True)
        dzi = jnp.sum(dbbi * btr - dbbr * bti, axis=0, keepdims=True)
        dlr, dli, dld = vjp((da_ref[0:1, :], da_ref[1:2, :], dzr, dzi))
        dlr_ref[...] = dlr
        dli_ref[...] = dli
        ind = (lax.broadcasted_iota(jnp.int32, (NSTATE, 128), 0) // NS
               == lax.broadcasted_iota(jnp.int32, (NSTATE, 128), 1)).astype(f32)
        dld_ref[...] = _dot_hi(jnp.broadcast_to(dld, (8, NSTATE)), ind)

    return pl.pallas_call(
        body, name="s5_params_bwd",
        out_shape=[jax.ShapeDtypeStruct((1, NSTATE), f32), jax.ShapeDtypeStruct((1, NSTATE), f32),
                   jax.ShapeDtypeStruct((8, 128), f32), jax.ShapeDtypeStruct((2, GH, NSTATE), f32),
                   jax.ShapeDtypeStruct((2, GH, NSTATE), f32)],
        compiler_params=pltpu.CompilerParams(vmem_limit_bytes=56 * 1024 * 1024))(lr, li, ld, btr, bti, d_a, d_bb, d_cb)


def _fma(sr, si, ar, ai, qr, qi):
    return sr + ar * qr - ai * qi, si + ar * qi + ai * qr


def _scan_tile(sr, si, cr, ci, t8_ref, reverse):
    sg = -1.0 if reverse else 1.0
    for k in range(3):
        tk = 3 + k if reverse else k
        rot = 8 - (1 << k) if reverse else 1 << k
        sr, si = _fma(sr, si, t8_ref[0, tk], sg * t8_ref[1, tk], pltpu.roll(sr, rot, 0), pltpu.roll(si, rot, 0))
    tp = 7 if reverse else 6
    sr, si = _fma(sr, si, t8_ref[0, tp], sg * t8_ref[1, tp], cr, ci)
    e = 0 if reverse else 7
    return sr, si, jnp.broadcast_to(sr[e:e + 1, :], sr.shape), jnp.broadcast_to(si[e:e + 1, :], si.shape)


S5MC = 512


def _s5_input_map(u_ref, bb_ref, sr_sc, si_sc, l):
    for c in range(l // S5MC):
        rows = slice(c * S5MC, (c + 1) * S5MC)
        u = u_ref[0, rows, :]
        sr_sc[rows, :] = _dot(u, bb_ref[0, 0])
        si_sc[rows, :] = _dot(u, bb_ref[1, 0])


def _s5_forward_scan(sr_sc, si_sc, t8_ref, l):
    def step(k, carry):
        rows = pl.ds(pl.multiple_of(k * 8, 8), 8)
        sr, si, cr, ci = _scan_tile(sr_sc[rows, :], si_sc[rows, :], carry[0], carry[1], t8_ref, False)
        sr_sc[rows, :] = sr
        si_sc[rows, :] = si
        return cr, ci

    zero = jnp.zeros((8, SW), f32)
    lax.fori_loop(0, l // 8, step, (zero, zero), unroll=4)


def s5_fwd(proj3, t8, bb, cb, dskip):
    b, l, _ = proj3.shape

    def body(u_ref, t8_ref, bb_ref, cb_ref, d_ref, y_ref, sr_out, si_out):
        sr_sc, si_sc = sr_out.at[0], si_out.at[0]
        _s5_input_map(u_ref, bb_ref, sr_sc, si_sc, l)
        _s5_forward_scan(sr_sc, si_sc, t8_ref, l)
        for c in range(l // S5MC):
            rows = slice(c * S5MC, (c + 1) * S5MC)
            y = (_dot_nt(sr_sc[rows, :].astype(bf16), cb_ref[0, 0])
                 - _dot_nt(si_sc[rows, :].astype(bf16), cb_ref[1, 0]))
            y_ref[0, rows, :] = y + d_ref[...] * u_ref[0, rows, :].astype(f32)

    return pl.pallas_call(
        body, name="s5_fwd", grid=(SJ, b),
        in_specs=[pl.BlockSpec((1, l, SU), lambda j, bi: (bi, 0, j)),
                  pl.BlockSpec((2, 8, 8, SW), lambda j, bi: (0, 0, 0, j)),
                  pl.BlockSpec((2, 1, SU, SW), lambda j, bi: (0, j, 0, 0)),
                  pl.BlockSpec((2, 1, SU, SW), lambda j, bi: (0, j, 0, 0)),
                  pl.BlockSpec((1, SU), lambda j, bi: (0, j))],
        out_specs=[pl.BlockSpec((1, l, SU), lambda j, bi: (bi, 0, j)),
                   pl.BlockSpec((1, l, SW), lambda j, bi: (bi, 0, j)),
                   pl.BlockSpec((1, l, SW), lambda j, bi: (bi, 0, j))],
        out_shape=[jax.ShapeDtypeStruct((b, l, BW), f32), jax.ShapeDtypeStruct((b, l, NSTATE), f32),
                   jax.ShapeDtypeStruct((b, l, NSTATE), f32)],
        compiler_params=_cp(2))(proj3, t8, bb, cb, dskip)


def s5_bwd(proj3, dy, s_re, s_im, t8, bb, cb, dskip):
    b, l, _ = proj3.shape
    nt = l // 8

    def body(u_ref, dy_ref, sr_in, si_in, t8_ref, bb_ref, cb_ref, d_ref,
             du_ref, da_ref, dbb_ref, dcb_ref, dd_ref, gr_sc, gi_sc):
        bi = pl.program_id(1)
        sr_sc, si_sc = sr_in.at[0], si_in.at[0]

        @pl.when(bi == 0)
        def _():
            da_ref[...] = jnp.zeros_like(da_ref)
            dbb_ref[...] = jnp.zeros_like(dbb_ref)
            dcb_ref[...] = jnp.zeros_like(dcb_ref)
            dd_ref[...] = jnp.zeros_like(dd_ref)

        for c in range(l // S5MC):
            rows = slice(c * S5MC, (c + 1) * S5MC)
            dyb = dy_ref[0, rows, :].astype(bf16)
            gr_sc[rows, :] = _dot(dyb, cb_ref[0, 0])
            gi_sc[rows, :] = -_dot(dyb, cb_ref[1, 0])

        row = lax.broadcasted_iota(jnp.int32, (8, SW), 0)

        def step(i, carry):
            cr, ci, dar, dai = carry
            k = nt - 1 - i
            rows = pl.ds(pl.multiple_of(k * 8, 8), 8)
            gr, gi, cr, ci = _scan_tile(gr_sc[rows, :], gi_sc[rows, :], cr, ci, t8_ref, True)
            gr_sc[rows, :] = gr
            gi_sc[rows, :] = gi
            before = pl.ds(pl.multiple_of(jnp.maximum(k - 1, 0) * 8, 8), 8)
            live = jnp.where(k > 0, 1.0, 0.0)
            sr, si = sr_sc[rows, :], si_sc[rows, :]
            spr = jnp.where(row == 0, live * sr_sc[before, :][7:8, :], pltpu.roll(sr, 1, 0))
            spi = jnp.where(row == 0, live * si_sc[before, :][7:8, :], pltpu.roll(si, 1, 0))
            return cr, ci, dar + spr * gr + spi * gi, dai + spr * gi - spi * gr

        zero = jnp.zeros((8, SW), f32)
        _, _, dar, dai = lax.fori_loop(0, nt, step, (zero, zero, zero, zero), unroll=2)
        da_ref[0:1, :] += jnp.sum(dar, axis=0, keepdims=True)
        da_ref[1:2, :] += jnp.sum(dai, axis=0, keepdims=True)

        for c in range(l // S5MC):
            rows = slice(c * S5MC, (c + 1) * S5MC)
            u = u_ref[0, rows, :]
            dyv = dy_ref[0, rows, :]
            dyb = dyv.astype(bf16)
            grb, gib = gr_sc[rows, :].astype(bf16), gi_sc[rows, :].astype(bf16)
            dcb_ref[0, 0] += _dot_tn(dyb, sr_sc[rows, :].astype(bf16))
            dcb_ref[1, 0] -= _dot_tn(dyb, si_sc[rows, :].astype(bf16))
            dbb_ref[0, 0] += _dot_tn(u, grb)
            dbb_ref[1, 0] += _dot_tn(u, gib)
            du = _dot_nt(grb, bb_ref[0, 0]) + _dot_nt(gib, bb_ref[1, 0]) + d_ref[...] * dyv
            du_ref[0, rows, :] = du.astype(bf16)
            dd_ref[...] += jnp.sum(dyv * u.astype(f32), axis=0, keepdims=True)

    seq = pl.BlockSpec((1, l, SU), lambda j, bi: (bi, 0, j))
    sts = pl.BlockSpec((1, l, SW), lambda j, bi: (bi, 0, j))
    tab = pl.BlockSpec((2, 1, SU, SW), lambda j, bi: (0, j, 0, 0))
    return pl.pallas_call(
        body, name="s5_bwd", grid=(SJ, b),
        in_specs=[seq, seq, sts, sts, pl.BlockSpec((2, 8, 8, SW), lambda j, bi: (0, 0, 0, j)), tab, tab,
                  pl.BlockSpec((1, SU), lambda j, bi: (0, j))],
        out_specs=[seq, pl.BlockSpec((2, SW), lambda j, bi: (0, j)), tab, tab,
                   pl.BlockSpec((1, SU), lambda j, bi: (0, j))],
        out_shape=[jax.ShapeDtypeStruct((b, l, BW), bf16), jax.ShapeDtypeStruct((2, NSTATE), f32),
                   jax.ShapeDtypeStruct((2, SJ, SU, SW), f32), jax.ShapeDtypeStruct((2, SJ, SU, SW), f32),
                   jax.ShapeDtypeStruct((1, BW), f32)],
        scratch_shapes=[pltpu.VMEM((l, SW), f32)] * 2,
        compiler_params=_cp(2))(proj3, dy, s_re, s_im, t8, bb, cb, dskip)


AHC = 2
AHW = AHC * 128


def _att_mask(n, nb):
    if nb == 1:
        qi = lax.broadcasted_iota(jnp.int32, (ABLK, ABLK), 0)
        kj = lax.broadcasted_iota(jnp.int32, (ABLK, ABLK), 1)
        return kj <= qi
    qi = lax.broadcasted_iota(jnp.int32, (ABLK, 2 * ABLK), 0)
    kj = lax.broadcasted_iota(jnp.int32, (ABLK, 2 * ABLK), 1)
    return (kj >= qi) & (kj <= qi + ABLK) & ((n > 0) | (kj >= ABLK))


def _att_rows(it, nb, dil):
    r, n = it // nb, it % nb
    cur = pl.ds(r + n * (ABLK * dil), ABLK, stride=dil)
    prv = pl.ds(r + jnp.maximum(n - 1, 0) * (ABLK * dil), ABLK, stride=dil)
    return n, cur, prv


def _att_keys(ref, c, cur, prv, nb):
    if nb == 1:
        x = ref[c, cur, :].astype(bf16)
    else:
        x = jnp.concatenate([ref[c, prv, :], ref[c, cur, :]], axis=0).astype(bf16)
    head0 = lax.broadcasted_iota(jnp.int32, x.shape, 1) < HD
    zero = jnp.zeros_like(x)
    return jnp.concatenate([jnp.where(head0, x, zero), jnp.where(head0, zero, x)], axis=0)


def _per_head(nk, a0, a1):
    col = lax.broadcasted_iota(jnp.int32, (ABLK, 2 * nk), 1)
    return jnp.where(col < nk, a0, a1)


def _to_chunks(src_ref, dst):
    for c in range(AHC):
        dst[c] = src_ref[0, :, c * 128:(c + 1) * 128].astype(f32)


def att_fwd(proj3, g_idx, dil):
    b, l, _ = proj3.shape
    nb = l // dil // ABLK
    nhalf = BW // AHW

    def body(q_ref, k_ref, v_ref, o_ref, lse_ref, qf, kf, vf, of):
        hh = pl.program_id(1)
        _to_chunks(q_ref, qf)
        _to_chunks(k_ref, kf)
        _to_chunks(v_ref, vf)
        lane = lax.broadcasted_iota(jnp.int32, (ABLK, 128), 1)

        def step(it, carry):
            n, cur, prv = _att_rows(it, nb, dil)
            valid = _att_mask(n, nb)
            valid = jnp.concatenate([valid, valid], axis=1)
            nk = valid.shape[1] // 2
            lse_all = jnp.zeros((ABLK, 128), f32)
            for c in range(AHC):
                q = (qf[c, cur, :] * ATT_SCALE).astype(bf16)
                k = _att_keys(kf, c, cur, prv, nb)
                v = _att_keys(vf, c, cur, prv, nb)
                s = jnp.where(valid, _dot_nt(q, k), NEG)
                m0 = jnp.max(s[:, :nk], axis=-1, keepdims=True)
                m1 = jnp.max(s[:, nk:], axis=-1, keepdims=True)
                p = jnp.exp(s - _per_head(nk, m0, m1))
                den0 = jnp.sum(p[:, :nk], axis=-1, keepdims=True)
                den1 = jnp.sum(p[:, nk:], axis=-1, keepdims=True)
                of[c, cur, :] = _dot(p.astype(bf16), v) * jnp.where(lane < HD, 1.0 / den0, 1.0 / den1)
                head = hh * (2 * AHC) + 2 * c
                lse_all = (lse_all + jnp.where(lane == head, m0 + jnp.log(den0), 0.0)
                           + jnp.where(lane == head + 1, m1 + jnp.log(den1), 0.0))

            lse_ref[0, 0, cur, :] = lse_all
            return carry

        lax.fori_loop(0, dil * nb, step, 0, unroll=4)
        for c in range(AHC):
            o_ref[0, :, c * 128:(c + 1) * 128] = of[c].astype(bf16)

    col = lambda c: pl.BlockSpec((1, l, AHW), lambda bi, hh: (bi, 0, c * nhalf + hh))
    return pl.pallas_call(
        body, name=f"att_fwd{g_idx}", grid=(b, nhalf),
        in_specs=[col(1 + g_idx), col(4), col(5)],
        out_specs=[pl.BlockSpec((1, l, AHW), lambda bi, hh: (bi, 0, hh)),
                   pl.BlockSpec((1, 1, l, 128), lambda bi, hh: (bi, hh, 0, 0))],
        out_shape=[jax.ShapeDtypeStruct((b, l, BW), bf16), jax.ShapeDtypeStruct((b, nhalf, l, 128), f32)],
        scratch_shapes=[pltpu.VMEM((AHC, l, 128), f32)] * 4,
        compiler_params=_cp(2))(proj3, proj3, proj3)


def att_bwd(proj3, do, lse_tot, delta, g_idx, dil):
    b, l, _ = proj3.shape
    nb = l // dil // ABLK
    nhalf = BW // AHW

    def body(q_ref, k_ref, v_ref, do_ref, l_ref, dl_ref, dq_out, dk_out, dv_out, qf, kf, vf, dof,
             dq_ref, dk_ref, dv_ref):
        hh = pl.program_id(1)
        _to_chunks(q_ref, qf)
        _to_chunks(k_ref, kf)
        _to_chunks(v_ref, vf)
        _to_chunks(do_ref, dof)
        dk_ref[...] = jnp.zeros_like(dk_ref)
        dv_ref[...] = jnp.zeros_like(dv_ref)
        lane = lax.broadcasted_iota(jnp.int32, (ABLK, 128), 1)

        def step(it, carry):
            n, cur, prv = _att_rows(it, nb, dil)
            valid = _att_mask(n, nb)
            valid = jnp.concatenate([valid, valid], axis=1)
            nk = valid.shape[1] // 2
            lse_b = l_ref[0, cur, :]
            dl_b = dl_ref[0, cur, :]
            head0 = lax.broadcasted_iota(jnp.int32, (nk, 128), 1) < HD
            for c in range(AHC):
                q = (qf[c, cur, :] * ATT_SCALE).astype(bf16)
                dob = dof[c, cur, :].astype(bf16)
                k = _att_keys(kf, c, cur, prv, nb)
                v = _att_keys(vf, c, cur, prv, nb)
                head = hh * (2 * AHC) + 2 * c
                pick = lambda a, h: jnp.sum(jnp.where(lane == h, a, 0.0), axis=-1, keepdims=True)
                lse_h = _per_head(nk, pick(lse_b, head), pick(lse_b, head + 1))
                dl_h = _per_head(nk, pick(dl_b, head), pick(dl_b, head + 1))
                s = _dot_nt(q, k)
                p = jnp.where(valid, jnp.exp(jnp.minimum(s - lse_h, 60.0)), 0.0)
                ds = (p * (_dot_nt(dob, v) - dl_h)).astype(bf16)
                dq_ref[0, c, cur, :] = _dot(ds, k) * ATT_SCALE
                dk2 = _dot_tn(ds, q)
                dv2 = _dot_tn(p.astype(bf16), dob)
                dk = jnp.where(head0, dk2[:nk], dk2[nk:])
                dv = jnp.where(head0, dv2[:nk], dv2[nk:])
                if nb == 1:
                    dk_ref[0, c, cur, :] += dk
                    dv_ref[0, c, cur, :] += dv
                else:
                    dk_ref[0, c, cur, :] += dk[ABLK:]
                    dv_ref[0, c, cur, :] += dv[ABLK:]
                    dk_ref[0, c, prv, :] += dk[:ABLK]
                    dv_ref[0, c, prv, :] += dv[:ABLK]

            return carry

        lax.fori_loop(0, dil * nb, step, 0, unroll=4)
        dq_out[0] = dq_ref[0].astype(bf16)
        dk_out[0] = dk_ref[0].astype(bf16)
        dv_out[0] = dv_ref[0].astype(bf16)

    col = lambda c: pl.BlockSpec((1, l, AHW), lambda bi, hh: (bi, 0, c * nhalf + hh))
    own = pl.BlockSpec((1, l, AHW), lambda bi, hh: (bi, 0, hh))
    own128 = pl.BlockSpec((1, l, 128), lambda bi, hh: (bi, 0, 0))
    chunked = pl.BlockSpec((1, AHC, l, 128), lambda bi, hh: (bi, hh, 0, 0))
    return pl.pallas_call(
        body, name=f"att_bwd{g_idx}", grid=(b, nhalf),
        in_specs=[col(1 + g_idx), col(4), col(5), own, own128, own128],
        out_specs=[chunked] * 3,
        out_shape=[jax.ShapeDtypeStruct((b, BW // 128, l, 128), bf16)] * 3,
        scratch_shapes=[pltpu.VMEM((AHC, l, 128), f32)] * 4 + [pltpu.VMEM((1, AHC, l, 128), f32)] * 3,
        compiler_params=_cp(2, 56))(proj3, proj3, proj3, do, lse_tot, delta)


CPAD = 32
CTAIL = 16
CR = 128
CSLAB = CR + 40


def _tap_windows(slab, off, mis):
    ntap = (CW - 1 - mis) // 8 + 1
    rot = (off + mis) % 8
    base = off + mis - rot
    shifted = pltpu.roll(slab, CSLAB - rot, 0) if rot else slab
    for a in range(ntap):
        yield 8 * a + mis, shifted[base + 8 * a:base + 8 * a + CR]


def _fill_glu(cv_ref, pad, l):
    pad[0:CPAD, :] = jnp.zeros((CPAD, BW), f32)
    pad[CPAD:CPAD + l, :] = cv_ref[0, :, :BW].astype(f32) * _sigmoid(cv_ref[0, :, BW:].astype(f32))
    pad[CPAD + l:, :] = jnp.zeros((CTAIL, BW), f32)


def conv_fwd(proj3, cw, cb):
    b, l, _ = proj3.shape

    def body(cv_ref, w_ref, b_ref, o_ref, pad):
        _fill_glu(cv_ref, pad, l)
        for lc in range(BW // 128):
            lanes = slice(lc * 128, (lc + 1) * 128)
            wv = w_ref[:, lanes]

            def step(c, carry):
                base = pl.multiple_of(c * CR, CR)
                slab = pad[pl.ds(base, CSLAB), lanes]
                acc = jnp.zeros((CR, 128), f32) + b_ref[:, lanes]
                for mis in range(8):
                    for k, win in _tap_windows(slab, CPAD - (CW - 1), mis):
                        acc = acc + wv[k:k + 1] * win
                o_ref[0, pl.ds(base, CR), lanes] = acc
                return carry

            lax.fori_loop(0, l // CR, step, 0)

    return pl.pallas_call(
        body, name="conv_fwd", grid=(b,),
        in_specs=[pl.BlockSpec((1, l, 2 * BW), lambda i: (i, 0, 3)),
                  pl.BlockSpec((32, BW), lambda i: (0, 0)), pl.BlockSpec((1, BW), lambda i: (0, 0))],
        out_specs=pl.BlockSpec((1, l, BW), lambda i: (i, 0, 0)),
        out_shape=jax.ShapeDtypeStruct((b, l, BW), f32),
        scratch_shapes=[pltpu.VMEM((CPAD + l + CTAIL, BW), f32)], compiler_params=_cp(1))(proj3, cw, cb)


def conv_bwd(proj3, dhc, cw):
    b, l, _ = proj3.shape

    def body(cv_ref, d_ref, w_ref, dcv_ref, dw_ref, db_ref, pad, dpad):
        i = pl.program_id(0)

        @pl.when(i == 0)
        def _():
            dw_ref[...] = jnp.zeros_like(dw_ref)
            db_ref[...] = jnp.zeros_like(db_ref)

        _fill_glu(cv_ref, pad, l)
        dpad[0:l, :] = d_ref[0]
        dpad[l:, :] = jnp.zeros((CPAD + CTAIL, BW), f32)
        db_ref[...] += jnp.sum(d_ref[0], axis=0, keepdims=True)
        for lc in range(BW // 128):
            lanes = slice(lc * 128, (lc + 1) * 128)
            glanes = slice(BW + lc * 128, BW + (lc + 1) * 128)
            wv = w_ref[:, lanes]

            for mis in range(8):
                ntap = (CW - 1 - mis) // 8 + 1

                def dw_step(c, accs, mis=mis, lanes=lanes):
                    base = pl.multiple_of(c * CR, CR)
                    slab = pad[pl.ds(base, CSLAB), lanes]
                    dv = dpad[pl.ds(base, CR), lanes]
                    return tuple(acc + (dv * win).reshape(CR // 8, 8, 128).sum(axis=0) for acc, (_, win)
                                 in zip(accs, _tap_windows(slab, CPAD - (CW - 1), mis)))

                accs = lax.fori_loop(0, l // CR, dw_step, tuple(jnp.zeros((8, 128), f32) for _ in range(ntap)))
                for a in range(ntap):
                    k = 8 * a + mis
                    dw_ref[k:k + 1, lanes] += jnp.sum(accs[a], axis=0, keepdims=True)

            def dh_step(c, carry, lanes=lanes, glanes=glanes, wv=wv):
                base = pl.multiple_of(c * CR, CR)
                slab = dpad[pl.ds(base, CSLAB), lanes]
                acc = jnp.zeros((CR, 128), f32)
                for mis in range(8):
                    for kk, win in _tap_windows(slab, 0, mis):
                        acc = acc + wv[CW - 1 - kk:CW - kk] * win
                rows = pl.ds(base, CR)
                a = cv_ref[0, rows, lanes].astype(f32)
                sg = _sigmoid(cv_ref[0, rows, glanes].astype(f32))
                dcv_ref[0, rows, lanes] = (acc * sg).astype(bf16)
                dcv_ref[0, rows, glanes] = (acc * a * sg * (1.0 - sg)).astype(bf16)
                return carry

            lax.fori_loop(0, l // CR, dh_step, 0)

    return pl.pallas_call(
        body, name="conv_bwd", grid=(b,),
        in_specs=[pl.BlockSpec((1, l, 2 * BW), lambda i: (i, 0, 3)),
                  pl.BlockSpec((1, l, BW), lambda i: (i, 0, 0)),
                  pl.BlockSpec((32, BW), lambda i: (0, 0))],
        out_specs=[pl.BlockSpec((1, l, 2 * BW), lambda i: (i, 0, 0)),
                   pl.BlockSpec((32, BW), lambda i: (0, 0)), pl.BlockSpec((1, BW), lambda i: (0, 0))],
        out_shape=[jax.ShapeDtypeStruct((b, l, 2 * BW), bf16), jax.ShapeDtypeStruct((32, BW), f32),
                   jax.ShapeDtypeStruct((1, BW), f32)],
        scratch_shapes=[pltpu.VMEM((CPAD + l + CTAIL, BW), f32), pltpu.VMEM((l + CPAD + CTAIL, BW), f32)],
        compiler_params=_cp(1))(proj3, dhc, cw)


def _head_expand():
    r = lax.broadcasted_iota(jnp.int32, (128, BW), 0)
    c = lax.broadcasted_iota(jnp.int32, (128, BW), 1) // HD
    return (r == c).astype(f32)


def _head_reduce():
    r = lax.broadcasted_iota(jnp.int32, (BW, 128), 0) // HD
    c = lax.broadcasted_iota(jnp.int32, (BW, 128), 1)
    return (r == c).astype(f32)


def _merge_common(ys_ref, o_refs, l_refs, hc_ref, g_refs, bg_ref, lng_ref, lnb_ref, wglu_ref, watt_ref, wpw_ref):
    r = {}
    ysv = ys_ref[...]
    r["ys"] = ysv
    r["ysin"] = _gelu(ysv).astype(bf16)
    z = _dot(r["ysin"], wglu_ref[...])
    r["z1"], r["sg2"] = z[:, :D], _sigmoid(z[:, D:])
    r["y_s"] = r["z1"] * r["sg2"]
    ls = [lr_[0, 0] + lr_[0, 1] for lr_ in l_refs]
    mx = jnp.maximum(jnp.maximum(ls[0], ls[1]), ls[2])
    es = [jnp.exp(v - mx) for v in ls]
    tot = es[0] + es[1] + es[2]
    r["lse_tot"] = mx + jnp.log(tot)
    e_mat = _head_expand()
    o = jnp.zeros(ysv.shape, f32)
    for e, o_ref in zip(es, o_refs):
        o = o + _dot_hi(e / tot, e_mat) * o_ref[...].astype(f32)
    r["o"] = o
    r["ob"] = o.astype(bf16)
    r["y_a"] = _dot(r["ob"], watt_ref[...])
    hc = hc_ref[...]
    mu = jnp.mean(hc, axis=-1, keepdims=True)
    xc = hc - mu
    rstd = lax.rsqrt(jnp.mean(xc * xc, axis=-1, keepdims=True) + EPS)
    r["xh"], r["rstd"] = xc * rstd, rstd
    hn = r["xh"] * lng_ref[...] + lnb_ref[...]
    r["hn"] = hn
    r["sgn"] = _sigmoid(hn)
    r["hs"] = (hn * r["sgn"]).astype(bf16)
    r["y_c"] = _dot(r["hs"], wpw_ref[...])
    r["gates"] = [_sigmoid(g_refs[k][...].astype(f32) + bg_ref[:, k * D:(k + 1) * D]) for k in range(3)]
    r["merged"] = r["gates"][0] * r["y_s"] + r["gates"][1] * r["y_a"] + r["gates"][2] * r["y_c"]
    return r


TBM = 256


def _merge_in_specs(tok, tb, lses):
    w = lambda shape: pl.BlockSpec(shape, lambda i: (0, 0), pipeline_mode=pl.Buffered(1))
    nbl = lses[0].shape[2] // tb
    return ([pl.BlockSpec((tb, D), tok), pl.BlockSpec((tb, BW), tok)]
            + [pl.BlockSpec((tb, BW), tok)] * 3
            + [pl.BlockSpec((1, 2, tb, 128), lambda i: (i // nbl, 0, i % nbl, 0))] * 3
            + [pl.BlockSpec((tb, BW), tok)]
            + [pl.BlockSpec((tb, D), lambda i, k=k: (i, 4 + k)) for k in range(3)]
            + [w((1, 3 * D)), w((1, BW)), w((1, BW)), w((BW, 2 * D)), w((BW, D)), w((BW, D)), w((D, D))])


def merge_fwd(x, ys, os_, lses, hc, proj, bg, lng, lnb, wglu, watt, wpw, wout):
    n = x.shape[0]

    def body(x_ref, ys_ref, o1, o2, o3, l1, l2, l3, hc_ref, g0, g1, g2, bg_ref, lng_ref, lnb_ref,
             wglu_ref, watt_ref, wpw_ref, wout_ref, x1_ref):
        r = _merge_common(ys_ref, (o1, o2, o3), (l1, l2, l3), hc_ref, (g0, g1, g2), bg_ref, lng_ref, lnb_ref,
                          wglu_ref, watt_ref, wpw_ref)
        x1_ref[...] = x_ref[...] + _dot(r["merged"].astype(bf16), wout_ref[...])

    tok = lambda i: (i, 0)
    return pl.pallas_call(
        body, name="merge_fwd", grid=(n // TB,), in_specs=_merge_in_specs(tok, TB, lses),
        out_specs=pl.BlockSpec((TB, D), tok), out_shape=jax.ShapeDtypeStruct((n, D), f32),
        compiler_params=_cp(1, 56))(x, ys, *os_, *lses, hc, proj, proj, proj, bg, lng, lnb, wglu, watt, wpw, wout)


def merge_bwd(dx1, ys, os_, lses, hc, proj, bg, lng, lnb, wglu, watt, wpw, wout):
    n = dx1.shape[0]

    def body(dx_ref, ys_ref, o1, o2, o3, l1, l2, l3, hc_ref, g0, g1, g2, bg_ref, lng_ref, lnb_ref,
             wglu_ref, watt_ref, wpw_ref, wout_ref,
             dys_ref, do_ref, delta_ref, ltot_ref, dhc_ref, dgate_ref, ysin_ref, dz_ref, ob_ref, dya_ref,
             hs_ref, dyc_ref, mg_ref, dbg_ref, dlng_ref, dlnb_ref):
        i = pl.program_id(0)

        @pl.when(i == 0)
        def _():
            dbg_ref[...] = jnp.zeros_like(dbg_ref)
            dlng_ref[...] = jnp.zeros_like(dlng_ref)
            dlnb_ref[...] = jnp.zeros_like(dlnb_ref)

        r = _merge_common(ys_ref, (o1, o2, o3), (l1, l2, l3), hc_ref, (g0, g1, g2), bg_ref, lng_ref, lnb_ref,
                          wglu_ref, watt_ref, wpw_ref)
        mg_ref[...] = r["merged"].astype(bf16)
        ysin_ref[...] = r["ysin"]
        ob_ref[...] = r["ob"]
        hs_ref[...] = r["hs"]
        ltot_ref[...] = r["lse_tot"]
        dm = _dot_nt(dx_ref[...].astype(bf16), wout_ref[...])
        ys3 = (r["y_s"], r["y_a"], r["y_c"])
        for k in range(3):
            gk = r["gates"][k]
            dgr = dm * ys3[k] * gk * (1.0 - gk)
            dgate_ref[:, k * D:(k + 1) * D] = dgr.astype(bf16)
            dbg_ref[:, k * D:(k + 1) * D] += jnp.sum(dgr, axis=0, keepdims=True)
        dy_s = dm * r["gates"][0]
        sg2 = r["sg2"]
        dz = jnp.concatenate([dy_s * sg2, dy_s * r["z1"] * sg2 * (1.0 - sg2)], axis=1).astype(bf16)
        dz_ref[...] = dz
        dys_ref[...] = _dot_nt(dz, wglu_ref[...]) * _gelu_grad(r["ys"])
        dya = (dm * r["gates"][1]).astype(bf16)
        dya_ref[...] = dya
        do = _dot_nt(dya, watt_ref[...])
        do_ref[...] = do.astype(bf16)
        delta_ref[...] = _dot_hi(do * r["o"], _head_reduce())
        dyc = (dm * r["gates"][2]).astype(bf16)
        dyc_ref[...] = dyc
        sgn, hn = r["sgn"], r["hn"]
        dhn = _dot_nt(dyc, wpw_ref[...]) * sgn * (1.0 + hn * (1.0 - sgn))
        dlng_ref[...] += jnp.sum(dhn * r["xh"], axis=0, keepdims=True)
        dlnb_ref[...] += jnp.sum(dhn, axis=0, keepdims=True)
        dxh = dhn * lng_ref[...]
        xh = r["xh"]
        dhc_ref[...] = r["rstd"] * (dxh - jnp.mean(dxh, axis=-1, keepdims=True)
                                    - xh * jnp.mean(dxh * xh, axis=-1, keepdims=True))

    tok = lambda i: (i, 0)
    fix = lambda i: (0, 0)
    outs = [("dys", BW, f32), ("do", BW, bf16), ("delta", 128, f32), ("lse_tot", 128, f32), ("dhc", BW, f32),
            ("dgate", 3 * D, bf16), ("ysin", BW, bf16), ("dz", 2 * D, bf16), ("ob", BW, bf16), ("dya", D, bf16),
            ("hs", BW, bf16), ("dyc", D, bf16), ("merged", D, bf16)]
    small = [("dbg", 3 * D), ("dlng", BW), ("dlnb", BW)]
    res = pl.pallas_call(
        body, name="merge_bwd", grid=(n // TBM,), in_specs=_merge_in_specs(tok, TBM, lses),
        out_specs=[pl.BlockSpec((TBM, w), tok) for _, w, _ in outs] + [pl.BlockSpec((1, w), fix) for _, w in small],
        out_shape=[jax.ShapeDtypeStruct((n, w), dt) for _, w, dt in outs]
        + [jax.ShapeDtypeStruct((1, w), f32) for _, w in small],
        compiler_params=_cp(1, 56))(dx1, ys, *os_, *lses, hc, proj, proj, proj, bg, lng, lnb, wglu, watt, wpw, wout)
    return dict(zip([k for k, _, _ in outs] + [k for k, _ in small], res))


def assemble_dproj(du, dqs, dks, dvs, dcv, dgate):
    b, l, _ = du.shape
    nck = BW // 128

    def body(du_ref, q1, q2, q3, k1, k2, k3, v1, v2, v3, cv_ref, g_ref, o_ref):
        o_ref[0, :, 0:BW] = du_ref[0]
        for c in range(nck):
            for j, qr in enumerate((q1, q2, q3)):
                o_ref[0, :, (1 + j) * BW + c * 128:(1 + j) * BW + (c + 1) * 128] = qr[0, c]
            add3 = lambda r1, r2, r3: (r1[0, c].astype(f32) + r2[0, c].astype(f32) + r3[0, c].astype(f32)).astype(bf16)
            o_ref[0, :, 4 * BW + c * 128:4 * BW + (c + 1) * 128] = add3(k1, k2, k3)
            o_ref[0, :, 5 * BW + c * 128:5 * BW + (c + 1) * 128] = add3(v1, v2, v3)
        o_ref[0, :, 6 * BW:8 * BW] = cv_ref[0]
        o_ref[0, :, 8 * BW:] = g_ref[0]

    t = lambda w: pl.BlockSpec((1, TB, w), lambda bi, i: (bi, i, 0))
    ck = pl.BlockSpec((1, nck, TB, 128), lambda bi, i: (bi, 0, i, 0))
    return pl.pallas_call(
        body, name="assemble_dproj", grid=(b, l // TB),
        in_specs=[t(BW)] + [ck] * 9 + [t(2 * BW), t(3 * D)], out_specs=t(INC),
        out_shape=jax.ShapeDtypeStruct((b, l, INC), bf16), compiler_params=_cp(2))(du, *dqs, *dks, *dvs, dcv, dgate)


def _me():
    return lax.axis_index("x"), lax.axis_index("y"), lax.axis_index("c")


def _peers():
    x, y, c = _me()
    return [(x, y, 1 - c), (1 - x, y, c), (1 - x, y, 1 - c), (x, 1 - y, c), (x, 1 - y, 1 - c),
            (1 - x, 1 - y, c), (1 - x, 1 - y, 1 - c)]


def _rank(p):
    return 4 * p[0] + 2 * p[1] + p[2]


def allgather(arrs, name):
    na = len(arrs)
    units = [(a, j) for a in range(na) for j in range(arrs[a].shape[0])]
    nu = len(units)

    def body(*refs):
        ins, outs = refs[:na], refs[na:2 * na]
        send, recv, loc = refs[2 * na:]
        me = _rank(_me())
        local, remote = [], []
        for u, (a, j) in enumerate(units):
            own = pltpu.make_async_copy(ins[a].at[j], outs[a].at[j, me], loc.at[u])
            own.start()
            local.append(own)
        for u, (a, j) in enumerate(units):
            for k, p in enumerate(_peers()):
                cp = pltpu.make_async_remote_copy(src_ref=ins[a].at[j], dst_ref=outs[a].at[j, me],
                                                  send_sem=send.at[u, k], recv_sem=recv.at[u, k],
                                                  device_id=p, device_id_type=MESH)
                cp.start()
                remote.append(cp)
        for cp in local:
            cp.wait()
        for cp in remote:
            cp.wait()

    return pl.pallas_call(
        body, name=name, in_specs=[ANY] * na, out_specs=[ANY] * na,
        out_shape=[jax.ShapeDtypeStruct((a.shape[0], NDEV) + a.shape[1:], a.dtype) for a in arrs],
        scratch_shapes=[pltpu.SemaphoreType.DMA((nu, NDEV - 1)), pltpu.SemaphoreType.DMA((nu, NDEV - 1)),
                        pltpu.SemaphoreType.DMA((nu,))])(*arrs)


HBM = pl.BlockSpec(memory_space=pltpu.HBM)
SEM = pl.BlockSpec(memory_space=pltpu.SEMAPHORE)
_EFFECT = pltpu.SideEffectType.DATAFLOW_SIDE_EFFECTING


def _rank_slot(ref, r):
    if ref.shape[0] == NDEV:
        return ref.at[r]
    n = ref.shape[2] // 2
    return ref.at[r // 2, :, pl.ds(pl.multiple_of((r % 2) * n, 128), n)]


def _push_copies(srcs, lands, send, recv, scatter):
    me = _rank(_me())
    out = []
    for i in range(len(srcs)):
        for k, p in enumerate(_peers()):
            src = _rank_slot(srcs[i], _rank(p)) if scatter else srcs[i]
            dst = lands[i].at[k] if scatter else _rank_slot(lands[i], me)
            j = i * (NDEV - 1) + k
            out.append(pltpu.make_async_remote_copy(src_ref=src, dst_ref=dst, send_sem=send.at[j],
                                                    recv_sem=recv.at[j], device_id=p, device_id_type=MESH))
    return out


def push_start(srcs, lands, scatter, name, token):
    n = len(srcs)
    token = jnp.zeros((8, 128), f32) if token is None else token

    def body(*refs):
        for cp in _push_copies(refs[:n], refs[n:2 * n], refs[2 * n + 1], refs[2 * n + 2], scatter):
            cp.start()
        refs[-1][...] = refs[2 * n][...]

    sems = pltpu.SemaphoreType.DMA((n * (NDEV - 1),))
    vmem = pl.BlockSpec(memory_space=pltpu.VMEM)
    res = pl.pallas_call(
        body, name=name, in_specs=[HBM] * (2 * n) + [vmem], out_specs=[SEM, SEM] + [HBM] * (2 * n) + [vmem],
        out_shape=[sems, sems] + [pltpu.HBM(a.shape, a.dtype) for a in list(srcs) + list(lands)]
        + [jax.ShapeDtypeStruct((8, 128), f32)],
        input_output_aliases={i: 2 + i for i in range(2 * n)},
        compiler_params=pltpu.CompilerParams(has_side_effects=_EFFECT),
    )(*[pltpu.with_memory_space_constraint(a, pltpu.HBM) for a in list(srcs) + list(lands)], token)
    return res[0], res[1], res[2:2 + n], res[2 + n:2 + 2 * n], res[-1]


def push_wait(send, recv, srcs, lands, after, scatter, name):
    n = len(srcs)

    def body(*refs):
        for cp in _push_copies(refs[:n], refs[n:2 * n], refs[2 * n], refs[2 * n + 1], scatter):
            cp.wait_send()
            cp.wait_recv()

    res = pl.pallas_call(
        body, name=name, in_specs=[HBM] * (2 * n) + [SEM, SEM, ANY], out_specs=[HBM] * (2 * n),
        out_shape=[pltpu.HBM(a.shape, a.dtype) for a in list(srcs) + list(lands)],
        input_output_aliases={i: i for i in range(2 * n)},
        compiler_params=pltpu.CompilerParams(has_side_effects=_EFFECT),
    )(*srcs, *lands, send, recv, after)
    return res[:n], res[n:]


_C1 = 1.0 / (1.0 - ADAM_B1 ** ADAM_STEP)
_C2 = 1.0 / (1.0 - ADAM_B2 ** ADAM_STEP)


def _adamw(w, g, m, v):
    m = ADAM_B1 * m + (1.0 - ADAM_B1) * g
    v = ADAM_B2 * v + (1.0 - ADAM_B2) * (g * g)
    delta = -ADAM_LR * ((m * _C1) / (jnp.sqrt(v * _C2) + ADAM_EPS) + ADAM_WD * w)
    return delta, m, v


def adam_big(lands, owns, w, m, v, name):
    _, k, n = lands[0].shape
    tk = k
    while tk * n * 2 * NDEV > 2 * 1024 * 1024 and tk % 16 == 0:
        tk //= 2

    def body(*refs):
        l_refs, o_refs = refs[:DEPTH], refs[DEPTH:2 * DEPTH]
        w_ref, m_ref, v_ref, g_ref, d_ref, nm_ref, nv_ref = refs[2 * DEPTH:]
        for l in range(DEPTH):
            g = o_refs[l][...].astype(f32)
            for s in range(NDEV - 1):
                g = g + l_refs[l][s].astype(f32)
            d, nm, nv = _adamw(w_ref[l], g, m_ref[l], v_ref[l])
            g_ref[l], d_ref[l], nm_ref[l], nv_ref[l] = g, d, nm, nv

    blk = pl.BlockSpec((DEPTH, tk, n), lambda i: (0, i, 0))
    return pl.pallas_call(
        body, name=name, grid=(k // tk,),
        in_specs=[pl.BlockSpec((NDEV - 1, tk, n), lambda i: (0, i, 0))] * DEPTH
        + [pl.BlockSpec((tk, n), lambda i: (i, 0))] * DEPTH + [blk, blk, blk],
        out_specs=[blk] * 4, out_shape=[jax.ShapeDtypeStruct(w.shape, f32)] * 4,
        compiler_params=_cp(1))(*lands, *owns, w, m, v)


def adam_small(gath, w, m, v):
    r = w.shape[0]
    tr = 512

    def body(g_ref, w_ref, m_ref, v_ref, go_ref, d_ref, nm_ref, nv_ref):
        g = g_ref[0]
        for s in range(1, NDEV):
            g = g + g_ref[s]
        d, nm, nv = _adamw(w_ref[...], g, m_ref[...], v_ref[...])
        go_ref[...], d_ref[...], nm_ref[...], nv_ref[...] = g, d, nm, nv

    blk = pl.BlockSpec((tr, 128), lambda i: (i, 0))
    return pl.pallas_call(
        body, name="adam_small", grid=(r // tr,),
        in_specs=[pl.BlockSpec((NDEV, tr, 128), lambda i: (0, i, 0)), blk, blk, blk],
        out_specs=[blk] * 4, out_shape=[jax.ShapeDtypeStruct((r, 128), f32)] * 4,
        compiler_params=_cp(1))(gath, w, m, v)


SMALL = ["norm1_g", "b_gate", "ssm_lambda_re", "ssm_lambda_im", "ssm_log_dt", "ssm_b_re", "ssm_b_im",
         "ssm_c_re", "ssm_c_im", "ssm_d", "conv_w", "conv_b", "conv_ln_g", "conv_ln_b", "norm2_g", "final_g"]
BIG = ["w_in", "w_ssm_glu", "w_att_up", "w_conv_pw2", "w_out", "w_ffn_in", "w_ffn_out"]
ORDER = ["norm1_g", "w_in", "b_gate", "ssm_lambda_re", "ssm_lambda_im", "ssm_log_dt", "ssm_b_re", "ssm_b_im",
         "ssm_c_re", "ssm_c_im", "ssm_d", "w_ssm_glu", "w_att_up", "conv_w", "conv_b", "conv_ln_g", "conv_ln_b",
         "w_conv_pw2", "w_out", "norm2_g", "w_ffn_in", "w_ffn_out", "final_g"]
PACK_ROWS = 2560


def _pack(arrs):
    flat = jnp.concatenate([a.reshape(-1).astype(f32) for a in arrs])
    return jnp.pad(flat, (0, PACK_ROWS * 128 - flat.shape[0])).reshape(PACK_ROWS, 128)


def _unpack(pack, shapes):
    flat = pack.reshape(-1)
    out, off = [], 0
    for s in shapes:
        sz = math.prod(s)
        out.append(flat[off:off + sz].reshape(s))
        off += sz
    return out


def _bt(b):
    return b.transpose(2, 0, 1).reshape(GH, NSTATE)


def _bt_inv(bt):
    return bt.reshape(GH, NG, NS).transpose(1, 2, 0)


def _ct(c):
    return c.transpose(1, 0, 2).reshape(GH, NSTATE)


def _ct_inv(ct):
    return ct.reshape(GH, NG, NS).transpose(1, 0, 2)


def local_step(x, loss_target, P, weights, on_grads, start_token=None):
    bsz, seq, _ = x.shape
    n = bsz * seq

    def natural(g3):
        return g3.transpose(1, 0, 2).reshape(g3.shape[1], NDEV * g3.shape[2])

    tokens = [] if start_token is None else [start_token]

    def after_pushes(a):
        while tokens:
            a = a + tokens.pop()[0:1, 0:1]
        return a

    def pushed(tok):
        if tok is not None:
            tokens.append(tok)

    xs = x.reshape(n, D)
    saved = []
    conv_w_pad = None
    for l in range(DEPTH):
        S = {"x": xs}
        h1 = rms_fwd(xs, after_pushes(P["norm1_g"][l][None]))
        G = dict(weights(l, "in", h1))
        if conv_w_pad is None:
            conv_w_full = G["conv_w"].transpose(1, 2, 0, 3).reshape(DEPTH, CW, BW)
            conv_w_pad = jnp.pad(conv_w_full, ((0, 0), (0, 1), (0, 0)))
        w_in4 = G["w_in"][None]
        proj = inproj(h1, w_in4, 0)
        proj3 = proj.reshape(bsz, seq, INC)
        lr = P["ssm_lambda_re"][l].reshape(1, NSTATE)
        li = P["ssm_lambda_im"][l].reshape(1, NSTATE)
        ld = jnp.repeat(P["ssm_log_dt"][l], NS).reshape(1, NSTATE)
        btr, bti = _bt(P["ssm_b_re"][l]), _bt(P["ssm_b_im"][l])
        t8, bb, cb = s5_params(lr, li, ld, btr, bti, _ct(P["ssm_c_re"][l]), _ct(P["ssm_c_im"][l]))
        dskip = P["ssm_d"][l][None]
        ys, s_re, s_im = s5_fwd(proj3, t8, bb, cb, dskip)
        att = [att_fwd(proj3, gi, dil) for gi, (_, dil) in enumerate(PATTERNS)]
        hc = conv_fwd(proj3, conv_w_pad[l], P["conv_b"][l][None])
        G.update(weights(l, "mix", hc))
        wts = dict(wglu=natural(G["w_ssm_glu"]), watt=natural(G["w_att_up"]),
                   wpw=natural(G["w_conv_pw2"]), wout=G["w_out"].reshape(D, D))
        mi = dict(ys=ys.reshape(n, BW), os_=[a[0].reshape(n, BW) for a in att],
                  lses=[a[1] for a in att], hc=hc.reshape(n, BW),
                  proj=proj, bg=P["b_gate"][l][None], lng=P["conv_ln_g"][l][None], lnb=P["conv_ln_b"][l][None],
                  **wts)
        x1 = merge_fwd(xs, **mi)
        G.update(weights(l, "ffn", x1))
        w_ffn = (G["w_ffn_in"][None], G["w_ffn_out"][None])
        x2, z1s, z2s = ffn_fwd(x1, P["norm2_g"][l][None], *w_ffn, 0)
        S.update(h1=h1, proj=proj, proj3=proj3, tabs=(s_re, s_im, t8, bb, cb), mi=mi, x1=x1, w_in4=w_in4, w_ffn=w_ffn,
                 zs=(z1s, z2s),
                 sp=(lr, li, ld, btr, bti), dskip=dskip)
        saved.append(S)
        xs = x2

    loss8, dx, dfinal = loss_head(xs, P["final_g"][None], loss_target.reshape(n, D))

    small_g = {k: [None] * DEPTH for k in SMALL if k != "final_g"}
    tokblk = lambda w: pl.BlockSpec((1024, w), lambda s, i: (i, 0))
    colblk = lambda w: pl.BlockSpec((1024, w), lambda s, i: (i, s))
    sh3blk = lambda w: pl.BlockSpec((1, 1024, w), lambda s, i: (s, i, 0))
    for l in reversed(range(DEPTH)):
        S = saved[l]
        g2 = P["norm2_g"][l][None]
        dx1, h2, dz1, dz2, a4, dg2 = ffn_bwd(S["x1"], after_pushes(g2), *S["w_ffn"], 0, dx, *S["zs"])
        small_g["norm2_g"][l] = dg2
        dwa = mm_tn(h2, dz1, tokblk(D), sh3blk(NSH_FF), 4, D, NSH_FF, n, "dw_ffn_in_a")
        dwb = mm_tn(h2, dz2, tokblk(D), sh3blk(NSH_FF), 4, D, NSH_FF, n, "dw_ffn_in_b")
        pushed(on_grads(l, "ffn", dict(
            w_ffn_in=jnp.concatenate([dwa, dwb], axis=0),
            w_ffn_out=mm_tn(a4, dx, sh3blk(NSH_FF), tokblk(D), 4, NSH_FF, D, n,
                            "dw_ffn_out").reshape(NDEV, NSH_FF // 2, D))))
        mb = merge_bwd(dx1, **dict(S["mi"], lng=after_pushes(S["mi"]["lng"])))
        small_g["b_gate"][l], small_g["conv_ln_g"][l], small_g["conv_ln_b"][l] = mb["dbg"], mb["dlng"], mb["dlnb"]
        dws = dw_mix(mb["ysin"], mb["dz"], mb["ob"], mb["dya"], mb["hs"], mb["dyc"], mb["merged"], dx1)
        pushed(on_grads(l, "mix", dict(zip(("w_ssm_glu", "w_att_up", "w_conv_pw2", "w_out"), dws))))
        dcv, dcw, dcb = conv_bwd(S["proj3"], mb["dhc"].reshape(bsz, seq, BW), after_pushes(conv_w_pad[l]))
        small_g["conv_w"][l] = dcw[:CW].reshape(CW, NDEV, BW // NDEV).transpose(1, 0, 2)
        small_g["conv_b"][l] = dcb
        ab = [att_bwd(S["proj3"], mb["do"].reshape(bsz, seq, BW), mb["lse_tot"].reshape(bsz, seq, 128),
                      mb["delta"].reshape(bsz, seq, 128), gi, dil) for gi, (_, dil) in enumerate(PATTERNS)]
        du, d_a, d_bb, d_cb, d_d = s5_bwd(S["proj3"], mb["dys"].reshape(bsz, seq, BW), *S["tabs"], S["dskip"])
        lr, li, ld, btr, bti = S["sp"]
        dlr, dli, dld, dbt, dct = s5_params_bwd(lr, li, ld, btr, bti, d_a, d_bb, d_cb)
        small_g["ssm_lambda_re"][l], small_g["ssm_lambda_im"][l] = dlr.reshape(NG, NS), dli.reshape(NG, NS)
        small_g["ssm_log_dt"][l] = dld[0, :NG]
        small_g["ssm_b_re"][l], small_g["ssm_b_im"][l] = _bt_inv(dbt[0]), _bt_inv(dbt[1])
        small_g["ssm_c_re"][l], small_g["ssm_c_im"][l] = _ct_inv(dct[0]), _ct_inv(dct[1])
        small_g["ssm_d"][l] = d_d
        dproj = assemble_dproj(du, [a[0] for a in ab], [a[1] for a in ab], [a[2] for a in ab],
                               dcv, mb["dgate"].reshape(bsz, seq, 3 * D)).reshape(n, INC)
        nblk, wblk = S["w_in4"].shape[1], S["w_in4"].shape[3]
        pushed(on_grads(l, "in", dict(w_in=mm_tn(S["h1"], dproj, tokblk(D), colblk(wblk), nblk, D, wblk, n, "dw_in"))))
        if l == 0:
            pushed(on_grads(l, "small", dict(small_g=small_g, loss8=loss8, dfinal=dfinal)))
        dx, dg1 = inproj_bwd(dproj, S["w_in4"], 0, S["x"], after_pushes(P["norm1_g"][l][None]), dx1)
        small_g["norm1_g"][l] = dg1
    return loss8, dx, dfinal, small_g


def kernel(x, norm1_g, w_in, b_gate, ssm_lambda_re, ssm_lambda_im, ssm_log_dt, ssm_b_re, ssm_b_im, ssm_c_re, ssm_c_im, ssm_d, w_ssm_glu, w_att_up, conv_w, conv_b, conv_ln_g, conv_ln_b, w_conv_pw2, w_out, norm2_g, w_ffn_in, w_ffn_out, final_g, loss_target, m_norm1_g, m_w_in, m_b_gate, m_ssm_lambda_re, m_ssm_lambda_im, m_ssm_log_dt, m_ssm_b_re, m_ssm_b_im, m_ssm_c_re, m_ssm_c_im, m_ssm_d, m_w_ssm_glu, m_w_att_up, m_conv_w, m_conv_b, m_conv_ln_g, m_conv_ln_b, m_w_conv_pw2, m_w_out, m_norm2_g, m_w_ffn_in, m_w_ffn_out, m_final_g, v_norm1_g, v_w_in, v_b_gate, v_ssm_lambda_re, v_ssm_lambda_im, v_ssm_log_dt, v_ssm_b_re, v_ssm_b_im, v_ssm_c_re, v_ssm_c_im, v_ssm_d, v_w_ssm_glu, v_w_att_up, v_conv_w, v_conv_b, v_conv_ln_g, v_conv_ln_b, v_w_conv_pw2, v_w_out, v_norm2_g, v_w_ffn_in, v_w_ffn_out, v_final_g):
    args = dict(locals())
    W = {k: args[k] for k in ORDER}
    M = {k: args["m_" + k] for k in ORDER}
    V = {k: args["v_" + k] for k in ORDER}
    bsz, seq, _ = x.shape
    n = bsz * seq
    me = 4 * lax.axis_index("x") + 2 * lax.axis_index("y") + lax.axis_index("c")

    groups = {"in": ["w_in"], "mix": ["w_ssm_glu", "w_att_up", "w_conv_pw2", "w_out"], "ffn": ["w_ffn_in", "w_ffn_out"]}
    wb = {k: W[k].astype(bf16) for k in BIG}

    def landing(shard, paired=False):
        if paired:
            k_, n_ = shard.shape
            return lax.dynamic_update_slice(lax.empty((NDEV // 2, k_, 2 * n_), shard.dtype), shard[None],
                                            (me // 2, 0, (me % 2) * n_))
        return lax.dynamic_update_index_in_dim(lax.empty((NDEV,) + shard.shape, shard.dtype), shard, me, 0)

    def own_part(by_rank):
        if by_rank.shape[0] == NDEV:
            return lax.dynamic_index_in_dim(by_rank, me, 0, keepdims=False)
        n_ = by_rank.shape[2] // 2
        return lax.dynamic_slice(by_rank, (me // 2, 0, (me % 2) * n_), (1, by_rank.shape[1], n_))[0]

    plan = [("gather_a", [("w_in", 0), ("conv_w", None)]),
            ("gather_b", [(k, 0) for k in groups["mix"] + groups["ffn"]]),
            ("gather_c", [(k, 1) for k in BIG])]
    pending, token = {}, None
    for name, items in plan:
        shards = [conv_w if l is None else wb[k][l] for k, l in items]
        lands = [landing(s, k == "w_in") for (k, _), s in zip(items, shards)]
        send, recv, s_thru, l_thru, token = push_start(shards, lands, False, name, token)
        pending[name] = (send, recv, s_thru, l_thru, items)
    gathered = {}

    def weights(l, group, after):
        name = "gather_c" if l == 1 else ("gather_a" if group == "in" else "gather_b")
        if name in pending:
            send, recv, s_thru, l_thru, items = pending.pop(name)
            for item, arr in zip(items, push_wait(send, recv, s_thru, l_thru, after, False, name + "_wait")[1]):
                gathered[item] = arr
        res = {k: gathered[(k, l)] for k in groups[group]}
        if group == "in":
            res["conv_w"] = gathered[("conv_w", None)]
        return res

    big_g = {k: [None] * DEPTH for k in BIG}
    flights = []

    def start_exchange(items, name):
        parts = [big_g[k][l] for k, l in items]
        part = lambda p: p.shape[1:] if p.shape[0] == NDEV else (p.shape[1], p.shape[2] // 2)
        lands = [lax.empty((NDEV - 1,) + part(p), p.dtype) for p in parts]
        send, recv, s_thru, l_thru, tok = push_start(parts, lands, True, name, None)
        flights.append((send, recv, s_thru, l_thru, items, name))
        return tok

    names = [k for k in SMALL if k != "final_g"]
    shapes = [(DEPTH, NDEV, CW, BW // NDEV) if k == "conv_w" else W[k].shape for k in names] + [(D,), (1,)]
    small_flight = []

    def start_small(small_g, loss8, dfinal):
        sg_ = dict(small_g, norm1_g=[jnp.zeros((1, D), f32), small_g["norm1_g"][1]])
        gpack = _pack([jnp.stack([g.reshape(shapes[i][1:]) for g in sg_[k]]) for i, k in enumerate(names)]
                      + [dfinal, loss8[0, :1]])
        send, recv, s_thru, l_thru, tok = push_start([gpack], [landing(gpack)], False, "gather_small", None)
        small_flight.append((send, recv, s_thru, l_thru))
        return tok

    def on_grads(l, group, grads):
        if group == "small":
            return start_small(**grads)
        for k, g in grads.items():
            big_g[k][l] = g
        if l == 1 and group == "in":
            return start_exchange([(k, 1) for k in BIG], "exchange_l1")
        if l == 0:
            return start_exchange([(k, 0) for k in groups[group]], "exchange_l0_" + group)
        return None

    loss8, dx, dfinal, small_g = local_step(x, loss_target, W, weights, on_grads, token)

    landed, own = {}, {}
    for send, recv, s_thru, l_thru, items, name in flights:
        srcs, lands = push_wait(send, recv, s_thru, l_thru, dx, True, name + "_wait")
        for item, src, land in zip(items, srcs, lands):
            landed[item] = land
            own[item] = own_part(src)
    out = {}
    for k in BIG:
        items = [(k, l) for l in range(DEPTH)]
        out[k] = adam_big([landed[i] for i in items], [own[i] for i in items], W[k], M[k], V[k], "adam_" + k)

    def wpack(src):
        parts = [jnp.broadcast_to(src[k][:, None], shapes[i]) if k == "conv_w" else src[k] for i, k in enumerate(names)]
        return _pack(parts + [src["final_g"], jnp.ones((1,), f32)])

    send, recv, s_thru, l_thru = small_flight[0]
    gall = push_wait(send, recv, s_thru, l_thru, dx, False, "gather_small_wait")[1][0]
    (late,) = allgather([small_g["norm1_g"][0].reshape(1, D // 128, 128)], "allgather_late")
    gall = lax.dynamic_update_slice(gall, late[0], (0, 0, 0))
    sg, sd, sm, sv = [_unpack(p, shapes) for p in adam_small(gall, wpack(W), wpack(M), wpack(V))]
    for i, k in enumerate(names + ["final_g"]):
        vals = [t[i] for t in (sg, sd, sm, sv)]
        if k == "conv_w":
            vals = [lax.dynamic_index_in_dim(t, me, axis=1, keepdims=False) for t in vals]
        out[k] = vals
    loss = sg[-1].reshape(())

    res = [loss, dx.reshape(bsz, seq, D)]
    for j in range(4):
        res += [out[k][j] for k in ORDER]
    return tuple(res)
```

```python
import functools
import math

import jax
import jax.numpy as jnp
from jax import lax
from jax.experimental import pallas as pl
from jax.experimental.pallas import tpu as pltpu

f32 = jnp.float32
bf16 = jnp.bfloat16

D = 1024
DEPTH = 2
EPS = 1e-6
BW = 512
NG = 32
GH = 16
NS = 64
NSTATE = NG * NS
HD = 64
NH = 8
PATTERNS = ((128, 1), (512, 4), (2048, 16))
ABLK = 128
ATT_SCALE = HD ** -0.5
CW = 31
DFF = 2816
INC = 7168
NDEV = 8
NSH_IN = INC // NDEV
NSH_FF = 2 * DFF // NDEV
ADAM_LR, ADAM_B1, ADAM_B2, ADAM_EPS, ADAM_WD, ADAM_STEP = 0.001, 0.9, 0.999, 1e-08, 0.01, 10

TB = 512
SJ = 4
SW = NSTATE // SJ
SU = BW // SJ
NEG = -1e30
MESH = pl.DeviceIdType.MESH
ANY = pl.BlockSpec(memory_space=pl.ANY)


def _cp(n_axes, vmem_mb=48):
    return pltpu.CompilerParams(dimension_semantics=("arbitrary",) * n_axes,
                                vmem_limit_bytes=vmem_mb * 1024 * 1024)


def _dot(a, b):
    return jnp.dot(a, b, preferred_element_type=f32)


def _dot_nt(a, b):
    return lax.dot_general(a, b, (((1,), (1,)), ((), ())), preferred_element_type=f32)


def _dot_tn(a, b):
    return lax.dot_general(a, b, (((0,), (0,)), ((), ())), preferred_element_type=f32)


def _dot_hi(a, b):
    return jnp.dot(a, b, precision=lax.Precision.HIGHEST, preferred_element_type=f32)


def _sigmoid(x):
    return 1.0 / (1.0 + jnp.exp(-x))


_GC = math.sqrt(2.0 / math.pi)


def _gelu(x):
    return 0.5 * x * (1.0 + jnp.tanh(_GC * (x + 0.044715 * x * x * x)))


def _gelu_grad(x):
    t = jnp.tanh(_GC * (x + 0.044715 * x * x * x))
    return 0.5 * (1.0 + t) + 0.5 * x * (1.0 - t * t) * _GC * (1.0 + 3.0 * 0.044715 * x * x)


def _rms_stats(x):
    return lax.rsqrt(jnp.mean(x * x, axis=-1, keepdims=True) + EPS)


def _rms_bwd(x, g, dh):
    r = _rms_stats(x)
    dyg = dh * g
    dx = r * dyg - x * (r * r * r) * jnp.mean(dyg * x, axis=-1, keepdims=True)
    dg = jnp.sum(dh * x * r, axis=0, keepdims=True)
    return dx, dg


def rms_fwd(x, g):
    n = x.shape[0]

    def body(x_ref, g_ref, h_ref):
        xv = x_ref[...]
        h_ref[...] = (xv * _rms_stats(xv) * g_ref[...]).astype(bf16)

    return pl.pallas_call(
        body, name="rms_fwd", grid=(n // TB,),
        in_specs=[pl.BlockSpec((TB, D), lambda i: (i, 0)), pl.BlockSpec((1, D), lambda i: (0, 0))],
        out_specs=pl.BlockSpec((TB, D), lambda i: (i, 0)),
        out_shape=jax.ShapeDtypeStruct((n, D), bf16), compiler_params=_cp(1))(x, g)


def inproj(h, w4, layer):
    n = h.shape[0]
    tm = 1024
    nblk, wblk = w4.shape[1], w4.shape[3]

    def body(h_ref, w_ref, o_ref):
        o_ref[...] = _dot(h_ref[...], w_ref[0, 0]).astype(bf16)

    return pl.pallas_call(
        body, name="inproj", grid=(nblk, n // tm),
        in_specs=[pl.BlockSpec((tm, D), lambda s, i: (i, 0)),
                  pl.BlockSpec((1, 1, D, wblk), lambda s, i: (layer, s, 0, 0))],
        out_specs=pl.BlockSpec((tm, wblk), lambda s, i: (i, s)),
        out_shape=jax.ShapeDtypeStruct((n, INC), bf16), compiler_params=_cp(2))(h, w4)


def inproj_bwd(dproj, w4, layer, x, g, dres):
    n = x.shape[0]
    tm = 1024
    nblk, wblk = w4.shape[1], w4.shape[3]

    def body(dp_ref, w_ref, x_ref, g_ref, dr_ref, dx_ref, dg_ref, acc):
        i, s = pl.program_id(0), pl.program_id(1)

        @pl.when(s == 0)
        def _():
            acc[...] = jnp.zeros_like(acc)

        @pl.when((s == 0) & (i == 0))
        def _():
            dg_ref[...] = jnp.zeros_like(dg_ref)

        acc[...] += _dot_nt(dp_ref[...], w_ref[0, 0])

        @pl.when(s == nblk - 1)
        def _():
            dx, dg = _rms_bwd(x_ref[...], g_ref[...], acc[...])
            dx_ref[...] = dr_ref[...] + dx
            dg_ref[...] += dg

    return pl.pallas_call(
        body, name="inproj_bwd", grid=(n // tm, nblk),
        in_specs=[pl.BlockSpec((tm, wblk), lambda i, s: (i, s)),
                  pl.BlockSpec((1, 1, D, wblk), lambda i, s: (layer, s, 0, 0)),
                  pl.BlockSpec((tm, D), lambda i, s: (i, 0)),
                  pl.BlockSpec((1, D), lambda i, s: (0, 0)),
                  pl.BlockSpec((tm, D), lambda i, s: (i, 0))],
        out_specs=[pl.BlockSpec((tm, D), lambda i, s: (i, 0)), pl.BlockSpec((1, D), lambda i, s: (0, 0))],
        out_shape=[jax.ShapeDtypeStruct((n, D), f32), jax.ShapeDtypeStruct((1, D), f32)],
        scratch_shapes=[pltpu.VMEM((tm, D), f32)], compiler_params=_cp(2))(dproj, w4, x, g, dres)


def mm_tn(a, b, a_spec, b_spec, n_sh, ka, nb, m, name):
    tm = 1024

    def body(a_ref, b_ref, o_ref, acc):
        i = pl.program_id(1)

        @pl.when(i == 0)
        def _():
            acc[...] = jnp.zeros_like(acc)

        av = a_ref[...].reshape(tm, ka).astype(bf16)
        bv = b_ref[...].reshape(tm, nb).astype(bf16)
        acc[...] += _dot_tn(av, bv)

        @pl.when(i == m // tm - 1)
        def _():
            o_ref[0] = acc[...].astype(bf16)

    return pl.pallas_call(
        body, name=name, grid=(n_sh, m // tm), in_specs=[a_spec, b_spec],
        out_specs=pl.BlockSpec((1, ka, nb), lambda s, i: (s, 0, 0)),
        out_shape=jax.ShapeDtypeStruct((n_sh, ka, nb), bf16),
        scratch_shapes=[pltpu.VMEM((ka, nb), f32)], compiler_params=_cp(2))(a, b)


def dw_mix(ysin, dz, ob, dya, hs, dyc, merged, dx1):
    n = ysin.shape[0]
    tm = 512
    pairs = ((BW, 2 * D), (BW, D), (BW, D), (D, D))

    def body(a0, b0, a1, b1, a2, b2, a3, b3, o0, o1, o2, o3, c0, c1, c2, c3):
        i = pl.program_id(0)
        accs = (c0, c1, c2, c3)

        @pl.when(i == 0)
        def _():
            for c in accs:
                c[...] = jnp.zeros_like(c)

        for a, b_, c in zip((a0, a1, a2, a3), (b0, b1, b2, b3), accs):
            c[...] += _dot_tn(a[...], b_[...].astype(bf16))

        @pl.when(i == n // tm - 1)
        def _():
            for s in range(NDEV):
                o0[s] = c0[:, s * 256:(s + 1) * 256].astype(bf16)
                o1[s] = c1[:, s * 128:(s + 1) * 128].astype(bf16)
                o2[s] = c2[:, s * 128:(s + 1) * 128].astype(bf16)
                o3[s] = c3[s * 128:(s + 1) * 128, :].astype(bf16)

    tok = lambda w: pl.BlockSpec((tm, w), lambda i: (i, 0))
    whole = lambda shape: pl.BlockSpec(shape, lambda i: (0, 0, 0))
    outs = [(NDEV, BW, 256), (NDEV, BW, 128), (NDEV, BW, 128), (NDEV, D // NDEV, D)]
    return pl.pallas_call(
        body, name="dw_mix", grid=(n // tm,),
        in_specs=[tok(w) for pair in pairs for w in pair],
        out_specs=[whole(s) for s in outs], out_shape=[jax.ShapeDtypeStruct(s, bf16) for s in outs],
        scratch_shapes=[pltpu.VMEM(p, f32) for p in pairs],
        compiler_params=_cp(1, 56))(ysin, dz, ob, dya, hs, dyc, merged, dx1)


def ffn_fwd(x1, g2, w1, w2, layer):
    n = x1.shape[0]
    w2p = w2.reshape(w2.shape[0], 4, NSH_FF, D)

    def body(x_ref, g_ref, wa_ref, wb_ref, w2_ref, o_ref, z1_ref, z2_ref, h_sc):
        s = pl.program_id(1)

        @pl.when(s == 0)
        def _():
            xv = x_ref[...]
            h_sc[...] = (xv * _rms_stats(xv) * g_ref[...]).astype(bf16)
            o_ref[...] = xv

        h = h_sc[...]
        z1 = _dot(h, wa_ref[0, 0])
        z2 = _dot(h, wb_ref[0, 0])
        z1_ref[0] = z1.astype(bf16)
        z2_ref[0] = z2.astype(bf16)
        a = (z1 * _sigmoid(z1) * z2).astype(bf16)
        o_ref[...] += _dot(a, w2_ref[0, 0])

    tb = 2 * TB
    sh3 = pl.BlockSpec((1, tb, NSH_FF), lambda i, s: (s, i, 0))
    return pl.pallas_call(
        body, name="ffn_fwd", grid=(n // tb, 4),
        in_specs=[pl.BlockSpec((tb, D), lambda i, s: (i, 0)),
                  pl.BlockSpec((1, D), lambda i, s: (0, 0)),
                  pl.BlockSpec((1, 1, D, NSH_FF), lambda i, s: (layer, s, 0, 0)),
                  pl.BlockSpec((1, 1, D, NSH_FF), lambda i, s: (layer, s + 4, 0, 0)),
                  pl.BlockSpec((1, 1, NSH_FF, D), lambda i, s: (layer, s, 0, 0))],
        out_specs=[pl.BlockSpec((tb, D), lambda i, s: (i, 0)), sh3, sh3, pl.BlockSpec((tb, D), lambda i, s: (i, 0))],
        out_shape=[jax.ShapeDtypeStruct((n, D), f32), jax.ShapeDtypeStruct((4, n, NSH_FF), bf16),
                   jax.ShapeDtypeStruct((4, n, NSH_FF), bf16), jax.ShapeDtypeStruct((n, D), bf16)],
        compiler_params=_cp(2))(x1, g2, w1, w1, w2p)


def ffn_bwd(h, dx2, z1s, z2s, w1, w2, layer):
    n = h.shape[0]
    w2p = w2.reshape(w2.shape[0], 4, NSH_FF, D)
    nblk = n // TB

    def body(h_ref, dy_ref, z1_ref, z2_ref, wa_ref, wb_ref, w2_ref, dh_ref, dwa_ref, dwb_ref, dw2_ref,
             acc_a, acc_b, acc_2):
        i = pl.program_id(1)

        @pl.when(i == 0)
        def _():
            acc_a[...] = jnp.zeros_like(acc_a)
            acc_b[...] = jnp.zeros_like(acc_b)
            acc_2[...] = jnp.zeros_like(acc_2)

        hv = h_ref[...]
        dyb = dy_ref[...].astype(bf16)
        z1 = z1_ref[0].astype(f32)
        z2 = z2_ref[0].astype(f32)
        sg = _sigmoid(z1)
        sl = z1 * sg
        a = (sl * z2).astype(bf16)
        da = _dot_nt(dyb, w2_ref[0, 0])
        dz2 = (da * sl).astype(bf16)
        dz1 = (da * z2 * sg * (1.0 + z1 * (1.0 - sg))).astype(bf16)
        dh_ref[0] = (_dot_nt(dz1, wa_ref[0, 0]) + _dot_nt(dz2, wb_ref[0, 0])).astype(bf16)
        acc_a[...] += _dot_tn(hv, dz1)
        acc_b[...] += _dot_tn(hv, dz2)
        acc_2[...] += _dot_tn(a, dyb)

        @pl.when(i == nblk - 1)
        def _():
            dwa_ref[0] = acc_a[...].astype(bf16)
            dwb_ref[0] = acc_b[...].astype(bf16)
            dw2_ref[0] = acc_2[...].astype(bf16)

    tok = pl.BlockSpec((TB, D), lambda s, i: (i, 0))
    sh3 = pl.BlockSpec((1, TB, NSH_FF), lambda s, i: (s, i, 0))
    return pl.pallas_call(
        body, name="ffn_bwd", grid=(4, nblk),
        in_specs=[tok, tok, sh3, sh3,
                  pl.BlockSpec((1, 1, D, NSH_FF), lambda s, i: (layer, s, 0, 0)),
                  pl.BlockSpec((1, 1, D, NSH_FF), lambda s, i: (layer, s + 4, 0, 0)),
                  pl.BlockSpec((1, 1, NSH_FF, D), lambda s, i: (layer, s, 0, 0))],
        out_specs=[pl.BlockSpec((1, TB, D), lambda s, i: (s, i, 0)),
                   pl.BlockSpec((1, D, NSH_FF), lambda s, i: (s, 0, 0)),
                   pl.BlockSpec((1, D, NSH_FF), lambda s, i: (s, 0, 0)),
                   pl.BlockSpec((1, NSH_FF, D), lambda s, i: (s, 0, 0))],
        out_shape=[jax.ShapeDtypeStruct((4, n, D), bf16), jax.ShapeDtypeStruct((4, D, NSH_FF), bf16),
                   jax.ShapeDtypeStruct((4, D, NSH_FF), bf16), jax.ShapeDtypeStruct((4, NSH_FF, D), bf16)],
        scratch_shapes=[pltpu.VMEM((D, NSH_FF), f32), pltpu.VMEM((D, NSH_FF), f32), pltpu.VMEM((NSH_FF, D), f32)],
        compiler_params=_cp(2, 56))(h, dx2, z1s, z2s, w1, w1, w2p)


def ffn_bwd_fin(x1, g2, dx2, dh4):
    n = x1.shape[0]

    def body(x_ref, g_ref, dy_ref, dh_ref, dx_ref, dg_ref):
        i = pl.program_id(0)

        @pl.when(i == 0)
        def _():
            dg_ref[...] = jnp.zeros_like(dg_ref)

        dh = dh_ref[0].astype(f32)
        for s in range(1, 4):
            dh = dh + dh_ref[s].astype(f32)
        dx, dg = _rms_bwd(x_ref[...], g_ref[...], dh)
        dx_ref[...] = dy_ref[...] + dx
        dg_ref[...] += dg

    tok = pl.BlockSpec((TB, D), lambda i: (i, 0))
    return pl.pallas_call(
        body, name="ffn_bwd_fin", grid=(n // TB,),
        in_specs=[tok, pl.BlockSpec((1, D), lambda i: (0, 0)), tok, pl.BlockSpec((4, TB, D), lambda i: (0, i, 0))],
        out_specs=[tok, pl.BlockSpec((1, D), lambda i: (0, 0))],
        out_shape=[jax.ShapeDtypeStruct((n, D), f32), jax.ShapeDtypeStruct((1, D), f32)],
        compiler_params=_cp(1))(x1, g2, dx2, dh4)


def loss_head(x, g, target):
    n = x.shape[0]

    def body(x_ref, g_ref, t_ref, l_ref, dx_ref, dg_ref):
        i = pl.program_id(0)

        @pl.when(i == 0)
        def _():
            l_ref[...] = jnp.zeros_like(l_ref)
            dg_ref[...] = jnp.zeros_like(dg_ref)

        xv = x_ref[...]
        y = xv * _rms_stats(xv) * g_ref[...]
        e = y - t_ref[...]
        l_ref[...] += 0.5 * jnp.sum(jnp.sum(e * e, axis=-1, keepdims=True), axis=0, keepdims=True) * (1.0 / D)
        dx, dg = _rms_bwd(xv, g_ref[...], e * (1.0 / D))
        dx_ref[...] = dx
        dg_ref[...] += dg

    tok = lambda i: (i, 0)
    return pl.pallas_call(
        body, name="loss_head", grid=(n // TB,),
        in_specs=[pl.BlockSpec((TB, D), tok), pl.BlockSpec((1, D), lambda i: (0, 0)), pl.BlockSpec((TB, D), tok)],
        out_specs=[pl.BlockSpec((8, 128), lambda i: (0, 0)), pl.BlockSpec((TB, D), tok),
                   pl.BlockSpec((1, D), lambda i: (0, 0))],
        out_shape=[jax.ShapeDtypeStruct((8, 128), f32), jax.ShapeDtypeStruct((n, D), f32),
                   jax.ShapeDtypeStruct((1, D), f32)],
        compiler_params=_cp(1))(x, g, target)


def _disc(lr, li, ld):
    dt = jnp.exp(ld)
    mag = jnp.exp(lr * dt)
    ar = mag * jnp.cos(li * dt)
    ai = mag * jnp.sin(li * dt)
    nr, ni = ar - 1.0, ai
    den = lr * lr + li * li
    zr = (nr * lr + ni * li) / den
    zi = (ni * lr - nr * li) / den
    return ar, ai, zr, zi


def _blockdiag_mask(shape):
    r = lax.broadcasted_iota(jnp.int32, shape, 0) // GH
    c = lax.broadcasted_iota(jnp.int32, shape, 1) // NS
    return r == c


def s5_params(lr, li, ld, btr, bti, ctr, cti):
    def body(lr_ref, li_ref, ld_ref, btr_ref, bti_ref, ctr_ref, cti_ref, t8_ref, bb_ref, cb_ref):
        ar, ai, zr, zi = _disc(lr_ref[...], li_ref[...], ld_ref[...])
        pr_, pi_ = ar, ai
        pw2 = []
        for k in range(4):
            pw2.append((pr_, pi_))
            pr_, pi_ = pr_ * pr_ - pi_ * pi_, 2.0 * pr_ * pi_
        cm = lambda p, q: (p[0] * q[0] - p[1] * q[1], p[0] * q[1] + p[1] * q[0])
        pw = {1: pw2[0], 2: pw2[1], 4: pw2[2], 8: pw2[3]}
        pw[3], pw[5], pw[6] = cm(pw[2], pw[1]), cm(pw[4], pw[1]), cm(pw[4], pw[2])
        pw[7] = cm(pw[4], pw[3])
        row = lax.broadcasted_iota(jnp.int32, (8, NSTATE), 0)
        zero = jnp.zeros((8, NSTATE), f32)
        for c in range(2):
            for k in range(3):
                full = jnp.broadcast_to(pw2[k][c], (8, NSTATE))
                t8_ref[c, k] = jnp.where(row >= (1 << k), full, 0.0)
                t8_ref[c, 3 + k] = jnp.where(row + (1 << k) < 8, full, 0.0)
            up, down = zero, zero
            for j in range(8):
                up = up + jnp.where(row == j, pw[j + 1][c], 0.0)
                down = down + jnp.where(row == j, pw[8 - j][c], 0.0)
            t8_ref[c, 6] = up
            t8_ref[c, 7] = down
        bbr = zr * btr_ref[...] - zi * bti_ref[...]
        bbi = zr * bti_ref[...] + zi * btr_ref[...]
        mask = _blockdiag_mask((SU, SW))
        for j in range(SJ):
            cols = slice(j * SW, (j + 1) * SW)
            for c, (vb, vc) in enumerate(((bbr, ctr_ref[...]), (bbi, cti_ref[...]))):
                bb_ref[c, j] = jnp.where(mask, jnp.tile(vb[:, cols], (SU // GH, 1)), 0.0).astype(bf16)
                cb_ref[c, j] = jnp.where(mask, jnp.tile(vc[:, cols], (SU // GH, 1)), 0.0).astype(bf16)

    return pl.pallas_call(
        body, name="s5_params",
        out_shape=[jax.ShapeDtypeStruct((2, 8, 8, NSTATE), f32),
                   jax.ShapeDtypeStruct((2, SJ, SU, SW), bf16), jax.ShapeDtypeStruct((2, SJ, SU, SW), bf16)],
        compiler_params=pltpu.CompilerParams(vmem_limit_bytes=56 * 1024 * 1024))(lr, li, ld, btr, bti, ctr, cti)


def s5_params_bwd(lr, li, ld, btr, bti, d_a, d_bb, d_cb):
    def body(lr_ref, li_ref, ld_ref, btr_ref, bti_ref, da_ref, dbb_ref, dcb_ref,
             dlr_ref, dli_ref, dld_ref, dbt_ref, dct_ref):
        mask = _blockdiag_mask((SU, SW))

        def fold(ref, c):
            parts = []
            for j in range(SJ):
                v = jnp.where(mask, ref[c, j], 0.0)
                parts.append(v.reshape(SU // GH, GH, SW).sum(axis=0))
            return jnp.concatenate(parts, axis=1)

        dct_ref[0] = fold(dcb_ref, 0)
        dct_ref[1] = fold(dcb_ref, 1)
        dbbr, dbbi = fold(dbb_ref, 0), fold(dbb_ref, 1)
        lrv, liv, ldv = lr_ref[...], li_ref[...], ld_ref[...]
        (ar, ai, zr, zi), vjp = jax.vjp(_disc, lrv, liv, ldv)
        btr, bti = btr_ref[...], bti_ref[...]
        dbt_ref[0] = zr * dbbr + zi * dbbi
        dbt_ref[1] = zr * dbbi - zi * dbbr
        dzr = jnp.sum(dbbr * btr + dbbi * bti, axis=0, keepdims=True)
        dzi = jnp.sum(dbbi * btr - dbbr * bti, axis=0, keepdims=True)
        dlr, dli, dld = vjp((da_ref[0:1, :], da_ref[1:2, :], dzr, dzi))
        dlr_ref[...] = dlr
        dli_ref[...] = dli
        ind = (lax.broadcasted_iota(jnp.int32, (NSTATE, 128), 0) // NS
               == lax.broadcasted_iota(jnp.int32, (NSTATE, 128), 1)).astype(f32)
        dld_ref[...] = _dot_hi(jnp.broadcast_to(dld, (8, NSTATE)), ind)

    return pl.pallas_call(
        body, name="s5_params_bwd",
        out_shape=[jax.ShapeDtypeStruct((1, NSTATE), f32), jax.ShapeDtypeStruct((1, NSTATE), f32),
                   jax.ShapeDtypeStruct((8, 128), f32), jax.ShapeDtypeStruct((2, GH, NSTATE), f32),
                   jax.ShapeDtypeStruct((2, GH, NSTATE), f32)],
        compiler_params=pltpu.CompilerParams(vmem_limit_bytes=56 * 1024 * 1024))(lr, li, ld, btr, bti, d_a, d_bb, d_cb)


def _fma(sr, si, ar, ai, qr, qi):
    return sr + ar * qr - ai * qi, si + ar * qi + ai * qr


def _scan_tile(sr, si, cr, ci, t8_ref, reverse):
    sg = -1.0 if reverse else 1.0
    for k in range(3):
        tk = 3 + k if reverse else k
        rot = 8 - (1 << k) if reverse else 1 << k
        sr, si = _fma(sr, si, t8_ref[0, tk], sg * t8_ref[1, tk], pltpu.roll(sr, rot, 0), pltpu.roll(si, rot, 0))
    tp = 7 if reverse else 6
    sr, si = _fma(sr, si, t8_ref[0, tp], sg * t8_ref[1, tp], cr, ci)
    e = 0 if reverse else 7
    return sr, si, jnp.broadcast_to(sr[e:e + 1, :], sr.shape), jnp.broadcast_to(si[e:e + 1, :], si.shape)


S5MC = 512


def _s5_input_map(u_ref, bb_ref, sr_sc, si_sc, l):
    for c in range(l // S5MC):
        rows = slice(c * S5MC, (c + 1) * S5MC)
        u = u_ref[0, rows, :]
        sr_sc[rows, :] = _dot(u, bb_ref[0, 0])
        si_sc[rows, :] = _dot(u, bb_ref[1, 0])


def _s5_forward_scan(sr_sc, si_sc, t8_ref, l):
    def step(k, carry):
        rows = pl.ds(pl.multiple_of(k * 8, 8), 8)
        sr, si, cr, ci = _scan_tile(sr_sc[rows, :], si_sc[rows, :], carry[0], carry[1], t8_ref, False)
        sr_sc[rows, :] = sr
        si_sc[rows, :] = si
        return cr, ci

    zero = jnp.zeros((8, SW), f32)
    lax.fori_loop(0, l // 8, step, (zero, zero), unroll=4)


def s5_fwd(proj3, t8, bb, cb, dskip):
    b, l, _ = proj3.shape

    def body(u_ref, t8_ref, bb_ref, cb_ref, d_ref, y_ref, sr_out, si_out):
        sr_sc, si_sc = sr_out.at[0], si_out.at[0]
        _s5_input_map(u_ref, bb_ref, sr_sc, si_sc, l)
        _s5_forward_scan(sr_sc, si_sc, t8_ref, l)
        for c in range(l // S5MC):
            rows = slice(c * S5MC, (c + 1) * S5MC)
            y = (_dot_nt(sr_sc[rows, :].astype(bf16), cb_ref[0, 0])
                 - _dot_nt(si_sc[rows, :].astype(bf16), cb_ref[1, 0]))
            y_ref[0, rows, :] = y + d_ref[...] * u_ref[0, rows, :].astype(f32)

    return pl.pallas_call(
        body, name="s5_fwd", grid=(SJ, b),
        in_specs=[pl.BlockSpec((1, l, SU), lambda j, bi: (bi, 0, j)),
                  pl.BlockSpec((2, 8, 8, SW), lambda j, bi: (0, 0, 0, j)),
                  pl.BlockSpec((2, 1, SU, SW), lambda j, bi: (0, j, 0, 0)),
                  pl.BlockSpec((2, 1, SU, SW), lambda j, bi: (0, j, 0, 0)),
                  pl.BlockSpec((1, SU), lambda j, bi: (0, j))],
        out_specs=[pl.BlockSpec((1, l, SU), lambda j, bi: (bi, 0, j)),
                   pl.BlockSpec((1, l, SW), lambda j, bi: (bi, 0, j)),
                   pl.BlockSpec((1, l, SW), lambda j, bi: (bi, 0, j))],
        out_shape=[jax.ShapeDtypeStruct((b, l, BW), f32), jax.ShapeDtypeStruct((b, l, NSTATE), f32),
                   jax.ShapeDtypeStruct((b, l, NSTATE), f32)],
        compiler_params=_cp(2))(proj3, t8, bb, cb, dskip)


def s5_bwd(proj3, dy, s_re, s_im, t8, bb, cb, dskip):
    b, l, _ = proj3.shape
    nt = l // 8

    def body(u_ref, dy_ref, sr_in, si_in, t8_ref, bb_ref, cb_ref, d_ref,
             du_ref, da_ref, dbb_ref, dcb_ref, dd_ref, gr_sc, gi_sc):
        bi = pl.program_id(1)
        sr_sc, si_sc = sr_in.at[0], si_in.at[0]

        @pl.when(bi == 0)
        def _():
            da_ref[...] = jnp.zeros_like(da_ref)
            dbb_ref[...] = jnp.zeros_like(dbb_ref)
            dcb_ref[...] = jnp.zeros_like(dcb_ref)
            dd_ref[...] = jnp.zeros_like(dd_ref)

        for c in range(l // S5MC):
            rows = slice(c * S5MC, (c + 1) * S5MC)
            dyb = dy_ref[0, rows, :].astype(bf16)
            gr_sc[rows, :] = _dot(dyb, cb_ref[0, 0])
            gi_sc[rows, :] = -_dot(dyb, cb_ref[1, 0])

        row = lax.broadcasted_iota(jnp.int32, (8, SW), 0)

        def step(i, carry):
            cr, ci, dar, dai = carry
            k = nt - 1 - i
            rows = pl.ds(pl.multiple_of(k * 8, 8), 8)
            gr, gi, cr, ci = _scan_tile(gr_sc[rows, :], gi_sc[rows, :], cr, ci, t8_ref, True)
            gr_sc[rows, :] = gr
            gi_sc[rows, :] = gi
            before = pl.ds(pl.multiple_of(jnp.maximum(k - 1, 0) * 8, 8), 8)
            live = jnp.where(k > 0, 1.0, 0.0)
            sr, si = sr_sc[rows, :], si_sc[rows, :]
            spr = jnp.where(row == 0, live * sr_sc[before, :][7:8, :], pltpu.roll(sr, 1, 0))
            spi = jnp.where(row == 0, live * si_sc[before, :][7:8, :], pltpu.roll(si, 1, 0))
            return cr, ci, dar + spr * gr + spi * gi, dai + spr * gi - spi * gr

        zero = jnp.zeros((8, SW), f32)
        _, _, dar, dai = lax.fori_loop(0, nt, step, (zero, zero, zero, zero), unroll=2)
        da_ref[0:1, :] += jnp.sum(dar, axis=0, keepdims=True)
        da_ref[1:2, :] += jnp.sum(dai, axis=0, keepdims=True)

        for c in range(l // S5MC):
            rows = slice(c * S5MC, (c + 1) * S5MC)
            u = u_ref[0, rows, :]
            dyv = dy_ref[0, rows, :]
            dyb = dyv.astype(bf16)
            grb, gib = gr_sc[rows, :].astype(bf16), gi_sc[rows, :].astype(bf16)
            dcb_ref[0, 0] += _dot_tn(dyb, sr_sc[rows, :].astype(bf16))
            dcb_ref[1, 0] -= _dot_tn(dyb, si_sc[rows, :].astype(bf16))
            dbb_ref[0, 0] += _dot_tn(u, grb)
            dbb_ref[1, 0] += _dot_tn(u, gib)
            du = _dot_nt(grb, bb_ref[0, 0]) + _dot_nt(gib, bb_ref[1, 0]) + d_ref[...] * dyv
            du_ref[0, rows, :] = du.astype(bf16)
            dd_ref[...] += jnp.sum(dyv * u.astype(f32), axis=0, keepdims=True)

    seq = pl.BlockSpec((1, l, SU), lambda j, bi: (bi, 0, j))
    sts = pl.BlockSpec((1, l, SW), lambda j, bi: (bi, 0, j))
    tab = pl.BlockSpec((2, 1, SU, SW), lambda j, bi: (0, j, 0, 0))
    return pl.pallas_call(
        body, name="s5_bwd", grid=(SJ, b),
        in_specs=[seq, seq, sts, sts, pl.BlockSpec((2, 8, 8, SW), lambda j, bi: (0, 0, 0, j)), tab, tab,
                  pl.BlockSpec((1, SU), lambda j, bi: (0, j))],
        out_specs=[seq, pl.BlockSpec((2, SW), lambda j, bi: (0, j)), tab, tab,
                   pl.BlockSpec((1, SU), lambda j, bi: (0, j))],
        out_shape=[jax.ShapeDtypeStruct((b, l, BW), bf16), jax.ShapeDtypeStruct((2, NSTATE), f32),
                   jax.ShapeDtypeStruct((2, SJ, SU, SW), f32), jax.ShapeDtypeStruct((2, SJ, SU, SW), f32),
                   jax.ShapeDtypeStruct((1, BW), f32)],
        scratch_shapes=[pltpu.VMEM((l, SW), f32)] * 2,
        compiler_params=_cp(2))(proj3, dy, s_re, s_im, t8, bb, cb, dskip)


AHC = 2
AHW = AHC * 128


def _att_mask(n, nb):
    if nb == 1:
        qi = lax.broadcasted_iota(jnp.int32, (ABLK, ABLK), 0)
        kj = lax.broadcasted_iota(jnp.int32, (ABLK, ABLK), 1)
        return kj <= qi
    qi = lax.broadcasted_iota(jnp.int32, (ABLK, 2 * ABLK), 0)
    kj = lax.broadcasted_iota(jnp.int32, (ABLK, 2 * ABLK), 1)
    return (kj >= qi) & (kj <= qi + ABLK) & ((n > 0) | (kj >= ABLK))


def _att_rows(it, nb, dil):
    r, n = it // nb, it % nb
    cur = pl.ds(r + n * (ABLK * dil), ABLK, stride=dil)
    prv = pl.ds(r + jnp.maximum(n - 1, 0) * (ABLK * dil), ABLK, stride=dil)
    return n, cur, prv


def _att_keys(ref, c, cur, prv, nb):
    if nb == 1:
        x = ref[c, cur, :].astype(bf16)
    else:
        x = jnp.concatenate([ref[c, prv, :], ref[c, cur, :]], axis=0).astype(bf16)
    head0 = lax.broadcasted_iota(jnp.int32, x.shape, 1) < HD
    zero = jnp.zeros_like(x)
    return jnp.concatenate([jnp.where(head0, x, zero), jnp.where(head0, zero, x)], axis=0)


def _per_head(nk, a0, a1):
    col = lax.broadcasted_iota(jnp.int32, (ABLK, 2 * nk), 1)
    return jnp.where(col < nk, a0, a1)


def _to_chunks(src_ref, dst):
    for c in range(AHC):
        dst[c] = src_ref[0, :, c * 128:(c + 1) * 128].astype(f32)


def att_fwd(proj3, g_idx, dil):
    b, l, _ = proj3.shape
    nb = l // dil // ABLK
    nhalf = BW // AHW

    def body(q_ref, k_ref, v_ref, o_ref, lse_ref, qf, kf, vf, of):
        hh = pl.program_id(1)
        _to_chunks(q_ref, qf)
        _to_chunks(k_ref, kf)
        _to_chunks(v_ref, vf)
        lane = lax.broadcasted_iota(jnp.int32, (ABLK, 128), 1)

        def step(it, carry):
            n, cur, prv = _att_rows(it, nb, dil)
            valid = _att_mask(n, nb)
            valid = jnp.concatenate([valid, valid], axis=1)
            nk = valid.shape[1] // 2
            lse_all = jnp.zeros((ABLK, 128), f32)
            for c in range(AHC):
                q = (qf[c, cur, :] * ATT_SCALE).astype(bf16)
                k = _att_keys(kf, c, cur, prv, nb)
                v = _att_keys(vf, c, cur, prv, nb)
                s = jnp.where(valid, _dot_nt(q, k), NEG)
                m0 = jnp.max(s[:, :nk], axis=-1, keepdims=True)
                m1 = jnp.max(s[:, nk:], axis=-1, keepdims=True)
                p = jnp.exp(s - _per_head(nk, m0, m1))
                den0 = jnp.sum(p[:, :nk], axis=-1, keepdims=True)
                den1 = jnp.sum(p[:, nk:], axis=-1, keepdims=True)
                of[c, cur, :] = _dot(p.astype(bf16), v) * jnp.where(lane < HD, 1.0 / den0, 1.0 / den1)
                head = hh * (2 * AHC) + 2 * c
                lse_all = (lse_all + jnp.where(lane == head, m0 + jnp.log(den0), 0.0)
                           + jnp.where(lane == head + 1, m1 + jnp.log(den1), 0.0))

            lse_ref[0, 0, cur, :] = lse_all
            return carry

        lax.fori_loop(0, dil * nb, step, 0, unroll=4)
        for c in range(AHC):
            o_ref[0, :, c * 128:(c + 1) * 128] = of[c].astype(bf16)

    col = lambda c: pl.BlockSpec((1, l, AHW), lambda bi, hh: (bi, 0, c * nhalf + hh))
    return pl.pallas_call(
        body, name=f"att_fwd{g_idx}", grid=(b, nhalf),
        in_specs=[col(1 + g_idx), col(4), col(5)],
        out_specs=[pl.BlockSpec((1, l, AHW), lambda bi, hh: (bi, 0, hh)),
                   pl.BlockSpec((1, 1, l, 128), lambda bi, hh: (bi, hh, 0, 0))],
        out_shape=[jax.ShapeDtypeStruct((b, l, BW), bf16), jax.ShapeDtypeStruct((b, nhalf, l, 128), f32)],
        scratch_shapes=[pltpu.VMEM((AHC, l, 128), f32)] * 4,
        compiler_params=_cp(2))(proj3, proj3, proj3)


def att_bwd(proj3, do, lse_tot, delta, g_idx, dil):
    b, l, _ = proj3.shape
    nb = l // dil // ABLK
    nhalf = BW // AHW

    def body(q_ref, k_ref, v_ref, do_ref, l_ref, dl_ref, dq_out, dk_out, dv_out, qf, kf, vf, dof,
             dq_ref, dk_ref, dv_ref):
        hh = pl.program_id(1)
        _to_chunks(q_ref, qf)
        _to_chunks(k_ref, kf)
        _to_chunks(v_ref, vf)
        _to_chunks(do_ref, dof)
        dk_ref[...] = jnp.zeros_like(dk_ref)
        dv_ref[...] = jnp.zeros_like(dv_ref)
        lane = lax.broadcasted_iota(jnp.int32, (ABLK, 128), 1)

        def step(it, carry):
            n, cur, prv = _att_rows(it, nb, dil)
            valid = _att_mask(n, nb)
            valid = jnp.concatenate([valid, valid], axis=1)
            nk = valid.shape[1] // 2
            lse_b = l_ref[0, cur, :]
            dl_b = dl_ref[0, cur, :]
            head0 = lax.broadcasted_iota(jnp.int32, (nk, 128), 1) < HD
            for c in range(AHC):
                q = (qf[c, cur, :] * ATT_SCALE).astype(bf16)
                dob = dof[c, cur, :].astype(bf16)
                k = _att_keys(kf, c, cur, prv, nb)
                v = _att_keys(vf, c, cur, prv, nb)
                head = hh * (2 * AHC) + 2 * c
                pick = lambda a, h: jnp.sum(jnp.where(lane == h, a, 0.0), axis=-1, keepdims=True)
                lse_h = _per_head(nk, pick(lse_b, head), pick(lse_b, head + 1))
                dl_h = _per_head(nk, pick(dl_b, head), pick(dl_b, head + 1))
                s = _dot_nt(q, k)
                p = jnp.where(valid, jnp.exp(jnp.minimum(s - lse_h, 60.0)), 0.0)
                ds = (p * (_dot_nt(dob, v) - dl_h)).astype(bf16)
                dq_ref[0, c, cur, :] = _dot(ds, k) * ATT_SCALE
                dk2 = _dot_tn(ds, q)
                dv2 = _dot_tn(p.astype(bf16), dob)
                dk = jnp.where(head0, dk2[:nk], dk2[nk:])
                dv = jnp.where(head0, dv2[:nk], dv2[nk:])
                if nb == 1:
                    dk_ref[0, c, cur, :] += dk
                    dv_ref[0, c, cur, :] += dv
                else:
                    dk_ref[0, c, cur, :] += dk[ABLK:]
                    dv_ref[0, c, cur, :] += dv[ABLK:]
                    dk_ref[0, c, prv, :] += dk[:ABLK]
                    dv_ref[0, c, prv, :] += dv[:ABLK]

            return carry

        lax.fori_loop(0, dil * nb, step, 0, unroll=4)
        dq_out[0] = dq_ref[0].astype(bf16)
        dk_out[0] = dk_ref[0].astype(bf16)
        dv_out[0] = dv_ref[0].astype(bf16)

    col = lambda c: pl.BlockSpec((1, l, AHW), lambda bi, hh: (bi, 0, c * nhalf + hh))
    own = pl.BlockSpec((1, l, AHW), lambda bi, hh: (bi, 0, hh))
    own128 = pl.BlockSpec((1, l, 128), lambda bi, hh: (bi, 0, 0))
    chunked = pl.BlockSpec((1, AHC, l, 128), lambda bi, hh: (bi, hh, 0, 0))
    return pl.pallas_call(
        body, name=f"att_bwd{g_idx}", grid=(b, nhalf),
        in_specs=[col(1 + g_idx), col(4), col(5), own, own128, own128],
        out_specs=[chunked] * 3,
        out_shape=[jax.ShapeDtypeStruct((b, BW // 128, l, 128), bf16)] * 3,
        scratch_shapes=[pltpu.VMEM((AHC, l, 128), f32)] * 4 + [pltpu.VMEM((1, AHC, l, 128), f32)] * 3,
        compiler_params=_cp(2, 56))(proj3, proj3, proj3, do, lse_tot, delta)


CPAD = 32
CTAIL = 16
CR = 128
CSLAB = CR + 40


def _tap_windows(slab, off, mis):
    ntap = (CW - 1 - mis) // 8 + 1
    rot = (off + mis) % 8
    base = off + mis - rot
    shifted = pltpu.roll(slab, CSLAB - rot, 0) if rot else slab
    for a in range(ntap):
        yield 8 * a + mis, shifted[base + 8 * a:base + 8 * a + CR]


def _fill_glu(cv_ref, pad, l):
    pad[0:CPAD, :] = jnp.zeros((CPAD, BW), f32)
    pad[CPAD:CPAD + l, :] = cv_ref[0, :, :BW].astype(f32) * _sigmoid(cv_ref[0, :, BW:].astype(f32))
    pad[CPAD + l:, :] = jnp.zeros((CTAIL, BW), f32)


def conv_fwd(proj3, cw, cb):
    b, l, _ = proj3.shape

    def body(cv_ref, w_ref, b_ref, o_ref, pad):
        _fill_glu(cv_ref, pad, l)
        for lc in range(BW // 128):
            lanes = slice(lc * 128, (lc + 1) * 128)
            wv = w_ref[:, lanes]

            def step(c, carry):
                base = pl.multiple_of(c * CR, CR)
                slab = pad[pl.ds(base, CSLAB), lanes]
                acc = jnp.zeros((CR, 128), f32) + b_ref[:, lanes]
                for mis in range(8):
                    for k, win in _tap_windows(slab, CPAD - (CW - 1), mis):
                        acc = acc + wv[k:k + 1] * win
                o_ref[0, pl.ds(base, CR), lanes] = acc
                return carry

            lax.fori_loop(0, l // CR, step, 0)

    return pl.pallas_call(
        body, name="conv_fwd", grid=(b,),
        in_specs=[pl.BlockSpec((1, l, 2 * BW), lambda i: (i, 0, 3)),
                  pl.BlockSpec((32, BW), lambda i: (0, 0)), pl.BlockSpec((1, BW), lambda i: (0, 0))],
        out_specs=pl.BlockSpec((1, l, BW), lambda i: (i, 0, 0)),
        out_shape=jax.ShapeDtypeStruct((b, l, BW), f32),
        scratch_shapes=[pltpu.VMEM((CPAD + l + CTAIL, BW), f32)], compiler_params=_cp(1))(proj3, cw, cb)


def conv_bwd(proj3, dhc, cw):
    b, l, _ = proj3.shape

    def body(cv_ref, d_ref, w_ref, dcv_ref, dw_ref, db_ref, pad, dpad):
        i = pl.program_id(0)

        @pl.when(i == 0)
        def _():
            dw_ref[...] = jnp.zeros_like(dw_ref)
            db_ref[...] = jnp.zeros_like(db_ref)

        _fill_glu(cv_ref, pad, l)
        dpad[0:l, :] = d_ref[0]
        dpad[l:, :] = jnp.zeros((CPAD + CTAIL, BW), f32)
        db_ref[...] += jnp.sum(d_ref[0], axis=0, keepdims=True)
        for lc in range(BW // 128):
            lanes = slice(lc * 128, (lc + 1) * 128)
            glanes = slice(BW + lc * 128, BW + (lc + 1) * 128)
            wv = w_ref[:, lanes]

            for mis in range(8):
                ntap = (CW - 1 - mis) // 8 + 1

                def dw_step(c, accs, mis=mis, lanes=lanes):
                    base = pl.multiple_of(c * CR, CR)
                    slab = pad[pl.ds(base, CSLAB), lanes]
                    dv = dpad[pl.ds(base, CR), lanes]
                    return tuple(acc + (dv * win).reshape(CR // 8, 8, 128).sum(axis=0) for acc, (_, win)
                                 in zip(accs, _tap_windows(slab, CPAD - (CW - 1), mis)))

                accs = lax.fori_loop(0, l // CR, dw_step, tuple(jnp.zeros((8, 128), f32) for _ in range(ntap)))
                for a in range(ntap):
                    k = 8 * a + mis
                    dw_ref[k:k + 1, lanes] += jnp.sum(accs[a], axis=0, keepdims=True)

            def dh_step(c, carry, lanes=lanes, glanes=glanes, wv=wv):
                base = pl.multiple_of(c * CR, CR)
                slab = dpad[pl.ds(base, CSLAB), lanes]
                acc = jnp.zeros((CR, 128), f32)
                for mis in range(8):
                    for kk, win in _tap_windows(slab, 0, mis):
                        acc = acc + wv[CW - 1 - kk:CW - kk] * win
                rows = pl.ds(base, CR)
                a = cv_ref[0, rows, lanes].astype(f32)
                sg = _sigmoid(cv_ref[0, rows, glanes].astype(f32))
                dcv_ref[0, rows, lanes] = (acc * sg).astype(bf16)
                dcv_ref[0, rows, glanes] = (acc * a * sg * (1.0 - sg)).astype(bf16)
                return carry

            lax.fori_loop(0, l // CR, dh_step, 0)

    return pl.pallas_call(
        body, name="conv_bwd", grid=(b,),
        in_specs=[pl.BlockSpec((1, l, 2 * BW), lambda i: (i, 0, 3)),
                  pl.BlockSpec((1, l, BW), lambda i: (i, 0, 0)),
                  pl.BlockSpec((32, BW), lambda i: (0, 0))],
        out_specs=[pl.BlockSpec((1, l, 2 * BW), lambda i: (i, 0, 0)),
                   pl.BlockSpec((32, BW), lambda i: (0, 0)), pl.BlockSpec((1, BW), lambda i: (0, 0))],
        out_shape=[jax.ShapeDtypeStruct((b, l, 2 * BW), bf16), jax.ShapeDtypeStruct((32, BW), f32),
                   jax.ShapeDtypeStruct((1, BW), f32)],
        scratch_shapes=[pltpu.VMEM((CPAD + l + CTAIL, BW), f32), pltpu.VMEM((l + CPAD + CTAIL, BW), f32)],
        compiler_params=_cp(1))(proj3, dhc, cw)


def _head_expand():
    r = lax.broadcasted_iota(jnp.int32, (128, BW), 0)
    c = lax.broadcasted_iota(jnp.int32, (128, BW), 1) // HD
    return (r == c).astype(f32)


def _head_reduce():
    r = lax.broadcasted_iota(jnp.int32, (BW, 128), 0) // HD
    c = lax.broadcasted_iota(jnp.int32, (BW, 128), 1)
    return (r == c).astype(f32)


def _merge_common(ys_ref, o_refs, l_refs, hc_ref, g_refs, bg_ref, lng_ref, lnb_ref, wglu_ref, watt_ref, wpw_ref):
    r = {}
    ysv = ys_ref[...]
    r["ys"] = ysv
    r["ysin"] = _gelu(ysv).astype(bf16)
    z = _dot(r["ysin"], wglu_ref[...])
    r["z1"], r["sg2"] = z[:, :D], _sigmoid(z[:, D:])
    r["y_s"] = r["z1"] * r["sg2"]
    ls = [lr_[0, 0] + lr_[0, 1] for lr_ in l_refs]
    mx = jnp.maximum(jnp.maximum(ls[0], ls[1]), ls[2])
    es = [jnp.exp(v - mx) for v in ls]
    tot = es[0] + es[1] + es[2]
    r["lse_tot"] = mx + jnp.log(tot)
    e_mat = _head_expand()
    o = jnp.zeros(ysv.shape, f32)
    for e, o_ref in zip(es, o_refs):
        o = o + _dot_hi(e / tot, e_mat) * o_ref[...].astype(f32)
    r["o"] = o
    r["ob"] = o.astype(bf16)
    r["y_a"] = _dot(r["ob"], watt_ref[...])
    hc = hc_ref[...]
    mu = jnp.mean(hc, axis=-1, keepdims=True)
    xc = hc - mu
    rstd = lax.rsqrt(jnp.mean(xc * xc, axis=-1, keepdims=True) + EPS)
    r["xh"], r["rstd"] = xc * rstd, rstd
    hn = r["xh"] * lng_ref[...] + lnb_ref[...]
    r["hn"] = hn
    r["sgn"] = _sigmoid(hn)
    r["hs"] = (hn * r["sgn"]).astype(bf16)
    r["y_c"] = _dot(r["hs"], wpw_ref[...])
    r["gates"] = [_sigmoid(g_refs[k][...].astype(f32) + bg_ref[:, k * D:(k + 1) * D]) for k in range(3)]
    r["merged"] = r["gates"][0] * r["y_s"] + r["gates"][1] * r["y_a"] + r["gates"][2] * r["y_c"]
    return r


TBM = 256


def _merge_in_specs(tok, tb, lses):
    w = lambda shape: pl.BlockSpec(shape, lambda i: (0, 0), pipeline_mode=pl.Buffered(1))
    nbl = lses[0].shape[2] // tb
    return ([pl.BlockSpec((tb, D), tok), pl.BlockSpec((tb, BW), tok)]
            + [pl.BlockSpec((tb, BW), tok)] * 3
            + [pl.BlockSpec((1, 2, tb, 128), lambda i: (i // nbl, 0, i % nbl, 0))] * 3
            + [pl.BlockSpec((tb, BW), tok)]
            + [pl.BlockSpec((tb, D), lambda i, k=k: (i, 4 + k)) for k in range(3)]
            + [w((1, 3 * D)), w((1, BW)), w((1, BW)), w((BW, 2 * D)), w((BW, D)), w((BW, D)), w((D, D))])


def merge_fwd(x, ys, os_, lses, hc, proj, bg, lng, lnb, wglu, watt, wpw, wout):
    n = x.shape[0]

    def body(x_ref, ys_ref, o1, o2, o3, l1, l2, l3, hc_ref, g0, g1, g2, bg_ref, lng_ref, lnb_ref,
             wglu_ref, watt_ref, wpw_ref, wout_ref, x1_ref):
        r = _merge_common(ys_ref, (o1, o2, o3), (l1, l2, l3), hc_ref, (g0, g1, g2), bg_ref, lng_ref, lnb_ref,
                          wglu_ref, watt_ref, wpw_ref)
        x1_ref[...] = x_ref[...] + _dot(r["merged"].astype(bf16), wout_ref[...])

    tok = lambda i: (i, 0)
    return pl.pallas_call(
        body, name="merge_fwd", grid=(n // TB,), in_specs=_merge_in_specs(tok, TB, lses),
        out_specs=pl.BlockSpec((TB, D), tok), out_shape=jax.ShapeDtypeStruct((n, D), f32),
        compiler_params=_cp(1, 56))(x, ys, *os_, *lses, hc, proj, proj, proj, bg, lng, lnb, wglu, watt, wpw, wout)


def merge_bwd(dx1, ys, os_, lses, hc, proj, bg, lng, lnb, wglu, watt, wpw, wout):
    n = dx1.shape[0]

    def body(dx_ref, ys_ref, o1, o2, o3, l1, l2, l3, hc_ref, g0, g1, g2, bg_ref, lng_ref, lnb_ref,
             wglu_ref, watt_ref, wpw_ref, wout_ref,
             dys_ref, do_ref, delta_ref, ltot_ref, dhc_ref, dgate_ref, ysin_ref, dz_ref, ob_ref, dya_ref,
             hs_ref, dyc_ref, mg_ref, dbg_ref, dlng_ref, dlnb_ref):
        i = pl.program_id(0)

        @pl.when(i == 0)
        def _():
            dbg_ref[...] = jnp.zeros_like(dbg_ref)
            dlng_ref[...] = jnp.zeros_like(dlng_ref)
            dlnb_ref[...] = jnp.zeros_like(dlnb_ref)

        r = _merge_common(ys_ref, (o1, o2, o3), (l1, l2, l3), hc_ref, (g0, g1, g2), bg_ref, lng_ref, lnb_ref,
                          wglu_ref, watt_ref, wpw_ref)
        mg_ref[...] = r["merged"].astype(bf16)
        ysin_ref[...] = r["ysin"]
        ob_ref[...] = r["ob"]
        hs_ref[...] = r["hs"]
        ltot_ref[...] = r["lse_tot"]
        dm = _dot_nt(dx_ref[...].astype(bf16), wout_ref[...])
        ys3 = (r["y_s"], r["y_a"], r["y_c"])
        for k in range(3):
            gk = r["gates"][k]
            dgr = dm * ys3[k] * gk * (1.0 - gk)
            dgate_ref[:, k * D:(k + 1) * D] = dgr.astype(bf16)
            dbg_ref[:, k * D:(k + 1) * D] += jnp.sum(dgr, axis=0, keepdims=True)
        dy_s = dm * r["gates"][0]
        sg2 = r["sg2"]
        dz = jnp.concatenate([dy_s * sg2, dy_s * r["z1"] * sg2 * (1.0 - sg2)], axis=1).astype(bf16)
        dz_ref[...] = dz
        dys_ref[...] = _dot_nt(dz, wglu_ref[...]) * _gelu_grad(r["ys"])
        dya = (dm * r["gates"][1]).astype(bf16)
        dya_ref[...] = dya
        do = _dot_nt(dya, watt_ref[...])
        do_ref[...] = do.astype(bf16)
        delta_ref[...] = _dot_hi(do * r["o"], _head_reduce())
        dyc = (dm * r["gates"][2]).astype(bf16)
        dyc_ref[...] = dyc
        sgn, hn = r["sgn"], r["hn"]
        dhn = _dot_nt(dyc, wpw_ref[...]) * sgn * (1.0 + hn * (1.0 - sgn))
        dlng_ref[...] += jnp.sum(dhn * r["xh"], axis=0, keepdims=True)
        dlnb_ref[...] += jnp.sum(dhn, axis=0, keepdims=True)
        dxh = dhn * lng_ref[...]
        xh = r["xh"]
        dhc_ref[...] = r["rstd"] * (dxh - jnp.mean(dxh, axis=-1, keepdims=True)
                                    - xh * jnp.mean(dxh * xh, axis=-1, keepdims=True))

    tok = lambda i: (i, 0)
    fix = lambda i: (0, 0)
    outs = [("dys", BW, f32), ("do", BW, bf16), ("delta", 128, f32), ("lse_tot", 128, f32), ("dhc", BW, f32),
            ("dgate", 3 * D, bf16), ("ysin", BW, bf16), ("dz", 2 * D, bf16), ("ob", BW, bf16), ("dya", D, bf16),
            ("hs", BW, bf16), ("dyc", D, bf16), ("merged", D, bf16)]
    small = [("dbg", 3 * D), ("dlng", BW), ("dlnb", BW)]
    res = pl.pallas_call(
        body, name="merge_bwd", grid=(n // TBM,), in_specs=_merge_in_specs(tok, TBM, lses),
        out_specs=[pl.BlockSpec((TBM, w), tok) for _, w, _ in outs] + [pl.BlockSpec((1, w), fix) for _, w in small],
        out_shape=[jax.ShapeDtypeStruct((n, w), dt) for _, w, dt in outs]
        + [jax.ShapeDtypeStruct((1, w), f32) for _, w in small],
        compiler_params=_cp(1, 56))(dx1, ys, *os_, *lses, hc, proj, proj, proj, bg, lng, lnb, wglu, watt, wpw, wout)
    return dict(zip([k for k, _, _ in outs] + [k for k, _ in small], res))


def assemble_dproj(du, dqs, dks, dvs, dcv, dgate):
    b, l, _ = du.shape
    nck = BW // 128

    def body(du_ref, q1, q2, q3, k1, k2, k3, v1, v2, v3, cv_ref, g_ref, o_ref):
        o_ref[0, :, 0:BW] = du_ref[0]
        for c in range(nck):
            for j, qr in enumerate((q1, q2, q3)):
                o_ref[0, :, (1 + j) * BW + c * 128:(1 + j) * BW + (c + 1) * 128] = qr[0, c]
            add3 = lambda r1, r2, r3: (r1[0, c].astype(f32) + r2[0, c].astype(f32) + r3[0, c].astype(f32)).astype(bf16)
            o_ref[0, :, 4 * BW + c * 128:4 * BW + (c + 1) * 128] = add3(k1, k2, k3)
            o_ref[0, :, 5 * BW + c * 128:5 * BW + (c + 1) * 128] = add3(v1, v2, v3)
        o_ref[0, :, 6 * BW:8 * BW] = cv_ref[0]
        o_ref[0, :, 8 * BW:] = g_ref[0]

    t = lambda w: pl.BlockSpec((1, TB, w), lambda bi, i: (bi, i, 0))
    ck = pl.BlockSpec((1, nck, TB, 128), lambda bi, i: (bi, 0, i, 0))
    return pl.pallas_call(
        body, name="assemble_dproj", grid=(b, l // TB),
        in_specs=[t(BW)] + [ck] * 9 + [t(2 * BW), t(3 * D)], out_specs=t(INC),
        out_shape=jax.ShapeDtypeStruct((b, l, INC), bf16), compiler_params=_cp(2))(du, *dqs, *dks, *dvs, dcv, dgate)


def _me():
    return lax.axis_index("x"), lax.axis_index("y"), lax.axis_index("c")


def _peers():
    x, y, c = _me()
    return [(x, y, 1 - c), (1 - x, y, c), (1 - x, y, 1 - c), (x, 1 - y, c), (x, 1 - y, 1 - c),
            (1 - x, 1 - y, c), (1 - x, 1 - y, 1 - c)]


def _rank(p):
    return 4 * p[0] + 2 * p[1] + p[2]


def allgather(arrs, name):
    na = len(arrs)
    units = [(a, j) for a in range(na) for j in range(arrs[a].shape[0])]
    nu = len(units)

    def body(*refs):
        ins, outs = refs[:na], refs[na:2 * na]
        send, recv, loc = refs[2 * na:]
        me = _rank(_me())
        local, remote = [], []
        for u, (a, j) in enumerate(units):
            own = pltpu.make_async_copy(ins[a].at[j], outs[a].at[j, me], loc.at[u])
            own.start()
            local.append(own)
        for u, (a, j) in enumerate(units):
            for k, p in enumerate(_peers()):
                cp = pltpu.make_async_remote_copy(src_ref=ins[a].at[j], dst_ref=outs[a].at[j, me],
                                                  send_sem=send.at[u, k], recv_sem=recv.at[u, k],
                                                  device_id=p, device_id_type=MESH)
                cp.start()
                remote.append(cp)
        for cp in local:
            cp.wait()
        for cp in remote:
            cp.wait()

    return pl.pallas_call(
        body, name=name, in_specs=[ANY] * na, out_specs=[ANY] * na,
        out_shape=[jax.ShapeDtypeStruct((a.shape[0], NDEV) + a.shape[1:], a.dtype) for a in arrs],
        scratch_shapes=[pltpu.SemaphoreType.DMA((nu, NDEV - 1)), pltpu.SemaphoreType.DMA((nu, NDEV - 1)),
                        pltpu.SemaphoreType.DMA((nu,))])(*arrs)


HBM = pl.BlockSpec(memory_space=pltpu.HBM)
SEM = pl.BlockSpec(memory_space=pltpu.SEMAPHORE)
_EFFECT = pltpu.SideEffectType.DATAFLOW_SIDE_EFFECTING


def _rank_slot(ref, r):
    if ref.shape[0] == NDEV:
        return ref.at[r]
    n = ref.shape[2] // 2
    return ref.at[r // 2, :, pl.ds(pl.multiple_of((r % 2) * n, 128), n)]


def _push_copies(srcs, lands, send, recv, scatter):
    me = _rank(_me())
    out = []
    for i in range(len(srcs)):
        for k, p in enumerate(_peers()):
            src = _rank_slot(srcs[i], _rank(p)) if scatter else srcs[i]
            dst = lands[i].at[k] if scatter else _rank_slot(lands[i], me)
            j = i * (NDEV - 1) + k
            out.append(pltpu.make_async_remote_copy(src_ref=src, dst_ref=dst, send_sem=send.at[j],
                                                    recv_sem=recv.at[j], device_id=p, device_id_type=MESH))
    return out


def push_start(srcs, lands, scatter, name, token):
    n = len(srcs)
    token = jnp.zeros((8, 128), f32) if token is None else token

    def body(*refs):
        for cp in _push_copies(refs[:n], refs[n:2 * n], refs[2 * n + 1], refs[2 * n + 2], scatter):
            cp.start()
        refs[-1][...] = refs[2 * n][...]

    sems = pltpu.SemaphoreType.DMA((n * (NDEV - 1),))
    vmem = pl.BlockSpec(memory_space=pltpu.VMEM)
    res = pl.pallas_call(
        body, name=name, in_specs=[HBM] * (2 * n) + [vmem], out_specs=[SEM, SEM] + [HBM] * (2 * n) + [vmem],
        out_shape=[sems, sems] + [pltpu.HBM(a.shape, a.dtype) for a in list(srcs) + list(lands)]
        + [jax.ShapeDtypeStruct((8, 128), f32)],
        input_output_aliases={i: 2 + i for i in range(2 * n)},
        compiler_params=pltpu.CompilerParams(has_side_effects=_EFFECT),
    )(*[pltpu.with_memory_space_constraint(a, pltpu.HBM) for a in list(srcs) + list(lands)], token)
    return res[0], res[1], res[2:2 + n], res[2 + n:2 + 2 * n], res[-1]


def push_wait(send, recv, srcs, lands, after, scatter, name):
    n = len(srcs)

    def body(*refs):
        for cp in _push_copies(refs[:n], refs[n:2 * n], refs[2 * n], refs[2 * n + 1], scatter):
            cp.wait_send()
            cp.wait_recv()

    res = pl.pallas_call(
        body, name=name, in_specs=[HBM] * (2 * n) + [SEM, SEM, ANY], out_specs=[HBM] * (2 * n),
        out_shape=[pltpu.HBM(a.shape, a.dtype) for a in list(srcs) + list(lands)],
        input_output_aliases={i: i for i in range(2 * n)},
        compiler_params=pltpu.CompilerParams(has_side_effects=_EFFECT),
    )(*srcs, *lands, send, recv, after)
    return res[:n], res[n:]


_C1 = 1.0 / (1.0 - ADAM_B1 ** ADAM_STEP)
_C2 = 1.0 / (1.0 - ADAM_B2 ** ADAM_STEP)


def _adamw(w, g, m, v):
    m = ADAM_B1 * m + (1.0 - ADAM_B1) * g
    v = ADAM_B2 * v + (1.0 - ADAM_B2) * (g * g)
    delta = -ADAM_LR * ((m * _C1) / (jnp.sqrt(v * _C2) + ADAM_EPS) + ADAM_WD * w)
    return delta, m, v


def adam_big(lands, owns, w, m, v, name):
    _, k, n = lands[0].shape
    tk = k
    while tk * n * 2 * NDEV > 2 * 1024 * 1024 and tk % 16 == 0:
        tk //= 2

    def body(*refs):
        l_refs, o_refs = refs[:DEPTH], refs[DEPTH:2 * DEPTH]
        w_ref, m_ref, v_ref, g_ref, d_ref, nm_ref, nv_ref = refs[2 * DEPTH:]
        for l in range(DEPTH):
            g = o_refs[l][...].astype(f32)
            for s in range(NDEV - 1):
                g = g + l_refs[l][s].astype(f32)
            d, nm, nv = _adamw(w_ref[l], g, m_ref[l], v_ref[l])
            g_ref[l], d_ref[l], nm_ref[l], nv_ref[l] = g, d, nm, nv

    blk = pl.BlockSpec((DEPTH, tk, n), lambda i: (0, i, 0))
    return pl.pallas_call(
        body, name=name, grid=(k // tk,),
        in_specs=[pl.BlockSpec((NDEV - 1, tk, n), lambda i: (0, i, 0))] * DEPTH
        + [pl.BlockSpec((tk, n), lambda i: (i, 0))] * DEPTH + [blk, blk, blk],
        out_specs=[blk] * 4, out_shape=[jax.ShapeDtypeStruct(w.shape, f32)] * 4,
        compiler_params=_cp(1))(*lands, *owns, w, m, v)


def adam_small(gath, w, m, v):
    r = w.shape[0]
    tr = 512

    def body(g_ref, w_ref, m_ref, v_ref, go_ref, d_ref, nm_ref, nv_ref):
        g = g_ref[0]
        for s in range(1, NDEV):
            g = g + g_ref[s]
        d, nm, nv = _adamw(w_ref[...], g, m_ref[...], v_ref[...])
        go_ref[...], d_ref[...], nm_ref[...], nv_ref[...] = g, d, nm, nv

    blk = pl.BlockSpec((tr, 128), lambda i: (i, 0))
    return pl.pallas_call(
        body, name="adam_small", grid=(r // tr,),
        in_specs=[pl.BlockSpec((NDEV, tr, 128), lambda i: (0, i, 0)), blk, blk, blk],
        out_specs=[blk] * 4, out_shape=[jax.ShapeDtypeStruct((r, 128), f32)] * 4,
        compiler_params=_cp(1))(gath, w, m, v)


SMALL = ["norm1_g", "b_gate", "ssm_lambda_re", "ssm_lambda_im", "ssm_log_dt", "ssm_b_re", "ssm_b_im",
         "ssm_c_re", "ssm_c_im", "ssm_d", "conv_w", "conv_b", "conv_ln_g", "conv_ln_b", "norm2_g", "final_g"]
BIG = ["w_in", "w_ssm_glu", "w_att_up", "w_conv_pw2", "w_out", "w_ffn_in", "w_ffn_out"]
ORDER = ["norm1_g", "w_in", "b_gate", "ssm_lambda_re", "ssm_lambda_im", "ssm_log_dt", "ssm_b_re", "ssm_b_im",
         "ssm_c_re", "ssm_c_im", "ssm_d", "w_ssm_glu", "w_att_up", "conv_w", "conv_b", "conv_ln_g", "conv_ln_b",
         "w_conv_pw2", "w_out", "norm2_g", "w_ffn_in", "w_ffn_out", "final_g"]
PACK_ROWS = 2560


def _pack(arrs):
    flat = jnp.concatenate([a.reshape(-1).astype(f32) for a in arrs])
    return jnp.pad(flat, (0, PACK_ROWS * 128 - flat.shape[0])).reshape(PACK_ROWS, 128)


def _unpack(pack, shapes):
    flat = pack.reshape(-1)
    out, off = [], 0
    for s in shapes:
        sz = math.prod(s)
        out.append(flat[off:off + sz].reshape(s))
        off += sz
    return out


def _bt(b):
    return b.transpose(2, 0, 1).reshape(GH, NSTATE)


def _bt_inv(bt):
    return bt.reshape(GH, NG, NS).transpose(1, 2, 0)


def _ct(c):
    return c.transpose(1, 0, 2).reshape(GH, NSTATE)


def _ct_inv(ct):
    return ct.reshape(GH, NG, NS).transpose(1, 0, 2)


def local_step(x, loss_target, P, weights, on_grads, start_token=None):
    bsz, seq, _ = x.shape
    n = bsz * seq

    def natural(g3):
        return g3.transpose(1, 0, 2).reshape(g3.shape[1], NDEV * g3.shape[2])

    tokens = [] if start_token is None else [start_token]

    def after_pushes(a):
        while tokens:
            a = a + tokens.pop()[0:1, 0:1]
        return a

    def pushed(tok):
        if tok is not None:
            tokens.append(tok)

    xs = x.reshape(n, D)
    saved = []
    conv_w_pad = None
    for l in range(DEPTH):
        S = {"x": xs}
        h1 = rms_fwd(xs, after_pushes(P["norm1_g"][l][None]))
        G = dict(weights(l, "in", h1))
        if conv_w_pad is None:
            conv_w_full = G["conv_w"].transpose(1, 2, 0, 3).reshape(DEPTH, CW, BW)
            conv_w_pad = jnp.pad(conv_w_full, ((0, 0), (0, 1), (0, 0)))
        w_in4 = G["w_in"][None]
        proj = inproj(h1, w_in4, 0)
        proj3 = proj.reshape(bsz, seq, INC)
        lr = P["ssm_lambda_re"][l].reshape(1, NSTATE)
        li = P["ssm_lambda_im"][l].reshape(1, NSTATE)
        ld = jnp.repeat(P["ssm_log_dt"][l], NS).reshape(1, NSTATE)
        btr, bti = _bt(P["ssm_b_re"][l]), _bt(P["ssm_b_im"][l])
        t8, bb, cb = s5_params(lr, li, ld, btr, bti, _ct(P["ssm_c_re"][l]), _ct(P["ssm_c_im"][l]))
        dskip = P["ssm_d"][l][None]
        ys, s_re, s_im = s5_fwd(proj3, t8, bb, cb, dskip)
        att = [att_fwd(proj3, gi, dil) for gi, (_, dil) in enumerate(PATTERNS)]
        hc = conv_fwd(proj3, conv_w_pad[l], P["conv_b"][l][None])
        G.update(weights(l, "mix", hc))
        wts = dict(wglu=natural(G["w_ssm_glu"]), watt=natural(G["w_att_up"]),
                   wpw=natural(G["w_conv_pw2"]), wout=G["w_out"].reshape(D, D))
        mi = dict(ys=ys.reshape(n, BW), os_=[a[0].reshape(n, BW) for a in att],
                  lses=[a[1] for a in att], hc=hc.reshape(n, BW),
                  proj=proj, bg=P["b_gate"][l][None], lng=P["conv_ln_g"][l][None], lnb=P["conv_ln_b"][l][None],
                  **wts)
        x1 = merge_fwd(xs, **mi)
        G.update(weights(l, "ffn", x1))
        w_ffn = (G["w_ffn_in"][None], G["w_ffn_out"][None])
        x2, z1s, z2s, h2 = ffn_fwd(x1, P["norm2_g"][l][None], *w_ffn, 0)
        S.update(h1=h1, proj=proj, proj3=proj3, tabs=(s_re, s_im, t8, bb, cb), mi=mi, x1=x1, w_in4=w_in4, w_ffn=w_ffn,
                 zs=(z1s, z2s), h2=h2,
                 sp=(lr, li, ld, btr, bti), dskip=dskip)
        saved.append(S)
        xs = x2

    loss8, dx, dfinal = loss_head(xs, P["final_g"][None], loss_target.reshape(n, D))

    small_g = {k: [None] * DEPTH for k in SMALL if k != "final_g"}
    tokblk = lambda w: pl.BlockSpec((1024, w), lambda s, i: (i, 0))
    colblk = lambda w: pl.BlockSpec((1024, w), lambda s, i: (i, s))
    for l in reversed(range(DEPTH)):
        S = saved[l]
        g2 = P["norm2_g"][l][None]
        dh4, dwa, dwb, dw2 = ffn_bwd(S["h2"], dx, *S["zs"], *S["w_ffn"], 0)
        dx1, dg2 = ffn_bwd_fin(S["x1"], after_pushes(g2), dx, dh4)
        small_g["norm2_g"][l] = dg2
        pushed(on_grads(l, "ffn", dict(w_ffn_in=jnp.concatenate([dwa, dwb], axis=0),
                                       w_ffn_out=dw2.reshape(NDEV, NSH_FF // 2, D))))
        mb = merge_bwd(dx1, **dict(S["mi"], lng=after_pushes(S["mi"]["lng"])))
        small_g["b_gate"][l], small_g["conv_ln_g"][l], small_g["conv_ln_b"][l] = mb["dbg"], mb["dlng"], mb["dlnb"]
        dws = dw_mix(mb["ysin"], mb["dz"], mb["ob"], mb["dya"], mb["hs"], mb["dyc"], mb["merged"], dx1)
        pushed(on_grads(l, "mix", dict(zip(("w_ssm_glu", "w_att_up", "w_conv_pw2", "w_out"), dws))))
        dcv, dcw, dcb = conv_bwd(S["proj3"], mb["dhc"].reshape(bsz, seq, BW), after_pushes(conv_w_pad[l]))
        small_g["conv_w"][l] = dcw[:CW].reshape(CW, NDEV, BW // NDEV).transpose(1, 0, 2)
        small_g["conv_b"][l] = dcb
        ab = [att_bwd(S["proj3"], mb["do"].reshape(bsz, seq, BW), mb["lse_tot"].reshape(bsz, seq, 128),
                      mb["delta"].reshape(bsz, seq, 128), gi, dil) for gi, (_, dil) in enumerate(PATTERNS)]
        du, d_a, d_bb, d_cb, d_d = s5_bwd(S["proj3"], mb["dys"].reshape(bsz, seq, BW), *S["tabs"], S["dskip"])
        lr, li, ld, btr, bti = S["sp"]
        dlr, dli, dld, dbt, dct = s5_params_bwd(lr, li, ld, btr, bti, d_a, d_bb, d_cb)
        small_g["ssm_lambda_re"][l], small_g["ssm_lambda_im"][l] = dlr.reshape(NG, NS), dli.reshape(NG, NS)
        small_g["ssm_log_dt"][l] = dld[0, :NG]
        small_g["ssm_b_re"][l], small_g["ssm_b_im"][l] = _bt_inv(dbt[0]), _bt_inv(dbt[1])
        small_g["ssm_c_re"][l], small_g["ssm_c_im"][l] = _ct_inv(dct[0]), _ct_inv(dct[1])
        small_g["ssm_d"][l] = d_d
        dproj = assemble_dproj(du, [a[0] for a in ab], [a[1] for a in ab], [a[2] for a in ab],
                               dcv, mb["dgate"].reshape(bsz, seq, 3 * D)).reshape(n, INC)
        nblk, wblk = S["w_in4"].shape[1], S["w_in4"].shape[3]
        pushed(on_grads(l, "in", dict(w_in=mm_tn(S["h1"], dproj, tokblk(D), colblk(wblk), nblk, D, wblk, n, "dw_in"))))
        if l == 0:
            pushed(on_grads(l, "small", dict(small_g=small_g, loss8=loss8, dfinal=dfinal)))
        dx, dg1 = inproj_bwd(dproj, S["w_in4"], 0, S["x"], after_pushes(P["norm1_g"][l][None]), dx1)
        small_g["norm1_g"][l] = dg1
    return loss8, dx, dfinal, small_g


def kernel(x, norm1_g, w_in, b_gate, ssm_lambda_re, ssm_lambda_im, ssm_log_dt, ssm_b_re, ssm_b_im, ssm_c_re, ssm_c_im, ssm_d, w_ssm_glu, w_att_up, conv_w, conv_b, conv_ln_g, conv_ln_b, w_conv_pw2, w_out, norm2_g, w_ffn_in, w_ffn_out, final_g, loss_target, m_norm1_g, m_w_in, m_b_gate, m_ssm_lambda_re, m_ssm_lambda_im, m_ssm_log_dt, m_ssm_b_re, m_ssm_b_im, m_ssm_c_re, m_ssm_c_im, m_ssm_d, m_w_ssm_glu, m_w_att_up, m_conv_w, m_conv_b, m_conv_ln_g, m_conv_ln_b, m_w_conv_pw2, m_w_out, m_norm2_g, m_w_ffn_in, m_w_ffn_out, m_final_g, v_norm1_g, v_w_in, v_b_gate, v_ssm_lambda_re, v_ssm_lambda_im, v_ssm_log_dt, v_ssm_b_re, v_ssm_b_im, v_ssm_c_re, v_ssm_c_im, v_ssm_d, v_w_ssm_glu, v_w_att_up, v_conv_w, v_conv_b, v_conv_ln_g, v_conv_ln_b, v_w_conv_pw2, v_w_out, v_norm2_g, v_w_ffn_in, v_w_ffn_out, v_final_g):
    args = dict(locals())
    W = {k: args[k] for k in ORDER}
    M = {k: args["m_" + k] for k in ORDER}
    V = {k: args["v_" + k] for k in ORDER}
    bsz, seq, _ = x.shape
    n = bsz * seq
    me = 4 * lax.axis_index("x") + 2 * lax.axis_index("y") + lax.axis_index("c")

    groups = {"in": ["w_in"], "mix": ["w_ssm_glu", "w_att_up", "w_conv_pw2", "w_out"], "ffn": ["w_ffn_in", "w_ffn_out"]}
    wb = {k: W[k].astype(bf16) for k in BIG}

    def landing(shard, paired=False):
        if paired:
            k_, n_ = shard.shape
            return lax.dynamic_update_slice(lax.empty((NDEV // 2, k_, 2 * n_), shard.dtype), shard[None],
                                            (me // 2, 0, (me % 2) * n_))
        return lax.dynamic_update_index_in_dim(lax.empty((NDEV,) + shard.shape, shard.dtype), shard, me, 0)

    def own_part(by_rank):
        if by_rank.shape[0] == NDEV:
            return lax.dynamic_index_in_dim(by_rank, me, 0, keepdims=False)
        n_ = by_rank.shape[2] // 2
        return lax.dynamic_slice(by_rank, (me // 2, 0, (me % 2) * n_), (1, by_rank.shape[1], n_))[0]

    plan = [("gather_a", [("w_in", 0), ("conv_w", None)]),
            ("gather_b", [(k, 0) for k in groups["mix"] + groups["ffn"]]),
            ("gather_c", [(k, 1) for k in BIG])]
    pending, token = {}, None
    for name, items in plan:
        shards = [conv_w if l is None else wb[k][l] for k, l in items]
        lands = [landing(s, k == "w_in") for (k, _), s in zip(items, shards)]
        send, recv, s_thru, l_thru, token = push_start(shards, lands, False, name, token)
        pending[name] = (send, recv, s_thru, l_thru, items)
    gathered = {}

    def weights(l, group, after):
        name = "gather_c" if l == 1 else ("gather_a" if group == "in" else "gather_b")
        if name in pending:
            send, recv, s_thru, l_thru, items = pending.pop(name)
            for item, arr in zip(items, push_wait(send, recv, s_thru, l_thru, after, False, name + "_wait")[1]):
                gathered[item] = arr
        res = {k: gathered[(k, l)] for k in groups[group]}
        if group == "in":
            res["conv_w"] = gathered[("conv_w", None)]
        return res

    big_g = {k: [None] * DEPTH for k in BIG}
    flights = []

    def start_exchange(items, name):
        parts = [big_g[k][l] for k, l in items]
        part = lambda p: p.shape[1:] if p.shape[0] == NDEV else (p.shape[1], p.shape[2] // 2)
        lands = [lax.empty((NDEV - 1,) + part(p), p.dtype) for p in parts]
        send, recv, s_thru, l_thru, tok = push_start(parts, lands, True, name, None)
        flights.append((send, recv, s_thru, l_thru, items, name))
        return tok

    names = [k for k in SMALL if k != "final_g"]
    shapes = [(DEPTH, NDEV, CW, BW // NDEV) if k == "conv_w" else W[k].shape for k in names] + [(D,), (1,)]
    small_flight = []

    def start_small(small_g, loss8, dfinal):
        sg_ = dict(small_g, norm1_g=[jnp.zeros((1, D), f32), small_g["norm1_g"][1]])
        gpack = _pack([jnp.stack([g.reshape(shapes[i][1:]) for g in sg_[k]]) for i, k in enumerate(names)]
                      + [dfinal, loss8[0, :1]])
        send, recv, s_thru, l_thru, tok = push_start([gpack], [landing(gpack)], False, "gather_small", None)
        small_flight.append((send, recv, s_thru, l_thru))
        return tok

    def on_grads(l, group, grads):
        if group == "small":
            return start_small(**grads)
        for k, g in grads.items():
            big_g[k][l] = g
        if l == 1 and group == "in":
            return start_exchange([(k, 1) for k in BIG], "exchange_l1")
        if l == 0:
            return start_exchange([(k, 0) for k in groups[group]], "exchange_l0_" + group)
        return None

    loss8, dx, dfinal, small_g = local_step(x, loss_target, W, weights, on_grads, token)

    landed, own = {}, {}
    for send, recv, s_thru, l_thru, items, name in flights:
        srcs, lands = push_wait(send, recv, s_thru, l_thru, dx, True, name + "_wait")
        for item, src, land in zip(items, srcs, lands):
            landed[item] = land
            own[item] = own_part(src)
    out = {}
    for k in BIG:
        items = [(k, l) for l in range(DEPTH)]
        out[k] = adam_big([landed[i] for i in items], [own[i] for i in items], W[k], M[k], V[k], "adam_" + k)

    def wpack(src):
        parts = [jnp.broadcast_to(src[k][:, None], shapes[i]) if k == "conv_w" else src[k] for i, k in enumerate(names)]
        return _pack(parts + [src["final_g"], jnp.ones((1,), f32)])

    send, recv, s_thru, l_thru = small_flight[0]
    gall = push_wait(send, recv, s_thru, l_thru, dx, False, "gather_small_wait")[1][0]
    (late,) = allgather([small_g["norm1_g"][0].reshape(1, D // 128, 128)], "allgather_late")
    gall = lax.dynamic_update_slice(gall, late[0], (0, 0, 0))
    sg, sd, sm, sv = [_unpack(p, shapes) for p in adam_small(gall, wpack(W), wpack(M), wpack(V))]
    for i, k in enumerate(names + ["final_g"]):
        vals = [t[i] for t in (sg, sd, sm, sv)]
        if k == "conv_w":
            vals = [lax.dynamic_index_in_dim(t, me, axis=1, keepdims=False) for t in vals]
        out[k] = vals
    loss = sg[-1].reshape(())

    res = [loss, dx.reshape(bsz, seq, D)]
    for j in range(4):
        res += [out[k][j] for k in ORDER]
    return tuple(res)
```

```python
import functools
import math

import jax
import jax.numpy as jnp
from jax import lax
from jax.experimental import pallas as pl
from jax.experimental.pallas import tpu as pltpu

f32 = jnp.float32
bf16 = jnp.bfloat16

D = 1024
DEPTH = 2
EPS = 1e-6
BW = 512
NG = 32
GH = 16
NS = 64
NSTATE = NG * NS
HD = 64
NH = 8
PATTERNS = ((128, 1), (512, 4), (2048, 16))
ABLK = 128
ATT_SCALE = HD ** -0.5
CW = 31
DFF = 2816
INC = 7168
NDEV = 8
NSH_IN = INC // NDEV
NSH_FF = 2 * DFF // NDEV
ADAM_LR, ADAM_B1, ADAM_B2, ADAM_EPS, ADAM_WD, ADAM_STEP = 0.001, 0.9, 0.999, 1e-08, 0.01, 10

TB = 512
SJ = 4
SW = NSTATE // SJ
SU = BW // SJ
NEG = -1e30
MESH = pl.DeviceIdType.MESH
ANY = pl.BlockSpec(memory_space=pl.ANY)


def _cp(n_axes, vmem_mb=48):
    return pltpu.CompilerParams(dimension_semantics=("arbitrary",) * n_axes,
                                vmem_limit_bytes=vmem_mb * 1024 * 1024)


def _dot(a, b):
    return jnp.dot(a, b, preferred_element_type=f32)


def _dot_nt(a, b):
    return lax.dot_general(a, b, (((1,), (1,)), ((), ())), preferred_element_type=f32)


def _dot_tn(a, b):
    return lax.dot_general(a, b, (((0,), (0,)), ((), ())), preferred_element_type=f32)


def _dot_hi(a, b):
    return jnp.dot(a, b, precision=lax.Precision.HIGHEST, preferred_element_type=f32)


def _sigmoid(x):
    return 1.0 / (1.0 + jnp.exp(-x))


_GC = math.sqrt(2.0 / math.pi)


def _gelu(x):
    return 0.5 * x * (1.0 + jnp.tanh(_GC * (x + 0.044715 * x * x * x)))


def _gelu_grad(x):
    t = jnp.tanh(_GC * (x + 0.044715 * x * x * x))
    return 0.5 * (1.0 + t) + 0.5 * x * (1.0 - t * t) * _GC * (1.0 + 3.0 * 0.044715 * x * x)


def _rms_stats(x):
    return lax.rsqrt(jnp.mean(x * x, axis=-1, keepdims=True) + EPS)


def _rms_bwd(x, g, dh):
    r = _rms_stats(x)
    dyg = dh * g
    dx = r * dyg - x * (r * r * r) * jnp.mean(dyg * x, axis=-1, keepdims=True)
    dg = jnp.sum(dh * x * r, axis=0, keepdims=True)
    return dx, dg


def rms_fwd(x, g):
    n = x.shape[0]

    def body(x_ref, g_ref, h_ref):
        xv = x_ref[...]
        h_ref[...] = (xv * _rms_stats(xv) * g_ref[...]).astype(bf16)

    return pl.pallas_call(
        body, name="rms_fwd", grid=(n // TB,),
        in_specs=[pl.BlockSpec((TB, D), lambda i: (i, 0)), pl.BlockSpec((1, D), lambda i: (0, 0))],
        out_specs=pl.BlockSpec((TB, D), lambda i: (i, 0)),
        out_shape=jax.ShapeDtypeStruct((n, D), bf16), compiler_params=_cp(1))(x, g)


def inproj(h, w4, layer):
    n = h.shape[0]
    tm = 1024
    nblk, wblk = w4.shape[1], w4.shape[3]

    def body(h_ref, w_ref, o_ref):
        o_ref[...] = _dot(h_ref[...], w_ref[0, 0]).astype(bf16)

    return pl.pallas_call(
        body, name="inproj", grid=(nblk, n // tm),
        in_specs=[pl.BlockSpec((tm, D), lambda s, i: (i, 0)),
                  pl.BlockSpec((1, 1, D, wblk), lambda s, i: (layer, s, 0, 0))],
        out_specs=pl.BlockSpec((tm, wblk), lambda s, i: (i, s)),
        out_shape=jax.ShapeDtypeStruct((n, INC), bf16), compiler_params=_cp(2))(h, w4)


def inproj_bwd(dproj, w4, layer, x, g, dres):
    n = x.shape[0]
    tm = 1024
    nblk, wblk = w4.shape[1], w4.shape[3]

    def body(dp_ref, w_ref, x_ref, g_ref, dr_ref, dx_ref, dg_ref, acc):
        i, s = pl.program_id(0), pl.program_id(1)

        @pl.when(s == 0)
        def _():
            acc[...] = jnp.zeros_like(acc)

        @pl.when((s == 0) & (i == 0))
        def _():
            dg_ref[...] = jnp.zeros_like(dg_ref)

        acc[...] += _dot_nt(dp_ref[...], w_ref[0, 0])

        @pl.when(s == nblk - 1)
        def _():
            dx, dg = _rms_bwd(x_ref[...], g_ref[...], acc[...])
            dx_ref[...] = dr_ref[...] + dx
            dg_ref[...] += dg

    return pl.pallas_call(
        body, name="inproj_bwd", grid=(n // tm, nblk),
        in_specs=[pl.BlockSpec((tm, wblk), lambda i, s: (i, s)),
                  pl.BlockSpec((1, 1, D, wblk), lambda i, s: (layer, s, 0, 0)),
                  pl.BlockSpec((tm, D), lambda i, s: (i, 0)),
                  pl.BlockSpec((1, D), lambda i, s: (0, 0)),
                  pl.BlockSpec((tm, D), lambda i, s: (i, 0))],
        out_specs=[pl.BlockSpec((tm, D), lambda i, s: (i, 0)), pl.BlockSpec((1, D), lambda i, s: (0, 0))],
        out_shape=[jax.ShapeDtypeStruct((n, D), f32), jax.ShapeDtypeStruct((1, D), f32)],
        scratch_shapes=[pltpu.VMEM((tm, D), f32)], compiler_params=_cp(2))(dproj, w4, x, g, dres)


def mm_tn(a, b, a_spec, b_spec, n_sh, ka, nb, m, name):
    tm = 1024

    def body(a_ref, b_ref, o_ref, acc):
        i = pl.program_id(1)

        @pl.when(i == 0)
        def _():
            acc[...] = jnp.zeros_like(acc)

        av = a_ref[...].reshape(tm, ka).astype(bf16)
        bv = b_ref[...].reshape(tm, nb).astype(bf16)
        acc[...] += _dot_tn(av, bv)

        @pl.when(i == m // tm - 1)
        def _():
            o_ref[0] = acc[...].astype(bf16)

    return pl.pallas_call(
        body, name=name, grid=(n_sh, m // tm), in_specs=[a_spec, b_spec],
        out_specs=pl.BlockSpec((1, ka, nb), lambda s, i: (s, 0, 0)),
        out_shape=jax.ShapeDtypeStruct((n_sh, ka, nb), bf16),
        scratch_shapes=[pltpu.VMEM((ka, nb), f32)], compiler_params=_cp(2))(a, b)


def dw_mix(ysin, dz, ob, dya, hs, dyc, merged, dx1):
    n = ysin.shape[0]
    tm = 512
    pairs = ((BW, 2 * D), (BW, D), (BW, D), (D, D))

    def body(a0, b0, a1, b1, a2, b2, a3, b3, o0, o1, o2, o3, c0, c1, c2, c3):
        i = pl.program_id(0)
        accs = (c0, c1, c2, c3)

        @pl.when(i == 0)
        def _():
            for c in accs:
                c[...] = jnp.zeros_like(c)

        for a, b_, c in zip((a0, a1, a2, a3), (b0, b1, b2, b3), accs):
            c[...] += _dot_tn(a[...], b_[...].astype(bf16))

        @pl.when(i == n // tm - 1)
        def _():
            for s in range(NDEV):
                o0[s] = c0[:, s * 256:(s + 1) * 256].astype(bf16)
                o1[s] = c1[:, s * 128:(s + 1) * 128].astype(bf16)
                o2[s] = c2[:, s * 128:(s + 1) * 128].astype(bf16)
                o3[s] = c3[s * 128:(s + 1) * 128, :].astype(bf16)

    tok = lambda w: pl.BlockSpec((tm, w), lambda i: (i, 0))
    whole = lambda shape: pl.BlockSpec(shape, lambda i: (0, 0, 0))
    outs = [(NDEV, BW, 256), (NDEV, BW, 128), (NDEV, BW, 128), (NDEV, D // NDEV, D)]
    return pl.pallas_call(
        body, name="dw_mix", grid=(n // tm,),
        in_specs=[tok(w) for pair in pairs for w in pair],
        out_specs=[whole(s) for s in outs], out_shape=[jax.ShapeDtypeStruct(s, bf16) for s in outs],
        scratch_shapes=[pltpu.VMEM(p, f32) for p in pairs],
        compiler_params=_cp(1, 56))(ysin, dz, ob, dya, hs, dyc, merged, dx1)


def ffn_fwd(x1, g2, w1, w2, layer):
    n = x1.shape[0]
    w2p = w2.reshape(w2.shape[0], 4, NSH_FF, D)

    def body(x_ref, g_ref, wa_ref, wb_ref, w2_ref, o_ref, z1_ref, z2_ref, h_sc):
        s = pl.program_id(1)

        @pl.when(s == 0)
        def _():
            xv = x_ref[...]
            h_sc[...] = (xv * _rms_stats(xv) * g_ref[...]).astype(bf16)
            o_ref[...] = xv

        h = h_sc[...]
        z1 = _dot(h, wa_ref[0, 0])
        z2 = _dot(h, wb_ref[0, 0])
        z1_ref[0] = z1.astype(bf16)
        z2_ref[0] = z2.astype(bf16)
        a = (z1 * _sigmoid(z1) * z2).astype(bf16)
        o_ref[...] += _dot(a, w2_ref[0, 0])

    tb = 2 * TB
    sh3 = pl.BlockSpec((1, tb, NSH_FF), lambda i, s: (s, i, 0))
    return pl.pallas_call(
        body, name="ffn_fwd", grid=(n // tb, 4),
        in_specs=[pl.BlockSpec((tb, D), lambda i, s: (i, 0)),
                  pl.BlockSpec((1, D), lambda i, s: (0, 0)),
                  pl.BlockSpec((1, 1, D, NSH_FF), lambda i, s: (layer, s, 0, 0)),
                  pl.BlockSpec((1, 1, D, NSH_FF), lambda i, s: (layer, s + 4, 0, 0)),
                  pl.BlockSpec((1, 1, NSH_FF, D), lambda i, s: (layer, s, 0, 0))],
        out_specs=[pl.BlockSpec((tb, D), lambda i, s: (i, 0)), sh3, sh3, pl.BlockSpec((tb, D), lambda i, s: (i, 0))],
        out_shape=[jax.ShapeDtypeStruct((n, D), f32), jax.ShapeDtypeStruct((4, n, NSH_FF), bf16),
                   jax.ShapeDtypeStruct((4, n, NSH_FF), bf16), jax.ShapeDtypeStruct((n, D), bf16)],
        compiler_params=_cp(2))(x1, g2, w1, w1, w2p)


def ffn_bwd(h, dx2, z1s, z2s, w1, w2, layer):
    n = h.shape[0]
    w2p = w2.reshape(w2.shape[0], 4, NSH_FF, D)
    nblk = n // TB

    def body(h_ref, dy_ref, z1_ref, z2_ref, wa_ref, wb_ref, w2_ref, dh_ref, dwa_ref, dwb_ref, dw2_ref,
             acc_a, acc_b, acc_2):
        i = pl.program_id(1)

        @pl.when(i == 0)
        def _():
            acc_a[...] = jnp.zeros_like(acc_a)
            acc_b[...] = jnp.zeros_like(acc_b)
            acc_2[...] = jnp.zeros_like(acc_2)

        hv = h_ref[...]
        dyb = dy_ref[...].astype(bf16)
        z1 = z1_ref[0].astype(f32)
        z2 = z2_ref[0].astype(f32)
        sg = _sigmoid(z1)
        sl = z1 * sg
        a = (sl * z2).astype(bf16)
        da = _dot_nt(dyb, w2_ref[0, 0])
        dz2 = (da * sl).astype(bf16)
        dz1 = (da * z2 * sg * (1.0 + z1 * (1.0 - sg))).astype(bf16)
        dh_ref[0] = (_dot_nt(dz1, wa_ref[0, 0]) + _dot_nt(dz2, wb_ref[0, 0])).astype(bf16)
        acc_a[...] += _dot_tn(hv, dz1)
        acc_b[...] += _dot_tn(hv, dz2)
        acc_2[...] += _dot_tn(a, dyb)

        @pl.when(i == nblk - 1)
        def _():
            dwa_ref[0] = acc_a[...].astype(bf16)
            dwb_ref[0] = acc_b[...].astype(bf16)
            dw2_ref[0] = acc_2[...].astype(bf16)

    tok = pl.BlockSpec((TB, D), lambda s, i: (i, 0))
    sh3 = pl.BlockSpec((1, TB, NSH_FF), lambda s, i: (s, i, 0))
    return pl.pallas_call(
        body, name="ffn_bwd", grid=(4, nblk),
        in_specs=[tok, tok, sh3, sh3,
                  pl.BlockSpec((1, 1, D, NSH_FF), lambda s, i: (layer, s, 0, 0)),
                  pl.BlockSpec((1, 1, D, NSH_FF), lambda s, i: (layer, s + 4, 0, 0)),
                  pl.BlockSpec((1, 1, NSH_FF, D), lambda s, i: (layer, s, 0, 0))],
        out_specs=[pl.BlockSpec((1, TB, D), lambda s, i: (s, i, 0)),
                   pl.BlockSpec((1, D, NSH_FF), lambda s, i: (s, 0, 0)),
                   pl.BlockSpec((1, D, NSH_FF), lambda s, i: (s, 0, 0)),
                   pl.BlockSpec((1, NSH_FF, D), lambda s, i: (s, 0, 0))],
        out_shape=[jax.ShapeDtypeStruct((4, n, D), bf16), jax.ShapeDtypeStruct((4, D, NSH_FF), bf16),
                   jax.ShapeDtypeStruct((4, D, NSH_FF), bf16), jax.ShapeDtypeStruct((4, NSH_FF, D), bf16)],
        scratch_shapes=[pltpu.VMEM((D, NSH_FF), f32), pltpu.VMEM((D, NSH_FF), f32), pltpu.VMEM((NSH_FF, D), f32)],
        compiler_params=_cp(2, 56))(h, dx2, z1s, z2s, w1, w1, w2p)


def norm_bwd_fin(x, g, dres, dh_parts, name):
    n = x.shape[0]
    nparts = dh_parts.shape[0]

    def body(x_ref, g_ref, dy_ref, dh_ref, dx_ref, dg_ref):
        i = pl.program_id(0)

        @pl.when(i == 0)
        def _():
            dg_ref[...] = jnp.zeros_like(dg_ref)

        dh = dh_ref[0].astype(f32)
        for s in range(1, nparts):
            dh = dh + dh_ref[s].astype(f32)
        dx, dg = _rms_bwd(x_ref[...], g_ref[...], dh)
        dx_ref[...] = dy_ref[...] + dx
        dg_ref[...] += dg

    tok = pl.BlockSpec((TB, D), lambda i: (i, 0))
    return pl.pallas_call(
        body, name=name, grid=(n // TB,),
        in_specs=[tok, pl.BlockSpec((1, D), lambda i: (0, 0)), tok, pl.BlockSpec((nparts, TB, D), lambda i: (0, i, 0))],
        out_specs=[tok, pl.BlockSpec((1, D), lambda i: (0, 0))],
        out_shape=[jax.ShapeDtypeStruct((n, D), f32), jax.ShapeDtypeStruct((1, D), f32)],
        compiler_params=_cp(1))(x, g, dres, dh_parts)


def loss_head(x, g, target):
    n = x.shape[0]

    def body(x_ref, g_ref, t_ref, l_ref, dx_ref, dg_ref):
        i = pl.program_id(0)

        @pl.when(i == 0)
        def _():
            l_ref[...] = jnp.zeros_like(l_ref)
            dg_ref[...] = jnp.zeros_like(dg_ref)

        xv = x_ref[...]
        y = xv * _rms_stats(xv) * g_ref[...]
        e = y - t_ref[...]
        l_ref[...] += 0.5 * jnp.sum(jnp.sum(e * e, axis=-1, keepdims=True), axis=0, keepdims=True) * (1.0 / D)
        dx, dg = _rms_bwd(xv, g_ref[...], e * (1.0 / D))
        dx_ref[...] = dx
        dg_ref[...] += dg

    tok = lambda i: (i, 0)
    return pl.pallas_call(
        body, name="loss_head", grid=(n // TB,),
        in_specs=[pl.BlockSpec((TB, D), tok), pl.BlockSpec((1, D), lambda i: (0, 0)), pl.BlockSpec((TB, D), tok)],
        out_specs=[pl.BlockSpec((8, 128), lambda i: (0, 0)), pl.BlockSpec((TB, D), tok),
                   pl.BlockSpec((1, D), lambda i: (0, 0))],
        out_shape=[jax.ShapeDtypeStruct((8, 128), f32), jax.ShapeDtypeStruct((n, D), f32),
                   jax.ShapeDtypeStruct((1, D), f32)],
        compiler_params=_cp(1))(x, g, target)


def _disc(lr, li, ld):
    dt = jnp.exp(ld)
    mag = jnp.exp(lr * dt)
    ar = mag * jnp.cos(li * dt)
    ai = mag * jnp.sin(li * dt)
    nr, ni = ar - 1.0, ai
    den = lr * lr + li * li
    zr = (nr * lr + ni * li) / den
    zi = (ni * lr - nr * li) / den
    return ar, ai, zr, zi


def _blockdiag_mask(shape):
    r = lax.broadcasted_iota(jnp.int32, shape, 0) // GH
    c = lax.broadcasted_iota(jnp.int32, shape, 1) // NS
    return r == c


def s5_params(lr, li, ld, btr, bti, ctr, cti):
    def body(lr_ref, li_ref, ld_ref, btr_ref, bti_ref, ctr_ref, cti_ref, t8_ref, bb_ref, cb_ref):
        ar, ai, zr, zi = _disc(lr_ref[...], li_ref[...], ld_ref[...])
        pr_, pi_ = ar, ai
        pw2 = []
        for k in range(4):
            pw2.append((pr_, pi_))
            pr_, pi_ = pr_ * pr_ - pi_ * pi_, 2.0 * pr_ * pi_
        cm = lambda p, q: (p[0] * q[0] - p[1] * q[1], p[0] * q[1] + p[1] * q[0])
        pw = {1: pw2[0], 2: pw2[1], 4: pw2[2], 8: pw2[3]}
        pw[3], pw[5], pw[6] = cm(pw[2], pw[1]), cm(pw[4], pw[1]), cm(pw[4], pw[2])
        pw[7] = cm(pw[4], pw[3])
        row = lax.broadcasted_iota(jnp.int32, (8, NSTATE), 0)
        zero = jnp.zeros((8, NSTATE), f32)
        for c in range(2):
            for k in range(3):
                full = jnp.broadcast_to(pw2[k][c], (8, NSTATE))
                t8_ref[c, k] = jnp.where(row >= (1 << k), full, 0.0)
                t8_ref[c, 3 + k] = jnp.where(row + (1 << k) < 8, full, 0.0)
            up, down = zero, zero
            for j in range(8):
                up = up + jnp.where(row == j, pw[j + 1][c], 0.0)
                down = down + jnp.where(row == j, pw[8 - j][c], 0.0)
            t8_ref[c, 6] = up
            t8_ref[c, 7] = down
        bbr = zr * btr_ref[...] - zi * bti_ref[...]
        bbi = zr * bti_ref[...] + zi * btr_ref[...]
        mask = _blockdiag_mask((SU, SW))
        for j in range(SJ):
            cols = slice(j * SW, (j + 1) * SW)
            for c, (vb, vc) in enumerate(((bbr, ctr_ref[...]), (bbi, cti_ref[...]))):
                bb_ref[c, j] = jnp.where(mask, jnp.tile(vb[:, cols], (SU // GH, 1)), 0.0).astype(bf16)
                cb_ref[c, j] = jnp.where(mask, jnp.tile(vc[:, cols], (SU // GH, 1)), 0.0).astype(bf16)

    return pl.pallas_call(
        body, name="s5_params",
        out_shape=[jax.ShapeDtypeStruct((2, 8, 8, NSTATE), f32),
                   jax.ShapeDtypeStruct((2, SJ, SU, SW), bf16), jax.ShapeDtypeStruct((2, SJ, SU, SW), bf16)],
        compiler_params=pltpu.CompilerParams(vmem_limit_bytes=56 * 1024 * 1024))(lr, li, ld, btr, bti, ctr, cti)


def s5_params_bwd(lr, li, ld, btr, bti, d_a, d_bb, d_cb):
    def body(lr_ref, li_ref, ld_ref, btr_ref, bti_ref, da_ref, dbb_ref, dcb_ref,
             dlr_ref, dli_ref, dld_ref, dbt_ref, dct_ref):
        mask = _blockdiag_mask((SU, SW))

        def fold(ref, c):
            parts = []
            for j in range(SJ):
                v = jnp.where(mask, ref[c, j], 0.0)
                parts.append(v.reshape(SU // GH, GH, SW).sum(axis=0))
            return jnp.concatenate(parts, axis=1)

        dct_ref[0] = fold(dcb_ref, 0)
        dct_ref[1] = fold(dcb_ref, 1)
        dbbr, dbbi = fold(dbb_ref, 0), fold(dbb_ref, 1)
        lrv, liv, ldv = lr_ref[...], li_ref[...], ld_ref[...]
        (ar, ai, zr, zi), vjp = jax.vjp(_disc, lrv, liv, ldv)
        btr, bti = btr_ref[...], bti_ref[...]
        dbt_ref[0] = zr * dbbr + zi * dbbi
        dbt_ref[1] = zr * dbbi - zi * dbbr
        dzr = jnp.sum(dbbr * btr + dbbi * bti, axis=0, keepdims=True)
        dzi = jnp.sum(dbbi * btr - dbbr * bti, axis=0, keepdims=True)
        dlr, dli, dld = vjp((da_ref[0:1, :], da_ref[1:2, :], dzr, dzi))
        dlr_ref[...] = dlr
        dli_ref[...] = dli
        ind = (lax.broadcasted_iota(jnp.int32, (NSTATE, 128), 0) // NS
               == lax.broadcasted_iota(jnp.int32, (NSTATE, 128), 1)).astype(f32)
        dld_ref[...] = _dot_hi(jnp.broadcast_to(dld, (8, NSTATE)), ind)

    return pl.pallas_call(
        body, name="s5_params_bwd",
        out_shape=[jax.ShapeDtypeStruct((1, NSTATE), f32), jax.ShapeDtypeStruct((1, NSTATE), f32),
                   jax.ShapeDtypeStruct((8, 128), f32), jax.ShapeDtypeStruct((2, GH, NSTATE), f32),
                   jax.ShapeDtypeStruct((2, GH, NSTATE), f32)],
        compiler_params=pltpu.CompilerParams(vmem_limit_bytes=56 * 1024 * 1024))(lr, li, ld, btr, bti, d_a, d_bb, d_cb)


def _fma(sr, si, ar, ai, qr, qi):
    return sr + ar * qr - ai * qi, si + ar * qi + ai * qr


def _scan_tile(sr, si, cr, ci, t8_ref, reverse):
    sg = -1.0 if reverse else 1.0
    for k in range(3):
        tk = 3 + k if reverse else k
        rot = 8 - (1 << k) if reverse else 1 << k
        sr, si = _fma(sr, si, t8_ref[0, tk], sg * t8_ref[1, tk], pltpu.roll(sr, rot, 0), pltpu.roll(si, rot, 0))
    tp = 7 if reverse else 6
    sr, si = _fma(sr, si, t8_ref[0, tp], sg * t8_ref[1, tp], cr, ci)
    e = 0 if reverse else 7
    return sr, si, jnp.broadcast_to(sr[e:e + 1, :], sr.shape), jnp.broadcast_to(si[e:e + 1, :], si.shape)


S5MC = 512


def _s5_input_map(u_ref, bb_ref, sr_sc, si_sc, l):
    for c in range(l // S5MC):
        rows = slice(c * S5MC, (c + 1) * S5MC)
        u = u_ref[0, rows, :]
        sr_sc[rows, :] = _dot(u, bb_ref[0, 0])
        si_sc[rows, :] = _dot(u, bb_ref[1, 0])


def _s5_forward_scan(sr_sc, si_sc, t8_ref, l):
    def step(k, carry):
        rows = pl.ds(pl.multiple_of(k * 8, 8), 8)
        sr, si, cr, ci = _scan_tile(sr_sc[rows, :], si_sc[rows, :], carry[0], carry[1], t8_ref, False)
        sr_sc[rows, :] = sr
        si_sc[rows, :] = si
        return cr, ci

    zero = jnp.zeros((8, SW), f32)
    lax.fori_loop(0, l // 8, step, (zero, zero), unroll=4)


def s5_fwd(proj3, t8, bb, cb, dskip):
    b, l, _ = proj3.shape

    def body(u_ref, t8_ref, bb_ref, cb_ref, d_ref, y_ref, sr_out, si_out):
        sr_sc, si_sc = sr_out.at[0], si_out.at[0]
        _s5_input_map(u_ref, bb_ref, sr_sc, si_sc, l)
        _s5_forward_scan(sr_sc, si_sc, t8_ref, l)
        for c in range(l // S5MC):
            rows = slice(c * S5MC, (c + 1) * S5MC)
            y = (_dot_nt(sr_sc[rows, :].astype(bf16), cb_ref[0, 0])
                 - _dot_nt(si_sc[rows, :].astype(bf16), cb_ref[1, 0]))
            y_ref[0, rows, :] = y + d_ref[...] * u_ref[0, rows, :].astype(f32)

    return pl.pallas_call(
        body, name="s5_fwd", grid=(SJ, b),
        in_specs=[pl.BlockSpec((1, l, SU), lambda j, bi: (bi, 0, j)),
                  pl.BlockSpec((2, 8, 8, SW), lambda j, bi: (0, 0, 0, j)),
                  pl.BlockSpec((2, 1, SU, SW), lambda j, bi: (0, j, 0, 0)),
                  pl.BlockSpec((2, 1, SU, SW), lambda j, bi: (0, j, 0, 0)),
                  pl.BlockSpec((1, SU), lambda j, bi: (0, j))],
        out_specs=[pl.BlockSpec((1, l, SU), lambda j, bi: (bi, 0, j)),
                   pl.BlockSpec((1, l, SW), lambda j, bi: (bi, 0, j)),
                   pl.BlockSpec((1, l, SW), lambda j, bi: (bi, 0, j))],
        out_shape=[jax.ShapeDtypeStruct((b, l, BW), f32), jax.ShapeDtypeStruct((b, l, NSTATE), f32),
                   jax.ShapeDtypeStruct((b, l, NSTATE), f32)],
        compiler_params=_cp(2))(proj3, t8, bb, cb, dskip)


def s5_bwd(proj3, dy, s_re, s_im, t8, bb, cb, dskip):
    b, l, _ = proj3.shape
    nt = l // 8

    def body(u_ref, dy_ref, sr_in, si_in, t8_ref, bb_ref, cb_ref, d_ref,
             du_ref, da_ref, dbb_ref, dcb_ref, dd_ref, gr_sc, gi_sc):
        bi = pl.program_id(1)
        sr_sc, si_sc = sr_in.at[0], si_in.at[0]

        @pl.when(bi == 0)
        def _():
            da_ref[...] = jnp.zeros_like(da_ref)
            dbb_ref[...] = jnp.zeros_like(dbb_ref)
            dcb_ref[...] = jnp.zeros_like(dcb_ref)
            dd_ref[...] = jnp.zeros_like(dd_ref)

        for c in range(l // S5MC):
            rows = slice(c * S5MC, (c + 1) * S5MC)
            dyb = dy_ref[0, rows, :].astype(bf16)
            gr_sc[rows, :] = _dot(dyb, cb_ref[0, 0])
            gi_sc[rows, :] = -_dot(dyb, cb_ref[1, 0])

        row = lax.broadcasted_iota(jnp.int32, (8, SW), 0)

        def step(i, carry):
            cr, ci, dar, dai = carry
            k = nt - 1 - i
            rows = pl.ds(pl.multiple_of(k * 8, 8), 8)
            gr, gi, cr, ci = _scan_tile(gr_sc[rows, :], gi_sc[rows, :], cr, ci, t8_ref, True)
            gr_sc[rows, :] = gr
            gi_sc[rows, :] = gi
            before = pl.ds(pl.multiple_of(jnp.maximum(k - 1, 0) * 8, 8), 8)
            live = jnp.where(k > 0, 1.0, 0.0)
            sr, si = sr_sc[rows, :], si_sc[rows, :]
            spr = jnp.where(row == 0, live * sr_sc[before, :][7:8, :], pltpu.roll(sr, 1, 0))
            spi = jnp.where(row == 0, live * si_sc[before, :][7:8, :], pltpu.roll(si, 1, 0))
            return cr, ci, dar + spr * gr + spi * gi, dai + spr * gi - spi * gr

        zero = jnp.zeros((8, SW), f32)
        _, _, dar, dai = lax.fori_loop(0, nt, step, (zero, zero, zero, zero), unroll=2)
        da_ref[0:1, :] += jnp.sum(dar, axis=0, keepdims=True)
        da_ref[1:2, :] += jnp.sum(dai, axis=0, keepdims=True)

        for c in range(l // S5MC):
            rows = slice(c * S5MC, (c + 1) * S5MC)
            u = u_ref[0, rows, :]
            dyv = dy_ref[0, rows, :]
            dyb = dyv.astype(bf16)
            grb, gib = gr_sc[rows, :].astype(bf16), gi_sc[rows, :].astype(bf16)
            dcb_ref[0, 0] += _dot_tn(dyb, sr_sc[rows, :].astype(bf16))
            dcb_ref[1, 0] -= _dot_tn(dyb, si_sc[rows, :].astype(bf16))
            dbb_ref[0, 0] += _dot_tn(u, grb)
            dbb_ref[1, 0] += _dot_tn(u, gib)
            du = _dot_nt(grb, bb_ref[0, 0]) + _dot_nt(gib, bb_ref[1, 0]) + d_ref[...] * dyv
            du_ref[0, rows, :] = du.astype(bf16)
            dd_ref[...] += jnp.sum(dyv * u.astype(f32), axis=0, keepdims=True)

    seq = pl.BlockSpec((1, l, SU), lambda j, bi: (bi, 0, j))
    sts = pl.BlockSpec((1, l, SW), lambda j, bi: (bi, 0, j))
    tab = pl.BlockSpec((2, 1, SU, SW), lambda j, bi: (0, j, 0, 0))
    return pl.pallas_call(
        body, name="s5_bwd", grid=(SJ, b),
        in_specs=[seq, seq, sts, sts, pl.BlockSpec((2, 8, 8, SW), lambda j, bi: (0, 0, 0, j)), tab, tab,
                  pl.BlockSpec((1, SU), lambda j, bi: (0, j))],
        out_specs=[seq, pl.BlockSpec((2, SW), lambda j, bi: (0, j)), tab, tab,
                   pl.BlockSpec((1, SU), lambda j, bi: (0, j))],
        out_shape=[jax.ShapeDtypeStruct((b, l, BW), bf16), jax.ShapeDtypeStruct((2, NSTATE), f32),
                   jax.ShapeDtypeStruct((2, SJ, SU, SW), f32), jax.ShapeDtypeStruct((2, SJ, SU, SW), f32),
                   jax.ShapeDtypeStruct((1, BW), f32)],
        scratch_shapes=[pltpu.VMEM((l, SW), f32)] * 2,
        compiler_params=_cp(2))(proj3, dy, s_re, s_im, t8, bb, cb, dskip)


AHC = 2
AHW = AHC * 128


def _att_mask(n, nb):
    if nb == 1:
        qi = lax.broadcasted_iota(jnp.int32, (ABLK, ABLK), 0)
        kj = lax.broadcasted_iota(jnp.int32, (ABLK, ABLK), 1)
        return kj <= qi
    qi = lax.broadcasted_iota(jnp.int32, (ABLK, 2 * ABLK), 0)
    kj = lax.broadcasted_iota(jnp.int32, (ABLK, 2 * ABLK), 1)
    return (kj >= qi) & (kj <= qi + ABLK) & ((n > 0) | (kj >= ABLK))


def _att_rows(it, nb, dil):
    r, n = it // nb, it % nb
    cur = pl.ds(r + n * (ABLK * dil), ABLK, stride=dil)
    prv = pl.ds(r + jnp.maximum(n - 1, 0) * (ABLK * dil), ABLK, stride=dil)
    return n, cur, prv


def _att_keys(ref, c, cur, prv, nb):
    if nb == 1:
        x = ref[c, cur, :].astype(bf16)
    else:
        x = jnp.concatenate([ref[c, prv, :], ref[c, cur, :]], axis=0).astype(bf16)
    head0 = lax.broadcasted_iota(jnp.int32, x.shape, 1) < HD
    zero = jnp.zeros_like(x)
    return jnp.concatenate([jnp.where(head0, x, zero), jnp.where(head0, zero, x)], axis=0)


def _per_head(nk, a0, a1):
    col = lax.broadcasted_iota(jnp.int32, (ABLK, 2 * nk), 1)
    return jnp.where(col < nk, a0, a1)


def _to_chunks(src_ref, dst):
    for c in range(AHC):
        dst[c] = src_ref[0, :, c * 128:(c + 1) * 128].astype(f32)


def att_fwd(proj3, g_idx, dil):
    b, l, _ = proj3.shape
    nb = l // dil // ABLK
    nhalf = BW // AHW

    def body(q_ref, k_ref, v_ref, o_ref, lse_ref, qf, kf, vf, of):
        hh = pl.program_id(1)
        _to_chunks(q_ref, qf)
        _to_chunks(k_ref, kf)
        _to_chunks(v_ref, vf)
        lane = lax.broadcasted_iota(jnp.int32, (ABLK, 128), 1)

        def step(it, carry):
            n, cur, prv = _att_rows(it, nb, dil)
            valid = _att_mask(n, nb)
            valid = jnp.concatenate([valid, valid], axis=1)
            nk = valid.shape[1] // 2
            lse_all = jnp.zeros((ABLK, 128), f32)
            for c in range(AHC):
                q = (qf[c, cur, :] * ATT_SCALE).astype(bf16)
                k = _att_keys(kf, c, cur, prv, nb)
                v = _att_keys(vf, c, cur, prv, nb)
                s = jnp.where(valid, _dot_nt(q, k), NEG)
                m0 = jnp.max(s[:, :nk], axis=-1, keepdims=True)
                m1 = jnp.max(s[:, nk:], axis=-1, keepdims=True)
                p = jnp.exp(s - _per_head(nk, m0, m1))
                den0 = jnp.sum(p[:, :nk], axis=-1, keepdims=True)
                den1 = jnp.sum(p[:, nk:], axis=-1, keepdims=True)
                of[c, cur, :] = _dot(p.astype(bf16), v) * jnp.where(lane < HD, 1.0 / den0, 1.0 / den1)
                head = hh * (2 * AHC) + 2 * c
                lse_all = (lse_all + jnp.where(lane == head, m0 + jnp.log(den0), 0.0)
                           + jnp.where(lane == head + 1, m1 + jnp.log(den1), 0.0))

            lse_ref[0, 0, cur, :] = lse_all
            return carry

        lax.fori_loop(0, dil * nb, step, 0, unroll=4)
        for c in range(AHC):
            o_ref[0, :, c * 128:(c + 1) * 128] = of[c].astype(bf16)

    col = lambda c: pl.BlockSpec((1, l, AHW), lambda bi, hh: (bi, 0, c * nhalf + hh))
    return pl.pallas_call(
        body, name=f"att_fwd{g_idx}", grid=(b, nhalf),
        in_specs=[col(1 + g_idx), col(4), col(5)],
        out_specs=[pl.BlockSpec((1, l, AHW), lambda bi, hh: (bi, 0, hh)),
                   pl.BlockSpec((1, 1, l, 128), lambda bi, hh: (bi, hh, 0, 0))],
        out_shape=[jax.ShapeDtypeStruct((b, l, BW), bf16), jax.ShapeDtypeStruct((b, nhalf, l, 128), f32)],
        scratch_shapes=[pltpu.VMEM((AHC, l, 128), f32)] * 4,
        compiler_params=_cp(2))(proj3, proj3, proj3)


def att_bwd(proj3, do, lse_tot, delta, g_idx, dil):
    b, l, _ = proj3.shape
    nb = l // dil // ABLK
    nhalf = BW // AHW

    def body(q_ref, k_ref, v_ref, do_ref, l_ref, dl_ref, dq_out, dk_out, dv_out, qf, kf, vf, dof,
             dq_ref, dk_ref, dv_ref):
        hh = pl.program_id(1)
        _to_chunks(q_ref, qf)
        _to_chunks(k_ref, kf)
        _to_chunks(v_ref, vf)
        _to_chunks(do_ref, dof)
        dk_ref[...] = jnp.zeros_like(dk_ref)
        dv_ref[...] = jnp.zeros_like(dv_ref)
        lane = lax.broadcasted_iota(jnp.int32, (ABLK, 128), 1)

        def step(it, carry):
            n, cur, prv = _att_rows(it, nb, dil)
            valid = _att_mask(n, nb)
            valid = jnp.concatenate([valid, valid], axis=1)
            nk = valid.shape[1] // 2
            lse_b = l_ref[0, cur, :]
            dl_b = dl_ref[0, cur, :]
            head0 = lax.broadcasted_iota(jnp.int32, (nk, 128), 1) < HD
            for c in range(AHC):
                q = (qf[c, cur, :] * ATT_SCALE).astype(bf16)
                dob = dof[c, cur, :].astype(bf16)
                k = _att_keys(kf, c, cur, prv, nb)
                v = _att_keys(vf, c, cur, prv, nb)
                head = hh * (2 * AHC) + 2 * c
                pick = lambda a, h: jnp.sum(jnp.where(lane == h, a, 0.0), axis=-1, keepdims=True)
                lse_h = _per_head(nk, pick(lse_b, head), pick(lse_b, head + 1))
                dl_h = _per_head(nk, pick(dl_b, head), pick(dl_b, head + 1))
                s = _dot_nt(q, k)
                p = jnp.where(valid, jnp.exp(jnp.minimum(s - lse_h, 60.0)), 0.0)
                ds = (p * (_dot_nt(dob, v) - dl_h)).astype(bf16)
                dq_ref[0, c, cur, :] = _dot(ds, k) * ATT_SCALE
                dk2 = _dot_tn(ds, q)
                dv2 = _dot_tn(p.astype(bf16), dob)
                dk = jnp.where(head0, dk2[:nk], dk2[nk:])
                dv = jnp.where(head0, dv2[:nk], dv2[nk:])
                if nb == 1:
                    dk_ref[0, c, cur, :] += dk
                    dv_ref[0, c, cur, :] += dv
                else:
                    dk_ref[0, c, cur, :] += dk[ABLK:]
                    dv_ref[0, c, cur, :] += dv[ABLK:]
                    dk_ref[0, c, prv, :] += dk[:ABLK]
                    dv_ref[0, c, prv, :] += dv[:ABLK]

            return carry

        lax.fori_loop(0, dil * nb, step, 0, unroll=4)
        dq_out[0] = dq_ref[0].astype(bf16)
        dk_out[0] = dk_ref[0].astype(bf16)
        dv_out[0] = dv_ref[0].astype(bf16)

    col = lambda c: pl.BlockSpec((1, l, AHW), lambda bi, hh: (bi, 0, c * nhalf + hh))
    own = pl.BlockSpec((1, l, AHW), lambda bi, hh: (bi, 0, hh))
    own128 = pl.BlockSpec((1, l, 128), lambda bi, hh: (bi, 0, 0))
    chunked = pl.BlockSpec((1, AHC, l, 128), lambda bi, hh: (bi, hh, 0, 0))
    return pl.pallas_call(
        body, name=f"att_bwd{g_idx}", grid=(b, nhalf),
        in_specs=[col(1 + g_idx), col(4), col(5), own, own128, own128],
        out_specs=[chunked] * 3,
        out_shape=[jax.ShapeDtypeStruct((b, BW // 128, l, 128), bf16)] * 3,
        scratch_shapes=[pltpu.VMEM((AHC, l, 128), f32)] * 4 + [pltpu.VMEM((1, AHC, l, 128), f32)] * 3,
        compiler_params=_cp(2, 56))(proj3, proj3, proj3, do, lse_tot, delta)


CPAD = 32
CTAIL = 16
CR = 128
CSLAB = CR + 40


def _tap_windows(slab, off, mis):
    ntap = (CW - 1 - mis) // 8 + 1
    rot = (off + mis) % 8
    base = off + mis - rot
    shifted = pltpu.roll(slab, CSLAB - rot, 0) if rot else slab
    for a in range(ntap):
        yield 8 * a + mis, shifted[base + 8 * a:base + 8 * a + CR]


def _fill_glu(cv_ref, pad, l):
    pad[0:CPAD, :] = jnp.zeros((CPAD, BW), f32)
    pad[CPAD:CPAD + l, :] = cv_ref[0, :, :BW].astype(f32) * _sigmoid(cv_ref[0, :, BW:].astype(f32))
    pad[CPAD + l:, :] = jnp.zeros((CTAIL, BW), f32)


def conv_fwd(proj3, cw, cb):
    b, l, _ = proj3.shape

    def body(cv_ref, w_ref, b_ref, o_ref, pad):
        _fill_glu(cv_ref, pad, l)
        for lc in range(BW // 128):
            lanes = slice(lc * 128, (lc + 1) * 128)
            wv = w_ref[:, lanes]

            def step(c, carry):
                base = pl.multiple_of(c * CR, CR)
                slab = pad[pl.ds(base, CSLAB), lanes]
                acc = jnp.zeros((CR, 128), f32) + b_ref[:, lanes]
                for mis in range(8):
                    for k, win in _tap_windows(slab, CPAD - (CW - 1), mis):
                        acc = acc + wv[k:k + 1] * win
                o_ref[0, pl.ds(base, CR), lanes] = acc
                return carry

            lax.fori_loop(0, l // CR, step, 0)

    return pl.pallas_call(
        body, name="conv_fwd", grid=(b,),
        in_specs=[pl.BlockSpec((1, l, 2 * BW), lambda i: (i, 0, 3)),
                  pl.BlockSpec((32, BW), lambda i: (0, 0)), pl.BlockSpec((1, BW), lambda i: (0, 0))],
        out_specs=pl.BlockSpec((1, l, BW), lambda i: (i, 0, 0)),
        out_shape=jax.ShapeDtypeStruct((b, l, BW), f32),
        scratch_shapes=[pltpu.VMEM((CPAD + l + CTAIL, BW), f32)], compiler_params=_cp(1))(proj3, cw, cb)


def conv_bwd(proj3, dhc, cw):
    b, l, _ = proj3.shape

    def body(cv_ref, d_ref, w_ref, dcv_ref, dw_ref, db_ref, pad, dpad):
        i = pl.program_id(0)

        @pl.when(i == 0)
        def _():
            dw_ref[...] = jnp.zeros_like(dw_ref)
            db_ref[...] = jnp.zeros_like(db_ref)

        _fill_glu(cv_ref, pad, l)
        dpad[0:l, :] = d_ref[0]
        dpad[l:, :] = jnp.zeros((CPAD + CTAIL, BW), f32)
        db_ref[...] += jnp.sum(d_ref[0], axis=0, keepdims=True)
        for lc in range(BW // 128):
            lanes = slice(lc * 128, (lc + 1) * 128)
            glanes = slice(BW + lc * 128, BW + (lc + 1) * 128)
            wv = w_ref[:, lanes]

            for mis in range(8):
                ntap = (CW - 1 - mis) // 8 + 1

                def dw_step(c, accs, mis=mis, lanes=lanes):
                    base = pl.multiple_of(c * CR, CR)
                    slab = pad[pl.ds(base, CSLAB), lanes]
                    dv = dpad[pl.ds(base, CR), lanes]
                    return tuple(acc + (dv * win).reshape(CR // 8, 8, 128).sum(axis=0) for acc, (_, win)
                                 in zip(accs, _tap_windows(slab, CPAD - (CW - 1), mis)))

                accs = lax.fori_loop(0, l // CR, dw_step, tuple(jnp.zeros((8, 128), f32) for _ in range(ntap)))
                for a in range(ntap):
                    k = 8 * a + mis
                    dw_ref[k:k + 1, lanes] += jnp.sum(accs[a], axis=0, keepdims=True)

            def dh_step(c, carry, lanes=lanes, glanes=glanes, wv=wv):
                base = pl.multiple_of(c * CR, CR)
                slab = dpad[pl.ds(base, CSLAB), lanes]
                acc = jnp.zeros((CR, 128), f32)
                for mis in range(8):
                    for kk, win in _tap_windows(slab, 0, mis):
                        acc = acc + wv[CW - 1 - kk:CW - kk] * win
                rows = pl.ds(base, CR)
                a = cv_ref[0, rows, lanes].astype(f32)
                sg = _sigmoid(cv_ref[0, rows, glanes].astype(f32))
                dcv_ref[0, rows, lanes] = (acc * sg).astype(bf16)
                dcv_ref[0, rows, glanes] = (acc * a * sg * (1.0 - sg)).astype(bf16)
                return carry

            lax.fori_loop(0, l // CR, dh_step, 0)

    return pl.pallas_call(
        body, name="conv_bwd", grid=(b,),
        in_specs=[pl.BlockSpec((1, l, 2 * BW), lambda i: (i, 0, 3)),
                  pl.BlockSpec((1, l, BW), lambda i: (i, 0, 0)),
                  pl.BlockSpec((32, BW), lambda i: (0, 0))],
        out_specs=[pl.BlockSpec((1, l, 2 * BW), lambda i: (i, 0, 0)),
                   pl.BlockSpec((32, BW), lambda i: (0, 0)), pl.BlockSpec((1, BW), lambda i: (0, 0))],
        out_shape=[jax.ShapeDtypeStruct((b, l, 2 * BW), bf16), jax.ShapeDtypeStruct((32, BW), f32),
                   jax.ShapeDtypeStruct((1, BW), f32)],
        scratch_shapes=[pltpu.VMEM((CPAD + l + CTAIL, BW), f32), pltpu.VMEM((l + CPAD + CTAIL, BW), f32)],
        compiler_params=_cp(1))(proj3, dhc, cw)


def _head_expand():
    r = lax.broadcasted_iota(jnp.int32, (128, BW), 0)
    c = lax.broadcasted_iota(jnp.int32, (128, BW), 1) // HD
    return (r == c).astype(f32)


def _head_reduce():
    r = lax.broadcasted_iota(jnp.int32, (BW, 128), 0) // HD
    c = lax.broadcasted_iota(jnp.int32, (BW, 128), 1)
    return (r == c).astype(f32)


def _merge_common(ys_ref, o_refs, l_refs, hc_ref, g_refs, bg_ref, lng_ref, lnb_ref, wglu_ref, watt_ref, wpw_ref):
    r = {}
    ysv = ys_ref[...]
    r["ys"] = ysv
    r["ysin"] = _gelu(ysv).astype(bf16)
    z = _dot(r["ysin"], wglu_ref[...])
    r["z1"], r["sg2"] = z[:, :D], _sigmoid(z[:, D:])
    r["y_s"] = r["z1"] * r["sg2"]
    ls = [lr_[0, 0] + lr_[0, 1] for lr_ in l_refs]
    mx = jnp.maximum(jnp.maximum(ls[0], ls[1]), ls[2])
    es = [jnp.exp(v - mx) for v in ls]
    tot = es[0] + es[1] + es[2]
    r["lse_tot"] = mx + jnp.log(tot)
    e_mat = _head_expand()
    o = jnp.zeros(ysv.shape, f32)
    for e, o_ref in zip(es, o_refs):
        o = o + _dot_hi(e / tot, e_mat) * o_ref[...].astype(f32)
    r["o"] = o
    r["ob"] = o.astype(bf16)
    r["y_a"] = _dot(r["ob"], watt_ref[...])
    hc = hc_ref[...]
    mu = jnp.mean(hc, axis=-1, keepdims=True)
    xc = hc - mu
    rstd = lax.rsqrt(jnp.mean(xc * xc, axis=-1, keepdims=True) + EPS)
    r["xh"], r["rstd"] = xc * rstd, rstd
    hn = r["xh"] * lng_ref[...] + lnb_ref[...]
    r["hn"] = hn
    r["sgn"] = _sigmoid(hn)
    r["hs"] = (hn * r["sgn"]).astype(bf16)
    r["y_c"] = _dot(r["hs"], wpw_ref[...])
    r["gates"] = [_sigmoid(g_refs[k][...].astype(f32) + bg_ref[:, k * D:(k + 1) * D]) for k in range(3)]
    r["merged"] = r["gates"][0] * r["y_s"] + r["gates"][1] * r["y_a"] + r["gates"][2] * r["y_c"]
    return r


TBM = 256


def _merge_in_specs(tok, tb, lses):
    w = lambda shape: pl.BlockSpec(shape, lambda i: (0, 0), pipeline_mode=pl.Buffered(1))
    nbl = lses[0].shape[2] // tb
    return ([pl.BlockSpec((tb, D), tok), pl.BlockSpec((tb, BW), tok)]
            + [pl.BlockSpec((tb, BW), tok)] * 3
            + [pl.BlockSpec((1, 2, tb, 128), lambda i: (i // nbl, 0, i % nbl, 0))] * 3
            + [pl.BlockSpec((tb, BW), tok)]
            + [pl.BlockSpec((tb, D), lambda i, k=k: (i, 4 + k)) for k in range(3)]
            + [w((1, 3 * D)), w((1, BW)), w((1, BW)), w((BW, 2 * D)), w((BW, D)), w((BW, D)), w((D, D))])


def merge_fwd(x, ys, os_, lses, hc, proj, bg, lng, lnb, wglu, watt, wpw, wout):
    n = x.shape[0]

    def body(x_ref, ys_ref, o1, o2, o3, l1, l2, l3, hc_ref, g0, g1, g2, bg_ref, lng_ref, lnb_ref,
             wglu_ref, watt_ref, wpw_ref, wout_ref, x1_ref):
        r = _merge_common(ys_ref, (o1, o2, o3), (l1, l2, l3), hc_ref, (g0, g1, g2), bg_ref, lng_ref, lnb_ref,
                          wglu_ref, watt_ref, wpw_ref)
        x1_ref[...] = x_ref[...] + _dot(r["merged"].astype(bf16), wout_ref[...])

    tok = lambda i: (i, 0)
    return pl.pallas_call(
        body, name="merge_fwd", grid=(n // TB,), in_specs=_merge_in_specs(tok, TB, lses),
        out_specs=pl.BlockSpec((TB, D), tok), out_shape=jax.ShapeDtypeStruct((n, D), f32),
        compiler_params=_cp(1, 56))(x, ys, *os_, *lses, hc, proj, proj, proj, bg, lng, lnb, wglu, watt, wpw, wout)


def merge_bwd(dx1, ys, os_, lses, hc, proj, bg, lng, lnb, wglu, watt, wpw, wout):
    n = dx1.shape[0]

    def body(dx_ref, ys_ref, o1, o2, o3, l1, l2, l3, hc_ref, g0, g1, g2, bg_ref, lng_ref, lnb_ref,
             wglu_ref, watt_ref, wpw_ref, wout_ref,
             dys_ref, do_ref, delta_ref, ltot_ref, dhc_ref, dgate_ref, ysin_ref, dz_ref, ob_ref, dya_ref,
             hs_ref, dyc_ref, mg_ref, dbg_ref, dlng_ref, dlnb_ref):
        i = pl.program_id(0)

        @pl.when(i == 0)
        def _():
            dbg_ref[...] = jnp.zeros_like(dbg_ref)
            dlng_ref[...] = jnp.zeros_like(dlng_ref)
            dlnb_ref[...] = jnp.zeros_like(dlnb_ref)

        r = _merge_common(ys_ref, (o1, o2, o3), (l1, l2, l3), hc_ref, (g0, g1, g2), bg_ref, lng_ref, lnb_ref,
                          wglu_ref, watt_ref, wpw_ref)
        mg_ref[...] = r["merged"].astype(bf16)
        ysin_ref[...] = r["ysin"]
        ob_ref[...] = r["ob"]
        hs_ref[...] = r["hs"]
        ltot_ref[...] = r["lse_tot"]
        dm = _dot_nt(dx_ref[...].astype(bf16), wout_ref[...])
        ys3 = (r["y_s"], r["y_a"], r["y_c"])
        for k in range(3):
            gk = r["gates"][k]
            dgr = dm * ys3[k] * gk * (1.0 - gk)
            dgate_ref[:, k * D:(k + 1) * D] = dgr.astype(bf16)
            dbg_ref[:, k * D:(k + 1) * D] += jnp.sum(dgr, axis=0, keepdims=True)
        dy_s = dm * r["gates"][0]
        sg2 = r["sg2"]
        dz = jnp.concatenate([dy_s * sg2, dy_s * r["z1"] * sg2 * (1.0 - sg2)], axis=1).astype(bf16)
        dz_ref[...] = dz
        dys_ref[...] = _dot_nt(dz, wglu_ref[...]) * _gelu_grad(r["ys"])
        dya = (dm * r["gates"][1]).astype(bf16)
        dya_ref[...] = dya
        do = _dot_nt(dya, watt_ref[...])
        do_ref[...] = do.astype(bf16)
        delta_ref[...] = _dot_hi(do * r["o"], _head_reduce())
        dyc = (dm * r["gates"][2]).astype(bf16)
        dyc_ref[...] = dyc
        sgn, hn = r["sgn"], r["hn"]
        dhn = _dot_nt(dyc, wpw_ref[...]) * sgn * (1.0 + hn * (1.0 - sgn))
        dlng_ref[...] += jnp.sum(dhn * r["xh"], axis=0, keepdims=True)
        dlnb_ref[...] += jnp.sum(dhn, axis=0, keepdims=True)
        dxh = dhn * lng_ref[...]
        xh = r["xh"]
        dhc_ref[...] = r["rstd"] * (dxh - jnp.mean(dxh, axis=-1, keepdims=True)
                                    - xh * jnp.mean(dxh * xh, axis=-1, keepdims=True))

    tok = lambda i: (i, 0)
    fix = lambda i: (0, 0)
    outs = [("dys", BW, f32), ("do", BW, bf16), ("delta", 128, f32), ("lse_tot", 128, f32), ("dhc", BW, f32),
            ("dgate", 3 * D, bf16), ("ysin", BW, bf16), ("dz", 2 * D, bf16), ("ob", BW, bf16), ("dya", D, bf16),
            ("hs", BW, bf16), ("dyc", D, bf16), ("merged", D, bf16)]
    small = [("dbg", 3 * D), ("dlng", BW), ("dlnb", BW)]
    res = pl.pallas_call(
        body, name="merge_bwd", grid=(n // TBM,), in_specs=_merge_in_specs(tok, TBM, lses),
        out_specs=[pl.BlockSpec((TBM, w), tok) for _, w, _ in outs] + [pl.BlockSpec((1, w), fix) for _, w in small],
        out_shape=[jax.ShapeDtypeStruct((n, w), dt) for _, w, dt in outs]
        + [jax.ShapeDtypeStruct((1, w), f32) for _, w in small],
        compiler_params=_cp(1, 56))(dx1, ys, *os_, *lses, hc, proj, proj, proj, bg, lng, lnb, wglu, watt, wpw, wout)
    return dict(zip([k for k, _, _ in outs] + [k for k, _ in small], res))


def assemble_dproj(du, dqs, dks, dvs, dcv, dgate):
    b, l, _ = du.shape
    nck = BW // 128

    def body(du_ref, q1, q2, q3, k1, k2, k3, v1, v2, v3, cv_ref, g_ref, o_ref):
        o_ref[0, :, 0:BW] = du_ref[0]
        for c in range(nck):
            for j, qr in enumerate((q1, q2, q3)):
                o_ref[0, :, (1 + j) * BW + c * 128:(1 + j) * BW + (c + 1) * 128] = qr[0, c]
            add3 = lambda r1, r2, r3: (r1[0, c].astype(f32) + r2[0, c].astype(f32) + r3[0, c].astype(f32)).astype(bf16)
            o_ref[0, :, 4 * BW + c * 128:4 * BW + (c + 1) * 128] = add3(k1, k2, k3)
            o_ref[0, :, 5 * BW + c * 128:5 * BW + (c + 1) * 128] = add3(v1, v2, v3)
        o_ref[0, :, 6 * BW:8 * BW] = cv_ref[0]
        o_ref[0, :, 8 * BW:] = g_ref[0]

    t = lambda w: pl.BlockSpec((1, TB, w), lambda bi, i: (bi, i, 0))
    ck = pl.BlockSpec((1, nck, TB, 128), lambda bi, i: (bi, 0, i, 0))
    return pl.pallas_call(
        body, name="assemble_dproj", grid=(b, l // TB),
        in_specs=[t(BW)] + [ck] * 9 + [t(2 * BW), t(3 * D)], out_specs=t(INC),
        out_shape=jax.ShapeDtypeStruct((b, l, INC), bf16), compiler_params=_cp(2))(du, *dqs, *dks, *dvs, dcv, dgate)


def _me():
    return lax.axis_index("x"), lax.axis_index("y"), lax.axis_index("c")


def _peers():
    x, y, c = _me()
    return [(x, y, 1 - c), (1 - x, y, c), (1 - x, y, 1 - c), (x, 1 - y, c), (x, 1 - y, 1 - c),
            (1 - x, 1 - y, c), (1 - x, 1 - y, 1 - c)]


def _rank(p):
    return 4 * p[0] + 2 * p[1] + p[2]


def allgather(arrs, name):
    na = len(arrs)
    units = [(a, j) for a in range(na) for j in range(arrs[a].shape[0])]
    nu = len(units)

    def body(*refs):
        ins, outs = refs[:na], refs[na:2 * na]
        send, recv, loc = refs[2 * na:]
        me = _rank(_me())
        local, remote = [], []
        for u, (a, j) in enumerate(units):
            own = pltpu.make_async_copy(ins[a].at[j], outs[a].at[j, me], loc.at[u])
            own.start()
            local.append(own)
        for u, (a, j) in enumerate(units):
            for k, p in enumerate(_peers()):
                cp = pltpu.make_async_remote_copy(src_ref=ins[a].at[j], dst_ref=outs[a].at[j, me],
                                                  send_sem=send.at[u, k], recv_sem=recv.at[u, k],
                                                  device_id=p, device_id_type=MESH)
                cp.start()
                remote.append(cp)
        for cp in local:
            cp.wait()
        for cp in remote:
            cp.wait()

    return pl.pallas_call(
        body, name=name, in_specs=[ANY] * na, out_specs=[ANY] * na,
        out_shape=[jax.ShapeDtypeStruct((a.shape[0], NDEV) + a.shape[1:], a.dtype) for a in arrs],
        scratch_shapes=[pltpu.SemaphoreType.DMA((nu, NDEV - 1)), pltpu.SemaphoreType.DMA((nu, NDEV - 1)),
                        pltpu.SemaphoreType.DMA((nu,))])(*arrs)


HBM = pl.BlockSpec(memory_space=pltpu.HBM)
SEM = pl.BlockSpec(memory_space=pltpu.SEMAPHORE)
_EFFECT = pltpu.SideEffectType.DATAFLOW_SIDE_EFFECTING


def _rank_slot(ref, r):
    if ref.shape[0] == NDEV:
        return ref.at[r]
    n = ref.shape[2] // 2
    return ref.at[r // 2, :, pl.ds(pl.multiple_of((r % 2) * n, 128), n)]


def _push_copies(srcs, lands, send, recv, scatter):
    me = _rank(_me())
    out = []
    for i in range(len(srcs)):
        for k, p in enumerate(_peers()):
            src = _rank_slot(srcs[i], _rank(p)) if scatter else srcs[i]
            dst = lands[i].at[k] if scatter else _rank_slot(lands[i], me)
            j = i * (NDEV - 1) + k
            out.append(pltpu.make_async_remote_copy(src_ref=src, dst_ref=dst, send_sem=send.at[j],
                                                    recv_sem=recv.at[j], device_id=p, device_id_type=MESH))
    return out


def push_start(srcs, lands, scatter, name, token):
    n = len(srcs)
    token = jnp.zeros((8, 128), f32) if token is None else token

    def body(*refs):
        for cp in _push_copies(refs[:n], refs[n:2 * n], refs[2 * n + 1], refs[2 * n + 2], scatter):
            cp.start()
        refs[-1][...] = refs[2 * n][...]

    sems = pltpu.SemaphoreType.DMA((n * (NDEV - 1),))
    vmem = pl.BlockSpec(memory_space=pltpu.VMEM)
    res = pl.pallas_call(
        body, name=name, in_specs=[HBM] * (2 * n) + [vmem], out_specs=[SEM, SEM] + [HBM] * (2 * n) + [vmem],
        out_shape=[sems, sems] + [pltpu.HBM(a.shape, a.dtype) for a in list(srcs) + list(lands)]
        + [jax.ShapeDtypeStruct((8, 128), f32)],
        input_output_aliases={i: 2 + i for i in range(2 * n)},
        compiler_params=pltpu.CompilerParams(has_side_effects=_EFFECT),
    )(*[pltpu.with_memory_space_constraint(a, pltpu.HBM) for a in list(srcs) + list(lands)], token)
    return res[0], res[1], res[2:2 + n], res[2 + n:2 + 2 * n], res[-1]


def push_wait(send, recv, srcs, lands, after, scatter, name):
    n = len(srcs)
    after = list(after) if isinstance(after, (list, tuple)) else [after]

    def body(*refs):
        for cp in _push_copies(refs[:n], refs[n:2 * n], refs[2 * n], refs[2 * n + 1], scatter):
            cp.wait_send()
            cp.wait_recv()

    res = pl.pallas_call(
        body, name=name, in_specs=[HBM] * (2 * n) + [SEM, SEM] + [ANY] * len(after), out_specs=[HBM] * (2 * n),
        out_shape=[pltpu.HBM(a.shape, a.dtype) for a in list(srcs) + list(lands)],
        input_output_aliases={i: i for i in range(2 * n)},
        compiler_params=pltpu.CompilerParams(has_side_effects=_EFFECT),
    )(*srcs, *lands, send, recv, *after)
    return res[:n], res[n:]


_C1 = 1.0 / (1.0 - ADAM_B1 ** ADAM_STEP)
_C2 = 1.0 / (1.0 - ADAM_B2 ** ADAM_STEP)


def _adamw(w, g, m, v):
    m = ADAM_B1 * m + (1.0 - ADAM_B1) * g
    v = ADAM_B2 * v + (1.0 - ADAM_B2) * (g * g)
    delta = -ADAM_LR * ((m * _C1) / (jnp.sqrt(v * _C2) + ADAM_EPS) + ADAM_WD * w)
    return delta, m, v


def adam_big(lands, owns, w, m, v, name):
    _, k, n = lands[0].shape
    tk = k
    while tk * n * 2 * NDEV > 2 * 1024 * 1024 and tk % 16 == 0:
        tk //= 2

    def body(*refs):
        l_refs, o_refs = refs[:DEPTH], refs[DEPTH:2 * DEPTH]
        w_ref, m_ref, v_ref, g_ref, d_ref, nm_ref, nv_ref = refs[2 * DEPTH:]
        for l in range(DEPTH):
            g = o_refs[l][...].astype(f32)
            for s in range(NDEV - 1):
                g = g + l_refs[l][s].astype(f32)
            d, nm, nv = _adamw(w_ref[l], g, m_ref[l], v_ref[l])
            g_ref[l], d_ref[l], nm_ref[l], nv_ref[l] = g, d, nm, nv

    blk = pl.BlockSpec((DEPTH, tk, n), lambda i: (0, i, 0))
    return pl.pallas_call(
        body, name=name, grid=(k // tk,),
        in_specs=[pl.BlockSpec((NDEV - 1, tk, n), lambda i: (0, i, 0))] * DEPTH
        + [pl.BlockSpec((tk, n), lambda i: (i, 0))] * DEPTH + [blk, blk, blk],
        out_specs=[blk] * 4, out_shape=[jax.ShapeDtypeStruct(w.shape, f32)] * 4,
        compiler_params=_cp(1))(*lands, *owns, w, m, v)


def adam_small(gath, w, m, v):
    r = w.shape[0]
    tr = 512

    def body(g_ref, w_ref, m_ref, v_ref, go_ref, d_ref, nm_ref, nv_ref):
        g = g_ref[0]
        for s in range(1, NDEV):
            g = g + g_ref[s]
        d, nm, nv = _adamw(w_ref[...], g, m_ref[...], v_ref[...])
        go_ref[...], d_ref[...], nm_ref[...], nv_ref[...] = g, d, nm, nv

    blk = pl.BlockSpec((tr, 128), lambda i: (i, 0))
    return pl.pallas_call(
        body, name="adam_small", grid=(r // tr,),
        in_specs=[pl.BlockSpec((NDEV, tr, 128), lambda i: (0, i, 0)), blk, blk, blk],
        out_specs=[blk] * 4, out_shape=[jax.ShapeDtypeStruct((r, 128), f32)] * 4,
        compiler_params=_cp(1))(gath, w, m, v)


SMALL = ["norm1_g", "b_gate", "ssm_lambda_re", "ssm_lambda_im", "ssm_log_dt", "ssm_b_re", "ssm_b_im",
         "ssm_c_re", "ssm_c_im", "ssm_d", "conv_w", "conv_b", "conv_ln_g", "conv_ln_b", "norm2_g", "final_g"]
BIG = ["w_in", "w_ssm_glu", "w_att_up", "w_conv_pw2", "w_out", "w_ffn_in", "w_ffn_out"]
ORDER = ["norm1_g", "w_in", "b_gate", "ssm_lambda_re", "ssm_lambda_im", "ssm_log_dt", "ssm_b_re", "ssm_b_im",
         "ssm_c_re", "ssm_c_im", "ssm_d", "w_ssm_glu", "w_att_up", "conv_w", "conv_b", "conv_ln_g", "conv_ln_b",
         "w_conv_pw2", "w_out", "norm2_g", "w_ffn_in", "w_ffn_out", "final_g"]
PACK_ROWS = 2560


def _pack(arrs):
    flat = jnp.concatenate([a.reshape(-1).astype(f32) for a in arrs])
    return jnp.pad(flat, (0, PACK_ROWS * 128 - flat.shape[0])).reshape(PACK_ROWS, 128)


def _unpack(pack, shapes):
    flat = pack.reshape(-1)
    out, off = [], 0
    for s in shapes:
        sz = math.prod(s)
        out.append(flat[off:off + sz].reshape(s))
        off += sz
    return out


def _bt(b):
    return b.transpose(2, 0, 1).reshape(GH, NSTATE)


def _bt_inv(bt):
    return bt.reshape(GH, NG, NS).transpose(1, 2, 0)


def _ct(c):
    return c.transpose(1, 0, 2).reshape(GH, NSTATE)


def _ct_inv(ct):
    return ct.reshape(GH, NG, NS).transpose(1, 0, 2)


def local_step(x, loss_target, P, weights, on_grads, start_token=None):
    bsz, seq, _ = x.shape
    n = bsz * seq

    def natural(g3):
        return g3.transpose(1, 0, 2).reshape(g3.shape[1], NDEV * g3.shape[2])

    tokens = [] if start_token is None else [start_token]

    def after_pushes(a):
        while tokens:
            a = a + tokens.pop()[0:1, 0:1]
        return a

    def pushed(tok):
        if tok is not None:
            tokens.append(tok)

    s5_in, s5_tabs = [], []
    for l in range(DEPTH):
        lr = P["ssm_lambda_re"][l].reshape(1, NSTATE)
        li = P["ssm_lambda_im"][l].reshape(1, NSTATE)
        ld = jnp.repeat(P["ssm_log_dt"][l], NS).reshape(1, NSTATE)
        btr, bti = _bt(P["ssm_b_re"][l]), _bt(P["ssm_b_im"][l])
        s5_in.append((lr, li, ld, btr, bti))
        s5_tabs.append(s5_params(lr, li, ld, btr, bti, _ct(P["ssm_c_re"][l]), _ct(P["ssm_c_im"][l])))

    xs = x.reshape(n, D)
    saved = []
    conv_w_pad = None
    for l in range(DEPTH):
        S = {"x": xs}
        h1 = rms_fwd(xs, after_pushes(P["norm1_g"][l][None]))
        G = dict(weights(l, "in", [h1] + [t for tabs in s5_tabs for t in tabs] if l == 0 else h1))
        if conv_w_pad is None:
            conv_w_full = G["conv_w"].transpose(1, 2, 0, 3).reshape(DEPTH, CW, BW)
            conv_w_pad = jnp.pad(conv_w_full, ((0, 0), (0, 1), (0, 0)))
        w_in4 = G["w_in"][None]
        proj = inproj(h1, w_in4, 0)
        proj3 = proj.reshape(bsz, seq, INC)
        lr, li, ld, btr, bti = s5_in[l]
        t8, bb, cb = s5_tabs[l]
        dskip = P["ssm_d"][l][None]
        ys, s_re, s_im = s5_fwd(proj3, t8, bb, cb, dskip)
        att = [att_fwd(proj3, gi, dil) for gi, (_, dil) in enumerate(PATTERNS)]
        hc = conv_fwd(proj3, conv_w_pad[l], P["conv_b"][l][None])
        G.update(weights(l, "mix", hc))
        wts = dict(wglu=natural(G["w_ssm_glu"]), watt=natural(G["w_att_up"]),
                   wpw=natural(G["w_conv_pw2"]), wout=G["w_out"].reshape(D, D))
        mi = dict(ys=ys.reshape(n, BW), os_=[a[0].reshape(n, BW) for a in att],
                  lses=[a[1] for a in att], hc=hc.reshape(n, BW),
                  proj=proj, bg=P["b_gate"][l][None], lng=P["conv_ln_g"][l][None], lnb=P["conv_ln_b"][l][None],
                  **wts)
        x1 = merge_fwd(xs, **mi)
        G.update(weights(l, "ffn", x1))
        w_ffn = (G["w_ffn_in"][None], G["w_ffn_out"][None])
        x2, z1s, z2s, h2 = ffn_fwd(x1, P["norm2_g"][l][None], *w_ffn, 0)
        S.update(h1=h1, proj=proj, proj3=proj3, tabs=(s_re, s_im, t8, bb, cb), mi=mi, x1=x1, w_in4=w_in4, w_ffn=w_ffn,
                 zs=(z1s, z2s), h2=h2,
                 sp=(lr, li, ld, btr, bti), dskip=dskip)
        saved.append(S)
        xs = x2

    loss8, dx, dfinal = loss_head(xs, P["final_g"][None], loss_target.reshape(n, D))

    small_g = {k: [None] * DEPTH for k in SMALL if k != "final_g"}
    tokblk = lambda w: pl.BlockSpec((1024, w), lambda s, i: (i, 0))
    colblk = lambda w: pl.BlockSpec((1024, w), lambda s, i: (i, s))
    for l in reversed(range(DEPTH)):
        S = saved[l]
        g2 = P["norm2_g"][l][None]
        dh4, dwa, dwb, dw2 = ffn_bwd(S["h2"], dx, *S["zs"], *S["w_ffn"], 0)
        dx1, dg2 = norm_bwd_fin(S["x1"], after_pushes(g2), dx, dh4, "ffn_bwd_fin")
        small_g["norm2_g"][l] = dg2
        pushed(on_grads(l, "ffn", dict(w_ffn_in=jnp.concatenate([dwa, dwb], axis=0),
                                       w_ffn_out=dw2.reshape(NDEV, NSH_FF // 2, D))))
        mb = merge_bwd(dx1, **dict(S["mi"], lng=after_pushes(S["mi"]["lng"])))
        small_g["b_gate"][l], small_g["conv_ln_g"][l], small_g["conv_ln_b"][l] = mb["dbg"], mb["dlng"], mb["dlnb"]
        dws = dw_mix(mb["ysin"], mb["dz"], mb["ob"], mb["dya"], mb["hs"], mb["dyc"], mb["merged"], dx1)
        pushed(on_grads(l, "mix", dict(zip(("w_ssm_glu", "w_att_up", "w_conv_pw2", "w_out"), dws))))
        dcv, dcw, dcb = conv_bwd(S["proj3"], mb["dhc"].reshape(bsz, seq, BW), after_pushes(conv_w_pad[l]))
        small_g["conv_w"][l] = dcw[:CW].reshape(CW, NDEV, BW // NDEV).transpose(1, 0, 2)
        small_g["conv_b"][l] = dcb
        ab = [att_bwd(S["proj3"], mb["do"].reshape(bsz, seq, BW), mb["lse_tot"].reshape(bsz, seq, 128),
                      mb["delta"].reshape(bsz, seq, 128), gi, dil) for gi, (_, dil) in enumerate(PATTERNS)]
        du, d_a, d_bb, d_cb, d_d = s5_bwd(S["proj3"], mb["dys"].reshape(bsz, seq, BW), *S["tabs"], S["dskip"])
        lr, li, ld, btr, bti = S["sp"]
        dlr, dli, dld, dbt, dct = s5_params_bwd(lr, li, ld, btr, bti, d_a, d_bb, d_cb)
        small_g["ssm_lambda_re"][l], small_g["ssm_lambda_im"][l] = dlr.reshape(NG, NS), dli.reshape(NG, NS)
        small_g["ssm_log_dt"][l] = dld[0, :NG]
        small_g["ssm_b_re"][l], small_g["ssm_b_im"][l] = _bt_inv(dbt[0]), _bt_inv(dbt[1])
        small_g["ssm_c_re"][l], small_g["ssm_c_im"][l] = _ct_inv(dct[0]), _ct_inv(dct[1])
        small_g["ssm_d"][l] = d_d
        dproj = assemble_dproj(du, [a[0] for a in ab], [a[1] for a in ab], [a[2] for a in ab],
                               dcv, mb["dgate"].reshape(bsz, seq, 3 * D)).reshape(n, INC)
        nblk, wblk = S["w_in4"].shape[1], S["w_in4"].shape[3]
        pushed(on_grads(l, "in", dict(w_in=mm_tn(S["h1"], dproj, tokblk(D), colblk(wblk), nblk, D, wblk, n, "dw_in"))))
        if l == 0:
            pushed(on_grads(l, "small", dict(small_g=small_g, loss8=loss8, dfinal=dfinal)))
        dx, dg1 = inproj_bwd(dproj, S["w_in4"], 0, S["x"], after_pushes(P["norm1_g"][l][None]), dx1)
        small_g["norm1_g"][l] = dg1
    return loss8, dx, dfinal, small_g


def kernel(x, norm1_g, w_in, b_gate, ssm_lambda_re, ssm_lambda_im, ssm_log_dt, ssm_b_re, ssm_b_im, ssm_c_re, ssm_c_im, ssm_d, w_ssm_glu, w_att_up, conv_w, conv_b, conv_ln_g, conv_ln_b, w_conv_pw2, w_out, norm2_g, w_ffn_in, w_ffn_out, final_g, loss_target, m_norm1_g, m_w_in, m_b_gate, m_ssm_lambda_re, m_ssm_lambda_im, m_ssm_log_dt, m_ssm_b_re, m_ssm_b_im, m_ssm_c_re, m_ssm_c_im, m_ssm_d, m_w_ssm_glu, m_w_att_up, m_conv_w, m_conv_b, m_conv_ln_g, m_conv_ln_b, m_w_conv_pw2, m_w_out, m_norm2_g, m_w_ffn_in, m_w_ffn_out, m_final_g, v_norm1_g, v_w_in, v_b_gate, v_ssm_lambda_re, v_ssm_lambda_im, v_ssm_log_dt, v_ssm_b_re, v_ssm_b_im, v_ssm_c_re, v_ssm_c_im, v_ssm_d, v_w_ssm_glu, v_w_att_up, v_conv_w, v_conv_b, v_conv_ln_g, v_conv_ln_b, v_w_conv_pw2, v_w_out, v_norm2_g, v_w_ffn_in, v_w_ffn_out, v_final_g):
    args = dict(locals())
    W = {k: args[k] for k in ORDER}
    M = {k: args["m_" + k] for k in ORDER}
    V = {k: args["v_" + k] for k in ORDER}
    bsz, seq, _ = x.shape
    n = bsz * seq
    me = 4 * lax.axis_index("x") + 2 * lax.axis_index("y") + lax.axis_index("c")

    groups = {"in": ["w_in"], "mix": ["w_ssm_glu", "w_att_up", "w_conv_pw2", "w_out"], "ffn": ["w_ffn_in", "w_ffn_out"]}
    wb = {k: W[k].astype(bf16) for k in BIG}

    def landing(shard, paired=False):
        if paired:
            k_, n_ = shard.shape
            return lax.dynamic_update_slice(lax.empty((NDEV // 2, k_, 2 * n_), shard.dtype), shard[None],
                                            (me // 2, 0, (me % 2) * n_))
        return lax.dynamic_update_index_in_dim(lax.empty((NDEV,) + shard.shape, shard.dtype), shard, me, 0)

    def own_part(by_rank):
        if by_rank.shape[0] == NDEV:
            return lax.dynamic_index_in_dim(by_rank, me, 0, keepdims=False)
        n_ = by_rank.shape[2] // 2
        return lax.dynamic_slice(by_rank, (me // 2, 0, (me % 2) * n_), (1, by_rank.shape[1], n_))[0]

    plan = [("gather_a", [("w_in", 0), ("conv_w", None)]),
            ("gather_b", [(k, 0) for k in groups["mix"] + groups["ffn"]]),
            ("gather_c", [(k, 1) for k in BIG])]
    pending, token = {}, None
    for name, items in plan:
        shards = [conv_w if l is None else wb[k][l] for k, l in items]
        lands = [landing(s, k == "w_in") for (k, _), s in zip(items, shards)]
        send, recv, s_thru, l_thru, token = push_start(shards, lands, False, name, token)
        pending[name] = (send, recv, s_thru, l_thru, items)
    gathered = {}

    names = [k for k in SMALL if k != "final_g"]
    shapes = [(DEPTH, NDEV, CW, BW // NDEV) if k == "conv_w" else W[k].shape for k in names] + [(D,), (1,)]

    def wpack(src):
        parts = [jnp.broadcast_to(src[k][:, None], shapes[i]) if k == "conv_w" else src[k] for i, k in enumerate(names)]
        return _pack(parts + [src["final_g"], jnp.ones((1,), f32)])

    packs = [wpack(W), wpack(M), wpack(V)]

    def weights(l, group, after):
        name = "gather_c" if l == 1 else ("gather_a" if group == "in" else "gather_b")
        if name in pending:
            send, recv, s_thru, l_thru, items = pending.pop(name)
            if name == "gather_a":
                after = (list(after) if isinstance(after, (list, tuple)) else [after]) + packs
            for item, arr in zip(items, push_wait(send, recv, s_thru, l_thru, after, False, name + "_wait")[1]):
                gathered[item] = arr
        res = {k: gathered[(k, l)] for k in groups[group]}
        if group == "in":
            res["conv_w"] = gathered[("conv_w", None)]
        return res

    big_g = {k: [None] * DEPTH for k in BIG}
    flights = []

    def start_exchange(items, name):
        parts = [big_g[k][l] for k, l in items]
        part = lambda p: p.shape[1:] if p.shape[0] == NDEV else (p.shape[1], p.shape[2] // 2)
        lands = [lax.empty((NDEV - 1,) + part(p), p.dtype) for p in parts]
        send, recv, s_thru, l_thru, tok = push_start(parts, lands, True, name, None)
        flights.append((send, recv, s_thru, l_thru, items, name))
        return tok

    small_flight = []

    def start_small(small_g, loss8, dfinal):
        sg_ = dict(small_g, norm1_g=[jnp.zeros((1, D), f32), small_g["norm1_g"][1]])
        gpack = _pack([jnp.stack([g.reshape(shapes[i][1:]) for g in sg_[k]]) for i, k in enumerate(names)]
                      + [dfinal, loss8[0, :1]])
        send, recv, s_thru, l_thru, tok = push_start([gpack], [landing(gpack)], False, "gather_small", None)
        small_flight.append((send, recv, s_thru, l_thru))
        return tok

    def on_grads(l, group, grads):
        if group == "small":
            return start_small(**grads)
        for k, g in grads.items():
            big_g[k][l] = g
        if l == 1 and group == "in":
            return start_exchange([(k, 1) for k in BIG], "exchange_l1")
        if l == 0:
            return start_exchange([(k, 0) for k in groups[group]], "exchange_l0_" + group)
        return None

    loss8, dx, dfinal, small_g = local_step(x, loss_target, W, weights, on_grads, token)

    landed, own = {}, {}
    for send, recv, s_thru, l_thru, items, name in flights:
        srcs, lands = push_wait(send, recv, s_thru, l_thru, dx, True, name + "_wait")
        for item, src, land in zip(items, srcs, lands):
            landed[item] = land
            own[item] = own_part(src)
    out = {}
    for k in BIG:
        items = [(k, l) for l in range(DEPTH)]
        out[k] = adam_big([landed[i] for i in items], [own[i] for i in items], W[k], M[k], V[k], "adam_" + k)

    send, recv, s_thru, l_thru = small_flight[0]
    gall = push_wait(send, recv, s_thru, l_thru, out[BIG[-1]][0], False, "gather_small_wait")[1][0]
    (late,) = allgather([small_g["norm1_g"][0].reshape(1, D // 128, 128)], "allgather_late")
    gall = lax.dynamic_update_slice(gall, late[0], (0, 0, 0))
    sg, sd, sm, sv = [_unpack(p, shapes) for p in adam_small(gall, *packs)]
    for i, k in enumerate(names + ["final_g"]):
        vals = [t[i] for t in (sg, sd, sm, sv)]
        if k == "conv_w":
            vals = [lax.dynamic_index_in_dim(t, me, axis=1, keepdims=False) for t in vals]
        out[k] = vals
    loss = sg[-1].reshape(())

    res = [loss, dx.reshape(bsz, seq, D)]
    for j in range(4):
        res += [out[k][j] for k in ORDER]
    return tuple(res)
```

```python
import functools
import math

import jax
import jax.numpy as jnp
from jax import lax
from jax.experimental import pallas as pl
from jax.experimental.pallas import tpu as pltpu

f32 = jnp.float32
bf16 = jnp.bfloat16

D = 1024
DEPTH = 2
EPS = 1e-6
BW = 512
NG = 32
GH = 16
NS = 64
NSTATE = NG * NS
HD = 64
NH = 8
PATTERNS = ((128, 1), (512, 4), (2048, 16))
ABLK = 128
ATT_SCALE = HD ** -0.5
CW = 31
DFF = 2816
INC = 7168
NDEV = 8
NSH_IN = INC // NDEV
NSH_FF = 2 * DFF // NDEV
ADAM_LR, ADAM_B1, ADAM_B2, ADAM_EPS, ADAM_WD, ADAM_STEP = 0.001, 0.9, 0.999, 1e-08, 0.01, 10

TB = 512
SJ = 4
SW = NSTATE // SJ
SU = BW // SJ
NEG = -1e30
MESH = pl.DeviceIdType.MESH
ANY = pl.BlockSpec(memory_space=pl.ANY)


def _cp(n_axes, vmem_mb=48):
    return pltpu.CompilerParams(dimension_semantics=("arbitrary",) * n_axes,
                                vmem_limit_bytes=vmem_mb * 1024 * 1024)


def _dot(a, b):
    return jnp.dot(a, b, preferred_element_type=f32)


def _dot_nt(a, b):
    return lax.dot_general(a, b, (((1,), (1,)), ((), ())), preferred_element_type=f32)


def _dot_tn(a, b):
    return lax.dot_general(a, b, (((0,), (0,)), ((), ())), preferred_element_type=f32)


def _dot_hi(a, b):
    return jnp.dot(a, b, precision=lax.Precision.HIGHEST, preferred_element_type=f32)


def _sigmoid(x):
    return 1.0 / (1.0 + jnp.exp(-x))


_GC = math.sqrt(2.0 / math.pi)


def _gelu(x):
    return 0.5 * x * (1.0 + jnp.tanh(_GC * (x + 0.044715 * x * x * x)))


def _gelu_grad(x):
    t = jnp.tanh(_GC * (x + 0.044715 * x * x * x))
    return 0.5 * (1.0 + t) + 0.5 * x * (1.0 - t * t) * _GC * (1.0 + 3.0 * 0.044715 * x * x)


def _rms_stats(x):
    return lax.rsqrt(jnp.mean(x * x, axis=-1, keepdims=True) + EPS)


def _rms_bwd(x, g, dh):
    r = _rms_stats(x)
    dyg = dh * g
    dx = r * dyg - x * (r * r * r) * jnp.mean(dyg * x, axis=-1, keepdims=True)
    dg = jnp.sum(dh * x * r, axis=0, keepdims=True)
    return dx, dg


def rms_fwd(x, g):
    n = x.shape[0]

    def body(x_ref, g_ref, h_ref):
        xv = x_ref[...]
        h_ref[...] = (xv * _rms_stats(xv) * g_ref[...]).astype(bf16)

    return pl.pallas_call(
        body, name="rms_fwd", grid=(n // TB,),
        in_specs=[pl.BlockSpec((TB, D), lambda i: (i, 0)), pl.BlockSpec((1, D), lambda i: (0, 0))],
        out_specs=pl.BlockSpec((TB, D), lambda i: (i, 0)),
        out_shape=jax.ShapeDtypeStruct((n, D), bf16), compiler_params=_cp(1))(x, g)


def inproj(h, w4, layer):
    n = h.shape[0]
    tm = 1024
    nblk, wblk = w4.shape[1], w4.shape[3]

    def body(h_ref, w_ref, o_ref):
        o_ref[...] = _dot(h_ref[...], w_ref[0, 0]).astype(bf16)

    return pl.pallas_call(
        body, name="inproj", grid=(nblk, n // tm),
        in_specs=[pl.BlockSpec((tm, D), lambda s, i: (i, 0)),
                  pl.BlockSpec((1, 1, D, wblk), lambda s, i: (layer, s, 0, 0))],
        out_specs=pl.BlockSpec((tm, wblk), lambda s, i: (i, s)),
        out_shape=jax.ShapeDtypeStruct((n, INC), bf16), compiler_params=_cp(2))(h, w4)


def inproj_bwd(dproj, w4, layer, x, g, dres):
    n = x.shape[0]
    tm = 1024
    nblk, wblk = w4.shape[1], w4.shape[3]

    def body(dp_ref, w_ref, x_ref, g_ref, dr_ref, dx_ref, dg_ref, acc):
        i, s = pl.program_id(0), pl.program_id(1)

        @pl.when(s == 0)
        def _():
            acc[...] = jnp.zeros_like(acc)

        @pl.when((s == 0) & (i == 0))
        def _():
            dg_ref[...] = jnp.zeros_like(dg_ref)

        acc[...] += _dot_nt(dp_ref[...], w_ref[0, 0])

        @pl.when(s == nblk - 1)
        def _():
            dx, dg = _rms_bwd(x_ref[...], g_ref[...], acc[...])
            dx_ref[...] = dr_ref[...] + dx
            dg_ref[...] += dg

    return pl.pallas_call(
        body, name="inproj_bwd", grid=(n // tm, nblk),
        in_specs=[pl.BlockSpec((tm, wblk), lambda i, s: (i, s)),
                  pl.BlockSpec((1, 1, D, wblk), lambda i, s: (layer, s, 0, 0)),
                  pl.BlockSpec((tm, D), lambda i, s: (i, 0)),
                  pl.BlockSpec((1, D), lambda i, s: (0, 0)),
                  pl.BlockSpec((tm, D), lambda i, s: (i, 0))],
        out_specs=[pl.BlockSpec((tm, D), lambda i, s: (i, 0)), pl.BlockSpec((1, D), lambda i, s: (0, 0))],
        out_shape=[jax.ShapeDtypeStruct((n, D), f32), jax.ShapeDtypeStruct((1, D), f32)],
        scratch_shapes=[pltpu.VMEM((tm, D), f32)], compiler_params=_cp(2))(dproj, w4, x, g, dres)


def mm_tn(a, b, a_spec, b_spec, n_sh, ka, nb, m, name):
    tm = 1024

    def body(a_ref, b_ref, o_ref, acc):
        i = pl.program_id(1)

        @pl.when(i == 0)
        def _():
            acc[...] = jnp.zeros_like(acc)

        av = a_ref[...].reshape(tm, ka).astype(bf16)
        bv = b_ref[...].reshape(tm, nb).astype(bf16)
        acc[...] += _dot_tn(av, bv)

        @pl.when(i == m // tm - 1)
        def _():
            o_ref[0] = acc[...].astype(bf16)

    return pl.pallas_call(
        body, name=name, grid=(n_sh, m // tm), in_specs=[a_spec, b_spec],
        out_specs=pl.BlockSpec((1, ka, nb), lambda s, i: (s, 0, 0)),
        out_shape=jax.ShapeDtypeStruct((n_sh, ka, nb), bf16),
        scratch_shapes=[pltpu.VMEM((ka, nb), f32)], compiler_params=_cp(2))(a, b)


def dw_mix(ysin, dz, ob, dya, hs, dyc, merged, dx1):
    n = ysin.shape[0]
    tm = 512
    pairs = ((BW, 2 * D), (BW, D), (BW, D), (D, D))

    def body(a0, b0, a1, b1, a2, b2, a3, b3, o0, o1, o2, o3, c0, c1, c2, c3):
        i = pl.program_id(0)
        accs = (c0, c1, c2, c3)

        @pl.when(i == 0)
        def _():
            for c in accs:
                c[...] = jnp.zeros_like(c)

        for a, b_, c in zip((a0, a1, a2, a3), (b0, b1, b2, b3), accs):
            c[...] += _dot_tn(a[...], b_[...].astype(bf16))

        @pl.when(i == n // tm - 1)
        def _():
            for s in range(NDEV):
                o0[s] = c0[:, s * 256:(s + 1) * 256].astype(bf16)
                o1[s] = c1[:, s * 128:(s + 1) * 128].astype(bf16)
                o2[s] = c2[:, s * 128:(s + 1) * 128].astype(bf16)
                o3[s] = c3[s * 128:(s + 1) * 128, :].astype(bf16)

    tok = lambda w: pl.BlockSpec((tm, w), lambda i: (i, 0))
    whole = lambda shape: pl.BlockSpec(shape, lambda i: (0, 0, 0))
    outs = [(NDEV, BW, 256), (NDEV, BW, 128), (NDEV, BW, 128), (NDEV, D // NDEV, D)]
    return pl.pallas_call(
        body, name="dw_mix", grid=(n // tm,),
        in_specs=[tok(w) for pair in pairs for w in pair],
        out_specs=[whole(s) for s in outs], out_shape=[jax.ShapeDtypeStruct(s, bf16) for s in outs],
        scratch_shapes=[pltpu.VMEM(p, f32) for p in pairs],
        compiler_params=_cp(1, 56))(ysin, dz, ob, dya, hs, dyc, merged, dx1)


def ffn_fwd(x1, g2, w1, w2, layer):
    n = x1.shape[0]
    w2p = w2.reshape(w2.shape[0], 4, NSH_FF, D)

    def body(x_ref, g_ref, wa_ref, wb_ref, w2_ref, o_ref, z1_ref, z2_ref, h_sc):
        s = pl.program_id(1)

        @pl.when(s == 0)
        def _():
            xv = x_ref[...]
            h_sc[...] = (xv * _rms_stats(xv) * g_ref[...]).astype(bf16)
            o_ref[...] = xv

        h = h_sc[...]
        z1 = _dot(h, wa_ref[0, 0])
        z2 = _dot(h, wb_ref[0, 0])
        z1_ref[0] = z1.astype(bf16)
        z2_ref[0] = z2.astype(bf16)
        a = (z1 * _sigmoid(z1) * z2).astype(bf16)
        o_ref[...] += _dot(a, w2_ref[0, 0])

    tb = 2 * TB
    sh3 = pl.BlockSpec((1, tb, NSH_FF), lambda i, s: (s, i, 0))
    return pl.pallas_call(
        body, name="ffn_fwd", grid=(n // tb, 4),
        in_specs=[pl.BlockSpec((tb, D), lambda i, s: (i, 0)),
                  pl.BlockSpec((1, D), lambda i, s: (0, 0)),
                  pl.BlockSpec((1, 1, D, NSH_FF), lambda i, s: (layer, s, 0, 0)),
                  pl.BlockSpec((1, 1, D, NSH_FF), lambda i, s: (layer, s + 4, 0, 0)),
                  pl.BlockSpec((1, 1, NSH_FF, D), lambda i, s: (layer, s, 0, 0))],
        out_specs=[pl.BlockSpec((tb, D), lambda i, s: (i, 0)), sh3, sh3, pl.BlockSpec((tb, D), lambda i, s: (i, 0))],
        out_shape=[jax.ShapeDtypeStruct((n, D), f32), jax.ShapeDtypeStruct((4, n, NSH_FF), bf16),
                   jax.ShapeDtypeStruct((4, n, NSH_FF), bf16), jax.ShapeDtypeStruct((n, D), bf16)],
        compiler_params=_cp(2))(x1, g2, w1, w1, w2p)


def ffn_bwd(h, dx2, z1s, z2s, w1, w2, layer):
    n = h.shape[0]
    w2p = w2.reshape(w2.shape[0], 4, NSH_FF, D)
    nblk = n // TB

    def body(h_ref, dy_ref, z1_ref, z2_ref, wa_ref, wb_ref, w2_ref, dh_ref, dwa_ref, dwb_ref, dw2_ref,
             acc_a, acc_b, acc_2):
        i = pl.program_id(1)

        @pl.when(i == 0)
        def _():
            acc_a[...] = jnp.zeros_like(acc_a)
            acc_b[...] = jnp.zeros_like(acc_b)
            acc_2[...] = jnp.zeros_like(acc_2)

        hv = h_ref[...]
        dyb = dy_ref[...].astype(bf16)
        z1 = z1_ref[0].astype(f32)
        z2 = z2_ref[0].astype(f32)
        sg = _sigmoid(z1)
        sl = z1 * sg
        a = (sl * z2).astype(bf16)
        da = _dot_nt(dyb, w2_ref[0, 0])
        dz2 = (da * sl).astype(bf16)
        dz1 = (da * z2 * sg * (1.0 + z1 * (1.0 - sg))).astype(bf16)
        dh_ref[0] = (_dot_nt(dz1, wa_ref[0, 0]) + _dot_nt(dz2, wb_ref[0, 0])).astype(bf16)
        acc_a[...] += _dot_tn(hv, dz1)
        acc_b[...] += _dot_tn(hv, dz2)
        acc_2[...] += _dot_tn(a, dyb)

        @pl.when(i == nblk - 1)
        def _():
            dwa_ref[0] = acc_a[...].astype(bf16)
            dwb_ref[0] = acc_b[...].astype(bf16)
            dw2_ref[0] = acc_2[...].astype(bf16)

    tok = pl.BlockSpec((TB, D), lambda s, i: (i, 0))
    sh3 = pl.BlockSpec((1, TB, NSH_FF), lambda s, i: (s, i, 0))
    return pl.pallas_call(
        body, name="ffn_bwd", grid=(4, nblk),
        in_specs=[tok, tok, sh3, sh3,
                  pl.BlockSpec((1, 1, D, NSH_FF), lambda s, i: (layer, s, 0, 0)),
                  pl.BlockSpec((1, 1, D, NSH_FF), lambda s, i: (layer, s + 4, 0, 0)),
                  pl.BlockSpec((1, 1, NSH_FF, D), lambda s, i: (layer, s, 0, 0))],
        out_specs=[pl.BlockSpec((1, TB, D), lambda s, i: (s, i, 0)),
                   pl.BlockSpec((1, D, NSH_FF), lambda s, i: (s, 0, 0)),
                   pl.BlockSpec((1, D, NSH_FF), lambda s, i: (s, 0, 0)),
                   pl.BlockSpec((1, NSH_FF, D), lambda s, i: (s, 0, 0))],
        out_shape=[jax.ShapeDtypeStruct((4, n, D), bf16), jax.ShapeDtypeStruct((4, D, NSH_FF), bf16),
                   jax.ShapeDtypeStruct((4, D, NSH_FF), bf16), jax.ShapeDtypeStruct((4, NSH_FF, D), bf16)],
        scratch_shapes=[pltpu.VMEM((D, NSH_FF), f32), pltpu.VMEM((D, NSH_FF), f32), pltpu.VMEM((NSH_FF, D), f32)],
        compiler_params=_cp(2, 56))(h, dx2, z1s, z2s, w1, w1, w2p)


def norm_bwd_fin(x, g, dres, dh_parts, name):
    n = x.shape[0]
    nparts = dh_parts.shape[0]

    def body(x_ref, g_ref, dy_ref, dh_ref, dx_ref, dg_ref):
        i = pl.program_id(0)

        @pl.when(i == 0)
        def _():
            dg_ref[...] = jnp.zeros_like(dg_ref)

        dh = dh_ref[0].astype(f32)
        for s in range(1, nparts):
            dh = dh + dh_ref[s].astype(f32)
        dx, dg = _rms_bwd(x_ref[...], g_ref[...], dh)
        dx_ref[...] = dy_ref[...] + dx
        dg_ref[...] += dg

    tok = pl.BlockSpec((TB, D), lambda i: (i, 0))
    return pl.pallas_call(
        body, name=name, grid=(n // TB,),
        in_specs=[tok, pl.BlockSpec((1, D), lambda i: (0, 0)), tok, pl.BlockSpec((nparts, TB, D), lambda i: (0, i, 0))],
        out_specs=[tok, pl.BlockSpec((1, D), lambda i: (0, 0))],
        out_shape=[jax.ShapeDtypeStruct((n, D), f32), jax.ShapeDtypeStruct((1, D), f32)],
        compiler_params=_cp(1))(x, g, dres, dh_parts)


def loss_head(x, g, target):
    n = x.shape[0]

    def body(x_ref, g_ref, t_ref, l_ref, dx_ref, dg_ref):
        i = pl.program_id(0)

        @pl.when(i == 0)
        def _():
            l_ref[...] = jnp.zeros_like(l_ref)
            dg_ref[...] = jnp.zeros_like(dg_ref)

        xv = x_ref[...]
        y = xv * _rms_stats(xv) * g_ref[...]
        e = y - t_ref[...]
        l_ref[...] += 0.5 * jnp.sum(jnp.sum(e * e, axis=-1, keepdims=True), axis=0, keepdims=True) * (1.0 / D)
        dx, dg = _rms_bwd(xv, g_ref[...], e * (1.0 / D))
        dx_ref[...] = dx
        dg_ref[...] += dg

    tok = lambda i: (i, 0)
    return pl.pallas_call(
        body, name="loss_head", grid=(n // TB,),
        in_specs=[pl.BlockSpec((TB, D), tok), pl.BlockSpec((1, D), lambda i: (0, 0)), pl.BlockSpec((TB, D), tok)],
        out_specs=[pl.BlockSpec((8, 128), lambda i: (0, 0)), pl.BlockSpec((TB, D), tok),
                   pl.BlockSpec((1, D), lambda i: (0, 0))],
        out_shape=[jax.ShapeDtypeStruct((8, 128), f32), jax.ShapeDtypeStruct((n, D), f32),
                   jax.ShapeDtypeStruct((1, D), f32)],
        compiler_params=_cp(1))(x, g, target)


def _disc(lr, li, ld):
    dt = jnp.exp(ld)
    mag = jnp.exp(lr * dt)
    ar = mag * jnp.cos(li * dt)
    ai = mag * jnp.sin(li * dt)
    nr, ni = ar - 1.0, ai
    den = lr * lr + li * li
    zr = (nr * lr + ni * li) / den
    zi = (ni * lr - nr * li) / den
    return ar, ai, zr, zi


def _blockdiag_mask(shape):
    r = lax.broadcasted_iota(jnp.int32, shape, 0) // GH
    c = lax.broadcasted_iota(jnp.int32, shape, 1) // NS
    return r == c


def s5_params(lr, li, ld, btr, bti, ctr, cti):
    def body(lr_ref, li_ref, ld_ref, btr_ref, bti_ref, ctr_ref, cti_ref, t8_ref, bb_ref, cb_ref):
        ar, ai, zr, zi = _disc(lr_ref[...], li_ref[...], ld_ref[...])
        pr_, pi_ = ar, ai
        pw2 = []
        for k in range(4):
            pw2.append((pr_, pi_))
            pr_, pi_ = pr_ * pr_ - pi_ * pi_, 2.0 * pr_ * pi_
        cm = lambda p, q: (p[0] * q[0] - p[1] * q[1], p[0] * q[1] + p[1] * q[0])
        pw = {1: pw2[0], 2: pw2[1], 4: pw2[2], 8: pw2[3]}
        pw[3], pw[5], pw[6] = cm(pw[2], pw[1]), cm(pw[4], pw[1]), cm(pw[4], pw[2])
        pw[7] = cm(pw[4], pw[3])
        row = lax.broadcasted_iota(jnp.int32, (8, NSTATE), 0)
        zero = jnp.zeros((8, NSTATE), f32)
        for c in range(2):
            for k in range(3):
                full = jnp.broadcast_to(pw2[k][c], (8, NSTATE))
                t8_ref[c, k] = jnp.where(row >= (1 << k), full, 0.0)
                t8_ref[c, 3 + k] = jnp.where(row + (1 << k) < 8, full, 0.0)
            up, down = zero, zero
            for j in range(8):
                up = up + jnp.where(row == j, pw[j + 1][c], 0.0)
                down = down + jnp.where(row == j, pw[8 - j][c], 0.0)
            t8_ref[c, 6] = up
            t8_ref[c, 7] = down
        bbr = zr * btr_ref[...] - zi * bti_ref[...]
        bbi = zr * bti_ref[...] + zi * btr_ref[...]
        mask = _blockdiag_mask((SU, SW))
        for j in range(SJ):
            cols = slice(j * SW, (j + 1) * SW)
            for c, (vb, vc) in enumerate(((bbr, ctr_ref[...]), (bbi, cti_ref[...]))):
                bb_ref[c, j] = jnp.where(mask, jnp.tile(vb[:, cols], (SU // GH, 1)), 0.0).astype(bf16)
                cb_ref[c, j] = jnp.where(mask, jnp.tile(vc[:, cols], (SU // GH, 1)), 0.0).astype(bf16)

    return pl.pallas_call(
        body, name="s5_params",
        out_shape=[jax.ShapeDtypeStruct((2, 8, 8, NSTATE), f32),
                   jax.ShapeDtypeStruct((2, SJ, SU, SW), bf16), jax.ShapeDtypeStruct((2, SJ, SU, SW), bf16)],
        compiler_params=pltpu.CompilerParams(vmem_limit_bytes=56 * 1024 * 1024))(lr, li, ld, btr, bti, ctr, cti)


def s5_params_bwd(lr, li, ld, btr, bti, d_a, d_bb, d_cb):
    def body(lr_ref, li_ref, ld_ref, btr_ref, bti_ref, da_ref, dbb_ref, dcb_ref,
             dlr_ref, dli_ref, dld_ref, dbt_ref, dct_ref):
        mask = _blockdiag_mask((SU, SW))

        def fold(ref, c):
            parts = []
            for j in range(SJ):
                v = jnp.where(mask, ref[c, j], 0.0)
                parts.append(v.reshape(SU // GH, GH, SW).sum(axis=0))
            return jnp.concatenate(parts, axis=1)

        dct_ref[0] = fold(dcb_ref, 0)
        dct_ref[1] = fold(dcb_ref, 1)
        dbbr, dbbi = fold(dbb_ref, 0), fold(dbb_ref, 1)
        lrv, liv, ldv = lr_ref[...], li_ref[...], ld_ref[...]
        (ar, ai, zr, zi), vjp = jax.vjp(_disc, lrv, liv, ldv)
        btr, bti = btr_ref[...], bti_ref[...]
        dbt_ref[0] = zr * dbbr + zi * dbbi
        dbt_ref[1] = zr * dbbi - zi * dbbr
        dzr = jnp.sum(dbbr * btr + dbbi * bti, axis=0, keepdims=True)
        dzi = jnp.sum(dbbi * btr - dbbr * bti, axis=0, keepdims=True)
        dlr, dli, dld = vjp((da_ref[0:1, :], da_ref[1:2, :], dzr, dzi))
        dlr_ref[...] = dlr
        dli_ref[...] = dli
        ind = (lax.broadcasted_iota(jnp.int32, (NSTATE, 128), 0) // NS
               == lax.broadcasted_iota(jnp.int32, (NSTATE, 128), 1)).astype(f32)
        dld_ref[...] = _dot_hi(jnp.broadcast_to(dld, (8, NSTATE)), ind)

    return pl.pallas_call(
        body, name="s5_params_bwd",
        out_shape=[jax.ShapeDtypeStruct((1, NSTATE), f32), jax.ShapeDtypeStruct((1, NSTATE), f32),
                   jax.ShapeDtypeStruct((8, 128), f32), jax.ShapeDtypeStruct((2, GH, NSTATE), f32),
                   jax.ShapeDtypeStruct((2, GH, NSTATE), f32)],
        compiler_params=pltpu.CompilerParams(vmem_limit_bytes=56 * 1024 * 1024))(lr, li, ld, btr, bti, d_a, d_bb, d_cb)


def _fma(sr, si, ar, ai, qr, qi):
    return sr + ar * qr - ai * qi, si + ar * qi + ai * qr


def _scan_tile(sr, si, cr, ci, t8_ref, reverse):
    sg = -1.0 if reverse else 1.0
    for k in range(3):
        tk = 3 + k if reverse else k
        rot = 8 - (1 << k) if reverse else 1 << k
        sr, si = _fma(sr, si, t8_ref[0, tk], sg * t8_ref[1, tk], pltpu.roll(sr, rot, 0), pltpu.roll(si, rot, 0))
    tp = 7 if reverse else 6
    sr, si = _fma(sr, si, t8_ref[0, tp], sg * t8_ref[1, tp], cr, ci)
    e = 0 if reverse else 7
    return sr, si, jnp.broadcast_to(sr[e:e + 1, :], sr.shape), jnp.broadcast_to(si[e:e + 1, :], si.shape)


S5MC = 512


def _s5_input_map(u_ref, bb_ref, sr_sc, si_sc, l):
    for c in range(l // S5MC):
        rows = slice(c * S5MC, (c + 1) * S5MC)
        u = u_ref[0, rows, :]
        sr_sc[rows, :] = _dot(u, bb_ref[0, 0])
        si_sc[rows, :] = _dot(u, bb_ref[1, 0])


def _s5_forward_scan(sr_sc, si_sc, t8_ref, l):
    def step(k, carry):
        rows = pl.ds(pl.multiple_of(k * 8, 8), 8)
        sr, si, cr, ci = _scan_tile(sr_sc[rows, :], si_sc[rows, :], carry[0], carry[1], t8_ref, False)
        sr_sc[rows, :] = sr
        si_sc[rows, :] = si
        return cr, ci

    zero = jnp.zeros((8, SW), f32)
    lax.fori_loop(0, l // 8, step, (zero, zero), unroll=4)


def s5_fwd(proj3, t8, bb, cb, dskip):
    b, l, _ = proj3.shape

    def body(u_ref, t8_ref, bb_ref, cb_ref, d_ref, y_ref, sr_out, si_out):
        sr_sc, si_sc = sr_out.at[0], si_out.at[0]
        _s5_input_map(u_ref, bb_ref, sr_sc, si_sc, l)
        _s5_forward_scan(sr_sc, si_sc, t8_ref, l)
        for c in range(l // S5MC):
            rows = slice(c * S5MC, (c + 1) * S5MC)
            y = (_dot_nt(sr_sc[rows, :].astype(bf16), cb_ref[0, 0])
                 - _dot_nt(si_sc[rows, :].astype(bf16), cb_ref[1, 0]))
            y_ref[0, rows, :] = y + d_ref[...] * u_ref[0, rows, :].astype(f32)

    return pl.pallas_call(
        body, name="s5_fwd", grid=(SJ, b),
        in_specs=[pl.BlockSpec((1, l, SU), lambda j, bi: (bi, 0, j)),
                  pl.BlockSpec((2, 8, 8, SW), lambda j, bi: (0, 0, 0, j)),
                  pl.BlockSpec((2, 1, SU, SW), lambda j, bi: (0, j, 0, 0)),
                  pl.BlockSpec((2, 1, SU, SW), lambda j, bi: (0, j, 0, 0)),
                  pl.BlockSpec((1, SU), lambda j, bi: (0, j))],
        out_specs=[pl.BlockSpec((1, l, SU), lambda j, bi: (bi, 0, j)),
                   pl.BlockSpec((1, l, SW), lambda j, bi: (bi, 0, j)),
                   pl.BlockSpec((1, l, SW), lambda j, bi: (bi, 0, j))],
        out_shape=[jax.ShapeDtypeStruct((b, l, BW), f32), jax.ShapeDtypeStruct((b, l, NSTATE), f32),
                   jax.ShapeDtypeStruct((b, l, NSTATE), f32)],
        compiler_params=_cp(2))(proj3, t8, bb, cb, dskip)


def s5_bwd(proj3, dy, s_re, s_im, t8, bb, cb, dskip):
    b, l, _ = proj3.shape
    nt = l // 8

    def body(u_ref, dy_ref, sr_in, si_in, t8_ref, bb_ref, cb_ref, d_ref,
             du_ref, da_ref, dbb_ref, dcb_ref, dd_ref, gr_sc, gi_sc):
        bi = pl.program_id(1)
        sr_sc, si_sc = sr_in.at[0], si_in.at[0]

        @pl.when(bi == 0)
        def _():
            da_ref[...] = jnp.zeros_like(da_ref)
            dbb_ref[...] = jnp.zeros_like(dbb_ref)
            dcb_ref[...] = jnp.zeros_like(dcb_ref)
            dd_ref[...] = jnp.zeros_like(dd_ref)

        for c in range(l // S5MC):
            rows = slice(c * S5MC, (c + 1) * S5MC)
            dyb = dy_ref[0, rows, :].astype(bf16)
            gr_sc[rows, :] = _dot(dyb, cb_ref[0, 0])
            gi_sc[rows, :] = -_dot(dyb, cb_ref[1, 0])

        row = lax.broadcasted_iota(jnp.int32, (8, SW), 0)

        def step(i, carry):
            cr, ci, dar, dai = carry
            k = nt - 1 - i
            rows = pl.ds(pl.multiple_of(k * 8, 8), 8)
            gr, gi, cr, ci = _scan_tile(gr_sc[rows, :], gi_sc[rows, :], cr, ci, t8_ref, True)
            gr_sc[rows, :] = gr
            gi_sc[rows, :] = gi
            before = pl.ds(pl.multiple_of(jnp.maximum(k - 1, 0) * 8, 8), 8)
            live = jnp.where(k > 0, 1.0, 0.0)
            sr, si = sr_sc[rows, :], si_sc[rows, :]
            spr = jnp.where(row == 0, live * sr_sc[before, :][7:8, :], pltpu.roll(sr, 1, 0))
            spi = jnp.where(row == 0, live * si_sc[before, :][7:8, :], pltpu.roll(si, 1, 0))
            return cr, ci, dar + spr * gr + spi * gi, dai + spr * gi - spi * gr

        zero = jnp.zeros((8, SW), f32)
        _, _, dar, dai = lax.fori_loop(0, nt, step, (zero, zero, zero, zero), unroll=2)
        da_ref[0:1, :] += jnp.sum(dar, axis=0, keepdims=True)
        da_ref[1:2, :] += jnp.sum(dai, axis=0, keepdims=True)

        for c in range(l // S5MC):
            rows = slice(c * S5MC, (c + 1) * S5MC)
            u = u_ref[0, rows, :]
            dyv = dy_ref[0, rows, :]
            dyb = dyv.astype(bf16)
            grb, gib = gr_sc[rows, :].astype(bf16), gi_sc[rows, :].astype(bf16)
            dcb_ref[0, 0] += _dot_tn(dyb, sr_sc[rows, :].astype(bf16))
            dcb_ref[1, 0] -= _dot_tn(dyb, si_sc[rows, :].astype(bf16))
            dbb_ref[0, 0] += _dot_tn(u, grb)
            dbb_ref[1, 0] += _dot_tn(u, gib)
            du = _dot_nt(grb, bb_ref[0, 0]) + _dot_nt(gib, bb_ref[1, 0]) + d_ref[...] * dyv
            du_ref[0, rows, :] = du.astype(bf16)
            dd_ref[...] += jnp.sum(dyv * u.astype(f32), axis=0, keepdims=True)

    seq = pl.BlockSpec((1, l, SU), lambda j, bi: (bi, 0, j))
    sts = pl.BlockSpec((1, l, SW), lambda j, bi: (bi, 0, j))
    tab = pl.BlockSpec((2, 1, SU, SW), lambda j, bi: (0, j, 0, 0))
    return pl.pallas_call(
        body, name="s5_bwd", grid=(SJ, b),
        in_specs=[seq, seq, sts, sts, pl.BlockSpec((2, 8, 8, SW), lambda j, bi: (0, 0, 0, j)), tab, tab,
                  pl.BlockSpec((1, SU), lambda j, bi: (0, j))],
        out_specs=[seq, pl.BlockSpec((2, SW), lambda j, bi: (0, j)), tab, tab,
                   pl.BlockSpec((1, SU), lambda j, bi: (0, j))],
        out_shape=[jax.ShapeDtypeStruct((b, l, BW), bf16), jax.ShapeDtypeStruct((2, NSTATE), f32),
                   jax.ShapeDtypeStruct((2, SJ, SU, SW), f32), jax.ShapeDtypeStruct((2, SJ, SU, SW), f32),
                   jax.ShapeDtypeStruct((1, BW), f32)],
        scratch_shapes=[pltpu.VMEM((l, SW), f32)] * 2,
        compiler_params=_cp(2))(proj3, dy, s_re, s_im, t8, bb, cb, dskip)


AHC = 2
AHW = AHC * 128


def _att_mask(n, nb):
    if nb == 1:
        qi = lax.broadcasted_iota(jnp.int32, (ABLK, ABLK), 0)
        kj = lax.broadcasted_iota(jnp.int32, (ABLK, ABLK), 1)
        return kj <= qi
    qi = lax.broadcasted_iota(jnp.int32, (ABLK, 2 * ABLK), 0)
    kj = lax.broadcasted_iota(jnp.int32, (ABLK, 2 * ABLK), 1)
    return (kj >= qi) & (kj <= qi + ABLK) & ((n > 0) | (kj >= ABLK))


def _att_rows(it, nb, dil):
    r, n = it // nb, it % nb
    cur = pl.ds(r + n * (ABLK * dil), ABLK, stride=dil)
    prv = pl.ds(r + jnp.maximum(n - 1, 0) * (ABLK * dil), ABLK, stride=dil)
    return n, cur, prv


def _att_keys(ref, c, cur, prv, nb):
    if nb == 1:
        x = ref[c, cur, :].astype(bf16)
    else:
        x = jnp.concatenate([ref[c, prv, :], ref[c, cur, :]], axis=0).astype(bf16)
    head0 = lax.broadcasted_iota(jnp.int32, x.shape, 1) < HD
    zero = jnp.zeros_like(x)
    return jnp.concatenate([jnp.where(head0, x, zero), jnp.where(head0, zero, x)], axis=0)


def _per_head(nk, a0, a1):
    col = lax.broadcasted_iota(jnp.int32, (ABLK, 2 * nk), 1)
    return jnp.where(col < nk, a0, a1)


def _to_chunks(src_ref, dst):
    for c in range(AHC):
        dst[c] = src_ref[0, :, c * 128:(c + 1) * 128].astype(f32)


def att_fwd(proj3, g_idx, dil):
    b, l, _ = proj3.shape
    nb = l // dil // ABLK
    nhalf = BW // AHW

    def body(q_ref, k_ref, v_ref, o_ref, lse_ref, qf, kf, vf, of):
        hh = pl.program_id(1)
        _to_chunks(q_ref, qf)
        _to_chunks(k_ref, kf)
        _to_chunks(v_ref, vf)
        lane = lax.broadcasted_iota(jnp.int32, (ABLK, 128), 1)

        def step(it, carry):
            n, cur, prv = _att_rows(it, nb, dil)
            valid = _att_mask(n, nb)
            valid = jnp.concatenate([valid, valid], axis=1)
            nk = valid.shape[1] // 2
            lse_all = jnp.zeros((ABLK, 128), f32)
            for c in range(AHC):
                q = (qf[c, cur, :] * ATT_SCALE).astype(bf16)
                k = _att_keys(kf, c, cur, prv, nb)
                v = _att_keys(vf, c, cur, prv, nb)
                s = jnp.where(valid, _dot_nt(q, k), NEG)
                m0 = jnp.max(s[:, :nk], axis=-1, keepdims=True)
                m1 = jnp.max(s[:, nk:], axis=-1, keepdims=True)
                p = jnp.exp(s - _per_head(nk, m0, m1))
                den0 = jnp.sum(p[:, :nk], axis=-1, keepdims=True)
                den1 = jnp.sum(p[:, nk:], axis=-1, keepdims=True)
                of[c, cur, :] = _dot(p.astype(bf16), v) * jnp.where(lane < HD, 1.0 / den0, 1.0 / den1)
                head = hh * (2 * AHC) + 2 * c
                lse_all = (lse_all + jnp.where(lane == head, m0 + jnp.log(den0), 0.0)
                           + jnp.where(lane == head + 1, m1 + jnp.log(den1), 0.0))

            lse_ref[0, 0, cur, :] = lse_all
            return carry

        lax.fori_loop(0, dil * nb, step, 0, unroll=4)
        for c in range(AHC):
            o_ref[0, :, c * 128:(c + 1) * 128] = of[c].astype(bf16)

    col = lambda c: pl.BlockSpec((1, l, AHW), lambda bi, hh: (bi, 0, c * nhalf + hh))
    return pl.pallas_call(
        body, name=f"att_fwd{g_idx}", grid=(b, nhalf),
        in_specs=[col(1 + g_idx), col(4), col(5)],
        out_specs=[pl.BlockSpec((1, l, AHW), lambda bi, hh: (bi, 0, hh)),
                   pl.BlockSpec((1, 1, l, 128), lambda bi, hh: (bi, hh, 0, 0))],
        out_shape=[jax.ShapeDtypeStruct((b, l, BW), bf16), jax.ShapeDtypeStruct((b, nhalf, l, 128), f32)],
        scratch_shapes=[pltpu.VMEM((AHC, l, 128), f32)] * 4,
        compiler_params=_cp(2))(proj3, proj3, proj3)


def att_bwd(proj3, do, lse_tot, delta, g_idx, dil):
    b, l, _ = proj3.shape
    nb = l // dil // ABLK
    nhalf = BW // AHW

    def body(q_ref, k_ref, v_ref, do_ref, l_ref, dl_ref, dq_out, dk_out, dv_out, qf, kf, vf, dof,
             dq_ref, dk_ref, dv_ref):
        hh = pl.program_id(1)
        _to_chunks(q_ref, qf)
        _to_chunks(k_ref, kf)
        _to_chunks(v_ref, vf)
        _to_chunks(do_ref, dof)
        dk_ref[...] = jnp.zeros_like(dk_ref)
        dv_ref[...] = jnp.zeros_like(dv_ref)
        lane = lax.broadcasted_iota(jnp.int32, (ABLK, 128), 1)

        def step(it, carry):
            n, cur, prv = _att_rows(it, nb, dil)
            valid = _att_mask(n, nb)
            valid = jnp.concatenate([valid, valid], axis=1)
            nk = valid.shape[1] // 2
            lse_b = l_ref[0, cur, :]
            dl_b = dl_ref[0, cur, :]
            head0 = lax.broadcasted_iota(jnp.int32, (nk, 128), 1) < HD
            for c in range(AHC):
                q = (qf[c, cur, :] * ATT_SCALE).astype(bf16)
                dob = dof[c, cur, :].astype(bf16)
                k = _att_keys(kf, c, cur, prv, nb)
                v = _att_keys(vf, c, cur, prv, nb)
                head = hh * (2 * AHC) + 2 * c
                pick = lambda a, h: jnp.sum(jnp.where(lane == h, a, 0.0), axis=-1, keepdims=True)
                lse_h = _per_head(nk, pick(lse_b, head), pick(lse_b, head + 1))
                dl_h = _per_head(nk, pick(dl_b, head), pick(dl_b, head + 1))
                s = _dot_nt(q, k)
                p = jnp.where(valid, jnp.exp(jnp.minimum(s - lse_h, 60.0)), 0.0)
                ds = (p * (_dot_nt(dob, v) - dl_h)).astype(bf16)
                dq_ref[0, c, cur, :] = _dot(ds, k) * ATT_SCALE
                dk2 = _dot_tn(ds, q)
                dv2 = _dot_tn(p.astype(bf16), dob)
                dk = jnp.where(head0, dk2[:nk], dk2[nk:])
                dv = jnp.where(head0, dv2[:nk], dv2[nk:])
                if nb == 1:
                    dk_ref[0, c, cur, :] += dk
                    dv_ref[0, c, cur, :] += dv
                else:
                    dk_ref[0, c, cur, :] += dk[ABLK:]
                    dv_ref[0, c, cur, :] += dv[ABLK:]
                    dk_ref[0, c, prv, :] += dk[:ABLK]
                    dv_ref[0, c, prv, :] += dv[:ABLK]

            return carry

        lax.fori_loop(0, dil * nb, step, 0, unroll=4)
        dq_out[0] = dq_ref[0].astype(bf16)
        dk_out[0] = dk_ref[0].astype(bf16)
        dv_out[0] = dv_ref[0].astype(bf16)

    col = lambda c: pl.BlockSpec((1, l, AHW), lambda bi, hh: (bi, 0, c * nhalf + hh))
    own = pl.BlockSpec((1, l, AHW), lambda bi, hh: (bi, 0, hh))
    own128 = pl.BlockSpec((1, l, 128), lambda bi, hh: (bi, 0, 0))
    chunked = pl.BlockSpec((1, AHC, l, 128), lambda bi, hh: (bi, hh, 0, 0))
    return pl.pallas_call(
        body, name=f"att_bwd{g_idx}", grid=(b, nhalf),
        in_specs=[col(1 + g_idx), col(4), col(5), own, own128, own128],
        out_specs=[chunked] * 3,
        out_shape=[jax.ShapeDtypeStruct((b, BW // 128, l, 128), bf16)] * 3,
        scratch_shapes=[pltpu.VMEM((AHC, l, 128), f32)] * 4 + [pltpu.VMEM((1, AHC, l, 128), f32)] * 3,
        compiler_params=_cp(2, 56))(proj3, proj3, proj3, do, lse_tot, delta)


CPAD = 32
CTAIL = 16
CR = 128
CSLAB = CR + 40


def _tap_windows(slab, off, mis):
    ntap = (CW - 1 - mis) // 8 + 1
    rot = (off + mis) % 8
    base = off + mis - rot
    shifted = pltpu.roll(slab, CSLAB - rot, 0) if rot else slab
    for a in range(ntap):
        yield 8 * a + mis, shifted[base + 8 * a:base + 8 * a + CR]


def _fill_glu(cv_ref, pad, l):
    pad[0:CPAD, :] = jnp.zeros((CPAD, BW), f32)
    pad[CPAD:CPAD + l, :] = cv_ref[0, :, :BW].astype(f32) * _sigmoid(cv_ref[0, :, BW:].astype(f32))
    pad[CPAD + l:, :] = jnp.zeros((CTAIL, BW), f32)


def conv_fwd(proj3, cw, cb):
    b, l, _ = proj3.shape

    def body(cv_ref, w_ref, b_ref, o_ref, pad):
        _fill_glu(cv_ref, pad, l)
        for lc in range(BW // 128):
            lanes = slice(lc * 128, (lc + 1) * 128)
            wv = w_ref[:, lanes]

            def step(c, carry):
                base = pl.multiple_of(c * CR, CR)
                slab = pad[pl.ds(base, CSLAB), lanes]
                acc = jnp.zeros((CR, 128), f32) + b_ref[:, lanes]
                for mis in range(8):
                    for k, win in _tap_windows(slab, CPAD - (CW - 1), mis):
                        acc = acc + wv[k:k + 1] * win
                o_ref[0, pl.ds(base, CR), lanes] = acc
                return carry

            lax.fori_loop(0, l // CR, step, 0)

    return pl.pallas_call(
        body, name="conv_fwd", grid=(b,),
        in_specs=[pl.BlockSpec((1, l, 2 * BW), lambda i: (i, 0, 3)),
                  pl.BlockSpec((32, BW), lambda i: (0, 0)), pl.BlockSpec((1, BW), lambda i: (0, 0))],
        out_specs=pl.BlockSpec((1, l, BW), lambda i: (i, 0, 0)),
        out_shape=jax.ShapeDtypeStruct((b, l, BW), f32),
        scratch_shapes=[pltpu.VMEM((CPAD + l + CTAIL, BW), f32)], compiler_params=_cp(1))(proj3, cw, cb)


def conv_bwd(proj3, dhc, cw):
    b, l, _ = proj3.shape

    def body(cv_ref, d_ref, w_ref, dcv_ref, dw_ref, db_ref, pad, dpad):
        i = pl.program_id(0)

        @pl.when(i == 0)
        def _():
            dw_ref[...] = jnp.zeros_like(dw_ref)
            db_ref[...] = jnp.zeros_like(db_ref)

        _fill_glu(cv_ref, pad, l)
        dpad[0:l, :] = d_ref[0]
        dpad[l:, :] = jnp.zeros((CPAD + CTAIL, BW), f32)
        db_ref[...] += jnp.sum(d_ref[0], axis=0, keepdims=True)
        for lc in range(BW // 128):
            lanes = slice(lc * 128, (lc + 1) * 128)
            glanes = slice(BW + lc * 128, BW + (lc + 1) * 128)
            wv = w_ref[:, lanes]

            for mis in range(8):
                ntap = (CW - 1 - mis) // 8 + 1

                def dw_step(c, accs, mis=mis, lanes=lanes):
                    base = pl.multiple_of(c * CR, CR)
                    slab = pad[pl.ds(base, CSLAB), lanes]
                    dv = dpad[pl.ds(base, CR), lanes]
                    return tuple(acc + (dv * win).reshape(CR // 8, 8, 128).sum(axis=0) for acc, (_, win)
                                 in zip(accs, _tap_windows(slab, CPAD - (CW - 1), mis)))

                accs = lax.fori_loop(0, l // CR, dw_step, tuple(jnp.zeros((8, 128), f32) for _ in range(ntap)))
                for a in range(ntap):
                    k = 8 * a + mis
                    dw_ref[k:k + 1, lanes] += jnp.sum(accs[a], axis=0, keepdims=True)

            def dh_step(c, carry, lanes=lanes, glanes=glanes, wv=wv):
                base = pl.multiple_of(c * CR, CR)
                slab = dpad[pl.ds(base, CSLAB), lanes]
                acc = jnp.zeros((CR, 128), f32)
                for mis in range(8):
                    for kk, win in _tap_windows(slab, 0, mis):
                        acc = acc + wv[CW - 1 - kk:CW - kk] * win
                rows = pl.ds(base, CR)
                a = cv_ref[0, rows, lanes].astype(f32)
                sg = _sigmoid(cv_ref[0, rows, glanes].astype(f32))
                dcv_ref[0, rows, lanes] = (acc * sg).astype(bf16)
                dcv_ref[0, rows, glanes] = (acc * a * sg * (1.0 - sg)).astype(bf16)
                return carry

            lax.fori_loop(0, l // CR, dh_step, 0)

    return pl.pallas_call(
        body, name="conv_bwd", grid=(b,),
        in_specs=[pl.BlockSpec((1, l, 2 * BW), lambda i: (i, 0, 3)),
                  pl.BlockSpec((1, l, BW), lambda i: (i, 0, 0)),
                  pl.BlockSpec((32, BW), lambda i: (0, 0))],
        out_specs=[pl.BlockSpec((1, l, 2 * BW), lambda i: (i, 0, 0)),
                   pl.BlockSpec((32, BW), lambda i: (0, 0)), pl.BlockSpec((1, BW), lambda i: (0, 0))],
        out_shape=[jax.ShapeDtypeStruct((b, l, 2 * BW), bf16), jax.ShapeDtypeStruct((32, BW), f32),
                   jax.ShapeDtypeStruct((1, BW), f32)],
        scratch_shapes=[pltpu.VMEM((CPAD + l + CTAIL, BW), f32), pltpu.VMEM((l + CPAD + CTAIL, BW), f32)],
        compiler_params=_cp(1))(proj3, dhc, cw)


def _head_expand():
    r = lax.broadcasted_iota(jnp.int32, (128, BW), 0)
    c = lax.broadcasted_iota(jnp.int32, (128, BW), 1) // HD
    return (r == c).astype(f32)


def _head_reduce():
    r = lax.broadcasted_iota(jnp.int32, (BW, 128), 0) // HD
    c = lax.broadcasted_iota(jnp.int32, (BW, 128), 1)
    return (r == c).astype(f32)


def _merge_common(ys_ref, o_refs, l_refs, hc_ref, g_refs, bg_ref, lng_ref, lnb_ref, wglu_ref, watt_ref, wpw_ref):
    r = {}
    ysv = ys_ref[...]
    r["ys"] = ysv
    r["ysin"] = _gelu(ysv).astype(bf16)
    z = _dot(r["ysin"], wglu_ref[...])
    r["z1"], r["sg2"] = z[:, :D], _sigmoid(z[:, D:])
    r["y_s"] = r["z1"] * r["sg2"]
    ls = [lr_[0, 0] + lr_[0, 1] for lr_ in l_refs]
    mx = jnp.maximum(jnp.maximum(ls[0], ls[1]), ls[2])
    es = [jnp.exp(v - mx) for v in ls]
    tot = es[0] + es[1] + es[2]
    r["lse_tot"] = mx + jnp.log(tot)
    e_mat = _head_expand()
    o = jnp.zeros(ysv.shape, f32)
    for e, o_ref in zip(es, o_refs):
        o = o + _dot_hi(e / tot, e_mat) * o_ref[...].astype(f32)
    r["o"] = o
    r["ob"] = o.astype(bf16)
    r["y_a"] = _dot(r["ob"], watt_ref[...])
    hc = hc_ref[...]
    mu = jnp.mean(hc, axis=-1, keepdims=True)
    xc = hc - mu
    rstd = lax.rsqrt(jnp.mean(xc * xc, axis=-1, keepdims=True) + EPS)
    r["xh"], r["rstd"] = xc * rstd, rstd
    hn = r["xh"] * lng_ref[...] + lnb_ref[...]
    r["hn"] = hn
    r["sgn"] = _sigmoid(hn)
    r["hs"] = (hn * r["sgn"]).astype(bf16)
    r["y_c"] = _dot(r["hs"], wpw_ref[...])
    r["gates"] = [_sigmoid(g_refs[k][...].astype(f32) + bg_ref[:, k * D:(k + 1) * D]) for k in range(3)]
    r["merged"] = r["gates"][0] * r["y_s"] + r["gates"][1] * r["y_a"] + r["gates"][2] * r["y_c"]
    return r


TBM = 256


def _merge_in_specs(tok, tb, lses):
    w = lambda shape: pl.BlockSpec(shape, lambda i: (0, 0), pipeline_mode=pl.Buffered(1))
    nbl = lses[0].shape[2] // tb
    return ([pl.BlockSpec((tb, D), tok), pl.BlockSpec((tb, BW), tok)]
            + [pl.BlockSpec((tb, BW), tok)] * 3
            + [pl.BlockSpec((1, 2, tb, 128), lambda i: (i // nbl, 0, i % nbl, 0))] * 3
            + [pl.BlockSpec((tb, BW), tok)]
            + [pl.BlockSpec((tb, D), lambda i, k=k: (i, 4 + k)) for k in range(3)]
            + [w((1, 3 * D)), w((1, BW)), w((1, BW)), w((BW, 2 * D)), w((BW, D)), w((BW, D)), w((D, D))])


def merge_fwd(x, ys, os_, lses, hc, proj, bg, lng, lnb, wglu, watt, wpw, wout):
    n = x.shape[0]

    def body(x_ref, ys_ref, o1, o2, o3, l1, l2, l3, hc_ref, g0, g1, g2, bg_ref, lng_ref, lnb_ref,
             wglu_ref, watt_ref, wpw_ref, wout_ref, x1_ref):
        r = _merge_common(ys_ref, (o1, o2, o3), (l1, l2, l3), hc_ref, (g0, g1, g2), bg_ref, lng_ref, lnb_ref,
                          wglu_ref, watt_ref, wpw_ref)
        x1_ref[...] = x_ref[...] + _dot(r["merged"].astype(bf16), wout_ref[...])

    tok = lambda i: (i, 0)
    return pl.pallas_call(
        body, name="merge_fwd", grid=(n // TB,), in_specs=_merge_in_specs(tok, TB, lses),
        out_specs=pl.BlockSpec((TB, D), tok), out_shape=jax.ShapeDtypeStruct((n, D), f32),
        compiler_params=_cp(1, 56))(x, ys, *os_, *lses, hc, proj, proj, proj, bg, lng, lnb, wglu, watt, wpw, wout)


def merge_bwd(dx1, ys, os_, lses, hc, proj, bg, lng, lnb, wglu, watt, wpw, wout):
    n = dx1.shape[0]

    def body(dx_ref, ys_ref, o1, o2, o3, l1, l2, l3, hc_ref, g0, g1, g2, bg_ref, lng_ref, lnb_ref,
             wglu_ref, watt_ref, wpw_ref, wout_ref,
             dys_ref, do_ref, delta_ref, ltot_ref, dhc_ref, dgate_ref, ysin_ref, dz_ref, ob_ref, dya_ref,
             hs_ref, dyc_ref, mg_ref, dbg_ref, dlng_ref, dlnb_ref):
        i = pl.program_id(0)

        @pl.when(i == 0)
        def _():
            dbg_ref[...] = jnp.zeros_like(dbg_ref)
            dlng_ref[...] = jnp.zeros_like(dlng_ref)
            dlnb_ref[...] = jnp.zeros_like(dlnb_ref)

        r = _merge_common(ys_ref, (o1, o2, o3), (l1, l2, l3), hc_ref, (g0, g1, g2), bg_ref, lng_ref, lnb_ref,
                          wglu_ref, watt_ref, wpw_ref)
        mg_ref[...] = r["merged"].astype(bf16)
        ysin_ref[...] = r["ysin"]
        ob_ref[...] = r["ob"]
        hs_ref[...] = r["hs"]
        ltot_ref[...] = r["lse_tot"]
        dm = _dot_nt(dx_ref[...].astype(bf16), wout_ref[...])
        ys3 = (r["y_s"], r["y_a"], r["y_c"])
        for k in range(3):
            gk = r["gates"][k]
            dgr = dm * ys3[k] * gk * (1.0 - gk)
            dgate_ref[:, k * D:(k + 1) * D] = dgr.astype(bf16)
            dbg_ref[:, k * D:(k + 1) * D] += jnp.sum(dgr, axis=0, keepdims=True)
        dy_s = dm * r["gates"][0]
        sg2 = r["sg2"]
        dz = jnp.concatenate([dy_s * sg2, dy_s * r["z1"] * sg2 * (1.0 - sg2)], axis=1).astype(bf16)
        dz_ref[...] = dz
        dys_ref[...] = _dot_nt(dz, wglu_ref[...]) * _gelu_grad(r["ys"])
        dya = (dm * r["gates"][1]).astype(bf16)
        dya_ref[...] = dya
        do = _dot_nt(dya, watt_ref[...])
        do_ref[...] = do.astype(bf16)
        delta_ref[...] = _dot_hi(do * r["o"], _head_reduce())
        dyc = (dm * r["gates"][2]).astype(bf16)
        dyc_ref[...] = dyc
        sgn, hn = r["sgn"], r["hn"]
        dhn = _dot_nt(dyc, wpw_ref[...]) * sgn * (1.0 + hn * (1.0 - sgn))
        dlng_ref[...] += jnp.sum(dhn * r["xh"], axis=0, keepdims=True)
        dlnb_ref[...] += jnp.sum(dhn, axis=0, keepdims=True)
        dxh = dhn * lng_ref[...]
        xh = r["xh"]
        dhc_ref[...] = r["rstd"] * (dxh - jnp.mean(dxh, axis=-1, keepdims=True)
                                    - xh * jnp.mean(dxh * xh, axis=-1, keepdims=True))

    tok = lambda i: (i, 0)
    fix = lambda i: (0, 0)
    outs = [("dys", BW, f32), ("do", BW, bf16), ("delta", 128, f32), ("lse_tot", 128, f32), ("dhc", BW, f32),
            ("dgate", 3 * D, bf16), ("ysin", BW, bf16), ("dz", 2 * D, bf16), ("ob", BW, bf16), ("dya", D, bf16),
            ("hs", BW, bf16), ("dyc", D, bf16), ("merged", D, bf16)]
    small = [("dbg", 3 * D), ("dlng", BW), ("dlnb", BW)]
    res = pl.pallas_call(
        body, name="merge_bwd", grid=(n // TBM,), in_specs=_merge_in_specs(tok, TBM, lses),
        out_specs=[pl.BlockSpec((TBM, w), tok) for _, w, _ in outs] + [pl.BlockSpec((1, w), fix) for _, w in small],
        out_shape=[jax.ShapeDtypeStruct((n, w), dt) for _, w, dt in outs]
        + [jax.ShapeDtypeStruct((1, w), f32) for _, w in small],
        compiler_params=_cp(1, 56))(dx1, ys, *os_, *lses, hc, proj, proj, proj, bg, lng, lnb, wglu, watt, wpw, wout)
    return dict(zip([k for k, _, _ in outs] + [k for k, _ in small], res))


def assemble_dproj(du, dqs, dks, dvs, dcv, dgate):
    b, l, _ = du.shape
    nck = BW // 128

    def body(du_ref, q1, q2, q3, k1, k2, k3, v1, v2, v3, cv_ref, g_ref, o_ref):
        o_ref[0, :, 0:BW] = du_ref[0]
        for c in range(nck):
            for j, qr in enumerate((q1, q2, q3)):
                o_ref[0, :, (1 + j) * BW + c * 128:(1 + j) * BW + (c + 1) * 128] = qr[0, c]
            add3 = lambda r1, r2, r3: (r1[0, c].astype(f32) + r2[0, c].astype(f32) + r3[0, c].astype(f32)).astype(bf16)
            o_ref[0, :, 4 * BW + c * 128:4 * BW + (c + 1) * 128] = add3(k1, k2, k3)
            o_ref[0, :, 5 * BW + c * 128:5 * BW + (c + 1) * 128] = add3(v1, v2, v3)
        o_ref[0, :, 6 * BW:8 * BW] = cv_ref[0]
        o_ref[0, :, 8 * BW:] = g_ref[0]

    t = lambda w: pl.BlockSpec((1, TB, w), lambda bi, i: (bi, i, 0))
    ck = pl.BlockSpec((1, nck, TB, 128), lambda bi, i: (bi, 0, i, 0))
    return pl.pallas_call(
        body, name="assemble_dproj", grid=(b, l // TB),
        in_specs=[t(BW)] + [ck] * 9 + [t(2 * BW), t(3 * D)], out_specs=t(INC),
        out_shape=jax.ShapeDtypeStruct((b, l, INC), bf16), compiler_params=_cp(2))(du, *dqs, *dks, *dvs, dcv, dgate)


def _me():
    return lax.axis_index("x"), lax.axis_index("y"), lax.axis_index("c")


def _peers():
    x, y, c = _me()
    return [(x, y, 1 - c), (1 - x, y, c), (1 - x, y, 1 - c), (x, 1 - y, c), (x, 1 - y, 1 - c),
            (1 - x, 1 - y, c), (1 - x, 1 - y, 1 - c)]


def _rank(p):
    return 4 * p[0] + 2 * p[1] + p[2]


def allgather(arrs, name):
    na = len(arrs)
    units = [(a, j) for a in range(na) for j in range(arrs[a].shape[0])]
    nu = len(units)

    def body(*refs):
        ins, outs = refs[:na], refs[na:2 * na]
        send, recv, loc = refs[2 * na:]
        me = _rank(_me())
        local, remote = [], []
        for u, (a, j) in enumerate(units):
            own = pltpu.make_async_copy(ins[a].at[j], outs[a].at[j, me], loc.at[u])
            own.start()
            local.append(own)
        for u, (a, j) in enumerate(units):
            for k, p in enumerate(_peers()):
                cp = pltpu.make_async_remote_copy(src_ref=ins[a].at[j], dst_ref=outs[a].at[j, me],
                                                  send_sem=send.at[u, k], recv_sem=recv.at[u, k],
                                                  device_id=p, device_id_type=MESH)
                cp.start()
                remote.append(cp)
        for cp in local:
            cp.wait()
        for cp in remote:
            cp.wait()

    return pl.pallas_call(
        body, name=name, in_specs=[ANY] * na, out_specs=[ANY] * na,
        out_shape=[jax.ShapeDtypeStruct((a.shape[0], NDEV) + a.shape[1:], a.dtype) for a in arrs],
        scratch_shapes=[pltpu.SemaphoreType.DMA((nu, NDEV - 1)), pltpu.SemaphoreType.DMA((nu, NDEV - 1)),
                        pltpu.SemaphoreType.DMA((nu,))])(*arrs)


HBM = pl.BlockSpec(memory_space=pltpu.HBM)
SEM = pl.BlockSpec(memory_space=pltpu.SEMAPHORE)
_EFFECT = pltpu.SideEffectType.DATAFLOW_SIDE_EFFECTING


def _rank_slot(ref, r):
    if ref.shape[0] == NDEV:
        return ref.at[r]
    n = ref.shape[2] // 2
    return ref.at[r // 2, :, pl.ds(pl.multiple_of((r % 2) * n, 128), n)]


def _push_copies(srcs, lands, send, recv, scatter):
    me = _rank(_me())
    out = []
    for i in range(len(srcs)):
        for k, p in enumerate(_peers()):
            src = _rank_slot(srcs[i], _rank(p)) if scatter else srcs[i]
            dst = lands[i].at[k] if scatter else _rank_slot(lands[i], me)
            j = i * (NDEV - 1) + k
            out.append(pltpu.make_async_remote_copy(src_ref=src, dst_ref=dst, send_sem=send.at[j],
                                                    recv_sem=recv.at[j], device_id=p, device_id_type=MESH))
    return out


def push_start(srcs, lands, scatter, name, token, after=None):
    n = len(srcs)
    token = jnp.zeros((8, 128), f32) if token is None else token
    extra = [] if after is None else [after]

    def body(*refs):
        nin = 2 * n + 1 + len(extra)
        for cp in _push_copies(refs[:n], refs[n:2 * n], refs[nin], refs[nin + 1], scatter):
            cp.start()
        refs[-1][...] = refs[2 * n][...]

    sems = pltpu.SemaphoreType.DMA((n * (NDEV - 1),))
    vmem = pl.BlockSpec(memory_space=pltpu.VMEM)
    res = pl.pallas_call(
        body, name=name, in_specs=[HBM] * (2 * n) + [vmem] + [ANY] * len(extra),
        out_specs=[SEM, SEM] + [HBM] * (2 * n) + [vmem],
        out_shape=[sems, sems] + [pltpu.HBM(a.shape, a.dtype) for a in list(srcs) + list(lands)]
        + [jax.ShapeDtypeStruct((8, 128), f32)],
        input_output_aliases={i: 2 + i for i in range(2 * n)},
        compiler_params=pltpu.CompilerParams(has_side_effects=_EFFECT),
    )(*[pltpu.with_memory_space_constraint(a, pltpu.HBM) for a in list(srcs) + list(lands)], token, *extra)
    return res[0], res[1], res[2:2 + n], res[2 + n:2 + 2 * n], res[-1]


def push_wait(send, recv, srcs, lands, after, scatter, name):
    n = len(srcs)
    after = list(after) if isinstance(after, (list, tuple)) else [after]

    def body(*refs):
        for cp in _push_copies(refs[:n], refs[n:2 * n], refs[2 * n], refs[2 * n + 1], scatter):
            cp.wait_send()
            cp.wait_recv()

    res = pl.pallas_call(
        body, name=name, in_specs=[HBM] * (2 * n) + [SEM, SEM] + [ANY] * len(after), out_specs=[HBM] * (2 * n),
        out_shape=[pltpu.HBM(a.shape, a.dtype) for a in list(srcs) + list(lands)],
        input_output_aliases={i: i for i in range(2 * n)},
        compiler_params=pltpu.CompilerParams(has_side_effects=_EFFECT),
    )(*srcs, *lands, send, recv, *after)
    return res[:n], res[n:]


_C1 = 1.0 / (1.0 - ADAM_B1 ** ADAM_STEP)
_C2 = 1.0 / (1.0 - ADAM_B2 ** ADAM_STEP)


def _adamw(w, g, m, v):
    m = ADAM_B1 * m + (1.0 - ADAM_B1) * g
    v = ADAM_B2 * v + (1.0 - ADAM_B2) * (g * g)
    delta = -ADAM_LR * ((m * _C1) / (jnp.sqrt(v * _C2) + ADAM_EPS) + ADAM_WD * w)
    return delta, m, v


def adam_big(lands, owns, w, m, v, name):
    _, k, n = lands[0].shape
    tk = k
    while tk * n * 2 * NDEV > 2 * 1024 * 1024 and tk % 16 == 0:
        tk //= 2

    def body(*refs):
        l_refs, o_refs = refs[:DEPTH], refs[DEPTH:2 * DEPTH]
        w_ref, m_ref, v_ref, g_ref, d_ref, nm_ref, nv_ref = refs[2 * DEPTH:]
        for l in range(DEPTH):
            g = o_refs[l][...].astype(f32)
            for s in range(NDEV - 1):
                g = g + l_refs[l][s].astype(f32)
            d, nm, nv = _adamw(w_ref[l], g, m_ref[l], v_ref[l])
            g_ref[l], d_ref[l], nm_ref[l], nv_ref[l] = g, d, nm, nv

    blk = pl.BlockSpec((DEPTH, tk, n), lambda i: (0, i, 0))
    return pl.pallas_call(
        body, name=name, grid=(k // tk,),
        in_specs=[pl.BlockSpec((NDEV - 1, tk, n), lambda i: (0, i, 0))] * DEPTH
        + [pl.BlockSpec((tk, n), lambda i: (i, 0))] * DEPTH + [blk, blk, blk],
        out_specs=[blk] * 4, out_shape=[jax.ShapeDtypeStruct(w.shape, f32)] * 4,
        compiler_params=_cp(1))(*lands, *owns, w, m, v)


def adam_small(gath, w, m, v):
    r = w.shape[0]
    tr = 512

    def body(g_ref, w_ref, m_ref, v_ref, go_ref, d_ref, nm_ref, nv_ref):
        g = g_ref[0]
        for s in range(1, NDEV):
            g = g + g_ref[s]
        d, nm, nv = _adamw(w_ref[...], g, m_ref[...], v_ref[...])
        go_ref[...], d_ref[...], nm_ref[...], nv_ref[...] = g, d, nm, nv

    blk = pl.BlockSpec((tr, 128), lambda i: (i, 0))
    return pl.pallas_call(
        body, name="adam_small", grid=(r // tr,),
        in_specs=[pl.BlockSpec((NDEV, tr, 128), lambda i: (0, i, 0)), blk, blk, blk],
        out_specs=[blk] * 4, out_shape=[jax.ShapeDtypeStruct((r, 128), f32)] * 4,
        compiler_params=_cp(1))(gath, w, m, v)


SMALL = ["norm1_g", "b_gate", "ssm_lambda_re", "ssm_lambda_im", "ssm_log_dt", "ssm_b_re", "ssm_b_im",
         "ssm_c_re", "ssm_c_im", "ssm_d", "conv_w", "conv_b", "conv_ln_g", "conv_ln_b", "norm2_g", "final_g"]
BIG = ["w_in", "w_ssm_glu", "w_att_up", "w_conv_pw2", "w_out", "w_ffn_in", "w_ffn_out"]
ORDER = ["norm1_g", "w_in", "b_gate", "ssm_lambda_re", "ssm_lambda_im", "ssm_log_dt", "ssm_b_re", "ssm_b_im",
         "ssm_c_re", "ssm_c_im", "ssm_d", "w_ssm_glu", "w_att_up", "conv_w", "conv_b", "conv_ln_g", "conv_ln_b",
         "w_conv_pw2", "w_out", "norm2_g", "w_ffn_in", "w_ffn_out", "final_g"]
PACK_ROWS = 2560


def _pack(arrs):
    flat = jnp.concatenate([a.reshape(-1).astype(f32) for a in arrs])
    return jnp.pad(flat, (0, PACK_ROWS * 128 - flat.shape[0])).reshape(PACK_ROWS, 128)


def _unpack(pack, shapes):
    flat = pack.reshape(-1)
    out, off = [], 0
    for s in shapes:
        sz = math.prod(s)
        out.append(flat[off:off + sz].reshape(s))
        off += sz
    return out


def _bt(b):
    return b.transpose(2, 0, 1).reshape(GH, NSTATE)


def _bt_inv(bt):
    return bt.reshape(GH, NG, NS).transpose(1, 2, 0)


def _ct(c):
    return c.transpose(1, 0, 2).reshape(GH, NSTATE)


def _ct_inv(ct):
    return ct.reshape(GH, NG, NS).transpose(1, 0, 2)


def local_step(x, loss_target, P, weights, on_grads, start_token=None):
    bsz, seq, _ = x.shape
    n = bsz * seq

    def natural(g3):
        return g3.transpose(1, 0, 2).reshape(g3.shape[1], NDEV * g3.shape[2])

    tokens = [] if start_token is None else [start_token]

    def after_pushes(a):
        while tokens:
            a = a + tokens.pop()[0:1, 0:1]
        return a

    def pushed(tok):
        if tok is not None:
            tokens.append(tok)

    def take(layer, group, after):
        got = dict(weights(layer, group, after))
        pushed(got.pop("_token", None))
        return got

    s5_in, s5_tabs = [], []
    for l in range(DEPTH):
        lr = P["ssm_lambda_re"][l].reshape(1, NSTATE)
        li = P["ssm_lambda_im"][l].reshape(1, NSTATE)
        ld = jnp.repeat(P["ssm_log_dt"][l], NS).reshape(1, NSTATE)
        btr, bti = _bt(P["ssm_b_re"][l]), _bt(P["ssm_b_im"][l])
        s5_in.append((lr, li, ld, btr, bti))
        s5_tabs.append(s5_params(lr, li, ld, btr, bti, _ct(P["ssm_c_re"][l]), _ct(P["ssm_c_im"][l])))

    xs = x.reshape(n, D)
    saved = []
    conv_w_pad = None
    for l in range(DEPTH):
        S = {"x": xs}
        h1 = rms_fwd(xs, after_pushes(P["norm1_g"][l][None]))
        G = take(l, "in", [h1] + [t for tabs in s5_tabs for t in tabs] if l == 0 else h1)
        if conv_w_pad is None:
            conv_w_full = G["conv_w"].transpose(1, 2, 0, 3).reshape(DEPTH, CW, BW)
            conv_w_pad = jnp.pad(conv_w_full, ((0, 0), (0, 1), (0, 0)))
        w_in4 = G["w_in"][None]
        proj = inproj(h1, w_in4, 0)
        proj3 = proj.reshape(bsz, seq, INC)
        lr, li, ld, btr, bti = s5_in[l]
        t8, bb, cb = s5_tabs[l]
        dskip = P["ssm_d"][l][None]
        ys, s_re, s_im = s5_fwd(proj3, t8, bb, cb, dskip)
        att = [att_fwd(proj3, gi, dil) for gi, (_, dil) in enumerate(PATTERNS)]
        hc = conv_fwd(proj3, conv_w_pad[l], P["conv_b"][l][None])
        G.update(take(l, "mix", hc))
        wts = dict(wglu=natural(G["w_ssm_glu"]), watt=natural(G["w_att_up"]),
                   wpw=natural(G["w_conv_pw2"]), wout=G["w_out"].reshape(D, D))
        mi = dict(ys=ys.reshape(n, BW), os_=[a[0].reshape(n, BW) for a in att],
                  lses=[a[1] for a in att], hc=hc.reshape(n, BW),
                  proj=proj, bg=P["b_gate"][l][None], lng=after_pushes(P["conv_ln_g"][l][None]),
                  lnb=P["conv_ln_b"][l][None],
                  **wts)
        x1 = merge_fwd(xs, **mi)
        G.update(take(l, "ffn", x1))
        w_ffn = (G["w_ffn_in"][None], G["w_ffn_out"][None])
        x2, z1s, z2s, h2 = ffn_fwd(x1, P["norm2_g"][l][None], *w_ffn, 0)
        S.update(h1=h1, proj=proj, proj3=proj3, tabs=(s_re, s_im, t8, bb, cb), mi=mi, x1=x1, w_in4=w_in4, w_ffn=w_ffn,
                 zs=(z1s, z2s), h2=h2,
                 sp=(lr, li, ld, btr, bti), dskip=dskip)
        saved.append(S)
        xs = x2

    loss8, dx, dfinal = loss_head(xs, P["final_g"][None], loss_target.reshape(n, D))

    small_g = {k: [None] * DEPTH for k in SMALL if k != "final_g"}
    tokblk = lambda w: pl.BlockSpec((1024, w), lambda s, i: (i, 0))
    colblk = lambda w: pl.BlockSpec((1024, w), lambda s, i: (i, s))
    for l in reversed(range(DEPTH)):
        S = saved[l]
        g2 = P["norm2_g"][l][None]
        dh4, dwa, dwb, dw2 = ffn_bwd(S["h2"], dx, *S["zs"], *S["w_ffn"], 0)
        dx1, dg2 = norm_bwd_fin(S["x1"], after_pushes(g2), dx, dh4, "ffn_bwd_fin")
        small_g["norm2_g"][l] = dg2
        pushed(on_grads(l, "ffn", dict(w_ffn_in=jnp.concatenate([dwa, dwb], axis=0),
                                       w_ffn_out=dw2.reshape(NDEV, NSH_FF // 2, D))))
        mb = merge_bwd(dx1, **dict(S["mi"], lng=after_pushes(S["mi"]["lng"])))
        small_g["b_gate"][l], small_g["conv_ln_g"][l], small_g["conv_ln_b"][l] = mb["dbg"], mb["dlng"], mb["dlnb"]
        dws = dw_mix(mb["ysin"], mb["dz"], mb["ob"], mb["dya"], mb["hs"], mb["dyc"], mb["merged"], dx1)
        pushed(on_grads(l, "mix", dict(zip(("w_ssm_glu", "w_att_up", "w_conv_pw2", "w_out"), dws))))
        dcv, dcw, dcb = conv_bwd(S["proj3"], mb["dhc"].reshape(bsz, seq, BW), after_pushes(conv_w_pad[l]))
        small_g["conv_w"][l] = dcw[:CW].reshape(CW, NDEV, BW // NDEV).transpose(1, 0, 2)
        small_g["conv_b"][l] = dcb
        ab = [att_bwd(S["proj3"], mb["do"].reshape(bsz, seq, BW), mb["lse_tot"].reshape(bsz, seq, 128),
                      mb["delta"].reshape(bsz, seq, 128), gi, dil) for gi, (_, dil) in enumerate(PATTERNS)]
        du, d_a, d_bb, d_cb, d_d = s5_bwd(S["proj3"], mb["dys"].reshape(bsz, seq, BW), *S["tabs"], S["dskip"])
        lr, li, ld, btr, bti = S["sp"]
        dlr, dli, dld, dbt, dct = s5_params_bwd(lr, li, ld, btr, bti, d_a, d_bb, d_cb)
        small_g["ssm_lambda_re"][l], small_g["ssm_lambda_im"][l] = dlr.reshape(NG, NS), dli.reshape(NG, NS)
        small_g["ssm_log_dt"][l] = dld[0, :NG]
        small_g["ssm_b_re"][l], small_g["ssm_b_im"][l] = _bt_inv(dbt[0]), _bt_inv(dbt[1])
        small_g["ssm_c_re"][l], small_g["ssm_c_im"][l] = _ct_inv(dct[0]), _ct_inv(dct[1])
        small_g["ssm_d"][l] = d_d
        dproj = assemble_dproj(du, [a[0] for a in ab], [a[1] for a in ab], [a[2] for a in ab],
                               dcv, mb["dgate"].reshape(bsz, seq, 3 * D)).reshape(n, INC)
        nblk, wblk = S["w_in4"].shape[1], S["w_in4"].shape[3]
        pushed(on_grads(l, "in", dict(w_in=mm_tn(S["h1"], dproj, tokblk(D), colblk(wblk), nblk, D, wblk, n, "dw_in"))))
        if l == 0:
            pushed(on_grads(l, "small", dict(small_g=small_g, loss8=loss8, dfinal=dfinal)))
        dx, dg1 = inproj_bwd(dproj, S["w_in4"], 0, S["x"], after_pushes(P["norm1_g"][l][None]), dx1)
        small_g["norm1_g"][l] = dg1
    return loss8, dx, dfinal, small_g


def kernel(x, norm1_g, w_in, b_gate, ssm_lambda_re, ssm_lambda_im, ssm_log_dt, ssm_b_re, ssm_b_im, ssm_c_re, ssm_c_im, ssm_d, w_ssm_glu, w_att_up, conv_w, conv_b, conv_ln_g, conv_ln_b, w_conv_pw2, w_out, norm2_g, w_ffn_in, w_ffn_out, final_g, loss_target, m_norm1_g, m_w_in, m_b_gate, m_ssm_lambda_re, m_ssm_lambda_im, m_ssm_log_dt, m_ssm_b_re, m_ssm_b_im, m_ssm_c_re, m_ssm_c_im, m_ssm_d, m_w_ssm_glu, m_w_att_up, m_conv_w, m_conv_b, m_conv_ln_g, m_conv_ln_b, m_w_conv_pw2, m_w_out, m_norm2_g, m_w_ffn_in, m_w_ffn_out, m_final_g, v_norm1_g, v_w_in, v_b_gate, v_ssm_lambda_re, v_ssm_lambda_im, v_ssm_log_dt, v_ssm_b_re, v_ssm_b_im, v_ssm_c_re, v_ssm_c_im, v_ssm_d, v_w_ssm_glu, v_w_att_up, v_conv_w, v_conv_b, v_conv_ln_g, v_conv_ln_b, v_w_conv_pw2, v_w_out, v_norm2_g, v_w_ffn_in, v_w_ffn_out, v_final_g):
    args = dict(locals())
    W = {k: args[k] for k in ORDER}
    M = {k: args["m_" + k] for k in ORDER}
    V = {k: args["v_" + k] for k in ORDER}
    bsz, seq, _ = x.shape
    n = bsz * seq
    me = 4 * lax.axis_index("x") + 2 * lax.axis_index("y") + lax.axis_index("c")

    groups = {"in": ["w_in"], "mix": ["w_ssm_glu", "w_att_up", "w_conv_pw2", "w_out"], "ffn": ["w_ffn_in", "w_ffn_out"]}
    wb = {k: W[k].astype(bf16) for k in BIG}

    def landing(shard, paired=False):
        if paired:
            k_, n_ = shard.shape
            return lax.dynamic_update_slice(lax.empty((NDEV // 2, k_, 2 * n_), shard.dtype), shard[None],
                                            (me // 2, 0, (me % 2) * n_))
        return lax.dynamic_update_index_in_dim(lax.empty((NDEV,) + shard.shape, shard.dtype), shard, me, 0)

    def own_part(by_rank):
        if by_rank.shape[0] == NDEV:
            return lax.dynamic_index_in_dim(by_rank, me, 0, keepdims=False)
        n_ = by_rank.shape[2] // 2
        return lax.dynamic_slice(by_rank, (me // 2, 0, (me % 2) * n_), (1, by_rank.shape[1], n_))[0]

    plan = [("gather_a", [("w_in", 0), ("conv_w", None)]),
            ("gather_b", [(k, 0) for k in groups["mix"] + groups["ffn"]]),
            ("gather_c", [(k, 1) for k in BIG])]
    pending, gathered = {}, {}

    def start_gather(name, token, after=None):
        items = dict(plan)[name]
        shards = [conv_w if l is None else wb[k][l] for k, l in items]
        lands = [landing(s, k == "w_in") for (k, _), s in zip(items, shards)]
        send, recv, s_thru, l_thru, token = push_start(shards, lands, False, name, token, after)
        pending[name] = (send, recv, s_thru, l_thru, items)
        return token

    token = start_gather("gather_b", start_gather("gather_a", None))

    names = [k for k in SMALL if k != "final_g"]
    shapes = [(DEPTH, NDEV, CW, BW // NDEV) if k == "conv_w" else W[k].shape for k in names] + [(D,), (1,)]

    def wpack(src):
        parts = [jnp.broadcast_to(src[k][:, None], shapes[i]) if k == "conv_w" else src[k] for i, k in enumerate(names)]
        return _pack(parts + [src["final_g"], jnp.ones((1,), f32)])

    packs = [wpack(W), wpack(M), wpack(V)]

    def weights(l, group, after):
        name = "gather_c" if l == 1 else ("gather_a" if group == "in" else "gather_b")
        if name in pending:
            send, recv, s_thru, l_thru, items = pending.pop(name)
            if name == "gather_a":
                after = (list(after) if isinstance(after, (list, tuple)) else [after]) + packs
            for item, arr in zip(items, push_wait(send, recv, s_thru, l_thru, after, False, name + "_wait")[1]):
                gathered[item] = arr
            if name == "gather_b":
                gathered["_token"] = start_gather("gather_c", None, after=gathered[items[0]])
        res = {k: gathered[(k, l)] for k in groups[group]}
        if group == "in":
            res["conv_w"] = gathered[("conv_w", None)]
        if "_token" in gathered:
            res["_token"] = gathered.pop("_token")
        return res

    big_g = {k: [None] * DEPTH for k in BIG}
    flights = []

    def start_exchange(items, name):
        parts = [big_g[k][l] for k, l in items]
        part = lambda p: p.shape[1:] if p.shape[0] == NDEV else (p.shape[1], p.shape[2] // 2)
        lands = [lax.empty((NDEV - 1,) + part(p), p.dtype) for p in parts]
        send, recv, s_thru, l_thru, tok = push_start(parts, lands, True, name, None)
        flights.append((send, recv, s_thru, l_thru, items, name))
        return tok

    small_flight = []

    def start_small(small_g, loss8, dfinal):
        sg_ = dict(small_g, norm1_g=[jnp.zeros((1, D), f32), small_g["norm1_g"][1]])
        gpack = _pack([jnp.stack([g.reshape(shapes[i][1:]) for g in sg_[k]]) for i, k in enumerate(names)]
                      + [dfinal, loss8[0, :1]])
        send, recv, s_thru, l_thru, tok = push_start([gpack], [landing(gpack)], False, "gather_small", None)
        small_flight.append((send, recv, s_thru, l_thru))
        return tok

    def on_grads(l, group, grads):
        if group == "small":
            return start_small(**grads)
        for k, g in grads.items():
            big_g[k][l] = g
        if l == 1 and group == "in":
            return start_exchange([(k, 1) for k in BIG], "exchange_l1")
        if l == 0:
            return start_exchange([(k, 0) for k in groups[group]], "exchange_l0_" + group)
        return None

    loss8, dx, dfinal, small_g = local_step(x, loss_target, W, weights, on_grads, token)

    landed, own = {}, {}
    for send, recv, s_thru, l_thru, items, name in flights:
        srcs, lands = push_wait(send, recv, s_thru, l_thru, dx, True, name + "_wait")
        for item, src, land in zip(items, srcs, lands):
            landed[item] = land
            own[item] = own_part(src)
    out = {}
    for k in BIG:
        items = [(k, l) for l in range(DEPTH)]
        out[k] = adam_big([landed[i] for i in items], [own[i] for i in items], W[k], M[k], V[k], "adam_" + k)

    send, recv, s_thru, l_thru = small_flight[0]
    gall = push_wait(send, recv, s_thru, l_thru, out[BIG[-1]][0], False, "gather_small_wait")[1][0]
    (late,) = allgather([small_g["norm1_g"][0].reshape(1, D // 128, 128)], "allgather_late")
    gall = lax.dynamic_update_slice(gall, late[0], (0, 0, 0))
    sg, sd, sm, sv = [_unpack(p, shapes) for p in adam_small(gall, *packs)]
    for i, k in enumerate(names + ["final_g"]):
        vals = [t[i] for t in (sg, sd, sm, sv)]
        if k == "conv_w":
            vals = [lax.dynamic_index_in_dim(t, me, axis=1, keepdims=False) for t in vals]
        out[k] = vals
    loss = sg[-1].reshape(())

    res = [loss, dx.reshape(bsz, seq, D)]
    for j in range(4):
        res += [out[k][j] for k in ORDER]
    return tuple(res)
```

```python
import functools
import math

import jax
import jax.numpy as jnp
from jax import lax
from jax.experimental import pallas as pl
from jax.experimental.pallas import tpu as pltpu

f32 = jnp.float32
bf16 = jnp.bfloat16

D = 1024
DEPTH = 2
EPS = 1e-6
BW = 512
NG = 32
GH = 16
NS = 64
NSTATE = NG * NS
HD = 64
NH = 8
PATTERNS = ((128, 1), (512, 4), (2048, 16))
ABLK = 128
ATT_SCALE = HD ** -0.5
CW = 31
DFF = 2816
INC = 7168
NDEV = 8
NSH_IN = INC // NDEV
NSH_FF = 2 * DFF // NDEV
ADAM_LR, ADAM_B1, ADAM_B2, ADAM_EPS, ADAM_WD, ADAM_STEP = 0.001, 0.9, 0.999, 1e-08, 0.01, 10

TB = 512
SJ = 4
SW = NSTATE // SJ
SU = BW // SJ
NEG = -1e30
MESH = pl.DeviceIdType.MESH
ANY = pl.BlockSpec(memory_space=pl.ANY)


def _cp(n_axes, vmem_mb=48):
    return pltpu.CompilerParams(dimension_semantics=("arbitrary",) * n_axes,
                                vmem_limit_bytes=vmem_mb * 1024 * 1024)


def _dot(a, b):
    return jnp.dot(a, b, preferred_element_type=f32)


def _dot_nt(a, b):
    return lax.dot_general(a, b, (((1,), (1,)), ((), ())), preferred_element_type=f32)


def _dot_tn(a, b):
    return lax.dot_general(a, b, (((0,), (0,)), ((), ())), preferred_element_type=f32)


def _dot_hi(a, b):
    return jnp.dot(a, b, precision=lax.Precision.HIGHEST, preferred_element_type=f32)


def _sigmoid(x):
    return 1.0 / (1.0 + jnp.exp(-x))


_GC = math.sqrt(2.0 / math.pi)


def _gelu(x):
    return 0.5 * x * (1.0 + jnp.tanh(_GC * (x + 0.044715 * x * x * x)))


def _gelu_grad(x):
    t = jnp.tanh(_GC * (x + 0.044715 * x * x * x))
    return 0.5 * (1.0 + t) + 0.5 * x * (1.0 - t * t) * _GC * (1.0 + 3.0 * 0.044715 * x * x)


def _rms_stats(x):
    return lax.rsqrt(jnp.mean(x * x, axis=-1, keepdims=True) + EPS)


def _rms_bwd(x, g, dh):
    r = _rms_stats(x)
    dyg = dh * g
    dx = r * dyg - x * (r * r * r) * jnp.mean(dyg * x, axis=-1, keepdims=True)
    dg = jnp.sum(dh * x * r, axis=0, keepdims=True)
    return dx, dg


def rms_fwd(x, g):
    n = x.shape[0]

    def body(x_ref, g_ref, h_ref):
        xv = x_ref[...]
        h_ref[...] = (xv * _rms_stats(xv) * g_ref[...]).astype(bf16)

    return pl.pallas_call(
        body, name="rms_fwd", grid=(n // TB,),
        in_specs=[pl.BlockSpec((TB, D), lambda i: (i, 0)), pl.BlockSpec((1, D), lambda i: (0, 0))],
        out_specs=pl.BlockSpec((TB, D), lambda i: (i, 0)),
        out_shape=jax.ShapeDtypeStruct((n, D), bf16), compiler_params=_cp(1))(x, g)


def inproj(h, w4, layer):
    n = h.shape[0]
    tm = 1024
    nblk, wblk = w4.shape[1], w4.shape[3]

    def body(h_ref, w_ref, o_ref):
        o_ref[...] = _dot(h_ref[...], w_ref[0, 0]).astype(bf16)

    return pl.pallas_call(
        body, name="inproj", grid=(nblk, n // tm),
        in_specs=[pl.BlockSpec((tm, D), lambda s, i: (i, 0)),
                  pl.BlockSpec((1, 1, D, wblk), lambda s, i: (layer, s, 0, 0))],
        out_specs=pl.BlockSpec((tm, wblk), lambda s, i: (i, s)),
        out_shape=jax.ShapeDtypeStruct((n, INC), bf16), compiler_params=_cp(2))(h, w4)


def inproj_bwd(dproj, w4, layer, x, g, dres):
    n = x.shape[0]
    tm = 1024
    nblk, wblk = w4.shape[1], w4.shape[3]

    def body(dp_ref, w_ref, x_ref, g_ref, dr_ref, dx_ref, dg_ref, acc):
        i, s = pl.program_id(0), pl.program_id(1)

        @pl.when(s == 0)
        def _():
            acc[...] = jnp.zeros_like(acc)

        @pl.when((s == 0) & (i == 0))
        def _():
            dg_ref[...] = jnp.zeros_like(dg_ref)

        acc[...] += _dot_nt(dp_ref[...], w_ref[0, 0])

        @pl.when(s == nblk - 1)
        def _():
            dx, dg = _rms_bwd(x_ref[...], g_ref[...], acc[...])
            dx_ref[...] = dr_ref[...] + dx
            dg_ref[...] += dg

    return pl.pallas_call(
        body, name="inproj_bwd", grid=(n // tm, nblk),
        in_specs=[pl.BlockSpec((tm, wblk), lambda i, s: (i, s)),
                  pl.BlockSpec((1, 1, D, wblk), lambda i, s: (layer, s, 0, 0)),
                  pl.BlockSpec((tm, D), lambda i, s: (i, 0)),
                  pl.BlockSpec((1, D), lambda i, s: (0, 0)),
                  pl.BlockSpec((tm, D), lambda i, s: (i, 0))],
        out_specs=[pl.BlockSpec((tm, D), lambda i, s: (i, 0)), pl.BlockSpec((1, D), lambda i, s: (0, 0))],
        out_shape=[jax.ShapeDtypeStruct((n, D), f32), jax.ShapeDtypeStruct((1, D), f32)],
        scratch_shapes=[pltpu.VMEM((tm, D), f32)], compiler_params=_cp(2))(dproj, w4, x, g, dres)


def mm_tn(a, b, a_spec, b_spec, n_sh, ka, nb, m, name):
    tm = 1024

    def body(a_ref, b_ref, o_ref, acc):
        i = pl.program_id(1)

        @pl.when(i == 0)
        def _():
            acc[...] = jnp.zeros_like(acc)

        av = a_ref[...].reshape(tm, ka).astype(bf16)
        bv = b_ref[...].reshape(tm, nb).astype(bf16)
        acc[...] += _dot_tn(av, bv)

        @pl.when(i == m // tm - 1)
        def _():
            o_ref[0] = acc[...].astype(bf16)

    return pl.pallas_call(
        body, name=name, grid=(n_sh, m // tm), in_specs=[a_spec, b_spec],
        out_specs=pl.BlockSpec((1, ka, nb), lambda s, i: (s, 0, 0)),
        out_shape=jax.ShapeDtypeStruct((n_sh, ka, nb), bf16),
        scratch_shapes=[pltpu.VMEM((ka, nb), f32)], compiler_params=_cp(2))(a, b)


def dw_mix(ysin, dz, ob, dya, hs, dyc, merged, dx1):
    n = ysin.shape[0]
    tm = 512
    pairs = ((BW, 2 * D), (BW, D), (BW, D), (D, D))

    def body(a0, b0, a1, b1, a2, b2, a3, b3, o0, o1, o2, o3, c0, c1, c2, c3):
        i = pl.program_id(0)
        accs = (c0, c1, c2, c3)

        @pl.when(i == 0)
        def _():
            for c in accs:
                c[...] = jnp.zeros_like(c)

        for a, b_, c in zip((a0, a1, a2, a3), (b0, b1, b2, b3), accs):
            c[...] += _dot_tn(a[...], b_[...].astype(bf16))

        @pl.when(i == n // tm - 1)
        def _():
            for s in range(NDEV):
                o0[s] = c0[:, s * 256:(s + 1) * 256].astype(bf16)
                o1[s] = c1[:, s * 128:(s + 1) * 128].astype(bf16)
                o2[s] = c2[:, s * 128:(s + 1) * 128].astype(bf16)
                o3[s] = c3[s * 128:(s + 1) * 128, :].astype(bf16)

    tok = lambda w: pl.BlockSpec((tm, w), lambda i: (i, 0))
    whole = lambda shape: pl.BlockSpec(shape, lambda i: (0, 0, 0))
    outs = [(NDEV, BW, 256), (NDEV, BW, 128), (NDEV, BW, 128), (NDEV, D // NDEV, D)]
    return pl.pallas_call(
        body, name="dw_mix", grid=(n // tm,),
        in_specs=[tok(w) for pair in pairs for w in pair],
        out_specs=[whole(s) for s in outs], out_shape=[jax.ShapeDtypeStruct(s, bf16) for s in outs],
        scratch_shapes=[pltpu.VMEM(p, f32) for p in pairs],
        compiler_params=_cp(1, 56))(ysin, dz, ob, dya, hs, dyc, merged, dx1)


def ffn_fwd(x1, g2, w1, w2, layer):
    n = x1.shape[0]
    w2p = w2.reshape(w2.shape[0], 4, NSH_FF, D)

    def body(x_ref, g_ref, wa_ref, wb_ref, w2_ref, o_ref, z1_ref, z2_ref, h_sc):
        s = pl.program_id(1)

        @pl.when(s == 0)
        def _():
            xv = x_ref[...]
            h_sc[...] = (xv * _rms_stats(xv) * g_ref[...]).astype(bf16)
            o_ref[...] = xv

        h = h_sc[...]
        z1 = _dot(h, wa_ref[0, 0])
        z2 = _dot(h, wb_ref[0, 0])
        z1_ref[0] = z1.astype(bf16)
        z2_ref[0] = z2.astype(bf16)
        a = (z1 * _sigmoid(z1) * z2).astype(bf16)
        o_ref[...] += _dot(a, w2_ref[0, 0])

    tb = 2 * TB
    sh3 = pl.BlockSpec((1, tb, NSH_FF), lambda i, s: (s, i, 0))
    return pl.pallas_call(
        body, name="ffn_fwd", grid=(n // tb, 4),
        in_specs=[pl.BlockSpec((tb, D), lambda i, s: (i, 0)),
                  pl.BlockSpec((1, D), lambda i, s: (0, 0)),
                  pl.BlockSpec((1, 1, D, NSH_FF), lambda i, s: (layer, s, 0, 0)),
                  pl.BlockSpec((1, 1, D, NSH_FF), lambda i, s: (layer, s + 4, 0, 0)),
                  pl.BlockSpec((1, 1, NSH_FF, D), lambda i, s: (layer, s, 0, 0))],
        out_specs=[pl.BlockSpec((tb, D), lambda i, s: (i, 0)), sh3, sh3, pl.BlockSpec((tb, D), lambda i, s: (i, 0))],
        out_shape=[jax.ShapeDtypeStruct((n, D), f32), jax.ShapeDtypeStruct((4, n, NSH_FF), bf16),
                   jax.ShapeDtypeStruct((4, n, NSH_FF), bf16), jax.ShapeDtypeStruct((n, D), bf16)],
        compiler_params=_cp(2))(x1, g2, w1, w1, w2p)


def ffn_bwd(h, dx2, z1s, z2s, w1, w2, layer):
    n = h.shape[0]
    w2p = w2.reshape(w2.shape[0], 4, NSH_FF, D)
    nblk = n // TB

    def body(h_ref, dy_ref, z1_ref, z2_ref, wa_ref, wb_ref, w2_ref, dh_ref, dwa_ref, dwb_ref, dw2_ref,
             acc_a, acc_b, acc_2):
        i = pl.program_id(1)

        @pl.when(i == 0)
        def _():
            acc_a[...] = jnp.zeros_like(acc_a)
            acc_b[...] = jnp.zeros_like(acc_b)
            acc_2[...] = jnp.zeros_like(acc_2)

        hv = h_ref[...]
        dyb = dy_ref[...].astype(bf16)
        z1 = z1_ref[0].astype(f32)
        z2 = z2_ref[0].astype(f32)
        sg = _sigmoid(z1)
        sl = z1 * sg
        a = (sl * z2).astype(bf16)
        da = _dot_nt(dyb, w2_ref[0, 0])
        dz2 = (da * sl).astype(bf16)
        dz1 = (da * z2 * sg * (1.0 + z1 * (1.0 - sg))).astype(bf16)
        dh_ref[0] = (_dot_nt(dz1, wa_ref[0, 0]) + _dot_nt(dz2, wb_ref[0, 0])).astype(bf16)
        acc_a[...] += _dot_tn(hv, dz1)
        acc_b[...] += _dot_tn(hv, dz2)
        acc_2[...] += _dot_tn(a, dyb)

        @pl.when(i == nblk - 1)
        def _():
            dwa_ref[0] = acc_a[...].astype(bf16)
            dwb_ref[0] = acc_b[...].astype(bf16)
            dw2_ref[0] = acc_2[...].astype(bf16)

    tok = pl.BlockSpec((TB, D), lambda s, i: (i, 0))
    sh3 = pl.BlockSpec((1, TB, NSH_FF), lambda s, i: (s, i, 0))
    return pl.pallas_call(
        body, name="ffn_bwd", grid=(4, nblk),
        in_specs=[tok, tok, sh3, sh3,
                  pl.BlockSpec((1, 1, D, NSH_FF), lambda s, i: (layer, s, 0, 0)),
                  pl.BlockSpec((1, 1, D, NSH_FF), lambda s, i: (layer, s + 4, 0, 0)),
                  pl.BlockSpec((1, 1, NSH_FF, D), lambda s, i: (layer, s, 0, 0))],
        out_specs=[pl.BlockSpec((1, TB, D), lambda s, i: (s, i, 0)),
                   pl.BlockSpec((1, D, NSH_FF), lambda s, i: (s, 0, 0)),
                   pl.BlockSpec((1, D, NSH_FF), lambda s, i: (s, 0, 0)),
                   pl.BlockSpec((1, NSH_FF, D), lambda s, i: (s, 0, 0))],
        out_shape=[jax.ShapeDtypeStruct((4, n, D), bf16), jax.ShapeDtypeStruct((4, D, NSH_FF), bf16),
                   jax.ShapeDtypeStruct((4, D, NSH_FF), bf16), jax.ShapeDtypeStruct((4, NSH_FF, D), bf16)],
        scratch_shapes=[pltpu.VMEM((D, NSH_FF), f32), pltpu.VMEM((D, NSH_FF), f32), pltpu.VMEM((NSH_FF, D), f32)],
        compiler_params=_cp(2, 56))(h, dx2, z1s, z2s, w1, w1, w2p)


def norm_bwd_fin(x, g, dres, dh_parts, name):
    n = x.shape[0]
    nparts = dh_parts.shape[0]

    def body(x_ref, g_ref, dy_ref, dh_ref, dx_ref, dg_ref):
        i = pl.program_id(0)

        @pl.when(i == 0)
        def _():
            dg_ref[...] = jnp.zeros_like(dg_ref)

        dh = dh_ref[0].astype(f32)
        for s in range(1, nparts):
            dh = dh + dh_ref[s].astype(f32)
        dx, dg = _rms_bwd(x_ref[...], g_ref[...], dh)
        dx_ref[...] = dy_ref[...] + dx
        dg_ref[...] += dg

    tok = pl.BlockSpec((TB, D), lambda i: (i, 0))
    return pl.pallas_call(
        body, name=name, grid=(n // TB,),
        in_specs=[tok, pl.BlockSpec((1, D), lambda i: (0, 0)), tok, pl.BlockSpec((nparts, TB, D), lambda i: (0, i, 0))],
        out_specs=[tok, pl.BlockSpec((1, D), lambda i: (0, 0))],
        out_shape=[jax.ShapeDtypeStruct((n, D), f32), jax.ShapeDtypeStruct((1, D), f32)],
        compiler_params=_cp(1))(x, g, dres, dh_parts)


def loss_head(x, g, target):
    n = x.shape[0]

    def body(x_ref, g_ref, t_ref, l_ref, dx_ref, dg_ref):
        i = pl.program_id(0)

        @pl.when(i == 0)
        def _():
            l_ref[...] = jnp.zeros_like(l_ref)
            dg_ref[...] = jnp.zeros_like(dg_ref)

        xv = x_ref[...]
        y = xv * _rms_stats(xv) * g_ref[...]
        e = y - t_ref[...]
        l_ref[...] += 0.5 * jnp.sum(jnp.sum(e * e, axis=-1, keepdims=True), axis=0, keepdims=True) * (1.0 / D)
        dx, dg = _rms_bwd(xv, g_ref[...], e * (1.0 / D))
        dx_ref[...] = dx
        dg_ref[...] += dg

    tok = lambda i: (i, 0)
    return pl.pallas_call(
        body, name="loss_head", grid=(n // TB,),
        in_specs=[pl.BlockSpec((TB, D), tok), pl.BlockSpec((1, D), lambda i: (0, 0)), pl.BlockSpec((TB, D), tok)],
        out_specs=[pl.BlockSpec((8, 128), lambda i: (0, 0)), pl.BlockSpec((TB, D), tok),
                   pl.BlockSpec((1, D), lambda i: (0, 0))],
        out_shape=[jax.ShapeDtypeStruct((8, 128), f32), jax.ShapeDtypeStruct((n, D), f32),
                   jax.ShapeDtypeStruct((1, D), f32)],
        compiler_params=_cp(1))(x, g, target)


def _disc(lr, li, ld):
    dt = jnp.exp(ld)
    mag = jnp.exp(lr * dt)
    ar = mag * jnp.cos(li * dt)
    ai = mag * jnp.sin(li * dt)
    nr, ni = ar - 1.0, ai
    den = lr * lr + li * li
    zr = (nr * lr + ni * li) / den
    zi = (ni * lr - nr * li) / den
    return ar, ai, zr, zi


def _blockdiag_mask(shape):
    r = lax.broadcasted_iota(jnp.int32, shape, 0) // GH
    c = lax.broadcasted_iota(jnp.int32, shape, 1) // NS
    return r == c


def s5_params(lr, li, ld, btr, bti, ctr, cti):
    def body(lr_ref, li_ref, ld_ref, btr_ref, bti_ref, ctr_ref, cti_ref, t8_ref, bb_ref, cb_ref):
        ar, ai, zr, zi = _disc(lr_ref[...], li_ref[...], ld_ref[...])
        pr_, pi_ = ar, ai
        pw2 = []
        for k in range(4):
            pw2.append((pr_, pi_))
            pr_, pi_ = pr_ * pr_ - pi_ * pi_, 2.0 * pr_ * pi_
        cm = lambda p, q: (p[0] * q[0] - p[1] * q[1], p[0] * q[1] + p[1] * q[0])
        pw = {1: pw2[0], 2: pw2[1], 4: pw2[2], 8: pw2[3]}
        pw[3], pw[5], pw[6] = cm(pw[2], pw[1]), cm(pw[4], pw[1]), cm(pw[4], pw[2])
        pw[7] = cm(pw[4], pw[3])
        row = lax.broadcasted_iota(jnp.int32, (8, NSTATE), 0)
        zero = jnp.zeros((8, NSTATE), f32)
        for c in range(2):
            for k in range(3):
                full = jnp.broadcast_to(pw2[k][c], (8, NSTATE))
                t8_ref[c, k] = jnp.where(row >= (1 << k), full, 0.0)
                t8_ref[c, 3 + k] = jnp.where(row + (1 << k) < 8, full, 0.0)
            up, down = zero, zero
            for j in range(8):
                up = up + jnp.where(row == j, pw[j + 1][c], 0.0)
                down = down + jnp.where(row == j, pw[8 - j][c], 0.0)
            t8_ref[c, 6] = up
            t8_ref[c, 7] = down
        bbr = zr * btr_ref[...] - zi * bti_ref[...]
        bbi = zr * bti_ref[...] + zi * btr_ref[...]
        mask = _blockdiag_mask((SU, SW))
        for j in range(SJ):
            cols = slice(j * SW, (j + 1) * SW)
            for c, (vb, vc) in enumerate(((bbr, ctr_ref[...]), (bbi, cti_ref[...]))):
                bb_ref[c, j] = jnp.where(mask, jnp.tile(vb[:, cols], (SU // GH, 1)), 0.0).astype(bf16)
                cb_ref[c, j] = jnp.where(mask, jnp.tile(vc[:, cols], (SU // GH, 1)), 0.0).astype(bf16)

    return pl.pallas_call(
        body, name="s5_params",
        out_shape=[jax.ShapeDtypeStruct((2, 8, 8, NSTATE), f32),
                   jax.ShapeDtypeStruct((2, SJ, SU, SW), bf16), jax.ShapeDtypeStruct((2, SJ, SU, SW), bf16)],
        compiler_params=pltpu.CompilerParams(vmem_limit_bytes=56 * 1024 * 1024))(lr, li, ld, btr, bti, ctr, cti)


def s5_params_bwd(lr, li, ld, btr, bti, d_a, d_bb, d_cb):
    def body(lr_ref, li_ref, ld_ref, btr_ref, bti_ref, da_ref, dbb_ref, dcb_ref,
             dlr_ref, dli_ref, dld_ref, dbt_ref, dct_ref):
        mask = _blockdiag_mask((SU, SW))

        def fold(ref, c):
            parts = []
            for j in range(SJ):
                v = jnp.where(mask, ref[c, j], 0.0)
                parts.append(v.reshape(SU // GH, GH, SW).sum(axis=0))
            return jnp.concatenate(parts, axis=1)

        dct_ref[0] = fold(dcb_ref, 0)
        dct_ref[1] = fold(dcb_ref, 1)
        dbbr, dbbi = fold(dbb_ref, 0), fold(dbb_ref, 1)
        lrv, liv, ldv = lr_ref[...], li_ref[...], ld_ref[...]
        (ar, ai, zr, zi), vjp = jax.vjp(_disc, lrv, liv, ldv)
        btr, bti = btr_ref[...], bti_ref[...]
        dbt_ref[0] = zr * dbbr + zi * dbbi
        dbt_ref[1] = zr * dbbi - zi * dbbr
        dzr = jnp.sum(dbbr * btr + dbbi * bti, axis=0, keepdims=True)
        dzi = jnp.sum(dbbi * btr - dbbr * bti, axis=0, keepdims=True)
        dlr, dli, dld = vjp((da_ref[0:1, :], da_ref[1:2, :], dzr, dzi))
        dlr_ref[...] = dlr
        dli_ref[...] = dli
        ind = (lax.broadcasted_iota(jnp.int32, (NSTATE, 128), 0) // NS
               == lax.broadcasted_iota(jnp.int32, (NSTATE, 128), 1)).astype(f32)
        dld_ref[...] = _dot_hi(jnp.broadcast_to(dld, (8, NSTATE)), ind)

    return pl.pallas_call(
        body, name="s5_params_bwd",
        out_shape=[jax.ShapeDtypeStruct((1, NSTATE), f32), jax.ShapeDtypeStruct((1, NSTATE), f32),
                   jax.ShapeDtypeStruct((8, 128), f32), jax.ShapeDtypeStruct((2, GH, NSTATE), f32),
                   jax.ShapeDtypeStruct((2, GH, NSTATE), f32)],
        compiler_params=pltpu.CompilerParams(vmem_limit_bytes=56 * 1024 * 1024))(lr, li, ld, btr, bti, d_a, d_bb, d_cb)


def _fma(sr, si, ar, ai, qr, qi):
    return sr + ar * qr - ai * qi, si + ar * qi + ai * qr


def _scan_tile(sr, si, cr, ci, t8_ref, reverse):
    sg = -1.0 if reverse else 1.0
    for k in range(3):
        tk = 3 + k if reverse else k
        rot = 8 - (1 << k) if reverse else 1 << k
        sr, si = _fma(sr, si, t8_ref[0, tk], sg * t8_ref[1, tk], pltpu.roll(sr, rot, 0), pltpu.roll(si, rot, 0))
    tp = 7 if reverse else 6
    sr, si = _fma(sr, si, t8_ref[0, tp], sg * t8_ref[1, tp], cr, ci)
    e = 0 if reverse else 7
    return sr, si, jnp.broadcast_to(sr[e:e + 1, :], sr.shape), jnp.broadcast_to(si[e:e + 1, :], si.shape)


S5MC = 512


def _s5_input_map(u_ref, bb_ref, sr_sc, si_sc, l):
    for c in range(l // S5MC):
        rows = slice(c * S5MC, (c + 1) * S5MC)
        u = u_ref[0, rows, :]
        sr_sc[rows, :] = _dot(u, bb_ref[0, 0])
        si_sc[rows, :] = _dot(u, bb_ref[1, 0])


def _s5_forward_scan(sr_sc, si_sc, t8_ref, l):
    def step(k, carry):
        rows = pl.ds(pl.multiple_of(k * 8, 8), 8)
        sr, si, cr, ci = _scan_tile(sr_sc[rows, :], si_sc[rows, :], carry[0], carry[1], t8_ref, False)
        sr_sc[rows, :] = sr
        si_sc[rows, :] = si
        return cr, ci

    zero = jnp.zeros((8, SW), f32)
    lax.fori_loop(0, l // 8, step, (zero, zero), unroll=4)


def s5_fwd(proj3, t8, bb, cb, dskip):
    b, l, _ = proj3.shape

    def body(u_ref, t8_ref, bb_ref, cb_ref, d_ref, y_ref, sr_out, si_out):
        sr_sc, si_sc = sr_out.at[0], si_out.at[0]
        _s5_input_map(u_ref, bb_ref, sr_sc, si_sc, l)
        _s5_forward_scan(sr_sc, si_sc, t8_ref, l)
        for c in range(l // S5MC):
            rows = slice(c * S5MC, (c + 1) * S5MC)
            y = (_dot_nt(sr_sc[rows, :].astype(bf16), cb_ref[0, 0])
                 - _dot_nt(si_sc[rows, :].astype(bf16), cb_ref[1, 0]))
            y_ref[0, rows, :] = y + d_ref[...] * u_ref[0, rows, :].astype(f32)

    return pl.pallas_call(
        body, name="s5_fwd", grid=(SJ, b),
        in_specs=[pl.BlockSpec((1, l, SU), lambda j, bi: (bi, 0, j)),
                  pl.BlockSpec((2, 8, 8, SW), lambda j, bi: (0, 0, 0, j)),
                  pl.BlockSpec((2, 1, SU, SW), lambda j, bi: (0, j, 0, 0)),
                  pl.BlockSpec((2, 1, SU, SW), lambda j, bi: (0, j, 0, 0)),
                  pl.BlockSpec((1, SU), lambda j, bi: (0, j))],
        out_specs=[pl.BlockSpec((1, l, SU), lambda j, bi: (bi, 0, j)),
                   pl.BlockSpec((1, l, SW), lambda j, bi: (bi, 0, j)),
                   pl.BlockSpec((1, l, SW), lambda j, bi: (bi, 0, j))],
        out_shape=[jax.ShapeDtypeStruct((b, l, BW), f32), jax.ShapeDtypeStruct((b, l, NSTATE), f32),
                   jax.ShapeDtypeStruct((b, l, NSTATE), f32)],
        compiler_params=_cp(2))(proj3, t8, bb, cb, dskip)


def s5_bwd(proj3, dy, s_re, s_im, t8, bb, cb, dskip):
    b, l, _ = proj3.shape
    nt = l // 8

    def body(u_ref, dy_ref, sr_in, si_in, t8_ref, bb_ref, cb_ref, d_ref,
             du_ref, da_ref, dbb_ref, dcb_ref, dd_ref, gr_sc, gi_sc):
        bi = pl.program_id(1)
        sr_sc, si_sc = sr_in.at[0], si_in.at[0]

        @pl.when(bi == 0)
        def _():
            da_ref[...] = jnp.zeros_like(da_ref)
            dbb_ref[...] = jnp.zeros_like(dbb_ref)
            dcb_ref[...] = jnp.zeros_like(dcb_ref)
            dd_ref[...] = jnp.zeros_like(dd_ref)

        for c in range(l // S5MC):
            rows = slice(c * S5MC, (c + 1) * S5MC)
            dyb = dy_ref[0, rows, :].astype(bf16)
            gr_sc[rows, :] = _dot(dyb, cb_ref[0, 0])
            gi_sc[rows, :] = -_dot(dyb, cb_ref[1, 0])

        row = lax.broadcasted_iota(jnp.int32, (8, SW), 0)

        def step(i, carry):
            cr, ci, dar, dai = carry
            k = nt - 1 - i
            rows = pl.ds(pl.multiple_of(k * 8, 8), 8)
            gr, gi, cr, ci = _scan_tile(gr_sc[rows, :], gi_sc[rows, :], cr, ci, t8_ref, True)
            gr_sc[rows, :] = gr
            gi_sc[rows, :] = gi
            before = pl.ds(pl.multiple_of(jnp.maximum(k - 1, 0) * 8, 8), 8)
            live = jnp.where(k > 0, 1.0, 0.0)
            sr, si = sr_sc[rows, :], si_sc[rows, :]
            spr = jnp.where(row == 0, live * sr_sc[before, :][7:8, :], pltpu.roll(sr, 1, 0))
            spi = jnp.where(row == 0, live * si_sc[before, :][7:8, :], pltpu.roll(si, 1, 0))
            return cr, ci, dar + spr * gr + spi * gi, dai + spr * gi - spi * gr

        zero = jnp.zeros((8, SW), f32)
        _, _, dar, dai = lax.fori_loop(0, nt, step, (zero, zero, zero, zero), unroll=2)
        da_ref[0:1, :] += jnp.sum(dar, axis=0, keepdims=True)
        da_ref[1:2, :] += jnp.sum(dai, axis=0, keepdims=True)

        for c in range(l // S5MC):
            rows = slice(c * S5MC, (c + 1) * S5MC)
            u = u_ref[0, rows, :]
            dyv = dy_ref[0, rows, :]
            dyb = dyv.astype(bf16)
            grb, gib = gr_sc[rows, :].astype(bf16), gi_sc[rows, :].astype(bf16)
            dcb_ref[0, 0] += _dot_tn(dyb, sr_sc[rows, :].astype(bf16))
            dcb_ref[1, 0] -= _dot_tn(dyb, si_sc[rows, :].astype(bf16))
            dbb_ref[0, 0] += _dot_tn(u, grb)
            dbb_ref[1, 0] += _dot_tn(u, gib)
            du = _dot_nt(grb, bb_ref[0, 0]) + _dot_nt(gib, bb_ref[1, 0]) + d_ref[...] * dyv
            du_ref[0, rows, :] = du.astype(bf16)
            dd_ref[...] += jnp.sum(dyv * u.astype(f32), axis=0, keepdims=True)

    seq = pl.BlockSpec((1, l, SU), lambda j, bi: (bi, 0, j))
    sts = pl.BlockSpec((1, l, SW), lambda j, bi: (bi, 0, j))
    tab = pl.BlockSpec((2, 1, SU, SW), lambda j, bi: (0, j, 0, 0))
    return pl.pallas_call(
        body, name="s5_bwd", grid=(SJ, b),
        in_specs=[seq, seq, sts, sts, pl.BlockSpec((2, 8, 8, SW), lambda j, bi: (0, 0, 0, j)), tab, tab,
                  pl.BlockSpec((1, SU), lambda j, bi: (0, j))],
        out_specs=[seq, pl.BlockSpec((2, SW), lambda j, bi: (0, j)), tab, tab,
                   pl.BlockSpec((1, SU), lambda j, bi: (0, j))],
        out_shape=[jax.ShapeDtypeStruct((b, l, BW), bf16), jax.ShapeDtypeStruct((2, NSTATE), f32),
                   jax.ShapeDtypeStruct((2, SJ, SU, SW), f32), jax.ShapeDtypeStruct((2, SJ, SU, SW), f32),
                   jax.ShapeDtypeStruct((1, BW), f32)],
        scratch_shapes=[pltpu.VMEM((l, SW), f32)] * 2,
        compiler_params=_cp(2))(proj3, dy, s_re, s_im, t8, bb, cb, dskip)


AHC = 2
AHW = AHC * 128


def _att_mask(n, nb):
    if nb == 1:
        qi = lax.broadcasted_iota(jnp.int32, (ABLK, ABLK), 0)
        kj = lax.broadcasted_iota(jnp.int32, (ABLK, ABLK), 1)
        return kj <= qi
    qi = lax.broadcasted_iota(jnp.int32, (ABLK, 2 * ABLK), 0)
    kj = lax.broadcasted_iota(jnp.int32, (ABLK, 2 * ABLK), 1)
    return (kj >= qi) & (kj <= qi + ABLK) & ((n > 0) | (kj >= ABLK))


def _att_rows(it, nb, dil):
    r, n = it // nb, it % nb
    cur = pl.ds(r + n * (ABLK * dil), ABLK, stride=dil)
    prv = pl.ds(r + jnp.maximum(n - 1, 0) * (ABLK * dil), ABLK, stride=dil)
    return n, cur, prv


def _att_keys(ref, c, cur, prv, nb):
    if nb == 1:
        x = ref[c, cur, :].astype(bf16)
    else:
        x = jnp.concatenate([ref[c, prv, :], ref[c, cur, :]], axis=0).astype(bf16)
    head0 = lax.broadcasted_iota(jnp.int32, x.shape, 1) < HD
    zero = jnp.zeros_like(x)
    return jnp.concatenate([jnp.where(head0, x, zero), jnp.where(head0, zero, x)], axis=0)


def _per_head(nk, a0, a1):
    col = lax.broadcasted_iota(jnp.int32, (ABLK, 2 * nk), 1)
    return jnp.where(col < nk, a0, a1)


def _to_chunks(src_ref, dst):
    for c in range(AHC):
        dst[c] = src_ref[0, :, c * 128:(c + 1) * 128].astype(f32)


def att_fwd(proj3, g_idx, dil):
    b, l, _ = proj3.shape
    nb = l // dil // ABLK
    nhalf = BW // AHW

    def body(q_ref, k_ref, v_ref, o_ref, lse_ref, qf, kf, vf, of):
        hh = pl.program_id(1)
        _to_chunks(q_ref, qf)
        _to_chunks(k_ref, kf)
        _to_chunks(v_ref, vf)
        lane = lax.broadcasted_iota(jnp.int32, (ABLK, 128), 1)

        def step(it, carry):
            n, cur, prv = _att_rows(it, nb, dil)
            valid = _att_mask(n, nb)
            nk = valid.shape[1]
            lse_all = jnp.zeros((ABLK, 128), f32)
            for c in range(AHC):
                q = (qf[c, cur, :] * ATT_SCALE).astype(bf16)
                k = _att_keys(kf, c, cur, prv, nb)
                v = _att_keys(vf, c, cur, prv, nb)
                head = hh * (2 * AHC) + 2 * c
                if nb == 1:
                    v2 = jnp.concatenate([valid, valid], axis=1)
                    s = jnp.where(v2, _dot_nt(q, k), NEG)
                    m0 = jnp.max(s[:, :nk], axis=-1, keepdims=True)
                    m1 = jnp.max(s[:, nk:], axis=-1, keepdims=True)
                    p = jnp.exp(s - _per_head(nk, m0, m1))
                    den0 = jnp.sum(p[:, :nk], axis=-1, keepdims=True)
                    den1 = jnp.sum(p[:, nk:], axis=-1, keepdims=True)
                    of[c, cur, :] = _dot(p.astype(bf16), v) * jnp.where(lane < HD, 1.0 / den0, 1.0 / den1)
                    lse_all = (lse_all + jnp.where(lane == head, m0 + jnp.log(den0), 0.0)
                               + jnp.where(lane == head + 1, m1 + jnp.log(den1), 0.0))
                    continue
                acc = None
                for hl in range(2):
                    s = jnp.where(valid, _dot_nt(q, k[hl * nk:(hl + 1) * nk]), NEG)
                    m = jnp.max(s, axis=-1, keepdims=True)
                    p = jnp.exp(s - m)
                    den = jnp.sum(p, axis=-1, keepdims=True)
                    o_h = _dot(p.astype(bf16), v[hl * nk:(hl + 1) * nk]) * (1.0 / den)
                    acc = o_h if acc is None else acc + o_h
                    lse_all = lse_all + jnp.where(lane == head + hl, m + jnp.log(den), 0.0)
                of[c, cur, :] = acc

            lse_ref[0, 0, cur, :] = lse_all
            return carry

        lax.fori_loop(0, dil * nb, step, 0, unroll=4)
        for c in range(AHC):
            o_ref[0, :, c * 128:(c + 1) * 128] = of[c].astype(bf16)

    col = lambda c: pl.BlockSpec((1, l, AHW), lambda bi, hh: (bi, 0, c * nhalf + hh))
    return pl.pallas_call(
        body, name=f"att_fwd{g_idx}", grid=(b, nhalf),
        in_specs=[col(1 + g_idx), col(4), col(5)],
        out_specs=[pl.BlockSpec((1, l, AHW), lambda bi, hh: (bi, 0, hh)),
                   pl.BlockSpec((1, 1, l, 128), lambda bi, hh: (bi, hh, 0, 0))],
        out_shape=[jax.ShapeDtypeStruct((b, l, BW), bf16), jax.ShapeDtypeStruct((b, nhalf, l, 128), f32)],
        scratch_shapes=[pltpu.VMEM((AHC, l, 128), f32)] * 4,
        compiler_params=_cp(2))(proj3, proj3, proj3)


def att_bwd(proj3, do, lse_tot, delta, g_idx, dil):
    b, l, _ = proj3.shape
    nb = l // dil // ABLK
    nhalf = BW // AHW

    def body(q_ref, k_ref, v_ref, do_ref, l_ref, dl_ref, dq_out, dk_out, dv_out, qf, kf, vf, dof,
             dq_ref, dk_ref, dv_ref):
        hh = pl.program_id(1)
        _to_chunks(q_ref, qf)
        _to_chunks(k_ref, kf)
        _to_chunks(v_ref, vf)
        _to_chunks(do_ref, dof)
        dk_ref[...] = jnp.zeros_like(dk_ref)
        dv_ref[...] = jnp.zeros_like(dv_ref)
        lane = lax.broadcasted_iota(jnp.int32, (ABLK, 128), 1)

        def step(it, carry):
            n, cur, prv = _att_rows(it, nb, dil)
            valid = _att_mask(n, nb)
            valid = jnp.concatenate([valid, valid], axis=1)
            nk = valid.shape[1] // 2
            lse_b = l_ref[0, cur, :]
            dl_b = dl_ref[0, cur, :]
            head0 = lax.broadcasted_iota(jnp.int32, (nk, 128), 1) < HD
            for c in range(AHC):
                q = (qf[c, cur, :] * ATT_SCALE).astype(bf16)
                dob = dof[c, cur, :].astype(bf16)
                k = _att_keys(kf, c, cur, prv, nb)
                v = _att_keys(vf, c, cur, prv, nb)
                head = hh * (2 * AHC) + 2 * c
                pick = lambda a, h: jnp.sum(jnp.where(lane == h, a, 0.0), axis=-1, keepdims=True)
                lse_h = _per_head(nk, pick(lse_b, head), pick(lse_b, head + 1))
                dl_h = _per_head(nk, pick(dl_b, head), pick(dl_b, head + 1))
                s = _dot_nt(q, k)
                p = jnp.where(valid, jnp.exp(jnp.minimum(s - lse_h, 60.0)), 0.0)
                ds = (p * (_dot_nt(dob, v) - dl_h)).astype(bf16)
                dq_ref[0, c, cur, :] = _dot(ds, k) * ATT_SCALE
                dk2 = _dot_tn(ds, q)
                dv2 = _dot_tn(p.astype(bf16), dob)
                dk = jnp.where(head0, dk2[:nk], dk2[nk:])
                dv = jnp.where(head0, dv2[:nk], dv2[nk:])
                if nb == 1:
                    dk_ref[0, c, cur, :] += dk
                    dv_ref[0, c, cur, :] += dv
                else:
                    dk_ref[0, c, cur, :] += dk[ABLK:]
                    dv_ref[0, c, cur, :] += dv[ABLK:]
                    dk_ref[0, c, prv, :] += dk[:ABLK]
                    dv_ref[0, c, prv, :] += dv[:ABLK]

            return carry

        lax.fori_loop(0, dil * nb, step, 0, unroll=4)
        dq_out[0] = dq_ref[0].astype(bf16)
        dk_out[0] = dk_ref[0].astype(bf16)
        dv_out[0] = dv_ref[0].astype(bf16)

    col = lambda c: pl.BlockSpec((1, l, AHW), lambda bi, hh: (bi, 0, c * nhalf + hh))
    own = pl.BlockSpec((1, l, AHW), lambda bi, hh: (bi, 0, hh))
    own128 = pl.BlockSpec((1, l, 128), lambda bi, hh: (bi, 0, 0))
    chunked = pl.BlockSpec((1, AHC, l, 128), lambda bi, hh: (bi, hh, 0, 0))
    return pl.pallas_call(
        body, name=f"att_bwd{g_idx}", grid=(b, nhalf),
        in_specs=[col(1 + g_idx), col(4), col(5), own, own128, own128],
        out_specs=[chunked] * 3,
        out_shape=[jax.ShapeDtypeStruct((b, BW // 128, l, 128), bf16)] * 3,
        scratch_shapes=[pltpu.VMEM((AHC, l, 128), f32)] * 4 + [pltpu.VMEM((1, AHC, l, 128), f32)] * 3,
        compiler_params=_cp(2, 56))(proj3, proj3, proj3, do, lse_tot, delta)


CPAD = 32
CTAIL = 16
CR = 128
CSLAB = CR + 40


def _tap_windows(slab, off, mis):
    ntap = (CW - 1 - mis) // 8 + 1
    rot = (off + mis) % 8
    base = off + mis - rot
    shifted = pltpu.roll(slab, CSLAB - rot, 0) if rot else slab
    for a in range(ntap):
        yield 8 * a + mis, shifted[base + 8 * a:base + 8 * a + CR]


def _fill_glu(cv_ref, pad, l):
    pad[0:CPAD, :] = jnp.zeros((CPAD, BW), f32)
    pad[CPAD:CPAD + l, :] = cv_ref[0, :, :BW].astype(f32) * _sigmoid(cv_ref[0, :, BW:].astype(f32))
    pad[CPAD + l:, :] = jnp.zeros((CTAIL, BW), f32)


def conv_fwd(proj3, cw, cb):
    b, l, _ = proj3.shape

    def body(cv_ref, w_ref, b_ref, o_ref, pad):
        _fill_glu(cv_ref, pad, l)
        for lc in range(BW // 128):
            lanes = slice(lc * 128, (lc + 1) * 128)
            wv = w_ref[:, lanes]

            def step(c, carry):
                base = pl.multiple_of(c * CR, CR)
                slab = pad[pl.ds(base, CSLAB), lanes]
                acc = jnp.zeros((CR, 128), f32) + b_ref[:, lanes]
                for mis in range(8):
                    for k, win in _tap_windows(slab, CPAD - (CW - 1), mis):
                        acc = acc + wv[k:k + 1] * win
                o_ref[0, pl.ds(base, CR), lanes] = acc
                return carry

            lax.fori_loop(0, l // CR, step, 0)

    return pl.pallas_call(
        body, name="conv_fwd", grid=(b,),
        in_specs=[pl.BlockSpec((1, l, 2 * BW), lambda i: (i, 0, 3)),
                  pl.BlockSpec((32, BW), lambda i: (0, 0)), pl.BlockSpec((1, BW), lambda i: (0, 0))],
        out_specs=pl.BlockSpec((1, l, BW), lambda i: (i, 0, 0)),
        out_shape=jax.ShapeDtypeStruct((b, l, BW), f32),
        scratch_shapes=[pltpu.VMEM((CPAD + l + CTAIL, BW), f32)], compiler_params=_cp(1))(proj3, cw, cb)


def conv_bwd(proj3, dhc, cw):
    b, l, _ = proj3.shape

    def body(cv_ref, d_ref, w_ref, dcv_ref, dw_ref, db_ref, pad, dpad):
        i = pl.program_id(0)

        @pl.when(i == 0)
        def _():
            dw_ref[...] = jnp.zeros_like(dw_ref)
            db_ref[...] = jnp.zeros_like(db_ref)

        _fill_glu(cv_ref, pad, l)
        dpad[0:l, :] = d_ref[0]
        dpad[l:, :] = jnp.zeros((CPAD + CTAIL, BW), f32)
        db_ref[...] += jnp.sum(d_ref[0], axis=0, keepdims=True)
        for lc in range(BW // 128):
            lanes = slice(lc * 128, (lc + 1) * 128)
            glanes = slice(BW + lc * 128, BW + (lc + 1) * 128)
            wv = w_ref[:, lanes]

            for mis in range(8):
                ntap = (CW - 1 - mis) // 8 + 1

                def dw_step(c, accs, mis=mis, lanes=lanes):
                    base = pl.multiple_of(c * CR, CR)
                    slab = pad[pl.ds(base, CSLAB), lanes]
                    dv = dpad[pl.ds(base, CR), lanes]
                    return tuple(acc + (dv * win).reshape(CR // 8, 8, 128).sum(axis=0) for acc, (_, win)
                                 in zip(accs, _tap_windows(slab, CPAD - (CW - 1), mis)))

                accs = lax.fori_loop(0, l // CR, dw_step, tuple(jnp.zeros((8, 128), f32) for _ in range(ntap)))
                for a in range(ntap):
                    k = 8 * a + mis
                    dw_ref[k:k + 1, lanes] += jnp.sum(accs[a], axis=0, keepdims=True)

            def dh_step(c, carry, lanes=lanes, glanes=glanes, wv=wv):
                base = pl.multiple_of(c * CR, CR)
                slab = dpad[pl.ds(base, CSLAB), lanes]
                acc = jnp.zeros((CR, 128), f32)
                for mis in range(8):
                    for kk, win in _tap_windows(slab, 0, mis):
                        acc = acc + wv[CW - 1 - kk:CW - kk] * win
                rows = pl.ds(base, CR)
                a = cv_ref[0, rows, lanes].astype(f32)
                sg = _sigmoid(cv_ref[0, rows, glanes].astype(f32))
                dcv_ref[0, rows, lanes] = (acc * sg).astype(bf16)
                dcv_ref[0, rows, glanes] = (acc * a * sg * (1.0 - sg)).astype(bf16)
                return carry

            lax.fori_loop(0, l // CR, dh_step, 0)

    return pl.pallas_call(
        body, name="conv_bwd", grid=(b,),
        in_specs=[pl.BlockSpec((1, l, 2 * BW), lambda i: (i, 0, 3)),
                  pl.BlockSpec((1, l, BW), lambda i: (i, 0, 0)),
                  pl.BlockSpec((32, BW), lambda i: (0, 0))],
        out_specs=[pl.BlockSpec((1, l, 2 * BW), lambda i: (i, 0, 0)),
                   pl.BlockSpec((32, BW), lambda i: (0, 0)), pl.BlockSpec((1, BW), lambda i: (0, 0))],
        out_shape=[jax.ShapeDtypeStruct((b, l, 2 * BW), bf16), jax.ShapeDtypeStruct((32, BW), f32),
                   jax.ShapeDtypeStruct((1, BW), f32)],
        scratch_shapes=[pltpu.VMEM((CPAD + l + CTAIL, BW), f32), pltpu.VMEM((l + CPAD + CTAIL, BW), f32)],
        compiler_params=_cp(1))(proj3, dhc, cw)


def _head_expand():
    r = lax.broadcasted_iota(jnp.int32, (128, BW), 0)
    c = lax.broadcasted_iota(jnp.int32, (128, BW), 1) // HD
    return (r == c).astype(f32)


def _head_reduce():
    r = lax.broadcasted_iota(jnp.int32, (BW, 128), 0) // HD
    c = lax.broadcasted_iota(jnp.int32, (BW, 128), 1)
    return (r == c).astype(f32)


def _merge_common(ys_ref, o_refs, l_refs, hc_ref, g_refs, bg_ref, lng_ref, lnb_ref, wglu_ref, watt_ref, wpw_ref):
    r = {}
    ysv = ys_ref[...]
    r["ys"] = ysv
    r["ysin"] = _gelu(ysv).astype(bf16)
    z = _dot(r["ysin"], wglu_ref[...])
    r["z1"], r["sg2"] = z[:, :D], _sigmoid(z[:, D:])
    r["y_s"] = r["z1"] * r["sg2"]
    ls = [lr_[0, 0] + lr_[0, 1] for lr_ in l_refs]
    mx = jnp.maximum(jnp.maximum(ls[0], ls[1]), ls[2])
    es = [jnp.exp(v - mx) for v in ls]
    tot = es[0] + es[1] + es[2]
    r["lse_tot"] = mx + jnp.log(tot)
    e_mat = _head_expand()
    o = jnp.zeros(ysv.shape, f32)
    for e, o_ref in zip(es, o_refs):
        o = o + _dot_hi(e / tot, e_mat) * o_ref[...].astype(f32)
    r["o"] = o
    r["ob"] = o.astype(bf16)
    r["y_a"] = _dot(r["ob"], watt_ref[...])
    hc = hc_ref[...]
    mu = jnp.mean(hc, axis=-1, keepdims=True)
    xc = hc - mu
    rstd = lax.rsqrt(jnp.mean(xc * xc, axis=-1, keepdims=True) + EPS)
    r["xh"], r["rstd"] = xc * rstd, rstd
    hn = r["xh"] * lng_ref[...] + lnb_ref[...]
    r["hn"] = hn
    r["sgn"] = _sigmoid(hn)
    r["hs"] = (hn * r["sgn"]).astype(bf16)
    r["y_c"] = _dot(r["hs"], wpw_ref[...])
    r["gates"] = [_sigmoid(g_refs[k][...].astype(f32) + bg_ref[:, k * D:(k + 1) * D]) for k in range(3)]
    r["merged"] = r["gates"][0] * r["y_s"] + r["gates"][1] * r["y_a"] + r["gates"][2] * r["y_c"]
    return r


TBM = 256


def _merge_in_specs(tok, tb, lses):
    w = lambda shape: pl.BlockSpec(shape, lambda i: (0, 0), pipeline_mode=pl.Buffered(1))
    nbl = lses[0].shape[2] // tb
    return ([pl.BlockSpec((tb, D), tok), pl.BlockSpec((tb, BW), tok)]
            + [pl.BlockSpec((tb, BW), tok)] * 3
            + [pl.BlockSpec((1, 2, tb, 128), lambda i: (i // nbl, 0, i % nbl, 0))] * 3
            + [pl.BlockSpec((tb, BW), tok)]
            + [pl.BlockSpec((tb, D), lambda i, k=k: (i, 4 + k)) for k in range(3)]
            + [w((1, 3 * D)), w((1, BW)), w((1, BW)), w((BW, 2 * D)), w((BW, D)), w((BW, D)), w((D, D))])


def merge_fwd(x, ys, os_, lses, hc, proj, bg, lng, lnb, wglu, watt, wpw, wout):
    n = x.shape[0]

    def body(x_ref, ys_ref, o1, o2, o3, l1, l2, l3, hc_ref, g0, g1, g2, bg_ref, lng_ref, lnb_ref,
             wglu_ref, watt_ref, wpw_ref, wout_ref, x1_ref):
        r = _merge_common(ys_ref, (o1, o2, o3), (l1, l2, l3), hc_ref, (g0, g1, g2), bg_ref, lng_ref, lnb_ref,
                          wglu_ref, watt_ref, wpw_ref)
        x1_ref[...] = x_ref[...] + _dot(r["merged"].astype(bf16), wout_ref[...])

    tok = lambda i: (i, 0)
    return pl.pallas_call(
        body, name="merge_fwd", grid=(n // TB,), in_specs=_merge_in_specs(tok, TB, lses),
        out_specs=pl.BlockSpec((TB, D), tok), out_shape=jax.ShapeDtypeStruct((n, D), f32),
        compiler_params=_cp(1, 56))(x, ys, *os_, *lses, hc, proj, proj, proj, bg, lng, lnb, wglu, watt, wpw, wout)


def merge_bwd(dx1, ys, os_, lses, hc, proj, bg, lng, lnb, wglu, watt, wpw, wout):
    n = dx1.shape[0]

    def body(dx_ref, ys_ref, o1, o2, o3, l1, l2, l3, hc_ref, g0, g1, g2, bg_ref, lng_ref, lnb_ref,
             wglu_ref, watt_ref, wpw_ref, wout_ref,
             dys_ref, do_ref, delta_ref, ltot_ref, dhc_ref, dgate_ref, ysin_ref, dz_ref, ob_ref, dya_ref,
             hs_ref, dyc_ref, mg_ref, dbg_ref, dlng_ref, dlnb_ref):
        i = pl.program_id(0)

        @pl.when(i == 0)
        def _():
            dbg_ref[...] = jnp.zeros_like(dbg_ref)
            dlng_ref[...] = jnp.zeros_like(dlng_ref)
            dlnb_ref[...] = jnp.zeros_like(dlnb_ref)

        r = _merge_common(ys_ref, (o1, o2, o3), (l1, l2, l3), hc_ref, (g0, g1, g2), bg_ref, lng_ref, lnb_ref,
                          wglu_ref, watt_ref, wpw_ref)
        mg_ref[...] = r["merged"].astype(bf16)
        ysin_ref[...] = r["ysin"]
        ob_ref[...] = r["ob"]
        hs_ref[...] = r["hs"]
        ltot_ref[...] = r["lse_tot"]
        dm = _dot_nt(dx_ref[...].astype(bf16), wout_ref[...])
        ys3 = (r["y_s"], r["y_a"], r["y_c"])
        for k in range(3):
            gk = r["gates"][k]
            dgr = dm * ys3[k] * gk * (1.0 - gk)
            dgate_ref[:, k * D:(k + 1) * D] = dgr.astype(bf16)
            dbg_ref[:, k * D:(k + 1) * D] += jnp.sum(dgr, axis=0, keepdims=True)
        dy_s = dm * r["gates"][0]
        sg2 = r["sg2"]
        dz = jnp.concatenate([dy_s * sg2, dy_s * r["z1"] * sg2 * (1.0 - sg2)], axis=1).astype(bf16)
        dz_ref[...] = dz
        dys_ref[...] = _dot_nt(dz, wglu_ref[...]) * _gelu_grad(r["ys"])
        dya = (dm * r["gates"][1]).astype(bf16)
        dya_ref[...] = dya
        do = _dot_nt(dya, watt_ref[...])
        do_ref[...] = do.astype(bf16)
        delta_ref[...] = _dot_hi(do * r["o"], _head_reduce())
        dyc = (dm * r["gates"][2]).astype(bf16)
        dyc_ref[...] = dyc
        sgn, hn = r["sgn"], r["hn"]
        dhn = _dot_nt(dyc, wpw_ref[...]) * sgn * (1.0 + hn * (1.0 - sgn))
        dlng_ref[...] += jnp.sum(dhn * r["xh"], axis=0, keepdims=True)
        dlnb_ref[...] += jnp.sum(dhn, axis=0, keepdims=True)
        dxh = dhn * lng_ref[...]
        xh = r["xh"]
        dhc_ref[...] = r["rstd"] * (dxh - jnp.mean(dxh, axis=-1, keepdims=True)
                                    - xh * jnp.mean(dxh * xh, axis=-1, keepdims=True))

    tok = lambda i: (i, 0)
    fix = lambda i: (0, 0)
    outs = [("dys", BW, f32), ("do", BW, bf16), ("delta", 128, f32), ("lse_tot", 128, f32), ("dhc", BW, f32),
            ("dgate", 3 * D, bf16), ("ysin", BW, bf16), ("dz", 2 * D, bf16), ("ob", BW, bf16), ("dya", D, bf16),
            ("hs", BW, bf16), ("dyc", D, bf16), ("merged", D, bf16)]
    small = [("dbg", 3 * D), ("dlng", BW), ("dlnb", BW)]
    res = pl.pallas_call(
        body, name="merge_bwd", grid=(n // TBM,), in_specs=_merge_in_specs(tok, TBM, lses),
        out_specs=[pl.BlockSpec((TBM, w), tok) for _, w, _ in outs] + [pl.BlockSpec((1, w), fix) for _, w in small],
        out_shape=[jax.ShapeDtypeStruct((n, w), dt) for _, w, dt in outs]
        + [jax.ShapeDtypeStruct((1, w), f32) for _, w in small],
        compiler_params=_cp(1, 56))(dx1, ys, *os_, *lses, hc, proj, proj, proj, bg, lng, lnb, wglu, watt, wpw, wout)
    return dict(zip([k for k, _, _ in outs] + [k for k, _ in small], res))


def assemble_dproj(du, dqs, dks, dvs, dcv, dgate):
    b, l, _ = du.shape
    nck = BW // 128

    def body(du_ref, q1, q2, q3, k1, k2, k3, v1, v2, v3, cv_ref, g_ref, o_ref):
        o_ref[0, :, 0:BW] = du_ref[0]
        for c in range(nck):
            for j, qr in enumerate((q1, q2, q3)):
                o_ref[0, :, (1 + j) * BW + c * 128:(1 + j) * BW + (c + 1) * 128] = qr[0, c]
            add3 = lambda r1, r2, r3: (r1[0, c].astype(f32) + r2[0, c].astype(f32) + r3[0, c].astype(f32)).astype(bf16)
            o_ref[0, :, 4 * BW + c * 128:4 * BW + (c + 1) * 128] = add3(k1, k2, k3)
            o_ref[0, :, 5 * BW + c * 128:5 * BW + (c + 1) * 128] = add3(v1, v2, v3)
        o_ref[0, :, 6 * BW:8 * BW] = cv_ref[0]
        o_ref[0, :, 8 * BW:] = g_ref[0]

    t = lambda w: pl.BlockSpec((1, TB, w), lambda bi, i: (bi, i, 0))
    ck = pl.BlockSpec((1, nck, TB, 128), lambda bi, i: (bi, 0, i, 0))
    return pl.pallas_call(
        body, name="assemble_dproj", grid=(b, l // TB),
        in_specs=[t(BW)] + [ck] * 9 + [t(2 * BW), t(3 * D)], out_specs=t(INC),
        out_shape=jax.ShapeDtypeStruct((b, l, INC), bf16), compiler_params=_cp(2))(du, *dqs, *dks, *dvs, dcv, dgate)


def _me():
    return lax.axis_index("x"), lax.axis_index("y"), lax.axis_index("c")


def _peers():
    x, y, c = _me()
    return [(x, y, 1 - c), (1 - x, y, c), (1 - x, y, 1 - c), (x, 1 - y, c), (x, 1 - y, 1 - c),
            (1 - x, 1 - y, c), (1 - x, 1 - y, 1 - c)]


def _rank(p):
    return 4 * p[0] + 2 * p[1] + p[2]


def allgather(arrs, name):
    na = len(arrs)
    units = [(a, j) for a in range(na) for j in range(arrs[a].shape[0])]
    nu = len(units)

    def body(*refs):
        ins, outs = refs[:na], refs[na:2 * na]
        send, recv, loc = refs[2 * na:]
        me = _rank(_me())
        local, remote = [], []
        for u, (a, j) in enumerate(units):
            own = pltpu.make_async_copy(ins[a].at[j], outs[a].at[j, me], loc.at[u])
            own.start()
            local.append(own)
        for u, (a, j) in enumerate(units):
            for k, p in enumerate(_peers()):
                cp = pltpu.make_async_remote_copy(src_ref=ins[a].at[j], dst_ref=outs[a].at[j, me],
                                                  send_sem=send.at[u, k], recv_sem=recv.at[u, k],
                                                  device_id=p, device_id_type=MESH)
                cp.start()
                remote.append(cp)
        for cp in local:
            cp.wait()
        for cp in remote:
            cp.wait()

    return pl.pallas_call(
        body, name=name, in_specs=[ANY] * na, out_specs=[ANY] * na,
        out_shape=[jax.ShapeDtypeStruct((a.shape[0], NDEV) + a.shape[1:], a.dtype) for a in arrs],
        scratch_shapes=[pltpu.SemaphoreType.DMA((nu, NDEV - 1)), pltpu.SemaphoreType.DMA((nu, NDEV - 1)),
                        pltpu.SemaphoreType.DMA((nu,))])(*arrs)


HBM = pl.BlockSpec(memory_space=pltpu.HBM)
SEM = pl.BlockSpec(memory_space=pltpu.SEMAPHORE)
_EFFECT = pltpu.SideEffectType.DATAFLOW_SIDE_EFFECTING


def _rank_slot(ref, r):
    if ref.shape[0] == NDEV:
        return ref.at[r]
    n = ref.shape[2] // 2
    return ref.at[r // 2, :, pl.ds(pl.multiple_of((r % 2) * n, 128), n)]


def _push_copies(srcs, lands, send, recv, scatter):
    me = _rank(_me())
    out = []
    for i in range(len(srcs)):
        for k, p in enumerate(_peers()):
            src = _rank_slot(srcs[i], _rank(p)) if scatter else srcs[i]
            dst = lands[i].at[k] if scatter else _rank_slot(lands[i], me)
            j = i * (NDEV - 1) + k
            out.append(pltpu.make_async_remote_copy(src_ref=src, dst_ref=dst, send_sem=send.at[j],
                                                    recv_sem=recv.at[j], device_id=p, device_id_type=MESH))
    return out


def push_start(srcs, lands, scatter, name, token):
    n = len(srcs)
    token = jnp.zeros((8, 128), f32) if token is None else token

    def body(*refs):
        for cp in _push_copies(refs[:n], refs[n:2 * n], refs[2 * n + 1], refs[2 * n + 2], scatter):
            cp.start()
        refs[-1][...] = refs[2 * n][...]

    sems = pltpu.SemaphoreType.DMA((n * (NDEV - 1),))
    vmem = pl.BlockSpec(memory_space=pltpu.VMEM)
    res = pl.pallas_call(
        body, name=name, in_specs=[HBM] * (2 * n) + [vmem], out_specs=[SEM, SEM] + [HBM] * (2 * n) + [vmem],
        out_shape=[sems, sems] + [pltpu.HBM(a.shape, a.dtype) for a in list(srcs) + list(lands)]
        + [jax.ShapeDtypeStruct((8, 128), f32)],
        input_output_aliases={i: 2 + i for i in range(2 * n)},
        compiler_params=pltpu.CompilerParams(has_side_effects=_EFFECT),
    )(*[pltpu.with_memory_space_constraint(a, pltpu.HBM) for a in list(srcs) + list(lands)], token)
    return res[0], res[1], res[2:2 + n], res[2 + n:2 + 2 * n], res[-1]


def push_wait(send, recv, srcs, lands, after, scatter, name):
    n = len(srcs)
    after = list(after) if isinstance(after, (list, tuple)) else [after]

    def body(*refs):
        for cp in _push_copies(refs[:n], refs[n:2 * n], refs[2 * n], refs[2 * n + 1], scatter):
            cp.wait_send()
            cp.wait_recv()

    res = pl.pallas_call(
        body, name=name, in_specs=[HBM] * (2 * n) + [SEM, SEM] + [ANY] * len(after), out_specs=[HBM] * (2 * n),
        out_shape=[pltpu.HBM(a.shape, a.dtype) for a in list(srcs) + list(lands)],
        input_output_aliases={i: i for i in range(2 * n)},
        compiler_params=pltpu.CompilerParams(has_side_effects=_EFFECT),
    )(*srcs, *lands, send, recv, *after)
    return res[:n], res[n:]


_C1 = 1.0 / (1.0 - ADAM_B1 ** ADAM_STEP)
_C2 = 1.0 / (1.0 - ADAM_B2 ** ADAM_STEP)


def _adamw(w, g, m, v):
    m = ADAM_B1 * m + (1.0 - ADAM_B1) * g
    v = ADAM_B2 * v + (1.0 - ADAM_B2) * (g * g)
    delta = -ADAM_LR * ((m * _C1) / (jnp.sqrt(v * _C2) + ADAM_EPS) + ADAM_WD * w)
    return delta, m, v


def adam_big(lands, owns, w, m, v, name):
    _, k, n = lands[0].shape
    tk = k
    while tk * n * 2 * NDEV > 2 * 1024 * 1024 and tk % 16 == 0:
        tk //= 2

    def body(*refs):
        l_refs, o_refs = refs[:DEPTH], refs[DEPTH:2 * DEPTH]
        w_ref, m_ref, v_ref, g_ref, d_ref, nm_ref, nv_ref = refs[2 * DEPTH:]
        for l in range(DEPTH):
            g = o_refs[l][...].astype(f32)
            for s in range(NDEV - 1):
                g = g + l_refs[l][s].astype(f32)
            d, nm, nv = _adamw(w_ref[l], g, m_ref[l], v_ref[l])
            g_ref[l], d_ref[l], nm_ref[l], nv_ref[l] = g, d, nm, nv

    blk = pl.BlockSpec((DEPTH, tk, n), lambda i: (0, i, 0))
    return pl.pallas_call(
        body, name=name, grid=(k // tk,),
        in_specs=[pl.BlockSpec((NDEV - 1, tk, n), lambda i: (0, i, 0))] * DEPTH
        + [pl.BlockSpec((tk, n), lambda i: (i, 0))] * DEPTH + [blk, blk, blk],
        out_specs=[blk] * 4, out_shape=[jax.ShapeDtypeStruct(w.shape, f32)] * 4,
        compiler_params=_cp(1))(*lands, *owns, w, m, v)


def adam_small(gath, w, m, v):
    r = w.shape[0]
    tr = 512

    def body(g_ref, w_ref, m_ref, v_ref, go_ref, d_ref, nm_ref, nv_ref):
        g = g_ref[0]
        for s in range(1, NDEV):
            g = g + g_ref[s]
        d, nm, nv = _adamw(w_ref[...], g, m_ref[...], v_ref[...])
        go_ref[...], d_ref[...], nm_ref[...], nv_ref[...] = g, d, nm, nv

    blk = pl.BlockSpec((tr, 128), lambda i: (i, 0))
    return pl.pallas_call(
        body, name="adam_small", grid=(r // tr,),
        in_specs=[pl.BlockSpec((NDEV, tr, 128), lambda i: (0, i, 0)), blk, blk, blk],
        out_specs=[blk] * 4, out_shape=[jax.ShapeDtypeStruct((r, 128), f32)] * 4,
        compiler_params=_cp(1))(gath, w, m, v)


SMALL = ["norm1_g", "b_gate", "ssm_lambda_re", "ssm_lambda_im", "ssm_log_dt", "ssm_b_re", "ssm_b_im",
         "ssm_c_re", "ssm_c_im", "ssm_d", "conv_w", "conv_b", "conv_ln_g", "conv_ln_b", "norm2_g", "final_g"]
BIG = ["w_in", "w_ssm_glu", "w_att_up", "w_conv_pw2", "w_out", "w_ffn_in", "w_ffn_out"]
ORDER = ["norm1_g", "w_in", "b_gate", "ssm_lambda_re", "ssm_lambda_im", "ssm_log_dt", "ssm_b_re", "ssm_b_im",
         "ssm_c_re", "ssm_c_im", "ssm_d", "w_ssm_glu", "w_att_up", "conv_w", "conv_b", "conv_ln_g", "conv_ln_b",
         "w_conv_pw2", "w_out", "norm2_g", "w_ffn_in", "w_ffn_out", "final_g"]
PACK_ROWS = 2560


def _pack(arrs):
    flat = jnp.concatenate([a.reshape(-1).astype(f32) for a in arrs])
    return jnp.pad(flat, (0, PACK_ROWS * 128 - flat.shape[0])).reshape(PACK_ROWS, 128)


def _unpack(pack, shapes):
    flat = pack.reshape(-1)
    out, off = [], 0
    for s in shapes:
        sz = math.prod(s)
        out.append(flat[off:off + sz].reshape(s))
        off += sz
    return out


def _bt(b):
    return b.transpose(2, 0, 1).reshape(GH, NSTATE)


def _bt_inv(bt):
    return bt.reshape(GH, NG, NS).transpose(1, 2, 0)


def _ct(c):
    return c.transpose(1, 0, 2).reshape(GH, NSTATE)


def _ct_inv(ct):
    return ct.reshape(GH, NG, NS).transpose(1, 0, 2)


def local_step(x, loss_target, P, weights, on_grads, start_token=None):
    bsz, seq, _ = x.shape
    n = bsz * seq

    def natural(g3):
        return g3.transpose(1, 0, 2).reshape(g3.shape[1], NDEV * g3.shape[2])

    tokens = [] if start_token is None else [start_token]

    def after_pushes(a):
        while tokens:
            a = a + tokens.pop()[0:1, 0:1]
        return a

    def pushed(tok):
        if tok is not None:
            tokens.append(tok)

    s5_in, s5_tabs = [], []
    for l in range(DEPTH):
        lr = P["ssm_lambda_re"][l].reshape(1, NSTATE)
        li = P["ssm_lambda_im"][l].reshape(1, NSTATE)
        ld = jnp.repeat(P["ssm_log_dt"][l], NS).reshape(1, NSTATE)
        btr, bti = _bt(P["ssm_b_re"][l]), _bt(P["ssm_b_im"][l])
        s5_in.append((lr, li, ld, btr, bti))
        s5_tabs.append(s5_params(lr, li, ld, btr, bti, _ct(P["ssm_c_re"][l]), _ct(P["ssm_c_im"][l])))

    xs = x.reshape(n, D)
    saved = []
    conv_w_pad = None
    for l in range(DEPTH):
        S = {"x": xs}
        h1 = rms_fwd(xs, after_pushes(P["norm1_g"][l][None]))
        G = dict(weights(l, "in", [h1] + [t for tabs in s5_tabs for t in tabs] if l == 0 else h1))
        if conv_w_pad is None:
            conv_w_full = G["conv_w"].transpose(1, 2, 0, 3).reshape(DEPTH, CW, BW)
            conv_w_pad = jnp.pad(conv_w_full, ((0, 0), (0, 1), (0, 0)))
        w_in4 = G["w_in"][None]
        proj = inproj(h1, w_in4, 0)
        proj3 = proj.reshape(bsz, seq, INC)
        lr, li, ld, btr, bti = s5_in[l]
        t8, bb, cb = s5_tabs[l]
        dskip = P["ssm_d"][l][None]
        ys, s_re, s_im = s5_fwd(proj3, t8, bb, cb, dskip)
        att = [att_fwd(proj3, gi, dil) for gi, (_, dil) in enumerate(PATTERNS)]
        hc = conv_fwd(proj3, conv_w_pad[l], P["conv_b"][l][None])
        G.update(weights(l, "mix", hc))
        wts = dict(wglu=natural(G["w_ssm_glu"]), watt=natural(G["w_att_up"]),
                   wpw=natural(G["w_conv_pw2"]), wout=G["w_out"].reshape(D, D))
        mi = dict(ys=ys.reshape(n, BW), os_=[a[0].reshape(n, BW) for a in att],
                  lses=[a[1] for a in att], hc=hc.reshape(n, BW),
                  proj=proj, bg=P["b_gate"][l][None], lng=P["conv_ln_g"][l][None], lnb=P["conv_ln_b"][l][None],
                  **wts)
        x1 = merge_fwd(xs, **mi)
        G.update(weights(l, "ffn", x1))
        w_ffn = (G["w_ffn_in"][None], G["w_ffn_out"][None])
        x2, z1s, z2s, h2 = ffn_fwd(x1, P["norm2_g"][l][None], *w_ffn, 0)
        S.update(h1=h1, proj=proj, proj3=proj3, tabs=(s_re, s_im, t8, bb, cb), mi=mi, x1=x1, w_in4=w_in4, w_ffn=w_ffn,
                 zs=(z1s, z2s), h2=h2,
                 sp=(lr, li, ld, btr, bti), dskip=dskip)
        saved.append(S)
        xs = x2

    loss8, dx, dfinal = loss_head(xs, P["final_g"][None], loss_target.reshape(n, D))

    small_g = {k: [None] * DEPTH for k in SMALL if k != "final_g"}
    tokblk = lambda w: pl.BlockSpec((1024, w), lambda s, i: (i, 0))
    colblk = lambda w: pl.BlockSpec((1024, w), lambda s, i: (i, s))
    for l in reversed(range(DEPTH)):
        S = saved[l]
        g2 = P["norm2_g"][l][None]
        dh4, dwa, dwb, dw2 = ffn_bwd(S["h2"], dx, *S["zs"], *S["w_ffn"], 0)
        dx1, dg2 = norm_bwd_fin(S["x1"], after_pushes(g2), dx, dh4, "ffn_bwd_fin")
        small_g["norm2_g"][l] = dg2
        pushed(on_grads(l, "ffn", dict(w_ffn_in=jnp.concatenate([dwa, dwb], axis=0),
                                       w_ffn_out=dw2.reshape(NDEV, NSH_FF // 2, D))))
        mb = merge_bwd(dx1, **dict(S["mi"], lng=after_pushes(S["mi"]["lng"])))
        small_g["b_gate"][l], small_g["conv_ln_g"][l], small_g["conv_ln_b"][l] = mb["dbg"], mb["dlng"], mb["dlnb"]
        dws = dw_mix(mb["ysin"], mb["dz"], mb["ob"], mb["dya"], mb["hs"], mb["dyc"], mb["merged"], dx1)
        pushed(on_grads(l, "mix", dict(zip(("w_ssm_glu", "w_att_up", "w_conv_pw2", "w_out"), dws))))
        dcv, dcw, dcb = conv_bwd(S["proj3"], mb["dhc"].reshape(bsz, seq, BW), after_pushes(conv_w_pad[l]))
        small_g["conv_w"][l] = dcw[:CW].reshape(CW, NDEV, BW // NDEV).transpose(1, 0, 2)
        small_g["conv_b"][l] = dcb
        ab = [att_bwd(S["proj3"], mb["do"].reshape(bsz, seq, BW), mb["lse_tot"].reshape(bsz, seq, 128),
                      mb["delta"].reshape(bsz, seq, 128), gi, dil) for gi, (_, dil) in enumerate(PATTERNS)]
        du, d_a, d_bb, d_cb, d_d = s5_bwd(S["proj3"], mb["dys"].reshape(bsz, seq, BW), *S["tabs"], S["dskip"])
        lr, li, ld, btr, bti = S["sp"]
        dlr, dli, dld, dbt, dct = s5_params_bwd(lr, li, ld, btr, bti, d_a, d_bb, d_cb)
        small_g["ssm_lambda_re"][l], small_g["ssm_lambda_im"][l] = dlr.reshape(NG, NS), dli.reshape(NG, NS)
        small_g["ssm_log_dt"][l] = dld[0, :NG]
        small_g["ssm_b_re"][l], small_g["ssm_b_im"][l] = _bt_inv(dbt[0]), _bt_inv(dbt[1])
        small_g["ssm_c_re"][l], small_g["ssm_c_im"][l] = _ct_inv(dct[0]), _ct_inv(dct[1])
        small_g["ssm_d"][l] = d_d
        dproj = assemble_dproj(du, [a[0] for a in ab], [a[1] for a in ab], [a[2] for a in ab],
                               dcv, mb["dgate"].reshape(bsz, seq, 3 * D)).reshape(n, INC)
        nblk, wblk = S["w_in4"].shape[1], S["w_in4"].shape[3]
        pushed(on_grads(l, "in", dict(w_in=mm_tn(S["h1"], dproj, tokblk(D), colblk(wblk), nblk, D, wblk, n, "dw_in"))))
        if l == 0:
            pushed(on_grads(l, "small", dict(small_g=small_g, loss8=loss8, dfinal=dfinal)))
        dx, dg1 = inproj_bwd(dproj, S["w_in4"], 0, S["x"], after_pushes(P["norm1_g"][l][None]), dx1)
        small_g["norm1_g"][l] = dg1
    return loss8, dx, dfinal, small_g


def kernel(x, norm1_g, w_in, b_gate, ssm_lambda_re, ssm_lambda_im, ssm_log_dt, ssm_b_re, ssm_b_im, ssm_c_re, ssm_c_im, ssm_d, w_ssm_glu, w_att_up, conv_w, conv_b, conv_ln_g, conv_ln_b, w_conv_pw2, w_out, norm2_g, w_ffn_in, w_ffn_out, final_g, loss_target, m_norm1_g, m_w_in, m_b_gate, m_ssm_lambda_re, m_ssm_lambda_im, m_ssm_log_dt, m_ssm_b_re, m_ssm_b_im, m_ssm_c_re, m_ssm_c_im, m_ssm_d, m_w_ssm_glu, m_w_att_up, m_conv_w, m_conv_b, m_conv_ln_g, m_conv_ln_b, m_w_conv_pw2, m_w_out, m_norm2_g, m_w_ffn_in, m_w_ffn_out, m_final_g, v_norm1_g, v_w_in, v_b_gate, v_ssm_lambda_re, v_ssm_lambda_im, v_ssm_log_dt, v_ssm_b_re, v_ssm_b_im, v_ssm_c_re, v_ssm_c_im, v_ssm_d, v_w_ssm_glu, v_w_att_up, v_conv_w, v_conv_b, v_conv_ln_g, v_conv_ln_b, v_w_conv_pw2, v_w_out, v_norm2_g, v_w_ffn_in, v_w_ffn_out, v_final_g):
    args = dict(locals())
    W = {k: args[k] for k in ORDER}
    M = {k: args["m_" + k] for k in ORDER}
    V = {k: args["v_" + k] for k in ORDER}
    bsz, seq, _ = x.shape
    n = bsz * seq
    me = 4 * lax.axis_index("x") + 2 * lax.axis_index("y") + lax.axis_index("c")

    groups = {"in": ["w_in"], "mix": ["w_ssm_glu", "w_att_up", "w_conv_pw2", "w_out"], "ffn": ["w_ffn_in", "w_ffn_out"]}
    wb = {k: W[k].astype(bf16) for k in BIG}

    def landing(shard, paired=False):
        if paired:
            k_, n_ = shard.shape
            return lax.dynamic_update_slice(lax.empty((NDEV // 2, k_, 2 * n_), shard.dtype), shard[None],
                                            (me // 2, 0, (me % 2) * n_))
        return lax.dynamic_update_index_in_dim(lax.empty((NDEV,) + shard.shape, shard.dtype), shard, me, 0)

    def own_part(by_rank):
        if by_rank.shape[0] == NDEV:
            return lax.dynamic_index_in_dim(by_rank, me, 0, keepdims=False)
        n_ = by_rank.shape[2] // 2
        return lax.dynamic_slice(by_rank, (me // 2, 0, (me % 2) * n_), (1, by_rank.shape[1], n_))[0]

    plan = [("gather_a", [("w_in", 0), ("conv_w", None)]),
            ("gather_b", [(k, 0) for k in groups["mix"] + groups["ffn"]]),
            ("gather_c", [(k, 1) for k in BIG])]
    pending, token = {}, None
    for name, items in plan:
        shards = [conv_w if l is None else wb[k][l] for k, l in items]
        lands = [landing(s, k == "w_in") for (k, _), s in zip(items, shards)]
        send, recv, s_thru, l_thru, token = push_start(shards, lands, False, name, token)
        pending[name] = (send, recv, s_thru, l_thru, items)
    gathered = {}

    names = [k for k in SMALL if k != "final_g"]
    shapes = [(DEPTH, NDEV, CW, BW // NDEV) if k == "conv_w" else W[k].shape for k in names] + [(D,), (1,)]

    def wpack(src):
        parts = [jnp.broadcast_to(src[k][:, None], shapes[i]) if k == "conv_w" else src[k] for i, k in enumerate(names)]
        return _pack(parts + [src["final_g"], jnp.ones((1,), f32)])

    packs = [wpack(W), wpack(M), wpack(V)]

    def weights(l, group, after):
        name = "gather_c" if l == 1 else ("gather_a" if group == "in" else "gather_b")
        if name in pending:
            send, recv, s_thru, l_thru, items = pending.pop(name)
            if name == "gather_a":
                after = (list(after) if isinstance(after, (list, tuple)) else [after]) + packs
            for item, arr in zip(items, push_wait(send, recv, s_thru, l_thru, after, False, name + "_wait")[1]):
                gathered[item] = arr
        res = {k: gathered[(k, l)] for k in groups[group]}
        if group == "in":
            res["conv_w"] = gathered[("conv_w", None)]
        return res

    big_g = {k: [None] * DEPTH for k in BIG}
    flights = []

    def start_exchange(items, name):
        parts = [big_g[k][l] for k, l in items]
        part = lambda p: p.shape[1:] if p.shape[0] == NDEV else (p.shape[1], p.shape[2] // 2)
        lands = [lax.empty((NDEV - 1,) + part(p), p.dtype) for p in parts]
        send, recv, s_thru, l_thru, tok = push_start(parts, lands, True, name, None)
        flights.append((send, recv, s_thru, l_thru, items, name))
        return tok

    small_flight = []

    def start_small(small_g, loss8, dfinal):
        sg_ = dict(small_g, norm1_g=[jnp.zeros((1, D), f32), small_g["norm1_g"][1]])
        gpack = _pack([jnp.stack([g.reshape(shapes[i][1:]) for g in sg_[k]]) for i, k in enumerate(names)]
                      + [dfinal, loss8[0, :1]])
        send, recv, s_thru, l_thru, tok = push_start([gpack], [landing(gpack)], False, "gather_small", None)
        small_flight.append((send, recv, s_thru, l_thru))
        return tok

    def on_grads(l, group, grads):
        if group == "small":
            return start_small(**grads)
        for k, g in grads.items():
            big_g[k][l] = g
        if l == 1 and group == "in":
            return start_exchange([(k, 1) for k in BIG], "exchange_l1")
        if l == 0:
            return start_exchange([(k, 0) for k in groups[group]], "exchange_l0_" + group)
        return None

    loss8, dx, dfinal, small_g = local_step(x, loss_target, W, weights, on_grads, token)

    landed, own = {}, {}
    for send, recv, s_thru, l_thru, items, name in flights:
        srcs, lands = push_wait(send, recv, s_thru, l_thru, dx, True, name + "_wait")
        for item, src, land in zip(items, srcs, lands):
            landed[item] = land
            own[item] = own_part(src)
    out = {}
    for k in BIG:
        items = [(k, l) for l in range(DEPTH)]
        out[k] = adam_big([landed[i] for i in items], [own[i] for i in items], W[k], M[k], V[k], "adam_" + k)

    send, recv, s_thru, l_thru = small_flight[0]
    gall = push_wait(send, recv, s_thru, l_thru, out[BIG[-1]][0], False, "gather_small_wait")[1][0]
    (late,) = allgather([small_g["norm1_g"][0].reshape(1, D // 128, 128)], "allgather_late")
    gall = lax.dynamic_update_slice(gall, late[0], (0, 0, 0))
    sg, sd, sm, sv = [_unpack(p, shapes) for p in adam_small(gall, *packs)]
    for i, k in enumerate(names + ["final_g"]):
        vals = [t[i] for t in (sg, sd, sm, sv)]
        if k == "conv_w":
            vals = [lax.dynamic_index_in_dim(t, me, axis=1, keepdims=False) for t in vals]
        out[k] = vals
    loss = sg[-1].reshape(())

    res = [loss, dx.reshape(bsz, seq, D)]
    for j in range(4):
        res += [out[k][j] for k in ORDER]
    return tuple(res)
```

```python
import functools
import math

import jax
import jax.numpy as jnp
from jax import lax
from jax.experimental import pallas as pl
from jax.experimental.pallas import tpu as pltpu

f32 = jnp.float32
bf16 = jnp.bfloat16

D = 1024
DEPTH = 2
EPS = 1e-6
BW = 512
NG = 32
GH = 16
NS = 64
NSTATE = NG * NS
HD = 64
NH = 8
PATTERNS = ((128, 1), (512, 4), (2048, 16))
ABLK = 128
ATT_SCALE = HD ** -0.5
CW = 31
DFF = 2816
INC = 7168
NDEV = 8
NSH_IN = INC // NDEV
NSH_FF = 2 * DFF // NDEV
ADAM_LR, ADAM_B1, ADAM_B2, ADAM_EPS, ADAM_WD, ADAM_STEP = 0.001, 0.9, 0.999, 1e-08, 0.01, 10

TB = 512
SJ = 4
SW = NSTATE // SJ
SU = BW // SJ
NEG = -1e30
MESH = pl.DeviceIdType.MESH
ANY = pl.BlockSpec(memory_space=pl.ANY)


def _cp(n_axes, vmem_mb=48):
    return pltpu.CompilerParams(dimension_semantics=("arbitrary",) * n_axes,
                                vmem_limit_bytes=vmem_mb * 1024 * 1024)


def _dot(a, b):
    return jnp.dot(a, b, preferred_element_type=f32)


def _dot_nt(a, b):
    return lax.dot_general(a, b, (((1,), (1,)), ((), ())), preferred_element_type=f32)


def _dot_tn(a, b):
    return lax.dot_general(a, b, (((0,), (0,)), ((), ())), preferred_element_type=f32)


def _dot_sel(a, sel):
    hi = a.astype(bf16)
    lo = (a - hi.astype(f32)).astype(bf16)
    sb = sel.astype(bf16)
    return _dot(hi, sb) + _dot(lo, sb)


def _dot_hi(a, b):
    return jnp.dot(a, b, precision=lax.Precision.HIGHEST, preferred_element_type=f32)


def _sigmoid(x):
    return 1.0 / (1.0 + jnp.exp(-x))


_GC = math.sqrt(2.0 / math.pi)


def _gelu(x):
    return 0.5 * x * (1.0 + jnp.tanh(_GC * (x + 0.044715 * x * x * x)))


def _gelu_grad(x):
    t = jnp.tanh(_GC * (x + 0.044715 * x * x * x))
    return 0.5 * (1.0 + t) + 0.5 * x * (1.0 - t * t) * _GC * (1.0 + 3.0 * 0.044715 * x * x)


def _rms_stats(x):
    return lax.rsqrt(jnp.mean(x * x, axis=-1, keepdims=True) + EPS)


def _rms_bwd(x, g, dh):
    r = _rms_stats(x)
    dyg = dh * g
    dx = r * dyg - x * (r * r * r) * jnp.mean(dyg * x, axis=-1, keepdims=True)
    dg = jnp.sum(dh * x * r, axis=0, keepdims=True)
    return dx, dg


def rms_fwd(x, g):
    n = x.shape[0]

    def body(x_ref, g_ref, h_ref):
        xv = x_ref[...]
        h_ref[...] = (xv * _rms_stats(xv) * g_ref[...]).astype(bf16)

    return pl.pallas_call(
        body, name="rms_fwd", grid=(n // TB,),
        in_specs=[pl.BlockSpec((TB, D), lambda i: (i, 0)), pl.BlockSpec((1, D), lambda i: (0, 0))],
        out_specs=pl.BlockSpec((TB, D), lambda i: (i, 0)),
        out_shape=jax.ShapeDtypeStruct((n, D), bf16), compiler_params=_cp(1))(x, g)


def inproj(h, w4, layer):
    n = h.shape[0]
    tm = 1024
    nblk, wblk = w4.shape[1], w4.shape[3]

    def body(h_ref, w_ref, o_ref):
        o_ref[...] = _dot(h_ref[...], w_ref[0, 0]).astype(bf16)

    return pl.pallas_call(
        body, name="inproj", grid=(nblk, n // tm),
        in_specs=[pl.BlockSpec((tm, D), lambda s, i: (i, 0)),
                  pl.BlockSpec((1, 1, D, wblk), lambda s, i: (layer, s, 0, 0))],
        out_specs=pl.BlockSpec((tm, wblk), lambda s, i: (i, s)),
        out_shape=jax.ShapeDtypeStruct((n, INC), bf16), compiler_params=_cp(2))(h, w4)


def inproj_bwd(dproj, w4, layer, x, g, dres):
    n = x.shape[0]
    tm = 1024
    nblk, wblk = w4.shape[1], w4.shape[3]

    def body(dp_ref, w_ref, x_ref, g_ref, dr_ref, dx_ref, dg_ref, acc):
        i, s = pl.program_id(0), pl.program_id(1)

        @pl.when(s == 0)
        def _():
            acc[...] = jnp.zeros_like(acc)

        @pl.when((s == 0) & (i == 0))
        def _():
            dg_ref[...] = jnp.zeros_like(dg_ref)

        acc[...] += _dot_nt(dp_ref[...], w_ref[0, 0])

        @pl.when(s == nblk - 1)
        def _():
            dx, dg = _rms_bwd(x_ref[...], g_ref[...], acc[...])
            dx_ref[...] = dr_ref[...] + dx
            dg_ref[...] += dg

    return pl.pallas_call(
        body, name="inproj_bwd", grid=(n // tm, nblk),
        in_specs=[pl.BlockSpec((tm, wblk), lambda i, s: (i, s)),
                  pl.BlockSpec((1, 1, D, wblk), lambda i, s: (layer, s, 0, 0)),
                  pl.BlockSpec((tm, D), lambda i, s: (i, 0)),
                  pl.BlockSpec((1, D), lambda i, s: (0, 0)),
                  pl.BlockSpec((tm, D), lambda i, s: (i, 0))],
        out_specs=[pl.BlockSpec((tm, D), lambda i, s: (i, 0)), pl.BlockSpec((1, D), lambda i, s: (0, 0))],
        out_shape=[jax.ShapeDtypeStruct((n, D), f32), jax.ShapeDtypeStruct((1, D), f32)],
        scratch_shapes=[pltpu.VMEM((tm, D), f32)], compiler_params=_cp(2))(dproj, w4, x, g, dres)


def mm_tn(a, b, a_spec, b_spec, n_sh, ka, nb, m, name):
    tm = 1024

    def body(a_ref, b_ref, o_ref, acc):
        i = pl.program_id(1)

        @pl.when(i == 0)
        def _():
            acc[...] = jnp.zeros_like(acc)

        av = a_ref[...].reshape(tm, ka).astype(bf16)
        bv = b_ref[...].reshape(tm, nb).astype(bf16)
        acc[...] += _dot_tn(av, bv)

        @pl.when(i == m // tm - 1)
        def _():
            o_ref[0] = acc[...].astype(bf16)

    return pl.pallas_call(
        body, name=name, grid=(n_sh, m // tm), in_specs=[a_spec, b_spec],
        out_specs=pl.BlockSpec((1, ka, nb), lambda s, i: (s, 0, 0)),
        out_shape=jax.ShapeDtypeStruct((n_sh, ka, nb), bf16),
        scratch_shapes=[pltpu.VMEM((ka, nb), f32)], compiler_params=_cp(2))(a, b)


def dw_mix(ysin, dz, ob, dya, hs, dyc, merged, dx1):
    n = ysin.shape[0]
    tm = 512
    pairs = ((BW, 2 * D), (BW, D), (BW, D), (D, D))

    def body(a0, b0, a1, b1, a2, b2, a3, b3, o0, o1, o2, o3, c0, c1, c2, c3):
        i = pl.program_id(0)
        accs = (c0, c1, c2, c3)

        @pl.when(i == 0)
        def _():
            for c in accs:
                c[...] = jnp.zeros_like(c)

        for a, b_, c in zip((a0, a1, a2, a3), (b0, b1, b2, b3), accs):
            c[...] += _dot_tn(a[...], b_[...].astype(bf16))

        @pl.when(i == n // tm - 1)
        def _():
            for s in range(NDEV):
                o0[s] = c0[:, s * 256:(s + 1) * 256].astype(bf16)
                o1[s] = c1[:, s * 128:(s + 1) * 128].astype(bf16)
                o2[s] = c2[:, s * 128:(s + 1) * 128].astype(bf16)
                o3[s] = c3[s * 128:(s + 1) * 128, :].astype(bf16)

    tok = lambda w: pl.BlockSpec((tm, w), lambda i: (i, 0))
    whole = lambda shape: pl.BlockSpec(shape, lambda i: (0, 0, 0))
    outs = [(NDEV, BW, 256), (NDEV, BW, 128), (NDEV, BW, 128), (NDEV, D // NDEV, D)]
    return pl.pallas_call(
        body, name="dw_mix", grid=(n // tm,),
        in_specs=[tok(w) for pair in pairs for w in pair],
        out_specs=[whole(s) for s in outs], out_shape=[jax.ShapeDtypeStruct(s, bf16) for s in outs],
        scratch_shapes=[pltpu.VMEM(p, f32) for p in pairs],
        compiler_params=_cp(1, 56))(ysin, dz, ob, dya, hs, dyc, merged, dx1)


def ffn_fwd(x1, g2, w1, w2, layer):
    n = x1.shape[0]
    w2p = w2.reshape(w2.shape[0], 4, NSH_FF, D)

    def body(x_ref, g_ref, wa_ref, wb_ref, w2_ref, o_ref, z1_ref, z2_ref, h_sc):
        s = pl.program_id(1)

        @pl.when(s == 0)
        def _():
            xv = x_ref[...]
            h_sc[...] = (xv * _rms_stats(xv) * g_ref[...]).astype(bf16)
            o_ref[...] = xv

        h = h_sc[...]
        z1 = _dot(h, wa_ref[0, 0])
        z2 = _dot(h, wb_ref[0, 0])
        z1_ref[0] = z1.astype(bf16)
        z2_ref[0] = z2.astype(bf16)
        a = (z1 * _sigmoid(z1) * z2).astype(bf16)
        o_ref[...] += _dot(a, w2_ref[0, 0])

    tb = 2 * TB
    sh3 = pl.BlockSpec((1, tb, NSH_FF), lambda i, s: (s, i, 0))
    return pl.pallas_call(
        body, name="ffn_fwd", grid=(n // tb, 4),
        in_specs=[pl.BlockSpec((tb, D), lambda i, s: (i, 0)),
                  pl.BlockSpec((1, D), lambda i, s: (0, 0)),
                  pl.BlockSpec((1, 1, D, NSH_FF), lambda i, s: (layer, s, 0, 0)),
                  pl.BlockSpec((1, 1, D, NSH_FF), lambda i, s: (layer, s + 4, 0, 0)),
                  pl.BlockSpec((1, 1, NSH_FF, D), lambda i, s: (layer, s, 0, 0))],
        out_specs=[pl.BlockSpec((tb, D), lambda i, s: (i, 0)), sh3, sh3, pl.BlockSpec((tb, D), lambda i, s: (i, 0))],
        out_shape=[jax.ShapeDtypeStruct((n, D), f32), jax.ShapeDtypeStruct((4, n, NSH_FF), bf16),
                   jax.ShapeDtypeStruct((4, n, NSH_FF), bf16), jax.ShapeDtypeStruct((n, D), bf16)],
        compiler_params=_cp(2))(x1, g2, w1, w1, w2p)


def ffn_bwd(h, dx2, z1s, z2s, w1, w2, layer):
    n = h.shape[0]
    w2p = w2.reshape(w2.shape[0], 4, NSH_FF, D)
    nblk = n // TB

    def body(h_ref, dy_ref, z1_ref, z2_ref, wa_ref, wb_ref, w2_ref, dh_ref, dwa_ref, dwb_ref, dw2_ref,
             acc_a, acc_b, acc_2):
        i = pl.program_id(1)

        @pl.when(i == 0)
        def _():
            acc_a[...] = jnp.zeros_like(acc_a)
            acc_b[...] = jnp.zeros_like(acc_b)
            acc_2[...] = jnp.zeros_like(acc_2)

        hv = h_ref[...]
        dyb = dy_ref[...].astype(bf16)
        z1 = z1_ref[0].astype(f32)
        z2 = z2_ref[0].astype(f32)
        sg = _sigmoid(z1)
        sl = z1 * sg
        a = (sl * z2).astype(bf16)
        da = _dot_nt(dyb, w2_ref[0, 0])
        dz2 = (da * sl).astype(bf16)
        dz1 = (da * z2 * sg * (1.0 + z1 * (1.0 - sg))).astype(bf16)
        dh_ref[0] = (_dot_nt(dz1, wa_ref[0, 0]) + _dot_nt(dz2, wb_ref[0, 0])).astype(bf16)
        acc_a[...] += _dot_tn(hv, dz1)
        acc_b[...] += _dot_tn(hv, dz2)
        acc_2[...] += _dot_tn(a, dyb)

        @pl.when(i == nblk - 1)
        def _():
            dwa_ref[0] = acc_a[...].astype(bf16)
            dwb_ref[0] = acc_b[...].astype(bf16)
            dw2_ref[0] = acc_2[...].astype(bf16)

    tok = pl.BlockSpec((TB, D), lambda s, i: (i, 0))
    sh3 = pl.BlockSpec((1, TB, NSH_FF), lambda s, i: (s, i, 0))
    return pl.pallas_call(
        body, name="ffn_bwd", grid=(4, nblk),
        in_specs=[tok, tok, sh3, sh3,
                  pl.BlockSpec((1, 1, D, NSH_FF), lambda s, i: (layer, s, 0, 0)),
                  pl.BlockSpec((1, 1, D, NSH_FF), lambda s, i: (layer, s + 4, 0, 0)),
                  pl.BlockSpec((1, 1, NSH_FF, D), lambda s, i: (layer, s, 0, 0))],
        out_specs=[pl.BlockSpec((1, TB, D), lambda s, i: (s, i, 0)),
                   pl.BlockSpec((1, D, NSH_FF), lambda s, i: (s, 0, 0)),
                   pl.BlockSpec((1, D, NSH_FF), lambda s, i: (s, 0, 0)),
                   pl.BlockSpec((1, NSH_FF, D), lambda s, i: (s, 0, 0))],
        out_shape=[jax.ShapeDtypeStruct((4, n, D), bf16), jax.ShapeDtypeStruct((4, D, NSH_FF), bf16),
                   jax.ShapeDtypeStruct((4, D, NSH_FF), bf16), jax.ShapeDtypeStruct((4, NSH_FF, D), bf16)],
        scratch_shapes=[pltpu.VMEM((D, NSH_FF), f32), pltpu.VMEM((D, NSH_FF), f32), pltpu.VMEM((NSH_FF, D), f32)],
        compiler_params=_cp(2, 56))(h, dx2, z1s, z2s, w1, w1, w2p)


def norm_bwd_fin(x, g, dres, dh_parts, name):
    n = x.shape[0]
    nparts = dh_parts.shape[0]

    def body(x_ref, g_ref, dy_ref, dh_ref, dx_ref, dg_ref):
        i = pl.program_id(0)

        @pl.when(i == 0)
        def _():
            dg_ref[...] = jnp.zeros_like(dg_ref)

        dh = dh_ref[0].astype(f32)
        for s in range(1, nparts):
            dh = dh + dh_ref[s].astype(f32)
        dx, dg = _rms_bwd(x_ref[...], g_ref[...], dh)
        dx_ref[...] = dy_ref[...] + dx
        dg_ref[...] += dg

    tok = pl.BlockSpec((TB, D), lambda i: (i, 0))
    return pl.pallas_call(
        body, name=name, grid=(n // TB,),
        in_specs=[tok, pl.BlockSpec((1, D), lambda i: (0, 0)), tok, pl.BlockSpec((nparts, TB, D), lambda i: (0, i, 0))],
        out_specs=[tok, pl.BlockSpec((1, D), lambda i: (0, 0))],
        out_shape=[jax.ShapeDtypeStruct((n, D), f32), jax.ShapeDtypeStruct((1, D), f32)],
        compiler_params=_cp(1))(x, g, dres, dh_parts)


def loss_head(x, g, target):
    n = x.shape[0]

    def body(x_ref, g_ref, t_ref, l_ref, dx_ref, dg_ref):
        i = pl.program_id(0)

        @pl.when(i == 0)
        def _():
            l_ref[...] = jnp.zeros_like(l_ref)
            dg_ref[...] = jnp.zeros_like(dg_ref)

        xv = x_ref[...]
        y = xv * _rms_stats(xv) * g_ref[...]
        e = y - t_ref[...]
        l_ref[...] += 0.5 * jnp.sum(jnp.sum(e * e, axis=-1, keepdims=True), axis=0, keepdims=True) * (1.0 / D)
        dx, dg = _rms_bwd(xv, g_ref[...], e * (1.0 / D))
        dx_ref[...] = dx
        dg_ref[...] += dg

    tok = lambda i: (i, 0)
    return pl.pallas_call(
        body, name="loss_head", grid=(n // TB,),
        in_specs=[pl.BlockSpec((TB, D), tok), pl.BlockSpec((1, D), lambda i: (0, 0)), pl.BlockSpec((TB, D), tok)],
        out_specs=[pl.BlockSpec((8, 128), lambda i: (0, 0)), pl.BlockSpec((TB, D), tok),
                   pl.BlockSpec((1, D), lambda i: (0, 0))],
        out_shape=[jax.ShapeDtypeStruct((8, 128), f32), jax.ShapeDtypeStruct((n, D), f32),
                   jax.ShapeDtypeStruct((1, D), f32)],
        compiler_params=_cp(1))(x, g, target)


def _disc(lr, li, ld):
    dt = jnp.exp(ld)
    mag = jnp.exp(lr * dt)
    ar = mag * jnp.cos(li * dt)
    ai = mag * jnp.sin(li * dt)
    nr, ni = ar - 1.0, ai
    den = lr * lr + li * li
    zr = (nr * lr + ni * li) / den
    zi = (ni * lr - nr * li) / den
    return ar, ai, zr, zi


def _blockdiag_mask(shape):
    r = lax.broadcasted_iota(jnp.int32, shape, 0) // GH
    c = lax.broadcasted_iota(jnp.int32, shape, 1) // NS
    return r == c


def s5_params(lr, li, ld, btr, bti, ctr, cti):
    def body(lr_ref, li_ref, ld_ref, btr_ref, bti_ref, ctr_ref, cti_ref, t8_ref, bb_ref, cb_ref):
        ar, ai, zr, zi = _disc(lr_ref[...], li_ref[...], ld_ref[...])
        pr_, pi_ = ar, ai
        pw2 = []
        for k in range(4):
            pw2.append((pr_, pi_))
            pr_, pi_ = pr_ * pr_ - pi_ * pi_, 2.0 * pr_ * pi_
        cm = lambda p, q: (p[0] * q[0] - p[1] * q[1], p[0] * q[1] + p[1] * q[0])
        pw = {1: pw2[0], 2: pw2[1], 4: pw2[2], 8: pw2[3]}
        pw[3], pw[5], pw[6] = cm(pw[2], pw[1]), cm(pw[4], pw[1]), cm(pw[4], pw[2])
        pw[7] = cm(pw[4], pw[3])
        row = lax.broadcasted_iota(jnp.int32, (8, NSTATE), 0)
        zero = jnp.zeros((8, NSTATE), f32)
        for c in range(2):
            for k in range(3):
                full = jnp.broadcast_to(pw2[k][c], (8, NSTATE))
                t8_ref[c, k] = jnp.where(row >= (1 << k), full, 0.0)
                t8_ref[c, 3 + k] = jnp.where(row + (1 << k) < 8, full, 0.0)
            up, down = zero, zero
            for j in range(8):
                up = up + jnp.where(row == j, pw[j + 1][c], 0.0)
                down = down + jnp.where(row == j, pw[8 - j][c], 0.0)
            t8_ref[c, 6] = up
            t8_ref[c, 7] = down
        bbr = zr * btr_ref[...] - zi * bti_ref[...]
        bbi = zr * bti_ref[...] + zi * btr_ref[...]
        mask = _blockdiag_mask((SU, SW))
        for j in range(SJ):
            cols = slice(j * SW, (j + 1) * SW)
            for c, (vb, vc) in enumerate(((bbr, ctr_ref[...]), (bbi, cti_ref[...]))):
                bb_ref[c, j] = jnp.where(mask, jnp.tile(vb[:, cols], (SU // GH, 1)), 0.0).astype(bf16)
                cb_ref[c, j] = jnp.where(mask, jnp.tile(vc[:, cols], (SU // GH, 1)), 0.0).astype(bf16)

    return pl.pallas_call(
        body, name="s5_params",
        out_shape=[jax.ShapeDtypeStruct((2, 8, 8, NSTATE), f32),
                   jax.ShapeDtypeStruct((2, SJ, SU, SW), bf16), jax.ShapeDtypeStruct((2, SJ, SU, SW), bf16)],
        compiler_params=pltpu.CompilerParams(vmem_limit_bytes=56 * 1024 * 1024))(lr, li, ld, btr, bti, ctr, cti)


def s5_params_bwd(lr, li, ld, btr, bti, d_a, d_bb, d_cb):
    def body(lr_ref, li_ref, ld_ref, btr_ref, bti_ref, da_ref, dbb_ref, dcb_ref,
             dlr_ref, dli_ref, dld_ref, dbt_ref, dct_ref):
        mask = _blockdiag_mask((SU, SW))

        def fold(ref, c):
            parts = []
            for j in range(SJ):
                v = jnp.where(mask, ref[c, j], 0.0)
                parts.append(v.reshape(SU // GH, GH, SW).sum(axis=0))
            return jnp.concatenate(parts, axis=1)

        dct_ref[0] = fold(dcb_ref, 0)
        dct_ref[1] = fold(dcb_ref, 1)
        dbbr, dbbi = fold(dbb_ref, 0), fold(dbb_ref, 1)
        lrv, liv, ldv = lr_ref[...], li_ref[...], ld_ref[...]
        (ar, ai, zr, zi), vjp = jax.vjp(_disc, lrv, liv, ldv)
        btr, bti = btr_ref[...], bti_ref[...]
        dbt_ref[0] = zr * dbbr + zi * dbbi
        dbt_ref[1] = zr * dbbi - zi * dbbr
        dzr = jnp.sum(dbbr * btr + dbbi * bti, axis=0, keepdims=True)
        dzi = jnp.sum(dbbi * btr - dbbr * bti, axis=0, keepdims=True)
        dlr, dli, dld = vjp((da_ref[0:1, :], da_ref[1:2, :], dzr, dzi))
        dlr_ref[...] = dlr
        dli_ref[...] = dli
        ind = (lax.broadcasted_iota(jnp.int32, (NSTATE, 128), 0) // NS
               == lax.broadcasted_iota(jnp.int32, (NSTATE, 128), 1)).astype(f32)
        dld_ref[...] = _dot_hi(jnp.broadcast_to(dld, (8, NSTATE)), ind)

    return pl.pallas_call(
        body, name="s5_params_bwd",
        out_shape=[jax.ShapeDtypeStruct((1, NSTATE), f32), jax.ShapeDtypeStruct((1, NSTATE), f32),
                   jax.ShapeDtypeStruct((8, 128), f32), jax.ShapeDtypeStruct((2, GH, NSTATE), f32),
                   jax.ShapeDtypeStruct((2, GH, NSTATE), f32)],
        compiler_params=pltpu.CompilerParams(vmem_limit_bytes=56 * 1024 * 1024))(lr, li, ld, btr, bti, d_a, d_bb, d_cb)


def _fma(sr, si, ar, ai, qr, qi):
    return sr + ar * qr - ai * qi, si + ar * qi + ai * qr


def _scan_tile(sr, si, cr, ci, t8_ref, reverse):
    sg = -1.0 if reverse else 1.0
    for k in range(3):
        tk = 3 + k if reverse else k
        rot = 8 - (1 << k) if reverse else 1 << k
        sr, si = _fma(sr, si, t8_ref[0, tk], sg * t8_ref[1, tk], pltpu.roll(sr, rot, 0), pltpu.roll(si, rot, 0))
    tp = 7 if reverse else 6
    sr, si = _fma(sr, si, t8_ref[0, tp], sg * t8_ref[1, tp], cr, ci)
    e = 0 if reverse else 7
    return sr, si, jnp.broadcast_to(sr[e:e + 1, :], sr.shape), jnp.broadcast_to(si[e:e + 1, :], si.shape)


S5MC = 512


def _s5_input_map(u_ref, bb_ref, sr_sc, si_sc, l):
    for c in range(l // S5MC):
        rows = slice(c * S5MC, (c + 1) * S5MC)
        u = u_ref[0, rows, :]
        sr_sc[rows, :] = _dot(u, bb_ref[0, 0])
        si_sc[rows, :] = _dot(u, bb_ref[1, 0])


def _s5_forward_scan(sr_sc, si_sc, t8_ref, l):
    def step(k, carry):
        rows = pl.ds(pl.multiple_of(k * 8, 8), 8)
        sr, si, cr, ci = _scan_tile(sr_sc[rows, :], si_sc[rows, :], carry[0], carry[1], t8_ref, False)
        sr_sc[rows, :] = sr
        si_sc[rows, :] = si
        return cr, ci

    zero = jnp.zeros((8, SW), f32)
    lax.fori_loop(0, l // 8, step, (zero, zero), unroll=4)


def s5_fwd(proj3, t8, bb, cb, dskip):
    b, l, _ = proj3.shape

    def body(u_ref, t8_ref, bb_ref, cb_ref, d_ref, y_ref, sr_out, si_out):
        sr_sc, si_sc = sr_out.at[0], si_out.at[0]
        _s5_input_map(u_ref, bb_ref, sr_sc, si_sc, l)
        _s5_forward_scan(sr_sc, si_sc, t8_ref, l)
        for c in range(l // S5MC):
            rows = slice(c * S5MC, (c + 1) * S5MC)
            y = (_dot_nt(sr_sc[rows, :].astype(bf16), cb_ref[0, 0])
                 - _dot_nt(si_sc[rows, :].astype(bf16), cb_ref[1, 0]))
            y_ref[0, rows, :] = y + d_ref[...] * u_ref[0, rows, :].astype(f32)

    return pl.pallas_call(
        body, name="s5_fwd", grid=(SJ, b),
        in_specs=[pl.BlockSpec((1, l, SU), lambda j, bi: (bi, 0, j)),
                  pl.BlockSpec((2, 8, 8, SW), lambda j, bi: (0, 0, 0, j)),
                  pl.BlockSpec((2, 1, SU, SW), lambda j, bi: (0, j, 0, 0)),
                  pl.BlockSpec((2, 1, SU, SW), lambda j, bi: (0, j, 0, 0)),
                  pl.BlockSpec((1, SU), lambda j, bi: (0, j))],
        out_specs=[pl.BlockSpec((1, l, SU), lambda j, bi: (bi, 0, j)),
                   pl.BlockSpec((1, l, SW), lambda j, bi: (bi, 0, j)),
                   pl.BlockSpec((1, l, SW), lambda j, bi: (bi, 0, j))],
        out_shape=[jax.ShapeDtypeStruct((b, l, BW), f32), jax.ShapeDtypeStruct((b, l, NSTATE), f32),
                   jax.ShapeDtypeStruct((b, l, NSTATE), f32)],
        compiler_params=_cp(2))(proj3, t8, bb, cb, dskip)


def s5_bwd(proj3, dy, s_re, s_im, t8, bb, cb, dskip):
    b, l, _ = proj3.shape
    nt = l // 8

    def body(u_ref, dy_ref, sr_in, si_in, t8_ref, bb_ref, cb_ref, d_ref,
             du_ref, da_ref, dbb_ref, dcb_ref, dd_ref, gr_sc, gi_sc):
        bi = pl.program_id(1)
        sr_sc, si_sc = sr_in.at[0], si_in.at[0]

        @pl.when(bi == 0)
        def _():
            da_ref[...] = jnp.zeros_like(da_ref)
            dbb_ref[...] = jnp.zeros_like(dbb_ref)
            dcb_ref[...] = jnp.zeros_like(dcb_ref)
            dd_ref[...] = jnp.zeros_like(dd_ref)

        for c in range(l // S5MC):
            rows = slice(c * S5MC, (c + 1) * S5MC)
            dyb = dy_ref[0, rows, :].astype(bf16)
            gr_sc[rows, :] = _dot(dyb, cb_ref[0, 0])
            gi_sc[rows, :] = -_dot(dyb, cb_ref[1, 0])

        row = lax.broadcasted_iota(jnp.int32, (8, SW), 0)

        def step(i, carry):
            cr, ci, dar, dai = carry
            k = nt - 1 - i
            rows = pl.ds(pl.multiple_of(k * 8, 8), 8)
            gr, gi, cr, ci = _scan_tile(gr_sc[rows, :], gi_sc[rows, :], cr, ci, t8_ref, True)
            gr_sc[rows, :] = gr
            gi_sc[rows, :] = gi
            before = pl.ds(pl.multiple_of(jnp.maximum(k - 1, 0) * 8, 8), 8)
            live = jnp.where(k > 0, 1.0, 0.0)
            sr, si = sr_sc[rows, :], si_sc[rows, :]
            spr = jnp.where(row == 0, live * sr_sc[before, :][7:8, :], pltpu.roll(sr, 1, 0))
            spi = jnp.where(row == 0, live * si_sc[before, :][7:8, :], pltpu.roll(si, 1, 0))
            return cr, ci, dar + spr * gr + spi * gi, dai + spr * gi - spi * gr

        zero = jnp.zeros((8, SW), f32)
        _, _, dar, dai = lax.fori_loop(0, nt, step, (zero, zero, zero, zero), unroll=2)
        da_ref[0:1, :] += jnp.sum(dar, axis=0, keepdims=True)
        da_ref[1:2, :] += jnp.sum(dai, axis=0, keepdims=True)

        for c in range(l // S5MC):
            rows = slice(c * S5MC, (c + 1) * S5MC)
            u = u_ref[0, rows, :]
            dyv = dy_ref[0, rows, :]
            dyb = dyv.astype(bf16)
            grb, gib = gr_sc[rows, :].astype(bf16), gi_sc[rows, :].astype(bf16)
            dcb_ref[0, 0] += _dot_tn(dyb, sr_sc[rows, :].astype(bf16))
            dcb_ref[1, 0] -= _dot_tn(dyb, si_sc[rows, :].astype(bf16))
            dbb_ref[0, 0] += _dot_tn(u, grb)
            dbb_ref[1, 0] += _dot_tn(u, gib)
            du = _dot_nt(grb, bb_ref[0, 0]) + _dot_nt(gib, bb_ref[1, 0]) + d_ref[...] * dyv
            du_ref[0, rows, :] = du.astype(bf16)
            dd_ref[...] += jnp.sum(dyv * u.astype(f32), axis=0, keepdims=True)

    seq = pl.BlockSpec((1, l, SU), lambda j, bi: (bi, 0, j))
    sts = pl.BlockSpec((1, l, SW), lambda j, bi: (bi, 0, j))
    tab = pl.BlockSpec((2, 1, SU, SW), lambda j, bi: (0, j, 0, 0))
    return pl.pallas_call(
        body, name="s5_bwd", grid=(SJ, b),
        in_specs=[seq, seq, sts, sts, pl.BlockSpec((2, 8, 8, SW), lambda j, bi: (0, 0, 0, j)), tab, tab,
                  pl.BlockSpec((1, SU), lambda j, bi: (0, j))],
        out_specs=[seq, pl.BlockSpec((2, SW), lambda j, bi: (0, j)), tab, tab,
                   pl.BlockSpec((1, SU), lambda j, bi: (0, j))],
        out_shape=[jax.ShapeDtypeStruct((b, l, BW), bf16), jax.ShapeDtypeStruct((2, NSTATE), f32),
                   jax.ShapeDtypeStruct((2, SJ, SU, SW), f32), jax.ShapeDtypeStruct((2, SJ, SU, SW), f32),
                   jax.ShapeDtypeStruct((1, BW), f32)],
        scratch_shapes=[pltpu.VMEM((l, SW), f32)] * 2,
        compiler_params=_cp(2))(proj3, dy, s_re, s_im, t8, bb, cb, dskip)


AHC = 2
AHW = AHC * 128


def _att_mask(n, nb):
    if nb == 1:
        qi = lax.broadcasted_iota(jnp.int32, (ABLK, ABLK), 0)
        kj = lax.broadcasted_iota(jnp.int32, (ABLK, ABLK), 1)
        return kj <= qi
    qi = lax.broadcasted_iota(jnp.int32, (ABLK, 2 * ABLK), 0)
    kj = lax.broadcasted_iota(jnp.int32, (ABLK, 2 * ABLK), 1)
    return (kj >= qi) & (kj <= qi + ABLK) & ((n > 0) | (kj >= ABLK))


def _att_rows(it, nb, dil):
    r, n = it // nb, it % nb
    cur = pl.ds(r + n * (ABLK * dil), ABLK, stride=dil)
    prv = pl.ds(r + jnp.maximum(n - 1, 0) * (ABLK * dil), ABLK, stride=dil)
    return n, cur, prv


def _att_keys(ref, c, cur, prv, nb):
    if nb == 1:
        x = ref[c, cur, :].astype(bf16)
    else:
        x = jnp.concatenate([ref[c, prv, :], ref[c, cur, :]], axis=0).astype(bf16)
    head0 = lax.broadcasted_iota(jnp.int32, x.shape, 1) < HD
    zero = jnp.zeros_like(x)
    return jnp.concatenate([jnp.where(head0, x, zero), jnp.where(head0, zero, x)], axis=0)


def _per_head(nk, a0, a1):
    col = lax.broadcasted_iota(jnp.int32, (ABLK, 2 * nk), 1)
    return jnp.where(col < nk, a0, a1)


def _to_chunks(src_ref, dst):
    for c in range(AHC):
        dst[c] = src_ref[0, :, c * 128:(c + 1) * 128].astype(f32)


def att_fwd(proj3, g_idx, dil):
    b, l, _ = proj3.shape
    nb = l // dil // ABLK
    nhalf = BW // AHW

    def body(q_ref, k_ref, v_ref, o_ref, lse_ref, qf, kf, vf, of):
        hh = pl.program_id(1)
        _to_chunks(q_ref, qf)
        _to_chunks(k_ref, kf)
        _to_chunks(v_ref, vf)
        lane = lax.broadcasted_iota(jnp.int32, (ABLK, 128), 1)

        def step(it, carry):
            n, cur, prv = _att_rows(it, nb, dil)
            valid = _att_mask(n, nb)
            nk = valid.shape[1]
            lse_all = jnp.zeros((ABLK, 128), f32)
            for c in range(AHC):
                q = (qf[c, cur, :] * ATT_SCALE).astype(bf16)
                k = _att_keys(kf, c, cur, prv, nb)
                v = _att_keys(vf, c, cur, prv, nb)
                head = hh * (2 * AHC) + 2 * c
                if nb == 1:
                    v2 = jnp.concatenate([valid, valid], axis=1)
                    s = jnp.where(v2, _dot_nt(q, k), NEG)
                    m0 = jnp.max(s[:, :nk], axis=-1, keepdims=True)
                    m1 = jnp.max(s[:, nk:], axis=-1, keepdims=True)
                    p = jnp.exp(s - _per_head(nk, m0, m1))
                    den0 = jnp.sum(p[:, :nk], axis=-1, keepdims=True)
                    den1 = jnp.sum(p[:, nk:], axis=-1, keepdims=True)
                    of[c, cur, :] = _dot(p.astype(bf16), v) * jnp.where(lane < HD, 1.0 / den0, 1.0 / den1)
                    lse_all = (lse_all + jnp.where(lane == head, m0 + jnp.log(den0), 0.0)
                               + jnp.where(lane == head + 1, m1 + jnp.log(den1), 0.0))
                    continue
                acc = None
                for hl in range(2):
                    s = jnp.where(valid, _dot_nt(q, k[hl * nk:(hl + 1) * nk]), NEG)
                    m = jnp.max(s, axis=-1, keepdims=True)
                    p = jnp.exp(s - m)
                    den = jnp.sum(p, axis=-1, keepdims=True)
                    o_h = _dot(p.astype(bf16), v[hl * nk:(hl + 1) * nk]) * (1.0 / den)
                    acc = o_h if acc is None else acc + o_h
                    lse_all = lse_all + jnp.where(lane == head + hl, m + jnp.log(den), 0.0)
                of[c, cur, :] = acc

            lse_ref[0, 0, cur, :] = lse_all
            return carry

        lax.fori_loop(0, dil * nb, step, 0, unroll=4)
        for c in range(AHC):
            o_ref[0, :, c * 128:(c + 1) * 128] = of[c].astype(bf16)

    col = lambda c: pl.BlockSpec((1, l, AHW), lambda bi, hh: (bi, 0, c * nhalf + hh))
    return pl.pallas_call(
        body, name=f"att_fwd{g_idx}", grid=(b, nhalf),
        in_specs=[col(1 + g_idx), col(4), col(5)],
        out_specs=[pl.BlockSpec((1, l, AHW), lambda bi, hh: (bi, 0, hh)),
                   pl.BlockSpec((1, 1, l, 128), lambda bi, hh: (bi, hh, 0, 0))],
        out_shape=[jax.ShapeDtypeStruct((b, l, BW), bf16), jax.ShapeDtypeStruct((b, nhalf, l, 128), f32)],
        scratch_shapes=[pltpu.VMEM((AHC, l, 128), f32)] * 4,
        compiler_params=_cp(2))(proj3, proj3, proj3)


def att_bwd(proj3, do, lse_tot, delta, g_idx, dil):
    b, l, _ = proj3.shape
    nb = l // dil // ABLK
    nhalf = BW // AHW

    def body(q_ref, k_ref, v_ref, do_ref, l_ref, dl_ref, dq_out, dk_out, dv_out, qf, kf, vf, dof,
             dq_ref, dk_ref, dv_ref):
        hh = pl.program_id(1)
        _to_chunks(q_ref, qf)
        _to_chunks(k_ref, kf)
        _to_chunks(v_ref, vf)
        _to_chunks(do_ref, dof)
        dk_ref[...] = jnp.zeros_like(dk_ref)
        dv_ref[...] = jnp.zeros_like(dv_ref)
        lane = lax.broadcasted_iota(jnp.int32, (ABLK, 128), 1)

        def step(it, carry):
            n, cur, prv = _att_rows(it, nb, dil)
            valid = _att_mask(n, nb)
            valid = jnp.concatenate([valid, valid], axis=1)
            nk = valid.shape[1] // 2
            lse_b = l_ref[0, cur, :]
            dl_b = dl_ref[0, cur, :]
            head0 = lax.broadcasted_iota(jnp.int32, (nk, 128), 1) < HD
            for c in range(AHC):
                q = (qf[c, cur, :] * ATT_SCALE).astype(bf16)
                dob = dof[c, cur, :].astype(bf16)
                k = _att_keys(kf, c, cur, prv, nb)
                v = _att_keys(vf, c, cur, prv, nb)
                head = hh * (2 * AHC) + 2 * c
                pick = lambda a, h: jnp.sum(jnp.where(lane == h, a, 0.0), axis=-1, keepdims=True)
                lse_h = _per_head(nk, pick(lse_b, head), pick(lse_b, head + 1))
                dl_h = _per_head(nk, pick(dl_b, head), pick(dl_b, head + 1))
                s = _dot_nt(q, k)
                p = jnp.where(valid, jnp.exp(jnp.minimum(s - lse_h, 60.0)), 0.0)
                ds = (p * (_dot_nt(dob, v) - dl_h)).astype(bf16)
                dq_ref[0, c, cur, :] = _dot(ds, k) * ATT_SCALE
                dk2 = _dot_tn(ds, q)
                dv2 = _dot_tn(p.astype(bf16), dob)
                dk = jnp.where(head0, dk2[:nk], dk2[nk:])
                dv = jnp.where(head0, dv2[:nk], dv2[nk:])
                if nb == 1:
                    dk_ref[0, c, cur, :] += dk
                    dv_ref[0, c, cur, :] += dv
                else:
                    dk_ref[0, c, cur, :] += dk[ABLK:]
                    dv_ref[0, c, cur, :] += dv[ABLK:]
                    dk_ref[0, c, prv, :] += dk[:ABLK]
                    dv_ref[0, c, prv, :] += dv[:ABLK]

            return carry

        lax.fori_loop(0, dil * nb, step, 0, unroll=4)
        dq_out[0] = dq_ref[0].astype(bf16)
        dk_out[0] = dk_ref[0].astype(bf16)
        dv_out[0] = dv_ref[0].astype(bf16)

    col = lambda c: pl.BlockSpec((1, l, AHW), lambda bi, hh: (bi, 0, c * nhalf + hh))
    own = pl.BlockSpec((1, l, AHW), lambda bi, hh: (bi, 0, hh))
    own128 = pl.BlockSpec((1, l, 128), lambda bi, hh: (bi, 0, 0))
    chunked = pl.BlockSpec((1, AHC, l, 128), lambda bi, hh: (bi, hh, 0, 0))
    return pl.pallas_call(
        body, name=f"att_bwd{g_idx}", grid=(b, nhalf),
        in_specs=[col(1 + g_idx), col(4), col(5), own, own128, own128],
        out_specs=[chunked] * 3,
        out_shape=[jax.ShapeDtypeStruct((b, BW // 128, l, 128), bf16)] * 3,
        scratch_shapes=[pltpu.VMEM((AHC, l, 128), f32)] * 4 + [pltpu.VMEM((1, AHC, l, 128), f32)] * 3,
        compiler_params=_cp(2, 56))(proj3, proj3, proj3, do, lse_tot, delta)


CPAD = 32
CTAIL = 16
CR = 128
CSLAB = CR + 40


def _tap_windows(slab, off, mis):
    ntap = (CW - 1 - mis) // 8 + 1
    rot = (off + mis) % 8
    base = off + mis - rot
    shifted = pltpu.roll(slab, CSLAB - rot, 0) if rot else slab
    for a in range(ntap):
        yield 8 * a + mis, shifted[base + 8 * a:base + 8 * a + CR]


def _fill_glu(cv_ref, pad, l):
    pad[0:CPAD, :] = jnp.zeros((CPAD, BW), f32)
    pad[CPAD:CPAD + l, :] = cv_ref[0, :, :BW].astype(f32) * _sigmoid(cv_ref[0, :, BW:].astype(f32))
    pad[CPAD + l:, :] = jnp.zeros((CTAIL, BW), f32)


def conv_fwd(proj3, cw, cb):
    b, l, _ = proj3.shape

    def body(cv_ref, w_ref, b_ref, o_ref, pad):
        _fill_glu(cv_ref, pad, l)
        for lc in range(BW // 128):
            lanes = slice(lc * 128, (lc + 1) * 128)
            wv = w_ref[:, lanes]

            def step(c, carry):
                base = pl.multiple_of(c * CR, CR)
                slab = pad[pl.ds(base, CSLAB), lanes]
                acc = jnp.zeros((CR, 128), f32) + b_ref[:, lanes]
                for mis in range(8):
                    for k, win in _tap_windows(slab, CPAD - (CW - 1), mis):
                        acc = acc + wv[k:k + 1] * win
                o_ref[0, pl.ds(base, CR), lanes] = acc
                return carry

            lax.fori_loop(0, l // CR, step, 0)

    return pl.pallas_call(
        body, name="conv_fwd", grid=(b,),
        in_specs=[pl.BlockSpec((1, l, 2 * BW), lambda i: (i, 0, 3)),
                  pl.BlockSpec((32, BW), lambda i: (0, 0)), pl.BlockSpec((1, BW), lambda i: (0, 0))],
        out_specs=pl.BlockSpec((1, l, BW), lambda i: (i, 0, 0)),
        out_shape=jax.ShapeDtypeStruct((b, l, BW), f32),
        scratch_shapes=[pltpu.VMEM((CPAD + l + CTAIL, BW), f32)], compiler_params=_cp(1))(proj3, cw, cb)


def conv_bwd(proj3, dhc, cw):
    b, l, _ = proj3.shape

    def body(cv_ref, d_ref, w_ref, dcv_ref, dw_ref, db_ref, pad, dpad):
        i = pl.program_id(0)

        @pl.when(i == 0)
        def _():
            dw_ref[...] = jnp.zeros_like(dw_ref)
            db_ref[...] = jnp.zeros_like(db_ref)

        _fill_glu(cv_ref, pad, l)
        dpad[0:l, :] = d_ref[0]
        dpad[l:, :] = jnp.zeros((CPAD + CTAIL, BW), f32)
        db_ref[...] += jnp.sum(d_ref[0], axis=0, keepdims=True)
        for lc in range(BW // 128):
            lanes = slice(lc * 128, (lc + 1) * 128)
            glanes = slice(BW + lc * 128, BW + (lc + 1) * 128)
            wv = w_ref[:, lanes]

            for mis in range(8):
                ntap = (CW - 1 - mis) // 8 + 1

                def dw_step(c, accs, mis=mis, lanes=lanes):
                    base = pl.multiple_of(c * CR, CR)
                    slab = pad[pl.ds(base, CSLAB), lanes]
                    dv = dpad[pl.ds(base, CR), lanes]
                    return tuple(acc + (dv * win).reshape(CR // 8, 8, 128).sum(axis=0) for acc, (_, win)
                                 in zip(accs, _tap_windows(slab, CPAD - (CW - 1), mis)))

                accs = lax.fori_loop(0, l // CR, dw_step, tuple(jnp.zeros((8, 128), f32) for _ in range(ntap)))
                for a in range(ntap):
                    k = 8 * a + mis
                    dw_ref[k:k + 1, lanes] += jnp.sum(accs[a], axis=0, keepdims=True)

            def dh_step(c, carry, lanes=lanes, glanes=glanes, wv=wv):
                base = pl.multiple_of(c * CR, CR)
                slab = dpad[pl.ds(base, CSLAB), lanes]
                acc = jnp.zeros((CR, 128), f32)
                for mis in range(8):
                    for kk, win in _tap_windows(slab, 0, mis):
                        acc = acc + wv[CW - 1 - kk:CW - kk] * win
                rows = pl.ds(base, CR)
                a = cv_ref[0, rows, lanes].astype(f32)
                sg = _sigmoid(cv_ref[0, rows, glanes].astype(f32))
                dcv_ref[0, rows, lanes] = (acc * sg).astype(bf16)
                dcv_ref[0, rows, glanes] = (acc * a * sg * (1.0 - sg)).astype(bf16)
                return carry

            lax.fori_loop(0, l // CR, dh_step, 0)

    return pl.pallas_call(
        body, name="conv_bwd", grid=(b,),
        in_specs=[pl.BlockSpec((1, l, 2 * BW), lambda i: (i, 0, 3)),
                  pl.BlockSpec((1, l, BW), lambda i: (i, 0, 0)),
                  pl.BlockSpec((32, BW), lambda i: (0, 0))],
        out_specs=[pl.BlockSpec((1, l, 2 * BW), lambda i: (i, 0, 0)),
                   pl.BlockSpec((32, BW), lambda i: (0, 0)), pl.BlockSpec((1, BW), lambda i: (0, 0))],
        out_shape=[jax.ShapeDtypeStruct((b, l, 2 * BW), bf16), jax.ShapeDtypeStruct((32, BW), f32),
                   jax.ShapeDtypeStruct((1, BW), f32)],
        scratch_shapes=[pltpu.VMEM((CPAD + l + CTAIL, BW), f32), pltpu.VMEM((l + CPAD + CTAIL, BW), f32)],
        compiler_params=_cp(1))(proj3, dhc, cw)


def _head_expand():
    r = lax.broadcasted_iota(jnp.int32, (128, BW), 0)
    c = lax.broadcasted_iota(jnp.int32, (128, BW), 1) // HD
    return (r == c).astype(f32)


def _head_reduce():
    r = lax.broadcasted_iota(jnp.int32, (BW, 128), 0) // HD
    c = lax.broadcasted_iota(jnp.int32, (BW, 128), 1)
    return (r == c).astype(f32)


def _merge_common(ys_ref, o_refs, l_refs, hc_ref, g_refs, bg_ref, lng_ref, lnb_ref, wglu_ref, watt_ref, wpw_ref):
    r = {}
    ysv = ys_ref[...]
    r["ys"] = ysv
    r["ysin"] = _gelu(ysv).astype(bf16)
    z = _dot(r["ysin"], wglu_ref[...])
    r["z1"], r["sg2"] = z[:, :D], _sigmoid(z[:, D:])
    r["y_s"] = r["z1"] * r["sg2"]
    ls = [lr_[0, 0] + lr_[0, 1] for lr_ in l_refs]
    mx = jnp.maximum(jnp.maximum(ls[0], ls[1]), ls[2])
    es = [jnp.exp(v - mx) for v in ls]
    tot = es[0] + es[1] + es[2]
    r["lse_tot"] = mx + jnp.log(tot)
    e_mat = _head_expand()
    o = jnp.zeros(ysv.shape, f32)
    for e, o_ref in zip(es, o_refs):
        o = o + _dot_sel(e / tot, e_mat) * o_ref[...].astype(f32)
    r["o"] = o
    r["ob"] = o.astype(bf16)
    r["y_a"] = _dot(r["ob"], watt_ref[...])
    hc = hc_ref[...]
    mu = jnp.mean(hc, axis=-1, keepdims=True)
    xc = hc - mu
    rstd = lax.rsqrt(jnp.mean(xc * xc, axis=-1, keepdims=True) + EPS)
    r["xh"], r["rstd"] = xc * rstd, rstd
    hn = r["xh"] * lng_ref[...] + lnb_ref[...]
    r["hn"] = hn
    r["sgn"] = _sigmoid(hn)
    r["hs"] = (hn * r["sgn"]).astype(bf16)
    r["y_c"] = _dot(r["hs"], wpw_ref[...])
    r["gates"] = [_sigmoid(g_refs[k][...].astype(f32) + bg_ref[:, k * D:(k + 1) * D]) for k in range(3)]
    r["merged"] = r["gates"][0] * r["y_s"] + r["gates"][1] * r["y_a"] + r["gates"][2] * r["y_c"]
    return r


TBM = 256


def _merge_in_specs(tok, tb, lses):
    w = lambda shape: pl.BlockSpec(shape, lambda i: (0, 0), pipeline_mode=pl.Buffered(1))
    nbl = lses[0].shape[2] // tb
    return ([pl.BlockSpec((tb, D), tok), pl.BlockSpec((tb, BW), tok)]
            + [pl.BlockSpec((tb, BW), tok)] * 3
            + [pl.BlockSpec((1, 2, tb, 128), lambda i: (i // nbl, 0, i % nbl, 0))] * 3
            + [pl.BlockSpec((tb, BW), tok)]
            + [pl.BlockSpec((tb, D), lambda i, k=k: (i, 4 + k)) for k in range(3)]
            + [w((1, 3 * D)), w((1, BW)), w((1, BW)), w((BW, 2 * D)), w((BW, D)), w((BW, D)), w((D, D))])


def merge_fwd(x, ys, os_, lses, hc, proj, bg, lng, lnb, wglu, watt, wpw, wout):
    n = x.shape[0]

    def body(x_ref, ys_ref, o1, o2, o3, l1, l2, l3, hc_ref, g0, g1, g2, bg_ref, lng_ref, lnb_ref,
             wglu_ref, watt_ref, wpw_ref, wout_ref, x1_ref):
        r = _merge_common(ys_ref, (o1, o2, o3), (l1, l2, l3), hc_ref, (g0, g1, g2), bg_ref, lng_ref, lnb_ref,
                          wglu_ref, watt_ref, wpw_ref)
        x1_ref[...] = x_ref[...] + _dot(r["merged"].astype(bf16), wout_ref[...])

    tok = lambda i: (i, 0)
    return pl.pallas_call(
        body, name="merge_fwd", grid=(n // TB,), in_specs=_merge_in_specs(tok, TB, lses),
        out_specs=pl.BlockSpec((TB, D), tok), out_shape=jax.ShapeDtypeStruct((n, D), f32),
        compiler_params=_cp(1, 56))(x, ys, *os_, *lses, hc, proj, proj, proj, bg, lng, lnb, wglu, watt, wpw, wout)


def merge_bwd(dx1, ys, os_, lses, hc, proj, bg, lng, lnb, wglu, watt, wpw, wout):
    n = dx1.shape[0]

    def body(dx_ref, ys_ref, o1, o2, o3, l1, l2, l3, hc_ref, g0, g1, g2, bg_ref, lng_ref, lnb_ref,
             wglu_ref, watt_ref, wpw_ref, wout_ref,
             dys_ref, do_ref, delta_ref, ltot_ref, dhc_ref, dgate_ref, ysin_ref, dz_ref, ob_ref, dya_ref,
             hs_ref, dyc_ref, mg_ref, dbg_ref, dlng_ref, dlnb_ref):
        i = pl.program_id(0)

        @pl.when(i == 0)
        def _():
            dbg_ref[...] = jnp.zeros_like(dbg_ref)
            dlng_ref[...] = jnp.zeros_like(dlng_ref)
            dlnb_ref[...] = jnp.zeros_like(dlnb_ref)

        r = _merge_common(ys_ref, (o1, o2, o3), (l1, l2, l3), hc_ref, (g0, g1, g2), bg_ref, lng_ref, lnb_ref,
                          wglu_ref, watt_ref, wpw_ref)
        mg_ref[...] = r["merged"].astype(bf16)
        ysin_ref[...] = r["ysin"]
        ob_ref[...] = r["ob"]
        hs_ref[...] = r["hs"]
        ltot_ref[...] = r["lse_tot"]
        dm = _dot_nt(dx_ref[...].astype(bf16), wout_ref[...])
        ys3 = (r["y_s"], r["y_a"], r["y_c"])
        for k in range(3):
            gk = r["gates"][k]
            dgr = dm * ys3[k] * gk * (1.0 - gk)
            dgate_ref[:, k * D:(k + 1) * D] = dgr.astype(bf16)
            dbg_ref[:, k * D:(k + 1) * D] += jnp.sum(dgr, axis=0, keepdims=True)
        dy_s = dm * r["gates"][0]
        sg2 = r["sg2"]
        dz = jnp.concatenate([dy_s * sg2, dy_s * r["z1"] * sg2 * (1.0 - sg2)], axis=1).astype(bf16)
        dz_ref[...] = dz
        dys_ref[...] = _dot_nt(dz, wglu_ref[...]) * _gelu_grad(r["ys"])
        dya = (dm * r["gates"][1]).astype(bf16)
        dya_ref[...] = dya
        do = _dot_nt(dya, watt_ref[...])
        do_ref[...] = do.astype(bf16)
        delta_ref[...] = _dot_sel(do * r["o"], _head_reduce())
        dyc = (dm * r["gates"][2]).astype(bf16)
        dyc_ref[...] = dyc
        sgn, hn = r["sgn"], r["hn"]
        dhn = _dot_nt(dyc, wpw_ref[...]) * sgn * (1.0 + hn * (1.0 - sgn))
        dlng_ref[...] += jnp.sum(dhn * r["xh"], axis=0, keepdims=True)
        dlnb_ref[...] += jnp.sum(dhn, axis=0, keepdims=True)
        dxh = dhn * lng_ref[...]
        xh = r["xh"]
        dhc_ref[...] = r["rstd"] * (dxh - jnp.mean(dxh, axis=-1, keepdims=True)
                                    - xh * jnp.mean(dxh * xh, axis=-1, keepdims=True))

    tok = lambda i: (i, 0)
    fix = lambda i: (0, 0)
    outs = [("dys", BW, f32), ("do", BW, bf16), ("delta", 128, f32), ("lse_tot", 128, f32), ("dhc", BW, f32),
            ("dgate", 3 * D, bf16), ("ysin", BW, bf16), ("dz", 2 * D, bf16), ("ob", BW, bf16), ("dya", D, bf16),
            ("hs", BW, bf16), ("dyc", D, bf16), ("merged", D, bf16)]
    small = [("dbg", 3 * D), ("dlng", BW), ("dlnb", BW)]
    res = pl.pallas_call(
        body, name="merge_bwd", grid=(n // TBM,), in_specs=_merge_in_specs(tok, TBM, lses),
        out_specs=[pl.BlockSpec((TBM, w), tok) for _, w, _ in outs] + [pl.BlockSpec((1, w), fix) for _, w in small],
        out_shape=[jax.ShapeDtypeStruct((n, w), dt) for _, w, dt in outs]
        + [jax.ShapeDtypeStruct((1, w), f32) for _, w in small],
        compiler_params=_cp(1, 56))(dx1, ys, *os_, *lses, hc, proj, proj, proj, bg, lng, lnb, wglu, watt, wpw, wout)
    return dict(zip([k for k, _, _ in outs] + [k for k, _ in small], res))


def assemble_dproj(du, dqs, dks, dvs, dcv, dgate):
    b, l, _ = du.shape
    nck = BW // 128

    def body(du_ref, q1, q2, q3, k1, k2, k3, v1, v2, v3, cv_ref, g_ref, o_ref):
        o_ref[0, :, 0:BW] = du_ref[0]
        for c in range(nck):
            for j, qr in enumerate((q1, q2, q3)):
                o_ref[0, :, (1 + j) * BW + c * 128:(1 + j) * BW + (c + 1) * 128] = qr[0, c]
            add3 = lambda r1, r2, r3: (r1[0, c].astype(f32) + r2[0, c].astype(f32) + r3[0, c].astype(f32)).astype(bf16)
            o_ref[0, :, 4 * BW + c * 128:4 * BW + (c + 1) * 128] = add3(k1, k2, k3)
            o_ref[0, :, 5 * BW + c * 128:5 * BW + (c + 1) * 128] = add3(v1, v2, v3)
        o_ref[0, :, 6 * BW:8 * BW] = cv_ref[0]
        o_ref[0, :, 8 * BW:] = g_ref[0]

    t = lambda w: pl.BlockSpec((1, TB, w), lambda bi, i: (bi, i, 0))
    ck = pl.BlockSpec((1, nck, TB, 128), lambda bi, i: (bi, 0, i, 0))
    return pl.pallas_call(
        body, name="assemble_dproj", grid=(b, l // TB),
        in_specs=[t(BW)] + [ck] * 9 + [t(2 * BW), t(3 * D)], out_specs=t(INC),
        out_shape=jax.ShapeDtypeStruct((b, l, INC), bf16), compiler_params=_cp(2))(du, *dqs, *dks, *dvs, dcv, dgate)


def _me():
    return lax.axis_index("x"), lax.axis_index("y"), lax.axis_index("c")


def _peers():
    x, y, c = _me()
    return [(x, y, 1 - c), (1 - x, y, c), (1 - x, y, 1 - c), (x, 1 - y, c), (x, 1 - y, 1 - c),
            (1 - x, 1 - y, c), (1 - x, 1 - y, 1 - c)]


def _rank(p):
    return 4 * p[0] + 2 * p[1] + p[2]


def allgather(arrs, name):
    na = len(arrs)
    units = [(a, j) for a in range(na) for j in range(arrs[a].shape[0])]
    nu = len(units)

    def body(*refs):
        ins, outs = refs[:na], refs[na:2 * na]
        send, recv, loc = refs[2 * na:]
        me = _rank(_me())
        local, remote = [], []
        for u, (a, j) in enumerate(units):
            own = pltpu.make_async_copy(ins[a].at[j], outs[a].at[j, me], loc.at[u])
            own.start()
            local.append(own)
        for u, (a, j) in enumerate(units):
            for k, p in enumerate(_peers()):
                cp = pltpu.make_async_remote_copy(src_ref=ins[a].at[j], dst_ref=outs[a].at[j, me],
                                                  send_sem=send.at[u, k], recv_sem=recv.at[u, k],
                                                  device_id=p, device_id_type=MESH)
                cp.start()
                remote.append(cp)
        for cp in local:
            cp.wait()
        for cp in remote:
            cp.wait()

    return pl.pallas_call(
        body, name=name, in_specs=[ANY] * na, out_specs=[ANY] * na,
        out_shape=[jax.ShapeDtypeStruct((a.shape[0], NDEV) + a.shape[1:], a.dtype) for a in arrs],
        scratch_shapes=[pltpu.SemaphoreType.DMA((nu, NDEV - 1)), pltpu.SemaphoreType.DMA((nu, NDEV - 1)),
                        pltpu.SemaphoreType.DMA((nu,))])(*arrs)


HBM = pl.BlockSpec(memory_space=pltpu.HBM)
SEM = pl.BlockSpec(memory_space=pltpu.SEMAPHORE)
_EFFECT = pltpu.SideEffectType.DATAFLOW_SIDE_EFFECTING


def _rank_slot(ref, r):
    if ref.shape[0] == NDEV:
        return ref.at[r]
    n = ref.shape[2] // 2
    return ref.at[r // 2, :, pl.ds(pl.multiple_of((r % 2) * n, 128), n)]


def _push_copies(srcs, lands, send, recv, scatter):
    me = _rank(_me())
    out = []
    for i in range(len(srcs)):
        for k, p in enumerate(_peers()):
            src = _rank_slot(srcs[i], _rank(p)) if scatter else srcs[i]
            dst = lands[i].at[k] if scatter else _rank_slot(lands[i], me)
            j = i * (NDEV - 1) + k
            out.append(pltpu.make_async_remote_copy(src_ref=src, dst_ref=dst, send_sem=send.at[j],
                                                    recv_sem=recv.at[j], device_id=p, device_id_type=MESH))
    return out


def push_start(srcs, lands, scatter, name, token):
    n = len(srcs)
    token = jnp.zeros((8, 128), f32) if token is None else token

    def body(*refs):
        for cp in _push_copies(refs[:n], refs[n:2 * n], refs[2 * n + 1], refs[2 * n + 2], scatter):
            cp.start()
        refs[-1][...] = refs[2 * n][...]

    sems = pltpu.SemaphoreType.DMA((n * (NDEV - 1),))
    vmem = pl.BlockSpec(memory_space=pltpu.VMEM)
    res = pl.pallas_call(
        body, name=name, in_specs=[HBM] * (2 * n) + [vmem], out_specs=[SEM, SEM] + [HBM] * (2 * n) + [vmem],
        out_shape=[sems, sems] + [pltpu.HBM(a.shape, a.dtype) for a in list(srcs) + list(lands)]
        + [jax.ShapeDtypeStruct((8, 128), f32)],
        input_output_aliases={i: 2 + i for i in range(2 * n)},
        compiler_params=pltpu.CompilerParams(has_side_effects=_EFFECT),
    )(*[pltpu.with_memory_space_constraint(a, pltpu.HBM) for a in list(srcs) + list(lands)], token)
    return res[0], res[1], res[2:2 + n], res[2 + n:2 + 2 * n], res[-1]


def push_wait(send, recv, srcs, lands, after, scatter, name):
    n = len(srcs)
    after = list(after) if isinstance(after, (list, tuple)) else [after]

    def body(*refs):
        for cp in _push_copies(refs[:n], refs[n:2 * n], refs[2 * n], refs[2 * n + 1], scatter):
            cp.wait_send()
            cp.wait_recv()

    res = pl.pallas_call(
        body, name=name, in_specs=[HBM] * (2 * n) + [SEM, SEM] + [ANY] * len(after), out_specs=[HBM] * (2 * n),
        out_shape=[pltpu.HBM(a.shape, a.dtype) for a in list(srcs) + list(lands)],
        input_output_aliases={i: i for i in range(2 * n)},
        compiler_params=pltpu.CompilerParams(has_side_effects=_EFFECT),
    )(*srcs, *lands, send, recv, *after)
    return res[:n], res[n:]


_C1 = 1.0 / (1.0 - ADAM_B1 ** ADAM_STEP)
_C2 = 1.0 / (1.0 - ADAM_B2 ** ADAM_STEP)


def _adamw(w, g, m, v):
    m = ADAM_B1 * m + (1.0 - ADAM_B1) * g
    v = ADAM_B2 * v + (1.0 - ADAM_B2) * (g * g)
    delta = -ADAM_LR * ((m * _C1) / (jnp.sqrt(v * _C2) + ADAM_EPS) + ADAM_WD * w)
    return delta, m, v


def adam_big(lands, owns, w, m, v, name):
    _, k, n = lands[0].shape
    tk = k
    while tk * n * 2 * NDEV > 2 * 1024 * 1024 and tk % 16 == 0:
        tk //= 2

    def body(*refs):
        l_refs, o_refs = refs[:DEPTH], refs[DEPTH:2 * DEPTH]
        w_ref, m_ref, v_ref, g_ref, d_ref, nm_ref, nv_ref = refs[2 * DEPTH:]
        for l in range(DEPTH):
            g = o_refs[l][...].astype(f32)
            for s in range(NDEV - 1):
                g = g + l_refs[l][s].astype(f32)
            d, nm, nv = _adamw(w_ref[l], g, m_ref[l], v_ref[l])
            g_ref[l], d_ref[l], nm_ref[l], nv_ref[l] = g, d, nm, nv

    blk = pl.BlockSpec((DEPTH, tk, n), lambda i: (0, i, 0))
    return pl.pallas_call(
        body, name=name, grid=(k // tk,),
        in_specs=[pl.BlockSpec((NDEV - 1, tk, n), lambda i: (0, i, 0))] * DEPTH
        + [pl.BlockSpec((tk, n), lambda i: (i, 0))] * DEPTH + [blk, blk, blk],
        out_specs=[blk] * 4, out_shape=[jax.ShapeDtypeStruct(w.shape, f32)] * 4,
        compiler_params=_cp(1))(*lands, *owns, w, m, v)


def adam_small(gath, w, m, v):
    r = w.shape[0]
    tr = 512

    def body(g_ref, w_ref, m_ref, v_ref, go_ref, d_ref, nm_ref, nv_ref):
        g = g_ref[0]
        for s in range(1, NDEV):
            g = g + g_ref[s]
        d, nm, nv = _adamw(w_ref[...], g, m_ref[...], v_ref[...])
        go_ref[...], d_ref[...], nm_ref[...], nv_ref[...] = g, d, nm, nv

    blk = pl.BlockSpec((tr, 128), lambda i: (i, 0))
    return pl.pallas_call(
        body, name="adam_small", grid=(r // tr,),
        in_specs=[pl.BlockSpec((NDEV, tr, 128), lambda i: (0, i, 0)), blk, blk, blk],
        out_specs=[blk] * 4, out_shape=[jax.ShapeDtypeStruct((r, 128), f32)] * 4,
        compiler_params=_cp(1))(gath, w, m, v)


SMALL = ["norm1_g", "b_gate", "ssm_lambda_re", "ssm_lambda_im", "ssm_log_dt", "ssm_b_re", "ssm_b_im",
         "ssm_c_re", "ssm_c_im", "ssm_d", "conv_w", "conv_b", "conv_ln_g", "conv_ln_b", "norm2_g", "final_g"]
BIG = ["w_in", "w_ssm_glu", "w_att_up", "w_conv_pw2", "w_out", "w_ffn_in", "w_ffn_out"]
ORDER = ["norm1_g", "w_in", "b_gate", "ssm_lambda_re", "ssm_lambda_im", "ssm_log_dt", "ssm_b_re", "ssm_b_im",
         "ssm_c_re", "ssm_c_im", "ssm_d", "w_ssm_glu", "w_att_up", "conv_w", "conv_b", "conv_ln_g", "conv_ln_b",
         "w_conv_pw2", "w_out", "norm2_g", "w_ffn_in", "w_ffn_out", "final_g"]
PACK_ROWS = 2560


def _pack(arrs):
    flat = jnp.concatenate([a.reshape(-1).astype(f32) for a in arrs])
    return jnp.pad(flat, (0, PACK_ROWS * 128 - flat.shape[0])).reshape(PACK_ROWS, 128)


def _unpack(pack, shapes):
    flat = pack.reshape(-1)
    out, off = [], 0
    for s in shapes:
        sz = math.prod(s)
        out.append(flat[off:off + sz].reshape(s))
        off += sz
    return out


def _bt(b):
    return b.transpose(2, 0, 1).reshape(GH, NSTATE)


def _bt_inv(bt):
    return bt.reshape(GH, NG, NS).transpose(1, 2, 0)


def _ct(c):
    return c.transpose(1, 0, 2).reshape(GH, NSTATE)


def _ct_inv(ct):
    return ct.reshape(GH, NG, NS).transpose(1, 0, 2)


def local_step(x, loss_target, P, weights, on_grads, start_token=None):
    bsz, seq, _ = x.shape
    n = bsz * seq

    def natural(g3):
        return g3.transpose(1, 0, 2).reshape(g3.shape[1], NDEV * g3.shape[2])

    tokens = [] if start_token is None else [start_token]

    def after_pushes(a):
        while tokens:
            a = a + tokens.pop()[0:1, 0:1]
        return a

    def pushed(tok):
        if tok is not None:
            tokens.append(tok)

    s5_in, s5_tabs = [], []
    for l in range(DEPTH):
        lr = P["ssm_lambda_re"][l].reshape(1, NSTATE)
        li = P["ssm_lambda_im"][l].reshape(1, NSTATE)
        ld = jnp.repeat(P["ssm_log_dt"][l], NS).reshape(1, NSTATE)
        btr, bti = _bt(P["ssm_b_re"][l]), _bt(P["ssm_b_im"][l])
        s5_in.append((lr, li, ld, btr, bti))
        s5_tabs.append(s5_params(lr, li, ld, btr, bti, _ct(P["ssm_c_re"][l]), _ct(P["ssm_c_im"][l])))

    xs = x.reshape(n, D)
    saved = []
    conv_w_pad = None
    for l in range(DEPTH):
        S = {"x": xs}
        h1 = rms_fwd(xs, after_pushes(P["norm1_g"][l][None]))
        G = dict(weights(l, "in", [h1] + [t for tabs in s5_tabs for t in tabs] if l == 0 else h1))
        if conv_w_pad is None:
            conv_w_full = G["conv_w"].transpose(1, 2, 0, 3).reshape(DEPTH, CW, BW)
            conv_w_pad = jnp.pad(conv_w_full, ((0, 0), (0, 1), (0, 0)))
        w_in4 = G["w_in"][None]
        proj = inproj(h1, w_in4, 0)
        proj3 = proj.reshape(bsz, seq, INC)
        lr, li, ld, btr, bti = s5_in[l]
        t8, bb, cb = s5_tabs[l]
        dskip = P["ssm_d"][l][None]
        ys, s_re, s_im = s5_fwd(proj3, t8, bb, cb, dskip)
        att = [att_fwd(proj3, gi, dil) for gi, (_, dil) in enumerate(PATTERNS)]
        hc = conv_fwd(proj3, conv_w_pad[l], P["conv_b"][l][None])
        G.update(weights(l, "mix", hc))
        wts = dict(wglu=natural(G["w_ssm_glu"]), watt=natural(G["w_att_up"]),
                   wpw=natural(G["w_conv_pw2"]), wout=G["w_out"].reshape(D, D))
        mi = dict(ys=ys.reshape(n, BW), os_=[a[0].reshape(n, BW) for a in att],
                  lses=[a[1] for a in att], hc=hc.reshape(n, BW),
                  proj=proj, bg=P["b_gate"][l][None], lng=P["conv_ln_g"][l][None], lnb=P["conv_ln_b"][l][None],
                  **wts)
        x1 = merge_fwd(xs, **mi)
        G.update(weights(l, "ffn", x1))
        w_ffn = (G["w_ffn_in"][None], G["w_ffn_out"][None])
        x2, z1s, z2s, h2 = ffn_fwd(x1, P["norm2_g"][l][None], *w_ffn, 0)
        S.update(h1=h1, proj=proj, proj3=proj3, tabs=(s_re, s_im, t8, bb, cb), mi=mi, x1=x1, w_in4=w_in4, w_ffn=w_ffn,
                 zs=(z1s, z2s), h2=h2,
                 sp=(lr, li, ld, btr, bti), dskip=dskip)
        saved.append(S)
        xs = x2

    loss8, dx, dfinal = loss_head(xs, P["final_g"][None], loss_target.reshape(n, D))

    small_g = {k: [None] * DEPTH for k in SMALL if k != "final_g"}
    tokblk = lambda w: pl.BlockSpec((1024, w), lambda s, i: (i, 0))
    colblk = lambda w: pl.BlockSpec((1024, w), lambda s, i: (i, s))
    for l in reversed(range(DEPTH)):
        S = saved[l]
        g2 = P["norm2_g"][l][None]
        dh4, dwa, dwb, dw2 = ffn_bwd(S["h2"], dx, *S["zs"], *S["w_ffn"], 0)
        dx1, dg2 = norm_bwd_fin(S["x1"], after_pushes(g2), dx, dh4, "ffn_bwd_fin")
        small_g["norm2_g"][l] = dg2
        pushed(on_grads(l, "ffn", dict(w_ffn_in=jnp.concatenate([dwa, dwb], axis=0),
                                       w_ffn_out=dw2.reshape(NDEV, NSH_FF // 2, D))))
        mb = merge_bwd(dx1, **dict(S["mi"], lng=after_pushes(S["mi"]["lng"])))
        small_g["b_gate"][l], small_g["conv_ln_g"][l], small_g["conv_ln_b"][l] = mb["dbg"], mb["dlng"], mb["dlnb"]
        dws = dw_mix(mb["ysin"], mb["dz"], mb["ob"], mb["dya"], mb["hs"], mb["dyc"], mb["merged"], dx1)
        pushed(on_grads(l, "mix", dict(zip(("w_ssm_glu", "w_att_up", "w_conv_pw2", "w_out"), dws))))
        dcv, dcw, dcb = conv_bwd(S["proj3"], mb["dhc"].reshape(bsz, seq, BW), after_pushes(conv_w_pad[l]))
        small_g["conv_w"][l] = dcw[:CW].reshape(CW, NDEV, BW // NDEV).transpose(1, 0, 2)
        small_g["conv_b"][l] = dcb
        ab = [att_bwd(S["proj3"], mb["do"].reshape(bsz, seq, BW), mb["lse_tot"].reshape(bsz, seq, 128),
                      mb["delta"].reshape(bsz, seq, 128), gi, dil) for gi, (_, dil) in enumerate(PATTERNS)]
        du, d_a, d_bb, d_cb, d_d = s5_bwd(S["proj3"], mb["dys"].reshape(bsz, seq, BW), *S["tabs"], S["dskip"])
        lr, li, ld, btr, bti = S["sp"]
        dlr, dli, dld, dbt, dct = s5_params_bwd(lr, li, ld, btr, bti, d_a, d_bb, d_cb)
        small_g["ssm_lambda_re"][l], small_g["ssm_lambda_im"][l] = dlr.reshape(NG, NS), dli.reshape(NG, NS)
        small_g["ssm_log_dt"][l] = dld[0, :NG]
        small_g["ssm_b_re"][l], small_g["ssm_b_im"][l] = _bt_inv(dbt[0]), _bt_inv(dbt[1])
        small_g["ssm_c_re"][l], small_g["ssm_c_im"][l] = _ct_inv(dct[0]), _ct_inv(dct[1])
        small_g["ssm_d"][l] = d_d
        dproj = assemble_dproj(du, [a[0] for a in ab], [a[1] for a in ab], [a[2] for a in ab],
                               dcv, mb["dgate"].reshape(bsz, seq, 3 * D)).reshape(n, INC)
        nblk, wblk = S["w_in4"].shape[1], S["w_in4"].shape[3]
        pushed(on_grads(l, "in", dict(w_in=mm_tn(S["h1"], dproj, tokblk(D), colblk(wblk), nblk, D, wblk, n, "dw_in"))))
        if l == 0:
            pushed(on_grads(l, "small", dict(small_g=small_g, loss8=loss8, dfinal=dfinal)))
        dx, dg1 = inproj_bwd(dproj, S["w_in4"], 0, S["x"], after_pushes(P["norm1_g"][l][None]), dx1)
        small_g["norm1_g"][l] = dg1
    return loss8, dx, dfinal, small_g


def kernel(x, norm1_g, w_in, b_gate, ssm_lambda_re, ssm_lambda_im, ssm_log_dt, ssm_b_re, ssm_b_im, ssm_c_re, ssm_c_im, ssm_d, w_ssm_glu, w_att_up, conv_w, conv_b, conv_ln_g, conv_ln_b, w_conv_pw2, w_out, norm2_g, w_ffn_in, w_ffn_out, final_g, loss_target, m_norm1_g, m_w_in, m_b_gate, m_ssm_lambda_re, m_ssm_lambda_im, m_ssm_log_dt, m_ssm_b_re, m_ssm_b_im, m_ssm_c_re, m_ssm_c_im, m_ssm_d, m_w_ssm_glu, m_w_att_up, m_conv_w, m_conv_b, m_conv_ln_g, m_conv_ln_b, m_w_conv_pw2, m_w_out, m_norm2_g, m_w_ffn_in, m_w_ffn_out, m_final_g, v_norm1_g, v_w_in, v_b_gate, v_ssm_lambda_re, v_ssm_lambda_im, v_ssm_log_dt, v_ssm_b_re, v_ssm_b_im, v_ssm_c_re, v_ssm_c_im, v_ssm_d, v_w_ssm_glu, v_w_att_up, v_conv_w, v_conv_b, v_conv_ln_g, v_conv_ln_b, v_w_conv_pw2, v_w_out, v_norm2_g, v_w_ffn_in, v_w_ffn_out, v_final_g):
    args = dict(locals())
    W = {k: args[k] for k in ORDER}
    M = {k: args["m_" + k] for k in ORDER}
    V = {k: args["v_" + k] for k in ORDER}
    bsz, seq, _ = x.shape
    n = bsz * seq
    me = 4 * lax.axis_index("x") + 2 * lax.axis_index("y") + lax.axis_index("c")

    groups = {"in": ["w_in"], "mix": ["w_ssm_glu", "w_att_up", "w_conv_pw2", "w_out"], "ffn": ["w_ffn_in", "w_ffn_out"]}
    wb = {k: W[k].astype(bf16) for k in BIG}

    def landing(shard, paired=False):
        if paired:
            k_, n_ = shard.shape
            return lax.dynamic_update_slice(lax.empty((NDEV // 2, k_, 2 * n_), shard.dtype), shard[None],
                                            (me // 2, 0, (me % 2) * n_))
        return lax.dynamic_update_index_in_dim(lax.empty((NDEV,) + shard.shape, shard.dtype), shard, me, 0)

    def own_part(by_rank):
        if by_rank.shape[0] == NDEV:
            return lax.dynamic_index_in_dim(by_rank, me, 0, keepdims=False)
        n_ = by_rank.shape[2] // 2
        return lax.dynamic_slice(by_rank, (me // 2, 0, (me % 2) * n_), (1, by_rank.shape[1], n_))[0]

    plan = [("gather_a", [("w_in", 0), ("conv_w", None)]),
            ("gather_b", [(k, 0) for k in groups["mix"] + groups["ffn"]]),
            ("gather_c", [(k, 1) for k in BIG])]
    pending, token = {}, None
    for name, items in plan:
        shards = [conv_w if l is None else wb[k][l] for k, l in items]
        lands = [landing(s, k == "w_in") for (k, _), s in zip(items, shards)]
        send, recv, s_thru, l_thru, token = push_start(shards, lands, False, name, token)
        pending[name] = (send, recv, s_thru, l_thru, items)
    gathered = {}

    names = [k for k in SMALL if k != "final_g"]
    shapes = [(DEPTH, NDEV, CW, BW // NDEV) if k == "conv_w" else W[k].shape for k in names] + [(D,), (1,)]

    def wpack(src):
        parts = [jnp.broadcast_to(src[k][:, None], shapes[i]) if k == "conv_w" else src[k] for i, k in enumerate(names)]
        return _pack(parts + [src["final_g"], jnp.ones((1,), f32)])

    packs = [wpack(W), wpack(M), wpack(V)]

    def weights(l, group, after):
        name = "gather_c" if l == 1 else ("gather_a" if group == "in" else "gather_b")
        if name in pending:
            send, recv, s_thru, l_thru, items = pending.pop(name)
            if name == "gather_a":
                after = (list(after) if isinstance(after, (list, tuple)) else [after]) + packs
            for item, arr in zip(items, push_wait(send, recv, s_thru, l_thru, after, False, name + "_wait")[1]):
                gathered[item] = arr
        res = {k: gathered[(k, l)] for k in groups[group]}
        if group == "in":
            res["conv_w"] = gathered[("conv_w", None)]
        return res

    big_g = {k: [None] * DEPTH for k in BIG}
    flights = []

    def start_exchange(items, name):
        parts = [big_g[k][l] for k, l in items]
        part = lambda p: p.shape[1:] if p.shape[0] == NDEV else (p.shape[1], p.shape[2] // 2)
        lands = [lax.empty((NDEV - 1,) + part(p), p.dtype) for p in parts]
        send, recv, s_thru, l_thru, tok = push_start(parts, lands, True, name, None)
        flights.append((send, recv, s_thru, l_thru, items, name))
        return tok

    small_flight = []

    def start_small(small_g, loss8, dfinal):
        sg_ = dict(small_g, norm1_g=[jnp.zeros((1, D), f32), small_g["norm1_g"][1]])
        gpack = _pack([jnp.stack([g.reshape(shapes[i][1:]) for g in sg_[k]]) for i, k in enumerate(names)]
                      + [dfinal, loss8[0, :1]])
        send, recv, s_thru, l_thru, tok = push_start([gpack], [landing(gpack)], False, "gather_small", None)
        small_flight.append((send, recv, s_thru, l_thru))
        return tok

    def on_grads(l, group, grads):
        if group == "small":
            return start_small(**grads)
        for k, g in grads.items():
            big_g[k][l] = g
        if l == 1 and group == "in":
            return start_exchange([(k, 1) for k in BIG], "exchange_l1")
        if l == 0:
            return start_exchange([(k, 0) for k in groups[group]], "exchange_l0_" + group)
        return None

    loss8, dx, dfinal, small_g = local_step(x, loss_target, W, weights, on_grads, token)

    landed, own = {}, {}
    for send, recv, s_thru, l_thru, items, name in flights:
        srcs, lands = push_wait(send, recv, s_thru, l_thru, dx, True, name + "_wait")
        for item, src, land in zip(items, srcs, lands):
            landed[item] = land
            own[item] = own_part(src)
    out = {}
    for k in BIG:
        items = [(k, l) for l in range(DEPTH)]
        out[k] = adam_big([landed[i] for i in items], [own[i] for i in items], W[k], M[k], V[k], "adam_" + k)

    send, recv, s_thru, l_thru = small_flight[0]
    gall = push_wait(send, recv, s_thru, l_thru, out[BIG[-1]][0], False, "gather_small_wait")[1][0]
    (late,) = allgather([small_g["norm1_g"][0].reshape(1, D // 128, 128)], "allgather_late")
    gall = lax.dynamic_update_slice(gall, late[0], (0, 0, 0))
    sg, sd, sm, sv = [_unpack(p, shapes) for p in adam_small(gall, *packs)]
    for i, k in enumerate(names + ["final_g"]):
        vals = [t[i] for t in (sg, sd, sm, sv)]
        if k == "conv_w":
            vals = [lax.dynamic_index_in_dim(t, me, axis=1, keepdims=False) for t in vals]
        out[k] = vals
    loss = sg[-1].reshape(())

    res = [loss, dx.reshape(bsz, seq, D)]
    for j in range(4):
        res += [out[k][j] for k in ORDER]
    return tuple(res)
```

```python
import functools
import math

import jax
import jax.numpy as jnp
from jax import lax
from jax.experimental import pallas as pl
from jax.experimental.pallas import tpu as pltpu

f32 = jnp.float32
bf16 = jnp.bfloat16

D = 1024
DEPTH = 2
EPS = 1e-6
BW = 512
NG = 32
GH = 16
NS = 64
NSTATE = NG * NS
HD = 64
NH = 8
PATTERNS = ((128, 1), (512, 4), (2048, 16))
ABLK = 128
ATT_SCALE = HD ** -0.5
CW = 31
DFF = 2816
INC = 7168
NDEV = 8
NSH_IN = INC // NDEV
NSH_FF = 2 * DFF // NDEV
ADAM_LR, ADAM_B1, ADAM_B2, ADAM_EPS, ADAM_WD, ADAM_STEP = 0.001, 0.9, 0.999, 1e-08, 0.01, 10

TB = 512
SJ = 4
SW = NSTATE // SJ
SU = BW // SJ
NEG = -1e30
MESH = pl.DeviceIdType.MESH
ANY = pl.BlockSpec(memory_space=pl.ANY)


def _cp(n_axes, vmem_mb=48):
    return pltpu.CompilerParams(dimension_semantics=("arbitrary",) * n_axes,
                                vmem_limit_bytes=vmem_mb * 1024 * 1024)


def _dot(a, b):
    return jnp.dot(a, b, preferred_element_type=f32)


def _dot_nt(a, b):
    return lax.dot_general(a, b, (((1,), (1,)), ((), ())), preferred_element_type=f32)


def _dot_tn(a, b):
    return lax.dot_general(a, b, (((0,), (0,)), ((), ())), preferred_element_type=f32)


def _dot_sel(a, sel):
    hi = a.astype(bf16)
    lo = (a - hi.astype(f32)).astype(bf16)
    sb = sel.astype(bf16)
    return _dot(hi, sb) + _dot(lo, sb)


def _dot_hi(a, b):
    return jnp.dot(a, b, precision=lax.Precision.HIGHEST, preferred_element_type=f32)


def _sigmoid(x):
    return 1.0 / (1.0 + jnp.exp(-x))


_GC = math.sqrt(2.0 / math.pi)


def _gelu(x):
    return 0.5 * x * (1.0 + jnp.tanh(_GC * (x + 0.044715 * x * x * x)))


def _gelu_grad(x):
    t = jnp.tanh(_GC * (x + 0.044715 * x * x * x))
    return 0.5 * (1.0 + t) + 0.5 * x * (1.0 - t * t) * _GC * (1.0 + 3.0 * 0.044715 * x * x)


def _rms_stats(x):
    return lax.rsqrt(jnp.mean(x * x, axis=-1, keepdims=True) + EPS)


def _rms_bwd(x, g, dh):
    r = _rms_stats(x)
    dyg = dh * g
    dx = r * dyg - x * (r * r * r) * jnp.mean(dyg * x, axis=-1, keepdims=True)
    dg = jnp.sum(dh * x * r, axis=0, keepdims=True)
    return dx, dg


def rms_fwd(x, g):
    n = x.shape[0]

    def body(x_ref, g_ref, h_ref):
        xv = x_ref[...]
        h_ref[...] = (xv * _rms_stats(xv) * g_ref[...]).astype(bf16)

    return pl.pallas_call(
        body, name="rms_fwd", grid=(n // TB,),
        in_specs=[pl.BlockSpec((TB, D), lambda i: (i, 0)), pl.BlockSpec((1, D), lambda i: (0, 0))],
        out_specs=pl.BlockSpec((TB, D), lambda i: (i, 0)),
        out_shape=jax.ShapeDtypeStruct((n, D), bf16), compiler_params=_cp(1))(x, g)


def inproj(h, w4, layer):
    n = h.shape[0]
    tm = 1024
    nblk, wblk = w4.shape[1], w4.shape[3]

    def body(h_ref, w_ref, o_ref):
        o_ref[...] = _dot(h_ref[...], w_ref[0, 0]).astype(bf16)

    return pl.pallas_call(
        body, name="inproj", grid=(nblk, n // tm),
        in_specs=[pl.BlockSpec((tm, D), lambda s, i: (i, 0)),
                  pl.BlockSpec((1, 1, D, wblk), lambda s, i: (layer, s, 0, 0))],
        out_specs=pl.BlockSpec((tm, wblk), lambda s, i: (i, s)),
        out_shape=jax.ShapeDtypeStruct((n, INC), bf16), compiler_params=_cp(2))(h, w4)


def inproj_bwd(dproj, w4, layer, x, g, dres):
    n = x.shape[0]
    tm = 1024
    nblk, wblk = w4.shape[1], w4.shape[3]

    def body(dp_ref, w_ref, x_ref, g_ref, dr_ref, dx_ref, dg_ref, acc):
        i, s = pl.program_id(0), pl.program_id(1)

        @pl.when(s == 0)
        def _():
            acc[...] = jnp.zeros_like(acc)

        @pl.when((s == 0) & (i == 0))
        def _():
            dg_ref[...] = jnp.zeros_like(dg_ref)

        acc[...] += _dot_nt(dp_ref[...], w_ref[0, 0])

        @pl.when(s == nblk - 1)
        def _():
            dx, dg = _rms_bwd(x_ref[...], g_ref[...], acc[...])
            dx_ref[...] = dr_ref[...] + dx
            dg_ref[...] += dg

    return pl.pallas_call(
        body, name="inproj_bwd", grid=(n // tm, nblk),
        in_specs=[pl.BlockSpec((tm, wblk), lambda i, s: (i, s)),
                  pl.BlockSpec((1, 1, D, wblk), lambda i, s: (layer, s, 0, 0)),
                  pl.BlockSpec((tm, D), lambda i, s: (i, 0)),
                  pl.BlockSpec((1, D), lambda i, s: (0, 0)),
                  pl.BlockSpec((tm, D), lambda i, s: (i, 0))],
        out_specs=[pl.BlockSpec((tm, D), lambda i, s: (i, 0)), pl.BlockSpec((1, D), lambda i, s: (0, 0))],
        out_shape=[jax.ShapeDtypeStruct((n, D), f32), jax.ShapeDtypeStruct((1, D), f32)],
        scratch_shapes=[pltpu.VMEM((tm, D), f32)], compiler_params=_cp(2))(dproj, w4, x, g, dres)


def mm_tn(a, b, a_spec, b_spec, n_sh, ka, nb, m, name):
    tm = 1024

    def body(a_ref, b_ref, o_ref, acc):
        i = pl.program_id(1)

        @pl.when(i == 0)
        def _():
            acc[...] = jnp.zeros_like(acc)

        av = a_ref[...].reshape(tm, ka).astype(bf16)
        bv = b_ref[...].reshape(tm, nb).astype(bf16)
        acc[...] += _dot_tn(av, bv)

        @pl.when(i == m // tm - 1)
        def _():
            o_ref[0] = acc[...].astype(bf16)

    return pl.pallas_call(
        body, name=name, grid=(n_sh, m // tm), in_specs=[a_spec, b_spec],
        out_specs=pl.BlockSpec((1, ka, nb), lambda s, i: (s, 0, 0)),
        out_shape=jax.ShapeDtypeStruct((n_sh, ka, nb), bf16),
        scratch_shapes=[pltpu.VMEM((ka, nb), f32)], compiler_params=_cp(2))(a, b)


def dw_mix(ysin, dz, ob, dya, hs, dyc, merged, dx1):
    n = ysin.shape[0]
    tm = 512
    pairs = ((BW, 2 * D), (BW, D), (BW, D), (D, D))

    def body(a0, b0, a1, b1, a2, b2, a3, b3, o0, o1, o2, o3, c0, c1, c2, c3):
        i = pl.program_id(0)
        accs = (c0, c1, c2, c3)

        @pl.when(i == 0)
        def _():
            for c in accs:
                c[...] = jnp.zeros_like(c)

        for a, b_, c in zip((a0, a1, a2, a3), (b0, b1, b2, b3), accs):
            c[...] += _dot_tn(a[...], b_[...].astype(bf16))

        @pl.when(i == n // tm - 1)
        def _():
            for s in range(NDEV):
                o0[s] = c0[:, s * 256:(s + 1) * 256].astype(bf16)
                o1[s] = c1[:, s * 128:(s + 1) * 128].astype(bf16)
                o2[s] = c2[:, s * 128:(s + 1) * 128].astype(bf16)
                o3[s] = c3[s * 128:(s + 1) * 128, :].astype(bf16)

    tok = lambda w: pl.BlockSpec((tm, w), lambda i: (i, 0))
    whole = lambda shape: pl.BlockSpec(shape, lambda i: (0, 0, 0))
    outs = [(NDEV, BW, 256), (NDEV, BW, 128), (NDEV, BW, 128), (NDEV, D // NDEV, D)]
    return pl.pallas_call(
        body, name="dw_mix", grid=(n // tm,),
        in_specs=[tok(w) for pair in pairs for w in pair],
        out_specs=[whole(s) for s in outs], out_shape=[jax.ShapeDtypeStruct(s, bf16) for s in outs],
        scratch_shapes=[pltpu.VMEM(p, f32) for p in pairs],
        compiler_params=_cp(1, 56))(ysin, dz, ob, dya, hs, dyc, merged, dx1)


def ffn_fwd(x1, g2, w1, w2, layer):
    n = x1.shape[0]
    w2p = w2.reshape(w2.shape[0], 4, NSH_FF, D)

    def body(x_ref, g_ref, wa_ref, wb_ref, w2_ref, o_ref, z1_ref, z2_ref, h_sc):
        s = pl.program_id(1)

        @pl.when(s == 0)
        def _():
            xv = x_ref[...]
            h_sc[...] = (xv * _rms_stats(xv) * g_ref[...]).astype(bf16)
            o_ref[...] = xv

        h = h_sc[...]
        z1 = _dot(h, wa_ref[0, 0])
        z2 = _dot(h, wb_ref[0, 0])
        z1_ref[0] = z1.astype(bf16)
        z2_ref[0] = z2.astype(bf16)
        a = (z1 * _sigmoid(z1) * z2).astype(bf16)
        o_ref[...] += _dot(a, w2_ref[0, 0])

    tb = 2 * TB
    sh3 = pl.BlockSpec((1, tb, NSH_FF), lambda i, s: (s, i, 0))
    return pl.pallas_call(
        body, name="ffn_fwd", grid=(n // tb, 4),
        in_specs=[pl.BlockSpec((tb, D), lambda i, s: (i, 0)),
                  pl.BlockSpec((1, D), lambda i, s: (0, 0)),
                  pl.BlockSpec((1, 1, D, NSH_FF), lambda i, s: (layer, s, 0, 0)),
                  pl.BlockSpec((1, 1, D, NSH_FF), lambda i, s: (layer, s + 4, 0, 0)),
                  pl.BlockSpec((1, 1, NSH_FF, D), lambda i, s: (layer, s, 0, 0))],
        out_specs=[pl.BlockSpec((tb, D), lambda i, s: (i, 0)), sh3, sh3, pl.BlockSpec((tb, D), lambda i, s: (i, 0))],
        out_shape=[jax.ShapeDtypeStruct((n, D), f32), jax.ShapeDtypeStruct((4, n, NSH_FF), bf16),
                   jax.ShapeDtypeStruct((4, n, NSH_FF), bf16), jax.ShapeDtypeStruct((n, D), bf16)],
        compiler_params=_cp(2))(x1, g2, w1, w1, w2p)


def ffn_bwd(h, dx2, z1s, z2s, w1, w2, layer):
    n = h.shape[0]
    w2p = w2.reshape(w2.shape[0], 4, NSH_FF, D)
    nblk = n // TB

    def body(h_ref, dy_ref, z1_ref, z2_ref, wa_ref, wb_ref, w2_ref, dh_ref, dwa_ref, dwb_ref, dw2_ref,
             acc_a, acc_b, acc_2):
        i = pl.program_id(1)

        @pl.when(i == 0)
        def _():
            acc_a[...] = jnp.zeros_like(acc_a)
            acc_b[...] = jnp.zeros_like(acc_b)
            acc_2[...] = jnp.zeros_like(acc_2)

        hv = h_ref[...]
        dyb = dy_ref[...].astype(bf16)
        z1 = z1_ref[0].astype(f32)
        z2 = z2_ref[0].astype(f32)
        sg = _sigmoid(z1)
        sl = z1 * sg
        a = (sl * z2).astype(bf16)
        da = _dot_nt(dyb, w2_ref[0, 0])
        dz2 = (da * sl).astype(bf16)
        dz1 = (da * z2 * sg * (1.0 + z1 * (1.0 - sg))).astype(bf16)
        dh_ref[0] = (_dot_nt(dz1, wa_ref[0, 0]) + _dot_nt(dz2, wb_ref[0, 0])).astype(bf16)
        acc_a[...] += _dot_tn(hv, dz1)
        acc_b[...] += _dot_tn(hv, dz2)
        acc_2[...] += _dot_tn(a, dyb)

        @pl.when(i == nblk - 1)
        def _():
            dwa_ref[0] = acc_a[...].astype(bf16)
            dwb_ref[0] = acc_b[...].astype(bf16)
            dw2_ref[0] = acc_2[...].astype(bf16)

    tok = pl.BlockSpec((TB, D), lambda s, i: (i, 0))
    sh3 = pl.BlockSpec((1, TB, NSH_FF), lambda s, i: (s, i, 0))
    return pl.pallas_call(
        body, name="ffn_bwd", grid=(4, nblk),
        in_specs=[tok, tok, sh3, sh3,
                  pl.BlockSpec((1, 1, D, NSH_FF), lambda s, i: (layer, s, 0, 0)),
                  pl.BlockSpec((1, 1, D, NSH_FF), lambda s, i: (layer, s + 4, 0, 0)),
                  pl.BlockSpec((1, 1, NSH_FF, D), lambda s, i: (layer, s, 0, 0))],
        out_specs=[pl.BlockSpec((1, TB, D), lambda s, i: (s, i, 0)),
                   pl.BlockSpec((1, D, NSH_FF), lambda s, i: (s, 0, 0)),
                   pl.BlockSpec((1, D, NSH_FF), lambda s, i: (s, 0, 0)),
                   pl.BlockSpec((1, NSH_FF, D), lambda s, i: (s, 0, 0))],
        out_shape=[jax.ShapeDtypeStruct((4, n, D), bf16), jax.ShapeDtypeStruct((4, D, NSH_FF), bf16),
                   jax.ShapeDtypeStruct((4, D, NSH_FF), bf16), jax.ShapeDtypeStruct((4, NSH_FF, D), bf16)],
        scratch_shapes=[pltpu.VMEM((D, NSH_FF), f32), pltpu.VMEM((D, NSH_FF), f32), pltpu.VMEM((NSH_FF, D), f32)],
        compiler_params=_cp(2, 56))(h, dx2, z1s, z2s, w1, w1, w2p)


def norm_bwd_fin(x, g, dres, dh_parts, name):
    n = x.shape[0]
    nparts = dh_parts.shape[0]

    def body(x_ref, g_ref, dy_ref, dh_ref, dx_ref, dg_ref):
        i = pl.program_id(0)

        @pl.when(i == 0)
        def _():
            dg_ref[...] = jnp.zeros_like(dg_ref)

        dh = dh_ref[0].astype(f32)
        for s in range(1, nparts):
            dh = dh + dh_ref[s].astype(f32)
        dx, dg = _rms_bwd(x_ref[...], g_ref[...], dh)
        dx_ref[...] = dy_ref[...] + dx
        dg_ref[...] += dg

    tok = pl.BlockSpec((TB, D), lambda i: (i, 0))
    return pl.pallas_call(
        body, name=name, grid=(n // TB,),
        in_specs=[tok, pl.BlockSpec((1, D), lambda i: (0, 0)), tok, pl.BlockSpec((nparts, TB, D), lambda i: (0, i, 0))],
        out_specs=[tok, pl.BlockSpec((1, D), lambda i: (0, 0))],
        out_shape=[jax.ShapeDtypeStruct((n, D), f32), jax.ShapeDtypeStruct((1, D), f32)],
        compiler_params=_cp(1))(x, g, dres, dh_parts)


def loss_head(x, g, target):
    n = x.shape[0]

    def body(x_ref, g_ref, t_ref, l_ref, dx_ref, dg_ref):
        i = pl.program_id(0)

        @pl.when(i == 0)
        def _():
            l_ref[...] = jnp.zeros_like(l_ref)
            dg_ref[...] = jnp.zeros_like(dg_ref)

        xv = x_ref[...]
        y = xv * _rms_stats(xv) * g_ref[...]
        e = y - t_ref[...]
        l_ref[...] += 0.5 * jnp.sum(jnp.sum(e * e, axis=-1, keepdims=True), axis=0, keepdims=True) * (1.0 / D)
        dx, dg = _rms_bwd(xv, g_ref[...], e * (1.0 / D))
        dx_ref[...] = dx
        dg_ref[...] += dg

    tok = lambda i: (i, 0)
    return pl.pallas_call(
        body, name="loss_head", grid=(n // TB,),
        in_specs=[pl.BlockSpec((TB, D), tok), pl.BlockSpec((1, D), lambda i: (0, 0)), pl.BlockSpec((TB, D), tok)],
        out_specs=[pl.BlockSpec((8, 128), lambda i: (0, 0)), pl.BlockSpec((TB, D), tok),
                   pl.BlockSpec((1, D), lambda i: (0, 0))],
        out_shape=[jax.ShapeDtypeStruct((8, 128), f32), jax.ShapeDtypeStruct((n, D), f32),
                   jax.ShapeDtypeStruct((1, D), f32)],
        compiler_params=_cp(1))(x, g, target)


def _disc(lr, li, ld):
    dt = jnp.exp(ld)
    mag = jnp.exp(lr * dt)
    ar = mag * jnp.cos(li * dt)
    ai = mag * jnp.sin(li * dt)
    nr, ni = ar - 1.0, ai
    den = lr * lr + li * li
    zr = (nr * lr + ni * li) / den
    zi = (ni * lr - nr * li) / den
    return ar, ai, zr, zi


def _blockdiag_mask(shape):
    r = lax.broadcasted_iota(jnp.int32, shape, 0) // GH
    c = lax.broadcasted_iota(jnp.int32, shape, 1) // NS
    return r == c


def s5_params(lr, li, ld, btr, bti, ctr, cti):
    def body(lr_ref, li_ref, ld_ref, btr_ref, bti_ref, ctr_ref, cti_ref, t8_ref, bb_ref, cb_ref):
        ar, ai, zr, zi = _disc(lr_ref[...], li_ref[...], ld_ref[...])
        pr_, pi_ = ar, ai
        pw2 = []
        for k in range(4):
            pw2.append((pr_, pi_))
            pr_, pi_ = pr_ * pr_ - pi_ * pi_, 2.0 * pr_ * pi_
        cm = lambda p, q: (p[0] * q[0] - p[1] * q[1], p[0] * q[1] + p[1] * q[0])
        pw = {1: pw2[0], 2: pw2[1], 4: pw2[2], 8: pw2[3]}
        pw[3], pw[5], pw[6] = cm(pw[2], pw[1]), cm(pw[4], pw[1]), cm(pw[4], pw[2])
        pw[7] = cm(pw[4], pw[3])
        row = lax.broadcasted_iota(jnp.int32, (8, NSTATE), 0)
        zero = jnp.zeros((8, NSTATE), f32)
        for c in range(2):
            for k in range(3):
                full = jnp.broadcast_to(pw2[k][c], (8, NSTATE))
                t8_ref[c, k] = jnp.where(row >= (1 << k), full, 0.0)
                t8_ref[c, 3 + k] = jnp.where(row + (1 << k) < 8, full, 0.0)
            up, down = zero, zero
            for j in range(8):
                up = up + jnp.where(row == j, pw[j + 1][c], 0.0)
                down = down + jnp.where(row == j, pw[8 - j][c], 0.0)
            t8_ref[c, 6] = up
            t8_ref[c, 7] = down
        bbr = zr * btr_ref[...] - zi * bti_ref[...]
        bbi = zr * bti_ref[...] + zi * btr_ref[...]
        mask = _blockdiag_mask((SU, SW))
        for j in range(SJ):
            cols = slice(j * SW, (j + 1) * SW)
            for c, (vb, vc) in enumerate(((bbr, ctr_ref[...]), (bbi, cti_ref[...]))):
                bb_ref[c, j] = jnp.where(mask, jnp.tile(vb[:, cols], (SU // GH, 1)), 0.0).astype(bf16)
                cb_ref[c, j] = jnp.where(mask, jnp.tile(vc[:, cols], (SU // GH, 1)), 0.0).astype(bf16)

    return pl.pallas_call(
        body, name="s5_params",
        out_shape=[jax.ShapeDtypeStruct((2, 8, 8, NSTATE), f32),
                   jax.ShapeDtypeStruct((2, SJ, SU, SW), bf16), jax.ShapeDtypeStruct((2, SJ, SU, SW), bf16)],
        compiler_params=pltpu.CompilerParams(vmem_limit_bytes=56 * 1024 * 1024))(lr, li, ld, btr, bti, ctr, cti)


def s5_params_bwd(lr, li, ld, btr, bti, d_a, d_bb, d_cb):
    def body(lr_ref, li_ref, ld_ref, btr_ref, bti_ref, da_ref, dbb_ref, dcb_ref,
             dlr_ref, dli_ref, dld_ref, dbt_ref, dct_ref):
        mask = _blockdiag_mask((SU, SW))

        def fold(ref, c):
            parts = []
            for j in range(SJ):
                v = jnp.where(mask, ref[c, j], 0.0)
                parts.append(v.reshape(SU // GH, GH, SW).sum(axis=0))
            return jnp.concatenate(parts, axis=1)

        dct_ref[0] = fold(dcb_ref, 0)
        dct_ref[1] = fold(dcb_ref, 1)
        dbbr, dbbi = fold(dbb_ref, 0), fold(dbb_ref, 1)
        lrv, liv, ldv = lr_ref[...], li_ref[...], ld_ref[...]
        (ar, ai, zr, zi), vjp = jax.vjp(_disc, lrv, liv, ldv)
        btr, bti = btr_ref[...], bti_ref[...]
        dbt_ref[0] = zr * dbbr + zi * dbbi
        dbt_ref[1] = zr * dbbi - zi * dbbr
        dzr = jnp.sum(dbbr * btr + dbbi * bti, axis=0, keepdims=True)
        dzi = jnp.sum(dbbi * btr - dbbr * bti, axis=0, keepdims=True)
        dlr, dli, dld = vjp((da_ref[0:1, :], da_ref[1:2, :], dzr, dzi))
        dlr_ref[...] = dlr
        dli_ref[...] = dli
        ind = (lax.broadcasted_iota(jnp.int32, (NSTATE, 128), 0) // NS
               == lax.broadcasted_iota(jnp.int32, (NSTATE, 128), 1)).astype(f32)
        dld_ref[...] = _dot_hi(jnp.broadcast_to(dld, (8, NSTATE)), ind)

    return pl.pallas_call(
        body, name="s5_params_bwd",
        out_shape=[jax.ShapeDtypeStruct((1, NSTATE), f32), jax.ShapeDtypeStruct((1, NSTATE), f32),
                   jax.ShapeDtypeStruct((8, 128), f32), jax.ShapeDtypeStruct((2, GH, NSTATE), f32),
                   jax.ShapeDtypeStruct((2, GH, NSTATE), f32)],
        compiler_params=pltpu.CompilerParams(vmem_limit_bytes=56 * 1024 * 1024))(lr, li, ld, btr, bti, d_a, d_bb, d_cb)


def _fma(sr, si, ar, ai, qr, qi):
    return sr + ar * qr - ai * qi, si + ar * qi + ai * qr


def _scan_tile(sr, si, cr, ci, t8_ref, reverse):
    sg = -1.0 if reverse else 1.0
    for k in range(3):
        tk = 3 + k if reverse else k
        rot = 8 - (1 << k) if reverse else 1 << k
        sr, si = _fma(sr, si, t8_ref[0, tk], sg * t8_ref[1, tk], pltpu.roll(sr, rot, 0), pltpu.roll(si, rot, 0))
    tp = 7 if reverse else 6
    sr, si = _fma(sr, si, t8_ref[0, tp], sg * t8_ref[1, tp], cr, ci)
    e = 0 if reverse else 7
    return sr, si, jnp.broadcast_to(sr[e:e + 1, :], sr.shape), jnp.broadcast_to(si[e:e + 1, :], si.shape)


S5MC = 512


def _s5_input_map(u_ref, bb_ref, sr_sc, si_sc, l):
    for c in range(l // S5MC):
        rows = slice(c * S5MC, (c + 1) * S5MC)
        u = u_ref[0, rows, :]
        sr_sc[rows, :] = _dot(u, bb_ref[0, 0])
        si_sc[rows, :] = _dot(u, bb_ref[1, 0])


S5TU = 4


def _s5_forward_scan(sr_sc, si_sc, t8_ref, l):
    def step(k, carry):
        cr, ci = carry
        for j in range(S5TU):
            rows = pl.ds(pl.multiple_of(k * (8 * S5TU), 8 * S5TU) + 8 * j, 8)
            sr, si, cr, ci = _scan_tile(sr_sc[rows, :], si_sc[rows, :], cr, ci, t8_ref, False)
            sr_sc[rows, :] = sr
            si_sc[rows, :] = si
        return cr, ci

    zero = jnp.zeros((8, SW), f32)
    lax.fori_loop(0, l // (8 * S5TU), step, (zero, zero))


def s5_fwd(proj3, t8, bb, cb, dskip):
    b, l, _ = proj3.shape

    def body(u_ref, t8_ref, bb_ref, cb_ref, d_ref, y_ref, sr_out, si_out):
        sr_sc, si_sc = sr_out.at[0], si_out.at[0]
        _s5_input_map(u_ref, bb_ref, sr_sc, si_sc, l)
        _s5_forward_scan(sr_sc, si_sc, t8_ref, l)
        for c in range(l // S5MC):
            rows = slice(c * S5MC, (c + 1) * S5MC)
            y = (_dot_nt(sr_sc[rows, :].astype(bf16), cb_ref[0, 0])
                 - _dot_nt(si_sc[rows, :].astype(bf16), cb_ref[1, 0]))
            y_ref[0, rows, :] = y + d_ref[...] * u_ref[0, rows, :].astype(f32)

    return pl.pallas_call(
        body, name="s5_fwd", grid=(SJ, b),
        in_specs=[pl.BlockSpec((1, l, SU), lambda j, bi: (bi, 0, j)),
                  pl.BlockSpec((2, 8, 8, SW), lambda j, bi: (0, 0, 0, j)),
                  pl.BlockSpec((2, 1, SU, SW), lambda j, bi: (0, j, 0, 0)),
                  pl.BlockSpec((2, 1, SU, SW), lambda j, bi: (0, j, 0, 0)),
                  pl.BlockSpec((1, SU), lambda j, bi: (0, j))],
        out_specs=[pl.BlockSpec((1, l, SU), lambda j, bi: (bi, 0, j)),
                   pl.BlockSpec((1, l, SW), lambda j, bi: (bi, 0, j)),
                   pl.BlockSpec((1, l, SW), lambda j, bi: (bi, 0, j))],
        out_shape=[jax.ShapeDtypeStruct((b, l, BW), f32), jax.ShapeDtypeStruct((b, l, NSTATE), f32),
                   jax.ShapeDtypeStruct((b, l, NSTATE), f32)],
        compiler_params=_cp(2))(proj3, t8, bb, cb, dskip)


def s5_bwd(proj3, dy, s_re, s_im, t8, bb, cb, dskip):
    b, l, _ = proj3.shape
    nt = l // 8

    def body(u_ref, dy_ref, sr_in, si_in, t8_ref, bb_ref, cb_ref, d_ref,
             du_ref, da_ref, dbb_ref, dcb_ref, dd_ref, gr_sc, gi_sc):
        bi = pl.program_id(1)
        sr_sc, si_sc = sr_in.at[0], si_in.at[0]

        @pl.when(bi == 0)
        def _():
            da_ref[...] = jnp.zeros_like(da_ref)
            dbb_ref[...] = jnp.zeros_like(dbb_ref)
            dcb_ref[...] = jnp.zeros_like(dcb_ref)
            dd_ref[...] = jnp.zeros_like(dd_ref)

        for c in range(l // S5MC):
            rows = slice(c * S5MC, (c + 1) * S5MC)
            dyb = dy_ref[0, rows, :].astype(bf16)
            gr_sc[rows, :] = _dot(dyb, cb_ref[0, 0])
            gi_sc[rows, :] = -_dot(dyb, cb_ref[1, 0])

        row = lax.broadcasted_iota(jnp.int32, (8, SW), 0)

        def step(i, carry):
            cr, ci, dar, dai = carry
            for j in range(S5TU):
                k = nt - 1 - (i * S5TU + j)
                rows = pl.ds(pl.multiple_of(k * 8, 8), 8)
                gr, gi, cr, ci = _scan_tile(gr_sc[rows, :], gi_sc[rows, :], cr, ci, t8_ref, True)
                gr_sc[rows, :] = gr
                gi_sc[rows, :] = gi
                before = pl.ds(pl.multiple_of(jnp.maximum(k - 1, 0) * 8, 8), 8)
                live = jnp.where(k > 0, 1.0, 0.0)
                sr, si = sr_sc[rows, :], si_sc[rows, :]
                spr = jnp.where(row == 0, live * sr_sc[before, :][7:8, :], pltpu.roll(sr, 1, 0))
                spi = jnp.where(row == 0, live * si_sc[before, :][7:8, :], pltpu.roll(si, 1, 0))
                dar, dai = dar + spr * gr + spi * gi, dai + spr * gi - spi * gr
            return cr, ci, dar, dai

        zero = jnp.zeros((8, SW), f32)
        _, _, dar, dai = lax.fori_loop(0, nt // S5TU, step, (zero, zero, zero, zero))
        da_ref[0:1, :] += jnp.sum(dar, axis=0, keepdims=True)
        da_ref[1:2, :] += jnp.sum(dai, axis=0, keepdims=True)

        for c in range(l // S5MC):
            rows = slice(c * S5MC, (c + 1) * S5MC)
            u = u_ref[0, rows, :]
            dyv = dy_ref[0, rows, :]
            dyb = dyv.astype(bf16)
            grb, gib = gr_sc[rows, :].astype(bf16), gi_sc[rows, :].astype(bf16)
            dcb_ref[0, 0] += _dot_tn(dyb, sr_sc[rows, :].astype(bf16))
            dcb_ref[1, 0] -= _dot_tn(dyb, si_sc[rows, :].astype(bf16))
            dbb_ref[0, 0] += _dot_tn(u, grb)
            dbb_ref[1, 0] += _dot_tn(u, gib)
            du = _dot_nt(grb, bb_ref[0, 0]) + _dot_nt(gib, bb_ref[1, 0]) + d_ref[...] * dyv
            du_ref[0, rows, :] = du.astype(bf16)
            dd_ref[...] += jnp.sum(dyv * u.astype(f32), axis=0, keepdims=True)

    seq = pl.BlockSpec((1, l, SU), lambda j, bi: (bi, 0, j))
    sts = pl.BlockSpec((1, l, SW), lambda j, bi: (bi, 0, j))
    tab = pl.BlockSpec((2, 1, SU, SW), lambda j, bi: (0, j, 0, 0))
    return pl.pallas_call(
        body, name="s5_bwd", grid=(SJ, b),
        in_specs=[seq, seq, sts, sts, pl.BlockSpec((2, 8, 8, SW), lambda j, bi: (0, 0, 0, j)), tab, tab,
                  pl.BlockSpec((1, SU), lambda j, bi: (0, j))],
        out_specs=[seq, pl.BlockSpec((2, SW), lambda j, bi: (0, j)), tab, tab,
                   pl.BlockSpec((1, SU), lambda j, bi: (0, j))],
        out_shape=[jax.ShapeDtypeStruct((b, l, BW), bf16), jax.ShapeDtypeStruct((2, NSTATE), f32),
                   jax.ShapeDtypeStruct((2, SJ, SU, SW), f32), jax.ShapeDtypeStruct((2, SJ, SU, SW), f32),
                   jax.ShapeDtypeStruct((1, BW), f32)],
        scratch_shapes=[pltpu.VMEM((l, SW), f32)] * 2,
        compiler_params=_cp(2))(proj3, dy, s_re, s_im, t8, bb, cb, dskip)


AHC = 2
AHW = AHC * 128


def _att_mask(n, nb):
    if nb == 1:
        qi = lax.broadcasted_iota(jnp.int32, (ABLK, ABLK), 0)
        kj = lax.broadcasted_iota(jnp.int32, (ABLK, ABLK), 1)
        return kj <= qi
    qi = lax.broadcasted_iota(jnp.int32, (ABLK, 2 * ABLK), 0)
    kj = lax.broadcasted_iota(jnp.int32, (ABLK, 2 * ABLK), 1)
    return (kj >= qi) & (kj <= qi + ABLK) & ((n > 0) | (kj >= ABLK))


def _att_rows(it, nb, dil):
    r, n = it // nb, it % nb
    cur = pl.ds(r + n * (ABLK * dil), ABLK, stride=dil)
    prv = pl.ds(r + jnp.maximum(n - 1, 0) * (ABLK * dil), ABLK, stride=dil)
    return n, cur, prv


def _att_keys(ref, c, cur, prv, nb):
    if nb == 1:
        x = ref[c, cur, :].astype(bf16)
    else:
        x = jnp.concatenate([ref[c, prv, :], ref[c, cur, :]], axis=0).astype(bf16)
    head0 = lax.broadcasted_iota(jnp.int32, x.shape, 1) < HD
    zero = jnp.zeros_like(x)
    return jnp.concatenate([jnp.where(head0, x, zero), jnp.where(head0, zero, x)], axis=0)


def _per_head(nk, a0, a1):
    col = lax.broadcasted_iota(jnp.int32, (ABLK, 2 * nk), 1)
    return jnp.where(col < nk, a0, a1)


def _to_chunks(src_ref, dst):
    for c in range(AHC):
        dst[c] = src_ref[0, :, c * 128:(c + 1) * 128].astype(f32)


def att_fwd(proj3, g_idx, dil):
    b, l, _ = proj3.shape
    nb = l // dil // ABLK
    nhalf = BW // AHW

    def body(q_ref, k_ref, v_ref, o_ref, lse_ref, qf, kf, vf, of):
        hh = pl.program_id(1)
        _to_chunks(q_ref, qf)
        _to_chunks(k_ref, kf)
        _to_chunks(v_ref, vf)
        lane = lax.broadcasted_iota(jnp.int32, (ABLK, 128), 1)

        def step(it, carry):
            n, cur, prv = _att_rows(it, nb, dil)
            valid = _att_mask(n, nb)
            nk = valid.shape[1]
            lse_all = jnp.zeros((ABLK, 128), f32)
            for c in range(AHC):
                q = (qf[c, cur, :] * ATT_SCALE).astype(bf16)
                k = _att_keys(kf, c, cur, prv, nb)
                v = _att_keys(vf, c, cur, prv, nb)
                head = hh * (2 * AHC) + 2 * c
                if nb == 1:
                    v2 = jnp.concatenate([valid, valid], axis=1)
                    s = jnp.where(v2, _dot_nt(q, k), NEG)
                    m0 = jnp.max(s[:, :nk], axis=-1, keepdims=True)
                    m1 = jnp.max(s[:, nk:], axis=-1, keepdims=True)
                    p = jnp.exp(s - _per_head(nk, m0, m1))
                    den0 = jnp.sum(p[:, :nk], axis=-1, keepdims=True)
                    den1 = jnp.sum(p[:, nk:], axis=-1, keepdims=True)
                    of[c, cur, :] = _dot(p.astype(bf16), v) * jnp.where(lane < HD, 1.0 / den0, 1.0 / den1)
                    lse_all = (lse_all + jnp.where(lane == head, m0 + jnp.log(den0), 0.0)
                               + jnp.where(lane == head + 1, m1 + jnp.log(den1), 0.0))
                    continue
                acc = None
                for hl in range(2):
                    s = jnp.where(valid, _dot_nt(q, k[hl * nk:(hl + 1) * nk]), NEG)
                    m = jnp.max(s, axis=-1, keepdims=True)
                    p = jnp.exp(s - m)
                    den = jnp.sum(p, axis=-1, keepdims=True)
                    o_h = _dot(p.astype(bf16), v[hl * nk:(hl + 1) * nk]) * (1.0 / den)
                    acc = o_h if acc is None else acc + o_h
                    lse_all = lse_all + jnp.where(lane == head + hl, m + jnp.log(den), 0.0)
                of[c, cur, :] = acc

            lse_ref[0, 0, cur, :] = lse_all
            return carry

        lax.fori_loop(0, dil * nb, step, 0, unroll=4)
        for c in range(AHC):
            o_ref[0, :, c * 128:(c + 1) * 128] = of[c].astype(bf16)

    col = lambda c: pl.BlockSpec((1, l, AHW), lambda bi, hh: (bi, 0, c * nhalf + hh))
    return pl.pallas_call(
        body, name=f"att_fwd{g_idx}", grid=(b, nhalf),
        in_specs=[col(1 + g_idx), col(4), col(5)],
        out_specs=[pl.BlockSpec((1, l, AHW), lambda bi, hh: (bi, 0, hh)),
                   pl.BlockSpec((1, 1, l, 128), lambda bi, hh: (bi, hh, 0, 0))],
        out_shape=[jax.ShapeDtypeStruct((b, l, BW), bf16), jax.ShapeDtypeStruct((b, nhalf, l, 128), f32)],
        scratch_shapes=[pltpu.VMEM((AHC, l, 128), f32)] * 4,
        compiler_params=_cp(2))(proj3, proj3, proj3)


def att_bwd(proj3, do, lse_tot, delta, g_idx, dil):
    b, l, _ = proj3.shape
    nb = l // dil // ABLK
    nhalf = BW // AHW

    def body(q_ref, k_ref, v_ref, do_ref, l_ref, dl_ref, dq_out, dk_out, dv_out, qf, kf, vf, dof,
             dq_ref, dk_ref, dv_ref):
        hh = pl.program_id(1)
        _to_chunks(q_ref, qf)
        _to_chunks(k_ref, kf)
        _to_chunks(v_ref, vf)
        _to_chunks(do_ref, dof)
        dk_ref[...] = jnp.zeros_like(dk_ref)
        dv_ref[...] = jnp.zeros_like(dv_ref)
        lane = lax.broadcasted_iota(jnp.int32, (ABLK, 128), 1)

        def step(it, carry):
            n, cur, prv = _att_rows(it, nb, dil)
            valid = _att_mask(n, nb)
            valid = jnp.concatenate([valid, valid], axis=1)
            nk = valid.shape[1] // 2
            lse_b = l_ref[0, cur, :]
            dl_b = dl_ref[0, cur, :]
            head0 = lax.broadcasted_iota(jnp.int32, (nk, 128), 1) < HD
            for c in range(AHC):
                q = (qf[c, cur, :] * ATT_SCALE).astype(bf16)
                dob = dof[c, cur, :].astype(bf16)
                k = _att_keys(kf, c, cur, prv, nb)
                v = _att_keys(vf, c, cur, prv, nb)
                head = hh * (2 * AHC) + 2 * c
                pick = lambda a, h: jnp.sum(jnp.where(lane == h, a, 0.0), axis=-1, keepdims=True)
                lse_h = _per_head(nk, pick(lse_b, head), pick(lse_b, head + 1))
                dl_h = _per_head(nk, pick(dl_b, head), pick(dl_b, head + 1))
                s = _dot_nt(q, k)
                p = jnp.where(valid, jnp.exp(jnp.minimum(s - lse_h, 60.0)), 0.0)
                ds = (p * (_dot_nt(dob, v) - dl_h)).astype(bf16)
                dq_ref[0, c, cur, :] = _dot(ds, k) * ATT_SCALE
                dk2 = _dot_tn(ds, q)
                dv2 = _dot_tn(p.astype(bf16), dob)
                dk = jnp.where(head0, dk2[:nk], dk2[nk:])
                dv = jnp.where(head0, dv2[:nk], dv2[nk:])
                if nb == 1:
                    dk_ref[0, c, cur, :] += dk
                    dv_ref[0, c, cur, :] += dv
                else:
                    dk_ref[0, c, cur, :] += dk[ABLK:]
                    dv_ref[0, c, cur, :] += dv[ABLK:]
                    dk_ref[0, c, prv, :] += dk[:ABLK]
                    dv_ref[0, c, prv, :] += dv[:ABLK]

            return carry

        lax.fori_loop(0, dil * nb, step, 0, unroll=4)
        dq_out[0] = dq_ref[0].astype(bf16)
        dk_out[0] = dk_ref[0].astype(bf16)
        dv_out[0] = dv_ref[0].astype(bf16)

    col = lambda c: pl.BlockSpec((1, l, AHW), lambda bi, hh: (bi, 0, c * nhalf + hh))
    own = pl.BlockSpec((1, l, AHW), lambda bi, hh: (bi, 0, hh))
    own128 = pl.BlockSpec((1, l, 128), lambda bi, hh: (bi, 0, 0))
    chunked = pl.BlockSpec((1, AHC, l, 128), lambda bi, hh: (bi, hh, 0, 0))
    return pl.pallas_call(
        body, name=f"att_bwd{g_idx}", grid=(b, nhalf),
        in_specs=[col(1 + g_idx), col(4), col(5), own, own128, own128],
        out_specs=[chunked] * 3,
        out_shape=[jax.ShapeDtypeStruct((b, BW // 128, l, 128), bf16)] * 3,
        scratch_shapes=[pltpu.VMEM((AHC, l, 128), f32)] * 4 + [pltpu.VMEM((1, AHC, l, 128), f32)] * 3,
        compiler_params=_cp(2, 56))(proj3, proj3, proj3, do, lse_tot, delta)


CPAD = 32
CTAIL = 16
CR = 128
CSLAB = CR + 40


def _tap_windows(slab, off, mis):
    ntap = (CW - 1 - mis) // 8 + 1
    rot = (off + mis) % 8
    base = off + mis - rot
    shifted = pltpu.roll(slab, CSLAB - rot, 0) if rot else slab
    for a in range(ntap):
        yield 8 * a + mis, shifted[base + 8 * a:base + 8 * a + CR]


def _fill_glu(cv_ref, pad, l):
    pad[0:CPAD, :] = jnp.zeros((CPAD, BW), f32)
    pad[CPAD:CPAD + l, :] = cv_ref[0, :, :BW].astype(f32) * _sigmoid(cv_ref[0, :, BW:].astype(f32))
    pad[CPAD + l:, :] = jnp.zeros((CTAIL, BW), f32)


def conv_fwd(proj3, cw, cb):
    b, l, _ = proj3.shape

    def body(cv_ref, w_ref, b_ref, o_ref, pad):
        _fill_glu(cv_ref, pad, l)
        for lc in range(BW // 128):
            lanes = slice(lc * 128, (lc + 1) * 128)
            wv = w_ref[:, lanes]

            def step(c, carry):
                base = pl.multiple_of(c * CR, CR)
                slab = pad[pl.ds(base, CSLAB), lanes]
                acc = jnp.zeros((CR, 128), f32) + b_ref[:, lanes]
                for mis in range(8):
                    for k, win in _tap_windows(slab, CPAD - (CW - 1), mis):
                        acc = acc + wv[k:k + 1] * win
                o_ref[0, pl.ds(base, CR), lanes] = acc
                return carry

            lax.fori_loop(0, l // CR, step, 0)

    return pl.pallas_call(
        body, name="conv_fwd", grid=(b,),
        in_specs=[pl.BlockSpec((1, l, 2 * BW), lambda i: (i, 0, 3)),
                  pl.BlockSpec((32, BW), lambda i: (0, 0)), pl.BlockSpec((1, BW), lambda i: (0, 0))],
        out_specs=pl.BlockSpec((1, l, BW), lambda i: (i, 0, 0)),
        out_shape=jax.ShapeDtypeStruct((b, l, BW), f32),
        scratch_shapes=[pltpu.VMEM((CPAD + l + CTAIL, BW), f32)], compiler_params=_cp(1))(proj3, cw, cb)


def conv_bwd(proj3, dhc, cw):
    b, l, _ = proj3.shape

    def body(cv_ref, d_ref, w_ref, dcv_ref, dw_ref, db_ref, pad, dpad):
        i = pl.program_id(0)

        @pl.when(i == 0)
        def _():
            dw_ref[...] = jnp.zeros_like(dw_ref)
            db_ref[...] = jnp.zeros_like(db_ref)

        _fill_glu(cv_ref, pad, l)
        dpad[0:l, :] = d_ref[0]
        dpad[l:, :] = jnp.zeros((CPAD + CTAIL, BW), f32)
        db_ref[...] += jnp.sum(d_ref[0], axis=0, keepdims=True)
        for lc in range(BW // 128):
            lanes = slice(lc * 128, (lc + 1) * 128)
            glanes = slice(BW + lc * 128, BW + (lc + 1) * 128)
            wv = w_ref[:, lanes]

            for mis in range(8):
                ntap = (CW - 1 - mis) // 8 + 1

                def dw_step(c, accs, mis=mis, lanes=lanes):
                    base = pl.multiple_of(c * CR, CR)
                    slab = pad[pl.ds(base, CSLAB), lanes]
                    dv = dpad[pl.ds(base, CR), lanes]
                    return tuple(acc + (dv * win).reshape(CR // 8, 8, 128).sum(axis=0) for acc, (_, win)
                                 in zip(accs, _tap_windows(slab, CPAD - (CW - 1), mis)))

                accs = lax.fori_loop(0, l // CR, dw_step, tuple(jnp.zeros((8, 128), f32) for _ in range(ntap)))
                for a in range(ntap):
                    k = 8 * a + mis
                    dw_ref[k:k + 1, lanes] += jnp.sum(accs[a], axis=0, keepdims=True)

            def dh_step(c, carry, lanes=lanes, glanes=glanes, wv=wv):
                base = pl.multiple_of(c * CR, CR)
                slab = dpad[pl.ds(base, CSLAB), lanes]
                acc = jnp.zeros((CR, 128), f32)
                for mis in range(8):
                    for kk, win in _tap_windows(slab, 0, mis):
                        acc = acc + wv[CW - 1 - kk:CW - kk] * win
                rows = pl.ds(base, CR)
                a = cv_ref[0, rows, lanes].astype(f32)
                sg = _sigmoid(cv_ref[0, rows, glanes].astype(f32))
                dcv_ref[0, rows, lanes] = (acc * sg).astype(bf16)
                dcv_ref[0, rows, glanes] = (acc * a * sg * (1.0 - sg)).astype(bf16)
                return carry

            lax.fori_loop(0, l // CR, dh_step, 0)

    return pl.pallas_call(
        body, name="conv_bwd", grid=(b,),
        in_specs=[pl.BlockSpec((1, l, 2 * BW), lambda i: (i, 0, 3)),
                  pl.BlockSpec((1, l, BW), lambda i: (i, 0, 0)),
                  pl.BlockSpec((32, BW), lambda i: (0, 0))],
        out_specs=[pl.BlockSpec((1, l, 2 * BW), lambda i: (i, 0, 0)),
                   pl.BlockSpec((32, BW), lambda i: (0, 0)), pl.BlockSpec((1, BW), lambda i: (0, 0))],
        out_shape=[jax.ShapeDtypeStruct((b, l, 2 * BW), bf16), jax.ShapeDtypeStruct((32, BW), f32),
                   jax.ShapeDtypeStruct((1, BW), f32)],
        scratch_shapes=[pltpu.VMEM((CPAD + l + CTAIL, BW), f32), pltpu.VMEM((l + CPAD + CTAIL, BW), f32)],
        compiler_params=_cp(1))(proj3, dhc, cw)


def _head_expand():
    r = lax.broadcasted_iota(jnp.int32, (128, BW), 0)
    c = lax.broadcasted_iota(jnp.int32, (128, BW), 1) // HD
    return (r == c).astype(f32)


def _head_reduce():
    r = lax.broadcasted_iota(jnp.int32, (BW, 128), 0) // HD
    c = lax.broadcasted_iota(jnp.int32, (BW, 128), 1)
    return (r == c).astype(f32)


def _merge_common(ys_ref, o_refs, l_refs, hc_ref, g_refs, bg_ref, lng_ref, lnb_ref, wglu_ref, watt_ref, wpw_ref):
    r = {}
    ysv = ys_ref[...]
    r["ys"] = ysv
    r["ysin"] = _gelu(ysv).astype(bf16)
    z = _dot(r["ysin"], wglu_ref[...])
    r["z1"], r["sg2"] = z[:, :D], _sigmoid(z[:, D:])
    r["y_s"] = r["z1"] * r["sg2"]
    ls = [lr_[0, 0] + lr_[0, 1] for lr_ in l_refs]
    mx = jnp.maximum(jnp.maximum(ls[0], ls[1]), ls[2])
    es = [jnp.exp(v - mx) for v in ls]
    tot = es[0] + es[1] + es[2]
    r["lse_tot"] = mx + jnp.log(tot)
    e_mat = _head_expand()
    o = jnp.zeros(ysv.shape, f32)
    for e, o_ref in zip(es, o_refs):
        o = o + _dot_sel(e / tot, e_mat) * o_ref[...].astype(f32)
    r["o"] = o
    r["ob"] = o.astype(bf16)
    r["y_a"] = _dot(r["ob"], watt_ref[...])
    hc = hc_ref[...]
    mu = jnp.mean(hc, axis=-1, keepdims=True)
    xc = hc - mu
    rstd = lax.rsqrt(jnp.mean(xc * xc, axis=-1, keepdims=True) + EPS)
    r["xh"], r["rstd"] = xc * rstd, rstd
    hn = r["xh"] * lng_ref[...] + lnb_ref[...]
    r["hn"] = hn
    r["sgn"] = _sigmoid(hn)
    r["hs"] = (hn * r["sgn"]).astype(bf16)
    r["y_c"] = _dot(r["hs"], wpw_ref[...])
    r["gates"] = [_sigmoid(g_refs[k][...].astype(f32) + bg_ref[:, k * D:(k + 1) * D]) for k in range(3)]
    r["merged"] = r["gates"][0] * r["y_s"] + r["gates"][1] * r["y_a"] + r["gates"][2] * r["y_c"]
    return r


TBM = 256


def _merge_in_specs(tok, tb, lses):
    w = lambda shape: pl.BlockSpec(shape, lambda i: (0, 0), pipeline_mode=pl.Buffered(1))
    nbl = lses[0].shape[2] // tb
    return ([pl.BlockSpec((tb, D), tok), pl.BlockSpec((tb, BW), tok)]
            + [pl.BlockSpec((tb, BW), tok)] * 3
            + [pl.BlockSpec((1, 2, tb, 128), lambda i: (i // nbl, 0, i % nbl, 0))] * 3
            + [pl.BlockSpec((tb, BW), tok)]
            + [pl.BlockSpec((tb, D), lambda i, k=k: (i, 4 + k)) for k in range(3)]
            + [w((1, 3 * D)), w((1, BW)), w((1, BW)), w((BW, 2 * D)), w((BW, D)), w((BW, D)), w((D, D))])


def merge_fwd(x, ys, os_, lses, hc, proj, bg, lng, lnb, wglu, watt, wpw, wout):
    n = x.shape[0]

    def body(x_ref, ys_ref, o1, o2, o3, l1, l2, l3, hc_ref, g0, g1, g2, bg_ref, lng_ref, lnb_ref,
             wglu_ref, watt_ref, wpw_ref, wout_ref, x1_ref):
        r = _merge_common(ys_ref, (o1, o2, o3), (l1, l2, l3), hc_ref, (g0, g1, g2), bg_ref, lng_ref, lnb_ref,
                          wglu_ref, watt_ref, wpw_ref)
        x1_ref[...] = x_ref[...] + _dot(r["merged"].astype(bf16), wout_ref[...])

    tok = lambda i: (i, 0)
    return pl.pallas_call(
        body, name="merge_fwd", grid=(n // TB,), in_specs=_merge_in_specs(tok, TB, lses),
        out_specs=pl.BlockSpec((TB, D), tok), out_shape=jax.ShapeDtypeStruct((n, D), f32),
        compiler_params=_cp(1, 56))(x, ys, *os_, *lses, hc, proj, proj, proj, bg, lng, lnb, wglu, watt, wpw, wout)


def merge_bwd(dx1, ys, os_, lses, hc, proj, bg, lng, lnb, wglu, watt, wpw, wout):
    n = dx1.shape[0]

    def body(dx_ref, ys_ref, o1, o2, o3, l1, l2, l3, hc_ref, g0, g1, g2, bg_ref, lng_ref, lnb_ref,
             wglu_ref, watt_ref, wpw_ref, wout_ref,
             dys_ref, do_ref, delta_ref, ltot_ref, dhc_ref, dgate_ref, ysin_ref, dz_ref, ob_ref, dya_ref,
             hs_ref, dyc_ref, mg_ref, dbg_ref, dlng_ref, dlnb_ref):
        i = pl.program_id(0)

        @pl.when(i == 0)
        def _():
            dbg_ref[...] = jnp.zeros_like(dbg_ref)
            dlng_ref[...] = jnp.zeros_like(dlng_ref)
            dlnb_ref[...] = jnp.zeros_like(dlnb_ref)

        r = _merge_common(ys_ref, (o1, o2, o3), (l1, l2, l3), hc_ref, (g0, g1, g2), bg_ref, lng_ref, lnb_ref,
                          wglu_ref, watt_ref, wpw_ref)
        mg_ref[...] = r["merged"].astype(bf16)
        ysin_ref[...] = r["ysin"]
        ob_ref[...] = r["ob"]
        hs_ref[...] = r["hs"]
        ltot_ref[...] = r["lse_tot"]
        dm = _dot_nt(dx_ref[...].astype(bf16), wout_ref[...])
        ys3 = (r["y_s"], r["y_a"], r["y_c"])
        for k in range(3):
            gk = r["gates"][k]
            dgr = dm * ys3[k] * gk * (1.0 - gk)
            dgate_ref[:, k * D:(k + 1) * D] = dgr.astype(bf16)
            dbg_ref[:, k * D:(k + 1) * D] += jnp.sum(dgr, axis=0, keepdims=True)
        dy_s = dm * r["gates"][0]
        sg2 = r["sg2"]
        dz = jnp.concatenate([dy_s * sg2, dy_s * r["z1"] * sg2 * (1.0 - sg2)], axis=1).astype(bf16)
        dz_ref[...] = dz
        dys_ref[...] = _dot_nt(dz, wglu_ref[...]) * _gelu_grad(r["ys"])
        dya = (dm * r["gates"][1]).astype(bf16)
        dya_ref[...] = dya
        do = _dot_nt(dya, watt_ref[...])
        do_ref[...] = do.astype(bf16)
        delta_ref[...] = _dot_sel(do * r["o"], _head_reduce())
        dyc = (dm * r["gates"][2]).astype(bf16)
        dyc_ref[...] = dyc
        sgn, hn = r["sgn"], r["hn"]
        dhn = _dot_nt(dyc, wpw_ref[...]) * sgn * (1.0 + hn * (1.0 - sgn))
        dlng_ref[...] += jnp.sum(dhn * r["xh"], axis=0, keepdims=True)
        dlnb_ref[...] += jnp.sum(dhn, axis=0, keepdims=True)
        dxh = dhn * lng_ref[...]
        xh = r["xh"]
        dhc_ref[...] = r["rstd"] * (dxh - jnp.mean(dxh, axis=-1, keepdims=True)
                                    - xh * jnp.mean(dxh * xh, axis=-1, keepdims=True))

    tok = lambda i: (i, 0)
    fix = lambda i: (0, 0)
    outs = [("dys", BW, f32), ("do", BW, bf16), ("delta", 128, f32), ("lse_tot", 128, f32), ("dhc", BW, f32),
            ("dgate", 3 * D, bf16), ("ysin", BW, bf16), ("dz", 2 * D, bf16), ("ob", BW, bf16), ("dya", D, bf16),
            ("hs", BW, bf16), ("dyc", D, bf16), ("merged", D, bf16)]
    small = [("dbg", 3 * D), ("dlng", BW), ("dlnb", BW)]
    res = pl.pallas_call(
        body, name="merge_bwd", grid=(n // TBM,), in_specs=_merge_in_specs(tok, TBM, lses),
        out_specs=[pl.BlockSpec((TBM, w), tok) for _, w, _ in outs] + [pl.BlockSpec((1, w), fix) for _, w in small],
        out_shape=[jax.ShapeDtypeStruct((n, w), dt) for _, w, dt in outs]
        + [jax.ShapeDtypeStruct((1, w), f32) for _, w in small],
        compiler_params=_cp(1, 56))(dx1, ys, *os_, *lses, hc, proj, proj, proj, bg, lng, lnb, wglu, watt, wpw, wout)
    return dict(zip([k for k, _, _ in outs] + [k for k, _ in small], res))


def assemble_dproj(du, dqs, dks, dvs, dcv, dgate):
    b, l, _ = du.shape
    nck = BW // 128

    def body(du_ref, q1, q2, q3, k1, k2, k3, v1, v2, v3, cv_ref, g_ref, o_ref):
        o_ref[0, :, 0:BW] = du_ref[0]
        for c in range(nck):
            for j, qr in enumerate((q1, q2, q3)):
                o_ref[0, :, (1 + j) * BW + c * 128:(1 + j) * BW + (c + 1) * 128] = qr[0, c]
            add3 = lambda r1, r2, r3: (r1[0, c].astype(f32) + r2[0, c].astype(f32) + r3[0, c].astype(f32)).astype(bf16)
            o_ref[0, :, 4 * BW + c * 128:4 * BW + (c + 1) * 128] = add3(k1, k2, k3)
            o_ref[0, :, 5 * BW + c * 128:5 * BW + (c + 1) * 128] = add3(v1, v2, v3)
        o_ref[0, :, 6 * BW:8 * BW] = cv_ref[0]
        o_ref[0, :, 8 * BW:] = g_ref[0]

    t = lambda w: pl.BlockSpec((1, TB, w), lambda bi, i: (bi, i, 0))
    ck = pl.BlockSpec((1, nck, TB, 128), lambda bi, i: (bi, 0, i, 0))
    return pl.pallas_call(
        body, name="assemble_dproj", grid=(b, l // TB),
        in_specs=[t(BW)] + [ck] * 9 + [t(2 * BW), t(3 * D)], out_specs=t(INC),
        out_shape=jax.ShapeDtypeStruct((b, l, INC), bf16), compiler_params=_cp(2))(du, *dqs, *dks, *dvs, dcv, dgate)


def _me():
    return lax.axis_index("x"), lax.axis_index("y"), lax.axis_index("c")


def _peers():
    x, y, c = _me()
    return [(x, y, 1 - c), (1 - x, y, c), (1 - x, y, 1 - c), (x, 1 - y, c), (x, 1 - y, 1 - c),
            (1 - x, 1 - y, c), (1 - x, 1 - y, 1 - c)]


def _rank(p):
    return 4 * p[0] + 2 * p[1] + p[2]


def allgather(arrs, name):
    na = len(arrs)
    units = [(a, j) for a in range(na) for j in range(arrs[a].shape[0])]
    nu = len(units)

    def body(*refs):
        ins, outs = refs[:na], refs[na:2 * na]
        send, recv, loc = refs[2 * na:]
        me = _rank(_me())
        local, remote = [], []
        for u, (a, j) in enumerate(units):
            own = pltpu.make_async_copy(ins[a].at[j], outs[a].at[j, me], loc.at[u])
            own.start()
            local.append(own)
        for u, (a, j) in enumerate(units):
            for k, p in enumerate(_peers()):
                cp = pltpu.make_async_remote_copy(src_ref=ins[a].at[j], dst_ref=outs[a].at[j, me],
                                                  send_sem=send.at[u, k], recv_sem=recv.at[u, k],
                                                  device_id=p, device_id_type=MESH)
                cp.start()
                remote.append(cp)
        for cp in local:
            cp.wait()
        for cp in remote:
            cp.wait()

    return pl.pallas_call(
        body, name=name, in_specs=[ANY] * na, out_specs=[ANY] * na,
        out_shape=[jax.ShapeDtypeStruct((a.shape[0], NDEV) + a.shape[1:], a.dtype) for a in arrs],
        scratch_shapes=[pltpu.SemaphoreType.DMA((nu, NDEV - 1)), pltpu.SemaphoreType.DMA((nu, NDEV - 1)),
                        pltpu.SemaphoreType.DMA((nu,))])(*arrs)


HBM = pl.BlockSpec(memory_space=pltpu.HBM)
SEM = pl.BlockSpec(memory_space=pltpu.SEMAPHORE)
_EFFECT = pltpu.SideEffectType.DATAFLOW_SIDE_EFFECTING


def _rank_slot(ref, r):
    if ref.shape[0] == NDEV:
        return ref.at[r]
    n = ref.shape[2] // 2
    return ref.at[r // 2, :, pl.ds(pl.multiple_of((r % 2) * n, 128), n)]


def _push_copies(srcs, lands, send, recv, scatter):
    me = _rank(_me())
    out = []
    for i in range(len(srcs)):
        for k, p in enumerate(_peers()):
            src = _rank_slot(srcs[i], _rank(p)) if scatter else srcs[i]
            dst = lands[i].at[k] if scatter else _rank_slot(lands[i], me)
            j = i * (NDEV - 1) + k
            out.append(pltpu.make_async_remote_copy(src_ref=src, dst_ref=dst, send_sem=send.at[j],
                                                    recv_sem=recv.at[j], device_id=p, device_id_type=MESH))
    return out


def push_start(srcs, lands, scatter, name, token):
    n = len(srcs)
    token = jnp.zeros((8, 128), f32) if token is None else token

    def body(*refs):
        for cp in _push_copies(refs[:n], refs[n:2 * n], refs[2 * n + 1], refs[2 * n + 2], scatter):
            cp.start()
        refs[-1][...] = refs[2 * n][...]

    sems = pltpu.SemaphoreType.DMA((n * (NDEV - 1),))
    vmem = pl.BlockSpec(memory_space=pltpu.VMEM)
    res = pl.pallas_call(
        body, name=name, in_specs=[HBM] * (2 * n) + [vmem], out_specs=[SEM, SEM] + [HBM] * (2 * n) + [vmem],
        out_shape=[sems, sems] + [pltpu.HBM(a.shape, a.dtype) for a in list(srcs) + list(lands)]
        + [jax.ShapeDtypeStruct((8, 128), f32)],
        input_output_aliases={i: 2 + i for i in range(2 * n)},
        compiler_params=pltpu.CompilerParams(has_side_effects=_EFFECT),
    )(*[pltpu.with_memory_space_constraint(a, pltpu.HBM) for a in list(srcs) + list(lands)], token)
    return res[0], res[1], res[2:2 + n], res[2 + n:2 + 2 * n], res[-1]


def push_wait(send, recv, srcs, lands, after, scatter, name):
    n = len(srcs)
    after = list(after) if isinstance(after, (list, tuple)) else [after]

    def body(*refs):
        for cp in _push_copies(refs[:n], refs[n:2 * n], refs[2 * n], refs[2 * n + 1], scatter):
            cp.wait_send()
            cp.wait_recv()

    res = pl.pallas_call(
        body, name=name, in_specs=[HBM] * (2 * n) + [SEM, SEM] + [ANY] * len(after), out_specs=[HBM] * (2 * n),
        out_shape=[pltpu.HBM(a.shape, a.dtype) for a in list(srcs) + list(lands)],
        input_output_aliases={i: i for i in range(2 * n)},
        compiler_params=pltpu.CompilerParams(has_side_effects=_EFFECT),
    )(*srcs, *lands, send, recv, *after)
    return res[:n], res[n:]


_C1 = 1.0 / (1.0 - ADAM_B1 ** ADAM_STEP)
_C2 = 1.0 / (1.0 - ADAM_B2 ** ADAM_STEP)


def _adamw(w, g, m, v):
    m = ADAM_B1 * m + (1.0 - ADAM_B1) * g
    v = ADAM_B2 * v + (1.0 - ADAM_B2) * (g * g)
    delta = -ADAM_LR * ((m * _C1) / (jnp.sqrt(v * _C2) + ADAM_EPS) + ADAM_WD * w)
    return delta, m, v


def adam_big(lands, owns, w, m, v, name):
    _, k, n = lands[0].shape
    tk = k
    while tk * n * 2 * NDEV > 2 * 1024 * 1024 and tk % 16 == 0:
        tk //= 2

    def body(*refs):
        l_refs, o_refs = refs[:DEPTH], refs[DEPTH:2 * DEPTH]
        w_ref, m_ref, v_ref, g_ref, d_ref, nm_ref, nv_ref = refs[2 * DEPTH:]
        for l in range(DEPTH):
            g = o_refs[l][...].astype(f32)
            for s in range(NDEV - 1):
                g = g + l_refs[l][s].astype(f32)
            d, nm, nv = _adamw(w_ref[l], g, m_ref[l], v_ref[l])
            g_ref[l], d_ref[l], nm_ref[l], nv_ref[l] = g, d, nm, nv

    blk = pl.BlockSpec((DEPTH, tk, n), lambda i: (0, i, 0))
    return pl.pallas_call(
        body, name=name, grid=(k // tk,),
        in_specs=[pl.BlockSpec((NDEV - 1, tk, n), lambda i: (0, i, 0))] * DEPTH
        + [pl.BlockSpec((tk, n), lambda i: (i, 0))] * DEPTH + [blk, blk, blk],
        out_specs=[blk] * 4, out_shape=[jax.ShapeDtypeStruct(w.shape, f32)] * 4,
        compiler_params=_cp(1))(*lands, *owns, w, m, v)


def adam_small(gath, w, m, v):
    r = w.shape[0]
    tr = 512

    def body(g_ref, w_ref, m_ref, v_ref, go_ref, d_ref, nm_ref, nv_ref):
        g = g_ref[0]
        for s in range(1, NDEV):
            g = g + g_ref[s]
        d, nm, nv = _adamw(w_ref[...], g, m_ref[...], v_ref[...])
        go_ref[...], d_ref[...], nm_ref[...], nv_ref[...] = g, d, nm, nv

    blk = pl.BlockSpec((tr, 128), lambda i: (i, 0))
    return pl.pallas_call(
        body, name="adam_small", grid=(r // tr,),
        in_specs=[pl.BlockSpec((NDEV, tr, 128), lambda i: (0, i, 0)), blk, blk, blk],
        out_specs=[blk] * 4, out_shape=[jax.ShapeDtypeStruct((r, 128), f32)] * 4,
        compiler_params=_cp(1))(gath, w, m, v)


SMALL = ["norm1_g", "b_gate", "ssm_lambda_re", "ssm_lambda_im", "ssm_log_dt", "ssm_b_re", "ssm_b_im",
         "ssm_c_re", "ssm_c_im", "ssm_d", "conv_w", "conv_b", "conv_ln_g", "conv_ln_b", "norm2_g", "final_g"]
BIG = ["w_in", "w_ssm_glu", "w_att_up", "w_conv_pw2", "w_out", "w_ffn_in", "w_ffn_out"]
ORDER = ["norm1_g", "w_in", "b_gate", "ssm_lambda_re", "ssm_lambda_im", "ssm_log_dt", "ssm_b_re", "ssm_b_im",
         "ssm_c_re", "ssm_c_im", "ssm_d", "w_ssm_glu", "w_att_up", "conv_w", "conv_b", "conv_ln_g", "conv_ln_b",
         "w_conv_pw2", "w_out", "norm2_g", "w_ffn_in", "w_ffn_out", "final_g"]
PACK_ROWS = 2560


def _pack(arrs):
    flat = jnp.concatenate([a.reshape(-1).astype(f32) for a in arrs])
    return jnp.pad(flat, (0, PACK_ROWS * 128 - flat.shape[0])).reshape(PACK_ROWS, 128)


def _unpack(pack, shapes):
    flat = pack.reshape(-1)
    out, off = [], 0
    for s in shapes:
        sz = math.prod(s)
        out.append(flat[off:off + sz].reshape(s))
        off += sz
    return out


def _bt(b):
    return b.transpose(2, 0, 1).reshape(GH, NSTATE)


def _bt_inv(bt):
    return bt.reshape(GH, NG, NS).transpose(1, 2, 0)


def _ct(c):
    return c.transpose(1, 0, 2).reshape(GH, NSTATE)


def _ct_inv(ct):
    return ct.reshape(GH, NG, NS).transpose(1, 0, 2)


def local_step(x, loss_target, P, weights, on_grads, start_token=None):
    bsz, seq, _ = x.shape
    n = bsz * seq

    def natural(g3):
        return g3.transpose(1, 0, 2).reshape(g3.shape[1], NDEV * g3.shape[2])

    tokens = [] if start_token is None else [start_token]

    def after_pushes(a):
        while tokens:
            a = a + tokens.pop()[0:1, 0:1]
        return a

    def pushed(tok):
        if tok is not None:
            tokens.append(tok)

    s5_in, s5_tabs = [], []
    for l in range(DEPTH):
        lr = P["ssm_lambda_re"][l].reshape(1, NSTATE)
        li = P["ssm_lambda_im"][l].reshape(1, NSTATE)
        ld = jnp.repeat(P["ssm_log_dt"][l], NS).reshape(1, NSTATE)
        btr, bti = _bt(P["ssm_b_re"][l]), _bt(P["ssm_b_im"][l])
        s5_in.append((lr, li, ld, btr, bti))
        s5_tabs.append(s5_params(lr, li, ld, btr, bti, _ct(P["ssm_c_re"][l]), _ct(P["ssm_c_im"][l])))

    xs = x.reshape(n, D)
    saved = []
    conv_w_pad = None
    for l in range(DEPTH):
        S = {"x": xs}
        h1 = rms_fwd(xs, after_pushes(P["norm1_g"][l][None]))
        G = dict(weights(l, "in", [h1] + [t for tabs in s5_tabs for t in tabs] if l == 0 else h1))
        if conv_w_pad is None:
            conv_w_full = G["conv_w"].transpose(1, 2, 0, 3).reshape(DEPTH, CW, BW)
            conv_w_pad = jnp.pad(conv_w_full, ((0, 0), (0, 1), (0, 0)))
        w_in4 = G["w_in"][None]
        proj = inproj(h1, w_in4, 0)
        proj3 = proj.reshape(bsz, seq, INC)
        lr, li, ld, btr, bti = s5_in[l]
        t8, bb, cb = s5_tabs[l]
        dskip = P["ssm_d"][l][None]
        ys, s_re, s_im = s5_fwd(proj3, t8, bb, cb, dskip)
        att = [att_fwd(proj3, gi, dil) for gi, (_, dil) in enumerate(PATTERNS)]
        hc = conv_fwd(proj3, conv_w_pad[l], P["conv_b"][l][None])
        G.update(weights(l, "mix", hc))
        wts = dict(wglu=natural(G["w_ssm_glu"]), watt=natural(G["w_att_up"]),
                   wpw=natural(G["w_conv_pw2"]), wout=G["w_out"].reshape(D, D))
        mi = dict(ys=ys.reshape(n, BW), os_=[a[0].reshape(n, BW) for a in att],
                  lses=[a[1] for a in att], hc=hc.reshape(n, BW),
                  proj=proj, bg=P["b_gate"][l][None], lng=P["conv_ln_g"][l][None], lnb=P["conv_ln_b"][l][None],
                  **wts)
        x1 = merge_fwd(xs, **mi)
        G.update(weights(l, "ffn", x1))
        w_ffn = (G["w_ffn_in"][None], G["w_ffn_out"][None])
        x2, z1s, z2s, h2 = ffn_fwd(x1, P["norm2_g"][l][None], *w_ffn, 0)
        S.update(h1=h1, proj=proj, proj3=proj3, tabs=(s_re, s_im, t8, bb, cb), mi=mi, x1=x1, w_in4=w_in4, w_ffn=w_ffn,
                 zs=(z1s, z2s), h2=h2,
                 sp=(lr, li, ld, btr, bti), dskip=dskip)
        saved.append(S)
        xs = x2

    loss8, dx, dfinal = loss_head(xs, P["final_g"][None], loss_target.reshape(n, D))

    small_g = {k: [None] * DEPTH for k in SMALL if k != "final_g"}
    tokblk = lambda w: pl.BlockSpec((1024, w), lambda s, i: (i, 0))
    colblk = lambda w: pl.BlockSpec((1024, w), lambda s, i: (i, s))
    for l in reversed(range(DEPTH)):
        S = saved[l]
        g2 = P["norm2_g"][l][None]
        dh4, dwa, dwb, dw2 = ffn_bwd(S["h2"], dx, *S["zs"], *S["w_ffn"], 0)
        dx1, dg2 = norm_bwd_fin(S["x1"], after_pushes(g2), dx, dh4, "ffn_bwd_fin")
        small_g["norm2_g"][l] = dg2
        pushed(on_grads(l, "ffn", dict(w_ffn_in=jnp.concatenate([dwa, dwb], axis=0),
                                       w_ffn_out=dw2.reshape(NDEV, NSH_FF // 2, D))))
        mb = merge_bwd(dx1, **dict(S["mi"], lng=after_pushes(S["mi"]["lng"])))
        small_g["b_gate"][l], small_g["conv_ln_g"][l], small_g["conv_ln_b"][l] = mb["dbg"], mb["dlng"], mb["dlnb"]
        dws = dw_mix(mb["ysin"], mb["dz"], mb["ob"], mb["dya"], mb["hs"], mb["dyc"], mb["merged"], dx1)
        pushed(on_grads(l, "mix", dict(zip(("w_ssm_glu", "w_att_up", "w_conv_pw2", "w_out"), dws))))
        dcv, dcw, dcb = conv_bwd(S["proj3"], mb["dhc"].reshape(bsz, seq, BW), after_pushes(conv_w_pad[l]))
        small_g["conv_w"][l] = dcw[:CW].reshape(CW, NDEV, BW // NDEV).transpose(1, 0, 2)
        small_g["conv_b"][l] = dcb
        ab = [att_bwd(S["proj3"], mb["do"].reshape(bsz, seq, BW), mb["lse_tot"].reshape(bsz, seq, 128),
                      mb["delta"].reshape(bsz, seq, 128), gi, dil) for gi, (_, dil) in enumerate(PATTERNS)]
        du, d_a, d_bb, d_cb, d_d = s5_bwd(S["proj3"], mb["dys"].reshape(bsz, seq, BW), *S["tabs"], S["dskip"])
        lr, li, ld, btr, bti = S["sp"]
        dlr, dli, dld, dbt, dct = s5_params_bwd(lr, li, ld, btr, bti, d_a, d_bb, d_cb)
        small_g["ssm_lambda_re"][l], small_g["ssm_lambda_im"][l] = dlr.reshape(NG, NS), dli.reshape(NG, NS)
        small_g["ssm_log_dt"][l] = dld[0, :NG]
        small_g["ssm_b_re"][l], small_g["ssm_b_im"][l] = _bt_inv(dbt[0]), _bt_inv(dbt[1])
        small_g["ssm_c_re"][l], small_g["ssm_c_im"][l] = _ct_inv(dct[0]), _ct_inv(dct[1])
        small_g["ssm_d"][l] = d_d
        dproj = assemble_dproj(du, [a[0] for a in ab], [a[1] for a in ab], [a[2] for a in ab],
                               dcv, mb["dgate"].reshape(bsz, seq, 3 * D)).reshape(n, INC)
        nblk, wblk = S["w_in4"].shape[1], S["w_in4"].shape[3]
        pushed(on_grads(l, "in", dict(w_in=mm_tn(S["h1"], dproj, tokblk(D), colblk(wblk), nblk, D, wblk, n, "dw_in"))))
        if l == 0:
            pushed(on_grads(l, "small", dict(small_g=small_g, loss8=loss8, dfinal=dfinal)))
        dx, dg1 = inproj_bwd(dproj, S["w_in4"], 0, S["x"], after_pushes(P["norm1_g"][l][None]), dx1)
        small_g["norm1_g"][l] = dg1
    return loss8, dx, dfinal, small_g


def kernel(x, norm1_g, w_in, b_gate, ssm_lambda_re, ssm_lambda_im, ssm_log_dt, ssm_b_re, ssm_b_im, ssm_c_re, ssm_c_im, ssm_d, w_ssm_glu, w_att_up, conv_w, conv_b, conv_ln_g, conv_ln_b, w_conv_pw2, w_out, norm2_g, w_ffn_in, w_ffn_out, final_g, loss_target, m_norm1_g, m_w_in, m_b_gate, m_ssm_lambda_re, m_ssm_lambda_im, m_ssm_log_dt, m_ssm_b_re, m_ssm_b_im, m_ssm_c_re, m_ssm_c_im, m_ssm_d, m_w_ssm_glu, m_w_att_up, m_conv_w, m_conv_b, m_conv_ln_g, m_conv_ln_b, m_w_conv_pw2, m_w_out, m_norm2_g, m_w_ffn_in, m_w_ffn_out, m_final_g, v_norm1_g, v_w_in, v_b_gate, v_ssm_lambda_re, v_ssm_lambda_im, v_ssm_log_dt, v_ssm_b_re, v_ssm_b_im, v_ssm_c_re, v_ssm_c_im, v_ssm_d, v_w_ssm_glu, v_w_att_up, v_conv_w, v_conv_b, v_conv_ln_g, v_conv_ln_b, v_w_conv_pw2, v_w_out, v_norm2_g, v_w_ffn_in, v_w_ffn_out, v_final_g):
    args = dict(locals())
    W = {k: args[k] for k in ORDER}
    M = {k: args["m_" + k] for k in ORDER}
    V = {k: args["v_" + k] for k in ORDER}
    bsz, seq, _ = x.shape
    n = bsz * seq
    me = 4 * lax.axis_index("x") + 2 * lax.axis_index("y") + lax.axis_index("c")

    groups = {"in": ["w_in"], "mix": ["w_ssm_glu", "w_att_up", "w_conv_pw2", "w_out"], "ffn": ["w_ffn_in", "w_ffn_out"]}
    wb = {k: W[k].astype(bf16) for k in BIG}

    def landing(shard, paired=False):
        if paired:
            k_, n_ = shard.shape
            return lax.dynamic_update_slice(lax.empty((NDEV // 2, k_, 2 * n_), shard.dtype), shard[None],
                                            (me // 2, 0, (me % 2) * n_))
        return lax.dynamic_update_index_in_dim(lax.empty((NDEV,) + shard.shape, shard.dtype), shard, me, 0)

    def own_part(by_rank):
        if by_rank.shape[0] == NDEV:
            return lax.dynamic_index_in_dim(by_rank, me, 0, keepdims=False)
        n_ = by_rank.shape[2] // 2
        return lax.dynamic_slice(by_rank, (me // 2, 0, (me % 2) * n_), (1, by_rank.shape[1], n_))[0]

    plan = [("gather_a", [("w_in", 0), ("conv_w", None)]),
            ("gather_b", [(k, 0) for k in groups["mix"] + groups["ffn"]]),
            ("gather_c", [(k, 1) for k in BIG])]
    pending, token = {}, None
    for name, items in plan:
        shards = [conv_w if l is None else wb[k][l] for k, l in items]
        lands = [landing(s, k == "w_in") for (k, _), s in zip(items, shards)]
        send, recv, s_thru, l_thru, token = push_start(shards, lands, False, name, token)
        pending[name] = (send, recv, s_thru, l_thru, items)
    gathered = {}

    names = [k for k in SMALL if k != "final_g"]
    shapes = [(DEPTH, NDEV, CW, BW // NDEV) if k == "conv_w" else W[k].shape for k in names] + [(D,), (1,)]

    def wpack(src):
        parts = [jnp.broadcast_to(src[k][:, None], shapes[i]) if k == "conv_w" else src[k] for i, k in enumerate(names)]
        return _pack(parts + [src["final_g"], jnp.ones((1,), f32)])

    packs = [wpack(W), wpack(M), wpack(V)]

    def weights(l, group, after):
        name = "gather_c" if l == 1 else ("gather_a" if group == "in" else "gather_b")
        if name in pending:
            send, recv, s_thru, l_thru, items = pending.pop(name)
            if name == "gather_a":
                after = (list(after) if isinstance(after, (list, tuple)) else [after]) + packs
            for item, arr in zip(items, push_wait(send, recv, s_thru, l_thru, after, False, name + "_wait")[1]):
                gathered[item] = arr
        res = {k: gathered[(k, l)] for k in groups[group]}
        if group == "in":
            res["conv_w"] = gathered[("conv_w", None)]
        return res

    big_g = {k: [None] * DEPTH for k in BIG}
    flights = []

    def start_exchange(items, name):
        parts = [big_g[k][l] for k, l in items]
        part = lambda p: p.shape[1:] if p.shape[0] == NDEV else (p.shape[1], p.shape[2] // 2)
        lands = [lax.empty((NDEV - 1,) + part(p), p.dtype) for p in parts]
        send, recv, s_thru, l_thru, tok = push_start(parts, lands, True, name, None)
        flights.append((send, recv, s_thru, l_thru, items, name))
        return tok

    small_flight = []

    def start_small(small_g, loss8, dfinal):
        sg_ = dict(small_g, norm1_g=[jnp.zeros((1, D), f32), small_g["norm1_g"][1]])
        gpack = _pack([jnp.stack([g.reshape(shapes[i][1:]) for g in sg_[k]]) for i, k in enumerate(names)]
                      + [dfinal, loss8[0, :1]])
        send, recv, s_thru, l_thru, tok = push_start([gpack], [landing(gpack)], False, "gather_small", None)
        small_flight.append((send, recv, s_thru, l_thru))
        return tok

    def on_grads(l, group, grads):
        if group == "small":
            return start_small(**grads)
        for k, g in grads.items():
            big_g[k][l] = g
        if l == 1 and group == "in":
            return start_exchange([(k, 1) for k in BIG], "exchange_l1")
        if l == 0:
            return start_exchange([(k, 0) for k in groups[group]], "exchange_l0_" + group)
        return None

    loss8, dx, dfinal, small_g = local_step(x, loss_target, W, weights, on_grads, token)

    landed, own = {}, {}
    for send, recv, s_thru, l_thru, items, name in flights:
        srcs, lands = push_wait(send, recv, s_thru, l_thru, dx, True, name + "_wait")
        for item, src, land in zip(items, srcs, lands):
            landed[item] = land
            own[item] = own_part(src)
    out = {}
    for k in BIG:
        items = [(k, l) for l in range(DEPTH)]
        out[k] = adam_big([landed[i] for i in items], [own[i] for i in items], W[k], M[k], V[k], "adam_" + k)

    send, recv, s_thru, l_thru = small_flight[0]
    gall = push_wait(send, recv, s_thru, l_thru, out[BIG[-1]][0], False, "gather_small_wait")[1][0]
    (late,) = allgather([small_g["norm1_g"][0].reshape(1, D // 128, 128)], "allgather_late")
    gall = lax.dynamic_update_slice(gall, late[0], (0, 0, 0))
    sg, sd, sm, sv = [_unpack(p, shapes) for p in adam_small(gall, *packs)]
    for i, k in enumerate(names + ["final_g"]):
        vals = [t[i] for t in (sg, sd, sm, sv)]
        if k == "conv_w":
            vals = [lax.dynamic_index_in_dim(t, me, axis=1, keepdims=False) for t in vals]
        out[k] = vals
    loss = sg[-1].reshape(())

    res = [loss, dx.reshape(bsz, seq, D)]
    for j in range(4):
        res += [out[k][j] for k in ORDER]
    return tuple(res)
```

```python
import functools
import math

import jax
import jax.numpy as jnp
from jax import lax
from jax.experimental import pallas as pl
from jax.experimental.pallas import tpu as pltpu

f32 = jnp.float32
bf16 = jnp.bfloat16

D = 1024
DEPTH = 2
EPS = 1e-6
BW = 512
NG = 32
GH = 16
NS = 64
NSTATE = NG * NS
HD = 64
NH = 8
PATTERNS = ((128, 1), (512, 4), (2048, 16))
ABLK = 128
ATT_SCALE = HD ** -0.5
CW = 31
DFF = 2816
INC = 7168
NDEV = 8
NSH_IN = INC // NDEV
NSH_FF = 2 * DFF // NDEV
ADAM_LR, ADAM_B1, ADAM_B2, ADAM_EPS, ADAM_WD, ADAM_STEP = 0.001, 0.9, 0.999, 1e-08, 0.01, 10

TB = 512
SJ = 4
SW = NSTATE // SJ
SU = BW // SJ
NEG = -1e30
MESH = pl.DeviceIdType.MESH
ANY = pl.BlockSpec(memory_space=pl.ANY)


def _cp(n_axes, vmem_mb=48):
    return pltpu.CompilerParams(dimension_semantics=("arbitrary",) * n_axes,
                                vmem_limit_bytes=vmem_mb * 1024 * 1024)


def _dot(a, b):
    return jnp.dot(a, b, preferred_element_type=f32)


def _dot_nt(a, b):
    return lax.dot_general(a, b, (((1,), (1,)), ((), ())), preferred_element_type=f32)


def _dot_tn(a, b):
    return lax.dot_general(a, b, (((0,), (0,)), ((), ())), preferred_element_type=f32)


def _dot_sel(a, sel):
    hi = a.astype(bf16)
    lo = (a - hi.astype(f32)).astype(bf16)
    sb = sel.astype(bf16)
    return _dot(hi, sb) + _dot(lo, sb)


def _dot_hi(a, b):
    return jnp.dot(a, b, precision=lax.Precision.HIGHEST, preferred_element_type=f32)


def _sigmoid(x):
    return 1.0 / (1.0 + jnp.exp(-x))


_GC = math.sqrt(2.0 / math.pi)


def _gelu(x):
    return 0.5 * x * (1.0 + jnp.tanh(_GC * (x + 0.044715 * x * x * x)))


def _gelu_grad(x):
    t = jnp.tanh(_GC * (x + 0.044715 * x * x * x))
    return 0.5 * (1.0 + t) + 0.5 * x * (1.0 - t * t) * _GC * (1.0 + 3.0 * 0.044715 * x * x)


def _rms_stats(x):
    return lax.rsqrt(jnp.mean(x * x, axis=-1, keepdims=True) + EPS)


def _rms_bwd(x, g, dh):
    r = _rms_stats(x)
    dyg = dh * g
    dx = r * dyg - x * (r * r * r) * jnp.mean(dyg * x, axis=-1, keepdims=True)
    dg = jnp.sum(dh * x * r, axis=0, keepdims=True)
    return dx, dg


def rms_fwd(x, g):
    n = x.shape[0]

    def body(x_ref, g_ref, h_ref):
        xv = x_ref[...]
        h_ref[...] = (xv * _rms_stats(xv) * g_ref[...]).astype(bf16)

    return pl.pallas_call(
        body, name="rms_fwd", grid=(n // TB,),
        in_specs=[pl.BlockSpec((TB, D), lambda i: (i, 0)), pl.BlockSpec((1, D), lambda i: (0, 0))],
        out_specs=pl.BlockSpec((TB, D), lambda i: (i, 0)),
        out_shape=jax.ShapeDtypeStruct((n, D), bf16), compiler_params=_cp(1))(x, g)


def inproj(h, w4, layer):
    n = h.shape[0]
    tm = 1024
    nblk, wblk = w4.shape[1], w4.shape[3]

    def body(h_ref, w_ref, o_ref):
        o_ref[...] = _dot(h_ref[...], w_ref[0, 0]).astype(bf16)

    return pl.pallas_call(
        body, name="inproj", grid=(nblk, n // tm),
        in_specs=[pl.BlockSpec((tm, D), lambda s, i: (i, 0)),
                  pl.BlockSpec((1, 1, D, wblk), lambda s, i: (layer, s, 0, 0))],
        out_specs=pl.BlockSpec((tm, wblk), lambda s, i: (i, s)),
        out_shape=jax.ShapeDtypeStruct((n, INC), bf16), compiler_params=_cp(2))(h, w4)


def inproj_bwd(dproj, w4, layer, x, g, dres):
    n = x.shape[0]
    tm = 1024
    nblk, wblk = w4.shape[1], w4.shape[3]

    def body(dp_ref, w_ref, x_ref, g_ref, dr_ref, dx_ref, dg_ref, acc):
        i, s = pl.program_id(0), pl.program_id(1)

        @pl.when(s == 0)
        def _():
            acc[...] = jnp.zeros_like(acc)

        @pl.when((s == 0) & (i == 0))
        def _():
            dg_ref[...] = jnp.zeros_like(dg_ref)

        acc[...] += _dot_nt(dp_ref[...], w_ref[0, 0])

        @pl.when(s == nblk - 1)
        def _():
            dx, dg = _rms_bwd(x_ref[...], g_ref[...], acc[...])
            dx_ref[...] = dr_ref[...] + dx
            dg_ref[...] += dg

    return pl.pallas_call(
        body, name="inproj_bwd", grid=(n // tm, nblk),
        in_specs=[pl.BlockSpec((tm, wblk), lambda i, s: (i, s)),
                  pl.BlockSpec((1, 1, D, wblk), lambda i, s: (layer, s, 0, 0)),
                  pl.BlockSpec((tm, D), lambda i, s: (i, 0)),
                  pl.BlockSpec((1, D), lambda i, s: (0, 0)),
                  pl.BlockSpec((tm, D), lambda i, s: (i, 0))],
        out_specs=[pl.BlockSpec((tm, D), lambda i, s: (i, 0)), pl.BlockSpec((1, D), lambda i, s: (0, 0))],
        out_shape=[jax.ShapeDtypeStruct((n, D), f32), jax.ShapeDtypeStruct((1, D), f32)],
        scratch_shapes=[pltpu.VMEM((tm, D), f32)], compiler_params=_cp(2))(dproj, w4, x, g, dres)


def mm_tn(a, b, a_spec, b_spec, n_sh, ka, nb, m, name):
    tm = 1024

    def body(a_ref, b_ref, o_ref, acc):
        i = pl.program_id(1)

        @pl.when(i == 0)
        def _():
            acc[...] = jnp.zeros_like(acc)

        av = a_ref[...].reshape(tm, ka).astype(bf16)
        bv = b_ref[...].reshape(tm, nb).astype(bf16)
        acc[...] += _dot_tn(av, bv)

        @pl.when(i == m // tm - 1)
        def _():
            o_ref[0] = acc[...].astype(bf16)

    return pl.pallas_call(
        body, name=name, grid=(n_sh, m // tm), in_specs=[a_spec, b_spec],
        out_specs=pl.BlockSpec((1, ka, nb), lambda s, i: (s, 0, 0)),
        out_shape=jax.ShapeDtypeStruct((n_sh, ka, nb), bf16),
        scratch_shapes=[pltpu.VMEM((ka, nb), f32)], compiler_params=_cp(2))(a, b)


def dw_mix(ysin, dz, ob, dya, hs, dyc, merged, dx1):
    n = ysin.shape[0]
    tm = 512
    pairs = ((BW, 2 * D), (BW, D), (BW, D), (D, D))

    def body(a0, b0, a1, b1, a2, b2, a3, b3, o0, o1, o2, o3, c0, c1, c2, c3):
        i = pl.program_id(0)
        accs = (c0, c1, c2, c3)

        @pl.when(i == 0)
        def _():
            for c in accs:
                c[...] = jnp.zeros_like(c)

        for a, b_, c in zip((a0, a1, a2, a3), (b0, b1, b2, b3), accs):
            c[...] += _dot_tn(a[...], b_[...].astype(bf16))

        @pl.when(i == n // tm - 1)
        def _():
            for s in range(NDEV):
                o0[s] = c0[:, s * 256:(s + 1) * 256].astype(bf16)
                o1[s] = c1[:, s * 128:(s + 1) * 128].astype(bf16)
                o2[s] = c2[:, s * 128:(s + 1) * 128].astype(bf16)
                o3[s] = c3[s * 128:(s + 1) * 128, :].astype(bf16)

    tok = lambda w: pl.BlockSpec((tm, w), lambda i: (i, 0))
    whole = lambda shape: pl.BlockSpec(shape, lambda i: (0, 0, 0))
    outs = [(NDEV, BW, 256), (NDEV, BW, 128), (NDEV, BW, 128), (NDEV, D // NDEV, D)]
    return pl.pallas_call(
        body, name="dw_mix", grid=(n // tm,),
        in_specs=[tok(w) for pair in pairs for w in pair],
        out_specs=[whole(s) for s in outs], out_shape=[jax.ShapeDtypeStruct(s, bf16) for s in outs],
        scratch_shapes=[pltpu.VMEM(p, f32) for p in pairs],
        compiler_params=_cp(1, 56))(ysin, dz, ob, dya, hs, dyc, merged, dx1)


def ffn_fwd(x1, g2, w1, w2, layer):
    n = x1.shape[0]
    w2p = w2.reshape(w2.shape[0], 4, NSH_FF, D)

    def body(x_ref, g_ref, wa_ref, wb_ref, w2_ref, o_ref, z1_ref, z2_ref, h_sc):
        s = pl.program_id(1)

        @pl.when(s == 0)
        def _():
            xv = x_ref[...]
            h_sc[...] = (xv * _rms_stats(xv) * g_ref[...]).astype(bf16)
            o_ref[...] = xv

        h = h_sc[...]
        z1 = _dot(h, wa_ref[0, 0])
        z2 = _dot(h, wb_ref[0, 0])
        z1_ref[0] = z1.astype(bf16)
        z2_ref[0] = z2.astype(bf16)
        a = (z1 * _sigmoid(z1) * z2).astype(bf16)
        o_ref[...] += _dot(a, w2_ref[0, 0])

    tb = 2 * TB
    sh3 = pl.BlockSpec((1, tb, NSH_FF), lambda i, s: (s, i, 0))
    return pl.pallas_call(
        body, name="ffn_fwd", grid=(n // tb, 4),
        in_specs=[pl.BlockSpec((tb, D), lambda i, s: (i, 0)),
                  pl.BlockSpec((1, D), lambda i, s: (0, 0)),
                  pl.BlockSpec((1, 1, D, NSH_FF), lambda i, s: (layer, s, 0, 0)),
                  pl.BlockSpec((1, 1, D, NSH_FF), lambda i, s: (layer, s + 4, 0, 0)),
                  pl.BlockSpec((1, 1, NSH_FF, D), lambda i, s: (layer, s, 0, 0))],
        out_specs=[pl.BlockSpec((tb, D), lambda i, s: (i, 0)), sh3, sh3, pl.BlockSpec((tb, D), lambda i, s: (i, 0))],
        out_shape=[jax.ShapeDtypeStruct((n, D), f32), jax.ShapeDtypeStruct((4, n, NSH_FF), bf16),
                   jax.ShapeDtypeStruct((4, n, NSH_FF), bf16), jax.ShapeDtypeStruct((n, D), bf16)],
        compiler_params=_cp(2))(x1, g2, w1, w1, w2p)


def ffn_bwd(h, dx2, z1s, z2s, w1, w2, layer):
    n = h.shape[0]
    w2p = w2.reshape(w2.shape[0], 4, NSH_FF, D)
    nblk = n // TB

    def body(h_ref, dy_ref, z1_ref, z2_ref, wa_ref, wb_ref, w2_ref, dh_ref, dwa_ref, dwb_ref, dw2_ref,
             acc_a, acc_b, acc_2):
        i = pl.program_id(1)

        @pl.when(i == 0)
        def _():
            acc_a[...] = jnp.zeros_like(acc_a)
            acc_b[...] = jnp.zeros_like(acc_b)
            acc_2[...] = jnp.zeros_like(acc_2)

        hv = h_ref[...]
        dyb = dy_ref[...].astype(bf16)
        z1 = z1_ref[0].astype(f32)
        z2 = z2_ref[0].astype(f32)
        sg = _sigmoid(z1)
        sl = z1 * sg
        a = (sl * z2).astype(bf16)
        da = _dot_nt(dyb, w2_ref[0, 0])
        dz2 = (da * sl).astype(bf16)
        dz1 = (da * z2 * sg * (1.0 + z1 * (1.0 - sg))).astype(bf16)
        dh_ref[0] = (_dot_nt(dz1, wa_ref[0, 0]) + _dot_nt(dz2, wb_ref[0, 0])).astype(bf16)
        acc_a[...] += _dot_tn(hv, dz1)
        acc_b[...] += _dot_tn(hv, dz2)
        acc_2[...] += _dot_tn(a, dyb)

        @pl.when(i == nblk - 1)
        def _():
            dwa_ref[0] = acc_a[...].astype(bf16)
            dwb_ref[0] = acc_b[...].astype(bf16)
            dw2_ref[0] = acc_2[...].astype(bf16)

    tok = pl.BlockSpec((TB, D), lambda s, i: (i, 0))
    sh3 = pl.BlockSpec((1, TB, NSH_FF), lambda s, i: (s, i, 0))
    return pl.pallas_call(
        body, name="ffn_bwd", grid=(4, nblk),
        in_specs=[tok, tok, sh3, sh3,
                  pl.BlockSpec((1, 1, D, NSH_FF), lambda s, i: (layer, s, 0, 0)),
                  pl.BlockSpec((1, 1, D, NSH_FF), lambda s, i: (layer, s + 4, 0, 0)),
                  pl.BlockSpec((1, 1, NSH_FF, D), lambda s, i: (layer, s, 0, 0))],
        out_specs=[pl.BlockSpec((1, TB, D), lambda s, i: (s, i, 0)),
                   pl.BlockSpec((1, D, NSH_FF), lambda s, i: (s, 0, 0)),
                   pl.BlockSpec((1, D, NSH_FF), lambda s, i: (s, 0, 0)),
                   pl.BlockSpec((1, NSH_FF, D), lambda s, i: (s, 0, 0))],
        out_shape=[jax.ShapeDtypeStruct((4, n, D), bf16), jax.ShapeDtypeStruct((4, D, NSH_FF), bf16),
                   jax.ShapeDtypeStruct((4, D, NSH_FF), bf16), jax.ShapeDtypeStruct((4, NSH_FF, D), bf16)],
        scratch_shapes=[pltpu.VMEM((D, NSH_FF), f32), pltpu.VMEM((D, NSH_FF), f32), pltpu.VMEM((NSH_FF, D), f32)],
        compiler_params=_cp(2, 56))(h, dx2, z1s, z2s, w1, w1, w2p)


def norm_bwd_fin(x, g, dres, dh_parts, name):
    n = x.shape[0]
    nparts = dh_parts.shape[0]

    def body(x_ref, g_ref, dy_ref, dh_ref, dx_ref, dg_ref):
        i = pl.program_id(0)

        @pl.when(i == 0)
        def _():
            dg_ref[...] = jnp.zeros_like(dg_ref)

        dh = dh_ref[0].astype(f32)
        for s in range(1, nparts):
            dh = dh + dh_ref[s].astype(f32)
        dx, dg = _rms_bwd(x_ref[...], g_ref[...], dh)
        dx_ref[...] = dy_ref[...] + dx
        dg_ref[...] += dg

    tok = pl.BlockSpec((TB, D), lambda i: (i, 0))
    return pl.pallas_call(
        body, name=name, grid=(n // TB,),
        in_specs=[tok, pl.BlockSpec((1, D), lambda i: (0, 0)), tok, pl.BlockSpec((nparts, TB, D), lambda i: (0, i, 0))],
        out_specs=[tok, pl.BlockSpec((1, D), lambda i: (0, 0))],
        out_shape=[jax.ShapeDtypeStruct((n, D), f32), jax.ShapeDtypeStruct((1, D), f32)],
        compiler_params=_cp(1))(x, g, dres, dh_parts)


def loss_head(x, g, target):
    n = x.shape[0]

    def body(x_ref, g_ref, t_ref, l_ref, dx_ref, dg_ref):
        i = pl.program_id(0)

        @pl.when(i == 0)
        def _():
            l_ref[...] = jnp.zeros_like(l_ref)
            dg_ref[...] = jnp.zeros_like(dg_ref)

        xv = x_ref[...]
        y = xv * _rms_stats(xv) * g_ref[...]
        e = y - t_ref[...]
        l_ref[...] += 0.5 * jnp.sum(jnp.sum(e * e, axis=-1, keepdims=True), axis=0, keepdims=True) * (1.0 / D)
        dx, dg = _rms_bwd(xv, g_ref[...], e * (1.0 / D))
        dx_ref[...] = dx
        dg_ref[...] += dg

    tok = lambda i: (i, 0)
    return pl.pallas_call(
        body, name="loss_head", grid=(n // TB,),
        in_specs=[pl.BlockSpec((TB, D), tok), pl.BlockSpec((1, D), lambda i: (0, 0)), pl.BlockSpec((TB, D), tok)],
        out_specs=[pl.BlockSpec((8, 128), lambda i: (0, 0)), pl.BlockSpec((TB, D), tok),
                   pl.BlockSpec((1, D), lambda i: (0, 0))],
        out_shape=[jax.ShapeDtypeStruct((8, 128), f32), jax.ShapeDtypeStruct((n, D), f32),
                   jax.ShapeDtypeStruct((1, D), f32)],
        compiler_params=_cp(1))(x, g, target)


def _disc(lr, li, ld):
    dt = jnp.exp(ld)
    mag = jnp.exp(lr * dt)
    ar = mag * jnp.cos(li * dt)
    ai = mag * jnp.sin(li * dt)
    nr, ni = ar - 1.0, ai
    den = lr * lr + li * li
    zr = (nr * lr + ni * li) / den
    zi = (ni * lr - nr * li) / den
    return ar, ai, zr, zi


def _blockdiag_mask(shape):
    r = lax.broadcasted_iota(jnp.int32, shape, 0) // GH
    c = lax.broadcasted_iota(jnp.int32, shape, 1) // NS
    return r == c


def s5_params(lr, li, ld, btr, bti, ctr, cti):
    def body(lr_ref, li_ref, ld_ref, btr_ref, bti_ref, ctr_ref, cti_ref, t8_ref, bb_ref, cb_ref):
        ar, ai, zr, zi = _disc(lr_ref[...], li_ref[...], ld_ref[...])
        pr_, pi_ = ar, ai
        pw2 = []
        for k in range(4):
            pw2.append((pr_, pi_))
            pr_, pi_ = pr_ * pr_ - pi_ * pi_, 2.0 * pr_ * pi_
        cm = lambda p, q: (p[0] * q[0] - p[1] * q[1], p[0] * q[1] + p[1] * q[0])
        pw = {1: pw2[0], 2: pw2[1], 4: pw2[2], 8: pw2[3]}
        pw[3], pw[5], pw[6] = cm(pw[2], pw[1]), cm(pw[4], pw[1]), cm(pw[4], pw[2])
        pw[7] = cm(pw[4], pw[3])
        row = lax.broadcasted_iota(jnp.int32, (8, NSTATE), 0)
        zero = jnp.zeros((8, NSTATE), f32)
        for c in range(2):
            for k in range(3):
                full = jnp.broadcast_to(pw2[k][c], (8, NSTATE))
                t8_ref[c, k] = jnp.where(row >= (1 << k), full, 0.0)
                t8_ref[c, 3 + k] = jnp.where(row + (1 << k) < 8, full, 0.0)
            up, down = zero, zero
            for j in range(8):
                up = up + jnp.where(row == j, pw[j + 1][c], 0.0)
                down = down + jnp.where(row == j, pw[8 - j][c], 0.0)
            t8_ref[c, 6] = up
            t8_ref[c, 7] = down
        bbr = zr * btr_ref[...] - zi * bti_ref[...]
        bbi = zr * bti_ref[...] + zi * btr_ref[...]
        mask = _blockdiag_mask((SU, SW))
        for j in range(SJ):
            cols = slice(j * SW, (j + 1) * SW)
            for c, (vb, vc) in enumerate(((bbr, ctr_ref[...]), (bbi, cti_ref[...]))):
                bb_ref[c, j] = jnp.where(mask, jnp.tile(vb[:, cols], (SU // GH, 1)), 0.0).astype(bf16)
                cb_ref[c, j] = jnp.where(mask, jnp.tile(vc[:, cols], (SU // GH, 1)), 0.0).astype(bf16)

    return pl.pallas_call(
        body, name="s5_params",
        out_shape=[jax.ShapeDtypeStruct((2, 8, 8, NSTATE), f32),
                   jax.ShapeDtypeStruct((2, SJ, SU, SW), bf16), jax.ShapeDtypeStruct((2, SJ, SU, SW), bf16)],
        compiler_params=pltpu.CompilerParams(vmem_limit_bytes=56 * 1024 * 1024))(lr, li, ld, btr, bti, ctr, cti)


def s5_params_bwd(lr, li, ld, btr, bti, d_a, d_bb, d_cb):
    def body(lr_ref, li_ref, ld_ref, btr_ref, bti_ref, da_ref, dbb_ref, dcb_ref,
             dlr_ref, dli_ref, dld_ref, dbt_ref, dct_ref):
        mask = _blockdiag_mask((SU, SW))

        def fold(ref, c):
            parts = []
            for j in range(SJ):
                v = jnp.where(mask, ref[c, j], 0.0)
                parts.append(v.reshape(SU // GH, GH, SW).sum(axis=0))
            return jnp.concatenate(parts, axis=1)

        dct_ref[0] = fold(dcb_ref, 0)
        dct_ref[1] = fold(dcb_ref, 1)
        dbbr, dbbi = fold(dbb_ref, 0), fold(dbb_ref, 1)
        lrv, liv, ldv = lr_ref[...], li_ref[...], ld_ref[...]
        (ar, ai, zr, zi), vjp = jax.vjp(_disc, lrv, liv, ldv)
        btr, bti = btr_ref[...], bti_ref[...]
        dbt_ref[0] = zr * dbbr + zi * dbbi
        dbt_ref[1] = zr * dbbi - zi * dbbr
        dzr = jnp.sum(dbbr * btr + dbbi * bti, axis=0, keepdims=True)
        dzi = jnp.sum(dbbi * btr - dbbr * bti, axis=0, keepdims=True)
        dlr, dli, dld = vjp((da_ref[0:1, :], da_ref[1:2, :], dzr, dzi))
        dlr_ref[...] = dlr
        dli_ref[...] = dli
        ind = (lax.broadcasted_iota(jnp.int32, (NSTATE, 128), 0) // NS
               == lax.broadcasted_iota(jnp.int32, (NSTATE, 128), 1)).astype(f32)
        dld_ref[...] = _dot_hi(jnp.broadcast_to(dld, (8, NSTATE)), ind)

    return pl.pallas_call(
        body, name="s5_params_bwd",
        out_shape=[jax.ShapeDtypeStruct((1, NSTATE), f32), jax.ShapeDtypeStruct((1, NSTATE), f32),
                   jax.ShapeDtypeStruct((8, 128), f32), jax.ShapeDtypeStruct((2, GH, NSTATE), f32),
                   jax.ShapeDtypeStruct((2, GH, NSTATE), f32)],
        compiler_params=pltpu.CompilerParams(vmem_limit_bytes=56 * 1024 * 1024))(lr, li, ld, btr, bti, d_a, d_bb, d_cb)


def _fma(sr, si, ar, ai, qr, qi):
    return sr + ar * qr - ai * qi, si + ar * qi + ai * qr


def _scan_tile(sr, si, cr, ci, t8_ref, reverse):
    sg = -1.0 if reverse else 1.0
    for k in range(3):
        tk = 3 + k if reverse else k
        rot = 8 - (1 << k) if reverse else 1 << k
        sr, si = _fma(sr, si, t8_ref[0, tk], sg * t8_ref[1, tk], pltpu.roll(sr, rot, 0), pltpu.roll(si, rot, 0))
    tp = 7 if reverse else 6
    sr, si = _fma(sr, si, t8_ref[0, tp], sg * t8_ref[1, tp], cr, ci)
    e = 0 if reverse else 7
    return sr, si, jnp.broadcast_to(sr[e:e + 1, :], sr.shape), jnp.broadcast_to(si[e:e + 1, :], si.shape)


S5MC = 512


def _s5_input_map(u_ref, bb_ref, sr_sc, si_sc, l):
    for c in range(l // S5MC):
        rows = slice(c * S5MC, (c + 1) * S5MC)
        u = u_ref[0, rows, :]
        sr_sc[rows, :] = _dot(u, bb_ref[0, 0])
        si_sc[rows, :] = _dot(u, bb_ref[1, 0])


S5TU = 4


def _s5_forward_scan(sr_sc, si_sc, t8_ref, l):
    def step(k, carry):
        cr, ci = carry
        for j in range(S5TU):
            rows = pl.ds(pl.multiple_of(k * (8 * S5TU), 8 * S5TU) + 8 * j, 8)
            sr, si, cr, ci = _scan_tile(sr_sc[rows, :], si_sc[rows, :], cr, ci, t8_ref, False)
            sr_sc[rows, :] = sr
            si_sc[rows, :] = si
        return cr, ci

    zero = jnp.zeros((8, SW), f32)
    lax.fori_loop(0, l // (8 * S5TU), step, (zero, zero))


def s5_fwd(proj3, t8, bb, cb, dskip):
    b, l, _ = proj3.shape

    def body(u_ref, t8_ref, bb_ref, cb_ref, d_ref, y_ref, sr_out, si_out):
        sr_sc, si_sc = sr_out.at[0], si_out.at[0]
        _s5_input_map(u_ref, bb_ref, sr_sc, si_sc, l)
        _s5_forward_scan(sr_sc, si_sc, t8_ref, l)
        for c in range(l // S5MC):
            rows = slice(c * S5MC, (c + 1) * S5MC)
            y = (_dot_nt(sr_sc[rows, :].astype(bf16), cb_ref[0, 0])
                 - _dot_nt(si_sc[rows, :].astype(bf16), cb_ref[1, 0]))
            y_ref[0, rows, :] = y + d_ref[...] * u_ref[0, rows, :].astype(f32)

    return pl.pallas_call(
        body, name="s5_fwd", grid=(SJ, b),
        in_specs=[pl.BlockSpec((1, l, SU), lambda j, bi: (bi, 0, j)),
                  pl.BlockSpec((2, 8, 8, SW), lambda j, bi: (0, 0, 0, j)),
                  pl.BlockSpec((2, 1, SU, SW), lambda j, bi: (0, j, 0, 0)),
                  pl.BlockSpec((2, 1, SU, SW), lambda j, bi: (0, j, 0, 0)),
                  pl.BlockSpec((1, SU), lambda j, bi: (0, j))],
        out_specs=[pl.BlockSpec((1, l, SU), lambda j, bi: (bi, 0, j)),
                   pl.BlockSpec((1, l, SW), lambda j, bi: (bi, 0, j)),
                   pl.BlockSpec((1, l, SW), lambda j, bi: (bi, 0, j))],
        out_shape=[jax.ShapeDtypeStruct((b, l, BW), f32), jax.ShapeDtypeStruct((b, l, NSTATE), f32),
                   jax.ShapeDtypeStruct((b, l, NSTATE), f32)],
        compiler_params=_cp(2))(proj3, t8, bb, cb, dskip)


def s5_bwd(proj3, dy, s_re, s_im, t8, bb, cb, dskip):
    b, l, _ = proj3.shape
    nt = l // 8

    def body(u_ref, dy_ref, sr_in, si_in, t8_ref, bb_ref, cb_ref, d_ref,
             du_ref, da_ref, dbb_ref, dcb_ref, dd_ref, gr_sc, gi_sc):
        bi = pl.program_id(1)
        sr_sc, si_sc = sr_in.at[0], si_in.at[0]

        @pl.when(bi == 0)
        def _():
            da_ref[...] = jnp.zeros_like(da_ref)
            dbb_ref[...] = jnp.zeros_like(dbb_ref)
            dcb_ref[...] = jnp.zeros_like(dcb_ref)
            dd_ref[...] = jnp.zeros_like(dd_ref)

        for c in range(l // S5MC):
            rows = slice(c * S5MC, (c + 1) * S5MC)
            dyb = dy_ref[0, rows, :].astype(bf16)
            gr_sc[rows, :] = _dot(dyb, cb_ref[0, 0])
            gi_sc[rows, :] = -_dot(dyb, cb_ref[1, 0])

        row = lax.broadcasted_iota(jnp.int32, (8, SW), 0)

        def step(i, carry):
            cr, ci, dar, dai = carry
            for j in range(S5TU):
                k = nt - 1 - (i * S5TU + j)
                rows = pl.ds(pl.multiple_of(k * 8, 8), 8)
                gr, gi, cr, ci = _scan_tile(gr_sc[rows, :], gi_sc[rows, :], cr, ci, t8_ref, True)
                gr_sc[rows, :] = gr
                gi_sc[rows, :] = gi
                before = pl.ds(pl.multiple_of(jnp.maximum(k - 1, 0) * 8, 8), 8)
                live = jnp.where(k > 0, 1.0, 0.0)
                sr, si = sr_sc[rows, :], si_sc[rows, :]
                spr = jnp.where(row == 0, live * sr_sc[before, :][7:8, :], pltpu.roll(sr, 1, 0))
                spi = jnp.where(row == 0, live * si_sc[before, :][7:8, :], pltpu.roll(si, 1, 0))
                dar, dai = dar + spr * gr + spi * gi, dai + spr * gi - spi * gr
            return cr, ci, dar, dai

        zero = jnp.zeros((8, SW), f32)
        _, _, dar, dai = lax.fori_loop(0, nt // S5TU, step, (zero, zero, zero, zero))
        da_ref[0:1, :] += jnp.sum(dar, axis=0, keepdims=True)
        da_ref[1:2, :] += jnp.sum(dai, axis=0, keepdims=True)

        for c in range(l // S5MC):
            rows = slice(c * S5MC, (c + 1) * S5MC)
            u = u_ref[0, rows, :]
            dyv = dy_ref[0, rows, :]
            dyb = dyv.astype(bf16)
            grb, gib = gr_sc[rows, :].astype(bf16), gi_sc[rows, :].astype(bf16)
            dcb_ref[0, 0] += _dot_tn(dyb, sr_sc[rows, :].astype(bf16))
            dcb_ref[1, 0] -= _dot_tn(dyb, si_sc[rows, :].astype(bf16))
            dbb_ref[0, 0] += _dot_tn(u, grb)
            dbb_ref[1, 0] += _dot_tn(u, gib)
            du = _dot_nt(grb, bb_ref[0, 0]) + _dot_nt(gib, bb_ref[1, 0]) + d_ref[...] * dyv
            du_ref[0, rows, :] = du.astype(bf16)
            dd_ref[...] += jnp.sum(dyv * u.astype(f32), axis=0, keepdims=True)

    seq = pl.BlockSpec((1, l, SU), lambda j, bi: (bi, 0, j))
    sts = pl.BlockSpec((1, l, SW), lambda j, bi: (bi, 0, j))
    tab = pl.BlockSpec((2, 1, SU, SW), lambda j, bi: (0, j, 0, 0))
    return pl.pallas_call(
        body, name="s5_bwd", grid=(SJ, b),
        in_specs=[seq, seq, sts, sts, pl.BlockSpec((2, 8, 8, SW), lambda j, bi: (0, 0, 0, j)), tab, tab,
                  pl.BlockSpec((1, SU), lambda j, bi: (0, j))],
        out_specs=[seq, pl.BlockSpec((2, SW), lambda j, bi: (0, j)), tab, tab,
                   pl.BlockSpec((1, SU), lambda j, bi: (0, j))],
        out_shape=[jax.ShapeDtypeStruct((b, l, BW), bf16), jax.ShapeDtypeStruct((2, NSTATE), f32),
                   jax.ShapeDtypeStruct((2, SJ, SU, SW), f32), jax.ShapeDtypeStruct((2, SJ, SU, SW), f32),
                   jax.ShapeDtypeStruct((1, BW), f32)],
        scratch_shapes=[pltpu.VMEM((l, SW), f32)] * 2,
        compiler_params=_cp(2))(proj3, dy, s_re, s_im, t8, bb, cb, dskip)


AHC = 2
AHW = AHC * 128


def _att_mask(n, nb):
    if nb == 1:
        qi = lax.broadcasted_iota(jnp.int32, (ABLK, ABLK), 0)
        kj = lax.broadcasted_iota(jnp.int32, (ABLK, ABLK), 1)
        return kj <= qi
    qi = lax.broadcasted_iota(jnp.int32, (ABLK, 2 * ABLK), 0)
    kj = lax.broadcasted_iota(jnp.int32, (ABLK, 2 * ABLK), 1)
    return (kj >= qi) & (kj <= qi + ABLK) & ((n > 0) | (kj >= ABLK))


def _att_rows(it, nb, dil):
    r, n = it // nb, it % nb
    cur = pl.ds(r + n * (ABLK * dil), ABLK, stride=dil)
    prv = pl.ds(r + jnp.maximum(n - 1, 0) * (ABLK * dil), ABLK, stride=dil)
    return n, cur, prv


def _att_keys(ref, c, cur, prv, nb):
    if nb == 1:
        x = ref[c, cur, :].astype(bf16)
    else:
        x = jnp.concatenate([ref[c, prv, :], ref[c, cur, :]], axis=0).astype(bf16)
    head0 = lax.broadcasted_iota(jnp.int32, x.shape, 1) < HD
    zero = jnp.zeros_like(x)
    return jnp.concatenate([jnp.where(head0, x, zero), jnp.where(head0, zero, x)], axis=0)


def _per_head(nk, a0, a1):
    col = lax.broadcasted_iota(jnp.int32, (ABLK, 2 * nk), 1)
    return jnp.where(col < nk, a0, a1)


def _to_chunks(src_ref, dst):
    for c in range(AHC):
        dst[c] = src_ref[0, :, c * 128:(c + 1) * 128].astype(f32)


def att_fwd(proj3, g_idx, dil):
    b, l, _ = proj3.shape
    nb = l // dil // ABLK
    nhalf = BW // AHW

    def body(q_ref, k_ref, v_ref, o_ref, lse_ref, qf, kf, vf, of):
        hh = pl.program_id(1)
        _to_chunks(q_ref, qf)
        _to_chunks(k_ref, kf)
        _to_chunks(v_ref, vf)
        lane = lax.broadcasted_iota(jnp.int32, (ABLK, 128), 1)

        def step(it, carry):
            n, cur, prv = _att_rows(it, nb, dil)
            valid = _att_mask(n, nb)
            nk = valid.shape[1]
            lse_all = jnp.zeros((ABLK, 128), f32)
            for c in range(AHC):
                q = (qf[c, cur, :] * ATT_SCALE).astype(bf16)
                k = _att_keys(kf, c, cur, prv, nb)
                v = _att_keys(vf, c, cur, prv, nb)
                head = hh * (2 * AHC) + 2 * c
                if nb == 1:
                    v2 = jnp.concatenate([valid, valid], axis=1)
                    s = jnp.where(v2, _dot_nt(q, k), NEG)
                    m0 = jnp.max(s[:, :nk], axis=-1, keepdims=True)
                    m1 = jnp.max(s[:, nk:], axis=-1, keepdims=True)
                    p = jnp.exp(s - _per_head(nk, m0, m1))
                    den0 = jnp.sum(p[:, :nk], axis=-1, keepdims=True)
                    den1 = jnp.sum(p[:, nk:], axis=-1, keepdims=True)
                    of[c, cur, :] = _dot(p.astype(bf16), v) * jnp.where(lane < HD, 1.0 / den0, 1.0 / den1)
                    lse_all = (lse_all + jnp.where(lane == head, m0 + jnp.log(den0), 0.0)
                               + jnp.where(lane == head + 1, m1 + jnp.log(den1), 0.0))
                    continue
                acc = None
                for hl in range(2):
                    s = jnp.where(valid, _dot_nt(q, k[hl * nk:(hl + 1) * nk]), NEG)
                    m = jnp.max(s, axis=-1, keepdims=True)
                    p = jnp.exp(s - m)
                    den = jnp.sum(p, axis=-1, keepdims=True)
                    o_h = _dot(p.astype(bf16), v[hl * nk:(hl + 1) * nk]) * (1.0 / den)
                    acc = o_h if acc is None else acc + o_h
                    lse_all = lse_all + jnp.where(lane == head + hl, m + jnp.log(den), 0.0)
                of[c, cur, :] = acc

            lse_ref[0, 0, cur, :] = lse_all
            return carry

        lax.fori_loop(0, dil * nb, step, 0, unroll=4)
        for c in range(AHC):
            o_ref[0, :, c * 128:(c + 1) * 128] = of[c].astype(bf16)

    col = lambda c: pl.BlockSpec((1, l, AHW), lambda bi, hh: (bi, 0, c * nhalf + hh))
    return pl.pallas_call(
        body, name=f"att_fwd{g_idx}", grid=(b, nhalf),
        in_specs=[col(1 + g_idx), col(4), col(5)],
        out_specs=[pl.BlockSpec((1, l, AHW), lambda bi, hh: (bi, 0, hh)),
                   pl.BlockSpec((1, 1, l, 128), lambda bi, hh: (bi, hh, 0, 0))],
        out_shape=[jax.ShapeDtypeStruct((b, l, BW), bf16), jax.ShapeDtypeStruct((b, nhalf, l, 128), f32)],
        scratch_shapes=[pltpu.VMEM((AHC, l, 128), f32)] * 4,
        compiler_params=_cp(2))(proj3, proj3, proj3)


def att_bwd(proj3, do, lse_tot, delta, g_idx, dil):
    b, l, _ = proj3.shape
    nb = l // dil // ABLK
    nhalf = BW // AHW

    def body(q_ref, k_ref, v_ref, do_ref, l_ref, dl_ref, dq_out, dk_out, dv_out, qf, kf, vf, dof,
             dq_ref, dk_ref, dv_ref):
        hh = pl.program_id(1)
        _to_chunks(q_ref, qf)
        _to_chunks(k_ref, kf)
        _to_chunks(v_ref, vf)
        _to_chunks(do_ref, dof)
        dk_ref[...] = jnp.zeros_like(dk_ref)
        dv_ref[...] = jnp.zeros_like(dv_ref)
        lane = lax.broadcasted_iota(jnp.int32, (ABLK, 128), 1)

        def step(it, carry):
            n, cur, prv = _att_rows(it, nb, dil)
            valid = _att_mask(n, nb)
            valid = jnp.concatenate([valid, valid], axis=1)
            nk = valid.shape[1] // 2
            lse_b = l_ref[0, cur, :]
            dl_b = dl_ref[0, cur, :]
            head0 = lax.broadcasted_iota(jnp.int32, (nk, 128), 1) < HD
            for c in range(AHC):
                q = (qf[c, cur, :] * ATT_SCALE).astype(bf16)
                dob = dof[c, cur, :].astype(bf16)
                k = _att_keys(kf, c, cur, prv, nb)
                v = _att_keys(vf, c, cur, prv, nb)
                head = hh * (2 * AHC) + 2 * c
                pick = lambda a, h: jnp.sum(jnp.where(lane == h, a, 0.0), axis=-1, keepdims=True)
                lse_h = _per_head(nk, pick(lse_b, head), pick(lse_b, head + 1))
                dl_h = _per_head(nk, pick(dl_b, head), pick(dl_b, head + 1))
                s = _dot_nt(q, k)
                p = jnp.where(valid, jnp.exp(jnp.minimum(s - lse_h, 60.0)), 0.0)
                ds = (p * (_dot_nt(dob, v) - dl_h)).astype(bf16)
                dq_ref[0, c, cur, :] = _dot(ds, k) * ATT_SCALE
                dk2 = _dot_tn(ds, q)
                dv2 = _dot_tn(p.astype(bf16), dob)
                dk = jnp.where(head0, dk2[:nk], dk2[nk:])
                dv = jnp.where(head0, dv2[:nk], dv2[nk:])
                if nb == 1:
                    dk_ref[0, c, cur, :] += dk
                    dv_ref[0, c, cur, :] += dv
                else:
                    dk_ref[0, c, cur, :] += dk[ABLK:]
                    dv_ref[0, c, cur, :] += dv[ABLK:]
                    dk_ref[0, c, prv, :] += dk[:ABLK]
                    dv_ref[0, c, prv, :] += dv[:ABLK]

            return carry

        lax.fori_loop(0, dil * nb, step, 0, unroll=4)
        dq_out[0] = dq_ref[0].astype(bf16)
        dk_out[0] = dk_ref[0].astype(bf16)
        dv_out[0] = dv_ref[0].astype(bf16)

    col = lambda c: pl.BlockSpec((1, l, AHW), lambda bi, hh: (bi, 0, c * nhalf + hh))
    own = pl.BlockSpec((1, l, AHW), lambda bi, hh: (bi, 0, hh))
    own128 = pl.BlockSpec((1, l, 128), lambda bi, hh: (bi, 0, 0))
    chunked = pl.BlockSpec((1, AHC, l, 128), lambda bi, hh: (bi, hh, 0, 0))
    return pl.pallas_call(
        body, name=f"att_bwd{g_idx}", grid=(b, nhalf),
        in_specs=[col(1 + g_idx), col(4), col(5), own, own128, own128],
        out_specs=[chunked] * 3,
        out_shape=[jax.ShapeDtypeStruct((b, BW // 128, l, 128), bf16)] * 3,
        scratch_shapes=[pltpu.VMEM((AHC, l, 128), f32)] * 4 + [pltpu.VMEM((1, AHC, l, 128), f32)] * 3,
        compiler_params=_cp(2, 56))(proj3, proj3, proj3, do, lse_tot, delta)


CPAD = 32
CTAIL = 16
CR = 128
CSLAB = CR + 40


def _tap_windows(slab, off, mis):
    ntap = (CW - 1 - mis) // 8 + 1
    rot = (off + mis) % 8
    base = off + mis - rot
    shifted = pltpu.roll(slab, CSLAB - rot, 0) if rot else slab
    for a in range(ntap):
        yield 8 * a + mis, shifted[base + 8 * a:base + 8 * a + CR]


def _fill_glu(cv_ref, pad, l):
    pad[0:CPAD, :] = jnp.zeros((CPAD, BW), f32)
    pad[CPAD:CPAD + l, :] = cv_ref[0, :, :BW].astype(f32) * _sigmoid(cv_ref[0, :, BW:].astype(f32))
    pad[CPAD + l:, :] = jnp.zeros((CTAIL, BW), f32)


def conv_fwd(proj3, cw, cb):
    b, l, _ = proj3.shape

    def body(cv_ref, w_ref, b_ref, o_ref, pad):
        _fill_glu(cv_ref, pad, l)
        for lc in range(BW // 128):
            lanes = slice(lc * 128, (lc + 1) * 128)
            wv = w_ref[:, lanes]

            def step(c, carry):
                base = pl.multiple_of(c * CR, CR)
                slab = pad[pl.ds(base, CSLAB), lanes]
                acc = jnp.zeros((CR, 128), f32) + b_ref[:, lanes]
                for mis in range(8):
                    for k, win in _tap_windows(slab, CPAD - (CW - 1), mis):
                        acc = acc + wv[k:k + 1] * win
                o_ref[0, pl.ds(base, CR), lanes] = acc
                return carry

            lax.fori_loop(0, l // CR, step, 0)

    return pl.pallas_call(
        body, name="conv_fwd", grid=(b,),
        in_specs=[pl.BlockSpec((1, l, 2 * BW), lambda i: (i, 0, 3)),
                  pl.BlockSpec((32, BW), lambda i: (0, 0)), pl.BlockSpec((1, BW), lambda i: (0, 0))],
        out_specs=pl.BlockSpec((1, l, BW), lambda i: (i, 0, 0)),
        out_shape=jax.ShapeDtypeStruct((b, l, BW), f32),
        scratch_shapes=[pltpu.VMEM((CPAD + l + CTAIL, BW), f32)], compiler_params=_cp(1))(proj3, cw, cb)


def conv_bwd(proj3, dhc, cw):
    b, l, _ = proj3.shape

    def body(cv_ref, d_ref, w_ref, dcv_ref, dw_ref, db_ref, pad, dpad):
        i = pl.program_id(0)

        @pl.when(i == 0)
        def _():
            dw_ref[...] = jnp.zeros_like(dw_ref)
            db_ref[...] = jnp.zeros_like(db_ref)

        _fill_glu(cv_ref, pad, l)
        dpad[0:l, :] = d_ref[0]
        dpad[l:, :] = jnp.zeros((CPAD + CTAIL, BW), f32)
        db_ref[...] += jnp.sum(d_ref[0], axis=0, keepdims=True)
        for lc in range(BW // 128):
            lanes = slice(lc * 128, (lc + 1) * 128)
            glanes = slice(BW + lc * 128, BW + (lc + 1) * 128)
            wv = w_ref[:, lanes]

            for mis in range(8):
                ntap = (CW - 1 - mis) // 8 + 1

                def dw_step(c, accs, mis=mis, lanes=lanes):
                    base = pl.multiple_of(c * CR, CR)
                    slab = pad[pl.ds(base, CSLAB), lanes]
                    dv = dpad[pl.ds(base, CR), lanes]
                    return tuple(acc + (dv * win).reshape(CR // 8, 8, 128).sum(axis=0) for acc, (_, win)
                                 in zip(accs, _tap_windows(slab, CPAD - (CW - 1), mis)))

                accs = lax.fori_loop(0, l // CR, dw_step, tuple(jnp.zeros((8, 128), f32) for _ in range(ntap)))
                for a in range(ntap):
                    k = 8 * a + mis
                    dw_ref[k:k + 1, lanes] += jnp.sum(accs[a], axis=0, keepdims=True)

            def dh_step(c, carry, lanes=lanes, glanes=glanes, wv=wv):
                base = pl.multiple_of(c * CR, CR)
                slab = dpad[pl.ds(base, CSLAB), lanes]
                acc = jnp.zeros((CR, 128), f32)
                for mis in range(8):
                    for kk, win in _tap_windows(slab, 0, mis):
                        acc = acc + wv[CW - 1 - kk:CW - kk] * win
                rows = pl.ds(base, CR)
                a = cv_ref[0, rows, lanes].astype(f32)
                sg = _sigmoid(cv_ref[0, rows, glanes].astype(f32))
                dcv_ref[0, rows, lanes] = (acc * sg).astype(bf16)
                dcv_ref[0, rows, glanes] = (acc * a * sg * (1.0 - sg)).astype(bf16)
                return carry

            lax.fori_loop(0, l // CR, dh_step, 0)

    return pl.pallas_call(
        body, name="conv_bwd", grid=(b,),
        in_specs=[pl.BlockSpec((1, l, 2 * BW), lambda i: (i, 0, 3)),
                  pl.BlockSpec((1, l, BW), lambda i: (i, 0, 0)),
                  pl.BlockSpec((32, BW), lambda i: (0, 0))],
        out_specs=[pl.BlockSpec((1, l, 2 * BW), lambda i: (i, 0, 0)),
                   pl.BlockSpec((32, BW), lambda i: (0, 0)), pl.BlockSpec((1, BW), lambda i: (0, 0))],
        out_shape=[jax.ShapeDtypeStruct((b, l, 2 * BW), bf16), jax.ShapeDtypeStruct((32, BW), f32),
                   jax.ShapeDtypeStruct((1, BW), f32)],
        scratch_shapes=[pltpu.VMEM((CPAD + l + CTAIL, BW), f32), pltpu.VMEM((l + CPAD + CTAIL, BW), f32)],
        compiler_params=_cp(1))(proj3, dhc, cw)


def _head_expand():
    r = lax.broadcasted_iota(jnp.int32, (128, BW), 0)
    c = lax.broadcasted_iota(jnp.int32, (128, BW), 1) // HD
    return (r == c).astype(f32)


def _head_reduce():
    r = lax.broadcasted_iota(jnp.int32, (BW, 128), 0) // HD
    c = lax.broadcasted_iota(jnp.int32, (BW, 128), 1)
    return (r == c).astype(f32)


def _merge_common(ys_ref, o_refs, l_refs, hc_ref, g_refs, bg_ref, lng_ref, lnb_ref, wglu_ref, watt_ref, wpw_ref):
    r = {}
    ysv = ys_ref[...]
    r["ys"] = ysv
    r["ysin"] = _gelu(ysv).astype(bf16)
    z = _dot(r["ysin"], wglu_ref[...])
    r["z1"], r["sg2"] = z[:, :D], _sigmoid(z[:, D:])
    r["y_s"] = r["z1"] * r["sg2"]
    ls = [lr_[0, 0] + lr_[0, 1] for lr_ in l_refs]
    mx = jnp.maximum(jnp.maximum(ls[0], ls[1]), ls[2])
    es = [jnp.exp(v - mx) for v in ls]
    tot = es[0] + es[1] + es[2]
    r["lse_tot"] = mx + jnp.log(tot)
    e_mat = _head_expand()
    o = jnp.zeros(ysv.shape, f32)
    for e, o_ref in zip(es, o_refs):
        o = o + _dot_sel(e / tot, e_mat) * o_ref[...].astype(f32)
    r["o"] = o
    r["ob"] = o.astype(bf16)
    r["y_a"] = _dot(r["ob"], watt_ref[...])
    hc = hc_ref[...]
    mu = jnp.mean(hc, axis=-1, keepdims=True)
    xc = hc - mu
    rstd = lax.rsqrt(jnp.mean(xc * xc, axis=-1, keepdims=True) + EPS)
    r["xh"], r["rstd"] = xc * rstd, rstd
    hn = r["xh"] * lng_ref[...] + lnb_ref[...]
    r["hn"] = hn
    r["sgn"] = _sigmoid(hn)
    r["hs"] = (hn * r["sgn"]).astype(bf16)
    r["y_c"] = _dot(r["hs"], wpw_ref[...])
    r["gates"] = [_sigmoid(g_refs[k][...].astype(f32) + bg_ref[:, k * D:(k + 1) * D]) for k in range(3)]
    r["merged"] = r["gates"][0] * r["y_s"] + r["gates"][1] * r["y_a"] + r["gates"][2] * r["y_c"]
    return r


TBM = 256


def _merge_in_specs(tok, tb, lses):
    w = lambda shape: pl.BlockSpec(shape, lambda i: (0, 0), pipeline_mode=pl.Buffered(1))
    nbl = lses[0].shape[2] // tb
    return ([pl.BlockSpec((tb, D), tok), pl.BlockSpec((tb, BW), tok)]
            + [pl.BlockSpec((tb, BW), tok)] * 3
            + [pl.BlockSpec((1, 2, tb, 128), lambda i: (i // nbl, 0, i % nbl, 0))] * 3
            + [pl.BlockSpec((tb, BW), tok)]
            + [pl.BlockSpec((tb, D), lambda i, k=k: (i, 4 + k)) for k in range(3)]
            + [w((1, 3 * D)), w((1, BW)), w((1, BW)), w((BW, 2 * D)), w((BW, D)), w((BW, D)), w((D, D))])


def merge_fwd(x, ys, os_, lses, hc, proj, bg, lng, lnb, wglu, watt, wpw, wout):
    n = x.shape[0]

    def body(x_ref, ys_ref, o1, o2, o3, l1, l2, l3, hc_ref, g0, g1, g2, bg_ref, lng_ref, lnb_ref,
             wglu_ref, watt_ref, wpw_ref, wout_ref, x1_ref):
        r = _merge_common(ys_ref, (o1, o2, o3), (l1, l2, l3), hc_ref, (g0, g1, g2), bg_ref, lng_ref, lnb_ref,
                          wglu_ref, watt_ref, wpw_ref)
        x1_ref[...] = x_ref[...] + _dot(r["merged"].astype(bf16), wout_ref[...])

    tok = lambda i: (i, 0)
    return pl.pallas_call(
        body, name="merge_fwd", grid=(n // TB,), in_specs=_merge_in_specs(tok, TB, lses),
        out_specs=pl.BlockSpec((TB, D), tok), out_shape=jax.ShapeDtypeStruct((n, D), f32),
        compiler_params=_cp(1, 56))(x, ys, *os_, *lses, hc, proj, proj, proj, bg, lng, lnb, wglu, watt, wpw, wout)


def merge_bwd(dx1, ys, os_, lses, hc, proj, bg, lng, lnb, wglu, watt, wpw, wout):
    n = dx1.shape[0]

    def body(dx_ref, ys_ref, o1, o2, o3, l1, l2, l3, hc_ref, g0, g1, g2, bg_ref, lng_ref, lnb_ref,
             wglu_ref, watt_ref, wpw_ref, wout_ref,
             dys_ref, do_ref, delta_ref, ltot_ref, dhc_ref, dgate_ref, ysin_ref, dz_ref, ob_ref, dya_ref,
             hs_ref, dyc_ref, mg_ref, dbg_ref, dlng_ref, dlnb_ref):
        i = pl.program_id(0)

        @pl.when(i == 0)
        def _():
            dbg_ref[...] = jnp.zeros_like(dbg_ref)
            dlng_ref[...] = jnp.zeros_like(dlng_ref)
            dlnb_ref[...] = jnp.zeros_like(dlnb_ref)

        r = _merge_common(ys_ref, (o1, o2, o3), (l1, l2, l3), hc_ref, (g0, g1, g2), bg_ref, lng_ref, lnb_ref,
                          wglu_ref, watt_ref, wpw_ref)
        mg_ref[...] = r["merged"].astype(bf16)
        ysin_ref[...] = r["ysin"]
        ob_ref[...] = r["ob"]
        hs_ref[...] = r["hs"]
        ltot_ref[...] = r["lse_tot"]
        dm = _dot_nt(dx_ref[...].astype(bf16), wout_ref[...])
        ys3 = (r["y_s"], r["y_a"], r["y_c"])
        for k in range(3):
            gk = r["gates"][k]
            dgr = dm * ys3[k] * gk * (1.0 - gk)
            dgate_ref[:, k * D:(k + 1) * D] = dgr.astype(bf16)
            dbg_ref[:, k * D:(k + 1) * D] += jnp.sum(dgr, axis=0, keepdims=True)
        dy_s = dm * r["gates"][0]
        sg2 = r["sg2"]
        dz = jnp.concatenate([dy_s * sg2, dy_s * r["z1"] * sg2 * (1.0 - sg2)], axis=1).astype(bf16)
        dz_ref[...] = dz
        dys_ref[...] = _dot_nt(dz, wglu_ref[...]) * _gelu_grad(r["ys"])
        dya = (dm * r["gates"][1]).astype(bf16)
        dya_ref[...] = dya
        do = _dot_nt(dya, watt_ref[...])
        do_ref[...] = do.astype(bf16)
        delta_ref[...] = _dot_sel(do * r["o"], _head_reduce())
        dyc = (dm * r["gates"][2]).astype(bf16)
        dyc_ref[...] = dyc
        sgn, hn = r["sgn"], r["hn"]
        dhn = _dot_nt(dyc, wpw_ref[...]) * sgn * (1.0 + hn * (1.0 - sgn))
        dlng_ref[...] += jnp.sum(dhn * r["xh"], axis=0, keepdims=True)
        dlnb_ref[...] += jnp.sum(dhn, axis=0, keepdims=True)
        dxh = dhn * lng_ref[...]
        xh = r["xh"]
        dhc_ref[...] = r["rstd"] * (dxh - jnp.mean(dxh, axis=-1, keepdims=True)
                                    - xh * jnp.mean(dxh * xh, axis=-1, keepdims=True))

    tok = lambda i: (i, 0)
    fix = lambda i: (0, 0)
    outs = [("dys", BW, f32), ("do", BW, bf16), ("delta", 128, f32), ("lse_tot", 128, f32), ("dhc", BW, f32),
            ("dgate", 3 * D, bf16), ("ysin", BW, bf16), ("dz", 2 * D, bf16), ("ob", BW, bf16), ("dya", D, bf16),
            ("hs", BW, bf16), ("dyc", D, bf16), ("merged", D, bf16)]
    small = [("dbg", 3 * D), ("dlng", BW), ("dlnb", BW)]
    res = pl.pallas_call(
        body, name="merge_bwd", grid=(n // TBM,), in_specs=_merge_in_specs(tok, TBM, lses),
        out_specs=[pl.BlockSpec((TBM, w), tok) for _, w, _ in outs] + [pl.BlockSpec((1, w), fix) for _, w in small],
        out_shape=[jax.ShapeDtypeStruct((n, w), dt) for _, w, dt in outs]
        + [jax.ShapeDtypeStruct((1, w), f32) for _, w in small],
        compiler_params=_cp(1, 56))(dx1, ys, *os_, *lses, hc, proj, proj, proj, bg, lng, lnb, wglu, watt, wpw, wout)
    return dict(zip([k for k, _, _ in outs] + [k for k, _ in small], res))


def assemble_dproj(du, dqs, dks, dvs, dcv, dgate):
    b, l, _ = du.shape
    nck = BW // 128

    def body(du_ref, q1, q2, q3, k1, k2, k3, v1, v2, v3, cv_ref, g_ref, o_ref):
        o_ref[0, :, 0:BW] = du_ref[0]
        for c in range(nck):
            for j, qr in enumerate((q1, q2, q3)):
                o_ref[0, :, (1 + j) * BW + c * 128:(1 + j) * BW + (c + 1) * 128] = qr[0, c]
            add3 = lambda r1, r2, r3: (r1[0, c].astype(f32) + r2[0, c].astype(f32) + r3[0, c].astype(f32)).astype(bf16)
            o_ref[0, :, 4 * BW + c * 128:4 * BW + (c + 1) * 128] = add3(k1, k2, k3)
            o_ref[0, :, 5 * BW + c * 128:5 * BW + (c + 1) * 128] = add3(v1, v2, v3)
        o_ref[0, :, 6 * BW:8 * BW] = cv_ref[0]
        o_ref[0, :, 8 * BW:] = g_ref[0]

    t = lambda w: pl.BlockSpec((1, TB, w), lambda bi, i: (bi, i, 0))
    ck = pl.BlockSpec((1, nck, TB, 128), lambda bi, i: (bi, 0, i, 0))
    return pl.pallas_call(
        body, name="assemble_dproj", grid=(b, l // TB),
        in_specs=[t(BW)] + [ck] * 9 + [t(2 * BW), t(3 * D)], out_specs=t(INC),
        out_shape=jax.ShapeDtypeStruct((b, l, INC), bf16), compiler_params=_cp(2))(du, *dqs, *dks, *dvs, dcv, dgate)


def _me():
    return lax.axis_index("x"), lax.axis_index("y"), lax.axis_index("c")


def _peers():
    x, y, c = _me()
    return [(x, y, 1 - c), (1 - x, y, c), (1 - x, y, 1 - c), (x, 1 - y, c), (x, 1 - y, 1 - c),
            (1 - x, 1 - y, c), (1 - x, 1 - y, 1 - c)]


def _rank(p):
    return 4 * p[0] + 2 * p[1] + p[2]


def allgather(arrs, name):
    na = len(arrs)
    units = [(a, j) for a in range(na) for j in range(arrs[a].shape[0])]
    nu = len(units)

    def body(*refs):
        ins, outs = refs[:na], refs[na:2 * na]
        send, recv, loc = refs[2 * na:]
        me = _rank(_me())
        local, remote = [], []
        for u, (a, j) in enumerate(units):
            own = pltpu.make_async_copy(ins[a].at[j], outs[a].at[j, me], loc.at[u])
            own.start()
            local.append(own)
        for u, (a, j) in enumerate(units):
            for k, p in enumerate(_peers()):
                cp = pltpu.make_async_remote_copy(src_ref=ins[a].at[j], dst_ref=outs[a].at[j, me],
                                                  send_sem=send.at[u, k], recv_sem=recv.at[u, k],
                                                  device_id=p, device_id_type=MESH)
                cp.start()
                remote.append(cp)
        for cp in local:
            cp.wait()
        for cp in remote:
            cp.wait()

    return pl.pallas_call(
        body, name=name, in_specs=[ANY] * na, out_specs=[ANY] * na,
        out_shape=[jax.ShapeDtypeStruct((a.shape[0], NDEV) + a.shape[1:], a.dtype) for a in arrs],
        scratch_shapes=[pltpu.SemaphoreType.DMA((nu, NDEV - 1)), pltpu.SemaphoreType.DMA((nu, NDEV - 1)),
                        pltpu.SemaphoreType.DMA((nu,))])(*arrs)


HBM = pl.BlockSpec(memory_space=pltpu.HBM)
SEM = pl.BlockSpec(memory_space=pltpu.SEMAPHORE)
_EFFECT = pltpu.SideEffectType.DATAFLOW_SIDE_EFFECTING


def _rank_slot(ref, r):
    if ref.shape[0] == NDEV:
        return ref.at[r]
    n = ref.shape[2] // 2
    return ref.at[r // 2, :, pl.ds(pl.multiple_of((r % 2) * n, 128), n)]


def _push_copies(srcs, lands, send, recv, scatter):
    me = _rank(_me())
    out = []
    for i in range(len(srcs)):
        for k, p in enumerate(_peers()):
            src = _rank_slot(srcs[i], _rank(p)) if scatter else srcs[i]
            dst = lands[i].at[k] if scatter else _rank_slot(lands[i], me)
            j = i * (NDEV - 1) + k
            out.append(pltpu.make_async_remote_copy(src_ref=src, dst_ref=dst, send_sem=send.at[j],
                                                    recv_sem=recv.at[j], device_id=p, device_id_type=MESH))
    return out


def push_start(srcs, lands, scatter, name, token):
    n = len(srcs)
    token = jnp.zeros((8, 128), f32) if token is None else token

    def body(*refs):
        for cp in _push_copies(refs[:n], refs[n:2 * n], refs[2 * n + 1], refs[2 * n + 2], scatter):
            cp.start()
        refs[-1][...] = refs[2 * n][...]

    sems = pltpu.SemaphoreType.DMA((n * (NDEV - 1),))
    vmem = pl.BlockSpec(memory_space=pltpu.VMEM)
    res = pl.pallas_call(
        body, name=name, in_specs=[HBM] * (2 * n) + [vmem], out_specs=[SEM, SEM] + [HBM] * (2 * n) + [vmem],
        out_shape=[sems, sems] + [pltpu.HBM(a.shape, a.dtype) for a in list(srcs) + list(lands)]
        + [jax.ShapeDtypeStruct((8, 128), f32)],
        input_output_aliases={i: 2 + i for i in range(2 * n)},
        compiler_params=pltpu.CompilerParams(has_side_effects=_EFFECT),
    )(*[pltpu.with_memory_space_constraint(a, pltpu.HBM) for a in list(srcs) + list(lands)], token)
    return res[0], res[1], res[2:2 + n], res[2 + n:2 + 2 * n], res[-1]


def push_wait(send, recv, srcs, lands, after, scatter, name):
    n = len(srcs)
    after = list(after) if isinstance(after, (list, tuple)) else [after]

    def body(*refs):
        for cp in _push_copies(refs[:n], refs[n:2 * n], refs[2 * n], refs[2 * n + 1], scatter):
            cp.wait_send()
            cp.wait_recv()

    res = pl.pallas_call(
        body, name=name, in_specs=[HBM] * (2 * n) + [SEM, SEM] + [ANY] * len(after), out_specs=[HBM] * (2 * n),
        out_shape=[pltpu.HBM(a.shape, a.dtype) for a in list(srcs) + list(lands)],
        input_output_aliases={i: i for i in range(2 * n)},
        compiler_params=pltpu.CompilerParams(has_side_effects=_EFFECT),
    )(*srcs, *lands, send, recv, *after)
    return res[:n], res[n:]


_C1 = 1.0 / (1.0 - ADAM_B1 ** ADAM_STEP)
_C2 = 1.0 / (1.0 - ADAM_B2 ** ADAM_STEP)


def _adamw(w, g, m, v):
    m = ADAM_B1 * m + (1.0 - ADAM_B1) * g
    v = ADAM_B2 * v + (1.0 - ADAM_B2) * (g * g)
    delta = -ADAM_LR * ((m * _C1) / (jnp.sqrt(v * _C2) + ADAM_EPS) + ADAM_WD * w)
    return delta, m, v


def adam_big(lands, owns, w, m, v, name):
    _, k, n = lands[0].shape
    tk = k
    while tk * n * 2 * NDEV > 2 * 1024 * 1024 and tk % 16 == 0:
        tk //= 2

    def body(*refs):
        l_refs, o_refs = refs[:DEPTH], refs[DEPTH:2 * DEPTH]
        w_ref, m_ref, v_ref, g_ref, d_ref, nm_ref, nv_ref = refs[2 * DEPTH:]
        for l in range(DEPTH):
            g = o_refs[l][...].astype(f32)
            for s in range(NDEV - 1):
                g = g + l_refs[l][s].astype(f32)
            d, nm, nv = _adamw(w_ref[l], g, m_ref[l], v_ref[l])
            g_ref[l], d_ref[l], nm_ref[l], nv_ref[l] = g, d, nm, nv

    blk = pl.BlockSpec((DEPTH, tk, n), lambda i: (0, i, 0))
    return pl.pallas_call(
        body, name=name, grid=(k // tk,),
        in_specs=[pl.BlockSpec((NDEV - 1, tk, n), lambda i: (0, i, 0))] * DEPTH
        + [pl.BlockSpec((tk, n), lambda i: (i, 0))] * DEPTH + [blk, blk, blk],
        out_specs=[blk] * 4, out_shape=[jax.ShapeDtypeStruct(w.shape, f32)] * 4,
        compiler_params=_cp(1))(*lands, *owns, w, m, v)


def adam_small(gath, w, m, v):
    r = w.shape[0]
    tr = 512

    def body(g_ref, w_ref, m_ref, v_ref, go_ref, d_ref, nm_ref, nv_ref):
        g = g_ref[0]
        for s in range(1, NDEV):
            g = g + g_ref[s]
        d, nm, nv = _adamw(w_ref[...], g, m_ref[...], v_ref[...])
        go_ref[...], d_ref[...], nm_ref[...], nv_ref[...] = g, d, nm, nv

    blk = pl.BlockSpec((tr, 128), lambda i: (i, 0))
    return pl.pallas_call(
        body, name="adam_small", grid=(r // tr,),
        in_specs=[pl.BlockSpec((NDEV, tr, 128), lambda i: (0, i, 0)), blk, blk, blk],
        out_specs=[blk] * 4, out_shape=[jax.ShapeDtypeStruct((r, 128), f32)] * 4,
        compiler_params=_cp(1))(gath, w, m, v)


SMALL = ["norm1_g", "b_gate", "ssm_lambda_re", "ssm_lambda_im", "ssm_log_dt", "ssm_b_re", "ssm_b_im",
         "ssm_c_re", "ssm_c_im", "ssm_d", "conv_w", "conv_b", "conv_ln_g", "conv_ln_b", "norm2_g", "final_g"]
BIG = ["w_in", "w_ssm_glu", "w_att_up", "w_conv_pw2", "w_out", "w_ffn_in", "w_ffn_out"]
ORDER = ["norm1_g", "w_in", "b_gate", "ssm_lambda_re", "ssm_lambda_im", "ssm_log_dt", "ssm_b_re", "ssm_b_im",
         "ssm_c_re", "ssm_c_im", "ssm_d", "w_ssm_glu", "w_att_up", "conv_w", "conv_b", "conv_ln_g", "conv_ln_b",
         "w_conv_pw2", "w_out", "norm2_g", "w_ffn_in", "w_ffn_out", "final_g"]
PACK_ROWS = 2560


def _pack(arrs):
    flat = jnp.concatenate([a.reshape(-1).astype(f32) for a in arrs])
    return jnp.pad(flat, (0, PACK_ROWS * 128 - flat.shape[0])).reshape(PACK_ROWS, 128)


def _unpack(pack, shapes):
    flat = pack.reshape(-1)
    out, off = [], 0
    for s in shapes:
        sz = math.prod(s)
        out.append(flat[off:off + sz].reshape(s))
        off += sz
    return out


def _bt(b):
    return b.transpose(2, 0, 1).reshape(GH, NSTATE)


def _bt_inv(bt):
    return bt.reshape(GH, NG, NS).transpose(1, 2, 0)


def _ct(c):
    return c.transpose(1, 0, 2).reshape(GH, NSTATE)


def _ct_inv(ct):
    return ct.reshape(GH, NG, NS).transpose(1, 0, 2)


def local_step(x, loss_target, P, weights, on_grads, start_token=None):
    bsz, seq, _ = x.shape
    n = bsz * seq

    def natural(g3):
        return g3.transpose(1, 0, 2).reshape(g3.shape[1], NDEV * g3.shape[2])

    tokens = [] if start_token is None else [start_token]

    def after_pushes(a):
        while tokens:
            a = a + tokens.pop()[0:1, 0:1]
        return a

    def pushed(tok):
        if tok is not None:
            tokens.append(tok)

    s5_in, s5_tabs = [], []
    for l in range(DEPTH):
        lr = P["ssm_lambda_re"][l].reshape(1, NSTATE)
        li = P["ssm_lambda_im"][l].reshape(1, NSTATE)
        ld = jnp.repeat(P["ssm_log_dt"][l], NS).reshape(1, NSTATE)
        btr, bti = _bt(P["ssm_b_re"][l]), _bt(P["ssm_b_im"][l])
        s5_in.append((lr, li, ld, btr, bti))
        s5_tabs.append(s5_params(lr, li, ld, btr, bti, _ct(P["ssm_c_re"][l]), _ct(P["ssm_c_im"][l])))

    xs = x.reshape(n, D)
    saved = []
    conv_w_pad = None
    for l in range(DEPTH):
        S = {"x": xs}
        h1 = rms_fwd(xs, after_pushes(P["norm1_g"][l][None]))
        G = dict(weights(l, "in", [h1] + [t for tabs in s5_tabs for t in tabs] if l == 0 else h1))
        if conv_w_pad is None:
            conv_w_full = G["conv_w"].transpose(1, 2, 0, 3).reshape(DEPTH, CW, BW)
            conv_w_pad = jnp.pad(conv_w_full, ((0, 0), (0, 1), (0, 0)))
        w_in4 = G["w_in"][None]
        proj = inproj(h1, w_in4, 0)
        proj3 = proj.reshape(bsz, seq, INC)
        lr, li, ld, btr, bti = s5_in[l]
        t8, bb, cb = s5_tabs[l]
        dskip = P["ssm_d"][l][None]
        ys, s_re, s_im = s5_fwd(proj3, t8, bb, cb, dskip)
        att = [att_fwd(proj3, gi, dil) for gi, (_, dil) in enumerate(PATTERNS)]
        hc = conv_fwd(proj3, conv_w_pad[l], P["conv_b"][l][None])
        G.update(weights(l, "mix", hc))
        wts = dict(wglu=natural(G["w_ssm_glu"]), watt=natural(G["w_att_up"]),
                   wpw=natural(G["w_conv_pw2"]), wout=G["w_out"].reshape(D, D))
        mi = dict(ys=ys.reshape(n, BW), os_=[a[0].reshape(n, BW) for a in att],
                  lses=[a[1] for a in att], hc=hc.reshape(n, BW),
                  proj=proj, bg=P["b_gate"][l][None], lng=P["conv_ln_g"][l][None], lnb=P["conv_ln_b"][l][None],
                  **wts)
        x1 = merge_fwd(xs, **mi)
        G.update(weights(l, "ffn", x1))
        w_ffn = (G["w_ffn_in"][None], G["w_ffn_out"][None])
        x2, z1s, z2s, h2 = ffn_fwd(x1, P["norm2_g"][l][None], *w_ffn, 0)
        S.update(h1=h1, proj=proj, proj3=proj3, tabs=(s_re, s_im, t8, bb, cb), mi=mi, x1=x1, w_in4=w_in4, w_ffn=w_ffn,
                 zs=(z1s, z2s), h2=h2,
                 sp=(lr, li, ld, btr, bti), dskip=dskip)
        saved.append(S)
        xs = x2

    loss8, dx, dfinal = loss_head(xs, P["final_g"][None], loss_target.reshape(n, D))

    small_g = {k: [None] * DEPTH for k in SMALL if k != "final_g"}
    tokblk = lambda w: pl.BlockSpec((1024, w), lambda s, i: (i, 0))
    colblk = lambda w: pl.BlockSpec((1024, w), lambda s, i: (i, s))
    for l in reversed(range(DEPTH)):
        S = saved[l]
        g2 = P["norm2_g"][l][None]
        dh4, dwa, dwb, dw2 = ffn_bwd(S["h2"], dx, *S["zs"], *S["w_ffn"], 0)
        dx1, dg2 = norm_bwd_fin(S["x1"], after_pushes(g2), dx, dh4, "ffn_bwd_fin")
        small_g["norm2_g"][l] = dg2
        pushed(on_grads(l, "ffn", dict(w_ffn_in=jnp.concatenate([dwa, dwb], axis=0),
                                       w_ffn_out=dw2.reshape(NDEV, NSH_FF // 2, D))))
        mb = merge_bwd(dx1, **dict(S["mi"], lng=after_pushes(S["mi"]["lng"])))
        small_g["b_gate"][l], small_g["conv_ln_g"][l], small_g["conv_ln_b"][l] = mb["dbg"], mb["dlng"], mb["dlnb"]
        dws = dw_mix(mb["ysin"], mb["dz"], mb["ob"], mb["dya"], mb["hs"], mb["dyc"], mb["merged"], dx1)
        pushed(on_grads(l, "mix", dict(zip(("w_ssm_glu", "w_att_up", "w_conv_pw2", "w_out"), dws))))
        dcv, dcw, dcb = conv_bwd(S["proj3"], mb["dhc"].reshape(bsz, seq, BW), after_pushes(conv_w_pad[l]))
        small_g["conv_w"][l] = dcw[:CW].reshape(CW, NDEV, BW // NDEV).transpose(1, 0, 2)
        small_g["conv_b"][l] = dcb
        ab = [att_bwd(S["proj3"], mb["do"].reshape(bsz, seq, BW), mb["lse_tot"].reshape(bsz, seq, 128),
                      mb["delta"].reshape(bsz, seq, 128), gi, dil) for gi, (_, dil) in enumerate(PATTERNS)]
        du, d_a, d_bb, d_cb, d_d = s5_bwd(S["proj3"], mb["dys"].reshape(bsz, seq, BW), *S["tabs"], S["dskip"])
        lr, li, ld, btr, bti = S["sp"]
        dlr, dli, dld, dbt, dct = s5_params_bwd(lr, li, ld, btr, bti, d_a, d_bb, d_cb)
        small_g["ssm_lambda_re"][l], small_g["ssm_lambda_im"][l] = dlr.reshape(NG, NS), dli.reshape(NG, NS)
        small_g["ssm_log_dt"][l] = dld[0, :NG]
        small_g["ssm_b_re"][l], small_g["ssm_b_im"][l] = _bt_inv(dbt[0]), _bt_inv(dbt[1])
        small_g["ssm_c_re"][l], small_g["ssm_c_im"][l] = _ct_inv(dct[0]), _ct_inv(dct[1])
        small_g["ssm_d"][l] = d_d
        dproj = assemble_dproj(du, [a[0] for a in ab], [a[1] for a in ab], [a[2] for a in ab],
                               dcv, mb["dgate"].reshape(bsz, seq, 3 * D)).reshape(n, INC)
        nblk, wblk = S["w_in4"].shape[1], S["w_in4"].shape[3]
        pushed(on_grads(l, "in", dict(w_in=mm_tn(S["h1"], dproj, tokblk(D), colblk(wblk), nblk, D, wblk, n, "dw_in"))))
        if l == 0:
            pushed(on_grads(l, "small", dict(small_g=small_g, loss8=loss8, dfinal=dfinal)))
        dx, dg1 = inproj_bwd(dproj, S["w_in4"], 0, S["x"], after_pushes(P["norm1_g"][l][None]), dx1)
        small_g["norm1_g"][l] = dg1
    return loss8, dx, dfinal, small_g


def kernel(x, norm1_g, w_in, b_gate, ssm_lambda_re, ssm_lambda_im, ssm_log_dt, ssm_b_re, ssm_b_im, ssm_c_re, ssm_c_im, ssm_d, w_ssm_glu, w_att_up, conv_w, conv_b, conv_ln_g, conv_ln_b, w_conv_pw2, w_out, norm2_g, w_ffn_in, w_ffn_out, final_g, loss_target, m_norm1_g, m_w_in, m_b_gate, m_ssm_lambda_re, m_ssm_lambda_im, m_ssm_log_dt, m_ssm_b_re, m_ssm_b_im, m_ssm_c_re, m_ssm_c_im, m_ssm_d, m_w_ssm_glu, m_w_att_up, m_conv_w, m_conv_b, m_conv_ln_g, m_conv_ln_b, m_w_conv_pw2, m_w_out, m_norm2_g, m_w_ffn_in, m_w_ffn_out, m_final_g, v_norm1_g, v_w_in, v_b_gate, v_ssm_lambda_re, v_ssm_lambda_im, v_ssm_log_dt, v_ssm_b_re, v_ssm_b_im, v_ssm_c_re, v_ssm_c_im, v_ssm_d, v_w_ssm_glu, v_w_att_up, v_conv_w, v_conv_b, v_conv_ln_g, v_conv_ln_b, v_w_conv_pw2, v_w_out, v_norm2_g, v_w_ffn_in, v_w_ffn_out, v_final_g):
    args = dict(locals())
    W = {k: args[k] for k in ORDER}
    M = {k: args["m_" + k] for k in ORDER}
    V = {k: args["v_" + k] for k in ORDER}
    bsz, seq, _ = x.shape
    n = bsz * seq
    me = 4 * lax.axis_index("x") + 2 * lax.axis_index("y") + lax.axis_index("c")

    groups = {"in": ["w_in"], "mix": ["w_ssm_glu", "w_att_up", "w_conv_pw2", "w_out"], "ffn": ["w_ffn_in", "w_ffn_out"]}
    wb = {k: W[k].astype(bf16) for k in BIG}

    def landing(shard, paired=False):
        if paired:
            k_, n_ = shard.shape
            return lax.dynamic_update_slice(lax.empty((NDEV // 2, k_, 2 * n_), shard.dtype), shard[None],
                                            (me // 2, 0, (me % 2) * n_))
        return lax.dynamic_update_index_in_dim(lax.empty((NDEV,) + shard.shape, shard.dtype), shard, me, 0)

    def own_part(by_rank):
        if by_rank.shape[0] == NDEV:
            return lax.dynamic_index_in_dim(by_rank, me, 0, keepdims=False)
        n_ = by_rank.shape[2] // 2
        return lax.dynamic_slice(by_rank, (me // 2, 0, (me % 2) * n_), (1, by_rank.shape[1], n_))[0]

    plan = [("gather_a", [("w_in", 0), ("conv_w", None)]),
            ("gather_mix", [(k, 0) for k in groups["mix"]]),
            ("gather_ffn", [(k, 0) for k in groups["ffn"]]),
            ("gather_c", [(k, 1) for k in BIG])]
    pending, token = {}, None
    for name, items in plan:
        shards = [conv_w if l is None else wb[k][l] for k, l in items]
        lands = [landing(s, k == "w_in") for (k, _), s in zip(items, shards)]
        send, recv, s_thru, l_thru, token = push_start(shards, lands, False, name, token)
        pending[name] = (send, recv, s_thru, l_thru, items)
    gathered = {}

    names = [k for k in SMALL if k != "final_g"]
    shapes = [(DEPTH, NDEV, CW, BW // NDEV) if k == "conv_w" else W[k].shape for k in names] + [(D,), (1,)]

    def wpack(src):
        parts = [jnp.broadcast_to(src[k][:, None], shapes[i]) if k == "conv_w" else src[k] for i, k in enumerate(names)]
        return _pack(parts + [src["final_g"], jnp.ones((1,), f32)])

    packs = [wpack(W), wpack(M), wpack(V)]

    def weights(l, group, after):
        name = "gather_c" if l == 1 else {"in": "gather_a", "mix": "gather_mix", "ffn": "gather_ffn"}[group]
        if name in pending:
            send, recv, s_thru, l_thru, items = pending.pop(name)
            if name == "gather_a":
                after = (list(after) if isinstance(after, (list, tuple)) else [after]) + packs
            for item, arr in zip(items, push_wait(send, recv, s_thru, l_thru, after, False, name + "_wait")[1]):
                gathered[item] = arr
        res = {k: gathered[(k, l)] for k in groups[group]}
        if group == "in":
            res["conv_w"] = gathered[("conv_w", None)]
        return res

    big_g = {k: [None] * DEPTH for k in BIG}
    flights = []

    def start_exchange(items, name):
        parts = [big_g[k][l] for k, l in items]
        part = lambda p: p.shape[1:] if p.shape[0] == NDEV else (p.shape[1], p.shape[2] // 2)
        lands = [lax.empty((NDEV - 1,) + part(p), p.dtype) for p in parts]
        send, recv, s_thru, l_thru, tok = push_start(parts, lands, True, name, None)
        flights.append((send, recv, s_thru, l_thru, items, name))
        return tok

    small_flight = []

    def start_small(small_g, loss8, dfinal):
        sg_ = dict(small_g, norm1_g=[jnp.zeros((1, D), f32), small_g["norm1_g"][1]])
        gpack = _pack([jnp.stack([g.reshape(shapes[i][1:]) for g in sg_[k]]) for i, k in enumerate(names)]
                      + [dfinal, loss8[0, :1]])
        send, recv, s_thru, l_thru, tok = push_start([gpack], [landing(gpack)], False, "gather_small", None)
        small_flight.append((send, recv, s_thru, l_thru))
        return tok

    def on_grads(l, group, grads):
        if group == "small":
            return start_small(**grads)
        for k, g in grads.items():
            big_g[k][l] = g
        if l == 1 and group == "in":
            return start_exchange([(k, 1) for k in BIG], "exchange_l1")
        if l == 0:
            return start_exchange([(k, 0) for k in groups[group]], "exchange_l0_" + group)
        return None

    loss8, dx, dfinal, small_g = local_step(x, loss_target, W, weights, on_grads, token)

    landed, own = {}, {}
    for send, recv, s_thru, l_thru, items, name in flights:
        srcs, lands = push_wait(send, recv, s_thru, l_thru, dx, True, name + "_wait")
        for item, src, land in zip(items, srcs, lands):
            landed[item] = land
            own[item] = own_part(src)
    out = {}
    for k in BIG:
        items = [(k, l) for l in range(DEPTH)]
        out[k] = adam_big([landed[i] for i in items], [own[i] for i in items], W[k], M[k], V[k], "adam_" + k)

    send, recv, s_thru, l_thru = small_flight[0]
    gall = push_wait(send, recv, s_thru, l_thru, out[BIG[-1]][0], False, "gather_small_wait")[1][0]
    (late,) = allgather([small_g["norm1_g"][0].reshape(1, D // 128, 128)], "allgather_late")
    gall = lax.dynamic_update_slice(gall, late[0], (0, 0, 0))
    sg, sd, sm, sv = [_unpack(p, shapes) for p in adam_small(gall, *packs)]
    for i, k in enumerate(names + ["final_g"]):
        vals = [t[i] for t in (sg, sd, sm, sv)]
        if k == "conv_w":
            vals = [lax.dynamic_index_in_dim(t, me, axis=1, keepdims=False) for t in vals]
        out[k] = vals
    loss = sg[-1].reshape(())

    res = [loss, dx.reshape(bsz, seq, D)]
    for j in range(4):
        res += [out[k][j] for k in ORDER]
    return tuple(res)
```
